```python
import math
import jax, jax.numpy as jnp
from jax import lax
import numpy as np

D_MODEL = 1024
BATCH = 8
SEQ = 16384
DEPTH = 2

N_MEM = 256
DN_HEADS = 4
DN_DK = 128
DN_DV = 128
DN_CONV = 4
DN_CHUNK = 64
SWA_HEADS = 8
SWA_KV_HEADS = 2
SWA_DH = 64
WINDOW = 128
XA_HEADS = 4
XA_DH = 128
D_FF = 2816
N_BRANCH = 3
BRANCH_W = 512
DEEPNORM_ALPHA = (2 * DEPTH) ** 0.25
DEEPNORM_BETA = (8 * DEPTH) ** -0.25
LN_EPS = 1e-5
RMS_EPS = 1e-6
NEG_INF = -1e30
IN_SPLITS = (DN_HEADS * DN_DK, DN_HEADS * DN_DK, DN_HEADS * DN_DV, DN_HEADS, DN_HEADS,
             DN_HEADS * DN_DV, SWA_HEADS * SWA_DH, SWA_KV_HEADS * SWA_DH, SWA_KV_HEADS * SWA_DH,
             XA_HEADS * XA_DH, N_BRANCH * D_MODEL)
D_IN = sum(IN_SPLITS)
VALUE_SEGMENTS = (2, 8)

kernel_name = "hybrid_deltanet_swa_sink_memxattn_macaron_deepnorm"


def layer_norm(x, g, b):
    xf = x.astype(jnp.float32)
    mu = xf.mean(-1, keepdims=True)
    var = jnp.square(xf - mu).mean(-1, keepdims=True)
    return ((xf - mu) * lax.rsqrt(var + LN_EPS) * g.astype(jnp.float32) + b.astype(jnp.float32)).astype(x.dtype)


def swiglu(x, w_gu, w_down):
    gate, up = jnp.split(x @ w_gu, 2, axis=-1)
    return (jax.nn.silu(gate) * up) @ w_down


def causal_depthwise_conv(x, w):
    K, C = w.shape
    return lax.conv_general_dilated(x, w[:, None, :].astype(x.dtype), window_strides=(1,),
                                    padding=[(K - 1, 0)], dimension_numbers=('NWC', 'WIO', 'NWC'),
                                    feature_group_count=C)


def gated_delta_rule(q, k, v, g, beta):
    f32 = jnp.float32
    B_, S_, H, dk = q.shape
    dv = v.shape[-1]
    C = DN_CHUNK
    N = S_ // C

    def chunks(t):
        t = t.astype(f32).reshape((B_, N, C, H) + t.shape[3:])
        return jnp.moveaxis(t, 3, 1)

    q = chunks(q) * (dk ** -0.5)
    k, v, beta, g = chunks(k), chunks(v), chunks(beta), chunks(g)
    g = jnp.cumsum(g, axis=-1)
    tril = jnp.tril(jnp.ones((C, C), bool))
    strict = jnp.tril(jnp.ones((C, C), bool), -1)
    decay = jnp.exp(jnp.where(tril, g[..., :, None] - g[..., None, :], NEG_INF))

    k_beta = k * beta[..., None]
    a = jnp.where(strict, jnp.einsum('bhncd,bhnsd->bhncs', k_beta, k) * decay, 0.0)
    t_mat = a + jnp.eye(C, dtype=f32)
    rhs = jnp.concatenate([v * beta[..., None], k_beta * jnp.exp(g)[..., None]], axis=-1)
    sol = lax.linalg.triangular_solve(t_mat, rhs, left_side=True, lower=True, unit_diagonal=True)
    u, w = sol[..., :dv], sol[..., dv:]

    qk = jnp.where(tril, jnp.einsum('bhncd,bhnsd->bhncs', q, k) * decay, 0.0)
    g_last = g[..., -1]
    k_tail = k * jnp.exp(g_last[..., None] - g)[..., None]
    q_dec = q * jnp.exp(g)[..., None]

    def step(S, xs):
        q_i, qk_i, u_i, w_i, kt_i, gl_i = xs
        v_new = u_i - jnp.einsum('bhcd,bhde->bhce', w_i, S)
        o = jnp.einsum('bhcd,bhde->bhce', q_i, S) + jnp.einsum('bhcs,bhse->bhce', qk_i, v_new)
        S = S * jnp.exp(gl_i)[..., None, None] + jnp.einsum('bhcd,bhce->bhde', kt_i, v_new)
        return S, o

    xs = tuple(jnp.moveaxis(t, 2, 0) for t in (q_dec, qk, u, w, k_tail, g_last))
    S0 = jnp.zeros((B_, H, dk, dv), f32)
    _, o = lax.scan(step, S0, xs)
    o = jnp.moveaxis(o, 0, 2)
    return jnp.moveaxis(o, 1, 3).reshape(B_, S_, H, dv)


def deltanet_branch(q, k, v, b, a, z, conv_w, a_log, dt_bias, norm_w):
    f32 = jnp.float32
    B_, S_, _ = q.shape
    qkv = jax.nn.silu(causal_depthwise_conv(jnp.concatenate([q, k, v], axis=-1), conv_w))
    q, k, v = jnp.split(qkv.astype(f32), 3, axis=-1)
    q = q.reshape(B_, S_, DN_HEADS, DN_DK)
    k = k.reshape(B_, S_, DN_HEADS, DN_DK)
    v = v.reshape(B_, S_, DN_HEADS, DN_DV)
    q = q * lax.rsqrt(jnp.sum(q * q, -1, keepdims=True) + RMS_EPS)
    k = k * lax.rsqrt(jnp.sum(k * k, -1, keepdims=True) + RMS_EPS)
    beta = jax.nn.sigmoid(b.astype(f32))
    g = -jnp.exp(a_log.astype(f32)) * jax.nn.softplus(a.astype(f32) + dt_bias.astype(f32))
    o = gated_delta_rule(q, k, v, g, beta)
    o = o * lax.rsqrt(jnp.mean(o * o, -1, keepdims=True) + RMS_EPS) * norm_w.astype(f32)
    o = o * jax.nn.silu(z.astype(f32).reshape(B_, S_, DN_HEADS, DN_DV))
    return o.reshape(B_, S_, DN_HEADS * DN_DV).astype(z.dtype)


def sliding_window_attention(q, k, v, sinks):
    f32 = jnp.float32
    B_, S_, _ = q.shape
    Hkv, G, dh, W = SWA_KV_HEADS, SWA_HEADS // SWA_KV_HEADS, SWA_DH, WINDOW
    nb = S_ // W
    qb = q.reshape(B_, nb, W, Hkv, G, dh)

    def band(t):
        tb = t.reshape(B_, nb, W, Hkv, dh)
        prev = jnp.pad(tb[:, :-1], ((0, 0), (1, 0), (0, 0), (0, 0), (0, 0)))
        return jnp.concatenate([prev, tb], axis=2)

    kb, vb = band(k), band(v)
    s = jnp.einsum('bnqhgd,bnkhd->bnhgqk', qb, kb).astype(f32) * (dh ** -0.5)
    q_pos = jnp.arange(nb)[:, None] * W + jnp.arange(W)[None, :]
    k_pos = jnp.arange(nb)[:, None] * W - W + jnp.arange(2 * W)[None, :]
    diff = q_pos[:, :, None] - k_pos[:, None, :]
    mask = (diff >= 0) & (diff < W) & (k_pos[:, None, :] >= 0)
    s = jnp.where(mask[None, :, None, None], s, NEG_INF)
    sink = sinks.astype(f32).reshape(Hkv, G)[None, None, :, :, None, None]
    m = jnp.maximum(s.max(-1, keepdims=True), sink)
    p = jnp.exp(s - m)
    p = p / (p.sum(-1, keepdims=True) + jnp.exp(sink - m))
    o = jnp.einsum('bnhgqk,bnkhd->bnqhgd', p.astype(v.dtype), vb)
    return o.reshape(B_, S_, SWA_HEADS * dh)


def memory_cross_attention(q, mem_n, w_mem_kv):
    B_, S_, _ = q.shape
    M = mem_n.shape[1]
    k, v = jnp.split(mem_n @ w_mem_kv, 2, axis=-1)
    k = k.reshape(B_, M, XA_HEADS, XA_DH)
    v = v.reshape(B_, M, XA_HEADS, XA_DH)
    q = q.reshape(B_, S_, XA_HEADS, XA_DH)
    s = jnp.einsum('bshd,bmhd->bhsm', q, k).astype(jnp.float32) * (XA_DH ** -0.5)
    p = jax.nn.softmax(s, axis=-1)
    o = jnp.einsum('bhsm,bmhd->bshd', p.astype(v.dtype), v)
    return o.reshape(B_, S_, XA_HEADS * XA_DH)


def hybrid_layer(x, mem_n, ln_g, ln_b, ffn1_w_gu, ffn1_w_down, w_in, dn_conv_w, dn_a_log,
                 dn_dt_bias, dn_norm_w, swa_sinks, w_mem_kv, w_branch, w_out, ffn2_w_gu, ffn2_w_down):
    B_, S_, D = x.shape
    h = layer_norm(DEEPNORM_ALPHA * x + 0.5 * swiglu(x, ffn1_w_gu, ffn1_w_down), ln_g[0], ln_b[0])
    split_idx = [int(i) for i in np.cumsum(IN_SPLITS)[:-1]]
    (dn_q, dn_k, dn_v, dn_b, dn_a, dn_z, sw_q, sw_k, sw_v, xa_q, gates) = jnp.split(h @ w_in, split_idx, axis=-1)
    o_dn = deltanet_branch(dn_q, dn_k, dn_v, dn_b, dn_a, dn_z, dn_conv_w, dn_a_log, dn_dt_bias, dn_norm_w)
    o_sw = sliding_window_attention(sw_q, sw_k, sw_v, swa_sinks)
    o_xa = memory_cross_attention(xa_q, mem_n, w_mem_kv)
    branches = jnp.stack([o_dn, o_sw, o_xa], axis=2)
    gates = jax.nn.sigmoid(gates.reshape(B_, S_, N_BRANCH, D))
    merged = jnp.sum(gates * jnp.einsum('bsnc,ncd->bsnd', branches, w_branch), axis=2)
    h = layer_norm(DEEPNORM_ALPHA * h + merged @ w_out, ln_g[1], ln_b[1])
    return layer_norm(DEEPNORM_ALPHA * h + 0.5 * swiglu(h, ffn2_w_gu, ffn2_w_down), ln_g[2], ln_b[2])


def _fwd_setup_inputs(seed: int = 0) -> dict:
    key = jax.random.key(seed)
    ks = jax.random.split(key, 20)
    f32 = jnp.float32
    nrm = lambda k, shape, s: jax.random.normal(k, shape, f32) * s
    col_scale = np.concatenate([np.full(n, DEEPNORM_BETA if i in VALUE_SEGMENTS else 1.0, np.float32)
                                for i, n in enumerate(IN_SPLITS)])
    kv_scale = np.concatenate([np.ones(XA_HEADS * XA_DH, np.float32),
                               np.full(XA_HEADS * XA_DH, DEEPNORM_BETA, np.float32)])
    dt = jnp.exp(jax.random.uniform(ks[9], (DEPTH, DN_HEADS), f32, math.log(1e-3), math.log(1e-1)))
    return {
        "x": nrm(ks[0], (BATCH, SEQ, D_MODEL), 1.0),
        "mem": nrm(ks[1], (BATCH, N_MEM, D_MODEL), 1.0),
        "mem_ln_g": 1.0 + nrm(ks[2], (D_MODEL,), 0.02),
        "mem_ln_b": nrm(ks[3], (D_MODEL,), 0.02),
        "ln_g": 1.0 + nrm(ks[4], (DEPTH, 3, D_MODEL), 0.02),
        "ln_b": nrm(ks[5], (DEPTH, 3, D_MODEL), 0.02),
        "ffn1_w_gu": nrm(ks[6], (DEPTH, D_MODEL, 2 * D_FF), DEEPNORM_BETA * D_MODEL ** -0.5),
        "ffn1_w_down": nrm(ks[7], (DEPTH, D_FF, D_MODEL), DEEPNORM_BETA * D_FF ** -0.5),
        "w_in": nrm(ks[8], (DEPTH, D_MODEL, D_IN), D_MODEL ** -0.5) * jnp.asarray(col_scale),
        "dn_conv_w": nrm(ks[10], (DEPTH, DN_CONV, 3 * DN_HEADS * DN_DK), DN_CONV ** -0.5),
        "dn_a_log": jnp.log(jax.random.uniform(ks[11], (DEPTH, DN_HEADS), f32, 1.0, 16.0)),
        "dn_dt_bias": dt + jnp.log(-jnp.expm1(-dt)),
        "dn_norm_w": 1.0 + nrm(ks[12], (DEPTH, DN_DV), 0.02),
        "swa_sinks": nrm(ks[13], (DEPTH, SWA_HEADS), 0.5),
        "w_mem_kv": nrm(ks[14], (DEPTH, D_MODEL, 2 * XA_HEADS * XA_DH), D_MODEL ** -0.5) * jnp.asarray(kv_scale),
        "w_branch": nrm(ks[15], (DEPTH, N_BRANCH, BRANCH_W, D_MODEL), DEEPNORM_BETA * BRANCH_W ** -0.5),
        "w_out": nrm(ks[16], (DEPTH, D_MODEL, D_MODEL), DEEPNORM_BETA * D_MODEL ** -0.5),
        "ffn2_w_gu": nrm(ks[17], (DEPTH, D_MODEL, 2 * D_FF), DEEPNORM_BETA * D_MODEL ** -0.5),
        "ffn2_w_down": nrm(ks[18], (DEPTH, D_FF, D_MODEL), DEEPNORM_BETA * D_FF ** -0.5),
    }


def _fwd_reference(x, mem, mem_ln_g, mem_ln_b, ln_g, ln_b, ffn1_w_gu, ffn1_w_down, w_in, dn_conv_w,
              dn_a_log, dn_dt_bias, dn_norm_w, swa_sinks, w_mem_kv, w_branch, w_out, ffn2_w_gu, ffn2_w_down):
    mem_n = layer_norm(mem, mem_ln_g, mem_ln_b)
    for l in range(DEPTH):
        x = hybrid_layer(x, mem_n, ln_g[l], ln_b[l], ffn1_w_gu[l], ffn1_w_down[l], w_in[l], dn_conv_w[l],
                         dn_a_log[l], dn_dt_bias[l], dn_norm_w[l], swa_sinks[l], w_mem_kv[l], w_branch[l],
                         w_out[l], ffn2_w_gu[l], ffn2_w_down[l])
    return x


import jax as _jax
import jax.numpy as _jnp

TWIN_FORMAT = 'train_step'
FWD_PARAMS = ['x', 'mem', 'mem_ln_g', 'mem_ln_b', 'ln_g', 'ln_b', 'ffn1_w_gu', 'ffn1_w_down', 'w_in', 'dn_conv_w', 'dn_a_log', 'dn_dt_bias', 'dn_norm_w', 'swa_sinks', 'w_mem_kv', 'w_branch', 'w_out', 'ffn2_w_gu', 'ffn2_w_down']
TWIN_WEIGHTS = ['mem_ln_g', 'mem_ln_b', 'ln_g', 'ln_b', 'ffn1_w_gu', 'ffn1_w_down', 'w_in', 'dn_conv_w', 'dn_a_log', 'dn_dt_bias', 'dn_norm_w', 'swa_sinks', 'w_mem_kv', 'w_branch', 'w_out', 'ffn2_w_gu', 'ffn2_w_down']
TWIN_DIFF_INPUT = 'x'
TWIN_INPUTS = ['x', 'mem', 'mem_ln_g', 'mem_ln_b', 'ln_g', 'ln_b', 'ffn1_w_gu', 'ffn1_w_down', 'w_in', 'dn_conv_w', 'dn_a_log', 'dn_dt_bias', 'dn_norm_w', 'swa_sinks', 'w_mem_kv', 'w_branch', 'w_out', 'ffn2_w_gu', 'ffn2_w_down', 'loss_target', 'm_mem_ln_g', 'm_mem_ln_b', 'm_ln_g', 'm_ln_b', 'm_ffn1_w_gu', 'm_ffn1_w_down', 'm_w_in', 'm_dn_conv_w', 'm_dn_a_log', 'm_dn_dt_bias', 'm_dn_norm_w', 'm_swa_sinks', 'm_w_mem_kv', 'm_w_branch', 'm_w_out', 'm_ffn2_w_gu', 'm_ffn2_w_down', 'v_mem_ln_g', 'v_mem_ln_b', 'v_ln_g', 'v_ln_b', 'v_ffn1_w_gu', 'v_ffn1_w_down', 'v_w_in', 'v_dn_conv_w', 'v_dn_a_log', 'v_dn_dt_bias', 'v_dn_norm_w', 'v_swa_sinks', 'v_w_mem_kv', 'v_w_branch', 'v_w_out', 'v_ffn2_w_gu', 'v_ffn2_w_down']
TWIN_OUTPUTS = ['loss', 'grad_x', 'grad_mem_ln_g', 'grad_mem_ln_b', 'grad_ln_g', 'grad_ln_b', 'grad_ffn1_w_gu', 'grad_ffn1_w_down', 'grad_w_in', 'grad_dn_conv_w', 'grad_dn_a_log', 'grad_dn_dt_bias', 'grad_dn_norm_w', 'grad_swa_sinks', 'grad_w_mem_kv', 'grad_w_branch', 'grad_w_out', 'grad_ffn2_w_gu', 'grad_ffn2_w_down', 'delta_mem_ln_g', 'delta_mem_ln_b', 'delta_ln_g', 'delta_ln_b', 'delta_ffn1_w_gu', 'delta_ffn1_w_down', 'delta_w_in', 'delta_dn_conv_w', 'delta_dn_a_log', 'delta_dn_dt_bias', 'delta_dn_norm_w', 'delta_swa_sinks', 'delta_w_mem_kv', 'delta_w_branch', 'delta_w_out', 'delta_ffn2_w_gu', 'delta_ffn2_w_down', 'new_m_mem_ln_g', 'new_m_mem_ln_b', 'new_m_ln_g', 'new_m_ln_b', 'new_m_ffn1_w_gu', 'new_m_ffn1_w_down', 'new_m_w_in', 'new_m_dn_conv_w', 'new_m_dn_a_log', 'new_m_dn_dt_bias', 'new_m_dn_norm_w', 'new_m_swa_sinks', 'new_m_w_mem_kv', 'new_m_w_branch', 'new_m_w_out', 'new_m_ffn2_w_gu', 'new_m_ffn2_w_down', 'new_v_mem_ln_g', 'new_v_mem_ln_b', 'new_v_ln_g', 'new_v_ln_b', 'new_v_ffn1_w_gu', 'new_v_ffn1_w_down', 'new_v_w_in', 'new_v_dn_conv_w', 'new_v_dn_a_log', 'new_v_dn_dt_bias', 'new_v_dn_norm_w', 'new_v_swa_sinks', 'new_v_w_mem_kv', 'new_v_w_branch', 'new_v_w_out', 'new_v_ffn2_w_gu', 'new_v_ffn2_w_down']
TWIN_LEAF_KINDS = {'loss': 'loss', 'grad_x': 'grad_x', 'grad_mem_ln_g': 'grad_w', 'grad_mem_ln_b': 'grad_w', 'grad_ln_g': 'grad_w', 'grad_ln_b': 'grad_w', 'grad_ffn1_w_gu': 'grad_w', 'grad_ffn1_w_down': 'grad_w', 'grad_w_in': 'grad_w', 'grad_dn_conv_w': 'grad_w', 'grad_dn_a_log': 'grad_w', 'grad_dn_dt_bias': 'grad_w', 'grad_dn_norm_w': 'grad_w', 'grad_swa_sinks': 'grad_w', 'grad_w_mem_kv': 'grad_w', 'grad_w_branch': 'grad_w', 'grad_w_out': 'grad_w', 'grad_ffn2_w_gu': 'grad_w', 'grad_ffn2_w_down': 'grad_w', 'delta_mem_ln_g': 'delta_w', 'delta_mem_ln_b': 'delta_w', 'delta_ln_g': 'delta_w', 'delta_ln_b': 'delta_w', 'delta_ffn1_w_gu': 'delta_w', 'delta_ffn1_w_down': 'delta_w', 'delta_w_in': 'delta_w', 'delta_dn_conv_w': 'delta_w', 'delta_dn_a_log': 'delta_w', 'delta_dn_dt_bias': 'delta_w', 'delta_dn_norm_w': 'delta_w', 'delta_swa_sinks': 'delta_w', 'delta_w_mem_kv': 'delta_w', 'delta_w_branch': 'delta_w', 'delta_w_out': 'delta_w', 'delta_ffn2_w_gu': 'delta_w', 'delta_ffn2_w_down': 'delta_w', 'new_m_mem_ln_g': 'new_m', 'new_m_mem_ln_b': 'new_m', 'new_m_ln_g': 'new_m', 'new_m_ln_b': 'new_m', 'new_m_ffn1_w_gu': 'new_m', 'new_m_ffn1_w_down': 'new_m', 'new_m_w_in': 'new_m', 'new_m_dn_conv_w': 'new_m', 'new_m_dn_a_log': 'new_m', 'new_m_dn_dt_bias': 'new_m', 'new_m_dn_norm_w': 'new_m', 'new_m_swa_sinks': 'new_m', 'new_m_w_mem_kv': 'new_m', 'new_m_w_branch': 'new_m', 'new_m_w_out': 'new_m', 'new_m_ffn2_w_gu': 'new_m', 'new_m_ffn2_w_down': 'new_m', 'new_v_mem_ln_g': 'new_v', 'new_v_mem_ln_b': 'new_v', 'new_v_ln_g': 'new_v', 'new_v_ln_b': 'new_v', 'new_v_ffn1_w_gu': 'new_v', 'new_v_ffn1_w_down': 'new_v', 'new_v_w_in': 'new_v', 'new_v_dn_conv_w': 'new_v', 'new_v_dn_a_log': 'new_v', 'new_v_dn_dt_bias': 'new_v', 'new_v_dn_norm_w': 'new_v', 'new_v_swa_sinks': 'new_v', 'new_v_w_mem_kv': 'new_v', 'new_v_w_branch': 'new_v', 'new_v_w_out': 'new_v', 'new_v_ffn2_w_gu': 'new_v', 'new_v_ffn2_w_down': 'new_v'}


def _forward(args):
    return _fwd_reference(*[args[k] for k in FWD_PARAMS])


def _output_shape():
    def fwd():
        inp = _fwd_setup_inputs(0)
        return _fwd_reference(*[inp[k] for k in FWD_PARAMS])
    out = _jax.eval_shape(fwd)
    return out.shape, out.dtype

N_MICROBATCH = 1
ADAM_LR = 0.001
ADAM_B1 = 0.9
ADAM_B2 = 0.999
ADAM_EPS = 1e-08
ADAM_WD = 0.01
ADAM_STEP = 10
PER_EXAMPLE_BATCH_AXIS = {'x': 0, 'mem': 0, 'loss_target': 0}
SHARED_INPUTS = []
_WEIGHT_DTYPES = {'mem_ln_g': _jnp.float32, 'mem_ln_b': _jnp.float32, 'ln_g': _jnp.float32, 'ln_b': _jnp.float32, 'ffn1_w_gu': _jnp.float32, 'ffn1_w_down': _jnp.float32, 'w_in': _jnp.float32, 'dn_conv_w': _jnp.float32, 'dn_a_log': _jnp.float32, 'dn_dt_bias': _jnp.float32, 'dn_norm_w': _jnp.float32, 'swa_sinks': _jnp.float32, 'w_mem_kv': _jnp.float32, 'w_branch': _jnp.float32, 'w_out': _jnp.float32, 'ffn2_w_gu': _jnp.float32, 'ffn2_w_down': _jnp.float32}
MOMENT_SCALE = {'mem_ln_g': 3.947044e-03, 'mem_ln_b': 1.040302e-01, 'ln_g': 5.259648e+01, 'ln_b': 2.569079e+00, 'ffn1_w_gu': 1.030235e-02, 'ffn1_w_down': 1.689254e-02, 'w_in': 1.821665e-02, 'dn_conv_w': 2.167136e-02, 'dn_a_log': 2.208761e-01, 'dn_dt_bias': 2.107189e-01, 'dn_norm_w': 5.730704e-02, 'swa_sinks': 5.546419e-03, 'w_mem_kv': 4.732424e-03, 'w_branch': 2.378748e-02, 'w_out': 4.127783e-02, 'ffn2_w_gu': 1.027812e-02, 'ffn2_w_down': 1.691933e-02}


def _to_microbatches(a, axis):
    t = _jnp.moveaxis(a, axis, 0)
    t = t.reshape((N_MICROBATCH, t.shape[0] // N_MICROBATCH) + t.shape[1:])
    return _jnp.moveaxis(t, 1, axis + 1)


def setup_inputs(seed: int = 0) -> dict:
    inp = _fwd_setup_inputs(seed)
    key = _jax.random.fold_in(_jax.random.key(seed), 7919)
    shape, _ = _output_shape()
    out = dict(inp)
    out["loss_target"] = _jax.random.normal(_jax.random.fold_in(key, 0), shape, _jnp.float32)
    for i, name in enumerate(TWIN_WEIGHTS):
        w = inp[name].astype(_jnp.float32)
        if MOMENT_SCALE is None:
            s = _jnp.sqrt(_jnp.mean(_jnp.square(w)) + 1e-30)
        else:
            s = MOMENT_SCALE[name]
        km, kv = _jax.random.split(_jax.random.fold_in(key, i + 1))
        out[name] = w
        out["m_" + name] = s * _jax.random.normal(km, w.shape, _jnp.float32)
        out["v_" + name] = (s * s) * _jax.random.uniform(kv, w.shape, _jnp.float32, 0.5, 1.5)
    if N_MICROBATCH > 1:
        for name, axis in PER_EXAMPLE_BATCH_AXIS.items():
            out[name] = _to_microbatches(out[name], axis)
    return {'x': out['x'], 'mem': out['mem'], 'mem_ln_g': out['mem_ln_g'], 'mem_ln_b': out['mem_ln_b'], 'ln_g': out['ln_g'], 'ln_b': out['ln_b'], 'ffn1_w_gu': out['ffn1_w_gu'], 'ffn1_w_down': out['ffn1_w_down'], 'w_in': out['w_in'], 'dn_conv_w': out['dn_conv_w'], 'dn_a_log': out['dn_a_log'], 'dn_dt_bias': out['dn_dt_bias'], 'dn_norm_w': out['dn_norm_w'], 'swa_sinks': out['swa_sinks'], 'w_mem_kv': out['w_mem_kv'], 'w_branch': out['w_branch'], 'w_out': out['w_out'], 'ffn2_w_gu': out['ffn2_w_gu'], 'ffn2_w_down': out['ffn2_w_down'], 'loss_target': out['loss_target'], 'm_mem_ln_g': out['m_mem_ln_g'], 'm_mem_ln_b': out['m_mem_ln_b'], 'm_ln_g': out['m_ln_g'], 'm_ln_b': out['m_ln_b'], 'm_ffn1_w_gu': out['m_ffn1_w_gu'], 'm_ffn1_w_down': out['m_ffn1_w_down'], 'm_w_in': out['m_w_in'], 'm_dn_conv_w': out['m_dn_conv_w'], 'm_dn_a_log': out['m_dn_a_log'], 'm_dn_dt_bias': out['m_dn_dt_bias'], 'm_dn_norm_w': out['m_dn_norm_w'], 'm_swa_sinks': out['m_swa_sinks'], 'm_w_mem_kv': out['m_w_mem_kv'], 'm_w_branch': out['m_w_branch'], 'm_w_out': out['m_w_out'], 'm_ffn2_w_gu': out['m_ffn2_w_gu'], 'm_ffn2_w_down': out['m_ffn2_w_down'], 'v_mem_ln_g': out['v_mem_ln_g'], 'v_mem_ln_b': out['v_mem_ln_b'], 'v_ln_g': out['v_ln_g'], 'v_ln_b': out['v_ln_b'], 'v_ffn1_w_gu': out['v_ffn1_w_gu'], 'v_ffn1_w_down': out['v_ffn1_w_down'], 'v_w_in': out['v_w_in'], 'v_dn_conv_w': out['v_dn_conv_w'], 'v_dn_a_log': out['v_dn_a_log'], 'v_dn_dt_bias': out['v_dn_dt_bias'], 'v_dn_norm_w': out['v_dn_norm_w'], 'v_swa_sinks': out['v_swa_sinks'], 'v_w_mem_kv': out['v_w_mem_kv'], 'v_w_branch': out['v_w_branch'], 'v_w_out': out['v_w_out'], 'v_ffn2_w_gu': out['v_ffn2_w_gu'], 'v_ffn2_w_down': out['v_ffn2_w_down']}


def _loss(weights, diff, rest, loss_target):
    with _jax.named_scope("forward"):
        args = {**rest, TWIN_DIFF_INPUT: diff, **{k: w.astype(_WEIGHT_DTYPES[k]) for k, w in weights.items()}}
        y = _forward(args)
    with _jax.named_scope("loss_head"):
        err = _jnp.square(y.astype(_jnp.float32) - loss_target)
        return 0.5 * _jnp.sum(_jnp.mean(err, axis=-1)) if err.ndim else 0.5 * err


def _adamw(w, g, m, v):
    m = ADAM_B1 * m + (1.0 - ADAM_B1) * g
    v = ADAM_B2 * v + (1.0 - ADAM_B2) * _jnp.square(g)
    m_hat = m / (1.0 - ADAM_B1 ** ADAM_STEP)
    v_hat = v / (1.0 - ADAM_B2 ** ADAM_STEP)
    delta = -ADAM_LR * (m_hat / (_jnp.sqrt(v_hat) + ADAM_EPS) + ADAM_WD * w)
    return delta, m, v


def reference(x, mem, mem_ln_g, mem_ln_b, ln_g, ln_b, ffn1_w_gu, ffn1_w_down, w_in, dn_conv_w, dn_a_log, dn_dt_bias, dn_norm_w, swa_sinks, w_mem_kv, w_branch, w_out, ffn2_w_gu, ffn2_w_down, loss_target, m_mem_ln_g, m_mem_ln_b, m_ln_g, m_ln_b, m_ffn1_w_gu, m_ffn1_w_down, m_w_in, m_dn_conv_w, m_dn_a_log, m_dn_dt_bias, m_dn_norm_w, m_swa_sinks, m_w_mem_kv, m_w_branch, m_w_out, m_ffn2_w_gu, m_ffn2_w_down, v_mem_ln_g, v_mem_ln_b, v_ln_g, v_ln_b, v_ffn1_w_gu, v_ffn1_w_down, v_w_in, v_dn_conv_w, v_dn_a_log, v_dn_dt_bias, v_dn_norm_w, v_swa_sinks, v_w_mem_kv, v_w_branch, v_w_out, v_ffn2_w_gu, v_ffn2_w_down):
    given = dict(x=x, mem=mem, mem_ln_g=mem_ln_g, mem_ln_b=mem_ln_b, ln_g=ln_g, ln_b=ln_b, ffn1_w_gu=ffn1_w_gu, ffn1_w_down=ffn1_w_down, w_in=w_in, dn_conv_w=dn_conv_w, dn_a_log=dn_a_log, dn_dt_bias=dn_dt_bias, dn_norm_w=dn_norm_w, swa_sinks=swa_sinks, w_mem_kv=w_mem_kv, w_branch=w_branch, w_out=w_out, ffn2_w_gu=ffn2_w_gu, ffn2_w_down=ffn2_w_down, loss_target=loss_target, m_mem_ln_g=m_mem_ln_g, m_mem_ln_b=m_mem_ln_b, m_ln_g=m_ln_g, m_ln_b=m_ln_b, m_ffn1_w_gu=m_ffn1_w_gu, m_ffn1_w_down=m_ffn1_w_down, m_w_in=m_w_in, m_dn_conv_w=m_dn_conv_w, m_dn_a_log=m_dn_a_log, m_dn_dt_bias=m_dn_dt_bias, m_dn_norm_w=m_dn_norm_w, m_swa_sinks=m_swa_sinks, m_w_mem_kv=m_w_mem_kv, m_w_branch=m_w_branch, m_w_out=m_w_out, m_ffn2_w_gu=m_ffn2_w_gu, m_ffn2_w_down=m_ffn2_w_down, v_mem_ln_g=v_mem_ln_g, v_mem_ln_b=v_mem_ln_b, v_ln_g=v_ln_g, v_ln_b=v_ln_b, v_ffn1_w_gu=v_ffn1_w_gu, v_ffn1_w_down=v_ffn1_w_down, v_w_in=v_w_in, v_dn_conv_w=v_dn_conv_w, v_dn_a_log=v_dn_a_log, v_dn_dt_bias=v_dn_dt_bias, v_dn_norm_w=v_dn_norm_w, v_swa_sinks=v_swa_sinks, v_w_mem_kv=v_w_mem_kv, v_w_branch=v_w_branch, v_w_out=v_w_out, v_ffn2_w_gu=v_ffn2_w_gu, v_ffn2_w_down=v_ffn2_w_down)
    weights = {n: given[n] for n in TWIN_WEIGHTS}
    shared = {n: given[n] for n in SHARED_INPUTS}
    per_example = {n: given[n] for n in ['x', 'mem']}
    grad_fn = _jax.value_and_grad(_loss, argnums=(0, 1))

    def one_microbatch(ex, loss_target):
        ex = dict(ex)
        diff = ex.pop(TWIN_DIFF_INPUT)
        return grad_fn(weights, diff, {**shared, **ex}, loss_target)

    if N_MICROBATCH == 1:
        loss, (grad_w, grad_x) = one_microbatch(per_example, given["loss_target"])
    else:
        def body(carry, xs):
            loss_sum, grad_sum = carry
            l_k, (gw_k, gx_k) = one_microbatch(xs[0], xs[1])
            with _jax.named_scope("update"):
                return (loss_sum + l_k, _jax.tree.map(_jnp.add, grad_sum, gw_k)), gx_k

        init = (_jnp.zeros((), _jnp.float32), _jax.tree.map(_jnp.zeros_like, weights))
        (loss, grad_w), grad_x = _jax.lax.scan(body, init, (per_example, given["loss_target"]))
    with _jax.named_scope("update"):
        delta_w, new_m, new_v = {}, {}, {}
        for n in TWIN_WEIGHTS:
            delta_w[n], new_m[n], new_v[n] = _adamw(weights[n], grad_w[n], given["m_" + n], given["v_" + n])
    return (loss, grad_x, *[grad_w[n] for n in TWIN_WEIGHTS], *[delta_w[n] for n in TWIN_WEIGHTS],
            *[new_m[n] for n in TWIN_WEIGHTS], *[new_v[n] for n in TWIN_WEIGHTS])
```

```python
import functools

import jax
import jax.numpy as jnp
from jax import lax
from jax.experimental import pallas as pl
from jax.experimental.pallas import tpu as pltpu

f32, bf16, i32 = jnp.float32, jnp.bfloat16, jnp.int32
HI = lax.Precision.HIGHEST
MESH = pl.DeviceIdType.MESH

D_MODEL = 1024
DEPTH = 2
DN_HEADS, DN_DK, DN_CONV, DN_CHUNK = 4, 128, 4, 64
SWA_HEADS, SWA_KV_HEADS, SWA_DH, WINDOW = 8, 2, 64, 128
XA_HEADS, XA_DH = 4, 128
D_FF = 2816
N_BRANCH, BRANCH_W = 3, 512
ALPHA = (2 * DEPTH) ** 0.25
LN_EPS, RMS_EPS, NEG_INF = 1e-5, 1e-6, -1e30
D_IN = 6408
D_INP = 6656
LR, B1, B2, EPS, WD, STEP = 0.001, 0.9, 0.999, 1e-08, 0.01, 10

LANES = 128
VMEM_LIMIT = 56 << 20
N_SHARD = 4
ROW_ALIGN = 2048

WEIGHTS = ['mem_ln_g', 'mem_ln_b', 'ln_g', 'ln_b', 'ffn1_w_gu', 'ffn1_w_down', 'w_in', 'dn_conv_w', 'dn_a_log',
           'dn_dt_bias', 'dn_norm_w', 'swa_sinks', 'w_mem_kv', 'w_branch', 'w_out', 'ffn2_w_gu', 'ffn2_w_down']
SHARD_AXIS = {'mem_ln_g': None, 'mem_ln_b': None, 'ln_g': 2, 'ln_b': 2, 'ffn1_w_gu': 2, 'ffn1_w_down': 1, 'w_in': 2,
              'dn_conv_w': 2, 'dn_a_log': None, 'dn_dt_bias': None, 'dn_norm_w': None, 'swa_sinks': None,
              'w_mem_kv': 1, 'w_branch': 3, 'w_out': 1, 'ffn2_w_gu': 2, 'ffn2_w_down': 1}
MATRICES = ['ffn1_w_gu', 'ffn1_w_down', 'w_in', 'w_mem_kv', 'w_branch', 'w_out', 'ffn2_w_gu', 'ffn2_w_down']
SMALL_SHARDED = ['ln_g', 'ln_b', 'dn_conv_w']


def _dg(a, b, mode, hi):
    nb = a.ndim - 2
    bd = tuple(range(nb))
    ca = nb if mode == 'tn' else nb + 1
    cb = nb + 1 if mode == 'nt' else nb
    dims = (((ca,), (cb,)), (bd, bd))
    if hi:
        return lax.dot_general(a.astype(f32), b.astype(f32), dims, precision=HI, preferred_element_type=f32)
    return lax.dot_general(a.astype(bf16), b.astype(bf16), dims, preferred_element_type=f32)


@functools.partial(jax.custom_vjp, nondiff_argnums=(2, 3))
def _dot(a, b, mode, hi):
    return _dg(a, b, mode, hi)


def _dot_fwd(a, b, mode, hi):
    return _dg(a, b, mode, hi), (a, b)


def _dot_bwd(mode, hi, res, g):
    a, b = res
    if mode == 'nn':
        da, db = _dg(g, b, 'nt', hi), _dg(a, g, 'tn', hi)
    elif mode == 'nt':
        da, db = _dg(g, b, 'nn', hi), _dg(g, a, 'tn', hi)
    else:
        da, db = _dg(b, g, 'nt', hi), _dg(a, g, 'nn', hi)
    return da.astype(a.dtype), db.astype(b.dtype)


_dot.defvjp(_dot_fwd, _dot_bwd)


def bdot(a, b, mode):
    return _dot(a, b, mode, False)


def hdot(a, b, mode):
    return _dot(a, b, mode, True)


def _shift_rows_impl(x, xprev, s):
    rows = lax.broadcasted_iota(i32, x.shape, 0)
    return jnp.where(rows >= s, pltpu.roll(x, s, 0), pltpu.roll(xprev, s, 0))


@functools.partial(jax.custom_vjp, nondiff_argnums=(2,))
def _shift_rows(x, xprev, s):
    return _shift_rows_impl(x, xprev, s)


def _shift_rows_fwd(x, xprev, s):
    return _shift_rows_impl(x, xprev, s), None


def _shift_rows_bwd(s, _, g):
    n = g.shape[0]
    rows = lax.broadcasted_iota(i32, g.shape, 0)
    r = pltpu.roll(g, n - s, 0)
    keep = rows < n - s
    return jnp.where(keep, r, 0.0), jnp.where(keep, 0.0, r)


_shift_rows.defvjp(_shift_rows_fwd, _shift_rows_bwd)


def _lane_pick(x, lane):
    idx = lax.broadcasted_iota(i32, x.shape, x.ndim - 1)
    return jnp.sum(jnp.where(idx == lane, x, 0.0), axis=-1, keepdims=True)


def _silu(x):
    return x * jax.nn.sigmoid(x)


def _full(a):
    nd = a.ndim
    return (a, tuple(a.shape), lambda i, _nd=nd: (0,) * _nd)


def _rows(a, tm, col=0, width=None):
    width = a.shape[1] if width is None else width
    return (a, (tm, width), lambda i, _c=col: (i, _c))


def _tile_call(name, fn, n, ins, outs, accs=()):
    n_in, n_out, n_acc = len(ins), len(outs), len(accs)

    def body(*refs):
        i = pl.program_id(0)
        res = fn(i, *[r[...] for r in refs[:n_in]])
        if not isinstance(res, (tuple, list)):
            res = (res,)
        assert len(res) == n_out + n_acc, (name, len(res), n_out, n_acc)
        for r, v in zip(refs[n_in:n_in + n_out], res[:n_out]):
            r[...] = v.astype(r.dtype)
        if n_acc:
            acc_refs = refs[n_in + n_out:]

            @pl.when(i == 0)
            def _():
                for r in acc_refs:
                    r[...] = jnp.zeros(r.shape, r.dtype)

            for r, v in zip(acc_refs, res[n_out:]):
                r[...] += v.astype(r.dtype)

    out_shape = [jax.ShapeDtypeStruct(s, d) for s, d, _, _ in outs] + [jax.ShapeDtypeStruct(s, d) for s, d in accs]
    out_specs = [pl.BlockSpec(b, m) for _, _, b, m in outs]
    out_specs += [pl.BlockSpec(tuple(s), lambda i, _nd=len(s): (0,) * _nd) for s, _ in accs]
    res = pl.pallas_call(
        body, name=name, grid=(n,),
        in_specs=[pl.BlockSpec(b, m) for _, b, m in ins],
        out_specs=out_specs, out_shape=out_shape,
        compiler_params=pltpu.CompilerParams(dimension_semantics=("arbitrary",), vmem_limit_bytes=VMEM_LIMIT),
    )(*[a for a, _, _ in ins])
    return res


def _pick(n, cands):
    for c in cands:
        if n % c == 0:
            return c
    return n


def _mm(name, a, b, mode, out_dtype=f32, add=None):
    if mode == 'tn':
        K, M = a.shape
    else:
        M, K = a.shape
    N = b.shape[0] if mode == 'nt' else b.shape[1]
    tm = _pick(M, (1024, 512, 256, 128))
    tn = _pick(N, (512, 256, 128))
    tk = K if K <= D_FF else _pick(K, (2048, 1024, 512, 256, 128))
    nk = K // tk
    ca = 0 if mode == 'tn' else 1
    cb = 1 if mode == 'nt' else 0
    dims = (((ca,), (cb,)), ((), ()))

    def body(*refs):
        a_ref, b_ref = refs[0], refs[1]
        add_ref = refs[2] if add is not None else None
        o_ref, acc_ref = refs[-2], refs[-1]
        k = pl.program_id(2)

        @pl.when(k == 0)
        def _():
            acc_ref[...] = jnp.zeros(acc_ref.shape, f32)

        acc_ref[...] += lax.dot_general(a_ref[...].astype(bf16), b_ref[...].astype(bf16), dims, preferred_element_type=f32)

        @pl.when(k == nk - 1)
        def _():
            r = acc_ref[...]
            if add_ref is not None:
                r = r + add_ref[...].astype(f32)
            o_ref[...] = r.astype(o_ref.dtype)

    a_spec = pl.BlockSpec((tk, tm), lambda i, j, k: (k, i)) if mode == 'tn' else pl.BlockSpec((tm, tk), lambda i, j, k: (i, k))
    b_spec = pl.BlockSpec((tn, tk), lambda i, j, k: (j, k)) if mode == 'nt' else pl.BlockSpec((tk, tn), lambda i, j, k: (k, j))
    in_specs, args = [a_spec, b_spec], [a, b]
    if add is not None:
        in_specs.append(pl.BlockSpec((tm, tn), lambda i, j, k: (i, j)))
        args.append(add)
    return pl.pallas_call(
        body, name=name, grid=(M // tm, N // tn, nk), in_specs=in_specs,
        out_specs=pl.BlockSpec((tm, tn), lambda i, j, k: (i, j)),
        out_shape=jax.ShapeDtypeStruct((M, N), out_dtype),
        scratch_shapes=[pltpu.VMEM((tm, tn), f32)],
        compiler_params=pltpu.CompilerParams(dimension_semantics=("parallel", "parallel", "arbitrary"), vmem_limit_bytes=VMEM_LIMIT),
    )(*args)


def _swiglu(u):
    return _silu(u[:, :D_FF]) * u[:, D_FF:]


def _res_ln(x, f, g, b, s):
    r = ALPHA * x + s * f
    mu = jnp.mean(r, axis=-1, keepdims=True)
    rc = r - mu
    var = jnp.mean(rc * rc, axis=-1, keepdims=True)
    return rc * lax.rsqrt(var + LN_EPS) * g + b


def _ln(x, g, b):
    mu = jnp.mean(x, axis=-1, keepdims=True)
    xc = x - mu
    var = jnp.mean(xc * xc, axis=-1, keepdims=True)
    return xc * lax.rsqrt(var + LN_EPS) * g + b


def _dn_pre(first, xc, xp, ba, cw, hp):
    xp = jnp.where(first, 0.0, xp)
    y = cw[DN_CONV - 1:DN_CONV, :] * xc
    for j in range(DN_CONV - 1):
        y = y + cw[j:j + 1, :] * _shift_rows(xc, xp, DN_CONV - 1 - j)
    c = _silu(y)
    qs, ks, vs, gs, bs = [], [], [], [], []
    nqk = DN_HEADS * DN_DK
    for h in range(DN_HEADS):
        q = c[:, h * DN_DK:(h + 1) * DN_DK]
        k = c[:, nqk + h * DN_DK:nqk + (h + 1) * DN_DK]
        v = c[:, 2 * nqk + h * DN_DK:2 * nqk + (h + 1) * DN_DK]
        qs.append(q * lax.rsqrt(jnp.sum(q * q, axis=-1, keepdims=True) + RMS_EPS))
        ks.append(k * lax.rsqrt(jnp.sum(k * k, axis=-1, keepdims=True) + RMS_EPS))
        vs.append(v)
        beta = jax.nn.sigmoid(_lane_pick(ba, h))
        a_log = _lane_pick(hp[0:1, :], h)
        dt = _lane_pick(hp[1:2, :], h)
        g = -jnp.exp(a_log) * jax.nn.softplus(_lane_pick(ba, DN_HEADS + h) + dt)
        gs.append(jnp.broadcast_to(g, q.shape))
        bs.append(jnp.broadcast_to(beta, q.shape))
    return tuple(jnp.stack(t, axis=0) for t in (qs, ks, vs, gs, bs))


def _wy(q, k, v, gb, bb):
    B, C, _ = q.shape
    ri = lax.broadcasted_iota(i32, (B, C, C), 1)
    ci = lax.broadcasted_iota(i32, (B, C, C), 2)
    tril, strict = ri >= ci, ri > ci
    ones_l = jnp.full((B, C, LANES), 1.0 / LANES, f32)
    gc = hdot(tril.astype(f32), gb, 'nn')
    gl = hdot(jnp.ones((B, C, C), f32), gb, 'nn')
    dmat = hdot(gc, ones_l, 'nt') - hdot(ones_l, gc, 'nt')
    decay = jnp.exp(jnp.where(tril, dmat, NEG_INF))
    qs = q * (DN_DK ** -0.5)
    kb = k * bb
    a = jnp.where(strict, bdot(kb, k, 'nt') * decay, 0.0)
    xu = v * bb
    xw = kb * jnp.exp(gc)
    p = -a
    n_it = max(1, (C - 1).bit_length())
    for it in range(n_it):
        xu = xu + hdot(p, xu, 'nn')
        xw = xw + hdot(p, xw, 'nn')
        if it < n_it - 1:
            p = hdot(p, p, 'nn')
    qk = jnp.where(tril, bdot(qs, k, 'nt') * decay, 0.0)
    kt = k * jnp.exp(gl - gc)
    qd = qs * jnp.exp(gc)
    return qd, qk, xu, xw, kt, jnp.exp(gl)


def _scan_step(S, qd, qk, u, w, kt, egl):
    vn = u - bdot(w, S, 'nn')
    o = bdot(qd, S, 'nn') + bdot(qk, vn, 'nn')
    e2 = jnp.concatenate([egl] * (DN_DK // DN_CHUNK), axis=1)
    return o, S * e2 + bdot(kt, vn, 'tn')


def _dn_post(o, z, nw):
    outs = []
    for h in range(DN_HEADS):
        oh = o[h]
        oh = oh * lax.rsqrt(jnp.mean(oh * oh, axis=-1, keepdims=True) + RMS_EPS) * nw
        outs.append(oh * _silu(z[:, h * DN_DK:(h + 1) * DN_DK]))
    return jnp.concatenate(outs, axis=1)


def _swa(first, q, kc, kp, vc, vp, snk):
    W = q.shape[1]
    G = SWA_HEADS // SWA_KV_HEADS
    r = lax.broadcasted_iota(i32, (G, W, 2 * W), 1)
    c = lax.broadcasted_iota(i32, (G, W, 2 * W), 2)
    mask = (c > r) & (c <= W + r) & jnp.logical_or(c >= W, jnp.logical_not(first))
    sink_all = _lane_pick(snk, 0)
    outs = []
    for j in range(SWA_KV_HEADS):
        qj = q[j * G:(j + 1) * G]
        kk = jnp.broadcast_to(jnp.concatenate([kp[j], kc[j]], axis=0)[None], (G, 2 * W, SWA_DH))
        vv = jnp.broadcast_to(jnp.concatenate([vp[j], vc[j]], axis=0)[None], (G, 2 * W, SWA_DH))
        s = jnp.where(mask, bdot(qj, kk, 'nt') * (SWA_DH ** -0.5), NEG_INF)
        sink = sink_all[j * G:(j + 1) * G][:, :, None]
        m = jnp.maximum(jnp.max(s, axis=-1, keepdims=True), sink)
        p = jnp.exp(s - m)
        p = p / (jnp.sum(p, axis=-1, keepdims=True) + jnp.exp(sink - m))
        outs.append(bdot(p, vv, 'nn'))
    return jnp.concatenate(outs, axis=0)


def _xa(q, kv):
    outs = []
    nk = XA_HEADS * XA_DH
    for h in range(XA_HEADS):
        qh = q[:, h * XA_DH:(h + 1) * XA_DH]
        kh = kv[:, h * XA_DH:(h + 1) * XA_DH]
        vh = kv[:, nk + h * XA_DH:nk + (h + 1) * XA_DH]
        s = bdot(qh, kh, 'nt') * (XA_DH ** -0.5)
        m = jnp.max(s, axis=-1, keepdims=True)
        p = jnp.exp(s - m)
        p = p / jnp.sum(p, axis=-1, keepdims=True)
        outs.append(bdot(p, vh, 'nn'))
    return jnp.concatenate(outs, axis=1)


def _merge(o_dn, o_sw, o_xa, gates, wb):
    acc = None
    for n, o in enumerate((o_dn, o_sw, o_xa)):
        t = jax.nn.sigmoid(gates[:, n * D_MODEL:(n + 1) * D_MODEL]) * bdot(o, wb[n], 'nn')
        acc = t if acc is None else acc + t
    return acc


def _row_tile(T, want):
    return _pick(T, tuple(c for c in (1024, 512, 256, 128, 64) if c <= want))


def _res_ln_fwd(name, x, f, g, b, s):
    T = x.shape[0]
    tm = _row_tile(T, 512)

    def fn(i, x, f, g, b):
        h = _res_ln(x, f, g, b, s)
        return h, h

    return _tile_call(name, fn, T // tm, [_rows(x, tm), _rows(f, tm), _full(g), _full(b)],
                      [((T, D_MODEL), f32, (tm, D_MODEL), lambda i: (i, 0)), ((T, D_MODEL), bf16, (tm, D_MODEL), lambda i: (i, 0))])


def _res_ln_bwd(name, x, f, g, b, s, dh):
    T = x.shape[0]
    tm = _row_tile(T, 512)

    def fn(i, x, f, g, b, dh):
        _, vjp = jax.vjp(lambda x, f, g, b: _res_ln(x, f, g, b, s), x, f, g, b)
        return vjp(dh)

    return _tile_call(name, fn, T // tm, [_rows(x, tm), _rows(f, tm), _full(g), _full(b), _rows(dh, tm)],
                      [((T, D_MODEL), f32, (tm, D_MODEL), lambda i: (i, 0)), ((T, D_MODEL), bf16, (tm, D_MODEL), lambda i: (i, 0))],
                      [((1, D_MODEL), f32), ((1, D_MODEL), f32)])


def _swiglu_fwd(name, u):
    T = u.shape[0]
    tm = _row_tile(T, 256)
    return _tile_call(name, lambda i, u: _swiglu(u.astype(f32)), T // tm, [_rows(u, tm)],
                      [((T, D_FF), bf16, (tm, D_FF), lambda i: (i, 0))])[0]


def _swiglu_bwd(name, u, da):
    T = u.shape[0]
    tm = _row_tile(T, 256)

    def fn(i, u, da):
        _, vjp = jax.vjp(_swiglu, u.astype(f32))
        return vjp(da.astype(f32))[0]

    return _tile_call(name, fn, T // tm, [_rows(u, tm), _rows(da, tm)],
                      [((T, 2 * D_FF), bf16, (tm, 2 * D_FF), lambda i: (i, 0))])[0]


def _hm_spec(tm, w=DN_DK):
    return ((DN_HEADS, tm, w), lambda i: (0, i, 0))


def _prev(i):
    return jnp.maximum(i - 1, 0)


def _dn_pre_fwd(name, p, cw, hp):
    T = p.shape[0]
    tm = _row_tile(T, 256)
    n = T // tm
    W3 = 3 * DN_HEADS * DN_DK
    ins = [(p, (tm, W3), lambda i: (i, 0)), (p, (tm, W3), lambda i: (_prev(i), 0)),
           (p, (tm, LANES), lambda i: (i, 6400 // LANES)), _full(cw), _full(hp)]
    blk, im = _hm_spec(tm)
    return _tile_call(name, lambda i, *a: _dn_pre(i == 0, *a), n, ins, [((DN_HEADS, T, DN_DK), f32, blk, im)] * 5)


def _dn_pre_bwd(name, p, cw, hp, cts):
    T = p.shape[0]
    tm = _row_tile(T, 256)
    n = T // tm
    W3 = 3 * DN_HEADS * DN_DK
    blk, im = _hm_spec(tm)
    ins = [(p, (tm, W3), lambda i: (i, 0)), (p, (tm, W3), lambda i: (_prev(i), 0)),
           (p, (tm, LANES), lambda i: (i, 6400 // LANES)), _full(cw), _full(hp)] + [(c, blk, im) for c in cts]

    def fn(i, xc, xp, ba, cw, hp, *cts):
        _, vjp = jax.vjp(lambda *a: _dn_pre(i == 0, *a), xc, xp, ba, cw, hp)
        return vjp(tuple(cts))

    return _tile_call(name, fn, n, ins,
                      [((T, W3), f32, (tm, W3), lambda i: (i, 0)), ((T, W3), f32, (tm, W3), lambda i: (i, 0)),
                       ((T, LANES), f32, (tm, LANES), lambda i: (i, 0))],
                      [(tuple(cw.shape), f32), (tuple(hp.shape), f32)])


def _wy_split(x, nc):
    return x.reshape(DN_HEADS * nc, DN_CHUNK, x.shape[-1])


def _wy_fwd(name, q, k, v, gb, bb):
    T = q.shape[1]
    tm = _row_tile(T, 256)
    nc = tm // DN_CHUNK

    def fn(i, *a):
        outs = _wy(*[_wy_split(t, nc) for t in a])
        return [o.reshape(DN_HEADS, tm, o.shape[-1]) for o in outs]

    blk, im = _hm_spec(tm)
    blk2, im2 = _hm_spec(tm, DN_CHUNK)
    ins = [(t, blk, im) for t in (q, k, v, gb, bb)]
    full = (DN_HEADS, T, DN_DK)
    outs = [(full, bf16, blk, im), ((DN_HEADS, T, DN_CHUNK), bf16, blk2, im2), (full, f32, blk, im),
            (full, bf16, blk, im), (full, bf16, blk, im), (full, f32, blk, im)]
    return _tile_call(name, fn, T // tm, ins, outs)


def _wy_bwd(name, q, k, v, gb, bb, cts):
    T = q.shape[1]
    tm = _row_tile(T, 256)
    nc = tm // DN_CHUNK

    def fn(i, q, k, v, gb, bb, *cts):
        _, vjp = jax.vjp(_wy, *[_wy_split(t, nc) for t in (q, k, v, gb, bb)])
        grads = vjp(tuple(_wy_split(c.astype(f32), nc) for c in cts))
        return [g.reshape(DN_HEADS, tm, DN_DK) for g in grads]

    blk, im = _hm_spec(tm)
    blk2, im2 = _hm_spec(tm, DN_CHUNK)
    ins = [(t, blk, im) for t in (q, k, v, gb, bb)]
    ins += [(c, blk2 if c.shape[-1] == DN_CHUNK else blk, im) for c in cts]
    return _tile_call(name, fn, T // tm, ins, [((DN_HEADS, T, DN_DK), f32, blk, im)] * 5)


def _scan_fwd(name, qd, qk, u, w, kt, egl):
    T = qd.shape[1]
    C = DN_CHUNK
    n = T // C

    def body(qd_ref, qk_ref, u_ref, w_ref, kt_ref, egl_ref, o_ref, sall_ref, s_ref):
        i = pl.program_id(0)

        @pl.when(i == 0)
        def _():
            s_ref[...] = jnp.zeros(s_ref.shape, f32)

        S = s_ref[...]
        sall_ref[...] = S[:, None]
        o, Sn = _scan_step(S, qd_ref[...], qk_ref[...], u_ref[...], w_ref[...], kt_ref[...], egl_ref[...])
        o_ref[...] = o
        s_ref[...] = Sn

    blk, im = _hm_spec(C)
    blk2, im2 = _hm_spec(C, C)
    return pl.pallas_call(
        body, name=name, grid=(n,),
        in_specs=[pl.BlockSpec(blk, im), pl.BlockSpec(blk2, im2)] + [pl.BlockSpec(blk, im)] * 4,
        out_specs=[pl.BlockSpec(blk, im), pl.BlockSpec((DN_HEADS, 1, DN_DK, DN_DK), lambda i: (0, i, 0, 0))],
        out_shape=[jax.ShapeDtypeStruct((DN_HEADS, T, DN_DK), f32), jax.ShapeDtypeStruct((DN_HEADS, n, DN_DK, DN_DK), f32)],
        scratch_shapes=[pltpu.VMEM((DN_HEADS, DN_DK, DN_DK), f32)],
        compiler_params=pltpu.CompilerParams(dimension_semantics=("arbitrary",), vmem_limit_bytes=VMEM_LIMIT),
    )(qd, qk, u, w, kt, egl)


def _scan_bwd(name, qd, qk, u, w, kt, egl, s_all, do):
    T = qd.shape[1]
    C = DN_CHUNK
    n = T // C

    def body(qd_ref, qk_ref, u_ref, w_ref, kt_ref, egl_ref, sall_ref, do_ref,
             dqd_ref, dqk_ref, du_ref, dw_ref, dkt_ref, degl_ref, ds_ref):
        i = pl.program_id(0)

        @pl.when(i == 0)
        def _():
            ds_ref[...] = jnp.zeros(ds_ref.shape, f32)

        S = sall_ref[...][:, 0]
        args = [r[...].astype(f32) for r in (qd_ref, qk_ref, u_ref, w_ref, kt_ref, egl_ref)]
        _, vjp = jax.vjp(_scan_step, S, *args)
        dS, dqd, dqk, du, dw, dkt, degl = vjp((do_ref[...], ds_ref[...]))
        ds_ref[...] = dS
        for r, v in zip((dqd_ref, dqk_ref, du_ref, dw_ref, dkt_ref, degl_ref), (dqd, dqk, du, dw, dkt, degl)):
            r[...] = v

    blk = (DN_HEADS, C, DN_DK)
    blk2 = (DN_HEADS, C, C)
    rim = lambda i: (0, n - 1 - i, 0)
    sp, sp2 = pl.BlockSpec(blk, rim), pl.BlockSpec(blk2, rim)
    full, full2 = jax.ShapeDtypeStruct((DN_HEADS, T, DN_DK), f32), jax.ShapeDtypeStruct((DN_HEADS, T, C), f32)
    return pl.pallas_call(
        body, name=name, grid=(n,),
        in_specs=[sp, sp2, sp, sp, sp, sp, pl.BlockSpec((DN_HEADS, 1, DN_DK, DN_DK), lambda i: (0, n - 1 - i, 0, 0)), sp],
        out_specs=[sp, sp2, sp, sp, sp, sp],
        out_shape=[full, full2, full, full, full, full],
        scratch_shapes=[pltpu.VMEM((DN_HEADS, DN_DK, DN_DK), f32)],
        compiler_params=pltpu.CompilerParams(dimension_semantics=("arbitrary",), vmem_limit_bytes=VMEM_LIMIT),
    )(qd, qk, u, w, kt, egl, s_all, do)


def _dn_post_fwd(name, o, p, nw):
    T = p.shape[0]
    tm = _row_tile(T, 512)
    blk, im = _hm_spec(tm)
    W = DN_HEADS * DN_DK
    return _tile_call(name, lambda i, o, z, nw: _dn_post(o, z, nw), T // tm,
                      [(o, blk, im), (p, (tm, W), lambda i: (i, 1536 // W)), _full(nw)],
                      [((T, W), f32, (tm, W), lambda i: (i, 0))])[0]


def _dn_post_bwd(name, o, p, nw, dout):
    T = p.shape[0]
    tm = _row_tile(T, 512)
    blk, im = _hm_spec(tm)
    W = DN_HEADS * DN_DK

    def fn(i, o, z, nw, dout):
        _, vjp = jax.vjp(_dn_post, o, z, nw)
        return vjp(dout)

    return _tile_call(name, fn, T // tm,
                      [(o, blk, im), (p, (tm, W), lambda i: (i, 1536 // W)), _full(nw), _rows(dout, tm)],
                      [((DN_HEADS, T, DN_DK), f32, blk, im), ((T, W), f32, (tm, W), lambda i: (i, 0))],
                      [((1, DN_DK), f32)])


def _swa_specs(T):
    W = WINDOW
    qs = ((SWA_HEADS, W, SWA_DH), lambda i: (0, i, 0))
    kc = ((SWA_KV_HEADS, W, SWA_DH), lambda i: (0, i, 0))
    kp = ((SWA_KV_HEADS, W, SWA_DH), lambda i: (0, _prev(i), 0))
    return qs, kc, kp


def _swa_fwd(name, q, k, v, snk):
    T = q.shape[1]
    qs, kc, kp = _swa_specs(T)
    return _tile_call(name, lambda i, *a: _swa(i == 0, *a), T // WINDOW,
                      [(q, *qs), (k, *kc), (k, *kp), (v, *kc), (v, *kp), _full(snk)],
                      [((SWA_HEADS, T, SWA_DH), f32, *qs)])[0]


def _swa_bwd(name, q, k, v, snk, do):
    T = q.shape[1]
    qs, kc, kp = _swa_specs(T)

    def fn(i, q, k_c, k_p, v_c, v_p, snk, do):
        _, vjp = jax.vjp(lambda *a: _swa(i == 0, *a), q, k_c, k_p, v_c, v_p, snk)
        return vjp(do)

    kv_out = ((SWA_KV_HEADS, T, SWA_DH), f32, *kc)
    return _tile_call(name, fn, T // WINDOW,
                      [(q, *qs), (k, *kc), (k, *kp), (v, *kc), (v, *kp), _full(snk), (do, *qs)],
                      [((SWA_HEADS, T, SWA_DH), f32, *qs), kv_out, kv_out, kv_out, kv_out],
                      [(tuple(snk.shape), f32)])


def _swa_comb(name, cur, prv):
    T = cur.shape[1]
    n = T // WINDOW
    blk = (SWA_KV_HEADS, WINDOW, SWA_DH)

    def fn(i, c, p):
        return c + jnp.where(i == n - 1, 0.0, p)

    return _tile_call(name, fn, n, [(cur, blk, lambda i: (0, i, 0)), (prv, blk, lambda i: (0, jnp.minimum(i + 1, n - 1), 0))],
                      [(tuple(cur.shape), f32, blk, lambda i: (0, i, 0))])[0]


def _xa_fwd(name, p, kv):
    T = p.shape[0]
    tm = _row_tile(T, 256)
    W = XA_HEADS * XA_DH
    return _tile_call(name, lambda i, q, kv: _xa(q, kv), T // tm, [(p, (tm, W), lambda i: (i, 2560 // W)), _full(kv)],
                      [((T, W), f32, (tm, W), lambda i: (i, 0))])[0]


def _xa_bwd(name, p, kv, do):
    T = p.shape[0]
    tm = _row_tile(T, 256)
    W = XA_HEADS * XA_DH

    def fn(i, q, kv, do):
        _, vjp = jax.vjp(_xa, q, kv)
        return vjp(do)

    return _tile_call(name, fn, T // tm, [(p, (tm, W), lambda i: (i, 2560 // W)), _full(kv), _rows(do, tm)],
                      [((T, W), bf16, (tm, W), lambda i: (i, 0))], [(tuple(kv.shape), f32)])


def _merge_fwd(name, o_dn, o_sw, o_xa, p, wb):
    T = p.shape[0]
    tm = _row_tile(T, 256)
    GW = N_BRANCH * D_MODEL
    return _tile_call(name, lambda i, a, b, c, g, w: _merge(a, b, c, g, w.astype(f32)), T // tm,
                      [_rows(o_dn, tm), _rows(o_sw, tm), _rows(o_xa, tm), (p, (tm, GW), lambda i: (i, 1)), _full(wb)],
                      [((T, D_MODEL), bf16, (tm, D_MODEL), lambda i: (i, 0))])[0]


def _merge_bwd(name, o_dn, o_sw, o_xa, p, wb, dm):
    T = p.shape[0]
    tm = _row_tile(T, 256)
    GW = N_BRANCH * D_MODEL

    def fn(i, a, b, c, g, w, dm):
        _, vjp = jax.vjp(_merge, a, b, c, g, w.astype(f32))
        return vjp(dm)

    bo = ((T, BRANCH_W), f32, (tm, BRANCH_W), lambda i: (i, 0))
    return _tile_call(name, fn, T // tm,
                      [_rows(o_dn, tm), _rows(o_sw, tm), _rows(o_xa, tm), (p, (tm, GW), lambda i: (i, 1)), _full(wb), _rows(dm, tm)],
                      [bo, bo, bo, ((T, GW), bf16, (tm, GW), lambda i: (i, 0))], [(tuple(wb.shape), f32)])


def _assemble_dp(name, dxc, dxp, dz, dswq, dxaq, dgates, dswk, dswv, dba):
    T = dz.shape[0]
    tm = _row_tile(T, 256)
    n = T // tm

    def fn(i, dxc, dxp, dz, dswq, dxaq, dgates, dswk, dswv, dba):
        dqkv = dxc + jnp.where(i == n - 1, 0.0, dxp)
        parts = [dqkv, dz, dswq, dxaq, dgates, dswk, dswv, dba, jnp.zeros((tm, LANES), f32)]
        return jnp.concatenate([t.astype(f32) for t in parts], axis=1)

    ins = [_rows(dxc, tm), (dxp, (tm, dxp.shape[1]), lambda i: (jnp.minimum(i + 1, n - 1), 0))]
    ins += [_rows(t, tm) for t in (dz, dswq, dxaq, dgates, dswk, dswv, dba)]
    return _tile_call(name, fn, n, ins, [((T, D_INP), bf16, (tm, D_INP), lambda i: (i, 0))])[0]


def _loss_grad(name, y, tgt):
    T = y.shape[0]
    tm = _row_tile(T, 512)

    def fn(i, y, t):
        e = y - t
        part = 0.5 * jnp.sum(jnp.mean(e * e, axis=-1, keepdims=True), axis=0, keepdims=True)
        return e * (1.0 / D_MODEL), jnp.broadcast_to(part, (1, LANES))

    return _tile_call(name, fn, T // tm, [_rows(y, tm), _rows(tgt, tm)],
                      [((T, D_MODEL), f32, (tm, D_MODEL), lambda i: (i, 0))], [((1, LANES), f32)])


def _mem_ln_fwd(name, mem, g, b):
    M = mem.shape[0]
    return _tile_call(name, lambda i, m, g, b: (_ln(m, g, b),), 1, [_full(mem), _full(g), _full(b)],
                      [((M, D_MODEL), bf16, (M, D_MODEL), lambda i: (0, 0))])[0]


def _mem_ln_bwd(name, mem, g, b, dmn):
    def fn(i, m, g, b, d):
        _, vjp = jax.vjp(lambda g, b: _ln(m, g, b), g, b)
        return vjp(d)

    return _tile_call(name, fn, 1, [_full(mem), _full(g), _full(b), _full(dmn)], [], [((1, D_MODEL), f32), ((1, D_MODEL), f32)])


def _pad_w_in(w):
    z = jnp.zeros((w.shape[0], D_INP - D_IN), w.dtype)
    return jnp.concatenate([w[:, 0:1536], w[:, 1544:2056], w[:, 2056:2568], w[:, 2824:3336], w[:, 3336:6408],
                            w[:, 2568:2696], w[:, 2696:2824], w[:, 1536:1544], z], axis=1)


def _unpad_dw_in(d):
    return jnp.concatenate([d[:, 0:1536], d[:, 6400:6408], d[:, 1536:2048], d[:, 2048:2560], d[:, 6144:6272],
                            d[:, 6272:6400], d[:, 2560:3072], d[:, 3072:6144]], axis=1)


def _heads_major(x, nh):
    return x.reshape(x.shape[0], nh, SWA_DH).transpose(1, 0, 2)


def _heads_minor(x):
    return x.transpose(1, 0, 2).reshape(x.shape[1], -1)


def _lane_row(v, rows=8):
    out = jnp.zeros((rows, LANES), f32)
    return out.at[0, :v.shape[0]].set(v)


def _layer_fwd(l, x, xb, mem_nb, W):
    n = lambda s: f"l{l}_{s}"
    sv = {}
    u1 = _mm(n("ffn1_gu"), xb, W['ffn1_w_gu'], 'nn', out_dtype=bf16)
    a1 = _swiglu_fwd(n("ffn1_act"), u1)
    f1 = _mm(n("ffn1_down"), a1, W['ffn1_w_down'], 'nn')
    h1, h1b = _res_ln_fwd(n("ln0"), x, f1, W['ln_g'][0:1], W['ln_b'][0:1], 0.5)
    p = _mm(n("w_in"), h1b, W['w_inp'], 'nn')
    q, k, v, gb, bb = _dn_pre_fwd(n("dn_pre"), p, W['conv_w'], W['hp'])
    qd, qk, u, w, kt, egl = _wy_fwd(n("dn_wy"), q, k, v, gb, bb)
    o_raw, s_all = _scan_fwd(n("dn_scan"), qd, qk, u, w, kt, egl)
    o_dn = _dn_post_fwd(n("dn_post"), o_raw, p, W['norm_w'])
    q_sw, k_sw, v_sw = _heads_major(p[:, 2048:2560], SWA_HEADS), _heads_major(p[:, 6144:6272], SWA_KV_HEADS), _heads_major(p[:, 6272:6400], SWA_KV_HEADS)
    o_sw = _heads_minor(_swa_fwd(n("swa"), q_sw, k_sw, v_sw, W['snk']))
    kv = _mm(n("mem_kv"), mem_nb, W['w_mem_kv'], 'nn')
    o_xa = _xa_fwd(n("xa"), p, kv)
    merged = _merge_fwd(n("merge"), o_dn, o_sw, o_xa, p, W['w_branch'])
    mo = _mm(n("w_out"), merged, W['w_out'], 'nn')
    h2, h2b = _res_ln_fwd(n("ln1"), h1, mo, W['ln_g'][1:2], W['ln_b'][1:2], 1.0)
    u2 = _mm(n("ffn2_gu"), h2b, W['ffn2_w_gu'], 'nn', out_dtype=bf16)
    a2 = _swiglu_fwd(n("ffn2_act"), u2)
    f2 = _mm(n("ffn2_down"), a2, W['ffn2_w_down'], 'nn')
    h3, h3b = _res_ln_fwd(n("ln2"), h2, f2, W['ln_g'][2:3], W['ln_b'][2:3], 0.5)
    sv = dict(x=x, xb=xb, u1=u1, a1=a1, f1=f1, h1=h1, h1b=h1b, p=p, dn=(q, k, v, gb, bb), wy=(qd, qk, u, w, kt, egl),
              o_raw=o_raw, s_all=s_all, o_dn=o_dn, sw=(q_sw, k_sw, v_sw), o_sw=o_sw, kv=kv, o_xa=o_xa, merged=merged,
              mo=mo, h2=h2, h2b=h2b, u2=u2, a2=a2, f2=f2)
    return h3, h3b, sv


def _ffn_bwd(n, tag, x, xb, u, a, f, g, b, w_gu, w_down, dh):
    dx_a, df, dg, db = _res_ln_bwd(n(f"{tag}_ln_bwd"), x, f, g, b, 0.5, dh)
    d_down = _mm(n(f"{tag}_down_dw"), a, df, 'tn')
    da = _mm(n(f"{tag}_down_dx"), df, w_down, 'nt', out_dtype=bf16)
    du = _swiglu_bwd(n(f"{tag}_act_bwd"), u, da)
    d_gu = _mm(n(f"{tag}_gu_dw"), xb, du, 'tn')
    dx = _mm(n(f"{tag}_gu_dx"), du, w_gu, 'nt', add=dx_a)
    return dx, d_gu, d_down, dg, db


def _layer_bwd(l, sv, mem_nb, W, dh3, dmem_acc):
    n = lambda s: f"l{l}_{s}"
    G = {}
    dh2, G['ffn2_w_gu'], G['ffn2_w_down'], dg2, db2 = _ffn_bwd(
        n, "ffn2", sv['h2'], sv['h2b'], sv['u2'], sv['a2'], sv['f2'], W['ln_g'][2:3], W['ln_b'][2:3], W['ffn2_w_gu'], W['ffn2_w_down'], dh3)
    dh1_a, dmo, dg1, db1 = _res_ln_bwd(n("ln1_bwd"), sv['h1'], sv['mo'], W['ln_g'][1:2], W['ln_b'][1:2], 1.0, dh2)
    G['w_out'] = _mm(n("w_out_dw"), sv['merged'], dmo, 'tn')
    dmerged = _mm(n("w_out_dx"), dmo, W['w_out'], 'nt')
    p = sv['p']
    do_dn, do_sw, do_xa, dgates, G['w_branch'] = _merge_bwd(n("merge_bwd"), sv['o_dn'], sv['o_sw'], sv['o_xa'], p, W['w_branch'], dmerged)
    dxaq, dkv = _xa_bwd(n("xa_bwd"), p, sv['kv'], do_xa)
    dkv = dkv.astype(bf16)
    G['w_mem_kv'] = _mm(n("mem_kv_dw"), mem_nb, dkv, 'tn')
    dmem_n = _mm(n("mem_kv_dx"), dkv, W['w_mem_kv'], 'nt', add=dmem_acc)
    q_sw, k_sw, v_sw = sv['sw']
    dq_sw, dkc, dkp, dvc, dvp, dsnk = _swa_bwd(n("swa_bwd"), q_sw, k_sw, v_sw, W['snk'], _heads_major(do_sw, SWA_HEADS))
    dswq = _heads_minor(dq_sw)
    dswk = _heads_minor(_swa_comb(n("swa_dk"), dkc, dkp))
    dswv = _heads_minor(_swa_comb(n("swa_dv"), dvc, dvp))
    do_raw, dz, dnw = _dn_post_bwd(n("dn_post_bwd"), sv['o_raw'], p, W['norm_w'], do_dn)
    cts = _scan_bwd(n("dn_scan_bwd"), *sv['wy'], sv['s_all'], do_raw)
    dcts = _wy_bwd(n("dn_wy_bwd"), *sv['dn'], cts)
    dxc, dxp, dba, dcw, dhp = _dn_pre_bwd(n("dn_pre_bwd"), p, W['conv_w'], W['hp'], dcts)
    dp = _assemble_dp(n("dp"), dxc, dxp, dz, dswq, dxaq, dgates, dswk, dswv, dba)
    G['w_in'] = _unpad_dw_in(_mm(n("w_in_dw"), sv['h1b'], dp, 'tn'))
    dh1 = _mm(n("w_in_dx"), dp, W['w_inp'], 'nt', add=dh1_a)
    dx, G['ffn1_w_gu'], G['ffn1_w_down'], dg0, db0 = _ffn_bwd(
        n, "ffn1", sv['x'], sv['xb'], sv['u1'], sv['a1'], sv['f1'], W['ln_g'][0:1], W['ln_b'][0:1], W['ffn1_w_gu'], W['ffn1_w_down'], dh1)
    G['ln_g'] = jnp.concatenate([dg0, dg1, dg2], axis=0)
    G['ln_b'] = jnp.concatenate([db0, db1, db2], axis=0)
    G['dn_conv_w'] = dcw
    G['dn_a_log'] = dhp[0, :DN_HEADS]
    G['dn_dt_bias'] = dhp[1, :DN_HEADS]
    G['dn_norm_w'] = dnw[0]
    G['swa_sinks'] = dsnk[:, 0]
    return dx, dmem_n, G


def _local_step(x, mem, tgt, Wf):
    mem_g, mem_b = Wf['mem_ln_g'][None, :], Wf['mem_ln_b'][None, :]
    mem_nb = _mem_ln_fwd("mem_ln", mem, mem_g, mem_b)
    layers = []
    for l in range(DEPTH):
        layers.append(dict(
            ln_g=Wf['ln_g'][l], ln_b=Wf['ln_b'][l], ffn1_w_gu=Wf['ffn1_w_gu'][l], ffn1_w_down=Wf['ffn1_w_down'][l],
            w_inp=_pad_w_in(Wf['w_in'][l]), conv_w=Wf['dn_conv_w'][l],
            hp=jnp.zeros((8, LANES), f32).at[0, :DN_HEADS].set(Wf['dn_a_log'][l]).at[1, :DN_HEADS].set(Wf['dn_dt_bias'][l]),
            norm_w=Wf['dn_norm_w'][l][None, :], snk=jnp.broadcast_to(Wf['swa_sinks'][l][:, None], (SWA_HEADS, LANES)),
            w_mem_kv=Wf['w_mem_kv'][l], w_branch=Wf['w_branch'][l], w_out=Wf['w_out'][l],
            ffn2_w_gu=Wf['ffn2_w_gu'][l], ffn2_w_down=Wf['ffn2_w_down'][l]))
    h, hb = x, x.astype(bf16)
    saved = []
    for l in range(DEPTH):
        h, hb, sv = _layer_fwd(l, h, hb, mem_nb, layers[l])
        saved.append(sv)
    dh, loss_row = _loss_grad("loss", h, tgt)
    grads = [None] * DEPTH
    dmem_n = None
    for l in reversed(range(DEPTH)):
        dh, dmem_n, grads[l] = _layer_bwd(l, saved[l], mem_nb, layers[l], dh, dmem_n)
    dmg, dmb = _mem_ln_bwd("mem_ln_bwd", mem, mem_g, mem_b, dmem_n)
    G = {k: jnp.stack([grads[l][k] for l in range(DEPTH)], axis=0) for k in grads[0]}
    G['mem_ln_g'], G['mem_ln_b'] = dmg[0], dmb[0]
    return loss_row, dh, G


SEG_ALIGN = 2048


def _round_up(n, m):
    return (n + m - 1) // m * m


def _layout(names, sizes, mult):
    off, table = 0, {}
    for nm in names:
        table[nm] = (off, sizes[nm])
        off += _round_up(sizes[nm], SEG_ALIGN)
    return table, _round_up(off // LANES, mult)


def _pack(table, rows, flat):
    names = list(table)
    lead = flat[names[0]].shape[:-1]
    parts, pos = [], 0
    for nm in names:
        off, size = table[nm]
        if off > pos:
            parts.append(jnp.zeros(lead + (off - pos,), flat[nm].dtype))
        parts.append(flat[nm])
        pos = off + size
    total = rows * LANES
    if total > pos:
        parts.append(jnp.zeros(lead + (total - pos,), parts[-1].dtype))
    return jnp.concatenate(parts, axis=-1).reshape(lead + (rows, LANES))


def _unpack(table, packed, nm):
    off, size = table[nm]
    flat = packed.reshape(packed.shape[:-2] + (-1,))
    return flat[..., off:off + size]


def _split_shards(full, ax):
    shp = full.shape
    t = full.reshape(shp[:ax] + (N_SHARD, shp[ax] // N_SHARD) + shp[ax + 1:])
    return jnp.moveaxis(t, ax, 0)


def _join_shards(sh4, ax):
    return jnp.concatenate([sh4[s] for s in range(N_SHARD)], axis=ax)


ANY = pl.BlockSpec(memory_space=pl.ANY)


def _me():
    return lax.axis_index("x"), lax.axis_index("y"), lax.axis_index("c")


def _comm_call(name, body, arrays, out_shapes, n_sem, n_local=0):
    n = len(arrays)
    scratch = [pltpu.SemaphoreType.DMA((n, n_sem)), pltpu.SemaphoreType.DMA((n, n_sem))]
    if n_local:
        scratch.append(pltpu.SemaphoreType.DMA((n,)))
    return pl.pallas_call(
        body, name=name, out_shape=out_shapes, in_specs=[ANY] * n, out_specs=[ANY] * n, scratch_shapes=scratch,
        compiler_params=pltpu.CompilerParams(has_side_effects=True),
    )(*arrays)


def _all_gather(name, xs):
    n = len(xs)

    def body(*refs):
        x_refs, out_refs = refs[:n], refs[n:2 * n]
        send_sems, recv_sems, local_sems = refs[2 * n:]
        mx, my, mc = _me()
        chips = [(1 - mx, my), (mx, 1 - my), (1 - mx, 1 - my)]

        def copy(a, k, shard, half, to, src=None):
            dst = out_refs[a].at[shard, half]
            return pltpu.make_async_remote_copy(src_ref=dst if src is None else src, dst_ref=dst, send_sem=send_sems.at[a, k],
                                                recv_sem=recv_sems.at[a, k], device_id=to, device_id_type=MESH)

        mine = [pltpu.make_async_copy(x_refs[a], out_refs[a].at[2 * mx + my], local_sems.at[a]) for a in range(n)]
        for cp in mine:
            cp.start()
        first = [copy(a, j, 2 * mx + my, mc, (cx, cy, mc), src=x_refs[a].at[mc]) for j, (cx, cy) in enumerate(chips) for a in range(n)]
        for cp in first:
            cp.start()
        passed = []
        for j, (cx, cy) in enumerate(chips):
            for a in range(n):
                copy(a, j, 2 * cx + cy, mc, (mx, my, mc)).wait_recv()
                passed.append(copy(a, 3 + j, 2 * cx + cy, mc, (mx, my, 1 - mc)))
                passed[-1].start()
        for j, (cx, cy) in enumerate(chips):
            for a in range(n):
                copy(a, 3 + j, 2 * cx + cy, 1 - mc, (mx, my, mc)).wait_recv()
        for cp in first + passed:
            cp.wait_send()
        for cp in mine:
            cp.wait()

    return _comm_call(name, body, xs, [jax.ShapeDtypeStruct((N_SHARD,) + x.shape, x.dtype) for x in xs], 6, n_local=1)


def _pair_exchange(name, items):
    n = len(items)

    def body(*refs):
        src_refs, dst_refs = refs[:n], refs[n:2 * n]
        send_sems, recv_sems = refs[2 * n:]
        mx, my, mc = _me()
        cps = [pltpu.make_async_remote_copy(src_ref=src_refs[a].at[s, 1 - mc], dst_ref=dst_refs[a].at[s], send_sem=send_sems.at[a, s],
                                            recv_sem=recv_sems.at[a, s], device_id=(mx, my, 1 - mc), device_id_type=MESH)
               for a in range(n) for s in range(N_SHARD)]
        for cp in cps:
            cp.start()
        for cp in cps:
            cp.wait()

    return _comm_call(name, body, items, [jax.ShapeDtypeStruct((N_SHARD,) + t.shape[2:], t.dtype) for t in items], N_SHARD)


def _chip_exchange(name, parts):
    n = len(parts)

    def body(*refs):
        p_refs, dst_refs = refs[:n], refs[n:2 * n]
        send_sems, recv_sems = refs[2 * n:]
        mx, my, mc = _me()
        chips = [(1 - mx, my), (mx, 1 - my), (1 - mx, 1 - my)]
        cps = [pltpu.make_async_remote_copy(src_ref=p_refs[a].at[2 * cx + cy], dst_ref=dst_refs[a].at[j], send_sem=send_sems.at[a, j],
                                            recv_sem=recv_sems.at[a, j], device_id=(cx, cy, mc), device_id_type=MESH)
               for a in range(n) for j, (cx, cy) in enumerate(chips)]
        for cp in cps:
            cp.start()
        for cp in cps:
            cp.wait()

    return _comm_call(name, body, parts, [jax.ShapeDtypeStruct((3,) + t.shape[1:], t.dtype) for t in parts], 3)


def _pair_gather(name, reds):
    n = len(reds)

    def body(*refs):
        r_refs, out_refs = refs[:n], refs[n:2 * n]
        send_sems, recv_sems, local_sems = refs[2 * n:]
        mx, my, mc = _me()

        def to_sibling(a, half):
            return pltpu.make_async_remote_copy(src_ref=r_refs[a], dst_ref=out_refs[a].at[half], send_sem=send_sems.at[a, 0],
                                                recv_sem=recv_sems.at[a, 0], device_id=(mx, my, 1 - mc), device_id_type=MESH)

        mine = [pltpu.make_async_copy(r_refs[a], out_refs[a].at[mc], local_sems.at[a]) for a in range(n)]
        sent = [to_sibling(a, mc) for a in range(n)]
        for cp in mine + sent:
            cp.start()
        for a in range(n):
            to_sibling(a, 1 - mc).wait_recv()
        for cp in sent:
            cp.wait_send()
        for cp in mine:
            cp.wait()

    return _comm_call(name, body, reds, [jax.ShapeDtypeStruct((2,) + t.shape, t.dtype) for t in reds], 1, n_local=1)


EW_BLOCK_BYTES = 1 << 20


def _ew_call(name, fn, ins, n_out):
    shape = ins[0].shape
    last = shape[-1]
    flat = [a.reshape(-1, last) for a in ins]
    R = flat[0].shape[0]
    cands = tuple(c for c in (4096, 2048, 1024, 512, 256, 128, 64, 32, 16, 8) if c * last * 4 <= EW_BLOCK_BYTES)
    tr = _pick(R, cands)
    res = _tile_call(name, lambda i, *a: fn(*a), R // tr, [(a, (tr, last), lambda i: (i, 0)) for a in flat],
                     [((R, last), f32, (tr, last), lambda i: (i, 0))] * n_out)
    return [r.reshape(shape) for r in res]


def _adamw(g, w, m, v):
    m = B1 * m + (1.0 - B1) * g
    v = B2 * v + (1.0 - B2) * jnp.square(g)
    m_hat = m / (1.0 - B1 ** STEP)
    v_hat = v / (1.0 - B2 ** STEP)
    return -LR * (m_hat / (jnp.sqrt(v_hat) + EPS) + WD * w), m, v


def kernel(x, mem, mem_ln_g, mem_ln_b, ln_g, ln_b, ffn1_w_gu, ffn1_w_down, w_in, dn_conv_w, dn_a_log, dn_dt_bias, dn_norm_w, swa_sinks, w_mem_kv, w_branch, w_out, ffn2_w_gu, ffn2_w_down, loss_target, m_mem_ln_g, m_mem_ln_b, m_ln_g, m_ln_b, m_ffn1_w_gu, m_ffn1_w_down, m_w_in, m_dn_conv_w, m_dn_a_log, m_dn_dt_bias, m_dn_norm_w, m_swa_sinks, m_w_mem_kv, m_w_branch, m_w_out, m_ffn2_w_gu, m_ffn2_w_down, v_mem_ln_g, v_mem_ln_b, v_ln_g, v_ln_b, v_ffn1_w_gu, v_ffn1_w_down, v_w_in, v_dn_conv_w, v_dn_a_log, v_dn_dt_bias, v_dn_norm_w, v_swa_sinks, v_w_mem_kv, v_w_branch, v_w_out, v_ffn2_w_gu, v_ffn2_w_down):
    args = dict(locals())
    Ws = {n: args[n] for n in WEIGHTS}
    Ms = {n: args["m_" + n] for n in WEIGHTS}
    Vs = {n: args["v_" + n] for n in WEIGHTS}
    mc = lax.axis_index("c")
    my_s = 2 * lax.axis_index("x") + lax.axis_index("y")
    small = [n for n in WEIGHTS if n not in MATRICES]

    ag_table, ag_rows = _layout(SMALL_SHARDED, {n: Ws[n].size for n in SMALL_SHARDED}, 16)
    ag_small = _pack(ag_table, ag_rows, {n: Ws[n].reshape(-1) for n in SMALL_SHARDED}).reshape(2, ag_rows // 2, LANES)
    gathered = _all_gather("all_gather_w", [Ws[n].astype(bf16) for n in MATRICES] + [ag_small])
    Wf = {n: _join_shards(g, SHARD_AXIS[n]) for n, g in zip(MATRICES, gathered)}
    g_small = gathered[-1].reshape(N_SHARD, ag_rows, LANES)
    for n in SMALL_SHARDED:
        Wf[n] = _join_shards(_unpack(ag_table, g_small, n).reshape((N_SHARD,) + Ws[n].shape), SHARD_AXIS[n])
    for n in WEIGHTS:
        if SHARD_AXIS[n] is None:
            Wf[n] = Ws[n]

    loss_row, dx, G = _local_step(x[0], mem[0], loss_target[0], Wf)

    table, rows = _layout(small + ['loss'], {**{n: Ws[n].size for n in small}, 'loss': 1}, 16)
    gflat = {n: (jnp.broadcast_to(G[n].reshape(1, -1), (N_SHARD, G[n].size)) if SHARD_AXIS[n] is None
                 else _split_shards(G[n], SHARD_AXIS[n]).reshape(N_SHARD, -1)) for n in small}
    gflat['loss'] = jnp.broadcast_to(loss_row[:, :1], (N_SHARD, 1))
    items = [_split_shards(G[n], SHARD_AXIS[n]) for n in MATRICES] + [_pack(table, rows, gflat).reshape(N_SHARD, 2, rows // 2, LANES)]
    tags = MATRICES + ['small']
    got = _pair_exchange("rs_pair", items)
    part = [_ew_call(f"rs_add_pair_{t}", lambda a, b: a + b, [lax.dynamic_index_in_dim(a, mc, axis=1, keepdims=False), g], 1)[0]
            for t, a, g in zip(tags, items, got)]
    others = _chip_exchange("rs_chips", part)
    red = [_ew_call(f"rs_add_chips_{t}", lambda a, fx, fy, fxy: (a + fy) + (fx + fxy),
                    [lax.dynamic_index_in_dim(p, my_s, axis=0, keepdims=False), o[0], o[1], o[2]], 1)[0]
           for t, p, o in zip(tags, part, others)]
    gsum = _pair_gather("rs_gather", red)

    outs = {}
    for n, g in zip(MATRICES, gsum):
        outs[n] = (g,) + tuple(_ew_call(f"adamw_{n}", _adamw, [g, Ws[n], Ms[n], Vs[n]], 3))
    gs = gsum[-1].reshape(rows, LANES)
    fill = {'loss': jnp.zeros((1,), f32)}
    packs = [_pack(table, rows, {**{n: src[n].reshape(-1) for n in small}, **fill}) for src in (Ws, Ms, Vs)]
    small_out = (gs,) + tuple(_ew_call("adamw_small", _adamw, [gs] + packs, 3))
    for n in small:
        outs[n] = tuple(_unpack(table, p, n).reshape(Ws[n].shape) for p in small_out)
    loss = _unpack(table, gs, 'loss').reshape(())
    return (loss, dx[None], *[outs[n][k] for k in range(4) for n in WEIGHTS])
```

```python
import functools

import jax
import jax.numpy as jnp
from jax import lax
from jax.experimental import pallas as pl
from jax.experimental.pallas import tpu as pltpu

f32, bf16, i32 = jnp.float32, jnp.bfloat16, jnp.int32
HI = lax.Precision.HIGHEST
MESH = pl.DeviceIdType.MESH

D_MODEL = 1024
DEPTH = 2
DN_HEADS, DN_DK, DN_CONV, DN_CHUNK = 4, 128, 4, 64
SWA_HEADS, SWA_KV_HEADS, SWA_DH, WINDOW = 8, 2, 64, 128
XA_HEADS, XA_DH = 4, 128
D_FF = 2816
N_BRANCH, BRANCH_W = 3, 512
ALPHA = (2 * DEPTH) ** 0.25
LN_EPS, RMS_EPS, NEG_INF = 1e-5, 1e-6, -1e30
D_IN = 6408
D_INP = 6656
LR, B1, B2, EPS, WD, STEP = 0.001, 0.9, 0.999, 1e-08, 0.01, 10

LANES = 128
VMEM_LIMIT = 56 << 20
N_SHARD = 4
ROW_ALIGN = 2048

WEIGHTS = ['mem_ln_g', 'mem_ln_b', 'ln_g', 'ln_b', 'ffn1_w_gu', 'ffn1_w_down', 'w_in', 'dn_conv_w', 'dn_a_log',
           'dn_dt_bias', 'dn_norm_w', 'swa_sinks', 'w_mem_kv', 'w_branch', 'w_out', 'ffn2_w_gu', 'ffn2_w_down']
SHARD_AXIS = {'mem_ln_g': None, 'mem_ln_b': None, 'ln_g': 2, 'ln_b': 2, 'ffn1_w_gu': 2, 'ffn1_w_down': 1, 'w_in': 2,
              'dn_conv_w': 2, 'dn_a_log': None, 'dn_dt_bias': None, 'dn_norm_w': None, 'swa_sinks': None,
              'w_mem_kv': 1, 'w_branch': 3, 'w_out': 1, 'ffn2_w_gu': 2, 'ffn2_w_down': 1}
MATRICES = ['ffn1_w_gu', 'ffn1_w_down', 'w_in', 'w_mem_kv', 'w_branch', 'w_out', 'ffn2_w_gu', 'ffn2_w_down']
SMALL_SHARDED = ['ln_g', 'ln_b', 'dn_conv_w']


def _dg(a, b, mode, hi):
    nb = a.ndim - 2
    bd = tuple(range(nb))
    ca = nb if mode == 'tn' else nb + 1
    cb = nb + 1 if mode == 'nt' else nb
    dims = (((ca,), (cb,)), (bd, bd))
    dot = lambda x, y: lax.dot_general(x, y, dims, preferred_element_type=f32)
    a_hi, b_hi = a.astype(bf16), b.astype(bf16)
    if not hi:
        return dot(a_hi, b_hi)
    a_lo = (a.astype(f32) - a_hi.astype(f32)).astype(bf16)
    b_lo = (b.astype(f32) - b_hi.astype(f32)).astype(bf16)
    return dot(a_hi, b_hi) + (dot(a_hi, b_lo) + dot(a_lo, b_hi))


@functools.partial(jax.custom_vjp, nondiff_argnums=(2, 3))
def _dot(a, b, mode, hi):
    return _dg(a, b, mode, hi)


def _dot_fwd(a, b, mode, hi):
    return _dg(a, b, mode, hi), (a, b)


def _dot_bwd(mode, hi, res, g):
    a, b = res
    if mode == 'nn':
        da, db = _dg(g, b, 'nt', hi), _dg(a, g, 'tn', hi)
    elif mode == 'nt':
        da, db = _dg(g, b, 'nn', hi), _dg(g, a, 'tn', hi)
    else:
        da, db = _dg(b, g, 'nt', hi), _dg(a, g, 'nn', hi)
    return da.astype(a.dtype), db.astype(b.dtype)


_dot.defvjp(_dot_fwd, _dot_bwd)


def bdot(a, b, mode):
    return _dot(a, b, mode, False)


def hdot(a, b, mode):
    return _dot(a, b, mode, True)


def _shift_rows_impl(x, xprev, s):
    rows = lax.broadcasted_iota(i32, x.shape, 0)
    return jnp.where(rows >= s, pltpu.roll(x, s, 0), pltpu.roll(xprev, s, 0))


@functools.partial(jax.custom_vjp, nondiff_argnums=(2,))
def _shift_rows(x, xprev, s):
    return _shift_rows_impl(x, xprev, s)


def _shift_rows_fwd(x, xprev, s):
    return _shift_rows_impl(x, xprev, s), None


def _shift_rows_bwd(s, _, g):
    n = g.shape[0]
    rows = lax.broadcasted_iota(i32, g.shape, 0)
    r = pltpu.roll(g, n - s, 0)
    keep = rows < n - s
    return jnp.where(keep, r, 0.0), jnp.where(keep, 0.0, r)


_shift_rows.defvjp(_shift_rows_fwd, _shift_rows_bwd)


def _lane_pick(x, lane):
    idx = lax.broadcasted_iota(i32, x.shape, x.ndim - 1)
    return jnp.sum(jnp.where(idx == lane, x, 0.0), axis=-1, keepdims=True)


def _silu(x):
    return x * jax.nn.sigmoid(x)


def _tri_inv(a):
    C = a.shape[-1]
    eye = (lax.broadcasted_iota(i32, a.shape, 1) == lax.broadcasted_iota(i32, a.shape, 2)).astype(f32)
    p = -a
    x = eye + p
    for _ in range((C - 1).bit_length() - 1):
        p = _dg(p, p, 'nn', True)
        x = x + _dg(x, p, 'nn', True)
    return x


@jax.custom_vjp
def _tri_solve(a, rhs):
    return _dg(_tri_inv(a), rhs, 'nn', True)


def _tri_solve_fwd(a, rhs):
    tinv = _tri_inv(a)
    sol = _dg(tinv, rhs, 'nn', True)
    return sol, (tinv, sol)


def _tri_solve_bwd(res, g):
    tinv, sol = res
    d_rhs = _dg(tinv, g, 'tn', True)
    return -_dg(d_rhs, sol, 'nt', True), d_rhs


_tri_solve.defvjp(_tri_solve_fwd, _tri_solve_bwd)


@functools.partial(jax.custom_vjp, nondiff_argnums=(1,))
def _lane_head(x, n):
    return x[:, :, :n]


def _lane_head_fwd(x, n):
    return x[:, :, :n], None


def _lane_head_bwd(n, _, g):
    s = jnp.sum(g, axis=-1, keepdims=True) * (1.0 / LANES)
    return (jnp.broadcast_to(s, g.shape[:-1] + (LANES,)),)


_lane_head.defvjp(_lane_head_fwd, _lane_head_bwd)


@functools.partial(jax.custom_vjp, nondiff_argnums=(1,))
def _last_row(x, c):
    return x[:, c - 1:, :]


def _last_row_fwd(x, c):
    return x[:, c - 1:, :], None


def _last_row_bwd(c, _, g):
    shape = (g.shape[0], c, g.shape[2])
    rows = lax.broadcasted_iota(i32, shape, 1)
    return (jnp.where(rows == c - 1, jnp.broadcast_to(g, shape), 0.0),)


_last_row.defvjp(_last_row_fwd, _last_row_bwd)


def _full(a):
    nd = a.ndim
    return (a, tuple(a.shape), lambda i, _nd=nd: (0,) * _nd)


def _rows(a, tm, col=0, width=None):
    width = a.shape[1] if width is None else width
    return (a, (tm, width), lambda i, _c=col: (i, _c))


def _tile_call(name, fn, n, ins, outs, accs=()):
    n_in, n_out, n_acc = len(ins), len(outs), len(accs)

    def body(*refs):
        i = pl.program_id(0)
        res = fn(i, *[r[...] for r in refs[:n_in]])
        if not isinstance(res, (tuple, list)):
            res = (res,)
        assert len(res) == n_out + n_acc, (name, len(res), n_out, n_acc)
        for r, v in zip(refs[n_in:n_in + n_out], res[:n_out]):
            r[...] = v.astype(r.dtype)
        if n_acc:
            acc_refs = refs[n_in + n_out:]

            @pl.when(i == 0)
            def _():
                for r in acc_refs:
                    r[...] = jnp.zeros(r.shape, r.dtype)

            for r, v in zip(acc_refs, res[n_out:]):
                r[...] += v.astype(r.dtype)

    out_shape = [jax.ShapeDtypeStruct(s, d) for s, d, _, _ in outs] + [jax.ShapeDtypeStruct(s, d) for s, d in accs]
    out_specs = [pl.BlockSpec(b, m) for _, _, b, m in outs]
    out_specs += [pl.BlockSpec(tuple(s), lambda i, _nd=len(s): (0,) * _nd) for s, _ in accs]
    res = pl.pallas_call(
        body, name=name, grid=(n,),
        in_specs=[pl.BlockSpec(b, m) for _, b, m in ins],
        out_specs=out_specs, out_shape=out_shape,
        compiler_params=pltpu.CompilerParams(dimension_semantics=("arbitrary",), vmem_limit_bytes=VMEM_LIMIT),
    )(*[a for a, _, _ in ins])
    return res


def _pick(n, cands):
    for c in cands:
        if n % c == 0:
            return c
    return n


def _mm(name, a, b, mode, out_dtype=f32, add=None):
    if mode == 'tn':
        K, M = a.shape
    else:
        M, K = a.shape
    N = b.shape[0] if mode == 'nt' else b.shape[1]
    tm = _pick(M, (1024, 1408, 512, 256, 128))
    tn = _pick(N, (512, 1408, 256, 128))
    tk = K if K <= 3328 else _pick(K, (3328, 2816, 2048, 1024, 512, 256, 128))
    nk = K // tk
    ca = 0 if mode == 'tn' else 1
    cb = 1 if mode == 'nt' else 0
    dims = (((ca,), (cb,)), ((), ()))

    def body(*refs):
        a_ref, b_ref = refs[0], refs[1]
        add_ref = refs[2] if add is not None else None
        part = lax.dot_general(a_ref[...].astype(bf16), b_ref[...].astype(bf16), dims, preferred_element_type=f32)

        def finish(r, o_ref):
            if add_ref is not None:
                r = r + add_ref[...].astype(f32)
            o_ref[...] = r.astype(o_ref.dtype)

        if nk == 1:
            finish(part, refs[-1])
            return
        o_ref, acc_ref = refs[-2], refs[-1]
        k = pl.program_id(2)

        @pl.when(k == 0)
        def _():
            acc_ref[...] = part

        @pl.when(k > 0)
        def _():
            acc_ref[...] += part

        @pl.when(k == nk - 1)
        def _():
            finish(acc_ref[...], o_ref)

    a_spec = pl.BlockSpec((tk, tm), lambda i, j, k: (k, i)) if mode == 'tn' else pl.BlockSpec((tm, tk), lambda i, j, k: (i, k))
    b_spec = pl.BlockSpec((tn, tk), lambda i, j, k: (j, k)) if mode == 'nt' else pl.BlockSpec((tk, tn), lambda i, j, k: (k, j))
    in_specs, args = [a_spec, b_spec], [a, b]
    if add is not None:
        in_specs.append(pl.BlockSpec((tm, tn), lambda i, j, k: (i, j)))
        args.append(add)
    return pl.pallas_call(
        body, name=name, grid=(M // tm, N // tn, nk), in_specs=in_specs,
        out_specs=pl.BlockSpec((tm, tn), lambda i, j, k: (i, j)),
        out_shape=jax.ShapeDtypeStruct((M, N), out_dtype),
        scratch_shapes=[pltpu.VMEM((tm, tn), f32)] if nk > 1 else [],
        compiler_params=pltpu.CompilerParams(dimension_semantics=("parallel", "parallel", "arbitrary"), vmem_limit_bytes=VMEM_LIMIT),
    )(*args)


def _swiglu(u):
    return _silu(u[:, :D_FF]) * u[:, D_FF:]


def _res_ln(x, f, g, b, s):
    r = ALPHA * x + s * f
    mu = jnp.mean(r, axis=-1, keepdims=True)
    rc = r - mu
    var = jnp.mean(rc * rc, axis=-1, keepdims=True)
    return rc * lax.rsqrt(var + LN_EPS) * g + b


def _ln(x, g, b):
    mu = jnp.mean(x, axis=-1, keepdims=True)
    xc = x - mu
    var = jnp.mean(xc * xc, axis=-1, keepdims=True)
    return xc * lax.rsqrt(var + LN_EPS) * g + b


def _dn_pre(first, xc, xp, ba, cw, hp):
    xp = jnp.where(first, 0.0, xp)
    y = cw[DN_CONV - 1:DN_CONV, :] * xc
    for j in range(DN_CONV - 1):
        y = y + cw[j:j + 1, :] * _shift_rows(xc, xp, DN_CONV - 1 - j)
    c = _silu(y)
    qs, ks, vs, gs, bs = [], [], [], [], []
    nqk = DN_HEADS * DN_DK
    for h in range(DN_HEADS):
        q = c[:, h * DN_DK:(h + 1) * DN_DK]
        k = c[:, nqk + h * DN_DK:nqk + (h + 1) * DN_DK]
        v = c[:, 2 * nqk + h * DN_DK:2 * nqk + (h + 1) * DN_DK]
        qs.append(q * lax.rsqrt(jnp.sum(q * q, axis=-1, keepdims=True) + RMS_EPS))
        ks.append(k * lax.rsqrt(jnp.sum(k * k, axis=-1, keepdims=True) + RMS_EPS))
        vs.append(v)
        beta = jax.nn.sigmoid(_lane_pick(ba, h))
        a_log = _lane_pick(hp[0:1, :], h)
        dt = _lane_pick(hp[1:2, :], h)
        g = -jnp.exp(a_log) * jax.nn.softplus(_lane_pick(ba, DN_HEADS + h) + dt)
        gs.append(jnp.broadcast_to(g, q.shape))
        bs.append(jnp.broadcast_to(beta, q.shape))
    return tuple(jnp.stack(t, axis=0) for t in (qs, ks, vs, gs, bs))


def _wy(q, k, v, gb, bb):
    B, C, _ = q.shape
    ri = lax.broadcasted_iota(i32, (B, C, C), 1)
    ci = lax.broadcasted_iota(i32, (B, C, C), 2)
    tril, strict = ri >= ci, ri > ci
    gc = hdot(tril.astype(f32), gb, 'nn')
    gl = jnp.broadcast_to(_last_row(gc, C), gc.shape)
    col = _lane_head(gc, C)
    decay = jnp.exp(jnp.where(tril, col - jnp.swapaxes(col, 1, 2), NEG_INF))
    qs = q * (DN_DK ** -0.5)
    kb = k * bb
    a = jnp.where(strict, bdot(kb, k, 'nt') * decay, 0.0)
    sol = _tri_solve(a, jnp.concatenate([v * bb, kb * jnp.exp(gc)], axis=-1))
    qk = jnp.where(tril, bdot(qs, k, 'nt') * decay, 0.0)
    kt = k * jnp.exp(gl - gc)
    qd = qs * jnp.exp(gc)
    return qd, qk, sol[..., :DN_DK], sol[..., DN_DK:], kt, jnp.exp(gl)


def _scan_step(S, qd, qk, u, w, kt, egl):
    vn = u - bdot(w, S, 'nn')
    o = bdot(qd, S, 'nn') + bdot(qk, vn, 'nn')
    e2 = jnp.concatenate([egl] * (DN_DK // DN_CHUNK), axis=1)
    return o, S * e2 + bdot(kt, vn, 'tn')


def _dn_post(o, z, nw):
    outs = []
    for h in range(DN_HEADS):
        oh = o[h]
        oh = oh * lax.rsqrt(jnp.mean(oh * oh, axis=-1, keepdims=True) + RMS_EPS) * nw
        outs.append(oh * _silu(z[:, h * DN_DK:(h + 1) * DN_DK]))
    return jnp.concatenate(outs, axis=1)


def _swa(first, q, kc, kp, vc, vp, snk):
    W = q.shape[1]
    G = SWA_HEADS // SWA_KV_HEADS
    r = lax.broadcasted_iota(i32, (G, W, 2 * W), 1)
    c = lax.broadcasted_iota(i32, (G, W, 2 * W), 2)
    mask = (c > r) & (c <= W + r) & jnp.logical_or(c >= W, jnp.logical_not(first))
    sink_all = _lane_pick(snk, 0)
    outs = []
    for j in range(SWA_KV_HEADS):
        qj = q[j * G:(j + 1) * G]
        kk = jnp.broadcast_to(jnp.concatenate([kp[j], kc[j]], axis=0)[None], (G, 2 * W, SWA_DH))
        vv = jnp.broadcast_to(jnp.concatenate([vp[j], vc[j]], axis=0)[None], (G, 2 * W, SWA_DH))
        s = jnp.where(mask, bdot(qj, kk, 'nt') * (SWA_DH ** -0.5), NEG_INF)
        sink = sink_all[j * G:(j + 1) * G][:, :, None]
        m = jnp.maximum(jnp.max(s, axis=-1, keepdims=True), sink)
        p = jnp.exp(s - m)
        p = p / (jnp.sum(p, axis=-1, keepdims=True) + jnp.exp(sink - m))
        outs.append(bdot(p, vv, 'nn'))
    return jnp.concatenate(outs, axis=0)


def _xa(q, kv):
    outs = []
    nk = XA_HEADS * XA_DH
    for h in range(XA_HEADS):
        qh = q[:, h * XA_DH:(h + 1) * XA_DH]
        kh = kv[:, h * XA_DH:(h + 1) * XA_DH]
        vh = kv[:, nk + h * XA_DH:nk + (h + 1) * XA_DH]
        s = bdot(qh, kh, 'nt') * (XA_DH ** -0.5)
        m = jnp.max(s, axis=-1, keepdims=True)
        p = jnp.exp(s - m)
        p = p / jnp.sum(p, axis=-1, keepdims=True)
        outs.append(bdot(p, vh, 'nn'))
    return jnp.concatenate(outs, axis=1)


def _merge(o_dn, o_sw, o_xa, gates, wb):
    acc = None
    for n, o in enumerate((o_dn, o_sw, o_xa)):
        t = jax.nn.sigmoid(gates[:, n * D_MODEL:(n + 1) * D_MODEL]) * bdot(o, wb[n], 'nn')
        acc = t if acc is None else acc + t
    return acc


def _row_tile(T, want):
    return _pick(T, tuple(c for c in (1024, 512, 256, 128, 64) if c <= want))


def _res_ln_fwd(name, x, f, g, b, s):
    T = x.shape[0]
    tm = _row_tile(T, 512)

    def fn(i, x, f, g, b):
        h = _res_ln(x, f, g, b, s)
        return h, h

    return _tile_call(name, fn, T // tm, [_rows(x, tm), _rows(f, tm), _full(g), _full(b)],
                      [((T, D_MODEL), f32, (tm, D_MODEL), lambda i: (i, 0)), ((T, D_MODEL), bf16, (tm, D_MODEL), lambda i: (i, 0))])


def _res_ln_bwd(name, x, f, g, b, s, dh):
    T = x.shape[0]
    tm = _row_tile(T, 512)

    def fn(i, x, f, g, b, dh):
        _, vjp = jax.vjp(lambda x, f, g, b: _res_ln(x, f, g, b, s), x, f, g, b)
        return vjp(dh)

    return _tile_call(name, fn, T // tm, [_rows(x, tm), _rows(f, tm), _full(g), _full(b), _rows(dh, tm)],
                      [((T, D_MODEL), f32, (tm, D_MODEL), lambda i: (i, 0)), ((T, D_MODEL), bf16, (tm, D_MODEL), lambda i: (i, 0))],
                      [((1, D_MODEL), f32), ((1, D_MODEL), f32)])


def _swiglu_fwd(name, u):
    T = u.shape[0]
    tm = _row_tile(T, 256)
    return _tile_call(name, lambda i, u: _swiglu(u.astype(f32)), T // tm, [_rows(u, tm)],
                      [((T, D_FF), bf16, (tm, D_FF), lambda i: (i, 0))])[0]


def _swiglu_bwd(name, u, da):
    T = u.shape[0]
    tm = _row_tile(T, 256)

    def fn(i, u, da):
        _, vjp = jax.vjp(_swiglu, u.astype(f32))
        return vjp(da.astype(f32))[0]

    return _tile_call(name, fn, T // tm, [_rows(u, tm), _rows(da, tm)],
                      [((T, 2 * D_FF), bf16, (tm, 2 * D_FF), lambda i: (i, 0))])[0]


def _hm_spec(tm, w=DN_DK):
    return ((DN_HEADS, tm, w), lambda i: (0, i, 0))


def _prev(i):
    return jnp.maximum(i - 1, 0)


def _dn_pre_fwd(name, p, cw, hp):
    T = p.shape[0]
    tm = _row_tile(T, 256)
    n = T // tm
    W3 = 3 * DN_HEADS * DN_DK
    ins = [(p, (tm, W3), lambda i: (i, 0)), (p, (tm, W3), lambda i: (_prev(i), 0)),
           (p, (tm, LANES), lambda i: (i, 6400 // LANES)), _full(cw), _full(hp)]
    blk, im = _hm_spec(tm)
    return _tile_call(name, lambda i, *a: _dn_pre(i == 0, *a), n, ins, [((DN_HEADS, T, DN_DK), f32, blk, im)] * 5)


def _dn_pre_bwd(name, p, cw, hp, cts):
    T = p.shape[0]
    tm = _row_tile(T, 256)
    n = T // tm
    W3 = 3 * DN_HEADS * DN_DK
    blk, im = _hm_spec(tm)
    ins = [(p, (tm, W3), lambda i: (i, 0)), (p, (tm, W3), lambda i: (_prev(i), 0)),
           (p, (tm, LANES), lambda i: (i, 6400 // LANES)), _full(cw), _full(hp)] + [(c, blk, im) for c in cts]

    def fn(i, xc, xp, ba, cw, hp, *cts):
        _, vjp = jax.vjp(lambda *a: _dn_pre(i == 0, *a), xc, xp, ba, cw, hp)
        return vjp(tuple(cts))

    return _tile_call(name, fn, n, ins,
                      [((T, W3), f32, (tm, W3), lambda i: (i, 0)), ((T, W3), f32, (tm, W3), lambda i: (i, 0)),
                       ((T, LANES), f32, (tm, LANES), lambda i: (i, 0))],
                      [(tuple(cw.shape), f32), (tuple(hp.shape), f32)])


def _wy_split(x, nc):
    return x.reshape(DN_HEADS * nc, DN_CHUNK, x.shape[-1])


def _wy_fwd(name, q, k, v, gb, bb):
    T = q.shape[1]
    tm = _row_tile(T, 256)
    nc = tm // DN_CHUNK

    def fn(i, *a):
        outs = _wy(*[_wy_split(t, nc) for t in a])
        return [o.reshape(DN_HEADS, tm, o.shape[-1]) for o in outs]

    blk, im = _hm_spec(tm)
    blk2, im2 = _hm_spec(tm, DN_CHUNK)
    ins = [(t, blk, im) for t in (q, k, v, gb, bb)]
    full = (DN_HEADS, T, DN_DK)
    outs = [(full, bf16, blk, im), ((DN_HEADS, T, DN_CHUNK), bf16, blk2, im2), (full, f32, blk, im),
            (full, bf16, blk, im), (full, bf16, blk, im), (full, f32, blk, im)]
    return _tile_call(name, fn, T // tm, ins, outs)


def _wy_bwd(name, q, k, v, gb, bb, cts):
    T = q.shape[1]
    tm = _row_tile(T, 256)
    nc = tm // DN_CHUNK

    def fn(i, q, k, v, gb, bb, *cts):
        _, vjp = jax.vjp(_wy, *[_wy_split(t, nc) for t in (q, k, v, gb, bb)])
        grads = vjp(tuple(_wy_split(c.astype(f32), nc) for c in cts))
        return [g.reshape(DN_HEADS, tm, DN_DK) for g in grads]

    blk, im = _hm_spec(tm)
    blk2, im2 = _hm_spec(tm, DN_CHUNK)
    ins = [(t, blk, im) for t in (q, k, v, gb, bb)]
    ins += [(c, blk2 if c.shape[-1] == DN_CHUNK else blk, im) for c in cts]
    return _tile_call(name, fn, T // tm, ins, [((DN_HEADS, T, DN_DK), f32, blk, im)] * 5)


def _scan_fwd(name, qd, qk, u, w, kt, egl):
    T = qd.shape[1]
    C = DN_CHUNK
    n = T // C

    def body(qd_ref, qk_ref, u_ref, w_ref, kt_ref, egl_ref, o_ref, sall_ref, s_ref):
        i = pl.program_id(0)

        @pl.when(i == 0)
        def _():
            s_ref[...] = jnp.zeros(s_ref.shape, f32)

        S = s_ref[...]
        sall_ref[...] = S[:, None]
        o, Sn = _scan_step(S, qd_ref[...], qk_ref[...], u_ref[...], w_ref[...], kt_ref[...], egl_ref[...])
        o_ref[...] = o
        s_ref[...] = Sn

    blk, im = _hm_spec(C)
    blk2, im2 = _hm_spec(C, C)
    return pl.pallas_call(
        body, name=name, grid=(n,),
        in_specs=[pl.BlockSpec(blk, im), pl.BlockSpec(blk2, im2)] + [pl.BlockSpec(blk, im)] * 4,
        out_specs=[pl.BlockSpec(blk, im), pl.BlockSpec((DN_HEADS, 1, DN_DK, DN_DK), lambda i: (0, i, 0, 0))],
        out_shape=[jax.ShapeDtypeStruct((DN_HEADS, T, DN_DK), f32), jax.ShapeDtypeStruct((DN_HEADS, n, DN_DK, DN_DK), f32)],
        scratch_shapes=[pltpu.VMEM((DN_HEADS, DN_DK, DN_DK), f32)],
        compiler_params=pltpu.CompilerParams(dimension_semantics=("arbitrary",), vmem_limit_bytes=VMEM_LIMIT),
    )(qd, qk, u, w, kt, egl)


def _scan_bwd(name, qd, qk, u, w, kt, egl, s_all, do):
    T = qd.shape[1]
    C = DN_CHUNK
    n = T // C

    def body(qd_ref, qk_ref, u_ref, w_ref, kt_ref, egl_ref, sall_ref, do_ref,
             dqd_ref, dqk_ref, du_ref, dw_ref, dkt_ref, degl_ref, ds_ref):
        i = pl.program_id(0)

        @pl.when(i == 0)
        def _():
            ds_ref[...] = jnp.zeros(ds_ref.shape, f32)

        S = sall_ref[...][:, 0]
        args = [r[...].astype(f32) for r in (qd_ref, qk_ref, u_ref, w_ref, kt_ref, egl_ref)]
        _, vjp = jax.vjp(_scan_step, S, *args)
        dS, dqd, dqk, du, dw, dkt, degl = vjp((do_ref[...], ds_ref[...]))
        ds_ref[...] = dS
        for r, v in zip((dqd_ref, dqk_ref, du_ref, dw_ref, dkt_ref, degl_ref), (dqd, dqk, du, dw, dkt, degl)):
            r[...] = v

    blk = (DN_HEADS, C, DN_DK)
    blk2 = (DN_HEADS, C, C)
    rim = lambda i: (0, n - 1 - i, 0)
    sp, sp2 = pl.BlockSpec(blk, rim), pl.BlockSpec(blk2, rim)
    full, full2 = jax.ShapeDtypeStruct((DN_HEADS, T, DN_DK), f32), jax.ShapeDtypeStruct((DN_HEADS, T, C), f32)
    return pl.pallas_call(
        body, name=name, grid=(n,),
        in_specs=[sp, sp2, sp, sp, sp, sp, pl.BlockSpec((DN_HEADS, 1, DN_DK, DN_DK), lambda i: (0, n - 1 - i, 0, 0)), sp],
        out_specs=[sp, sp2, sp, sp, sp, sp],
        out_shape=[full, full2, full, full, full, full],
        scratch_shapes=[pltpu.VMEM((DN_HEADS, DN_DK, DN_DK), f32)],
        compiler_params=pltpu.CompilerParams(dimension_semantics=("arbitrary",), vmem_limit_bytes=VMEM_LIMIT),
    )(qd, qk, u, w, kt, egl, s_all, do)


def _dn_post_fwd(name, o, p, nw):
    T = p.shape[0]
    tm = _row_tile(T, 512)
    blk, im = _hm_spec(tm)
    W = DN_HEADS * DN_DK
    return _tile_call(name, lambda i, o, z, nw: _dn_post(o, z, nw), T // tm,
                      [(o, blk, im), (p, (tm, W), lambda i: (i, 1536 // W)), _full(nw)],
                      [((T, W), f32, (tm, W), lambda i: (i, 0))])[0]


def _dn_post_bwd(name, o, p, nw, dout):
    T = p.shape[0]
    tm = _row_tile(T, 512)
    blk, im = _hm_spec(tm)
    W = DN_HEADS * DN_DK

    def fn(i, o, z, nw, dout):
        _, vjp = jax.vjp(_dn_post, o, z, nw)
        return vjp(dout)

    return _tile_call(name, fn, T // tm,
                      [(o, blk, im), (p, (tm, W), lambda i: (i, 1536 // W)), _full(nw), _rows(dout, tm)],
                      [((DN_HEADS, T, DN_DK), f32, blk, im), ((T, W), f32, (tm, W), lambda i: (i, 0))],
                      [((1, DN_DK), f32)])


def _swa_specs(T):
    W = WINDOW
    qs = ((SWA_HEADS, W, SWA_DH), lambda i: (0, i, 0))
    kc = ((SWA_KV_HEADS, W, SWA_DH), lambda i: (0, i, 0))
    kp = ((SWA_KV_HEADS, W, SWA_DH), lambda i: (0, _prev(i), 0))
    return qs, kc, kp


def _swa_fwd(name, q, k, v, snk):
    T = q.shape[1]
    qs, kc, kp = _swa_specs(T)
    return _tile_call(name, lambda i, *a: _swa(i == 0, *a), T // WINDOW,
                      [(q, *qs), (k, *kc), (k, *kp), (v, *kc), (v, *kp), _full(snk)],
                      [((SWA_HEADS, T, SWA_DH), f32, *qs)])[0]


def _swa_bwd(name, q, k, v, snk, do):
    T = q.shape[1]
    qs, kc, kp = _swa_specs(T)

    def fn(i, q, k_c, k_p, v_c, v_p, snk, do):
        _, vjp = jax.vjp(lambda *a: _swa(i == 0, *a), q, k_c, k_p, v_c, v_p, snk)
        return vjp(do)

    kv_out = ((SWA_KV_HEADS, T, SWA_DH), f32, *kc)
    return _tile_call(name, fn, T // WINDOW,
                      [(q, *qs), (k, *kc), (k, *kp), (v, *kc), (v, *kp), _full(snk), (do, *qs)],
                      [((SWA_HEADS, T, SWA_DH), f32, *qs), kv_out, kv_out, kv_out, kv_out],
                      [(tuple(snk.shape), f32)])


def _swa_comb(name, cur, prv):
    T = cur.shape[1]
    n = T // WINDOW
    blk = (SWA_KV_HEADS, WINDOW, SWA_DH)

    def fn(i, c, p):
        return c + jnp.where(i == n - 1, 0.0, p)

    return _tile_call(name, fn, n, [(cur, blk, lambda i: (0, i, 0)), (prv, blk, lambda i: (0, jnp.minimum(i + 1, n - 1), 0))],
                      [(tuple(cur.shape), f32, blk, lambda i: (0, i, 0))])[0]


def _xa_fwd(name, p, kv):
    T = p.shape[0]
    tm = _row_tile(T, 256)
    W = XA_HEADS * XA_DH
    return _tile_call(name, lambda i, q, kv: _xa(q, kv), T // tm, [(p, (tm, W), lambda i: (i, 2560 // W)), _full(kv)],
                      [((T, W), f32, (tm, W), lambda i: (i, 0))])[0]


def _xa_bwd(name, p, kv, do):
    T = p.shape[0]
    tm = _row_tile(T, 256)
    W = XA_HEADS * XA_DH

    def fn(i, q, kv, do):
        _, vjp = jax.vjp(_xa, q, kv)
        return vjp(do)

    return _tile_call(name, fn, T // tm, [(p, (tm, W), lambda i: (i, 2560 // W)), _full(kv), _rows(do, tm)],
                      [((T, W), bf16, (tm, W), lambda i: (i, 0))], [(tuple(kv.shape), f32)])


def _merge_fwd(name, o_dn, o_sw, o_xa, p, wb):
    T = p.shape[0]
    tm = _row_tile(T, 256)
    GW = N_BRANCH * D_MODEL
    return _tile_call(name, lambda i, a, b, c, g, w: _merge(a, b, c, g, w.astype(f32)), T // tm,
                      [_rows(o_dn, tm), _rows(o_sw, tm), _rows(o_xa, tm), (p, (tm, GW), lambda i: (i, 1)), _full(wb)],
                      [((T, D_MODEL), bf16, (tm, D_MODEL), lambda i: (i, 0))])[0]


def _merge_bwd(name, o_dn, o_sw, o_xa, p, wb, dm):
    T = p.shape[0]
    tm = _row_tile(T, 256)
    GW = N_BRANCH * D_MODEL

    def fn(i, a, b, c, g, w, dm):
        _, vjp = jax.vjp(_merge, a, b, c, g, w.astype(f32))
        return vjp(dm)

    bo = ((T, BRANCH_W), f32, (tm, BRANCH_W), lambda i: (i, 0))
    return _tile_call(name, fn, T // tm,
                      [_rows(o_dn, tm), _rows(o_sw, tm), _rows(o_xa, tm), (p, (tm, GW), lambda i: (i, 1)), _full(wb), _rows(dm, tm)],
                      [bo, bo, bo, ((T, GW), bf16, (tm, GW), lambda i: (i, 0))], [(tuple(wb.shape), f32)])


def _assemble_dp(name, dxc, dxp, dz, dswq, dxaq, dgates, dswk, dswv, dba):
    T = dz.shape[0]
    tm = _row_tile(T, 256)
    n = T // tm

    def fn(i, dxc, dxp, dz, dswq, dxaq, dgates, dswk, dswv, dba):
        dqkv = dxc + jnp.where(i == n - 1, 0.0, dxp)
        parts = [dqkv, dz, dswq, dxaq, dgates, dswk, dswv, dba, jnp.zeros((tm, LANES), f32)]
        return jnp.concatenate([t.astype(f32) for t in parts], axis=1)

    ins = [_rows(dxc, tm), (dxp, (tm, dxp.shape[1]), lambda i: (jnp.minimum(i + 1, n - 1), 0))]
    ins += [_rows(t, tm) for t in (dz, dswq, dxaq, dgates, dswk, dswv, dba)]
    return _tile_call(name, fn, n, ins, [((T, D_INP), bf16, (tm, D_INP), lambda i: (i, 0))])[0]


def _loss_grad(name, y, tgt):
    T = y.shape[0]
    tm = _row_tile(T, 512)

    def fn(i, y, t):
        e = y - t
        part = 0.5 * jnp.sum(jnp.mean(e * e, axis=-1, keepdims=True), axis=0, keepdims=True)
        return e * (1.0 / D_MODEL), jnp.broadcast_to(part, (1, LANES))

    return _tile_call(name, fn, T // tm, [_rows(y, tm), _rows(tgt, tm)],
                      [((T, D_MODEL), f32, (tm, D_MODEL), lambda i: (i, 0))], [((1, LANES), f32)])


def _mem_ln_fwd(name, mem, g, b):
    M = mem.shape[0]
    return _tile_call(name, lambda i, m, g, b: (_ln(m, g, b),), 1, [_full(mem), _full(g), _full(b)],
                      [((M, D_MODEL), bf16, (M, D_MODEL), lambda i: (0, 0))])[0]


def _mem_ln_bwd(name, mem, g, b, dmn):
    def fn(i, m, g, b, d):
        _, vjp = jax.vjp(lambda g, b: _ln(m, g, b), g, b)
        return vjp(d)

    return _tile_call(name, fn, 1, [_full(mem), _full(g), _full(b), _full(dmn)], [], [((1, D_MODEL), f32), ((1, D_MODEL), f32)])


def _pad_w_in(w):
    z = jnp.zeros((w.shape[0], D_INP - D_IN), w.dtype)
    return jnp.concatenate([w[:, 0:1536], w[:, 1544:2056], w[:, 2056:2568], w[:, 2824:3336], w[:, 3336:6408],
                            w[:, 2568:2696], w[:, 2696:2824], w[:, 1536:1544], z], axis=1)


def _unpad_dw_in(d):
    return jnp.concatenate([d[:, 0:1536], d[:, 6400:6408], d[:, 1536:2048], d[:, 2048:2560], d[:, 6144:6272],
                            d[:, 6272:6400], d[:, 2560:3072], d[:, 3072:6144]], axis=1)


def _heads_major(x, nh):
    return x.reshape(x.shape[0], nh, SWA_DH).transpose(1, 0, 2)


def _heads_minor(x):
    return x.transpose(1, 0, 2).reshape(x.shape[1], -1)


def _lane_row(v, rows=8):
    out = jnp.zeros((rows, LANES), f32)
    return out.at[0, :v.shape[0]].set(v)


def _layer_fwd(l, x, xb, mem_nb, W):
    n = lambda s: f"l{l}_{s}"
    sv = {}
    u1 = _mm(n("ffn1_gu"), xb, W['ffn1_w_gu'], 'nn', out_dtype=bf16)
    a1 = _swiglu_fwd(n("ffn1_act"), u1)
    f1 = _mm(n("ffn1_down"), a1, W['ffn1_w_down'], 'nn')
    h1, h1b = _res_ln_fwd(n("ln0"), x, f1, W['ln_g'][0:1], W['ln_b'][0:1], 0.5)
    p = _mm(n("w_in"), h1b, W['w_inp'], 'nn')
    q, k, v, gb, bb = _dn_pre_fwd(n("dn_pre"), p, W['conv_w'], W['hp'])
    qd, qk, u, w, kt, egl = _wy_fwd(n("dn_wy"), q, k, v, gb, bb)
    o_raw, s_all = _scan_fwd(n("dn_scan"), qd, qk, u, w, kt, egl)
    o_dn = _dn_post_fwd(n("dn_post"), o_raw, p, W['norm_w'])
    q_sw, k_sw, v_sw = _heads_major(p[:, 2048:2560], SWA_HEADS), _heads_major(p[:, 6144:6272], SWA_KV_HEADS), _heads_major(p[:, 6272:6400], SWA_KV_HEADS)
    o_sw = _heads_minor(_swa_fwd(n("swa"), q_sw, k_sw, v_sw, W['snk']))
    kv = _mm(n("mem_kv"), mem_nb, W['w_mem_kv'], 'nn')
    o_xa = _xa_fwd(n("xa"), p, kv)
    merged = _merge_fwd(n("merge"), o_dn, o_sw, o_xa, p, W['w_branch'])
    mo = _mm(n("w_out"), merged, W['w_out'], 'nn')
    h2, h2b = _res_ln_fwd(n("ln1"), h1, mo, W['ln_g'][1:2], W['ln_b'][1:2], 1.0)
    u2 = _mm(n("ffn2_gu"), h2b, W['ffn2_w_gu'], 'nn', out_dtype=bf16)
    a2 = _swiglu_fwd(n("ffn2_act"), u2)
    f2 = _mm(n("ffn2_down"), a2, W['ffn2_w_down'], 'nn')
    h3, h3b = _res_ln_fwd(n("ln2"), h2, f2, W['ln_g'][2:3], W['ln_b'][2:3], 0.5)
    sv = dict(x=x, xb=xb, u1=u1, a1=a1, f1=f1, h1=h1, h1b=h1b, p=p, dn=(q, k, v, gb, bb), wy=(qd, qk, u, w, kt, egl),
              o_raw=o_raw, s_all=s_all, o_dn=o_dn, sw=(q_sw, k_sw, v_sw), o_sw=o_sw, kv=kv, o_xa=o_xa, merged=merged,
              mo=mo, h2=h2, h2b=h2b, u2=u2, a2=a2, f2=f2)
    return h3, h3b, sv


def _ffn_bwd(n, tag, x, xb, u, a, f, g, b, w_gu_t, w_down_t, dh):
    dx_a, df, dg, db = _res_ln_bwd(n(f"{tag}_ln_bwd"), x, f, g, b, 0.5, dh)
    d_down = _mm(n(f"{tag}_down_dw"), a, df, 'tn')
    da = _mm(n(f"{tag}_down_dx"), df, w_down_t, 'nn', out_dtype=bf16)
    du = _swiglu_bwd(n(f"{tag}_act_bwd"), u, da)
    d_gu = _mm(n(f"{tag}_gu_dw"), xb, du, 'tn')
    dx = _mm(n(f"{tag}_gu_dx"), du, w_gu_t, 'nn', add=dx_a)
    return dx, d_gu, d_down, dg, db


def _layer_bwd(l, sv, mem_nb, W, dh3, dmem_acc):
    n = lambda s: f"l{l}_{s}"
    G = {}
    dh2, G['ffn2_w_gu'], G['ffn2_w_down'], dg2, db2 = _ffn_bwd(
        n, "ffn2", sv['h2'], sv['h2b'], sv['u2'], sv['a2'], sv['f2'], W['ln_g'][2:3], W['ln_b'][2:3], W['ffn2_w_gu_t'], W['ffn2_w_down_t'], dh3)
    dh1_a, dmo, dg1, db1 = _res_ln_bwd(n("ln1_bwd"), sv['h1'], sv['mo'], W['ln_g'][1:2], W['ln_b'][1:2], 1.0, dh2)
    G['w_out'] = _mm(n("w_out_dw"), sv['merged'], dmo, 'tn')
    dmerged = _mm(n("w_out_dx"), dmo, W['w_out_t'], 'nn')
    p = sv['p']
    do_dn, do_sw, do_xa, dgates, G['w_branch'] = _merge_bwd(n("merge_bwd"), sv['o_dn'], sv['o_sw'], sv['o_xa'], p, W['w_branch'], dmerged)
    dxaq, dkv = _xa_bwd(n("xa_bwd"), p, sv['kv'], do_xa)
    dkv = dkv.astype(bf16)
    G['w_mem_kv'] = _mm(n("mem_kv_dw"), mem_nb, dkv, 'tn')
    dmem_n = _mm(n("mem_kv_dx"), dkv, W['w_mem_kv_t'], 'nn', add=dmem_acc)
    q_sw, k_sw, v_sw = sv['sw']
    dq_sw, dkc, dkp, dvc, dvp, dsnk = _swa_bwd(n("swa_bwd"), q_sw, k_sw, v_sw, W['snk'], _heads_major(do_sw, SWA_HEADS))
    dswq = _heads_minor(dq_sw)
    dswk = _heads_minor(_swa_comb(n("swa_dk"), dkc, dkp))
    dswv = _heads_minor(_swa_comb(n("swa_dv"), dvc, dvp))
    do_raw, dz, dnw = _dn_post_bwd(n("dn_post_bwd"), sv['o_raw'], p, W['norm_w'], do_dn)
    cts = _scan_bwd(n("dn_scan_bwd"), *sv['wy'], sv['s_all'], do_raw)
    dcts = _wy_bwd(n("dn_wy_bwd"), *sv['dn'], cts)
    dxc, dxp, dba, dcw, dhp = _dn_pre_bwd(n("dn_pre_bwd"), p, W['conv_w'], W['hp'], dcts)
    dp = _assemble_dp(n("dp"), dxc, dxp, dz, dswq, dxaq, dgates, dswk, dswv, dba)
    G['w_in'] = _unpad_dw_in(_mm(n("w_in_dw"), sv['h1b'], dp, 'tn'))
    dh1 = _mm(n("w_in_dx"), dp, W['w_inp_t'], 'nn', add=dh1_a)
    dx, G['ffn1_w_gu'], G['ffn1_w_down'], dg0, db0 = _ffn_bwd(
        n, "ffn1", sv['x'], sv['xb'], sv['u1'], sv['a1'], sv['f1'], W['ln_g'][0:1], W['ln_b'][0:1], W['ffn1_w_gu_t'], W['ffn1_w_down_t'], dh1)
    G['ln_g'] = jnp.concatenate([dg0, dg1, dg2], axis=0)
    G['ln_b'] = jnp.concatenate([db0, db1, db2], axis=0)
    G['dn_conv_w'] = dcw
    G['dn_a_log'] = dhp[0, :DN_HEADS]
    G['dn_dt_bias'] = dhp[1, :DN_HEADS]
    G['dn_norm_w'] = dnw[0]
    G['swa_sinks'] = dsnk[:, 0]
    return dx, dmem_n, G


def _local_step(x, mem, tgt, Wf):
    mem_g, mem_b = Wf['mem_ln_g'][None, :], Wf['mem_ln_b'][None, :]
    mem_nb = _mem_ln_fwd("mem_ln", mem, mem_g, mem_b)
    layers = []
    for l in range(DEPTH):
        layers.append(dict(
            ln_g=Wf['ln_g'][l], ln_b=Wf['ln_b'][l], ffn1_w_gu=Wf['ffn1_w_gu'][l], ffn1_w_down=Wf['ffn1_w_down'][l],
            w_inp=_pad_w_in(Wf['w_in'][l]), conv_w=Wf['dn_conv_w'][l],
            hp=jnp.zeros((8, LANES), f32).at[0, :DN_HEADS].set(Wf['dn_a_log'][l]).at[1, :DN_HEADS].set(Wf['dn_dt_bias'][l]),
            norm_w=Wf['dn_norm_w'][l][None, :], snk=jnp.broadcast_to(Wf['swa_sinks'][l][:, None], (SWA_HEADS, LANES)),
            w_mem_kv=Wf['w_mem_kv'][l], w_branch=Wf['w_branch'][l], w_out=Wf['w_out'][l],
            ffn2_w_gu=Wf['ffn2_w_gu'][l], ffn2_w_down=Wf['ffn2_w_down'][l]))
        for k in ('ffn1_w_gu', 'ffn1_w_down', 'w_inp', 'w_mem_kv', 'w_out', 'ffn2_w_gu', 'ffn2_w_down'):
            layers[l][k + '_t'] = layers[l][k].T
    h, hb = x, x.astype(bf16)
    saved = []
    for l in range(DEPTH):
        h, hb, sv = _layer_fwd(l, h, hb, mem_nb, layers[l])
        saved.append(sv)
    dh, loss_row = _loss_grad("loss", h, tgt)
    grads = [None] * DEPTH
    dmem_n = None
    for l in reversed(range(DEPTH)):
        dh, dmem_n, grads[l] = _layer_bwd(l, saved[l], mem_nb, layers[l], dh, dmem_n)
    dmg, dmb = _mem_ln_bwd("mem_ln_bwd", mem, mem_g, mem_b, dmem_n)
    G = {k: jnp.stack([grads[l][k] for l in range(DEPTH)], axis=0) for k in grads[0]}
    G['mem_ln_g'], G['mem_ln_b'] = dmg[0], dmb[0]
    return loss_row, dh, G


SEG_ALIGN = 2048


def _round_up(n, m):
    return (n + m - 1) // m * m


def _layout(names, sizes, mult):
    off, table = 0, {}
    for nm in names:
        table[nm] = (off, sizes[nm])
        off += _round_up(sizes[nm], SEG_ALIGN)
    return table, _round_up(off // LANES, mult)


def _pack(table, rows, flat):
    names = list(table)
    lead = flat[names[0]].shape[:-1]
    parts, pos = [], 0
    for nm in names:
        off, size = table[nm]
        if off > pos:
            parts.append(jnp.zeros(lead + (off - pos,), flat[nm].dtype))
        parts.append(flat[nm])
        pos = off + size
    total = rows * LANES
    if total > pos:
        parts.append(jnp.zeros(lead + (total - pos,), parts[-1].dtype))
    return jnp.concatenate(parts, axis=-1).reshape(lead + (rows, LANES))


def _unpack(table, packed, nm):
    off, size = table[nm]
    flat = packed.reshape(packed.shape[:-2] + (-1,))
    return flat[..., off:off + size]


def _split_shards(full, ax):
    shp = full.shape
    t = full.reshape(shp[:ax] + (N_SHARD, shp[ax] // N_SHARD) + shp[ax + 1:])
    return jnp.moveaxis(t, ax, 0)


def _join_shards(sh4, ax):
    return jnp.concatenate([sh4[s] for s in range(N_SHARD)], axis=ax)


ANY = pl.BlockSpec(memory_space=pl.ANY)


def _me():
    return lax.axis_index("x"), lax.axis_index("y"), lax.axis_index("c")


def _comm_call(name, body, arrays, out_shapes, n_sem):
    n = len(arrays)
    scratch = [pltpu.SemaphoreType.DMA((n, n_sem)), pltpu.SemaphoreType.DMA((n, n_sem))]
    return pl.pallas_call(
        body, name=name, out_shape=out_shapes, in_specs=[ANY] * n, out_specs=[ANY] * n, scratch_shapes=scratch,
        compiler_params=pltpu.CompilerParams(has_side_effects=True),
    )(*arrays)


def _all_gather(name, xs):
    n = len(xs)

    def body(*refs):
        x_refs, out_refs = refs[:n], refs[n:2 * n]
        send_sems, recv_sems = refs[2 * n:]
        mx, my, mc = _me()
        chips = [(1 - mx, my), (mx, 1 - my), (1 - mx, 1 - my)]

        def copy(a, k, shard, half, to, src=None):
            dst = out_refs[a].at[shard, half]
            return pltpu.make_async_remote_copy(src_ref=dst if src is None else src, dst_ref=dst, send_sem=send_sems.at[a, k],
                                                recv_sem=recv_sems.at[a, k], device_id=to, device_id_type=MESH)

        first = [copy(a, j, 2 * mx + my, mc, (cx, cy, mc), src=x_refs[a].at[mc]) for j, (cx, cy) in enumerate(chips) for a in range(n)]
        for cp in first:
            cp.start()
        passed = []
        for j, (cx, cy) in enumerate(chips):
            for a in range(n):
                copy(a, j, 2 * cx + cy, mc, (mx, my, mc)).wait_recv()
                passed.append(copy(a, 3 + j, 2 * cx + cy, mc, (mx, my, 1 - mc)))
                passed[-1].start()
        for j, (cx, cy) in enumerate(chips):
            for a in range(n):
                copy(a, 3 + j, 2 * cx + cy, 1 - mc, (mx, my, mc)).wait_recv()
        for cp in first + passed:
            cp.wait_send()

    return _comm_call(name, body, xs, [jax.ShapeDtypeStruct((N_SHARD,) + x.shape, x.dtype) for x in xs], 6)


def _pair_exchange(name, items):
    n = len(items)

    def body(*refs):
        src_refs, dst_refs = refs[:n], refs[n:2 * n]
        send_sems, recv_sems = refs[2 * n:]
        mx, my, mc = _me()
        cps = [pltpu.make_async_remote_copy(src_ref=src_refs[a].at[s, 1 - mc], dst_ref=dst_refs[a].at[s], send_sem=send_sems.at[a, s],
                                            recv_sem=recv_sems.at[a, s], device_id=(mx, my, 1 - mc), device_id_type=MESH)
               for a in range(n) for s in range(N_SHARD)]
        for cp in cps:
            cp.start()
        for cp in cps:
            cp.wait()

    return _comm_call(name, body, items, [jax.ShapeDtypeStruct((N_SHARD,) + t.shape[2:], t.dtype) for t in items], N_SHARD)


def _chip_exchange(name, parts):
    n = len(parts)

    def body(*refs):
        p_refs, dst_refs = refs[:n], refs[n:2 * n]
        send_sems, recv_sems = refs[2 * n:]
        mx, my, mc = _me()
        chips = [(1 - mx, my), (mx, 1 - my), (1 - mx, 1 - my)]
        cps = [pltpu.make_async_remote_copy(src_ref=p_refs[a].at[2 * cx + cy], dst_ref=dst_refs[a].at[j], send_sem=send_sems.at[a, j],
                                            recv_sem=recv_sems.at[a, j], device_id=(cx, cy, mc), device_id_type=MESH)
               for a in range(n) for j, (cx, cy) in enumerate(chips)]
        for cp in cps:
            cp.start()
        for cp in cps:
            cp.wait()

    return _comm_call(name, body, parts, [jax.ShapeDtypeStruct((3,) + t.shape[1:], t.dtype) for t in parts], 3)


def _pair_swap(name, reds):
    n = len(reds)

    def body(*refs):
        r_refs, out_refs = refs[:n], refs[n:2 * n]
        send_sems, recv_sems = refs[2 * n:]
        mx, my, mc = _me()
        cps = [pltpu.make_async_remote_copy(src_ref=r_refs[a], dst_ref=out_refs[a], send_sem=send_sems.at[a, 0],
                                            recv_sem=recv_sems.at[a, 0], device_id=(mx, my, 1 - mc), device_id_type=MESH)
               for a in range(n)]
        for cp in cps:
            cp.start()
        for cp in cps:
            cp.wait()

    return _comm_call(name, body, reds, [jax.ShapeDtypeStruct(t.shape, t.dtype) for t in reds], 1)


EW_BLOCK_BYTES = 1 << 20


def _ew_call(name, fn, ins, n_out):
    shape = ins[0].shape
    last = shape[-1]
    flat = [a.reshape(-1, last) for a in ins]
    R = flat[0].shape[0]
    cands = tuple(c for c in (4096, 2048, 1024, 512, 256, 128, 64, 32, 16, 8) if c * last * 4 <= EW_BLOCK_BYTES)
    tr = _pick(R, cands)
    res = _tile_call(name, lambda i, *a: fn(*a), R // tr, [(a, (tr, last), lambda i: (i, 0)) for a in flat],
                     [((R, last), f32, (tr, last), lambda i: (i, 0))] * n_out)
    return [r.reshape(shape) for r in res]


def _adamw(g, w, m, v):
    m = B1 * m + (1.0 - B1) * g
    v = B2 * v + (1.0 - B2) * jnp.square(g)
    m_hat = m / (1.0 - B1 ** STEP)
    v_hat = v / (1.0 - B2 ** STEP)
    return -LR * (m_hat / (jnp.sqrt(v_hat) + EPS) + WD * w), m, v


def _adamw_call(name, mine, theirs, w, m, v, mc1):
    shape, last = w.shape, w.shape[-1]
    g2 = [t.reshape(-1, last) for t in (mine, theirs)]
    w3 = [t.reshape(2, -1, last) for t in (w, m, v)]
    R = g2[0].shape[0]
    tr = _pick(R, tuple(c for c in (4096, 2048, 1024, 512, 256, 128, 64, 32, 16, 8) if c * last * 4 <= EW_BLOCK_BYTES))

    def body(mc_ref, mine_ref, theirs_ref, w_ref, m_ref, v_ref, g_out, d_out, m_out, v_out):
        g = jnp.where(pl.program_id(0) == mc_ref[0], mine_ref[...], theirs_ref[...])
        d, nm, nv = _adamw(g, w_ref[0], m_ref[0], v_ref[0])
        g_out[0], d_out[0], m_out[0], v_out[0] = g, d, nm, nv

    half = pl.BlockSpec((tr, last), lambda h, i: (i, 0))
    full = pl.BlockSpec((1, tr, last), lambda h, i: (h, i, 0))
    res = pl.pallas_call(
        body, name=name, grid=(2, R // tr),
        in_specs=[pl.BlockSpec(memory_space=pltpu.SMEM), half, half, full, full, full], out_specs=[full] * 4,
        out_shape=[jax.ShapeDtypeStruct((2, R, last), f32)] * 4,
        compiler_params=pltpu.CompilerParams(dimension_semantics=("arbitrary", "arbitrary"), vmem_limit_bytes=VMEM_LIMIT),
    )(mc1, *g2, *w3)
    return tuple(r.reshape(shape) for r in res)


def kernel(x, mem, mem_ln_g, mem_ln_b, ln_g, ln_b, ffn1_w_gu, ffn1_w_down, w_in, dn_conv_w, dn_a_log, dn_dt_bias, dn_norm_w, swa_sinks, w_mem_kv, w_branch, w_out, ffn2_w_gu, ffn2_w_down, loss_target, m_mem_ln_g, m_mem_ln_b, m_ln_g, m_ln_b, m_ffn1_w_gu, m_ffn1_w_down, m_w_in, m_dn_conv_w, m_dn_a_log, m_dn_dt_bias, m_dn_norm_w, m_swa_sinks, m_w_mem_kv, m_w_branch, m_w_out, m_ffn2_w_gu, m_ffn2_w_down, v_mem_ln_g, v_mem_ln_b, v_ln_g, v_ln_b, v_ffn1_w_gu, v_ffn1_w_down, v_w_in, v_dn_conv_w, v_dn_a_log, v_dn_dt_bias, v_dn_norm_w, v_swa_sinks, v_w_mem_kv, v_w_branch, v_w_out, v_ffn2_w_gu, v_ffn2_w_down):
    args = dict(locals())
    Ws = {n: args[n] for n in WEIGHTS}
    Ms = {n: args["m_" + n] for n in WEIGHTS}
    Vs = {n: args["v_" + n] for n in WEIGHTS}
    mc = lax.axis_index("c")
    my_s = 2 * lax.axis_index("x") + lax.axis_index("y")
    small = [n for n in WEIGHTS if n not in MATRICES]

    ag_table, ag_rows = _layout(SMALL_SHARDED, {n: Ws[n].size for n in SMALL_SHARDED}, 16)
    ag_small = _pack(ag_table, ag_rows, {n: Ws[n].reshape(-1) for n in SMALL_SHARDED}).reshape(2, ag_rows // 2, LANES)
    local = [Ws[n].astype(bf16) for n in MATRICES] + [ag_small]
    gathered = _all_gather("all_gather_w", local)
    gathered = [jnp.stack([jnp.where(my_s == s, loc, g[s]) for s in range(N_SHARD)], axis=0) for loc, g in zip(local, gathered)]
    Wf = {n: _join_shards(g, SHARD_AXIS[n]) for n, g in zip(MATRICES, gathered)}
    g_small = gathered[-1].reshape(N_SHARD, ag_rows, LANES)
    for n in SMALL_SHARDED:
        Wf[n] = _join_shards(_unpack(ag_table, g_small, n).reshape((N_SHARD,) + Ws[n].shape), SHARD_AXIS[n])
    for n in WEIGHTS:
        if SHARD_AXIS[n] is None:
            Wf[n] = Ws[n]

    loss_row, dx, G = _local_step(x[0], mem[0], loss_target[0], Wf)

    table, rows = _layout(small + ['loss'], {**{n: Ws[n].size for n in small}, 'loss': 1}, 16)
    gflat = {n: (jnp.broadcast_to(G[n].reshape(1, -1), (N_SHARD, G[n].size)) if SHARD_AXIS[n] is None
                 else _split_shards(G[n], SHARD_AXIS[n]).reshape(N_SHARD, -1)) for n in small}
    gflat['loss'] = jnp.broadcast_to(loss_row[:, :1], (N_SHARD, 1))
    items = [_split_shards(G[n], SHARD_AXIS[n]) for n in MATRICES] + [_pack(table, rows, gflat).reshape(N_SHARD, 2, rows // 2, LANES)]
    tags = MATRICES + ['small']
    got = _pair_exchange("rs_pair", items)
    part = [_ew_call(f"rs_add_pair_{t}", lambda a, b: a + b, [lax.dynamic_index_in_dim(a, mc, axis=1, keepdims=False), g], 1)[0]
            for t, a, g in zip(tags, items, got)]
    others = _chip_exchange("rs_chips", part)
    red = [_ew_call(f"rs_add_chips_{t}", lambda a, fx, fy, fxy: (a + fy) + (fx + fxy),
                    [lax.dynamic_index_in_dim(p, my_s, axis=0, keepdims=False), o[0], o[1], o[2]], 1)[0]
           for t, p, o in zip(tags, part, others)]
    theirs = _pair_swap("rs_swap", red)

    mc1 = mc.astype(i32).reshape(1)
    outs = {}
    for n, a, b in zip(MATRICES, red, theirs):
        outs[n] = _adamw_call(f"adamw_{n}", a, b, Ws[n], Ms[n], Vs[n], mc1)
    fill = {'loss': jnp.zeros((1,), f32)}
    packs = [_pack(table, rows, {**{n: src[n].reshape(-1) for n in small}, **fill}).reshape(2, rows // 2, LANES) for src in (Ws, Ms, Vs)]
    small_out = [p.reshape(rows, LANES) for p in _adamw_call("adamw_small", red[-1], theirs[-1], *packs, mc1)]
    for n in small:
        outs[n] = tuple(_unpack(table, p, n).reshape(Ws[n].shape) for p in small_out)
    loss = _unpack(table, small_out[0], 'loss').reshape(())
    return (loss, dx[None], *[outs[n][k] for k in range(4) for n in WEIGHTS])
```

```python
import functools

import jax
import jax.numpy as jnp
from jax import lax
from jax.experimental import pallas as pl
from jax.experimental.pallas import tpu as pltpu

f32, bf16, i32 = jnp.float32, jnp.bfloat16, jnp.int32
HI = lax.Precision.HIGHEST
MESH = pl.DeviceIdType.MESH

D_MODEL = 1024
DEPTH = 2
DN_HEADS, DN_DK, DN_CONV, DN_CHUNK = 4, 128, 4, 64
SWA_HEADS, SWA_KV_HEADS, SWA_DH, WINDOW = 8, 2, 64, 128
XA_HEADS, XA_DH = 4, 128
D_FF = 2816
N_BRANCH, BRANCH_W = 3, 512
ALPHA = (2 * DEPTH) ** 0.25
LN_EPS, RMS_EPS, NEG_INF = 1e-5, 1e-6, -1e30
D_IN = 6408
D_INP = 6656
LR, B1, B2, EPS, WD, STEP = 0.001, 0.9, 0.999, 1e-08, 0.01, 10

LANES = 128
VMEM_LIMIT = 56 << 20
N_SHARD = 4
SCAN_CHUNKS = 4

WEIGHTS = ['mem_ln_g', 'mem_ln_b', 'ln_g', 'ln_b', 'ffn1_w_gu', 'ffn1_w_down', 'w_in', 'dn_conv_w', 'dn_a_log',
           'dn_dt_bias', 'dn_norm_w', 'swa_sinks', 'w_mem_kv', 'w_branch', 'w_out', 'ffn2_w_gu', 'ffn2_w_down']
SHARD_AXIS = {'mem_ln_g': None, 'mem_ln_b': None, 'ln_g': 2, 'ln_b': 2, 'ffn1_w_gu': 2, 'ffn1_w_down': 1, 'w_in': 2,
              'dn_conv_w': 2, 'dn_a_log': None, 'dn_dt_bias': None, 'dn_norm_w': None, 'swa_sinks': None,
              'w_mem_kv': 1, 'w_branch': 3, 'w_out': 1, 'ffn2_w_gu': 2, 'ffn2_w_down': 1}
MATRICES = ['ffn1_w_gu', 'ffn1_w_down', 'w_in', 'w_mem_kv', 'w_branch', 'w_out', 'ffn2_w_gu', 'ffn2_w_down']
SMALL_SHARDED = ['ln_g', 'ln_b', 'dn_conv_w']


def _dg(a, b, mode, hi):
    nb = a.ndim - 2
    bd = tuple(range(nb))
    ca = nb if mode == 'tn' else nb + 1
    cb = nb + 1 if mode == 'nt' else nb
    dims = (((ca,), (cb,)), (bd, bd))
    dot = lambda x, y: lax.dot_general(x, y, dims, preferred_element_type=f32)
    a_hi, b_hi = a.astype(bf16), b.astype(bf16)
    if not hi:
        return dot(a_hi, b_hi)
    a_lo = (a.astype(f32) - a_hi.astype(f32)).astype(bf16)
    b_lo = (b.astype(f32) - b_hi.astype(f32)).astype(bf16)
    return dot(a_hi, b_hi) + (dot(a_hi, b_lo) + dot(a_lo, b_hi))


@functools.partial(jax.custom_vjp, nondiff_argnums=(2, 3))
def _dot(a, b, mode, hi):
    return _dg(a, b, mode, hi)


def _dot_fwd(a, b, mode, hi):
    return _dg(a, b, mode, hi), (a, b)


def _dot_bwd(mode, hi, res, g):
    a, b = res
    if mode == 'nn':
        da, db = _dg(g, b, 'nt', hi), _dg(a, g, 'tn', hi)
    elif mode == 'nt':
        da, db = _dg(g, b, 'nn', hi), _dg(g, a, 'tn', hi)
    else:
        da, db = _dg(b, g, 'nt', hi), _dg(a, g, 'nn', hi)
    return da.astype(a.dtype), db.astype(b.dtype)


_dot.defvjp(_dot_fwd, _dot_bwd)


def bdot(a, b, mode):
    return _dot(a, b, mode, False)


def hdot(a, b, mode):
    return _dot(a, b, mode, True)


def _shift_rows_impl(x, xprev, s):
    rows = lax.broadcasted_iota(i32, x.shape, 0)
    return jnp.where(rows >= s, pltpu.roll(x, s, 0), pltpu.roll(xprev, s, 0))


@functools.partial(jax.custom_vjp, nondiff_argnums=(2,))
def _shift_rows(x, xprev, s):
    return _shift_rows_impl(x, xprev, s)


def _shift_rows_fwd(x, xprev, s):
    return _shift_rows_impl(x, xprev, s), None


def _shift_rows_bwd(s, _, g):
    n = g.shape[0]
    rows = lax.broadcasted_iota(i32, g.shape, 0)
    r = pltpu.roll(g, n - s, 0)
    keep = rows < n - s
    return jnp.where(keep, r, 0.0), jnp.where(keep, 0.0, r)


_shift_rows.defvjp(_shift_rows_fwd, _shift_rows_bwd)


def _lane_pick(x, lane):
    idx = lax.broadcasted_iota(i32, x.shape, x.ndim - 1)
    return jnp.sum(jnp.where(idx == lane, x, 0.0), axis=-1, keepdims=True)


def _silu(x):
    return x * jax.nn.sigmoid(x)


def _tri_inv(a):
    C = a.shape[-1]
    eye = (lax.broadcasted_iota(i32, a.shape, 1) == lax.broadcasted_iota(i32, a.shape, 2)).astype(f32)
    p = -a
    x = eye + p
    for _ in range((C - 1).bit_length() - 1):
        p = _dg(p, p, 'nn', True)
        x = x + _dg(x, p, 'nn', True)
    return x


@jax.custom_vjp
def _tri_solve(a, rhs):
    return _dg(_tri_inv(a), rhs, 'nn', True)


def _tri_solve_fwd(a, rhs):
    tinv = _tri_inv(a)
    sol = _dg(tinv, rhs, 'nn', True)
    return sol, (tinv, sol)


def _tri_solve_bwd(res, g):
    tinv, sol = res
    d_rhs = _dg(tinv, g, 'tn', True)
    return -_dg(d_rhs, sol, 'nt', True), d_rhs


_tri_solve.defvjp(_tri_solve_fwd, _tri_solve_bwd)


@functools.partial(jax.custom_vjp, nondiff_argnums=(1,))
def _lane_head(x, n):
    return x[:, :, :n]


def _lane_head_fwd(x, n):
    return x[:, :, :n], None


def _lane_head_bwd(n, _, g):
    s = jnp.sum(g, axis=-1, keepdims=True) * (1.0 / LANES)
    return (jnp.broadcast_to(s, g.shape[:-1] + (LANES,)),)


_lane_head.defvjp(_lane_head_fwd, _lane_head_bwd)


@functools.partial(jax.custom_vjp, nondiff_argnums=(1,))
def _last_row(x, c):
    return x[:, c - 1:, :]


def _last_row_fwd(x, c):
    return x[:, c - 1:, :], None


def _last_row_bwd(c, _, g):
    shape = (g.shape[0], c, g.shape[2])
    rows = lax.broadcasted_iota(i32, shape, 1)
    return (jnp.where(rows == c - 1, jnp.broadcast_to(g, shape), 0.0),)


_last_row.defvjp(_last_row_fwd, _last_row_bwd)


def _full(a):
    nd = a.ndim
    return (a, tuple(a.shape), lambda i, _nd=nd: (0,) * _nd)


def _rows(a, tm, col=0, width=None):
    width = a.shape[1] if width is None else width
    return (a, (tm, width), lambda i, _c=col: (i, _c))


def _tile_call(name, fn, n, ins, outs, accs=()):
    n_in, n_out, n_acc = len(ins), len(outs), len(accs)

    def body(*refs):
        i = pl.program_id(0)
        res = fn(i, *[r[...].astype(f32) for r in refs[:n_in]])
        if not isinstance(res, (tuple, list)):
            res = (res,)
        assert len(res) == n_out + n_acc, (name, len(res), n_out, n_acc)
        for r, v in zip(refs[n_in:n_in + n_out], res[:n_out]):
            r[...] = v.astype(r.dtype)
        if n_acc:
            acc_refs = refs[n_in + n_out:]

            @pl.when(i == 0)
            def _():
                for r in acc_refs:
                    r[...] = jnp.zeros(r.shape, r.dtype)

            for r, v in zip(acc_refs, res[n_out:]):
                r[...] += v.astype(r.dtype)

    out_shape = [jax.ShapeDtypeStruct(s, d) for s, d, _, _ in outs] + [jax.ShapeDtypeStruct(s, d) for s, d in accs]
    out_specs = [pl.BlockSpec(b, m) for _, _, b, m in outs]
    out_specs += [pl.BlockSpec(tuple(s), lambda i, _nd=len(s): (0,) * _nd) for s, _ in accs]
    res = pl.pallas_call(
        body, name=name, grid=(n,),
        in_specs=[pl.BlockSpec(b, m) for _, b, m in ins],
        out_specs=out_specs, out_shape=out_shape,
        compiler_params=pltpu.CompilerParams(dimension_semantics=("arbitrary",), vmem_limit_bytes=VMEM_LIMIT),
    )(*[a for a, _, _ in ins])
    return res


def _pick(n, cands):
    for c in cands:
        if n % c == 0:
            return c
    return n


def _mm(name, a, b, mode, out_dtype=f32, add=None):
    if mode == 'tn':
        K, M = a.shape
    else:
        M, K = a.shape
    N = b.shape[0] if mode == 'nt' else b.shape[1]
    tm = _pick(M, (1024, 1408, 512, 256, 128))
    tn = _pick(N, (512, 1408, 256, 128))
    tk = K if K <= 3328 else _pick(K, (3328, 2816, 2048, 1024, 512, 256, 128))
    nk = K // tk
    ca = 0 if mode == 'tn' else 1
    cb = 1 if mode == 'nt' else 0
    dims = (((ca,), (cb,)), ((), ()))

    def body(*refs):
        a_ref, b_ref = refs[0], refs[1]
        add_ref = refs[2] if add is not None else None
        part = lax.dot_general(a_ref[...].astype(bf16), b_ref[...].astype(bf16), dims, preferred_element_type=f32)

        def finish(r, o_ref):
            if add_ref is not None:
                r = r + add_ref[...].astype(f32)
            o_ref[...] = r.astype(o_ref.dtype)

        if nk == 1:
            finish(part, refs[-1])
            return
        o_ref, acc_ref = refs[-2], refs[-1]
        k = pl.program_id(2)

        @pl.when(k == 0)
        def _():
            acc_ref[...] = part

        @pl.when(k > 0)
        def _():
            acc_ref[...] += part

        @pl.when(k == nk - 1)
        def _():
            finish(acc_ref[...], o_ref)

    a_spec = pl.BlockSpec((tk, tm), lambda i, j, k: (k, i)) if mode == 'tn' else pl.BlockSpec((tm, tk), lambda i, j, k: (i, k))
    b_spec = pl.BlockSpec((tn, tk), lambda i, j, k: (j, k)) if mode == 'nt' else pl.BlockSpec((tk, tn), lambda i, j, k: (k, j))
    in_specs, args = [a_spec, b_spec], [a, b]
    if add is not None:
        in_specs.append(pl.BlockSpec((tm, tn), lambda i, j, k: (i, j)))
        args.append(add)
    return pl.pallas_call(
        body, name=name, grid=(M // tm, N // tn, nk), in_specs=in_specs,
        out_specs=pl.BlockSpec((tm, tn), lambda i, j, k: (i, j)),
        out_shape=jax.ShapeDtypeStruct((M, N), out_dtype),
        scratch_shapes=[pltpu.VMEM((tm, tn), f32)] if nk > 1 else [],
        compiler_params=pltpu.CompilerParams(dimension_semantics=("parallel", "parallel", "arbitrary"), vmem_limit_bytes=VMEM_LIMIT),
    )(*args)


def _swiglu(u):
    return _silu(u[:, :D_FF]) * u[:, D_FF:]


def _res_ln(x, f, g, b, s):
    r = ALPHA * x + s * f
    mu = jnp.mean(r, axis=-1, keepdims=True)
    rc = r - mu
    var = jnp.mean(rc * rc, axis=-1, keepdims=True)
    return rc * lax.rsqrt(var + LN_EPS) * g + b


def _ln(x, g, b):
    mu = jnp.mean(x, axis=-1, keepdims=True)
    xc = x - mu
    var = jnp.mean(xc * xc, axis=-1, keepdims=True)
    return xc * lax.rsqrt(var + LN_EPS) * g + b


def _dn_pre(first, xc, xp, ba, cw, hp):
    xp = jnp.where(first, 0.0, xp)
    y = cw[DN_CONV - 1:DN_CONV, :] * xc
    for j in range(DN_CONV - 1):
        y = y + cw[j:j + 1, :] * _shift_rows(xc, xp, DN_CONV - 1 - j)
    c = _silu(y)
    qs, ks, vs, gs, bs = [], [], [], [], []
    nqk = DN_HEADS * DN_DK
    for h in range(DN_HEADS):
        q = c[:, h * DN_DK:(h + 1) * DN_DK]
        k = c[:, nqk + h * DN_DK:nqk + (h + 1) * DN_DK]
        v = c[:, 2 * nqk + h * DN_DK:2 * nqk + (h + 1) * DN_DK]
        qs.append(q * lax.rsqrt(jnp.sum(q * q, axis=-1, keepdims=True) + RMS_EPS))
        ks.append(k * lax.rsqrt(jnp.sum(k * k, axis=-1, keepdims=True) + RMS_EPS))
        vs.append(v)
        beta = jax.nn.sigmoid(_lane_pick(ba, h))
        a_log = _lane_pick(hp[0:1, :], h)
        dt = _lane_pick(hp[1:2, :], h)
        g = -jnp.exp(a_log) * jax.nn.softplus(_lane_pick(ba, DN_HEADS + h) + dt)
        gs.append(jnp.broadcast_to(g, q.shape))
        bs.append(jnp.broadcast_to(beta, q.shape))
    return tuple(jnp.stack(t, axis=0) for t in (qs, ks, vs, gs, bs))


def _wy(q, k, v, gb, bb):
    B, C, _ = q.shape
    ri = lax.broadcasted_iota(i32, (B, C, C), 1)
    ci = lax.broadcasted_iota(i32, (B, C, C), 2)
    tril, strict = ri >= ci, ri > ci
    gc = hdot(tril.astype(f32), gb, 'nn')
    gl = jnp.broadcast_to(_last_row(gc, C), gc.shape)
    col = _lane_head(gc, C)
    decay = jnp.exp(jnp.where(tril, col - jnp.swapaxes(col, 1, 2), NEG_INF))
    qs = q * (DN_DK ** -0.5)
    kb = k * bb
    a = jnp.where(strict, bdot(kb, k, 'nt') * decay, 0.0)
    sol = _tri_solve(a, jnp.concatenate([v * bb, kb * jnp.exp(gc)], axis=-1))
    qk = jnp.where(tril, bdot(qs, k, 'nt') * decay, 0.0)
    kt = k * jnp.exp(gl - gc)
    qd = qs * jnp.exp(gc)
    return qd, qk, sol[..., :DN_DK], sol[..., DN_DK:], kt, jnp.exp(gl)


def _scan_step(S, qd, qk, u, w, kt, egl):
    vn = u - bdot(w, S, 'nn')
    o = bdot(qd, S, 'nn') + bdot(qk, vn, 'nn')
    e2 = jnp.concatenate([egl] * (DN_DK // DN_CHUNK), axis=1)
    return o, S * e2 + bdot(kt, vn, 'tn')


def _dn_post(o, z, nw):
    outs = []
    for h in range(DN_HEADS):
        oh = o[h]
        oh = oh * lax.rsqrt(jnp.mean(oh * oh, axis=-1, keepdims=True) + RMS_EPS) * nw
        outs.append(oh * _silu(z[:, h * DN_DK:(h + 1) * DN_DK]))
    return jnp.concatenate(outs, axis=1)


def _swa(first, q, kc, kp, vc, vp, snk):
    W = q.shape[1]
    G = SWA_HEADS // SWA_KV_HEADS
    r = lax.broadcasted_iota(i32, (G, W, 2 * W), 1)
    c = lax.broadcasted_iota(i32, (G, W, 2 * W), 2)
    mask = (c > W + r - WINDOW) & (c <= W + r) & jnp.logical_or(c >= W, jnp.logical_not(first))
    sink_all = _lane_pick(snk, 0)
    outs = []
    for j in range(SWA_KV_HEADS):
        qj = q[j * G:(j + 1) * G]
        kk = jnp.broadcast_to(jnp.concatenate([kp[j], kc[j]], axis=0)[None], (G, 2 * W, SWA_DH))
        vv = jnp.broadcast_to(jnp.concatenate([vp[j], vc[j]], axis=0)[None], (G, 2 * W, SWA_DH))
        s = jnp.where(mask, bdot(qj, kk, 'nt') * (SWA_DH ** -0.5), NEG_INF)
        sink = sink_all[j * G:(j + 1) * G][:, :, None]
        m = jnp.maximum(jnp.max(s, axis=-1, keepdims=True), sink)
        p = jnp.exp(s - m)
        p = p / (jnp.sum(p, axis=-1, keepdims=True) + jnp.exp(sink - m))
        outs.append(bdot(p, vv, 'nn'))
    return jnp.concatenate(outs, axis=0)


def _xa(q, kv):
    outs = []
    nk = XA_HEADS * XA_DH
    for h in range(XA_HEADS):
        qh = q[:, h * XA_DH:(h + 1) * XA_DH]
        kh = kv[:, h * XA_DH:(h + 1) * XA_DH]
        vh = kv[:, nk + h * XA_DH:nk + (h + 1) * XA_DH]
        s = bdot(qh, kh, 'nt') * (XA_DH ** -0.5)
        m = jnp.max(s, axis=-1, keepdims=True)
        p = jnp.exp(s - m)
        p = p / jnp.sum(p, axis=-1, keepdims=True)
        outs.append(bdot(p, vh, 'nn'))
    return jnp.concatenate(outs, axis=1)


def _merge(o_dn, o_sw, o_xa, gates, wb):
    acc = None
    for n, o in enumerate((o_dn, o_sw, o_xa)):
        t = jax.nn.sigmoid(gates[:, n * D_MODEL:(n + 1) * D_MODEL]) * bdot(o, wb[n], 'nn')
        acc = t if acc is None else acc + t
    return acc


def _row_tile(T, want):
    return _pick(T, tuple(c for c in (1024, 512, 256, 128, 64) if c <= want))


def _res_ln_fwd(name, x, f, g, b, s):
    T = x.shape[0]
    tm = _row_tile(T, 512)

    def fn(i, x, f, g, b):
        h = _res_ln(x, f, g, b, s)
        return h, h

    return _tile_call(name, fn, T // tm, [_rows(x, tm), _rows(f, tm), _full(g), _full(b)],
                      [((T, D_MODEL), f32, (tm, D_MODEL), lambda i: (i, 0)), ((T, D_MODEL), bf16, (tm, D_MODEL), lambda i: (i, 0))])


def _res_ln_bwd(name, x, f, g, b, s, dh):
    T = x.shape[0]
    tm = _row_tile(T, 512)

    def fn(i, x, f, g, b, dh):
        _, vjp = jax.vjp(lambda x, f, g, b: _res_ln(x, f, g, b, s), x, f, g, b)
        return vjp(dh)

    return _tile_call(name, fn, T // tm, [_rows(x, tm), _rows(f, tm), _full(g), _full(b), _rows(dh, tm)],
                      [((T, D_MODEL), f32, (tm, D_MODEL), lambda i: (i, 0)), ((T, D_MODEL), bf16, (tm, D_MODEL), lambda i: (i, 0))],
                      [((1, D_MODEL), f32), ((1, D_MODEL), f32)])


def _swiglu_fwd(name, u):
    T = u.shape[0]
    tm = _row_tile(T, 256)
    return _tile_call(name, lambda i, u: _swiglu(u.astype(f32)), T // tm, [_rows(u, tm)],
                      [((T, D_FF), bf16, (tm, D_FF), lambda i: (i, 0))])[0]


def _swiglu_bwd(name, u, da):
    T = u.shape[0]
    tm = _row_tile(T, 256)

    def fn(i, u, da):
        _, vjp = jax.vjp(_swiglu, u.astype(f32))
        return vjp(da.astype(f32))[0]

    return _tile_call(name, fn, T // tm, [_rows(u, tm), _rows(da, tm)],
                      [((T, 2 * D_FF), bf16, (tm, 2 * D_FF), lambda i: (i, 0))])[0]


def _hm_spec(tm, w=DN_DK):
    return ((DN_HEADS, tm, w), lambda i: (0, i, 0))


def _prev(i):
    return jnp.maximum(i - 1, 0)


def _dn_pre_fwd(name, p, pba, cw, hp):
    T = p.shape[0]
    tm = _row_tile(T, 256)
    n = T // tm
    W3 = 3 * DN_HEADS * DN_DK
    ins = [(p, (tm, W3), lambda i: (i, 0)), (p, (tm, W3), lambda i: (_prev(i), 0)),
           (pba, (tm, LANES), lambda i: (i, 0)), _full(cw), _full(hp)]
    blk, im = _hm_spec(tm)
    return _tile_call(name, lambda i, *a: _dn_pre(i == 0, *a), n, ins, [((DN_HEADS, T, DN_DK), f32, blk, im)] * 5)


def _dn_pre_bwd(name, p, pba, cw, hp, cts):
    T = p.shape[0]
    tm = _row_tile(T, 256)
    n = T // tm
    W3 = 3 * DN_HEADS * DN_DK
    blk, im = _hm_spec(tm)
    ins = [(p, (tm, W3), lambda i: (i, 0)), (p, (tm, W3), lambda i: (_prev(i), 0)),
           (pba, (tm, LANES), lambda i: (i, 0)), _full(cw), _full(hp)] + [(c, blk, im) for c in cts]

    def fn(i, xc, xp, ba, cw, hp, *cts):
        _, vjp = jax.vjp(lambda *a: _dn_pre(i == 0, *a), xc, xp, ba, cw, hp)
        return vjp(tuple(cts))

    return _tile_call(name, fn, n, ins,
                      [((T, W3), f32, (tm, W3), lambda i: (i, 0)), ((T, W3), f32, (tm, W3), lambda i: (i, 0)),
                       ((T, LANES), f32, (tm, LANES), lambda i: (i, 0))],
                      [(tuple(cw.shape), f32), (tuple(hp.shape), f32)])


def _wy_split(x, nc):
    return x.reshape(DN_HEADS * nc, DN_CHUNK, x.shape[-1])


def _wy_fwd(name, q, k, v, gb, bb):
    T = q.shape[1]
    tm = _row_tile(T, 256)
    nc = tm // DN_CHUNK

    def fn(i, *a):
        outs = _wy(*[_wy_split(t, nc) for t in a])
        return [o.reshape(DN_HEADS, tm, o.shape[-1]) for o in outs]

    blk, im = _hm_spec(tm)
    blk2, im2 = _hm_spec(tm, DN_CHUNK)
    ins = [(t, blk, im) for t in (q, k, v, gb, bb)]
    full = (DN_HEADS, T, DN_DK)
    outs = [(full, bf16, blk, im), ((DN_HEADS, T, DN_CHUNK), bf16, blk2, im2), (full, f32, blk, im),
            (full, bf16, blk, im), (full, bf16, blk, im), (full, f32, blk, im)]
    return _tile_call(name, fn, T // tm, ins, outs)


def _wy_bwd(name, q, k, v, gb, bb, cts):
    T = q.shape[1]
    tm = _row_tile(T, 256)
    nc = tm // DN_CHUNK

    def fn(i, q, k, v, gb, bb, *cts):
        _, vjp = jax.vjp(_wy, *[_wy_split(t, nc) for t in (q, k, v, gb, bb)])
        grads = vjp(tuple(_wy_split(c.astype(f32), nc) for c in cts))
        return [g.reshape(DN_HEADS, tm, DN_DK) for g in grads]

    blk, im = _hm_spec(tm)
    blk2, im2 = _hm_spec(tm, DN_CHUNK)
    ins = [(t, blk, im) for t in (q, k, v, gb, bb)]
    ins += [(c, blk2 if c.shape[-1] == DN_CHUNK else blk, im) for c in cts]
    return _tile_call(name, fn, T // tm, ins, [((DN_HEADS, T, DN_DK), f32, blk, im)] * 5)


def _scan_fwd(name, qd, qk, u, w, kt, egl):
    T = qd.shape[1]
    C = DN_CHUNK
    U = _pick(T // C, (SCAN_CHUNKS, 1))
    n = T // (C * U)

    def body(qd_ref, qk_ref, u_ref, w_ref, kt_ref, egl_ref, o_ref, sall_ref, s_ref):
        i = pl.program_id(0)

        @pl.when(i == 0)
        def _():
            s_ref[...] = jnp.zeros(s_ref.shape, f32)

        S = s_ref[...]
        for j in range(U):
            sl = pl.ds(j * C, C)
            sall_ref[:, j] = S
            o, S = _scan_step(S, *[r[:, sl, :] for r in (qd_ref, qk_ref, u_ref, w_ref, kt_ref, egl_ref)])
            o_ref[:, sl, :] = o
        s_ref[...] = S

    blk, im = _hm_spec(C * U)
    blk2, im2 = _hm_spec(C * U, C)
    return pl.pallas_call(
        body, name=name, grid=(n,),
        in_specs=[pl.BlockSpec(blk, im), pl.BlockSpec(blk2, im2)] + [pl.BlockSpec(blk, im)] * 4,
        out_specs=[pl.BlockSpec(blk, im), pl.BlockSpec((DN_HEADS, U, DN_DK, DN_DK), lambda i: (0, i, 0, 0))],
        out_shape=[jax.ShapeDtypeStruct((DN_HEADS, T, DN_DK), f32), jax.ShapeDtypeStruct((DN_HEADS, n * U, DN_DK, DN_DK), f32)],
        scratch_shapes=[pltpu.VMEM((DN_HEADS, DN_DK, DN_DK), f32)],
        compiler_params=pltpu.CompilerParams(dimension_semantics=("arbitrary",), vmem_limit_bytes=VMEM_LIMIT),
    )(qd, qk, u, w, kt, egl)


def _scan_bwd(name, qd, qk, u, w, kt, egl, s_all, do):
    T = qd.shape[1]
    C = DN_CHUNK
    U = _pick(T // C, (SCAN_CHUNKS, 1))
    n = T // (C * U)

    def body(qd_ref, qk_ref, u_ref, w_ref, kt_ref, egl_ref, sall_ref, do_ref,
             dqd_ref, dqk_ref, du_ref, dw_ref, dkt_ref, degl_ref, ds_ref):
        i = pl.program_id(0)

        @pl.when(i == 0)
        def _():
            ds_ref[...] = jnp.zeros(ds_ref.shape, f32)

        dS = ds_ref[...]
        for j in reversed(range(U)):
            sl = pl.ds(j * C, C)
            args = [r[:, sl, :].astype(f32) for r in (qd_ref, qk_ref, u_ref, w_ref, kt_ref, egl_ref)]
            _, vjp = jax.vjp(_scan_step, sall_ref[:, j], *args)
            dS, *cts = vjp((do_ref[:, sl, :], dS))
            for r, v in zip((dqd_ref, dqk_ref, du_ref, dw_ref, dkt_ref, degl_ref), cts):
                r[:, sl, :] = v
        ds_ref[...] = dS

    blk = (DN_HEADS, C * U, DN_DK)
    blk2 = (DN_HEADS, C * U, C)
    rim = lambda i: (0, n - 1 - i, 0)
    sp, sp2 = pl.BlockSpec(blk, rim), pl.BlockSpec(blk2, rim)
    full, full2 = jax.ShapeDtypeStruct((DN_HEADS, T, DN_DK), f32), jax.ShapeDtypeStruct((DN_HEADS, T, C), f32)
    return pl.pallas_call(
        body, name=name, grid=(n,),
        in_specs=[sp, sp2, sp, sp, sp, sp, pl.BlockSpec((DN_HEADS, U, DN_DK, DN_DK), lambda i: (0, n - 1 - i, 0, 0)), sp],
        out_specs=[sp, sp2, sp, sp, sp, sp],
        out_shape=[full, full2, full, full, full, full],
        scratch_shapes=[pltpu.VMEM((DN_HEADS, DN_DK, DN_DK), f32)],
        compiler_params=pltpu.CompilerParams(dimension_semantics=("arbitrary",), vmem_limit_bytes=VMEM_LIMIT),
    )(qd, qk, u, w, kt, egl, s_all, do)


def _dn_post_fwd(name, o, p, nw):
    T = p.shape[0]
    tm = _row_tile(T, 512)
    blk, im = _hm_spec(tm)
    W = DN_HEADS * DN_DK
    return _tile_call(name, lambda i, o, z, nw: _dn_post(o, z, nw), T // tm,
                      [(o, blk, im), (p, (tm, W), lambda i: (i, 1536 // W)), _full(nw)],
                      [((T, W), f32, (tm, W), lambda i: (i, 0))])[0]


def _dn_post_bwd(name, o, p, nw, dout):
    T = p.shape[0]
    tm = _row_tile(T, 512)
    blk, im = _hm_spec(tm)
    W = DN_HEADS * DN_DK

    def fn(i, o, z, nw, dout):
        _, vjp = jax.vjp(_dn_post, o, z, nw)
        return vjp(dout)

    return _tile_call(name, fn, T // tm,
                      [(o, blk, im), (p, (tm, W), lambda i: (i, 1536 // W)), _full(nw), _rows(dout, tm)],
                      [((DN_HEADS, T, DN_DK), f32, blk, im), ((T, W), f32, (tm, W), lambda i: (i, 0))],
                      [((1, DN_DK), f32)])


def _swa_tile(T):
    return _pick(T, (2 * WINDOW, WINDOW))


def _swa_specs(T):
    W = _swa_tile(T)
    qs =((SWA_HEADS, W, SWA_DH), lambda i: (0, i, 0))
    kc = ((SWA_KV_HEADS, W, SWA_DH), lambda i: (0, i, 0))
    kp = ((SWA_KV_HEADS, W, SWA_DH), lambda i: (0, _prev(i), 0))
    return qs, kc, kp


def _swa_fwd(name, q, k, v, snk):
    T = q.shape[1]
    qs, kc, kp = _swa_specs(T)
    return _tile_call(name, lambda i, *a: _swa(i == 0, *a), T // _swa_tile(T),
                      [(q, *qs), (k, *kc), (k, *kp), (v, *kc), (v, *kp), _full(snk)],
                      [((SWA_HEADS, T, SWA_DH), f32, *qs)])[0]


def _swa_bwd(name, q, k, v, snk, do):
    T = q.shape[1]
    qs, kc, kp = _swa_specs(T)

    def fn(i, q, k_c, k_p, v_c, v_p, snk, do):
        _, vjp = jax.vjp(lambda *a: _swa(i == 0, *a), q, k_c, k_p, v_c, v_p, snk)
        return vjp(do)

    kv_out = ((SWA_KV_HEADS, T, SWA_DH), f32, *kc)
    return _tile_call(name, fn, T // _swa_tile(T),
                      [(q, *qs), (k, *kc), (k, *kp), (v, *kc), (v, *kp), _full(snk), (do, *qs)],
                      [((SWA_HEADS, T, SWA_DH), f32, *qs), kv_out, kv_out, kv_out, kv_out],
                      [(tuple(snk.shape), f32)])


def _swa_comb(name, cur, prv):
    T = cur.shape[1]
    n = T // _swa_tile(T)
    blk = (SWA_KV_HEADS, _swa_tile(T), SWA_DH)

    def fn(i, c, p):
        return c + jnp.where(i == n - 1, 0.0, p)

    return _tile_call(name, fn, n, [(cur, blk, lambda i: (0, i, 0)), (prv, blk, lambda i: (0, jnp.minimum(i + 1, n - 1), 0))],
                      [(tuple(cur.shape), f32, blk, lambda i: (0, i, 0))])[0]


def _xa_fwd(name, p, kv):
    T = p.shape[0]
    tm = _row_tile(T, 256)
    W = XA_HEADS * XA_DH
    return _tile_call(name, lambda i, q, kv: _xa(q, kv), T // tm, [(p, (tm, W), lambda i: (i, 2560 // W)), _full(kv)],
                      [((T, W), f32, (tm, W), lambda i: (i, 0))])[0]


def _xa_bwd(name, p, kv, do):
    T = p.shape[0]
    tm = _row_tile(T, 256)
    W = XA_HEADS * XA_DH

    def fn(i, q, kv, do):
        _, vjp = jax.vjp(_xa, q, kv)
        return vjp(do)

    return _tile_call(name, fn, T // tm, [(p, (tm, W), lambda i: (i, 2560 // W)), _full(kv), _rows(do, tm)],
                      [((T, W), bf16, (tm, W), lambda i: (i, 0))], [(tuple(kv.shape), f32)])


def _merge_fwd(name, o_dn, o_sw, o_xa, p, wb):
    T = p.shape[0]
    tm = _row_tile(T, 256)
    GW = N_BRANCH * D_MODEL
    return _tile_call(name, lambda i, a, b, c, g, w: _merge(a, b, c, g, w.astype(f32)), T // tm,
                      [_rows(o_dn, tm), _rows(o_sw, tm), _rows(o_xa, tm), (p, (tm, GW), lambda i: (i, 1)), _full(wb)],
                      [((T, D_MODEL), bf16, (tm, D_MODEL), lambda i: (i, 0))])[0]


def _merge_bwd(name, o_dn, o_sw, o_xa, p, wb, dm):
    T = p.shape[0]
    tm = _row_tile(T, 256)
    GW = N_BRANCH * D_MODEL

    def fn(i, a, b, c, g, w, dm):
        _, vjp = jax.vjp(_merge, a, b, c, g, w.astype(f32))
        return vjp(dm)

    bo = ((T, BRANCH_W), f32, (tm, BRANCH_W), lambda i: (i, 0))
    return _tile_call(name, fn, T // tm,
                      [_rows(o_dn, tm), _rows(o_sw, tm), _rows(o_xa, tm), (p, (tm, GW), lambda i: (i, 1)), _full(wb), _rows(dm, tm)],
                      [bo, bo, bo, ((T, GW), bf16, (tm, GW), lambda i: (i, 0))], [(tuple(wb.shape), f32)])


def _assemble_dp(name, dxc, dxp, dz, dswq, dxaq, dgates, dswk, dswv, dba):
    T = dz.shape[0]
    tm = _row_tile(T, 256)
    n = T // tm

    def fn(i, dxc, dxp, dz, dswq, dxaq, dgates, dswk, dswv, dba):
        dqkv = dxc + jnp.where(i == n - 1, 0.0, dxp)
        parts = [dqkv, dz, dswq, dxaq, dgates, dswk, dswv, dba, jnp.zeros((tm, LANES), f32)]
        return jnp.concatenate([t.astype(f32) for t in parts], axis=1)

    ins = [_rows(dxc, tm), (dxp, (tm, dxp.shape[1]), lambda i: (jnp.minimum(i + 1, n - 1), 0))]
    ins += [_rows(t, tm) for t in (dz, dswq, dxaq, dgates, dswk, dswv, dba)]
    return _tile_call(name, fn, n, ins, [((T, D_INP), bf16, (tm, D_INP), lambda i: (i, 0))])[0]


def _loss_grad(name, y, tgt):
    T = y.shape[0]
    tm = _row_tile(T, 512)

    def fn(i, y, t):
        e = y - t
        part = 0.5 * jnp.sum(jnp.mean(e * e, axis=-1, keepdims=True), axis=0, keepdims=True)
        return e * (1.0 / D_MODEL), jnp.broadcast_to(part, (1, LANES))

    return _tile_call(name, fn, T // tm, [_rows(y, tm), _rows(tgt, tm)],
                      [((T, D_MODEL), f32, (tm, D_MODEL), lambda i: (i, 0))], [((1, LANES), f32)])


def _mem_ln_fwd(name, mem, g, b):
    M = mem.shape[0]
    return _tile_call(name, lambda i, m, g, b: (_ln(m, g, b),), 1, [_full(mem), _full(g), _full(b)],
                      [((M, D_MODEL), bf16, (M, D_MODEL), lambda i: (0, 0))])[0]


def _mem_ln_bwd(name, mem, g, b, dmn):
    def fn(i, m, g, b, d):
        _, vjp = jax.vjp(lambda g, b: _ln(m, g, b), g, b)
        return vjp(d)

    return _tile_call(name, fn, 1, [_full(mem), _full(g), _full(b), _full(dmn)], [], [((1, D_MODEL), f32), ((1, D_MODEL), f32)])


def _pad_w_in(w):
    z = jnp.zeros((w.shape[0], D_INP - D_IN), w.dtype)
    return jnp.concatenate([w[:, 0:1536], w[:, 1544:2056], w[:, 2056:2568], w[:, 2824:3336], w[:, 3336:6408],
                            w[:, 2568:2696], w[:, 2696:2824], w[:, 1536:1544], z], axis=1)


def _unpad_dw_in(d):
    return jnp.concatenate([d[:, 0:1536], d[:, 6400:6408], d[:, 1536:2048], d[:, 2048:2560], d[:, 6144:6272],
                            d[:, 6272:6400], d[:, 2560:3072], d[:, 3072:6144]], axis=1)


def _heads_major(x, nh):
    return x.reshape(x.shape[0], nh, SWA_DH).transpose(1, 0, 2)


def _heads_minor(x):
    return x.transpose(1, 0, 2).reshape(x.shape[1], -1)


def _lane_row(v, rows=8):
    out = jnp.zeros((rows, LANES), f32)
    return out.at[0, :v.shape[0]].set(v)


def _layer_fwd(l, x, xb, mem_nb, W):
    n = lambda s: f"l{l}_{s}"
    sv = {}
    u1 = _mm(n("ffn1_gu"), xb, W['ffn1_w_gu'], 'nn', out_dtype=bf16)
    a1 = _swiglu_fwd(n("ffn1_act"), u1)
    f1 = _mm(n("ffn1_down"), a1, W['ffn1_w_down'], 'nn')
    h1, h1b = _res_ln_fwd(n("ln0"), x, f1, W['ln_g'][0:1], W['ln_b'][0:1], 0.5)
    p = _mm(n("w_in"), h1b, W['w_inp'], 'nn', out_dtype=bf16)
    pba = _mm(n("w_in_ba"), h1b, W['w_ba'], 'nn')
    q, k, v, gb, bb = _dn_pre_fwd(n("dn_pre"), p, pba, W['conv_w'], W['hp'])
    qd, qk, u, w, kt, egl = _wy_fwd(n("dn_wy"), q, k, v, gb, bb)
    o_raw, s_all = _scan_fwd(n("dn_scan"), qd, qk, u, w, kt, egl)
    o_dn = _dn_post_fwd(n("dn_post"), o_raw, p, W['norm_w'])
    q_sw, k_sw, v_sw = _heads_major(p[:, 2048:2560], SWA_HEADS), _heads_major(p[:, 6144:6272], SWA_KV_HEADS), _heads_major(p[:, 6272:6400], SWA_KV_HEADS)
    o_sw = _heads_minor(_swa_fwd(n("swa"), q_sw, k_sw, v_sw, W['snk']))
    kv = _mm(n("mem_kv"), mem_nb, W['w_mem_kv'], 'nn')
    o_xa = _xa_fwd(n("xa"), p, kv)
    merged = _merge_fwd(n("merge"), o_dn, o_sw, o_xa, p, W['w_branch'])
    mo = _mm(n("w_out"), merged, W['w_out'], 'nn')
    h2, h2b = _res_ln_fwd(n("ln1"), h1, mo, W['ln_g'][1:2], W['ln_b'][1:2], 1.0)
    u2 = _mm(n("ffn2_gu"), h2b, W['ffn2_w_gu'], 'nn', out_dtype=bf16)
    a2 = _swiglu_fwd(n("ffn2_act"), u2)
    f2 = _mm(n("ffn2_down"), a2, W['ffn2_w_down'], 'nn')
    h3, h3b = _res_ln_fwd(n("ln2"), h2, f2, W['ln_g'][2:3], W['ln_b'][2:3], 0.5)
    sv = dict(x=x, xb=xb, u1=u1, a1=a1, f1=f1, h1=h1, h1b=h1b, p=p, pba=pba, dn=(q, k, v, gb, bb), wy=(qd, qk, u, w, kt, egl),
              o_raw=o_raw, s_all=s_all, o_dn=o_dn, sw=(q_sw, k_sw, v_sw), o_sw=o_sw, kv=kv, o_xa=o_xa, merged=merged,
              mo=mo, h2=h2, h2b=h2b, u2=u2, a2=a2, f2=f2)
    return h3, h3b, sv


def _ffn_bwd(n, tag, x, xb, u, a, f, g, b, w_gu_t, w_down_t, dh):
    dx_a, df, dg, db = _res_ln_bwd(n(f"{tag}_ln_bwd"), x, f, g, b, 0.5, dh)
    d_down = _mm(n(f"{tag}_down_dw"), a, df, 'tn')
    da = _mm(n(f"{tag}_down_dx"), df, w_down_t, 'nn', out_dtype=bf16)
    du = _swiglu_bwd(n(f"{tag}_act_bwd"), u, da)
    d_gu = _mm(n(f"{tag}_gu_dw"), xb, du, 'tn')
    dx = _mm(n(f"{tag}_gu_dx"), du, w_gu_t, 'nn', add=dx_a)
    return dx, d_gu, d_down, dg, db


def _layer_bwd(l, sv, mem_nb, W, dh3, dmem_acc):
    n = lambda s: f"l{l}_{s}"
    G = {}
    dh2, G['ffn2_w_gu'], G['ffn2_w_down'], dg2, db2 = _ffn_bwd(
        n, "ffn2", sv['h2'], sv['h2b'], sv['u2'], sv['a2'], sv['f2'], W['ln_g'][2:3], W['ln_b'][2:3], W['ffn2_w_gu_t'], W['ffn2_w_down_t'], dh3)
    dh1_a, dmo, dg1, db1 = _res_ln_bwd(n("ln1_bwd"), sv['h1'], sv['mo'], W['ln_g'][1:2], W['ln_b'][1:2], 1.0, dh2)
    G['w_out'] = _mm(n("w_out_dw"), sv['merged'], dmo, 'tn')
    dmerged = _mm(n("w_out_dx"), dmo, W['w_out_t'], 'nn')
    p = sv['p']
    do_dn, do_sw, do_xa, dgates, G['w_branch'] = _merge_bwd(n("merge_bwd"), sv['o_dn'], sv['o_sw'], sv['o_xa'], p, W['w_branch'], dmerged)
    dxaq, dkv = _xa_bwd(n("xa_bwd"), p, sv['kv'], do_xa)
    dkv = dkv.astype(bf16)
    G['w_mem_kv'] = _mm(n("mem_kv_dw"), mem_nb, dkv, 'tn')
    dmem_n = _mm(n("mem_kv_dx"), dkv, W['w_mem_kv_t'], 'nn', add=dmem_acc)
    q_sw, k_sw, v_sw = sv['sw']
    dq_sw, dkc, dkp, dvc, dvp, dsnk = _swa_bwd(n("swa_bwd"), q_sw, k_sw, v_sw, W['snk'], _heads_major(do_sw, SWA_HEADS))
    dswq = _heads_minor(dq_sw)
    dswk = _heads_minor(_swa_comb(n("swa_dk"), dkc, dkp))
    dswv = _heads_minor(_swa_comb(n("swa_dv"), dvc, dvp))
    do_raw, dz, dnw = _dn_post_bwd(n("dn_post_bwd"), sv['o_raw'], p, W['norm_w'], do_dn)
    cts = _scan_bwd(n("dn_scan_bwd"), *sv['wy'], sv['s_all'], do_raw)
    dcts = _wy_bwd(n("dn_wy_bwd"), *sv['dn'], cts)
    dxc, dxp, dba, dcw, dhp = _dn_pre_bwd(n("dn_pre_bwd"), p, sv['pba'], W['conv_w'], W['hp'], dcts)
    dp = _assemble_dp(n("dp"), dxc, dxp, dz, dswq, dxaq, dgates, dswk, dswv, dba)
    G['w_in'] = _unpad_dw_in(_mm(n("w_in_dw"), sv['h1b'], dp, 'tn'))
    dh1 = _mm(n("w_in_dx"), dp, W['w_inp_t'], 'nn', add=dh1_a)
    dx, G['ffn1_w_gu'], G['ffn1_w_down'], dg0, db0 = _ffn_bwd(
        n, "ffn1", sv['x'], sv['xb'], sv['u1'], sv['a1'], sv['f1'], W['ln_g'][0:1], W['ln_b'][0:1], W['ffn1_w_gu_t'], W['ffn1_w_down_t'], dh1)
    G['ln_g'] = jnp.concatenate([dg0, dg1, dg2], axis=0)
    G['ln_b'] = jnp.concatenate([db0, db1, db2], axis=0)
    G['dn_conv_w'] = dcw
    G['dn_a_log'] = dhp[0, :DN_HEADS]
    G['dn_dt_bias'] = dhp[1, :DN_HEADS]
    G['dn_norm_w'] = dnw[0]
    G['swa_sinks'] = dsnk[:, 0]
    return dx, dmem_n, G


def _local_step(x, mem, tgt, Wf):
    mem_g, mem_b = Wf['mem_ln_g'][None, :], Wf['mem_ln_b'][None, :]
    mem_nb = _mem_ln_fwd("mem_ln", mem, mem_g, mem_b)
    layers = []
    for l in range(DEPTH):
        layers.append(dict(
            ln_g=Wf['ln_g'][l], ln_b=Wf['ln_b'][l], ffn1_w_gu=Wf['ffn1_w_gu'][l], ffn1_w_down=Wf['ffn1_w_down'][l],
            w_inp=_pad_w_in(Wf['w_in'][l]), conv_w=Wf['dn_conv_w'][l],
            hp=jnp.zeros((8, LANES), f32).at[0, :DN_HEADS].set(Wf['dn_a_log'][l]).at[1, :DN_HEADS].set(Wf['dn_dt_bias'][l]),
            norm_w=Wf['dn_norm_w'][l][None, :], snk=jnp.broadcast_to(Wf['swa_sinks'][l][:, None], (SWA_HEADS, LANES)),
            w_mem_kv=Wf['w_mem_kv'][l], w_branch=Wf['w_branch'][l], w_out=Wf['w_out'][l],
            ffn2_w_gu=Wf['ffn2_w_gu'][l], ffn2_w_down=Wf['ffn2_w_down'][l]))
        layers[l]['w_ba'] = layers[l]['w_inp'][:, 6400:6400 + LANES]
        for k in ('ffn1_w_gu', 'ffn1_w_down', 'w_inp', 'w_mem_kv', 'w_out', 'ffn2_w_gu', 'ffn2_w_down'):
            layers[l][k + '_t'] = layers[l][k].T
    h, hb = x, x.astype(bf16)
    saved = []
    for l in range(DEPTH):
        h, hb, sv = _layer_fwd(l, h, hb, mem_nb, layers[l])
        saved.append(sv)
    dh, loss_row = _loss_grad("loss", h, tgt)
    grads = [None] * DEPTH
    dmem_n = None
    for l in reversed(range(DEPTH)):
        dh, dmem_n, grads[l] = _layer_bwd(l, saved[l], mem_nb, layers[l], dh, dmem_n)
    dmg, dmb = _mem_ln_bwd("mem_ln_bwd", mem, mem_g, mem_b, dmem_n)
    G = {k: jnp.stack([grads[l][k] for l in range(DEPTH)], axis=0) for k in grads[0]}
    G['mem_ln_g'], G['mem_ln_b'] = dmg[0], dmb[0]
    return loss_row, dh, G


SEG_ALIGN = 2048


def _round_up(n, m):
    return (n + m - 1) // m * m


def _layout(names, sizes, mult):
    off, table = 0, {}
    for nm in names:
        table[nm] = (off, sizes[nm])
        off += _round_up(sizes[nm], SEG_ALIGN)
    return table, _round_up(off // LANES, mult)


def _pack(table, rows, flat):
    names = list(table)
    lead = flat[names[0]].shape[:-1]
    parts, pos = [], 0
    for nm in names:
        off, size = table[nm]
        if off > pos:
            parts.append(jnp.zeros(lead + (off - pos,), flat[nm].dtype))
        parts.append(flat[nm])
        pos = off + size
    total = rows * LANES
    if total > pos:
        parts.append(jnp.zeros(lead + (total - pos,), parts[-1].dtype))
    return jnp.concatenate(parts, axis=-1).reshape(lead + (rows, LANES))


def _unpack(table, packed, nm):
    off, size = table[nm]
    flat = packed.reshape(packed.shape[:-2] + (-1,))
    return flat[..., off:off + size]


def _split_shards(full, ax):
    shp = full.shape
    t = full.reshape(shp[:ax] + (N_SHARD, shp[ax] // N_SHARD) + shp[ax + 1:])
    return jnp.moveaxis(t, ax, 0)


def _join_shards(sh4, ax):
    return jnp.concatenate([sh4[s] for s in range(N_SHARD)], axis=ax)


ANY = pl.BlockSpec(memory_space=pl.ANY)


def _me():
    return lax.axis_index("x"), lax.axis_index("y"), lax.axis_index("c")


def _comm_call(name, body, arrays, out_shapes, n_sem):
    n = len(arrays)
    scratch = [pltpu.SemaphoreType.DMA((n, n_sem)), pltpu.SemaphoreType.DMA((n, n_sem))]
    return pl.pallas_call(
        body, name=name, out_shape=out_shapes, in_specs=[ANY] * n, out_specs=[ANY] * n, scratch_shapes=scratch,
        compiler_params=pltpu.CompilerParams(has_side_effects=True),
    )(*arrays)


def _all_gather(name, xs):
    n = len(xs)

    def body(*refs):
        x_refs, out_refs = refs[:n], refs[n:2 * n]
        send_sems, recv_sems = refs[2 * n:]
        mx, my, mc = _me()
        chips = [(1 - mx, my), (mx, 1 - my), (1 - mx, 1 - my)]

        def copy(a, k, shard, half, to, src=None):
            dst = out_refs[a].at[shard, half]
            return pltpu.make_async_remote_copy(src_ref=dst if src is None else src, dst_ref=dst, send_sem=send_sems.at[a, k],
                                                recv_sem=recv_sems.at[a, k], device_id=to, device_id_type=MESH)

        first = [copy(a, j, 2 * mx + my, mc, (cx, cy, mc), src=x_refs[a].at[mc]) for j, (cx, cy) in enumerate(chips) for a in range(n)]
        for cp in first:
            cp.start()
        passed = []
        for j, (cx, cy) in enumerate(chips):
            for a in range(n):
                copy(a, j, 2 * cx + cy, mc, (mx, my, mc)).wait_recv()
                passed.append(copy(a, 3 + j, 2 * cx + cy, mc, (mx, my, 1 - mc)))
                passed[-1].start()
        for j, (cx, cy) in enumerate(chips):
            for a in range(n):
                copy(a, 3 + j, 2 * cx + cy, 1 - mc, (mx, my, mc)).wait_recv()
        for cp in first + passed:
            cp.wait_send()

    return _comm_call(name, body, xs, [jax.ShapeDtypeStruct((N_SHARD,) + x.shape, x.dtype) for x in xs], 6)


def _pair_exchange(name, items):
    n = len(items)

    def body(*refs):
        src_refs, dst_refs = refs[:n], refs[n:2 * n]
        send_sems, recv_sems = refs[2 * n:]
        mx, my, mc = _me()
        cps = [pltpu.make_async_remote_copy(src_ref=src_refs[a].at[s, 1 - mc], dst_ref=dst_refs[a].at[s], send_sem=send_sems.at[a, s],
                                            recv_sem=recv_sems.at[a, s], device_id=(mx, my, 1 - mc), device_id_type=MESH)
               for a in range(n) for s in range(N_SHARD)]
        for cp in cps:
            cp.start()
        for cp in cps:
            cp.wait()

    return _comm_call(name, body, items, [jax.ShapeDtypeStruct((N_SHARD,) + t.shape[2:], t.dtype) for t in items], N_SHARD)


def _chip_exchange(name, parts):
    n = len(parts)

    def body(*refs):
        p_refs, dst_refs = refs[:n], refs[n:2 * n]
        send_sems, recv_sems = refs[2 * n:]
        mx, my, mc = _me()
        chips = [(1 - mx, my), (mx, 1 - my), (1 - mx, 1 - my)]
        cps = [pltpu.make_async_remote_copy(src_ref=p_refs[a].at[2 * cx + cy], dst_ref=dst_refs[a].at[j], send_sem=send_sems.at[a, j],
                                            recv_sem=recv_sems.at[a, j], device_id=(cx, cy, mc), device_id_type=MESH)
               for a in range(n) for j, (cx, cy) in enumerate(chips)]
        for cp in cps:
            cp.start()
        for cp in cps:
            cp.wait()

    return _comm_call(name, body, parts, [jax.ShapeDtypeStruct((3,) + t.shape[1:], t.dtype) for t in parts], 3)


def _pair_swap(name, reds):
    n = len(reds)

    def body(*refs):
        r_refs, out_refs = refs[:n], refs[n:2 * n]
        send_sems, recv_sems = refs[2 * n:]
        mx, my, mc = _me()
        cps = [pltpu.make_async_remote_copy(src_ref=r_refs[a], dst_ref=out_refs[a], send_sem=send_sems.at[a, 0],
                                            recv_sem=recv_sems.at[a, 0], device_id=(mx, my, 1 - mc), device_id_type=MESH)
               for a in range(n)]
        for cp in cps:
            cp.start()
        for cp in cps:
            cp.wait()

    return _comm_call(name, body, reds, [jax.ShapeDtypeStruct(t.shape, t.dtype) for t in reds], 1)


EW_BLOCK_BYTES = 1 << 20


def _ew_call(name, fn, ins, n_out, out_dtype=f32):
    shape = ins[0].shape
    last = shape[-1]
    flat = [a.reshape(-1, last) for a in ins]
    R = flat[0].shape[0]
    cands = tuple(c for c in (4096, 2048, 1024, 512, 256, 128, 64, 32, 16, 8) if c * last * 4 <= EW_BLOCK_BYTES)
    tr = _pick(R, cands)
    res = _tile_call(name, lambda i, *a: fn(*a), R // tr, [(a, (tr, last), lambda i: (i, 0)) for a in flat],
                     [((R, last), out_dtype, (tr, last), lambda i: (i, 0))] * n_out)
    return [r.reshape(shape) for r in res]


def _adamw(g, w, m, v):
    m = B1 * m + (1.0 - B1) * g
    v = B2 * v + (1.0 - B2) * jnp.square(g)
    m_hat = m / (1.0 - B1 ** STEP)
    v_hat = v / (1.0 - B2 ** STEP)
    return -LR * (m_hat / (jnp.sqrt(v_hat) + EPS) + WD * w), m, v


def _adamw_call(name, mine, theirs, w, m, v, mc1):
    shape, last = w.shape, w.shape[-1]
    g2 = [t.reshape(-1, last) for t in (mine, theirs)]
    w3 = [t.reshape(2, -1, last) for t in (w, m, v)]
    R = g2[0].shape[0]
    tr = _pick(R, tuple(c for c in (4096, 2048, 1024, 512, 256, 128, 64, 32, 16, 8) if c * last * 4 <= EW_BLOCK_BYTES))

    def body(mc_ref, mine_ref, theirs_ref, w_ref, m_ref, v_ref, g_out, d_out, m_out, v_out):
        g = jnp.where(pl.program_id(0) == mc_ref[0], mine_ref[...], theirs_ref[...])
        d, nm, nv = _adamw(g, w_ref[0], m_ref[0], v_ref[0])
        g_out[0], d_out[0], m_out[0], v_out[0] = g, d, nm, nv

    half = pl.BlockSpec((tr, last), lambda h, i: (i, 0))
    full = pl.BlockSpec((1, tr, last), lambda h, i: (h, i, 0))
    res = pl.pallas_call(
        body, name=name, grid=(2, R // tr),
        in_specs=[pl.BlockSpec(memory_space=pltpu.SMEM), half, half, full, full, full], out_specs=[full] * 4,
        out_shape=[jax.ShapeDtypeStruct((2, R, last), f32)] * 4,
        compiler_params=pltpu.CompilerParams(dimension_semantics=("arbitrary", "arbitrary"), vmem_limit_bytes=VMEM_LIMIT),
    )(mc1, *g2, *w3)
    return tuple(r.reshape(shape) for r in res)


def kernel(x, mem, mem_ln_g, mem_ln_b, ln_g, ln_b, ffn1_w_gu, ffn1_w_down, w_in, dn_conv_w, dn_a_log, dn_dt_bias, dn_norm_w, swa_sinks, w_mem_kv, w_branch, w_out, ffn2_w_gu, ffn2_w_down, loss_target, m_mem_ln_g, m_mem_ln_b, m_ln_g, m_ln_b, m_ffn1_w_gu, m_ffn1_w_down, m_w_in, m_dn_conv_w, m_dn_a_log, m_dn_dt_bias, m_dn_norm_w, m_swa_sinks, m_w_mem_kv, m_w_branch, m_w_out, m_ffn2_w_gu, m_ffn2_w_down, v_mem_ln_g, v_mem_ln_b, v_ln_g, v_ln_b, v_ffn1_w_gu, v_ffn1_w_down, v_w_in, v_dn_conv_w, v_dn_a_log, v_dn_dt_bias, v_dn_norm_w, v_swa_sinks, v_w_mem_kv, v_w_branch, v_w_out, v_ffn2_w_gu, v_ffn2_w_down):
    args = dict(locals())
    Ws = {n: args[n] for n in WEIGHTS}
    Ms = {n: args["m_" + n] for n in WEIGHTS}
    Vs = {n: args["v_" + n] for n in WEIGHTS}
    mc = lax.axis_index("c")
    my_s = 2 * lax.axis_index("x") + lax.axis_index("y")
    small = [n for n in WEIGHTS if n not in MATRICES]

    ag_table, ag_rows = _layout(SMALL_SHARDED, {n: Ws[n].size for n in SMALL_SHARDED}, 16)
    ag_small = _pack(ag_table, ag_rows, {n: Ws[n].reshape(-1) for n in SMALL_SHARDED}).reshape(2, ag_rows // 2, LANES)
    local = [Ws[n].astype(bf16) for n in MATRICES] + [ag_small]
    gathered = _all_gather("all_gather_w", local)
    gathered = [jnp.stack([jnp.where(my_s == s, loc, g[s]) for s in range(N_SHARD)], axis=0) for loc, g in zip(local, gathered)]
    Wf = {n: _join_shards(g, SHARD_AXIS[n]) for n, g in zip(MATRICES, gathered)}
    g_small = gathered[-1].reshape(N_SHARD, ag_rows, LANES)
    for n in SMALL_SHARDED:
        Wf[n] = _join_shards(_unpack(ag_table, g_small, n).reshape((N_SHARD,) + Ws[n].shape), SHARD_AXIS[n])
    for n in WEIGHTS:
        if SHARD_AXIS[n] is None:
            Wf[n] = Ws[n]

    loss_row, dx, G = _local_step(x[0], mem[0], loss_target[0], Wf)

    table, rows = _layout(small + ['loss'], {**{n: Ws[n].size for n in small}, 'loss': 1}, 16)
    gflat = {n: (jnp.broadcast_to(G[n].reshape(1, -1), (N_SHARD, G[n].size)) if SHARD_AXIS[n] is None
                 else _split_shards(G[n], SHARD_AXIS[n]).reshape(N_SHARD, -1)) for n in small}
    gflat['loss'] = jnp.broadcast_to(loss_row[:, :1], (N_SHARD, 1))
    items = [_split_shards(G[n], SHARD_AXIS[n]) for n in MATRICES] + [_pack(table, rows, gflat).reshape(N_SHARD, 2, rows // 2, LANES)]
    tags = MATRICES + ['small']
    got = _pair_exchange("rs_pair", items)
    keep = [lax.dynamic_index_in_dim(a, mc, axis=1, keepdims=False) for a in items]
    wire = [bf16] * len(MATRICES) + [f32]
    part = [_ew_call(f"rs_add_pair_{t}", lambda a, b: a + b, [k, g], 1, out_dtype=dt)[0] for t, k, g, dt in zip(tags, keep, got, wire)]
    others = _chip_exchange("rs_chips", part)
    own = lambda a: lax.dynamic_index_in_dim(a, my_s, axis=0, keepdims=False)
    red = [_ew_call(f"rs_add_chips_{t}", lambda k, g, fx, fy, fxy: ((k + g) + fy) + (fx + fxy), [own(k), own(g), o[0], o[1], o[2]], 1)[0]
           for t, k, g, o in zip(tags, keep, got, others)]
    theirs = _pair_swap("rs_swap", red)

    mc1 = mc.astype(i32).reshape(1)
    outs = {}
    for n, a, b in zip(MATRICES, red, theirs):
        outs[n] = _adamw_call(f"adamw_{n}", a, b, Ws[n], Ms[n], Vs[n], mc1)
    fill = {'loss': jnp.zeros((1,), f32)}
    packs = [_pack(table, rows, {**{n: src[n].reshape(-1) for n in small}, **fill}).reshape(2, rows // 2, LANES) for src in (Ws, Ms, Vs)]
    small_out = [p.reshape(rows, LANES) for p in _adamw_call("adamw_small", red[-1], theirs[-1], *packs, mc1)]
    for n in small:
        outs[n] = tuple(_unpack(table, p, n).reshape(Ws[n].shape) for p in small_out)
    loss = _unpack(table, small_out[0], 'loss').reshape(())
    return (loss, dx[None], *[outs[n][k] for k in range(4) for n in WEIGHTS])
```

```python
import functools

import jax
import jax.numpy as jnp
from jax import lax
from jax.experimental import pallas as pl
from jax.experimental.pallas import tpu as pltpu

f32, bf16, i32 = jnp.float32, jnp.bfloat16, jnp.int32
HI = lax.Precision.HIGHEST
MESH = pl.DeviceIdType.MESH

D_MODEL = 1024
DEPTH = 2
DN_HEADS, DN_DK, DN_CONV, DN_CHUNK = 4, 128, 4, 64
SWA_HEADS, SWA_KV_HEADS, SWA_DH, WINDOW = 8, 2, 64, 128
XA_HEADS, XA_DH = 4, 128
D_FF = 2816
N_BRANCH, BRANCH_W = 3, 512
ALPHA = (2 * DEPTH) ** 0.25
LN_EPS, RMS_EPS, NEG_INF = 1e-5, 1e-6, -1e30
D_IN = 6408
D_INP = 6656
LR, B1, B2, EPS, WD, STEP = 0.001, 0.9, 0.999, 1e-08, 0.01, 10

LANES = 128
VMEM_LIMIT = 56 << 20
N_SHARD = 4
SCAN_CHUNKS = 4

WEIGHTS = ['mem_ln_g', 'mem_ln_b', 'ln_g', 'ln_b', 'ffn1_w_gu', 'ffn1_w_down', 'w_in', 'dn_conv_w', 'dn_a_log',
           'dn_dt_bias', 'dn_norm_w', 'swa_sinks', 'w_mem_kv', 'w_branch', 'w_out', 'ffn2_w_gu', 'ffn2_w_down']
SHARD_AXIS = {'mem_ln_g': None, 'mem_ln_b': None, 'ln_g': 2, 'ln_b': 2, 'ffn1_w_gu': 2, 'ffn1_w_down': 1, 'w_in': 2,
              'dn_conv_w': 2, 'dn_a_log': None, 'dn_dt_bias': None, 'dn_norm_w': None, 'swa_sinks': None,
              'w_mem_kv': 1, 'w_branch': 3, 'w_out': 1, 'ffn2_w_gu': 2, 'ffn2_w_down': 1}
MATRICES = ['ffn1_w_gu', 'ffn1_w_down', 'w_in', 'w_mem_kv', 'w_branch', 'w_out', 'ffn2_w_gu', 'ffn2_w_down']
SMALL_SHARDED = ['ln_g', 'ln_b', 'dn_conv_w']


def _dg(a, b, mode, hi):
    nb = a.ndim - 2
    bd = tuple(range(nb))
    ca = nb if mode == 'tn' else nb + 1
    cb = nb + 1 if mode == 'nt' else nb
    dims = (((ca,), (cb,)), (bd, bd))
    dot = lambda x, y: lax.dot_general(x, y, dims, preferred_element_type=f32)
    a_hi, b_hi = a.astype(bf16), b.astype(bf16)
    if not hi:
        return dot(a_hi, b_hi)
    a_lo = (a.astype(f32) - a_hi.astype(f32)).astype(bf16)
    b_lo = (b.astype(f32) - b_hi.astype(f32)).astype(bf16)
    return dot(a_hi, b_hi) + (dot(a_hi, b_lo) + dot(a_lo, b_hi))


@functools.partial(jax.custom_vjp, nondiff_argnums=(2, 3))
def _dot(a, b, mode, hi):
    return _dg(a, b, mode, hi)


def _dot_fwd(a, b, mode, hi):
    return _dg(a, b, mode, hi), (a, b)


def _dot_bwd(mode, hi, res, g):
    a, b = res
    if mode == 'nn':
        da, db = _dg(g, b, 'nt', hi), _dg(a, g, 'tn', hi)
    elif mode == 'nt':
        da, db = _dg(g, b, 'nn', hi), _dg(g, a, 'tn', hi)
    else:
        da, db = _dg(b, g, 'nt', hi), _dg(a, g, 'nn', hi)
    return da.astype(a.dtype), db.astype(b.dtype)


_dot.defvjp(_dot_fwd, _dot_bwd)


def bdot(a, b, mode):
    return _dot(a, b, mode, False)


def hdot(a, b, mode):
    return _dot(a, b, mode, True)


def _shift_rows_impl(x, xprev, s):
    rows = lax.broadcasted_iota(i32, x.shape, 0)
    return jnp.where(rows >= s, pltpu.roll(x, s, 0), pltpu.roll(xprev, s, 0))


@functools.partial(jax.custom_vjp, nondiff_argnums=(2,))
def _shift_rows(x, xprev, s):
    return _shift_rows_impl(x, xprev, s)


def _shift_rows_fwd(x, xprev, s):
    return _shift_rows_impl(x, xprev, s), None


def _shift_rows_bwd(s, _, g):
    n = g.shape[0]
    rows = lax.broadcasted_iota(i32, g.shape, 0)
    r = pltpu.roll(g, n - s, 0)
    keep = rows < n - s
    return jnp.where(keep, r, 0.0), jnp.where(keep, 0.0, r)


_shift_rows.defvjp(_shift_rows_fwd, _shift_rows_bwd)


def _lane_pick(x, lane):
    idx = lax.broadcasted_iota(i32, x.shape, x.ndim - 1)
    return jnp.sum(jnp.where(idx == lane, x, 0.0), axis=-1, keepdims=True)


def _silu(x):
    return x * jax.nn.sigmoid(x)


def _tri_inv(a):
    C = a.shape[-1]
    eye = (lax.broadcasted_iota(i32, a.shape, 1) == lax.broadcasted_iota(i32, a.shape, 2)).astype(f32)
    p = -a
    x = eye + p
    for _ in range((C - 1).bit_length() - 1):
        p = _dg(p, p, 'nn', True)
        x = x + _dg(x, p, 'nn', True)
    return x


@jax.custom_vjp
def _tri_solve(a, rhs):
    return _dg(_tri_inv(a), rhs, 'nn', True)


def _tri_solve_fwd(a, rhs):
    tinv = _tri_inv(a)
    sol = _dg(tinv, rhs, 'nn', True)
    return sol, (tinv, sol)


def _tri_solve_bwd(res, g):
    tinv, sol = res
    d_rhs = _dg(tinv, g, 'tn', True)
    return -_dg(d_rhs, sol, 'nt', True), d_rhs


_tri_solve.defvjp(_tri_solve_fwd, _tri_solve_bwd)


@functools.partial(jax.custom_vjp, nondiff_argnums=(1,))
def _lane_head(x, n):
    return x[:, :, :n]


def _lane_head_fwd(x, n):
    return x[:, :, :n], None


def _lane_head_bwd(n, _, g):
    s = jnp.sum(g, axis=-1, keepdims=True) * (1.0 / LANES)
    return (jnp.broadcast_to(s, g.shape[:-1] + (LANES,)),)


_lane_head.defvjp(_lane_head_fwd, _lane_head_bwd)


@functools.partial(jax.custom_vjp, nondiff_argnums=(1,))
def _last_row(x, c):
    return x[:, c - 1:, :]


def _last_row_fwd(x, c):
    return x[:, c - 1:, :], None


def _last_row_bwd(c, _, g):
    shape = (g.shape[0], c, g.shape[2])
    rows = lax.broadcasted_iota(i32, shape, 1)
    return (jnp.where(rows == c - 1, jnp.broadcast_to(g, shape), 0.0),)


_last_row.defvjp(_last_row_fwd, _last_row_bwd)


def _full(a):
    nd = a.ndim
    return (a, tuple(a.shape), lambda i, _nd=nd: (0,) * _nd)


def _rows(a, tm, col=0, width=None):
    width = a.shape[1] if width is None else width
    return (a, (tm, width), lambda i, _c=col: (i, _c))


def _tile_call(name, fn, n, ins, outs, accs=()):
    n_in, n_out, n_acc = len(ins), len(outs), len(accs)

    def body(*refs):
        i = pl.program_id(0)
        res = fn(i, *[r[...].astype(f32) for r in refs[:n_in]])
        if not isinstance(res, (tuple, list)):
            res = (res,)
        assert len(res) == n_out + n_acc, (name, len(res), n_out, n_acc)
        for r, v in zip(refs[n_in:n_in + n_out], res[:n_out]):
            r[...] = v.astype(r.dtype)
        if n_acc:
            acc_refs = refs[n_in + n_out:]

            @pl.when(i == 0)
            def _():
                for r in acc_refs:
                    r[...] = jnp.zeros(r.shape, r.dtype)

            for r, v in zip(acc_refs, res[n_out:]):
                r[...] += v.astype(r.dtype)

    out_shape = [jax.ShapeDtypeStruct(s, d) for s, d, _, _ in outs] + [jax.ShapeDtypeStruct(s, d) for s, d in accs]
    out_specs = [pl.BlockSpec(b, m) for _, _, b, m in outs]
    out_specs += [pl.BlockSpec(tuple(s), lambda i, _nd=len(s): (0,) * _nd) for s, _ in accs]
    res = pl.pallas_call(
        body, name=name, grid=(n,),
        in_specs=[pl.BlockSpec(b, m) for _, b, m in ins],
        out_specs=out_specs, out_shape=out_shape,
        compiler_params=pltpu.CompilerParams(dimension_semantics=("arbitrary",), vmem_limit_bytes=VMEM_LIMIT),
    )(*[a for a, _, _ in ins])
    return res


def _pick(n, cands):
    for c in cands:
        if n % c == 0:
            return c
    return n


def _mm(name, a, b, mode, out_dtype=f32, add=None):
    if mode == 'tn':
        K, M = a.shape
    else:
        M, K = a.shape
    N = b.shape[0] if mode == 'nt' else b.shape[1]
    tm = _pick(M, (1024, 1408, 512, 256, 128))
    tn = _pick(N, (512, 1408, 256, 128))
    tk = K if K <= 3328 else _pick(K, (3328, 2816, 2048, 1024, 512, 256, 128))
    nk = K // tk
    ca = 0 if mode == 'tn' else 1
    cb = 1 if mode == 'nt' else 0
    dims = (((ca,), (cb,)), ((), ()))

    def body(*refs):
        a_ref, b_ref = refs[0], refs[1]
        add_ref = refs[2] if add is not None else None
        part = lax.dot_general(a_ref[...].astype(bf16), b_ref[...].astype(bf16), dims, preferred_element_type=f32)

        def finish(r, o_ref):
            if add_ref is not None:
                r = r + add_ref[...].astype(f32)
            o_ref[...] = r.astype(o_ref.dtype)

        if nk == 1:
            finish(part, refs[-1])
            return
        o_ref, acc_ref = refs[-2], refs[-1]
        k = pl.program_id(2)

        @pl.when(k == 0)
        def _():
            acc_ref[...] = part

        @pl.when(k > 0)
        def _():
            acc_ref[...] += part

        @pl.when(k == nk - 1)
        def _():
            finish(acc_ref[...], o_ref)

    a_spec = pl.BlockSpec((tk, tm), lambda i, j, k: (k, i)) if mode == 'tn' else pl.BlockSpec((tm, tk), lambda i, j, k: (i, k))
    b_spec = pl.BlockSpec((tn, tk), lambda i, j, k: (j, k)) if mode == 'nt' else pl.BlockSpec((tk, tn), lambda i, j, k: (k, j))
    in_specs, args = [a_spec, b_spec], [a, b]
    if add is not None:
        in_specs.append(pl.BlockSpec((tm, tn), lambda i, j, k: (i, j)))
        args.append(add)
    return pl.pallas_call(
        body, name=name, grid=(M // tm, N // tn, nk), in_specs=in_specs,
        out_specs=pl.BlockSpec((tm, tn), lambda i, j, k: (i, j)),
        out_shape=jax.ShapeDtypeStruct((M, N), out_dtype),
        scratch_shapes=[pltpu.VMEM((tm, tn), f32)] if nk > 1 else [],
        compiler_params=pltpu.CompilerParams(dimension_semantics=("parallel", "parallel", "arbitrary"), vmem_limit_bytes=VMEM_LIMIT),
    )(*args)


def _swiglu(u):
    return _silu(u[:, :D_FF]) * u[:, D_FF:]


def _res_ln(x, f, g, b, s):
    r = ALPHA * x + s * f
    mu = jnp.mean(r, axis=-1, keepdims=True)
    rc = r - mu
    var = jnp.mean(rc * rc, axis=-1, keepdims=True)
    return rc * lax.rsqrt(var + LN_EPS) * g + b


def _ln(x, g, b):
    mu = jnp.mean(x, axis=-1, keepdims=True)
    xc = x - mu
    var = jnp.mean(xc * xc, axis=-1, keepdims=True)
    return xc * lax.rsqrt(var + LN_EPS) * g + b


def _dn_pre(first, xc, xp, ba, cw, hp):
    xp = jnp.where(first, 0.0, xp)
    y = cw[DN_CONV - 1:DN_CONV, :] * xc
    for j in range(DN_CONV - 1):
        y = y + cw[j:j + 1, :] * _shift_rows(xc, xp, DN_CONV - 1 - j)
    c = _silu(y)
    qs, ks, vs, gs, bs = [], [], [], [], []
    nqk = DN_HEADS * DN_DK
    for h in range(DN_HEADS):
        q = c[:, h * DN_DK:(h + 1) * DN_DK]
        k = c[:, nqk + h * DN_DK:nqk + (h + 1) * DN_DK]
        v = c[:, 2 * nqk + h * DN_DK:2 * nqk + (h + 1) * DN_DK]
        qs.append(q * lax.rsqrt(jnp.sum(q * q, axis=-1, keepdims=True) + RMS_EPS))
        ks.append(k * lax.rsqrt(jnp.sum(k * k, axis=-1, keepdims=True) + RMS_EPS))
        vs.append(v)
        beta = jax.nn.sigmoid(_lane_pick(ba, h))
        a_log = _lane_pick(hp[0:1, :], h)
        dt = _lane_pick(hp[1:2, :], h)
        g = -jnp.exp(a_log) * jax.nn.softplus(_lane_pick(ba, DN_HEADS + h) + dt)
        gs.append(jnp.broadcast_to(g, q.shape))
        bs.append(jnp.broadcast_to(beta, q.shape))
    return tuple(jnp.stack(t, axis=0) for t in (qs, ks, vs, gs, bs))


def _wy(q, k, v, gb, bb):
    B, C, _ = q.shape
    ri = lax.broadcasted_iota(i32, (B, C, C), 1)
    ci = lax.broadcasted_iota(i32, (B, C, C), 2)
    tril, strict = ri >= ci, ri > ci
    gc = hdot(tril.astype(f32), gb, 'nn')
    gl = jnp.broadcast_to(_last_row(gc, C), gc.shape)
    col = _lane_head(gc, C)
    decay = jnp.exp(jnp.where(tril, col - jnp.swapaxes(col, 1, 2), NEG_INF))
    qs = q * (DN_DK ** -0.5)
    kb = k * bb
    a = jnp.where(strict, bdot(kb, k, 'nt') * decay, 0.0)
    sol = _tri_solve(a, jnp.concatenate([v * bb, kb * jnp.exp(gc)], axis=-1))
    qk = jnp.where(tril, bdot(qs, k, 'nt') * decay, 0.0)
    kt = k * jnp.exp(gl - gc)
    qd = qs * jnp.exp(gc)
    return qd, qk, sol[..., :DN_DK], sol[..., DN_DK:], kt, jnp.exp(gl)


def _scan_step(S, qd, qk, u, w, kt, egl):
    vn = u - bdot(w, S, 'nn')
    o = bdot(qd, S, 'nn') + bdot(qk, vn, 'nn')
    e2 = jnp.concatenate([egl] * (DN_DK // DN_CHUNK), axis=1)
    return o, S * e2 + bdot(kt, vn, 'tn')


def _dn_post(o, z, nw):
    outs = []
    for h in range(DN_HEADS):
        oh = o[h]
        oh = oh * lax.rsqrt(jnp.mean(oh * oh, axis=-1, keepdims=True) + RMS_EPS) * nw
        outs.append(oh * _silu(z[:, h * DN_DK:(h + 1) * DN_DK]))
    return jnp.concatenate(outs, axis=1)


def _swa(first, q, kc, kp, vc, vp, snk):
    W = q.shape[1]
    G = SWA_HEADS // SWA_KV_HEADS
    r = lax.broadcasted_iota(i32, (G, W, 2 * W), 1)
    c = lax.broadcasted_iota(i32, (G, W, 2 * W), 2)
    mask = (c > W + r - WINDOW) & (c <= W + r) & jnp.logical_or(c >= W, jnp.logical_not(first))
    sink_all = _lane_pick(snk, 0)
    outs = []
    for j in range(SWA_KV_HEADS):
        qj = q[j * G:(j + 1) * G]
        kk = jnp.broadcast_to(jnp.concatenate([kp[j], kc[j]], axis=0)[None], (G, 2 * W, SWA_DH))
        vv = jnp.broadcast_to(jnp.concatenate([vp[j], vc[j]], axis=0)[None], (G, 2 * W, SWA_DH))
        s = jnp.where(mask, bdot(qj, kk, 'nt') * (SWA_DH ** -0.5), NEG_INF)
        sink = sink_all[j * G:(j + 1) * G][:, :, None]
        m = jnp.maximum(jnp.max(s, axis=-1, keepdims=True), sink)
        p = jnp.exp(s - m)
        p = p / (jnp.sum(p, axis=-1, keepdims=True) + jnp.exp(sink - m))
        outs.append(bdot(p, vv, 'nn'))
    return jnp.concatenate(outs, axis=0)


def _xa(q, kv):
    outs = []
    nk = XA_HEADS * XA_DH
    for h in range(XA_HEADS):
        qh = q[:, h * XA_DH:(h + 1) * XA_DH]
        kh = kv[:, h * XA_DH:(h + 1) * XA_DH]
        vh = kv[:, nk + h * XA_DH:nk + (h + 1) * XA_DH]
        s = bdot(qh, kh, 'nt') * (XA_DH ** -0.5)
        m = jnp.max(s, axis=-1, keepdims=True)
        p = jnp.exp(s - m)
        p = p / jnp.sum(p, axis=-1, keepdims=True)
        outs.append(bdot(p, vh, 'nn'))
    return jnp.concatenate(outs, axis=1)


def _merge(o_dn, o_sw, o_xa, gates, wb):
    acc = None
    for n, o in enumerate((o_dn, o_sw, o_xa)):
        t = jax.nn.sigmoid(gates[:, n * D_MODEL:(n + 1) * D_MODEL]) * bdot(o, wb[n], 'nn')
        acc = t if acc is None else acc + t
    return acc


def _row_tile(T, want):
    return _pick(T, tuple(c for c in (1024, 512, 256, 128, 64) if c <= want))


def _res_ln_fwd(name, x, f, g, b, s):
    T = x.shape[0]
    tm = _row_tile(T, 512)

    def fn(i, x, f, g, b):
        h = _res_ln(x, f, g, b, s)
        return h, h

    return _tile_call(name, fn, T // tm, [_rows(x, tm), _rows(f, tm), _full(g), _full(b)],
                      [((T, D_MODEL), f32, (tm, D_MODEL), lambda i: (i, 0)), ((T, D_MODEL), bf16, (tm, D_MODEL), lambda i: (i, 0))])


def _res_ln_bwd(name, x, f, g, b, s, dh):
    T = x.shape[0]
    tm = _row_tile(T, 512)

    def fn(i, x, f, g, b, dh):
        _, vjp = jax.vjp(lambda x, f, g, b: _res_ln(x, f, g, b, s), x, f, g, b)
        return vjp(dh)

    return _tile_call(name, fn, T // tm, [_rows(x, tm), _rows(f, tm), _full(g), _full(b), _rows(dh, tm)],
                      [((T, D_MODEL), f32, (tm, D_MODEL), lambda i: (i, 0)), ((T, D_MODEL), bf16, (tm, D_MODEL), lambda i: (i, 0))],
                      [((1, D_MODEL), f32), ((1, D_MODEL), f32)])


FF_BLK = 256


def _pair_gate_up(w):
    d = w.shape[0]
    return jnp.stack([w[:, :D_FF].reshape(d, -1, FF_BLK), w[:, D_FF:].reshape(d, -1, FF_BLK)], axis=2).reshape(d, 2 * D_FF)


def _unpair_gate_up(w):
    d = w.shape[0]
    t = w.reshape(d, -1, 2, FF_BLK)
    return jnp.concatenate([t[:, :, 0].reshape(d, D_FF), t[:, :, 1].reshape(d, D_FF)], axis=1)


def _swiglu_blk(u):
    return _silu(u[:, :FF_BLK]) * u[:, FF_BLK:]


def _mm_spec(tm, tn):
    return dict(compiler_params=pltpu.CompilerParams(dimension_semantics=("parallel", "arbitrary"), vmem_limit_bytes=VMEM_LIMIT))


def _gu_act(name, xb, w_pair):
    T, D = xb.shape
    tm = _pick(T, (1024, 512, 256, 128))

    def body(x_ref, w_ref, u_ref, a_ref):
        r = lax.dot_general(x_ref[...], w_ref[...], (((1,), (0,)), ((), ())), preferred_element_type=f32)
        u_ref[...] = r.astype(bf16)
        a_ref[...] = _swiglu_blk(r).astype(bf16)

    return pl.pallas_call(
        body, name=name, grid=(T // tm, D_FF // FF_BLK),
        in_specs=[pl.BlockSpec((tm, D), lambda i, j: (i, 0)), pl.BlockSpec((D, 2 * FF_BLK), lambda i, j: (0, j))],
        out_specs=[pl.BlockSpec((tm, 2 * FF_BLK), lambda i, j: (i, j)), pl.BlockSpec((tm, FF_BLK), lambda i, j: (i, j))],
        out_shape=[jax.ShapeDtypeStruct((T, 2 * D_FF), bf16), jax.ShapeDtypeStruct((T, D_FF), bf16)],
        **_mm_spec(tm, FF_BLK))(xb, w_pair)


def _down_dx_act_bwd(name, df, w_down_t, u):
    T, D = df.shape
    tm = _pick(T, (1024, 512, 256, 128))

    def body(df_ref, w_ref, u_ref, du_ref):
        da = lax.dot_general(df_ref[...], w_ref[...], (((1,), (0,)), ((), ())), preferred_element_type=f32)
        _, vjp = jax.vjp(_swiglu_blk, u_ref[...].astype(f32))
        du_ref[...] = vjp(da)[0].astype(bf16)

    return pl.pallas_call(
        body, name=name, grid=(T // tm, D_FF // FF_BLK),
        in_specs=[pl.BlockSpec((tm, D), lambda i, j: (i, 0)), pl.BlockSpec((D, FF_BLK), lambda i, j: (0, j)),
                  pl.BlockSpec((tm, 2 * FF_BLK), lambda i, j: (i, j))],
        out_specs=pl.BlockSpec((tm, 2 * FF_BLK), lambda i, j: (i, j)),
        out_shape=jax.ShapeDtypeStruct((T, 2 * D_FF), bf16),
        **_mm_spec(tm, FF_BLK))(df, w_down_t, u)


def _mm_res_ln(name, a, w, x, g, b, s):
    T, K = a.shape
    tm = _pick(T, (512, 256, 128))

    def body(a_ref, w_ref, x_ref, g_ref, b_ref, f_ref, h_ref, hb_ref):
        f = lax.dot_general(a_ref[...], w_ref[...], (((1,), (0,)), ((), ())), preferred_element_type=f32)
        h = _res_ln(x_ref[...], f, g_ref[...], b_ref[...], s)
        f_ref[...] = f
        h_ref[...] = h
        hb_ref[...] = h.astype(bf16)

    row = pl.BlockSpec((tm, D_MODEL), lambda i: (i, 0))
    vec = pl.BlockSpec((1, D_MODEL), lambda i: (0, 0))
    return pl.pallas_call(
        body, name=name, grid=(T // tm,),
        in_specs=[pl.BlockSpec((tm, K), lambda i: (i, 0)), pl.BlockSpec((K, D_MODEL), lambda i: (0, 0)), row, vec, vec],
        out_specs=[row, row, row],
        out_shape=[jax.ShapeDtypeStruct((T, D_MODEL), f32), jax.ShapeDtypeStruct((T, D_MODEL), f32), jax.ShapeDtypeStruct((T, D_MODEL), bf16)],
        compiler_params=pltpu.CompilerParams(dimension_semantics=("parallel",), vmem_limit_bytes=VMEM_LIMIT))(a, w, x, g, b)


def _hm_spec(tm, w=DN_DK):
    return ((DN_HEADS, tm, w), lambda i: (0, i, 0))


def _prev(i):
    return jnp.maximum(i - 1, 0)


def _dn_pre_fwd(name, p, pba, cw, hp):
    T = p.shape[0]
    tm = _row_tile(T, 256)
    n = T // tm
    W3 = 3 * DN_HEADS * DN_DK
    ins = [(p, (tm, W3), lambda i: (i, 0)), (p, (tm, W3), lambda i: (_prev(i), 0)),
           (pba, (tm, LANES), lambda i: (i, 0)), _full(cw), _full(hp)]
    blk, im = _hm_spec(tm)
    return _tile_call(name, lambda i, *a: _dn_pre(i == 0, *a), n, ins, [((DN_HEADS, T, DN_DK), f32, blk, im)] * 5)


def _dn_pre_bwd(name, p, pba, cw, hp, cts):
    T = p.shape[0]
    tm = _row_tile(T, 256)
    n = T // tm
    W3 = 3 * DN_HEADS * DN_DK
    blk, im = _hm_spec(tm)
    ins = [(p, (tm, W3), lambda i: (i, 0)), (p, (tm, W3), lambda i: (_prev(i), 0)),
           (pba, (tm, LANES), lambda i: (i, 0)), _full(cw), _full(hp)] + [(c, blk, im) for c in cts]

    def fn(i, xc, xp, ba, cw, hp, *cts):
        _, vjp = jax.vjp(lambda *a: _dn_pre(i == 0, *a), xc, xp, ba, cw, hp)
        return vjp(tuple(cts))

    return _tile_call(name, fn, n, ins,
                      [((T, W3), f32, (tm, W3), lambda i: (i, 0)), ((T, W3), f32, (tm, W3), lambda i: (i, 0)),
                       ((T, LANES), f32, (tm, LANES), lambda i: (i, 0))],
                      [(tuple(cw.shape), f32), (tuple(hp.shape), f32)])


def _wy_split(x, nc):
    return x.reshape(DN_HEADS * nc, DN_CHUNK, x.shape[-1])


def _wy_fwd(name, q, k, v, gb, bb):
    T = q.shape[1]
    tm = _row_tile(T, 256)
    nc = tm // DN_CHUNK

    def fn(i, *a):
        outs = _wy(*[_wy_split(t, nc) for t in a])
        return [o.reshape(DN_HEADS, tm, o.shape[-1]) for o in outs]

    blk, im = _hm_spec(tm)
    blk2, im2 = _hm_spec(tm, DN_CHUNK)
    ins = [(t, blk, im) for t in (q, k, v, gb, bb)]
    full = (DN_HEADS, T, DN_DK)
    outs = [(full, bf16, blk, im), ((DN_HEADS, T, DN_CHUNK), bf16, blk2, im2), (full, f32, blk, im),
            (full, bf16, blk, im), (full, bf16, blk, im), (full, f32, blk, im)]
    return _tile_call(name, fn, T // tm, ins, outs)


def _wy_bwd(name, q, k, v, gb, bb, cts):
    T = q.shape[1]
    tm = _row_tile(T, 256)
    nc = tm // DN_CHUNK

    def fn(i, q, k, v, gb, bb, *cts):
        _, vjp = jax.vjp(_wy, *[_wy_split(t, nc) for t in (q, k, v, gb, bb)])
        grads = vjp(tuple(_wy_split(c.astype(f32), nc) for c in cts))
        return [g.reshape(DN_HEADS, tm, DN_DK) for g in grads]

    blk, im = _hm_spec(tm)
    blk2, im2 = _hm_spec(tm, DN_CHUNK)
    ins = [(t, blk, im) for t in (q, k, v, gb, bb)]
    ins += [(c, blk2 if c.shape[-1] == DN_CHUNK else blk, im) for c in cts]
    return _tile_call(name, fn, T // tm, ins, [((DN_HEADS, T, DN_DK), f32, blk, im)] * 5)


def _scan_fwd(name, qd, qk, u, w, kt, egl):
    T = qd.shape[1]
    C = DN_CHUNK
    U = _pick(T // C, (SCAN_CHUNKS, 1))
    n = T // (C * U)

    def body(qd_ref, qk_ref, u_ref, w_ref, kt_ref, egl_ref, o_ref, sall_ref, s_ref):
        i = pl.program_id(0)

        @pl.when(i == 0)
        def _():
            s_ref[...] = jnp.zeros(s_ref.shape, f32)

        S = s_ref[...]
        for j in range(U):
            sl = pl.ds(j * C, C)
            sall_ref[:, j] = S
            o, S = _scan_step(S, *[r[:, sl, :] for r in (qd_ref, qk_ref, u_ref, w_ref, kt_ref, egl_ref)])
            o_ref[:, sl, :] = o
        s_ref[...] = S

    blk, im = _hm_spec(C * U)
    blk2, im2 = _hm_spec(C * U, C)
    return pl.pallas_call(
        body, name=name, grid=(n,),
        in_specs=[pl.BlockSpec(blk, im), pl.BlockSpec(blk2, im2)] + [pl.BlockSpec(blk, im)] * 4,
        out_specs=[pl.BlockSpec(blk, im), pl.BlockSpec((DN_HEADS, U, DN_DK, DN_DK), lambda i: (0, i, 0, 0))],
        out_shape=[jax.ShapeDtypeStruct((DN_HEADS, T, DN_DK), f32), jax.ShapeDtypeStruct((DN_HEADS, n * U, DN_DK, DN_DK), f32)],
        scratch_shapes=[pltpu.VMEM((DN_HEADS, DN_DK, DN_DK), f32)],
        compiler_params=pltpu.CompilerParams(dimension_semantics=("arbitrary",), vmem_limit_bytes=VMEM_LIMIT),
    )(qd, qk, u, w, kt, egl)


def _scan_bwd(name, qd, qk, u, w, kt, egl, s_all, do):
    T = qd.shape[1]
    C = DN_CHUNK
    U = _pick(T // C, (SCAN_CHUNKS, 1))
    n = T // (C * U)

    def body(qd_ref, qk_ref, u_ref, w_ref, kt_ref, egl_ref, sall_ref, do_ref,
             dqd_ref, dqk_ref, du_ref, dw_ref, dkt_ref, degl_ref, ds_ref):
        i = pl.program_id(0)

        @pl.when(i == 0)
        def _():
            ds_ref[...] = jnp.zeros(ds_ref.shape, f32)

        dS = ds_ref[...]
        for j in reversed(range(U)):
            sl = pl.ds(j * C, C)
            args = [r[:, sl, :].astype(f32) for r in (qd_ref, qk_ref, u_ref, w_ref, kt_ref, egl_ref)]
            _, vjp = jax.vjp(_scan_step, sall_ref[:, j], *args)
            dS, *cts = vjp((do_ref[:, sl, :], dS))
            for r, v in zip((dqd_ref, dqk_ref, du_ref, dw_ref, dkt_ref, degl_ref), cts):
                r[:, sl, :] = v
        ds_ref[...] = dS

    blk = (DN_HEADS, C * U, DN_DK)
    blk2 = (DN_HEADS, C * U, C)
    rim = lambda i: (0, n - 1 - i, 0)
    sp, sp2 = pl.BlockSpec(blk, rim), pl.BlockSpec(blk2, rim)
    full, full2 = jax.ShapeDtypeStruct((DN_HEADS, T, DN_DK), f32), jax.ShapeDtypeStruct((DN_HEADS, T, C), f32)
    return pl.pallas_call(
        body, name=name, grid=(n,),
        in_specs=[sp, sp2, sp, sp, sp, sp, pl.BlockSpec((DN_HEADS, U, DN_DK, DN_DK), lambda i: (0, n - 1 - i, 0, 0)), sp],
        out_specs=[sp, sp2, sp, sp, sp, sp],
        out_shape=[full, full2, full, full, full, full],
        scratch_shapes=[pltpu.VMEM((DN_HEADS, DN_DK, DN_DK), f32)],
        compiler_params=pltpu.CompilerParams(dimension_semantics=("arbitrary",), vmem_limit_bytes=VMEM_LIMIT),
    )(qd, qk, u, w, kt, egl, s_all, do)


def _dn_post_fwd(name, o, p, nw):
    T = p.shape[0]
    tm = _row_tile(T, 512)
    blk, im = _hm_spec(tm)
    W = DN_HEADS * DN_DK
    return _tile_call(name, lambda i, o, z, nw: _dn_post(o, z, nw), T // tm,
                      [(o, blk, im), (p, (tm, W), lambda i: (i, 1536 // W)), _full(nw)],
                      [((T, W), f32, (tm, W), lambda i: (i, 0))])[0]


def _dn_post_bwd(name, o, p, nw, dout):
    T = p.shape[0]
    tm = _row_tile(T, 512)
    blk, im = _hm_spec(tm)
    W = DN_HEADS * DN_DK

    def fn(i, o, z, nw, dout):
        _, vjp = jax.vjp(_dn_post, o, z, nw)
        return vjp(dout)

    return _tile_call(name, fn, T // tm,
                      [(o, blk, im), (p, (tm, W), lambda i: (i, 1536 // W)), _full(nw), _rows(dout, tm)],
                      [((DN_HEADS, T, DN_DK), f32, blk, im), ((T, W), f32, (tm, W), lambda i: (i, 0))],
                      [((1, DN_DK), f32)])


def _swa_tile(T, big):
    return _pick(T, (2 * WINDOW, WINDOW)) if big else WINDOW


def _swa_specs(T, big):
    W = _swa_tile(T, big)
    qs = ((SWA_HEADS, W, SWA_DH), lambda i: (0, i, 0))
    kc = ((SWA_KV_HEADS, W, SWA_DH), lambda i: (0, i, 0))
    kp = ((SWA_KV_HEADS, W, SWA_DH), lambda i: (0, _prev(i), 0))
    return qs, kc, kp


def _swa_fwd(name, q, k, v, snk):
    T = q.shape[1]
    qs, kc, kp = _swa_specs(T, True)
    return _tile_call(name, lambda i, *a: _swa(i == 0, *a), T // _swa_tile(T, True),
                      [(q, *qs), (k, *kc), (k, *kp), (v, *kc), (v, *kp), _full(snk)],
                      [((SWA_HEADS, T, SWA_DH), f32, *qs)])[0]


def _swa_bwd(name, q, k, v, snk, do):
    T = q.shape[1]
    qs, kc, kp = _swa_specs(T, False)

    def fn(i, q, k_c, k_p, v_c, v_p, snk, do):
        _, vjp = jax.vjp(lambda *a: _swa(i == 0, *a), q, k_c, k_p, v_c, v_p, snk)
        return vjp(do)

    kv_out = ((SWA_KV_HEADS, T, SWA_DH), f32, *kc)
    return _tile_call(name, fn, T // _swa_tile(T, False),
                      [(q, *qs), (k, *kc), (k, *kp), (v, *kc), (v, *kp), _full(snk), (do, *qs)],
                      [((SWA_HEADS, T, SWA_DH), f32, *qs), kv_out, kv_out, kv_out, kv_out],
                      [(tuple(snk.shape), f32)])


def _swa_comb(name, cur, prv):
    T = cur.shape[1]
    n = T // _swa_tile(T, False)
    blk = (SWA_KV_HEADS, _swa_tile(T, False), SWA_DH)

    def fn(i, c, p):
        return c + jnp.where(i == n - 1, 0.0, p)

    return _tile_call(name, fn, n, [(cur, blk, lambda i: (0, i, 0)), (prv, blk, lambda i: (0, jnp.minimum(i + 1, n - 1), 0))],
                      [(tuple(cur.shape), f32, blk, lambda i: (0, i, 0))])[0]


def _xa_fwd(name, p, kv):
    T = p.shape[0]
    tm = _row_tile(T, 256)
    W = XA_HEADS * XA_DH
    return _tile_call(name, lambda i, q, kv: _xa(q, kv), T // tm, [(p, (tm, W), lambda i: (i, 2560 // W)), _full(kv)],
                      [((T, W), f32, (tm, W), lambda i: (i, 0))])[0]


def _xa_bwd(name, p, kv, do):
    T = p.shape[0]
    tm = _row_tile(T, 256)
    W = XA_HEADS * XA_DH

    def fn(i, q, kv, do):
        _, vjp = jax.vjp(_xa, q, kv)
        return vjp(do)

    return _tile_call(name, fn, T // tm, [(p, (tm, W), lambda i: (i, 2560 // W)), _full(kv), _rows(do, tm)],
                      [((T, W), bf16, (tm, W), lambda i: (i, 0))], [(tuple(kv.shape), f32)])


def _merge_fwd(name, o_dn, o_sw, o_xa, p, wb):
    T = p.shape[0]
    tm = _row_tile(T, 256)
    GW = N_BRANCH * D_MODEL
    return _tile_call(name, lambda i, a, b, c, g, w: _merge(a, b, c, g, w.astype(f32)), T // tm,
                      [_rows(o_dn, tm), _rows(o_sw, tm), _rows(o_xa, tm), (p, (tm, GW), lambda i: (i, 1)), _full(wb)],
                      [((T, D_MODEL), bf16, (tm, D_MODEL), lambda i: (i, 0))])[0]


def _merge_bwd(name, o_dn, o_sw, o_xa, p, wb, dm):
    T = p.shape[0]
    tm = _row_tile(T, 256)
    GW = N_BRANCH * D_MODEL

    def fn(i, a, b, c, g, w, dm):
        _, vjp = jax.vjp(_merge, a, b, c, g, w.astype(f32))
        return vjp(dm)

    bo = ((T, BRANCH_W), f32, (tm, BRANCH_W), lambda i: (i, 0))
    return _tile_call(name, fn, T // tm,
                      [_rows(o_dn, tm), _rows(o_sw, tm), _rows(o_xa, tm), (p, (tm, GW), lambda i: (i, 1)), _full(wb), _rows(dm, tm)],
                      [bo, bo, bo, ((T, GW), bf16, (tm, GW), lambda i: (i, 0))], [(tuple(wb.shape), f32)])


def _assemble_dp(name, dxc, dxp, dz, dswq, dxaq, dgates, dswk, dswv, dba):
    T = dz.shape[0]
    tm = _row_tile(T, 256)
    n = T // tm

    def fn(i, dxc, dxp, dz, dswq, dxaq, dgates, dswk, dswv, dba):
        dqkv = dxc + jnp.where(i == n - 1, 0.0, dxp)
        parts = [dqkv, dz, dswq, dxaq, dgates, dswk, dswv, dba, jnp.zeros((tm, LANES), f32)]
        return jnp.concatenate([t.astype(f32) for t in parts], axis=1)

    ins = [_rows(dxc, tm), (dxp, (tm, dxp.shape[1]), lambda i: (jnp.minimum(i + 1, n - 1), 0))]
    ins += [_rows(t, tm) for t in (dz, dswq, dxaq, dgates, dswk, dswv, dba)]
    return _tile_call(name, fn, n, ins, [((T, D_INP), bf16, (tm, D_INP), lambda i: (i, 0))])[0]


def _loss_grad(name, y, tgt):
    T = y.shape[0]
    tm = _row_tile(T, 512)

    def fn(i, y, t):
        e = y - t
        part = 0.5 * jnp.sum(jnp.mean(e * e, axis=-1, keepdims=True), axis=0, keepdims=True)
        return e * (1.0 / D_MODEL), jnp.broadcast_to(part, (1, LANES))

    return _tile_call(name, fn, T // tm, [_rows(y, tm), _rows(tgt, tm)],
                      [((T, D_MODEL), f32, (tm, D_MODEL), lambda i: (i, 0))], [((1, LANES), f32)])


def _mem_ln_fwd(name, mem, g, b):
    M = mem.shape[0]
    return _tile_call(name, lambda i, m, g, b: (_ln(m, g, b),), 1, [_full(mem), _full(g), _full(b)],
                      [((M, D_MODEL), bf16, (M, D_MODEL), lambda i: (0, 0))])[0]


def _mem_ln_bwd(name, mem, g, b, dmn):
    def fn(i, m, g, b, d):
        _, vjp = jax.vjp(lambda g, b: _ln(m, g, b), g, b)
        return vjp(d)

    return _tile_call(name, fn, 1, [_full(mem), _full(g), _full(b), _full(dmn)], [], [((1, D_MODEL), f32), ((1, D_MODEL), f32)])


def _pad_w_in(w):
    z = jnp.zeros((w.shape[0], D_INP - D_IN), w.dtype)
    return jnp.concatenate([w[:, 0:1536], w[:, 1544:2056], w[:, 2056:2568], w[:, 2824:3336], w[:, 3336:6408],
                            w[:, 2568:2696], w[:, 2696:2824], w[:, 1536:1544], z], axis=1)


def _unpad_dw_in(d):
    return jnp.concatenate([d[:, 0:1536], d[:, 6400:6408], d[:, 1536:2048], d[:, 2048:2560], d[:, 6144:6272],
                            d[:, 6272:6400], d[:, 2560:3072], d[:, 3072:6144]], axis=1)


def _heads_major(x, nh):
    return x.reshape(x.shape[0], nh, SWA_DH).transpose(1, 0, 2)


def _heads_minor(x):
    return x.transpose(1, 0, 2).reshape(x.shape[1], -1)


def _lane_row(v, rows=8):
    out = jnp.zeros((rows, LANES), f32)
    return out.at[0, :v.shape[0]].set(v)


def _layer_fwd(l, x, xb, mem_nb, W):
    n = lambda s: f"l{l}_{s}"
    sv = {}
    u1, a1 = _gu_act(n("ffn1_gu"), xb, W['ffn1_w_gu_p'])
    f1, h1, h1b = _mm_res_ln(n("ffn1_down"), a1, W['ffn1_w_down'], x, W['ln_g'][0:1], W['ln_b'][0:1], 0.5)
    p = _mm(n("w_in"), h1b, W['w_inp'], 'nn', out_dtype=bf16)
    pba = _mm(n("w_in_ba"), h1b, W['w_ba'], 'nn')
    q, k, v, gb, bb = _dn_pre_fwd(n("dn_pre"), p, pba, W['conv_w'], W['hp'])
    qd, qk, u, w, kt, egl = _wy_fwd(n("dn_wy"), q, k, v, gb, bb)
    o_raw, s_all = _scan_fwd(n("dn_scan"), qd, qk, u, w, kt, egl)
    o_dn = _dn_post_fwd(n("dn_post"), o_raw, p, W['norm_w'])
    q_sw, k_sw, v_sw = _heads_major(p[:, 2048:2560], SWA_HEADS), _heads_major(p[:, 6144:6272], SWA_KV_HEADS), _heads_major(p[:, 6272:6400], SWA_KV_HEADS)
    o_sw = _heads_minor(_swa_fwd(n("swa"), q_sw, k_sw, v_sw, W['snk']))
    kv = _mm(n("mem_kv"), mem_nb, W['w_mem_kv'], 'nn')
    o_xa = _xa_fwd(n("xa"), p, kv)
    merged = _merge_fwd(n("merge"), o_dn, o_sw, o_xa, p, W['w_branch'])
    mo, h2, h2b = _mm_res_ln(n("w_out"), merged, W['w_out'], h1, W['ln_g'][1:2], W['ln_b'][1:2], 1.0)
    u2, a2 = _gu_act(n("ffn2_gu"), h2b, W['ffn2_w_gu_p'])
    f2, h3, h3b = _mm_res_ln(n("ffn2_down"), a2, W['ffn2_w_down'], h2, W['ln_g'][2:3], W['ln_b'][2:3], 0.5)
    sv = dict(x=x, xb=xb, u1=u1, a1=a1, f1=f1, h1=h1, h1b=h1b, p=p, pba=pba, dn=(q, k, v, gb, bb), wy=(qd, qk, u, w, kt, egl),
              o_raw=o_raw, s_all=s_all, o_dn=o_dn, sw=(q_sw, k_sw, v_sw), o_sw=o_sw, kv=kv, o_xa=o_xa, merged=merged,
              mo=mo, h2=h2, h2b=h2b, u2=u2, a2=a2, f2=f2)
    return h3, h3b, sv


def _ffn_bwd(n, tag, x, xb, u, a, f, g, b, w_gu_t, w_down_t, dh):
    dx_a, df, dg, db = _res_ln_bwd(n(f"{tag}_ln_bwd"), x, f, g, b, 0.5, dh)
    d_down = _mm(n(f"{tag}_down_dw"), a, df, 'tn')
    du = _down_dx_act_bwd(n(f"{tag}_down_dx"), df, w_down_t, u)
    d_gu = _unpair_gate_up(_mm(n(f"{tag}_gu_dw"), xb, du, 'tn'))
    dx = _mm(n(f"{tag}_gu_dx"), du, w_gu_t, 'nn', add=dx_a)
    return dx, d_gu, d_down, dg, db


def _layer_bwd(l, sv, mem_nb, W, dh3, dmem_acc):
    n = lambda s: f"l{l}_{s}"
    G = {}
    dh2, G['ffn2_w_gu'], G['ffn2_w_down'], dg2, db2 = _ffn_bwd(
        n, "ffn2", sv['h2'], sv['h2b'], sv['u2'], sv['a2'], sv['f2'], W['ln_g'][2:3], W['ln_b'][2:3], W['ffn2_w_gu_t'], W['ffn2_w_down_t'], dh3)
    dh1_a, dmo, dg1, db1 = _res_ln_bwd(n("ln1_bwd"), sv['h1'], sv['mo'], W['ln_g'][1:2], W['ln_b'][1:2], 1.0, dh2)
    G['w_out'] = _mm(n("w_out_dw"), sv['merged'], dmo, 'tn')
    dmerged = _mm(n("w_out_dx"), dmo, W['w_out_t'], 'nn')
    p = sv['p']
    do_dn, do_sw, do_xa, dgates, G['w_branch'] = _merge_bwd(n("merge_bwd"), sv['o_dn'], sv['o_sw'], sv['o_xa'], p, W['w_branch'], dmerged)
    dxaq, dkv = _xa_bwd(n("xa_bwd"), p, sv['kv'], do_xa)
    dkv = dkv.astype(bf16)
    G['w_mem_kv'] = _mm(n("mem_kv_dw"), mem_nb, dkv, 'tn')
    dmem_n = _mm(n("mem_kv_dx"), dkv, W['w_mem_kv_t'], 'nn', add=dmem_acc)
    q_sw, k_sw, v_sw = sv['sw']
    dq_sw, dkc, dkp, dvc, dvp, dsnk = _swa_bwd(n("swa_bwd"), q_sw, k_sw, v_sw, W['snk'], _heads_major(do_sw, SWA_HEADS))
    dswq = _heads_minor(dq_sw)
    dswk = _heads_minor(_swa_comb(n("swa_dk"), dkc, dkp))
    dswv = _heads_minor(_swa_comb(n("swa_dv"), dvc, dvp))
    do_raw, dz, dnw = _dn_post_bwd(n("dn_post_bwd"), sv['o_raw'], p, W['norm_w'], do_dn)
    cts = _scan_bwd(n("dn_scan_bwd"), *sv['wy'], sv['s_all'], do_raw)
    dcts = _wy_bwd(n("dn_wy_bwd"), *sv['dn'], cts)
    dxc, dxp, dba, dcw, dhp = _dn_pre_bwd(n("dn_pre_bwd"), p, sv['pba'], W['conv_w'], W['hp'], dcts)
    dp = _assemble_dp(n("dp"), dxc, dxp, dz, dswq, dxaq, dgates, dswk, dswv, dba)
    G['w_in'] = _unpad_dw_in(_mm(n("w_in_dw"), sv['h1b'], dp, 'tn'))
    dh1 = _mm(n("w_in_dx"), dp, W['w_inp_t'], 'nn', add=dh1_a)
    dx, G['ffn1_w_gu'], G['ffn1_w_down'], dg0, db0 = _ffn_bwd(
        n, "ffn1", sv['x'], sv['xb'], sv['u1'], sv['a1'], sv['f1'], W['ln_g'][0:1], W['ln_b'][0:1], W['ffn1_w_gu_t'], W['ffn1_w_down_t'], dh1)
    G['ln_g'] = jnp.concatenate([dg0, dg1, dg2], axis=0)
    G['ln_b'] = jnp.concatenate([db0, db1, db2], axis=0)
    G['dn_conv_w'] = dcw
    G['dn_a_log'] = dhp[0, :DN_HEADS]
    G['dn_dt_bias'] = dhp[1, :DN_HEADS]
    G['dn_norm_w'] = dnw[0]
    G['swa_sinks'] = dsnk[:, 0]
    return dx, dmem_n, G


def _local_step(x, mem, tgt, Wf):
    mem_g, mem_b = Wf['mem_ln_g'][None, :], Wf['mem_ln_b'][None, :]
    mem_nb = _mem_ln_fwd("mem_ln", mem, mem_g, mem_b)
    layers = []
    for l in range(DEPTH):
        layers.append(dict(
            ln_g=Wf['ln_g'][l], ln_b=Wf['ln_b'][l], ffn1_w_gu_p=_pair_gate_up(Wf['ffn1_w_gu'][l]), ffn1_w_down=Wf['ffn1_w_down'][l],
            w_inp=_pad_w_in(Wf['w_in'][l]), conv_w=Wf['dn_conv_w'][l],
            hp=jnp.zeros((8, LANES), f32).at[0, :DN_HEADS].set(Wf['dn_a_log'][l]).at[1, :DN_HEADS].set(Wf['dn_dt_bias'][l]),
            norm_w=Wf['dn_norm_w'][l][None, :], snk=jnp.broadcast_to(Wf['swa_sinks'][l][:, None], (SWA_HEADS, LANES)),
            w_mem_kv=Wf['w_mem_kv'][l], w_branch=Wf['w_branch'][l], w_out=Wf['w_out'][l],
            ffn2_w_gu_p=_pair_gate_up(Wf['ffn2_w_gu'][l]), ffn2_w_down=Wf['ffn2_w_down'][l]))
        layers[l]['w_ba'] = layers[l]['w_inp'][:, 6400:6400 + LANES]
        for k in ('ffn1_w_gu_p', 'ffn1_w_down', 'w_inp', 'w_mem_kv', 'w_out', 'ffn2_w_gu_p', 'ffn2_w_down'):
            layers[l][k.replace('_p', '') + '_t'] = layers[l][k].T
    h, hb = x, x.astype(bf16)
    saved = []
    for l in range(DEPTH):
        h, hb, sv = _layer_fwd(l, h, hb, mem_nb, layers[l])
        saved.append(sv)
    dh, loss_row = _loss_grad("loss", h, tgt)
    grads = [None] * DEPTH
    dmem_n = None
    for l in reversed(range(DEPTH)):
        dh, dmem_n, grads[l] = _layer_bwd(l, saved[l], mem_nb, layers[l], dh, dmem_n)
    dmg, dmb = _mem_ln_bwd("mem_ln_bwd", mem, mem_g, mem_b, dmem_n)
    G = {k: jnp.stack([grads[l][k] for l in range(DEPTH)], axis=0) for k in grads[0]}
    G['mem_ln_g'], G['mem_ln_b'] = dmg[0], dmb[0]
    return loss_row, dh, G


SEG_ALIGN = 2048


def _round_up(n, m):
    return (n + m - 1) // m * m


def _layout(names, sizes, mult):
    off, table = 0, {}
    for nm in names:
        table[nm] = (off, sizes[nm])
        off += _round_up(sizes[nm], SEG_ALIGN)
    return table, _round_up(off // LANES, mult)


def _pack(table, rows, flat):
    names = list(table)
    lead = flat[names[0]].shape[:-1]
    parts, pos = [], 0
    for nm in names:
        off, size = table[nm]
        if off > pos:
            parts.append(jnp.zeros(lead + (off - pos,), flat[nm].dtype))
        parts.append(flat[nm])
        pos = off + size
    total = rows * LANES
    if total > pos:
        parts.append(jnp.zeros(lead + (total - pos,), parts[-1].dtype))
    return jnp.concatenate(parts, axis=-1).reshape(lead + (rows, LANES))


def _unpack(table, packed, nm):
    off, size = table[nm]
    flat = packed.reshape(packed.shape[:-2] + (-1,))
    return flat[..., off:off + size]


def _split_shards(full, ax):
    shp = full.shape
    t = full.reshape(shp[:ax] + (N_SHARD, shp[ax] // N_SHARD) + shp[ax + 1:])
    return jnp.moveaxis(t, ax, 0)


def _join_shards(sh4, ax):
    return jnp.concatenate([sh4[s] for s in range(N_SHARD)], axis=ax)


ANY = pl.BlockSpec(memory_space=pl.ANY)


def _me():
    return lax.axis_index("x"), lax.axis_index("y"), lax.axis_index("c")


def _comm_call(name, body, arrays, out_shapes, n_sem):
    n = len(arrays)
    scratch = [pltpu.SemaphoreType.DMA((n, n_sem)), pltpu.SemaphoreType.DMA((n, n_sem))]
    return pl.pallas_call(
        body, name=name, out_shape=out_shapes, in_specs=[ANY] * n, out_specs=[ANY] * n, scratch_shapes=scratch,
        compiler_params=pltpu.CompilerParams(has_side_effects=True),
    )(*arrays)


def _all_gather(name, xs):
    n = len(xs)

    def body(*refs):
        x_refs, out_refs = refs[:n], refs[n:2 * n]
        send_sems, recv_sems = refs[2 * n:]
        mx, my, mc = _me()
        chips = [(1 - mx, my), (mx, 1 - my), (1 - mx, 1 - my)]

        def copy(a, k, shard, half, to, src=None):
            dst = out_refs[a].at[shard, half]
            return pltpu.make_async_remote_copy(src_ref=dst if src is None else src, dst_ref=dst, send_sem=send_sems.at[a, k],
                                                recv_sem=recv_sems.at[a, k], device_id=to, device_id_type=MESH)

        first = [copy(a, j, 2 * mx + my, mc, (cx, cy, mc), src=x_refs[a].at[mc]) for j, (cx, cy) in enumerate(chips) for a in range(n)]
        for cp in first:
            cp.start()
        passed = []
        for j, (cx, cy) in enumerate(chips):
            for a in range(n):
                copy(a, j, 2 * cx + cy, mc, (mx, my, mc)).wait_recv()
                passed.append(copy(a, 3 + j, 2 * cx + cy, mc, (mx, my, 1 - mc)))
                passed[-1].start()
        for j, (cx, cy) in enumerate(chips):
            for a in range(n):
                copy(a, 3 + j, 2 * cx + cy, 1 - mc, (mx, my, mc)).wait_recv()
        for cp in first + passed:
            cp.wait_send()

    return _comm_call(name, body, xs, [jax.ShapeDtypeStruct((N_SHARD,) + x.shape, x.dtype) for x in xs], 6)


def _pair_exchange(name, items):
    n = len(items)

    def body(*refs):
        src_refs, dst_refs = refs[:n], refs[n:2 * n]
        send_sems, recv_sems = refs[2 * n:]
        mx, my, mc = _me()
        cps = [pltpu.make_async_remote_copy(src_ref=src_refs[a].at[s, 1 - mc], dst_ref=dst_refs[a].at[s], send_sem=send_sems.at[a, s],
                                            recv_sem=recv_sems.at[a, s], device_id=(mx, my, 1 - mc), device_id_type=MESH)
               for a in range(n) for s in range(N_SHARD)]
        for cp in cps:
            cp.start()
        for cp in cps:
            cp.wait()

    return _comm_call(name, body, items, [jax.ShapeDtypeStruct((N_SHARD,) + t.shape[2:], t.dtype) for t in items], N_SHARD)


def _chip_exchange(name, parts):
    n = len(parts)

    def body(*refs):
        p_refs, dst_refs = refs[:n], refs[n:2 * n]
        send_sems, recv_sems = refs[2 * n:]
        mx, my, mc = _me()
        chips = [(1 - mx, my), (mx, 1 - my), (1 - mx, 1 - my)]
        cps = [pltpu.make_async_remote_copy(src_ref=p_refs[a].at[2 * cx + cy], dst_ref=dst_refs[a].at[j], send_sem=send_sems.at[a, j],
                                            recv_sem=recv_sems.at[a, j], device_id=(cx, cy, mc), device_id_type=MESH)
               for a in range(n) for j, (cx, cy) in enumerate(chips)]
        for cp in cps:
            cp.start()
        for cp in cps:
            cp.wait()

    return _comm_call(name, body, parts, [jax.ShapeDtypeStruct((3,) + t.shape[1:], t.dtype) for t in parts], 3)


def _pair_swap(name, reds):
    n = len(reds)

    def body(*refs):
        r_refs, out_refs = refs[:n], refs[n:2 * n]
        send_sems, recv_sems = refs[2 * n:]
        mx, my, mc = _me()
        cps = [pltpu.make_async_remote_copy(src_ref=r_refs[a], dst_ref=out_refs[a], send_sem=send_sems.at[a, 0],
                                            recv_sem=recv_sems.at[a, 0], device_id=(mx, my, 1 - mc), device_id_type=MESH)
               for a in range(n)]
        for cp in cps:
            cp.start()
        for cp in cps:
            cp.wait()

    return _comm_call(name, body, reds, [jax.ShapeDtypeStruct(t.shape, t.dtype) for t in reds], 1)


EW_BLOCK_BYTES = 1 << 20


def _ew_call(name, fn, ins, n_out, out_dtype=f32):
    shape = ins[0].shape
    last = shape[-1]
    flat = [a.reshape(-1, last) for a in ins]
    R = flat[0].shape[0]
    cands = tuple(c for c in (4096, 2048, 1024, 512, 256, 128, 64, 32, 16, 8) if c * last * 4 <= EW_BLOCK_BYTES)
    tr = _pick(R, cands)
    res = _tile_call(name, lambda i, *a: fn(*a), R // tr, [(a, (tr, last), lambda i: (i, 0)) for a in flat],
                     [((R, last), out_dtype, (tr, last), lambda i: (i, 0))] * n_out)
    return [r.reshape(shape) for r in res]


def _adamw(g, w, m, v):
    m = B1 * m + (1.0 - B1) * g
    v = B2 * v + (1.0 - B2) * jnp.square(g)
    m_hat = m / (1.0 - B1 ** STEP)
    v_hat = v / (1.0 - B2 ** STEP)
    return -LR * (m_hat / (jnp.sqrt(v_hat) + EPS) + WD * w), m, v


def _adamw_call(name, mine, theirs, w, m, v, mc1):
    shape, last = w.shape, w.shape[-1]
    g2 = [t.reshape(-1, last) for t in (mine, theirs)]
    w3 = [t.reshape(2, -1, last) for t in (w, m, v)]
    R = g2[0].shape[0]
    tr = _pick(R, tuple(c for c in (4096, 2048, 1024, 512, 256, 128, 64, 32, 16, 8) if c * last * 4 <= EW_BLOCK_BYTES))

    def body(mc_ref, mine_ref, theirs_ref, w_ref, m_ref, v_ref, g_out, d_out, m_out, v_out):
        g = jnp.where(pl.program_id(0) == mc_ref[0], mine_ref[...], theirs_ref[...])
        d, nm, nv = _adamw(g, w_ref[0], m_ref[0], v_ref[0])
        g_out[0], d_out[0], m_out[0], v_out[0] = g, d, nm, nv

    half = pl.BlockSpec((tr, last), lambda h, i: (i, 0))
    full = pl.BlockSpec((1, tr, last), lambda h, i: (h, i, 0))
    res = pl.pallas_call(
        body, name=name, grid=(2, R // tr),
        in_specs=[pl.BlockSpec(memory_space=pltpu.SMEM), half, half, full, full, full], out_specs=[full] * 4,
        out_shape=[jax.ShapeDtypeStruct((2, R, last), f32)] * 4,
        compiler_params=pltpu.CompilerParams(dimension_semantics=("arbitrary", "arbitrary"), vmem_limit_bytes=VMEM_LIMIT),
    )(mc1, *g2, *w3)
    return tuple(r.reshape(shape) for r in res)


def kernel(x, mem, mem_ln_g, mem_ln_b, ln_g, ln_b, ffn1_w_gu, ffn1_w_down, w_in, dn_conv_w, dn_a_log, dn_dt_bias, dn_norm_w, swa_sinks, w_mem_kv, w_branch, w_out, ffn2_w_gu, ffn2_w_down, loss_target, m_mem_ln_g, m_mem_ln_b, m_ln_g, m_ln_b, m_ffn1_w_gu, m_ffn1_w_down, m_w_in, m_dn_conv_w, m_dn_a_log, m_dn_dt_bias, m_dn_norm_w, m_swa_sinks, m_w_mem_kv, m_w_branch, m_w_out, m_ffn2_w_gu, m_ffn2_w_down, v_mem_ln_g, v_mem_ln_b, v_ln_g, v_ln_b, v_ffn1_w_gu, v_ffn1_w_down, v_w_in, v_dn_conv_w, v_dn_a_log, v_dn_dt_bias, v_dn_norm_w, v_swa_sinks, v_w_mem_kv, v_w_branch, v_w_out, v_ffn2_w_gu, v_ffn2_w_down):
    args = dict(locals())
    Ws = {n: args[n] for n in WEIGHTS}
    Ms = {n: args["m_" + n] for n in WEIGHTS}
    Vs = {n: args["v_" + n] for n in WEIGHTS}
    mc = lax.axis_index("c")
    my_s = 2 * lax.axis_index("x") + lax.axis_index("y")
    small = [n for n in WEIGHTS if n not in MATRICES]

    ag_table, ag_rows = _layout(SMALL_SHARDED, {n: Ws[n].size for n in SMALL_SHARDED}, 16)
    ag_small = _pack(ag_table, ag_rows, {n: Ws[n].reshape(-1) for n in SMALL_SHARDED}).reshape(2, ag_rows // 2, LANES)
    local = [Ws[n].astype(bf16) for n in MATRICES] + [ag_small]
    gathered = _all_gather("all_gather_w", local)
    gathered = [jnp.stack([jnp.where(my_s == s, loc, g[s]) for s in range(N_SHARD)], axis=0) for loc, g in zip(local, gathered)]
    Wf = {n: _join_shards(g, SHARD_AXIS[n]) for n, g in zip(MATRICES, gathered)}
    g_small = gathered[-1].reshape(N_SHARD, ag_rows, LANES)
    for n in SMALL_SHARDED:
        Wf[n] = _join_shards(_unpack(ag_table, g_small, n).reshape((N_SHARD,) + Ws[n].shape), SHARD_AXIS[n])
    for n in WEIGHTS:
        if SHARD_AXIS[n] is None:
            Wf[n] = Ws[n]

    loss_row, dx, G = _local_step(x[0], mem[0], loss_target[0], Wf)

    table, rows = _layout(small + ['loss'], {**{n: Ws[n].size for n in small}, 'loss': 1}, 16)
    gflat = {n: (jnp.broadcast_to(G[n].reshape(1, -1), (N_SHARD, G[n].size)) if SHARD_AXIS[n] is None
                 else _split_shards(G[n], SHARD_AXIS[n]).reshape(N_SHARD, -1)) for n in small}
    gflat['loss'] = jnp.broadcast_to(loss_row[:, :1], (N_SHARD, 1))
    items = [_split_shards(G[n], SHARD_AXIS[n]) for n in MATRICES] + [_pack(table, rows, gflat).reshape(N_SHARD, 2, rows // 2, LANES)]
    tags = MATRICES + ['small']
    got = _pair_exchange("rs_pair", items)
    keep = [lax.dynamic_index_in_dim(a, mc, axis=1, keepdims=False) for a in items]
    wire = [bf16] * len(MATRICES) + [f32]
    part = [_ew_call(f"rs_add_pair_{t}", lambda a, b: a + b, [k, g], 1, out_dtype=dt)[0] for t, k, g, dt in zip(tags, keep, got, wire)]
    others = _chip_exchange("rs_chips", part)
    own = lambda a: lax.dynamic_index_in_dim(a, my_s, axis=0, keepdims=False)
    red = [_ew_call(f"rs_add_chips_{t}", lambda k, g, fx, fy, fxy: ((k + g) + fy) + (fx + fxy), [own(k), own(g), o[0], o[1], o[2]], 1)[0]
           for t, k, g, o in zip(tags, keep, got, others)]
    theirs = _pair_swap("rs_swap", red)

    mc1 = mc.astype(i32).reshape(1)
    outs = {}
    for n, a, b in zip(MATRICES, red, theirs):
        outs[n] = _adamw_call(f"adamw_{n}", a, b, Ws[n], Ms[n], Vs[n], mc1)
    fill = {'loss': jnp.zeros((1,), f32)}
    packs = [_pack(table, rows, {**{n: src[n].reshape(-1) for n in small}, **fill}).reshape(2, rows // 2, LANES) for src in (Ws, Ms, Vs)]
    small_out = [p.reshape(rows, LANES) for p in _adamw_call("adamw_small", red[-1], theirs[-1], *packs, mc1)]
    for n in small:
        outs[n] = tuple(_unpack(table, p, n).reshape(Ws[n].shape) for p in small_out)
    loss = _unpack(table, small_out[0], 'loss').reshape(())
    return (loss, dx[None], *[outs[n][k] for k in range(4) for n in WEIGHTS])
```

```python
import functools

import jax
import jax.numpy as jnp
from jax import lax
from jax.experimental import pallas as pl
from jax.experimental.pallas import tpu as pltpu

f32, bf16, i32 = jnp.float32, jnp.bfloat16, jnp.int32
HI = lax.Precision.HIGHEST
MESH = pl.DeviceIdType.MESH

D_MODEL = 1024
DEPTH = 2
DN_HEADS, DN_DK, DN_CONV, DN_CHUNK = 4, 128, 4, 64
SWA_HEADS, SWA_KV_HEADS, SWA_DH, WINDOW = 8, 2, 64, 128
XA_HEADS, XA_DH = 4, 128
D_FF = 2816
N_BRANCH, BRANCH_W = 3, 512
ALPHA = (2 * DEPTH) ** 0.25
LN_EPS, RMS_EPS, NEG_INF = 1e-5, 1e-6, -1e30
D_IN = 6408
D_INP = 6656
LR, B1, B2, EPS, WD, STEP = 0.001, 0.9, 0.999, 1e-08, 0.01, 10

LANES = 128
VMEM_LIMIT = 56 << 20
N_SHARD = 4
SCAN_CHUNKS = 4

WEIGHTS = ['mem_ln_g', 'mem_ln_b', 'ln_g', 'ln_b', 'ffn1_w_gu', 'ffn1_w_down', 'w_in', 'dn_conv_w', 'dn_a_log',
           'dn_dt_bias', 'dn_norm_w', 'swa_sinks', 'w_mem_kv', 'w_branch', 'w_out', 'ffn2_w_gu', 'ffn2_w_down']
SHARD_AXIS = {'mem_ln_g': None, 'mem_ln_b': None, 'ln_g': 2, 'ln_b': 2, 'ffn1_w_gu': 2, 'ffn1_w_down': 1, 'w_in': 2,
              'dn_conv_w': 2, 'dn_a_log': None, 'dn_dt_bias': None, 'dn_norm_w': None, 'swa_sinks': None,
              'w_mem_kv': 1, 'w_branch': 3, 'w_out': 1, 'ffn2_w_gu': 2, 'ffn2_w_down': 1}
MATRICES = ['ffn1_w_gu', 'ffn1_w_down', 'w_in', 'w_mem_kv', 'w_branch', 'w_out', 'ffn2_w_gu', 'ffn2_w_down']
SMALL_SHARDED = ['ln_g', 'ln_b', 'dn_conv_w']


def _dg(a, b, mode, hi):
    nb = a.ndim - 2
    bd = tuple(range(nb))
    ca = nb if mode == 'tn' else nb + 1
    cb = nb + 1 if mode == 'nt' else nb
    dims = (((ca,), (cb,)), (bd, bd))
    dot = lambda x, y: lax.dot_general(x, y, dims, preferred_element_type=f32)
    a_hi, b_hi = a.astype(bf16), b.astype(bf16)
    if not hi:
        return dot(a_hi, b_hi)
    a_lo = (a.astype(f32) - a_hi.astype(f32)).astype(bf16)
    b_lo = (b.astype(f32) - b_hi.astype(f32)).astype(bf16)
    return dot(a_hi, b_hi) + (dot(a_hi, b_lo) + dot(a_lo, b_hi))


@functools.partial(jax.custom_vjp, nondiff_argnums=(2, 3))
def _dot(a, b, mode, hi):
    return _dg(a, b, mode, hi)


def _dot_fwd(a, b, mode, hi):
    return _dg(a, b, mode, hi), (a, b)


def _dot_bwd(mode, hi, res, g):
    a, b = res
    if mode == 'nn':
        da, db = _dg(g, b, 'nt', hi), _dg(a, g, 'tn', hi)
    elif mode == 'nt':
        da, db = _dg(g, b, 'nn', hi), _dg(g, a, 'tn', hi)
    else:
        da, db = _dg(b, g, 'nt', hi), _dg(a, g, 'nn', hi)
    return da.astype(a.dtype), db.astype(b.dtype)


_dot.defvjp(_dot_fwd, _dot_bwd)


def bdot(a, b, mode):
    return _dot(a, b, mode, False)


def hdot(a, b, mode):
    return _dot(a, b, mode, True)


def _shift_rows_impl(x, xprev, s):
    rows = lax.broadcasted_iota(i32, x.shape, 0)
    return jnp.where(rows >= s, pltpu.roll(x, s, 0), pltpu.roll(xprev, s, 0))


@functools.partial(jax.custom_vjp, nondiff_argnums=(2,))
def _shift_rows(x, xprev, s):
    return _shift_rows_impl(x, xprev, s)


def _shift_rows_fwd(x, xprev, s):
    return _shift_rows_impl(x, xprev, s), None


def _shift_rows_bwd(s, _, g):
    n = g.shape[0]
    rows = lax.broadcasted_iota(i32, g.shape, 0)
    r = pltpu.roll(g, n - s, 0)
    keep = rows < n - s
    return jnp.where(keep, r, 0.0), jnp.where(keep, 0.0, r)


_shift_rows.defvjp(_shift_rows_fwd, _shift_rows_bwd)


def _lane_pick(x, lane):
    idx = lax.broadcasted_iota(i32, x.shape, x.ndim - 1)
    return jnp.sum(jnp.where(idx == lane, x, 0.0), axis=-1, keepdims=True)


def _silu(x):
    return x * jax.nn.sigmoid(x)


def _tri_inv(a):
    C = a.shape[-1]
    eye = (lax.broadcasted_iota(i32, a.shape, 1) == lax.broadcasted_iota(i32, a.shape, 2)).astype(f32)
    p = -a
    x = eye + p
    for _ in range((C - 1).bit_length() - 1):
        p = _dg(p, p, 'nn', True)
        x = x + _dg(x, p, 'nn', True)
    return x


@jax.custom_vjp
def _tri_solve(a, rhs):
    return _dg(_tri_inv(a), rhs, 'nn', True)


def _tri_solve_fwd(a, rhs):
    tinv = _tri_inv(a)
    sol = _dg(tinv, rhs, 'nn', True)
    return sol, (tinv, sol)


def _tri_solve_bwd(res, g):
    tinv, sol = res
    d_rhs = _dg(tinv, g, 'tn', True)
    return -_dg(d_rhs, sol, 'nt', True), d_rhs


_tri_solve.defvjp(_tri_solve_fwd, _tri_solve_bwd)


@functools.partial(jax.custom_vjp, nondiff_argnums=(1,))
def _lane_head(x, n):
    return x[:, :, :n]


def _lane_head_fwd(x, n):
    return x[:, :, :n], None


def _lane_head_bwd(n, _, g):
    s = jnp.sum(g, axis=-1, keepdims=True) * (1.0 / LANES)
    return (jnp.broadcast_to(s, g.shape[:-1] + (LANES,)),)


_lane_head.defvjp(_lane_head_fwd, _lane_head_bwd)


@functools.partial(jax.custom_vjp, nondiff_argnums=(1,))
def _last_row(x, c):
    return x[:, c - 1:, :]


def _last_row_fwd(x, c):
    return x[:, c - 1:, :], None


def _last_row_bwd(c, _, g):
    shape = (g.shape[0], c, g.shape[2])
    rows = lax.broadcasted_iota(i32, shape, 1)
    return (jnp.where(rows == c - 1, jnp.broadcast_to(g, shape), 0.0),)


_last_row.defvjp(_last_row_fwd, _last_row_bwd)


def _full(a):
    nd = a.ndim
    return (a, tuple(a.shape), lambda i, _nd=nd: (0,) * _nd)


def _rows(a, tm, col=0, width=None):
    width = a.shape[1] if width is None else width
    return (a, (tm, width), lambda i, _c=col: (i, _c))


def _tile_call(name, fn, n, ins, outs, accs=()):
    n_in, n_out, n_acc = len(ins), len(outs), len(accs)

    def body(*refs):
        i = pl.program_id(0)
        res = fn(i, *[r[...].astype(f32) for r in refs[:n_in]])
        if not isinstance(res, (tuple, list)):
            res = (res,)
        assert len(res) == n_out + n_acc, (name, len(res), n_out, n_acc)
        for r, v in zip(refs[n_in:n_in + n_out], res[:n_out]):
            r[...] = v.astype(r.dtype)
        if n_acc:
            acc_refs = refs[n_in + n_out:]

            @pl.when(i == 0)
            def _():
                for r in acc_refs:
                    r[...] = jnp.zeros(r.shape, r.dtype)

            for r, v in zip(acc_refs, res[n_out:]):
                r[...] += v.astype(r.dtype)

    out_shape = [jax.ShapeDtypeStruct(s, d) for s, d, _, _ in outs] + [jax.ShapeDtypeStruct(s, d) for s, d in accs]
    out_specs = [pl.BlockSpec(b, m) for _, _, b, m in outs]
    out_specs += [pl.BlockSpec(tuple(s), lambda i, _nd=len(s): (0,) * _nd) for s, _ in accs]
    res = pl.pallas_call(
        body, name=name, grid=(n,),
        in_specs=[pl.BlockSpec(b, m) for _, b, m in ins],
        out_specs=out_specs, out_shape=out_shape,
        compiler_params=pltpu.CompilerParams(dimension_semantics=("arbitrary",), vmem_limit_bytes=VMEM_LIMIT),
    )(*[a for a, _, _ in ins])
    return res


def _pick(n, cands):
    for c in cands:
        if n % c == 0:
            return c
    return n


def _mm(name, a, b, mode, out_dtype=f32, add=None, out_cols=None):
    if mode == 'tn':
        K, M = a.shape
    else:
        M, K = a.shape
    N = b.shape[0] if mode == 'nt' else b.shape[1]
    tm = _pick(M, (1024, 1408, 512, 256, 128))
    tn = _pick(N, (512, 1408, 256, 128)) if out_cols is None else out_cols[0]
    col = (lambda j: j) if out_cols is None else out_cols[1]
    tk = K if K <= 3328 else _pick(K, (3328, 2816, 2048, 1024, 512, 256, 128))
    nk = K // tk
    ca = 0 if mode == 'tn' else 1
    cb = 1 if mode == 'nt' else 0
    dims = (((ca,), (cb,)), ((), ()))

    def body(*refs):
        a_ref, b_ref = refs[0], refs[1]
        add_ref = refs[2] if add is not None else None
        part = lax.dot_general(a_ref[...].astype(bf16), b_ref[...].astype(bf16), dims, preferred_element_type=f32)

        def finish(r, o_ref):
            if add_ref is not None:
                r = r + add_ref[...].astype(f32)
            o_ref[...] = r.astype(o_ref.dtype)

        if nk == 1:
            finish(part, refs[-1])
            return
        o_ref, acc_ref = refs[-2], refs[-1]
        k = pl.program_id(2)

        @pl.when(k == 0)
        def _():
            acc_ref[...] = part

        @pl.when(k > 0)
        def _():
            acc_ref[...] += part

        @pl.when(k == nk - 1)
        def _():
            finish(acc_ref[...], o_ref)

    a_spec = pl.BlockSpec((tk, tm), lambda i, j, k: (k, i)) if mode == 'tn' else pl.BlockSpec((tm, tk), lambda i, j, k: (i, k))
    b_spec = pl.BlockSpec((tn, tk), lambda i, j, k: (j, k)) if mode == 'nt' else pl.BlockSpec((tk, tn), lambda i, j, k: (k, j))
    in_specs, args = [a_spec, b_spec], [a, b]
    if add is not None:
        in_specs.append(pl.BlockSpec((tm, tn), lambda i, j, k: (i, j)))
        args.append(add)
    return pl.pallas_call(
        body, name=name, grid=(M // tm, N // tn, nk), in_specs=in_specs,
        out_specs=pl.BlockSpec((tm, tn), lambda i, j, k: (i, col(j))),
        out_shape=jax.ShapeDtypeStruct((M, N), out_dtype),
        scratch_shapes=[pltpu.VMEM((tm, tn), f32)] if nk > 1 else [],
        compiler_params=pltpu.CompilerParams(dimension_semantics=("parallel", "parallel", "arbitrary"), vmem_limit_bytes=VMEM_LIMIT),
    )(*args)


def _swiglu(u):
    return _silu(u[:, :D_FF]) * u[:, D_FF:]


def _res_ln(x, f, g, b, s):
    r = ALPHA * x + s * f
    mu = jnp.mean(r, axis=-1, keepdims=True)
    rc = r - mu
    var = jnp.mean(rc * rc, axis=-1, keepdims=True)
    return rc * lax.rsqrt(var + LN_EPS) * g + b


def _ln(x, g, b):
    mu = jnp.mean(x, axis=-1, keepdims=True)
    xc = x - mu
    var = jnp.mean(xc * xc, axis=-1, keepdims=True)
    return xc * lax.rsqrt(var + LN_EPS) * g + b


def _dn_pre(first, xc, xp, ba, cw, hp):
    xp = jnp.where(first, 0.0, xp)
    y = cw[DN_CONV - 1:DN_CONV, :] * xc
    for j in range(DN_CONV - 1):
        y = y + cw[j:j + 1, :] * _shift_rows(xc, xp, DN_CONV - 1 - j)
    c = _silu(y)
    qs, ks, vs, gs, bs = [], [], [], [], []
    nqk = DN_HEADS * DN_DK
    for h in range(DN_HEADS):
        q = c[:, h * DN_DK:(h + 1) * DN_DK]
        k = c[:, nqk + h * DN_DK:nqk + (h + 1) * DN_DK]
        v = c[:, 2 * nqk + h * DN_DK:2 * nqk + (h + 1) * DN_DK]
        qs.append(q * lax.rsqrt(jnp.sum(q * q, axis=-1, keepdims=True) + RMS_EPS))
        ks.append(k * lax.rsqrt(jnp.sum(k * k, axis=-1, keepdims=True) + RMS_EPS))
        vs.append(v)
        beta = jax.nn.sigmoid(_lane_pick(ba, h))
        a_log = _lane_pick(hp[0:1, :], h)
        dt = _lane_pick(hp[1:2, :], h)
        g = -jnp.exp(a_log) * jax.nn.softplus(_lane_pick(ba, DN_HEADS + h) + dt)
        gs.append(jnp.broadcast_to(g, q.shape))
        bs.append(jnp.broadcast_to(beta, q.shape))
    return tuple(jnp.stack(t, axis=0) for t in (qs, ks, vs, gs, bs))


def _wy(q, k, v, gb, bb):
    B, C, _ = q.shape
    ri = lax.broadcasted_iota(i32, (B, C, C), 1)
    ci = lax.broadcasted_iota(i32, (B, C, C), 2)
    tril, strict = ri >= ci, ri > ci
    gc = hdot(tril.astype(f32), gb, 'nn')
    gl = jnp.broadcast_to(_last_row(gc, C), gc.shape)
    col = _lane_head(gc, C)
    decay = jnp.exp(jnp.where(tril, col - jnp.swapaxes(col, 1, 2), NEG_INF))
    qs = q * (DN_DK ** -0.5)
    kb = k * bb
    a = jnp.where(strict, bdot(kb, k, 'nt') * decay, 0.0)
    sol = _tri_solve(a, jnp.concatenate([v * bb, kb * jnp.exp(gc)], axis=-1))
    qk = jnp.where(tril, bdot(qs, k, 'nt') * decay, 0.0)
    kt = k * jnp.exp(gl - gc)
    qd = qs * jnp.exp(gc)
    return qd, qk, sol[..., :DN_DK], sol[..., DN_DK:], kt, jnp.exp(gl)


def _scan_step(S, qd, qk, u, w, kt, egl):
    vn = u - bdot(w, S, 'nn')
    o = bdot(qd, S, 'nn') + bdot(qk, vn, 'nn')
    e2 = jnp.concatenate([egl] * (DN_DK // DN_CHUNK), axis=1)
    return o, S * e2 + bdot(kt, vn, 'tn')


def _dn_post(o, z, nw):
    outs = []
    for h in range(DN_HEADS):
        oh = o[h]
        oh = oh * lax.rsqrt(jnp.mean(oh * oh, axis=-1, keepdims=True) + RMS_EPS) * nw
        outs.append(oh * _silu(z[:, h * DN_DK:(h + 1) * DN_DK]))
    return jnp.concatenate(outs, axis=1)


def _swa(first, q, kc, kp, vc, vp, snk):
    W = q.shape[1]
    G = SWA_HEADS // SWA_KV_HEADS
    r = lax.broadcasted_iota(i32, (G, W, 2 * W), 1)
    c = lax.broadcasted_iota(i32, (G, W, 2 * W), 2)
    mask = (c > W + r - WINDOW) & (c <= W + r) & jnp.logical_or(c >= W, jnp.logical_not(first))
    sink_all = _lane_pick(snk, 0)
    outs = []
    for j in range(SWA_KV_HEADS):
        qj = q[j * G:(j + 1) * G]
        kk = jnp.broadcast_to(jnp.concatenate([kp[j], kc[j]], axis=0)[None], (G, 2 * W, SWA_DH))
        vv = jnp.broadcast_to(jnp.concatenate([vp[j], vc[j]], axis=0)[None], (G, 2 * W, SWA_DH))
        s = jnp.where(mask, bdot(qj, kk, 'nt') * (SWA_DH ** -0.5), NEG_INF)
        sink = sink_all[j * G:(j + 1) * G][:, :, None]
        m = jnp.maximum(jnp.max(s, axis=-1, keepdims=True), sink)
        p = jnp.exp(s - m)
        p = p / (jnp.sum(p, axis=-1, keepdims=True) + jnp.exp(sink - m))
        outs.append(bdot(p, vv, 'nn'))
    return jnp.concatenate(outs, axis=0)


def _xa(q, kv):
    outs = []
    nk = XA_HEADS * XA_DH
    for h in range(XA_HEADS):
        qh = q[:, h * XA_DH:(h + 1) * XA_DH]
        kh = kv[:, h * XA_DH:(h + 1) * XA_DH]
        vh = kv[:, nk + h * XA_DH:nk + (h + 1) * XA_DH]
        s = bdot(qh, kh, 'nt') * (XA_DH ** -0.5)
        m = jnp.max(s, axis=-1, keepdims=True)
        p = jnp.exp(s - m)
        p = p / jnp.sum(p, axis=-1, keepdims=True)
        outs.append(bdot(p, vh, 'nn'))
    return jnp.concatenate(outs, axis=1)


def _merge(o_dn, o_sw, o_xa, gates, wb):
    acc = None
    for n, o in enumerate((o_dn, o_sw, o_xa)):
        t = jax.nn.sigmoid(gates[:, n * D_MODEL:(n + 1) * D_MODEL]) * bdot(o, wb[n], 'nn')
        acc = t if acc is None else acc + t
    return acc


def _row_tile(T, want):
    return _pick(T, tuple(c for c in (1024, 512, 256, 128, 64) if c <= want))


def _res_ln_fwd(name, x, f, g, b, s):
    T = x.shape[0]
    tm = _row_tile(T, 512)

    def fn(i, x, f, g, b):
        h = _res_ln(x, f, g, b, s)
        return h, h

    return _tile_call(name, fn, T // tm, [_rows(x, tm), _rows(f, tm), _full(g), _full(b)],
                      [((T, D_MODEL), f32, (tm, D_MODEL), lambda i: (i, 0)), ((T, D_MODEL), bf16, (tm, D_MODEL), lambda i: (i, 0))])


def _res_ln_bwd(name, x, f, g, b, s, dh):
    T = x.shape[0]
    tm = _row_tile(T, 512)

    def fn(i, x, f, g, b, dh):
        _, vjp = jax.vjp(lambda x, f, g, b: _res_ln(x, f, g, b, s), x, f, g, b)
        return vjp(dh)

    return _tile_call(name, fn, T // tm, [_rows(x, tm), _rows(f, tm), _full(g), _full(b), _rows(dh, tm)],
                      [((T, D_MODEL), f32, (tm, D_MODEL), lambda i: (i, 0)), ((T, D_MODEL), bf16, (tm, D_MODEL), lambda i: (i, 0))],
                      [((1, D_MODEL), f32), ((1, D_MODEL), f32)])


FF_BLK = 256


def _pair_gate_up(w):
    d = w.shape[0]
    return jnp.stack([w[:, :D_FF].reshape(d, -1, FF_BLK), w[:, D_FF:].reshape(d, -1, FF_BLK)], axis=2).reshape(d, 2 * D_FF)


def _swiglu_blk(u):
    return _silu(u[:, :FF_BLK]) * u[:, FF_BLK:]


def _mm_spec(tm, tn):
    return dict(compiler_params=pltpu.CompilerParams(dimension_semantics=("parallel", "arbitrary"), vmem_limit_bytes=VMEM_LIMIT))


def _gu_act(name, xb, w_pair):
    T, D = xb.shape
    tm = _pick(T, (1024, 512, 256, 128))

    def body(x_ref, w_ref, u_ref, a_ref):
        r = lax.dot_general(x_ref[...], w_ref[...], (((1,), (0,)), ((), ())), preferred_element_type=f32)
        u_ref[...] = r.astype(bf16)
        a_ref[...] = _swiglu_blk(r).astype(bf16)

    return pl.pallas_call(
        body, name=name, grid=(T // tm, D_FF // FF_BLK),
        in_specs=[pl.BlockSpec((tm, D), lambda i, j: (i, 0)), pl.BlockSpec((D, 2 * FF_BLK), lambda i, j: (0, j))],
        out_specs=[pl.BlockSpec((tm, 2 * FF_BLK), lambda i, j: (i, j)), pl.BlockSpec((tm, FF_BLK), lambda i, j: (i, j))],
        out_shape=[jax.ShapeDtypeStruct((T, 2 * D_FF), bf16), jax.ShapeDtypeStruct((T, D_FF), bf16)],
        **_mm_spec(tm, FF_BLK))(xb, w_pair)


def _down_dx_act_bwd(name, df, w_down_t, u):
    T, D = df.shape
    tm = _pick(T, (1024, 512, 256, 128))

    def body(df_ref, w_ref, u_ref, du_ref):
        da = lax.dot_general(df_ref[...], w_ref[...], (((1,), (0,)), ((), ())), preferred_element_type=f32)
        _, vjp = jax.vjp(_swiglu_blk, u_ref[...].astype(f32))
        du_ref[...] = vjp(da)[0].astype(bf16)

    return pl.pallas_call(
        body, name=name, grid=(T // tm, D_FF // FF_BLK),
        in_specs=[pl.BlockSpec((tm, D), lambda i, j: (i, 0)), pl.BlockSpec((D, FF_BLK), lambda i, j: (0, j)),
                  pl.BlockSpec((tm, 2 * FF_BLK), lambda i, j: (i, j))],
        out_specs=pl.BlockSpec((tm, 2 * FF_BLK), lambda i, j: (i, j)),
        out_shape=jax.ShapeDtypeStruct((T, 2 * D_FF), bf16),
        **_mm_spec(tm, FF_BLK))(df, w_down_t, u)


def _mm_res_ln(name, a, w, x, g, b, s):
    T, K = a.shape
    tm = _pick(T, (512, 256, 128))

    def body(a_ref, w_ref, x_ref, g_ref, b_ref, f_ref, h_ref, hb_ref):
        f = lax.dot_general(a_ref[...], w_ref[...], (((1,), (0,)), ((), ())), preferred_element_type=f32)
        h = _res_ln(x_ref[...], f, g_ref[...], b_ref[...], s)
        f_ref[...] = f
        h_ref[...] = h
        hb_ref[...] = h.astype(bf16)

    row = pl.BlockSpec((tm, D_MODEL), lambda i: (i, 0))
    vec = pl.BlockSpec((1, D_MODEL), lambda i: (0, 0))
    return pl.pallas_call(
        body, name=name, grid=(T // tm,),
        in_specs=[pl.BlockSpec((tm, K), lambda i: (i, 0)), pl.BlockSpec((K, D_MODEL), lambda i: (0, 0)), row, vec, vec],
        out_specs=[row, row, row],
        out_shape=[jax.ShapeDtypeStruct((T, D_MODEL), f32), jax.ShapeDtypeStruct((T, D_MODEL), f32), jax.ShapeDtypeStruct((T, D_MODEL), bf16)],
        compiler_params=pltpu.CompilerParams(dimension_semantics=("parallel",), vmem_limit_bytes=VMEM_LIMIT))(a, w, x, g, b)


def _hm_spec(tm, w=DN_DK):
    return ((DN_HEADS, tm, w), lambda i: (0, i, 0))


def _prev(i):
    return jnp.maximum(i - 1, 0)


def _dn_pre_fwd(name, p, pba, cw, hp):
    T = p.shape[0]
    tm = _row_tile(T, 256)
    n = T // tm
    W3 = 3 * DN_HEADS * DN_DK
    ins = [(p, (tm, W3), lambda i: (i, 0)), (p, (tm, W3), lambda i: (_prev(i), 0)),
           (pba, (tm, LANES), lambda i: (i, 0)), _full(cw), _full(hp)]
    blk, im = _hm_spec(tm)
    return _tile_call(name, lambda i, *a: _dn_pre(i == 0, *a), n, ins, [((DN_HEADS, T, DN_DK), f32, blk, im)] * 5)


def _dn_pre_bwd(name, p, pba, cw, hp, cts):
    T = p.shape[0]
    tm = _row_tile(T, 256)
    n = T // tm
    W3 = 3 * DN_HEADS * DN_DK
    blk, im = _hm_spec(tm)
    ins = [(p, (tm, W3), lambda i: (i, 0)), (p, (tm, W3), lambda i: (_prev(i), 0)),
           (pba, (tm, LANES), lambda i: (i, 0)), _full(cw), _full(hp)] + [(c, blk, im) for c in cts]

    def fn(i, xc, xp, ba, cw, hp, *cts):
        _, vjp = jax.vjp(lambda *a: _dn_pre(i == 0, *a), xc, xp, ba, cw, hp)
        return vjp(tuple(cts))

    return _tile_call(name, fn, n, ins,
                      [((T, W3), f32, (tm, W3), lambda i: (i, 0)), ((T, W3), f32, (tm, W3), lambda i: (i, 0)),
                       ((T, LANES), f32, (tm, LANES), lambda i: (i, 0))],
                      [(tuple(cw.shape), f32), (tuple(hp.shape), f32)])


def _wy_split(x, nc):
    return x.reshape(DN_HEADS * nc, DN_CHUNK, x.shape[-1])


def _wy_fwd(name, q, k, v, gb, bb):
    T = q.shape[1]
    tm = _row_tile(T, 256)
    nc = tm // DN_CHUNK

    def fn(i, *a):
        outs = _wy(*[_wy_split(t, nc) for t in a])
        return [o.reshape(DN_HEADS, tm, o.shape[-1]) for o in outs]

    blk, im = _hm_spec(tm)
    blk2, im2 = _hm_spec(tm, DN_CHUNK)
    ins = [(t, blk, im) for t in (q, k, v, gb, bb)]
    full = (DN_HEADS, T, DN_DK)
    outs = [(full, bf16, blk, im), ((DN_HEADS, T, DN_CHUNK), bf16, blk2, im2), (full, f32, blk, im),
            (full, bf16, blk, im), (full, bf16, blk, im), (full, f32, blk, im)]
    return _tile_call(name, fn, T // tm, ins, outs)


def _wy_bwd(name, q, k, v, gb, bb, cts):
    T = q.shape[1]
    tm = _row_tile(T, 256)
    nc = tm // DN_CHUNK

    def fn(i, q, k, v, gb, bb, *cts):
        _, vjp = jax.vjp(_wy, *[_wy_split(t, nc) for t in (q, k, v, gb, bb)])
        grads = vjp(tuple(_wy_split(c.astype(f32), nc) for c in cts))
        return [g.reshape(DN_HEADS, tm, DN_DK) for g in grads]

    blk, im = _hm_spec(tm)
    blk2, im2 = _hm_spec(tm, DN_CHUNK)
    ins = [(t, blk, im) for t in (q, k, v, gb, bb)]
    ins += [(c, blk2 if c.shape[-1] == DN_CHUNK else blk, im) for c in cts]
    return _tile_call(name, fn, T // tm, ins, [((DN_HEADS, T, DN_DK), f32, blk, im)] * 5)


def _scan_fwd(name, qd, qk, u, w, kt, egl):
    T = qd.shape[1]
    C = DN_CHUNK
    U = _pick(T // C, (SCAN_CHUNKS, 1))
    n = T // (C * U)

    def body(qd_ref, qk_ref, u_ref, w_ref, kt_ref, egl_ref, o_ref, sall_ref, s_ref):
        i = pl.program_id(0)

        @pl.when(i == 0)
        def _():
            s_ref[...] = jnp.zeros(s_ref.shape, f32)

        S = s_ref[...]
        for j in range(U):
            sl = pl.ds(j * C, C)
            sall_ref[:, j] = S
            o, S = _scan_step(S, *[r[:, sl, :] for r in (qd_ref, qk_ref, u_ref, w_ref, kt_ref, egl_ref)])
            o_ref[:, sl, :] = o
        s_ref[...] = S

    blk, im = _hm_spec(C * U)
    blk2, im2 = _hm_spec(C * U, C)
    return pl.pallas_call(
        body, name=name, grid=(n,),
        in_specs=[pl.BlockSpec(blk, im), pl.BlockSpec(blk2, im2)] + [pl.BlockSpec(blk, im)] * 4,
        out_specs=[pl.BlockSpec(blk, im), pl.BlockSpec((DN_HEADS, U, DN_DK, DN_DK), lambda i: (0, i, 0, 0))],
        out_shape=[jax.ShapeDtypeStruct((DN_HEADS, T, DN_DK), f32), jax.ShapeDtypeStruct((DN_HEADS, n * U, DN_DK, DN_DK), f32)],
        scratch_shapes=[pltpu.VMEM((DN_HEADS, DN_DK, DN_DK), f32)],
        compiler_params=pltpu.CompilerParams(dimension_semantics=("arbitrary",), vmem_limit_bytes=VMEM_LIMIT),
    )(qd, qk, u, w, kt, egl)


def _scan_bwd(name, qd, qk, u, w, kt, egl, s_all, do):
    T = qd.shape[1]
    C = DN_CHUNK
    U = _pick(T // C, (SCAN_CHUNKS, 1))
    n = T // (C * U)

    def body(qd_ref, qk_ref, u_ref, w_ref, kt_ref, egl_ref, sall_ref, do_ref,
             dqd_ref, dqk_ref, du_ref, dw_ref, dkt_ref, degl_ref, ds_ref):
        i = pl.program_id(0)

        @pl.when(i == 0)
        def _():
            ds_ref[...] = jnp.zeros(ds_ref.shape, f32)

        dS = ds_ref[...]
        for j in reversed(range(U)):
            sl = pl.ds(j * C, C)
            args = [r[:, sl, :].astype(f32) for r in (qd_ref, qk_ref, u_ref, w_ref, kt_ref, egl_ref)]
            _, vjp = jax.vjp(_scan_step, sall_ref[:, j], *args)
            dS, *cts = vjp((do_ref[:, sl, :], dS))
            for r, v in zip((dqd_ref, dqk_ref, du_ref, dw_ref, dkt_ref, degl_ref), cts):
                r[:, sl, :] = v
        ds_ref[...] = dS

    blk = (DN_HEADS, C * U, DN_DK)
    blk2 = (DN_HEADS, C * U, C)
    rim = lambda i: (0, n - 1 - i, 0)
    sp, sp2 = pl.BlockSpec(blk, rim), pl.BlockSpec(blk2, rim)
    full, full2 = jax.ShapeDtypeStruct((DN_HEADS, T, DN_DK), f32), jax.ShapeDtypeStruct((DN_HEADS, T, C), f32)
    return pl.pallas_call(
        body, name=name, grid=(n,),
        in_specs=[sp, sp2, sp, sp, sp, sp, pl.BlockSpec((DN_HEADS, U, DN_DK, DN_DK), lambda i: (0, n - 1 - i, 0, 0)), sp],
        out_specs=[sp, sp2, sp, sp, sp, sp],
        out_shape=[full, full2, full, full, full, full],
        scratch_shapes=[pltpu.VMEM((DN_HEADS, DN_DK, DN_DK), f32)],
        compiler_params=pltpu.CompilerParams(dimension_semantics=("arbitrary",), vmem_limit_bytes=VMEM_LIMIT),
    )(qd, qk, u, w, kt, egl, s_all, do)


def _dn_post_fwd(name, o, p, nw):
    T = p.shape[0]
    tm = _row_tile(T, 512)
    blk, im = _hm_spec(tm)
    W = DN_HEADS * DN_DK
    return _tile_call(name, lambda i, o, z, nw: _dn_post(o, z, nw), T // tm,
                      [(o, blk, im), (p, (tm, W), lambda i: (i, 1536 // W)), _full(nw)],
                      [((T, W), f32, (tm, W), lambda i: (i, 0))])[0]


def _dn_post_bwd(name, o, p, nw, dout):
    T = p.shape[0]
    tm = _row_tile(T, 512)
    blk, im = _hm_spec(tm)
    W = DN_HEADS * DN_DK

    def fn(i, o, z, nw, dout):
        _, vjp = jax.vjp(_dn_post, o, z, nw)
        return vjp(dout)

    return _tile_call(name, fn, T // tm,
                      [(o, blk, im), (p, (tm, W), lambda i: (i, 1536 // W)), _full(nw), _rows(dout, tm)],
                      [((DN_HEADS, T, DN_DK), f32, blk, im), ((T, W), f32, (tm, W), lambda i: (i, 0))],
                      [((1, DN_DK), f32)])


def _swa_tile(T, big):
    return _pick(T, (2 * WINDOW, WINDOW)) if big else WINDOW


@functools.partial(jax.custom_vjp, nondiff_argnums=(1,))
def _heads_split(x, nh):
    return jnp.stack([x[:, SWA_DH * h:SWA_DH * (h + 1)] for h in range(nh)], axis=0)


@jax.custom_vjp
def _heads_merge(y):
    return jnp.concatenate([y[h] for h in range(y.shape[0])], axis=1)


_heads_split.defvjp(lambda x, nh: (_heads_split(x, nh), None), lambda nh, _, g: (_heads_merge(g),))
_heads_merge.defvjp(lambda y: (_heads_merge(y), None), lambda _, g: (_heads_split(g, g.shape[1] // SWA_DH),))


def _swa_rows(first, qp, kc, kp, vc, vp, snk):
    kv = [_heads_split(t, SWA_KV_HEADS) for t in (kc, kp, vc, vp)]
    return _heads_merge(_swa(first, _heads_split(qp, SWA_HEADS), *kv, snk))


def _swa_ins(p, snk, W):
    qw, kw = SWA_HEADS * SWA_DH, SWA_KV_HEADS * SWA_DH
    return [(p, (W, qw), lambda i: (i, 2048 // qw)),
            (p, (W, kw), lambda i: (i, 6144 // kw)), (p, (W, kw), lambda i: (_prev(i), 6144 // kw)),
            (p, (W, kw), lambda i: (i, 6272 // kw)), (p, (W, kw), lambda i: (_prev(i), 6272 // kw)), _full(snk)]


def _swa_fwd(name, p, snk):
    T = p.shape[0]
    W = _swa_tile(T, True)
    qw = SWA_HEADS * SWA_DH
    return _tile_call(name, lambda i, *a: _swa_rows(i == 0, *a), T // W, _swa_ins(p, snk, W),
                      [((T, qw), f32, (W, qw), lambda i: (i, 0))])[0]


def _swa_bwd(name, p, snk, do):
    T = p.shape[0]
    W = _swa_tile(T, False)
    qw, kw = SWA_HEADS * SWA_DH, SWA_KV_HEADS * SWA_DH

    def fn(i, qp, k_c, k_p, v_c, v_p, snk, do):
        _, vjp = jax.vjp(lambda *a: _swa_rows(i == 0, *a), qp, k_c, k_p, v_c, v_p, snk)
        return vjp(do)

    kv_out = ((T, kw), f32, (W, kw), lambda i: (i, 0))
    return _tile_call(name, fn, T // W, _swa_ins(p, snk, W) + [(do, (W, qw), lambda i: (i, 0))],
                      [((T, qw), bf16, (W, qw), lambda i: (i, 0)), kv_out, kv_out, kv_out, kv_out], [(tuple(snk.shape), f32)])


def _xa_fwd(name, p, kv):
    T = p.shape[0]
    tm = _row_tile(T, 256)
    W = XA_HEADS * XA_DH
    return _tile_call(name, lambda i, q, kv: _xa(q, kv), T // tm, [(p, (tm, W), lambda i: (i, 2560 // W)), _full(kv)],
                      [((T, W), f32, (tm, W), lambda i: (i, 0))])[0]


def _xa_bwd(name, p, kv, do):
    T = p.shape[0]
    tm = _row_tile(T, 256)
    W = XA_HEADS * XA_DH

    def fn(i, q, kv, do):
        _, vjp = jax.vjp(_xa, q, kv)
        return vjp(do)

    return _tile_call(name, fn, T // tm, [(p, (tm, W), lambda i: (i, 2560 // W)), _full(kv), _rows(do, tm)],
                      [((T, W), bf16, (tm, W), lambda i: (i, 0))], [(tuple(kv.shape), f32)])


def _merge_fwd(name, o_dn, o_sw, o_xa, p, wb):
    T = p.shape[0]
    tm = _row_tile(T, 256)
    GW = N_BRANCH * D_MODEL
    return _tile_call(name, lambda i, a, b, c, g, w: _merge(a, b, c, g, w.astype(f32)), T // tm,
                      [_rows(o_dn, tm), _rows(o_sw, tm), _rows(o_xa, tm), (p, (tm, GW), lambda i: (i, 1)), _full(wb)],
                      [((T, D_MODEL), bf16, (tm, D_MODEL), lambda i: (i, 0))])[0]


def _merge_bwd(name, o_dn, o_sw, o_xa, p, wb, dm):
    T = p.shape[0]
    tm = _row_tile(T, 256)
    GW = N_BRANCH * D_MODEL

    def fn(i, a, b, c, g, w, dm):
        _, vjp = jax.vjp(_merge, a, b, c, g, w.astype(f32))
        return vjp(dm)

    bo = ((T, BRANCH_W), f32, (tm, BRANCH_W), lambda i: (i, 0))
    return _tile_call(name, fn, T // tm,
                      [_rows(o_dn, tm), _rows(o_sw, tm), _rows(o_xa, tm), (p, (tm, GW), lambda i: (i, 1)), _full(wb), _rows(dm, tm)],
                      [bo, bo, bo, ((T, GW), bf16, (tm, GW), lambda i: (i, 0))], [(tuple(wb.shape), f32)])


def _assemble_dp(name, dxc, dxp, dz, dswq, dxaq, dgates, dkc, dkp, dvc, dvp, dba):
    T = dz.shape[0]
    tm = _row_tile(T, 256)
    n = T // tm
    r = tm // WINDOW
    nb = T // WINDOW

    def fn(i, dxc, dxp, dz, dswq, dxaq, dgates, dkc, dvc, dba, *nxt):
        dqkv = dxc + jnp.where(i == n - 1, 0.0, dxp)
        shifted = [jnp.concatenate([jnp.where(i * r + 1 + b <= nb - 1, blk, 0.0) for b, blk in enumerate(half)], axis=0)
                   for half in (nxt[:r], nxt[r:])]
        parts = [dqkv, dz, dswq, dxaq, dgates, dkc + shifted[0], dvc + shifted[1], dba, jnp.zeros((tm, LANES), f32)]
        return jnp.concatenate(parts, axis=1)

    ins = [_rows(dxc, tm), (dxp, (tm, dxp.shape[1]), lambda i: (jnp.minimum(i + 1, n - 1), 0))]
    ins += [_rows(t, tm) for t in (dz, dswq, dxaq, dgates, dkc, dvc, dba)]
    for t in (dkp, dvp):
        ins += [(t, (WINDOW, t.shape[1]), lambda i, b=b: (jnp.minimum(i * r + 1 + b, nb - 1), 0)) for b in range(r)]
    return _tile_call(name, fn, n, ins, [((T, D_INP), bf16, (tm, D_INP), lambda i: (i, 0))])[0]


def _loss_grad(name, y, tgt):
    T = y.shape[0]
    tm = _row_tile(T, 512)

    def fn(i, y, t):
        e = y - t
        part = 0.5 * jnp.sum(jnp.mean(e * e, axis=-1, keepdims=True), axis=0, keepdims=True)
        return e * (1.0 / D_MODEL), jnp.broadcast_to(part, (1, LANES))

    return _tile_call(name, fn, T // tm, [_rows(y, tm), _rows(tgt, tm)],
                      [((T, D_MODEL), f32, (tm, D_MODEL), lambda i: (i, 0))], [((1, LANES), f32)])


def _mem_ln_fwd(name, mem, g, b):
    M = mem.shape[0]
    return _tile_call(name, lambda i, m, g, b: (_ln(m, g, b),), 1, [_full(mem), _full(g), _full(b)],
                      [((M, D_MODEL), bf16, (M, D_MODEL), lambda i: (0, 0))])[0]


def _mem_ln_bwd(name, mem, g, b, dmn):
    def fn(i, m, g, b, d):
        _, vjp = jax.vjp(lambda g, b: _ln(m, g, b), g, b)
        return vjp(d)

    return _tile_call(name, fn, 1, [_full(mem), _full(g), _full(b), _full(dmn)], [], [((1, D_MODEL), f32), ((1, D_MODEL), f32)])


def _pad_w_in(w):
    z = jnp.zeros((w.shape[0], D_INP - D_IN), w.dtype)
    return jnp.concatenate([w[:, 0:1536], w[:, 1544:2056], w[:, 2056:2568], w[:, 2824:3336], w[:, 3336:6408],
                            w[:, 2568:2696], w[:, 2696:2824], w[:, 1536:1544], z], axis=1)


def _unpad_dw_in(d):
    return jnp.concatenate([d[:, 0:1536], d[:, 6400:6408], d[:, 1536:2048], d[:, 2048:2560], d[:, 6144:6272],
                            d[:, 6272:6400], d[:, 2560:3072], d[:, 3072:6144]], axis=1)


def _lane_row(v, rows=8):
    out = jnp.zeros((rows, LANES), f32)
    return out.at[0, :v.shape[0]].set(v)


def _layer_fwd(l, x, xb, mem_nb, W):
    n = lambda s: f"l{l}_{s}"
    sv = {}
    u1, a1 = _gu_act(n("ffn1_gu"), xb, W['ffn1_w_gu_p'])
    f1, h1, h1b = _mm_res_ln(n("ffn1_down"), a1, W['ffn1_w_down'], x, W['ln_g'][0:1], W['ln_b'][0:1], 0.5)
    p = _mm(n("w_in"), h1b, W['w_inp'], 'nn', out_dtype=bf16)
    pba = _mm(n("w_in_ba"), h1b, W['w_ba'], 'nn')
    q, k, v, gb, bb = _dn_pre_fwd(n("dn_pre"), p, pba, W['conv_w'], W['hp'])
    qd, qk, u, w, kt, egl = _wy_fwd(n("dn_wy"), q, k, v, gb, bb)
    o_raw, s_all = _scan_fwd(n("dn_scan"), qd, qk, u, w, kt, egl)
    o_dn = _dn_post_fwd(n("dn_post"), o_raw, p, W['norm_w'])
    o_sw = _swa_fwd(n("swa"), p, W['snk'])
    kv = _mm(n("mem_kv"), mem_nb, W['w_mem_kv'], 'nn')
    o_xa = _xa_fwd(n("xa"), p, kv)
    merged = _merge_fwd(n("merge"), o_dn, o_sw, o_xa, p, W['w_branch'])
    mo, h2, h2b = _mm_res_ln(n("w_out"), merged, W['w_out'], h1, W['ln_g'][1:2], W['ln_b'][1:2], 1.0)
    u2, a2 = _gu_act(n("ffn2_gu"), h2b, W['ffn2_w_gu_p'])
    f2, h3, h3b = _mm_res_ln(n("ffn2_down"), a2, W['ffn2_w_down'], h2, W['ln_g'][2:3], W['ln_b'][2:3], 0.5)
    sv = dict(x=x, xb=xb, u1=u1, a1=a1, f1=f1, h1=h1, h1b=h1b, p=p, pba=pba, dn=(q, k, v, gb, bb), wy=(qd, qk, u, w, kt, egl),
              o_raw=o_raw, s_all=s_all, o_dn=o_dn, o_sw=o_sw, kv=kv, o_xa=o_xa, merged=merged,
              mo=mo, h2=h2, h2b=h2b, u2=u2, a2=a2, f2=f2)
    return h3, h3b, sv


def _ffn_bwd(n, tag, x, xb, u, a, f, g, b, w_gu_t, w_down_t, dh):
    dx_a, df, dg, db = _res_ln_bwd(n(f"{tag}_ln_bwd"), x, f, g, b, 0.5, dh)
    d_down = _mm(n(f"{tag}_down_dw"), a, df, 'tn')
    du = _down_dx_act_bwd(n(f"{tag}_down_dx"), df, w_down_t, u)
    d_gu = _mm(n(f"{tag}_gu_dw"), xb, du, 'tn', out_cols=(FF_BLK, lambda j: (j % 2) * (D_FF // FF_BLK) + j // 2))
    dx = _mm(n(f"{tag}_gu_dx"), du, w_gu_t, 'nn', add=dx_a)
    return dx, d_gu, d_down, dg, db


def _layer_bwd(l, sv, mem_nb, W, dh3, dmem_acc):
    n = lambda s: f"l{l}_{s}"
    G = {}
    dh2, G['ffn2_w_gu'], G['ffn2_w_down'], dg2, db2 = _ffn_bwd(
        n, "ffn2", sv['h2'], sv['h2b'], sv['u2'], sv['a2'], sv['f2'], W['ln_g'][2:3], W['ln_b'][2:3], W['ffn2_w_gu_t'], W['ffn2_w_down_t'], dh3)
    dh1_a, dmo, dg1, db1 = _res_ln_bwd(n("ln1_bwd"), sv['h1'], sv['mo'], W['ln_g'][1:2], W['ln_b'][1:2], 1.0, dh2)
    G['w_out'] = _mm(n("w_out_dw"), sv['merged'], dmo, 'tn')
    dmerged = _mm(n("w_out_dx"), dmo, W['w_out_t'], 'nn')
    p = sv['p']
    do_dn, do_sw, do_xa, dgates, G['w_branch'] = _merge_bwd(n("merge_bwd"), sv['o_dn'], sv['o_sw'], sv['o_xa'], p, W['w_branch'], dmerged)
    dxaq, dkv = _xa_bwd(n("xa_bwd"), p, sv['kv'], do_xa)
    dkv = dkv.astype(bf16)
    G['w_mem_kv'] = _mm(n("mem_kv_dw"), mem_nb, dkv, 'tn')
    dmem_n = _mm(n("mem_kv_dx"), dkv, W['w_mem_kv_t'], 'nn', add=dmem_acc)
    dswq, dkc, dkp, dvc, dvp, dsnk = _swa_bwd(n("swa_bwd"), p, W['snk'], do_sw)
    do_raw, dz, dnw = _dn_post_bwd(n("dn_post_bwd"), sv['o_raw'], p, W['norm_w'], do_dn)
    cts = _scan_bwd(n("dn_scan_bwd"), *sv['wy'], sv['s_all'], do_raw)
    dcts = _wy_bwd(n("dn_wy_bwd"), *sv['dn'], cts)
    dxc, dxp, dba, dcw, dhp = _dn_pre_bwd(n("dn_pre_bwd"), p, sv['pba'], W['conv_w'], W['hp'], dcts)
    dp = _assemble_dp(n("dp"), dxc, dxp, dz, dswq, dxaq, dgates, dkc, dkp, dvc, dvp, dba)
    G['w_in'] = _unpad_dw_in(_mm(n("w_in_dw"), sv['h1b'], dp, 'tn'))
    dh1 = _mm(n("w_in_dx"), dp, W['w_inp_t'], 'nn', add=dh1_a)
    dx, G['ffn1_w_gu'], G['ffn1_w_down'], dg0, db0 = _ffn_bwd(
        n, "ffn1", sv['x'], sv['xb'], sv['u1'], sv['a1'], sv['f1'], W['ln_g'][0:1], W['ln_b'][0:1], W['ffn1_w_gu_t'], W['ffn1_w_down_t'], dh1)
    G['ln_g'] = jnp.concatenate([dg0, dg1, dg2], axis=0)
    G['ln_b'] = jnp.concatenate([db0, db1, db2], axis=0)
    G['dn_conv_w'] = dcw
    G['dn_a_log'] = dhp[0, :DN_HEADS]
    G['dn_dt_bias'] = dhp[1, :DN_HEADS]
    G['dn_norm_w'] = dnw[0]
    G['swa_sinks'] = dsnk[:, 0]
    return dx, dmem_n, G


def _local_step(x, mem, tgt, Wf):
    mem_g, mem_b = Wf['mem_ln_g'][None, :], Wf['mem_ln_b'][None, :]
    mem_nb = _mem_ln_fwd("mem_ln", mem, mem_g, mem_b)
    layers = []
    for l in range(DEPTH):
        layers.append(dict(
            ln_g=Wf['ln_g'][l], ln_b=Wf['ln_b'][l], ffn1_w_gu_p=_pair_gate_up(Wf['ffn1_w_gu'][l]), ffn1_w_down=Wf['ffn1_w_down'][l],
            w_inp=_pad_w_in(Wf['w_in'][l]), conv_w=Wf['dn_conv_w'][l],
            hp=jnp.zeros((8, LANES), f32).at[0, :DN_HEADS].set(Wf['dn_a_log'][l]).at[1, :DN_HEADS].set(Wf['dn_dt_bias'][l]),
            norm_w=Wf['dn_norm_w'][l][None, :], snk=jnp.broadcast_to(Wf['swa_sinks'][l][:, None], (SWA_HEADS, LANES)),
            w_mem_kv=Wf['w_mem_kv'][l], w_branch=Wf['w_branch'][l], w_out=Wf['w_out'][l],
            ffn2_w_gu_p=_pair_gate_up(Wf['ffn2_w_gu'][l]), ffn2_w_down=Wf['ffn2_w_down'][l]))
        layers[l]['w_ba'] = layers[l]['w_inp'][:, 6400:6400 + LANES]
        for k in ('ffn1_w_gu_p', 'ffn1_w_down', 'w_inp', 'w_mem_kv', 'w_out', 'ffn2_w_gu_p', 'ffn2_w_down'):
            layers[l][k.replace('_p', '') + '_t'] = layers[l][k].T
    h, hb = x, x.astype(bf16)
    saved = []
    for l in range(DEPTH):
        h, hb, sv = _layer_fwd(l, h, hb, mem_nb, layers[l])
        saved.append(sv)
    dh, loss_row = _loss_grad("loss", h, tgt)
    grads = [None] * DEPTH
    dmem_n = None
    for l in reversed(range(DEPTH)):
        dh, dmem_n, grads[l] = _layer_bwd(l, saved[l], mem_nb, layers[l], dh, dmem_n)
    dmg, dmb = _mem_ln_bwd("mem_ln_bwd", mem, mem_g, mem_b, dmem_n)
    G = {k: jnp.stack([grads[l][k] for l in range(DEPTH)], axis=0) for k in grads[0]}
    G['mem_ln_g'], G['mem_ln_b'] = dmg[0], dmb[0]
    return loss_row, dh, G


SEG_ALIGN = 2048


def _round_up(n, m):
    return (n + m - 1) // m * m


def _layout(names, sizes, mult):
    off, table = 0, {}
    for nm in names:
        table[nm] = (off, sizes[nm])
        off += _round_up(sizes[nm], SEG_ALIGN)
    return table, _round_up(off // LANES, mult)


def _pack(table, rows, flat):
    names = list(table)
    lead = flat[names[0]].shape[:-1]
    parts, pos = [], 0
    for nm in names:
        off, size = table[nm]
        if off > pos:
            parts.append(jnp.zeros(lead + (off - pos,), flat[nm].dtype))
        parts.append(flat[nm])
        pos = off + size
    total = rows * LANES
    if total > pos:
        parts.append(jnp.zeros(lead + (total - pos,), parts[-1].dtype))
    return jnp.concatenate(parts, axis=-1).reshape(lead + (rows, LANES))


def _unpack(table, packed, nm):
    off, size = table[nm]
    flat = packed.reshape(packed.shape[:-2] + (-1,))
    return flat[..., off:off + size]


def _split_shards(full, ax):
    shp = full.shape
    t = full.reshape(shp[:ax] + (N_SHARD, shp[ax] // N_SHARD) + shp[ax + 1:])
    return jnp.moveaxis(t, ax, 0)


def _join_shards(sh4, ax):
    return jnp.concatenate([sh4[s] for s in range(N_SHARD)], axis=ax)


ANY = pl.BlockSpec(memory_space=pl.ANY)


def _me():
    return lax.axis_index("x"), lax.axis_index("y"), lax.axis_index("c")


def _comm_call(name, body, arrays, out_shapes, n_sem):
    n = len(arrays)
    scratch = [pltpu.SemaphoreType.DMA((n, n_sem)), pltpu.SemaphoreType.DMA((n, n_sem))]
    return pl.pallas_call(
        body, name=name, out_shape=out_shapes, in_specs=[ANY] * n, out_specs=[ANY] * n, scratch_shapes=scratch,
        compiler_params=pltpu.CompilerParams(has_side_effects=True),
    )(*arrays)


def _all_gather(name, xs):
    n = len(xs)

    def body(*refs):
        x_refs, out_refs = refs[:n], refs[n:2 * n]
        send_sems, recv_sems = refs[2 * n:]
        mx, my, mc = _me()
        chips = [(1 - mx, my), (mx, 1 - my), (1 - mx, 1 - my)]

        def copy(a, k, shard, half, to, src=None):
            dst = out_refs[a].at[shard, half]
            return pltpu.make_async_remote_copy(src_ref=dst if src is None else src, dst_ref=dst, send_sem=send_sems.at[a, k],
                                                recv_sem=recv_sems.at[a, k], device_id=to, device_id_type=MESH)

        first = [copy(a, j, 2 * mx + my, mc, (cx, cy, mc), src=x_refs[a].at[mc]) for j, (cx, cy) in enumerate(chips) for a in range(n)]
        for cp in first:
            cp.start()
        passed = []
        for j, (cx, cy) in enumerate(chips):
            for a in range(n):
                copy(a, j, 2 * cx + cy, mc, (mx, my, mc)).wait_recv()
                passed.append(copy(a, 3 + j, 2 * cx + cy, mc, (mx, my, 1 - mc)))
                passed[-1].start()
        for j, (cx, cy) in enumerate(chips):
            for a in range(n):
                copy(a, 3 + j, 2 * cx + cy, 1 - mc, (mx, my, mc)).wait_recv()
        for cp in first + passed:
            cp.wait_send()

    return _comm_call(name, body, xs, [jax.ShapeDtypeStruct((N_SHARD,) + x.shape, x.dtype) for x in xs], 6)


def _pair_exchange(name, items):
    n = len(items)

    def body(*refs):
        src_refs, dst_refs = refs[:n], refs[n:2 * n]
        send_sems, recv_sems = refs[2 * n:]
        mx, my, mc = _me()
        cps = [pltpu.make_async_remote_copy(src_ref=src_refs[a].at[s, 1 - mc], dst_ref=dst_refs[a].at[s], send_sem=send_sems.at[a, s],
                                            recv_sem=recv_sems.at[a, s], device_id=(mx, my, 1 - mc), device_id_type=MESH)
               for a in range(n) for s in range(N_SHARD)]
        for cp in cps:
            cp.start()
        for cp in cps:
            cp.wait()

    return _comm_call(name, body, items, [jax.ShapeDtypeStruct((N_SHARD,) + t.shape[2:], t.dtype) for t in items], N_SHARD)


def _chip_exchange(name, parts):
    n = len(parts)

    def body(*refs):
        p_refs, dst_refs = refs[:n], refs[n:2 * n]
        send_sems, recv_sems = refs[2 * n:]
        mx, my, mc = _me()
        chips = [(1 - mx, my), (mx, 1 - my), (1 - mx, 1 - my)]
        cps = [pltpu.make_async_remote_copy(src_ref=p_refs[a].at[2 * cx + cy], dst_ref=dst_refs[a].at[j], send_sem=send_sems.at[a, j],
                                            recv_sem=recv_sems.at[a, j], device_id=(cx, cy, mc), device_id_type=MESH)
               for a in range(n) for j, (cx, cy) in enumerate(chips)]
        for cp in cps:
            cp.start()
        for cp in cps:
            cp.wait()

    return _comm_call(name, body, parts, [jax.ShapeDtypeStruct((3,) + t.shape[1:], t.dtype) for t in parts], 3)


def _pair_swap(name, reds):
    n = len(reds)

    def body(*refs):
        r_refs, out_refs = refs[:n], refs[n:2 * n]
        send_sems, recv_sems = refs[2 * n:]
        mx, my, mc = _me()
        cps = [pltpu.make_async_remote_copy(src_ref=r_refs[a], dst_ref=out_refs[a], send_sem=send_sems.at[a, 0],
                                            recv_sem=recv_sems.at[a, 0], device_id=(mx, my, 1 - mc), device_id_type=MESH)
               for a in range(n)]
        for cp in cps:
            cp.start()
        for cp in cps:
            cp.wait()

    return _comm_call(name, body, reds, [jax.ShapeDtypeStruct(t.shape, t.dtype) for t in reds], 1)


EW_BLOCK_BYTES = 1 << 20


def _ew_call(name, fn, ins, n_out, out_dtype=f32):
    shape = ins[0].shape
    last = shape[-1]
    flat = [a.reshape(-1, last) for a in ins]
    R = flat[0].shape[0]
    cands = tuple(c for c in (4096, 2048, 1024, 512, 256, 128, 64, 32, 16, 8) if c * last * 4 <= EW_BLOCK_BYTES)
    tr = _pick(R, cands)
    res = _tile_call(name, lambda i, *a: fn(*a), R // tr, [(a, (tr, last), lambda i: (i, 0)) for a in flat],
                     [((R, last), out_dtype, (tr, last), lambda i: (i, 0))] * n_out)
    return [r.reshape(shape) for r in res]


def _adamw(g, w, m, v):
    m = B1 * m + (1.0 - B1) * g
    v = B2 * v + (1.0 - B2) * jnp.square(g)
    m_hat = m / (1.0 - B1 ** STEP)
    v_hat = v / (1.0 - B2 ** STEP)
    return -LR * (m_hat / (jnp.sqrt(v_hat) + EPS) + WD * w), m, v


def _adamw_call(name, mine, theirs, w, m, v, mc1):
    shape, last = w.shape, w.shape[-1]
    g2 = [t.reshape(-1, last) for t in (mine, theirs)]
    w3 = [t.reshape(2, -1, last) for t in (w, m, v)]
    R = g2[0].shape[0]
    tr = _pick(R, tuple(c for c in (4096, 2048, 1024, 512, 256, 128, 64, 32, 16, 8) if c * last * 4 <= EW_BLOCK_BYTES))

    def body(mc_ref, mine_ref, theirs_ref, w_ref, m_ref, v_ref, g_out, d_out, m_out, v_out):
        g = jnp.where(pl.program_id(0) == mc_ref[0], mine_ref[...], theirs_ref[...])
        d, nm, nv = _adamw(g, w_ref[0], m_ref[0], v_ref[0])
        g_out[0], d_out[0], m_out[0], v_out[0] = g, d, nm, nv

    half = pl.BlockSpec((tr, last), lambda h, i: (i, 0))
    full = pl.BlockSpec((1, tr, last), lambda h, i: (h, i, 0))
    res = pl.pallas_call(
        body, name=name, grid=(2, R // tr),
        in_specs=[pl.BlockSpec(memory_space=pltpu.SMEM), half, half, full, full, full], out_specs=[full] * 4,
        out_shape=[jax.ShapeDtypeStruct((2, R, last), f32)] * 4,
        compiler_params=pltpu.CompilerParams(dimension_semantics=("arbitrary", "arbitrary"), vmem_limit_bytes=VMEM_LIMIT),
    )(mc1, *g2, *w3)
    return tuple(r.reshape(shape) for r in res)


def kernel(x, mem, mem_ln_g, mem_ln_b, ln_g, ln_b, ffn1_w_gu, ffn1_w_down, w_in, dn_conv_w, dn_a_log, dn_dt_bias, dn_norm_w, swa_sinks, w_mem_kv, w_branch, w_out, ffn2_w_gu, ffn2_w_down, loss_target, m_mem_ln_g, m_mem_ln_b, m_ln_g, m_ln_b, m_ffn1_w_gu, m_ffn1_w_down, m_w_in, m_dn_conv_w, m_dn_a_log, m_dn_dt_bias, m_dn_norm_w, m_swa_sinks, m_w_mem_kv, m_w_branch, m_w_out, m_ffn2_w_gu, m_ffn2_w_down, v_mem_ln_g, v_mem_ln_b, v_ln_g, v_ln_b, v_ffn1_w_gu, v_ffn1_w_down, v_w_in, v_dn_conv_w, v_dn_a_log, v_dn_dt_bias, v_dn_norm_w, v_swa_sinks, v_w_mem_kv, v_w_branch, v_w_out, v_ffn2_w_gu, v_ffn2_w_down):
    args = dict(locals())
    Ws = {n: args[n] for n in WEIGHTS}
    Ms = {n: args["m_" + n] for n in WEIGHTS}
    Vs = {n: args["v_" + n] for n in WEIGHTS}
    mc = lax.axis_index("c")
    my_s = 2 * lax.axis_index("x") + lax.axis_index("y")
    small = [n for n in WEIGHTS if n not in MATRICES]

    ag_table, ag_rows = _layout(SMALL_SHARDED, {n: Ws[n].size for n in SMALL_SHARDED}, 16)
    ag_small = _pack(ag_table, ag_rows, {n: Ws[n].reshape(-1) for n in SMALL_SHARDED}).reshape(2, ag_rows // 2, LANES)
    local = [Ws[n].astype(bf16) for n in MATRICES] + [ag_small]
    gathered = _all_gather("all_gather_w", local)
    gathered = [jnp.stack([jnp.where(my_s == s, loc, g[s]) for s in range(N_SHARD)], axis=0) for loc, g in zip(local, gathered)]
    Wf = {n: _join_shards(g, SHARD_AXIS[n]) for n, g in zip(MATRICES, gathered)}
    g_small = gathered[-1].reshape(N_SHARD, ag_rows, LANES)
    for n in SMALL_SHARDED:
        Wf[n] = _join_shards(_unpack(ag_table, g_small, n).reshape((N_SHARD,) + Ws[n].shape), SHARD_AXIS[n])
    for n in WEIGHTS:
        if SHARD_AXIS[n] is None:
            Wf[n] = Ws[n]

    loss_row, dx, G = _local_step(x[0], mem[0], loss_target[0], Wf)

    table, rows = _layout(small + ['loss'], {**{n: Ws[n].size for n in small}, 'loss': 1}, 16)
    gflat = {n: (jnp.broadcast_to(G[n].reshape(1, -1), (N_SHARD, G[n].size)) if SHARD_AXIS[n] is None
                 else _split_shards(G[n], SHARD_AXIS[n]).reshape(N_SHARD, -1)) for n in small}
    gflat['loss'] = jnp.broadcast_to(loss_row[:, :1], (N_SHARD, 1))
    items = [_split_shards(G[n], SHARD_AXIS[n]) for n in MATRICES] + [_pack(table, rows, gflat).reshape(N_SHARD, 2, rows // 2, LANES)]
    tags = MATRICES + ['small']
    got = _pair_exchange("rs_pair", items)
    keep = [lax.dynamic_index_in_dim(a, mc, axis=1, keepdims=False) for a in items]
    wire = [bf16] * len(MATRICES) + [f32]
    part = [_ew_call(f"rs_add_pair_{t}", lambda a, b: a + b, [k, g], 1, out_dtype=dt)[0] for t, k, g, dt in zip(tags, keep, got, wire)]
    others = _chip_exchange("rs_chips", part)
    own = lambda a: lax.dynamic_index_in_dim(a, my_s, axis=0, keepdims=False)
    red = [_ew_call(f"rs_add_chips_{t}", lambda k, g, fx, fy, fxy: ((k + g) + fy) + (fx + fxy), [own(k), own(g), o[0], o[1], o[2]], 1)[0]
           for t, k, g, o in zip(tags, keep, got, others)]
    theirs = _pair_swap("rs_swap", red)

    mc1 = mc.astype(i32).reshape(1)
    outs = {}
    for n, a, b in zip(MATRICES, red, theirs):
        outs[n] = _adamw_call(f"adamw_{n}", a, b, Ws[n], Ms[n], Vs[n], mc1)
    fill = {'loss': jnp.zeros((1,), f32)}
    packs = [_pack(table, rows, {**{n: src[n].reshape(-1) for n in small}, **fill}).reshape(2, rows // 2, LANES) for src in (Ws, Ms, Vs)]
    small_out = [p.reshape(rows, LANES) for p in _adamw_call("adamw_small", red[-1], theirs[-1], *packs, mc1)]
    for n in small:
        outs[n] = tuple(_unpack(table, p, n).reshape(Ws[n].shape) for p in small_out)
    loss = _unpack(table, small_out[0], 'loss').reshape(())
    return (loss, dx[None], *[outs[n][k] for k in range(4) for n in WEIGHTS])
```

```python
import functools

import jax
import jax.numpy as jnp
from jax import lax
from jax.experimental import pallas as pl
from jax.experimental.pallas import tpu as pltpu

f32, bf16, i32 = jnp.float32, jnp.bfloat16, jnp.int32
HI = lax.Precision.HIGHEST
MESH = pl.DeviceIdType.MESH

D_MODEL = 1024
DEPTH = 2
DN_HEADS, DN_DK, DN_CONV, DN_CHUNK = 4, 128, 4, 64
SWA_HEADS, SWA_KV_HEADS, SWA_DH, WINDOW = 8, 2, 64, 128
XA_HEADS, XA_DH = 4, 128
D_FF = 2816
N_BRANCH, BRANCH_W = 3, 512
ALPHA = (2 * DEPTH) ** 0.25
LN_EPS, RMS_EPS, NEG_INF = 1e-5, 1e-6, -1e30
D_IN = 6408
D_INP = 6656
LR, B1, B2, EPS, WD, STEP = 0.001, 0.9, 0.999, 1e-08, 0.01, 10

LANES = 128
VMEM_LIMIT = 56 << 20
N_SHARD = 4
SCAN_CHUNKS = 4
HALO = 8

WEIGHTS = ['mem_ln_g', 'mem_ln_b', 'ln_g', 'ln_b', 'ffn1_w_gu', 'ffn1_w_down', 'w_in', 'dn_conv_w', 'dn_a_log',
           'dn_dt_bias', 'dn_norm_w', 'swa_sinks', 'w_mem_kv', 'w_branch', 'w_out', 'ffn2_w_gu', 'ffn2_w_down']
SHARD_AXIS = {'mem_ln_g': None, 'mem_ln_b': None, 'ln_g': 2, 'ln_b': 2, 'ffn1_w_gu': 2, 'ffn1_w_down': 1, 'w_in': 2,
              'dn_conv_w': 2, 'dn_a_log': None, 'dn_dt_bias': None, 'dn_norm_w': None, 'swa_sinks': None,
              'w_mem_kv': 1, 'w_branch': 3, 'w_out': 1, 'ffn2_w_gu': 2, 'ffn2_w_down': 1}
MATRICES = ['ffn1_w_gu', 'ffn1_w_down', 'w_in', 'w_mem_kv', 'w_branch', 'w_out', 'ffn2_w_gu', 'ffn2_w_down']
SMALL_SHARDED = ['ln_g', 'ln_b', 'dn_conv_w']


def _dg(a, b, mode, hi):
    nb = a.ndim - 2
    bd = tuple(range(nb))
    ca = nb if mode == 'tn' else nb + 1
    cb = nb + 1 if mode == 'nt' else nb
    dims = (((ca,), (cb,)), (bd, bd))
    dot = lambda x, y: lax.dot_general(x, y, dims, preferred_element_type=f32)
    a_hi, b_hi = a.astype(bf16), b.astype(bf16)
    if not hi:
        return dot(a_hi, b_hi)
    a_lo = (a.astype(f32) - a_hi.astype(f32)).astype(bf16)
    b_lo = (b.astype(f32) - b_hi.astype(f32)).astype(bf16)
    return dot(a_hi, b_hi) + (dot(a_hi, b_lo) + dot(a_lo, b_hi))


@functools.partial(jax.custom_vjp, nondiff_argnums=(2, 3))
def _dot(a, b, mode, hi):
    return _dg(a, b, mode, hi)


def _dot_fwd(a, b, mode, hi):
    return _dg(a, b, mode, hi), (a, b)


def _dot_bwd(mode, hi, res, g):
    a, b = res
    if mode == 'nn':
        da, db = _dg(g, b, 'nt', hi), _dg(a, g, 'tn', hi)
    elif mode == 'nt':
        da, db = _dg(g, b, 'nn', hi), _dg(g, a, 'tn', hi)
    else:
        da, db = _dg(b, g, 'nt', hi), _dg(a, g, 'nn', hi)
    return da.astype(a.dtype), db.astype(b.dtype)


_dot.defvjp(_dot_fwd, _dot_bwd)


def bdot(a, b, mode):
    return _dot(a, b, mode, False)


def hdot(a, b, mode):
    return _dot(a, b, mode, True)


def _shift_rows_impl(x, xprev, s):
    rows = lax.broadcasted_iota(i32, x.shape, 0)
    return jnp.where(rows >= s, pltpu.roll(x, s, 0), pltpu.roll(xprev, s, 0))


@functools.partial(jax.custom_vjp, nondiff_argnums=(2,))
def _shift_rows(x, xprev, s):
    return _shift_rows_impl(x, xprev, s)


def _shift_rows_fwd(x, xprev, s):
    return _shift_rows_impl(x, xprev, s), None


def _shift_rows_bwd(s, _, g):
    n = g.shape[0]
    rows = lax.broadcasted_iota(i32, g.shape, 0)
    r = pltpu.roll(g, n - s, 0)
    keep = rows < n - s
    return jnp.where(keep, r, 0.0), jnp.where(keep, 0.0, r)


_shift_rows.defvjp(_shift_rows_fwd, _shift_rows_bwd)


def _lane_pick(x, lane):
    idx = lax.broadcasted_iota(i32, x.shape, x.ndim - 1)
    return jnp.sum(jnp.where(idx == lane, x, 0.0), axis=-1, keepdims=True)


def _silu(x):
    return x * jax.nn.sigmoid(x)


def _tri_inv(a):
    C = a.shape[-1]
    eye = (lax.broadcasted_iota(i32, a.shape, 1) == lax.broadcasted_iota(i32, a.shape, 2)).astype(f32)
    p = -a
    x = eye + p
    for _ in range((C - 1).bit_length() - 1):
        p = _dg(p, p, 'nn', True)
        x = x + _dg(x, p, 'nn', True)
    return x


@jax.custom_vjp
def _tri_solve(a, rhs):
    return _dg(_tri_inv(a), rhs, 'nn', True)


def _tri_solve_fwd(a, rhs):
    tinv = _tri_inv(a)
    sol = _dg(tinv, rhs, 'nn', True)
    return sol, (tinv, sol)


def _tri_solve_bwd(res, g):
    tinv, sol = res
    d_rhs = _dg(tinv, g, 'tn', True)
    return -_dg(d_rhs, sol, 'nt', True), d_rhs


_tri_solve.defvjp(_tri_solve_fwd, _tri_solve_bwd)


@functools.partial(jax.custom_vjp, nondiff_argnums=(1,))
def _lane_head(x, n):
    return x[:, :, :n]


def _lane_head_fwd(x, n):
    return x[:, :, :n], None


def _lane_head_bwd(n, _, g):
    s = jnp.sum(g, axis=-1, keepdims=True) * (1.0 / LANES)
    return (jnp.broadcast_to(s, g.shape[:-1] + (LANES,)),)


_lane_head.defvjp(_lane_head_fwd, _lane_head_bwd)


@functools.partial(jax.custom_vjp, nondiff_argnums=(1,))
def _last_row(x, c):
    return x[:, c - 1:, :]


def _last_row_fwd(x, c):
    return x[:, c - 1:, :], None


def _last_row_bwd(c, _, g):
    shape = (g.shape[0], c, g.shape[2])
    rows = lax.broadcasted_iota(i32, shape, 1)
    return (jnp.where(rows == c - 1, jnp.broadcast_to(g, shape), 0.0),)


_last_row.defvjp(_last_row_fwd, _last_row_bwd)


def _full(a):
    nd = a.ndim
    return (a, tuple(a.shape), lambda i, _nd=nd: (0,) * _nd)


def _rows(a, tm, col=0, width=None):
    width = a.shape[1] if width is None else width
    return (a, (tm, width), lambda i, _c=col: (i, _c))


def _tile_call(name, fn, n, ins, outs, accs=()):
    n_in, n_out, n_acc = len(ins), len(outs), len(accs)

    def body(*refs):
        i = pl.program_id(0)
        res = fn(i, *[r[...].astype(f32) for r in refs[:n_in]])
        if not isinstance(res, (tuple, list)):
            res = (res,)
        assert len(res) == n_out + n_acc, (name, len(res), n_out, n_acc)
        for r, v in zip(refs[n_in:n_in + n_out], res[:n_out]):
            r[...] = v.astype(r.dtype)
        if n_acc:
            acc_refs = refs[n_in + n_out:]

            @pl.when(i == 0)
            def _():
                for r in acc_refs:
                    r[...] = jnp.zeros(r.shape, r.dtype)

            for r, v in zip(acc_refs, res[n_out:]):
                r[...] += v.astype(r.dtype)

    out_shape = [jax.ShapeDtypeStruct(s, d) for s, d, _, _ in outs] + [jax.ShapeDtypeStruct(s, d) for s, d in accs]
    out_specs = [pl.BlockSpec(b, m) for _, _, b, m in outs]
    out_specs += [pl.BlockSpec(tuple(s), lambda i, _nd=len(s): (0,) * _nd) for s, _ in accs]
    res = pl.pallas_call(
        body, name=name, grid=(n,),
        in_specs=[pl.BlockSpec(b, m) for _, b, m in ins],
        out_specs=out_specs, out_shape=out_shape,
        compiler_params=pltpu.CompilerParams(dimension_semantics=("arbitrary",), vmem_limit_bytes=VMEM_LIMIT),
    )(*[a for a, _, _ in ins])
    return res


def _pick(n, cands):
    for c in cands:
        if n % c == 0:
            return c
    return n


def _mm(name, a, b, mode, out_dtype=f32, add=None, out_cols=None):
    if mode == 'tn':
        K, M = a.shape
    else:
        M, K = a.shape
    N = b.shape[0] if mode == 'nt' else b.shape[1]
    tm = _pick(M, (1024, 1408, 512, 256, 128))
    tn = _pick(N, (512, 1408, 256, 128)) if out_cols is None else out_cols[0]
    col = (lambda j: j) if out_cols is None else out_cols[1]
    tk = K if K <= 3328 else _pick(K, (3328, 2816, 2048, 1024, 512, 256, 128))
    nk = K // tk
    ca = 0 if mode == 'tn' else 1
    cb = 1 if mode == 'nt' else 0
    dims = (((ca,), (cb,)), ((), ()))

    def body(*refs):
        a_ref, b_ref = refs[0], refs[1]
        add_ref = refs[2] if add is not None else None
        part = lax.dot_general(a_ref[...].astype(bf16), b_ref[...].astype(bf16), dims, preferred_element_type=f32)

        def finish(r, o_ref):
            if add_ref is not None:
                r = r + add_ref[...].astype(f32)
            o_ref[...] = r.astype(o_ref.dtype)

        if nk == 1:
            finish(part, refs[-1])
            return
        o_ref, acc_ref = refs[-2], refs[-1]
        k = pl.program_id(2)

        @pl.when(k == 0)
        def _():
            acc_ref[...] = part

        @pl.when(k > 0)
        def _():
            acc_ref[...] += part

        @pl.when(k == nk - 1)
        def _():
            finish(acc_ref[...], o_ref)

    a_spec = pl.BlockSpec((tk, tm), lambda i, j, k: (k, i)) if mode == 'tn' else pl.BlockSpec((tm, tk), lambda i, j, k: (i, k))
    b_spec = pl.BlockSpec((tn, tk), lambda i, j, k: (j, k)) if mode == 'nt' else pl.BlockSpec((tk, tn), lambda i, j, k: (k, j))
    in_specs, args = [a_spec, b_spec], [a, b]
    if add is not None:
        in_specs.append(pl.BlockSpec((tm, tn), lambda i, j, k: (i, j)))
        args.append(add)
    return pl.pallas_call(
        body, name=name, grid=(M // tm, N // tn, nk), in_specs=in_specs,
        out_specs=pl.BlockSpec((tm, tn), lambda i, j, k: (i, col(j))),
        out_shape=jax.ShapeDtypeStruct((M, N), out_dtype),
        scratch_shapes=[pltpu.VMEM((tm, tn), f32)] if nk > 1 else [],
        compiler_params=pltpu.CompilerParams(dimension_semantics=("parallel", "parallel", "arbitrary"), vmem_limit_bytes=VMEM_LIMIT),
    )(*args)


def _swiglu(u):
    return _silu(u[:, :D_FF]) * u[:, D_FF:]


def _res_ln(x, f, g, b, s):
    r = ALPHA * x + s * f
    mu = jnp.mean(r, axis=-1, keepdims=True)
    rc = r - mu
    var = jnp.mean(rc * rc, axis=-1, keepdims=True)
    return rc * lax.rsqrt(var + LN_EPS) * g + b


def _ln(x, g, b):
    mu = jnp.mean(x, axis=-1, keepdims=True)
    xc = x - mu
    var = jnp.mean(xc * xc, axis=-1, keepdims=True)
    return xc * lax.rsqrt(var + LN_EPS) * g + b


def _dn_pre(first, xc, xp, ba, cw, hp):
    xp = jnp.where(first, 0.0, xp)
    y = cw[DN_CONV - 1:DN_CONV, :] * xc
    for j in range(DN_CONV - 1):
        y = y + cw[j:j + 1, :] * _shift_rows(xc, xp, DN_CONV - 1 - j)
    c = _silu(y)
    qs, ks, vs, gs, bs = [], [], [], [], []
    nqk = DN_HEADS * DN_DK
    for h in range(DN_HEADS):
        q = c[:, h * DN_DK:(h + 1) * DN_DK]
        k = c[:, nqk + h * DN_DK:nqk + (h + 1) * DN_DK]
        v = c[:, 2 * nqk + h * DN_DK:2 * nqk + (h + 1) * DN_DK]
        qs.append(q * lax.rsqrt(jnp.sum(q * q, axis=-1, keepdims=True) + RMS_EPS))
        ks.append(k * lax.rsqrt(jnp.sum(k * k, axis=-1, keepdims=True) + RMS_EPS))
        vs.append(v)
        beta = jax.nn.sigmoid(_lane_pick(ba, h))
        a_log = _lane_pick(hp[0:1, :], h)
        dt = _lane_pick(hp[1:2, :], h)
        g = -jnp.exp(a_log) * jax.nn.softplus(_lane_pick(ba, DN_HEADS + h) + dt)
        gs.append(jnp.broadcast_to(g, q.shape))
        bs.append(jnp.broadcast_to(beta, q.shape))
    return tuple(jnp.stack(t, axis=0) for t in (qs, ks, vs, gs, bs))


def _wy(q, k, v, gb, bb):
    B, C, _ = q.shape
    ri = lax.broadcasted_iota(i32, (B, C, C), 1)
    ci = lax.broadcasted_iota(i32, (B, C, C), 2)
    tril, strict = ri >= ci, ri > ci
    gc = hdot(tril.astype(f32), gb, 'nn')
    gl = jnp.broadcast_to(_last_row(gc, C), gc.shape)
    col = _lane_head(gc, C)
    decay = jnp.exp(jnp.where(tril, col - jnp.swapaxes(col, 1, 2), NEG_INF))
    qs = q * (DN_DK ** -0.5)
    kb = k * bb
    a = jnp.where(strict, bdot(kb, k, 'nt') * decay, 0.0)
    sol = _tri_solve(a, jnp.concatenate([v * bb, kb * jnp.exp(gc)], axis=-1))
    qk = jnp.where(tril, bdot(qs, k, 'nt') * decay, 0.0)
    kt = k * jnp.exp(gl - gc)
    qd = qs * jnp.exp(gc)
    return qd, qk, sol[..., :DN_DK], sol[..., DN_DK:], kt, jnp.exp(gl)


def _scan_step(S, qd, qk, u, w, kt, egl):
    vn = u - bdot(w, S, 'nn')
    o = bdot(qd, S, 'nn') + bdot(qk, vn, 'nn')
    e2 = jnp.concatenate([egl] * (DN_DK // DN_CHUNK), axis=1)
    return o, S * e2 + bdot(kt, vn, 'tn')


def _dn_post(o, z, nw):
    outs = []
    for h in range(DN_HEADS):
        oh = o[h]
        oh = oh * lax.rsqrt(jnp.mean(oh * oh, axis=-1, keepdims=True) + RMS_EPS) * nw
        outs.append(oh * _silu(z[:, h * DN_DK:(h + 1) * DN_DK]))
    return jnp.concatenate(outs, axis=1)


def _swa(first, q, kc, kp, vc, vp, snk):
    W = q.shape[1]
    G = SWA_HEADS // SWA_KV_HEADS
    r = lax.broadcasted_iota(i32, (G, W, 2 * W), 1)
    c = lax.broadcasted_iota(i32, (G, W, 2 * W), 2)
    mask = (c > W + r - WINDOW) & (c <= W + r) & jnp.logical_or(c >= W, jnp.logical_not(first))
    sink_all = _lane_pick(snk, 0)
    outs = []
    for j in range(SWA_KV_HEADS):
        qj = q[j * G:(j + 1) * G]
        kk = jnp.broadcast_to(jnp.concatenate([kp[j], kc[j]], axis=0)[None], (G, 2 * W, SWA_DH))
        vv = jnp.broadcast_to(jnp.concatenate([vp[j], vc[j]], axis=0)[None], (G, 2 * W, SWA_DH))
        s = jnp.where(mask, bdot(qj, kk, 'nt') * (SWA_DH ** -0.5), NEG_INF)
        sink = sink_all[j * G:(j + 1) * G][:, :, None]
        m = jnp.maximum(jnp.max(s, axis=-1, keepdims=True), sink)
        p = jnp.exp(s - m)
        p = p / (jnp.sum(p, axis=-1, keepdims=True) + jnp.exp(sink - m))
        outs.append(bdot(p, vv, 'nn'))
    return jnp.concatenate(outs, axis=0)


def _xa(q, kv):
    outs = []
    nk = XA_HEADS * XA_DH
    for h in range(XA_HEADS):
        qh = q[:, h * XA_DH:(h + 1) * XA_DH]
        kh = kv[:, h * XA_DH:(h + 1) * XA_DH]
        vh = kv[:, nk + h * XA_DH:nk + (h + 1) * XA_DH]
        s = bdot(qh, kh, 'nt') * (XA_DH ** -0.5)
        m = jnp.max(s, axis=-1, keepdims=True)
        p = jnp.exp(s - m)
        p = p / jnp.sum(p, axis=-1, keepdims=True)
        outs.append(bdot(p, vh, 'nn'))
    return jnp.concatenate(outs, axis=1)


def _merge(o_dn, o_sw, o_xa, gates, wb):
    acc = None
    for n, o in enumerate((o_dn, o_sw, o_xa)):
        t = jax.nn.sigmoid(gates[:, n * D_MODEL:(n + 1) * D_MODEL]) * bdot(o, wb[n], 'nn')
        acc = t if acc is None else acc + t
    return acc


def _row_tile(T, want):
    return _pick(T, tuple(c for c in (1024, 512, 256, 128, 64) if c <= want))


def _res_ln_fwd(name, x, f, g, b, s):
    T = x.shape[0]
    tm = _row_tile(T, 512)

    def fn(i, x, f, g, b):
        h = _res_ln(x, f, g, b, s)
        return h, h

    return _tile_call(name, fn, T // tm, [_rows(x, tm), _rows(f, tm), _full(g), _full(b)],
                      [((T, D_MODEL), f32, (tm, D_MODEL), lambda i: (i, 0)), ((T, D_MODEL), bf16, (tm, D_MODEL), lambda i: (i, 0))])


def _res_ln_bwd(name, x, f, g, b, s, dh):
    T = x.shape[0]
    tm = _row_tile(T, 512)

    def fn(i, x, f, g, b, dh):
        _, vjp = jax.vjp(lambda x, f, g, b: _res_ln(x, f, g, b, s), x, f, g, b)
        return vjp(dh)

    return _tile_call(name, fn, T // tm, [_rows(x, tm), _rows(f, tm), _full(g), _full(b), _rows(dh, tm)],
                      [((T, D_MODEL), f32, (tm, D_MODEL), lambda i: (i, 0)), ((T, D_MODEL), bf16, (tm, D_MODEL), lambda i: (i, 0))],
                      [((1, D_MODEL), f32), ((1, D_MODEL), f32)])


FF_BLK = D_FF // 2


def _pair_gate_up(w):
    d = w.shape[0]
    return jnp.stack([w[:, :D_FF].reshape(d, -1, FF_BLK), w[:, D_FF:].reshape(d, -1, FF_BLK)], axis=2).reshape(d, 2 * D_FF)


def _swiglu_blk(u):
    return _silu(u[:, :FF_BLK]) * u[:, FF_BLK:]


def _mm_spec(tm, tn):
    return dict(compiler_params=pltpu.CompilerParams(dimension_semantics=("parallel", "arbitrary"), vmem_limit_bytes=VMEM_LIMIT))


def _gu_act(name, xb, w_pair):
    T, D = xb.shape
    tm = _pick(T, (512, 256, 128))

    def body(x_ref, w_ref, u_ref, a_ref):
        r = lax.dot_general(x_ref[...], w_ref[...], (((1,), (0,)), ((), ())), preferred_element_type=f32)
        u_ref[...] = r.astype(bf16)
        a_ref[...] = _swiglu_blk(r).astype(bf16)

    return pl.pallas_call(
        body, name=name, grid=(T // tm, D_FF // FF_BLK),
        in_specs=[pl.BlockSpec((tm, D), lambda i, j: (i, 0)), pl.BlockSpec((D, 2 * FF_BLK), lambda i, j: (0, j))],
        out_specs=[pl.BlockSpec((tm, 2 * FF_BLK), lambda i, j: (i, j)), pl.BlockSpec((tm, FF_BLK), lambda i, j: (i, j))],
        out_shape=[jax.ShapeDtypeStruct((T, 2 * D_FF), bf16), jax.ShapeDtypeStruct((T, D_FF), bf16)],
        **_mm_spec(tm, FF_BLK))(xb, w_pair)


def _down_dx_act_bwd(name, df, w_down_t, u):
    T, D = df.shape
    tm = _pick(T, (256, 128))

    def body(df_ref, w_ref, u_ref, du_ref):
        da = lax.dot_general(df_ref[...], w_ref[...], (((1,), (0,)), ((), ())), preferred_element_type=f32)
        _, vjp = jax.vjp(_swiglu_blk, u_ref[...].astype(f32))
        du_ref[...] = vjp(da)[0].astype(bf16)

    return pl.pallas_call(
        body, name=name, grid=(T // tm, D_FF // FF_BLK),
        in_specs=[pl.BlockSpec((tm, D), lambda i, j: (i, 0)), pl.BlockSpec((D, FF_BLK), lambda i, j: (0, j)),
                  pl.BlockSpec((tm, 2 * FF_BLK), lambda i, j: (i, j))],
        out_specs=pl.BlockSpec((tm, 2 * FF_BLK), lambda i, j: (i, j)),
        out_shape=jax.ShapeDtypeStruct((T, 2 * D_FF), bf16),
        **_mm_spec(tm, FF_BLK))(df, w_down_t, u)


def _mm_res_ln(name, a, w, x, g, b, s):
    T, K = a.shape
    tm = _pick(T, (512, 256, 128))

    def body(a_ref, w_ref, x_ref, g_ref, b_ref, f_ref, h_ref, hb_ref):
        f = lax.dot_general(a_ref[...], w_ref[...], (((1,), (0,)), ((), ())), preferred_element_type=f32)
        h = _res_ln(x_ref[...], f, g_ref[...], b_ref[...], s)
        f_ref[...] = f
        h_ref[...] = h
        hb_ref[...] = h.astype(bf16)

    row = pl.BlockSpec((tm, D_MODEL), lambda i: (i, 0))
    vec = pl.BlockSpec((1, D_MODEL), lambda i: (0, 0))
    return pl.pallas_call(
        body, name=name, grid=(T // tm,),
        in_specs=[pl.BlockSpec((tm, K), lambda i: (i, 0)), pl.BlockSpec((K, D_MODEL), lambda i: (0, 0)), row, vec, vec],
        out_specs=[row, row, row],
        out_shape=[jax.ShapeDtypeStruct((T, D_MODEL), f32), jax.ShapeDtypeStruct((T, D_MODEL), f32), jax.ShapeDtypeStruct((T, D_MODEL), bf16)],
        compiler_params=pltpu.CompilerParams(dimension_semantics=("parallel",), vmem_limit_bytes=VMEM_LIMIT))(a, w, x, g, b)


def _hm_spec(tm, w=DN_DK):
    return ((DN_HEADS, tm, w), lambda i: (0, i, 0))


def _prev(i):
    return jnp.maximum(i - 1, 0)


def _dn_pre_fwd(name, p, pba, cw, hp):
    T = p.shape[0]
    tm = _row_tile(T, 256)
    n = T // tm
    W3 = 3 * DN_HEADS * DN_DK
    ins = [(p, (tm, W3), lambda i: (i, 0)), (p, (tm, W3), lambda i: (_prev(i), 0)),
           (pba, (tm, LANES), lambda i: (i, 0)), _full(cw), _full(hp)]
    blk, im = _hm_spec(tm)
    return _tile_call(name, lambda i, *a: _dn_pre(i == 0, *a), n, ins, [((DN_HEADS, T, DN_DK), f32, blk, im)] * 5)


def _dn_pre_bwd(name, p, pba, cw, hp, cts):
    T = p.shape[0]
    tm = _row_tile(T, 256)
    n = T // tm
    W3 = 3 * DN_HEADS * DN_DK
    blk, im = _hm_spec(tm)
    ins = [(p, (tm, W3), lambda i: (i, 0)), (p, (tm, W3), lambda i: (_prev(i), 0)),
           (pba, (tm, LANES), lambda i: (i, 0)), _full(cw), _full(hp)] + [(c, blk, im) for c in cts]

    def fn(i, xc, xp, ba, cw, hp, *cts):
        _, vjp = jax.vjp(lambda *a: _dn_pre(i == 0, *a), xc, xp, ba, cw, hp)
        dxc, dxp, dba, dcw, dhp = vjp(tuple(cts))
        return dxc, dxp[tm - HALO:], dba, dcw, dhp

    return _tile_call(name, fn, n, ins,
                      [((T, W3), f32, (tm, W3), lambda i: (i, 0)), ((n * HALO, W3), f32, (HALO, W3), lambda i: (i, 0)),
                       ((T, LANES), f32, (tm, LANES), lambda i: (i, 0))],
                      [(tuple(cw.shape), f32), (tuple(hp.shape), f32)])


def _wy_split(x, nc):
    return x.reshape(DN_HEADS * nc, DN_CHUNK, x.shape[-1])


def _wy_fwd(name, q, k, v, gb, bb):
    T = q.shape[1]
    tm = _row_tile(T, 256)
    nc = tm // DN_CHUNK

    def fn(i, *a):
        outs = _wy(*[_wy_split(t, nc) for t in a])
        return [o.reshape(DN_HEADS, tm, o.shape[-1]) for o in outs]

    blk, im = _hm_spec(tm)
    blk2, im2 = _hm_spec(tm, DN_CHUNK)
    ins = [(t, blk, im) for t in (q, k, v, gb, bb)]
    full = (DN_HEADS, T, DN_DK)
    outs = [(full, bf16, blk, im), ((DN_HEADS, T, DN_CHUNK), bf16, blk2, im2), (full, f32, blk, im),
            (full, bf16, blk, im), (full, bf16, blk, im), (full, f32, blk, im)]
    return _tile_call(name, fn, T // tm, ins, outs)


def _wy_bwd(name, q, k, v, gb, bb, cts):
    T = q.shape[1]
    tm = _row_tile(T, 256)
    nc = tm // DN_CHUNK

    def fn(i, q, k, v, gb, bb, *cts):
        _, vjp = jax.vjp(_wy, *[_wy_split(t, nc) for t in (q, k, v, gb, bb)])
        grads = vjp(tuple(_wy_split(c.astype(f32), nc) for c in cts))
        return [g.reshape(DN_HEADS, tm, DN_DK) for g in grads]

    blk, im = _hm_spec(tm)
    blk2, im2 = _hm_spec(tm, DN_CHUNK)
    ins = [(t, blk, im) for t in (q, k, v, gb, bb)]
    ins += [(c, blk2 if c.shape[-1] == DN_CHUNK else blk, im) for c in cts]
    return _tile_call(name, fn, T // tm, ins, [((DN_HEADS, T, DN_DK), f32, blk, im)] * 5)


def _scan_fwd(name, qd, qk, u, w, kt, egl):
    T = qd.shape[1]
    C = DN_CHUNK
    U = _pick(T // C, (SCAN_CHUNKS, 1))
    n = T // (C * U)

    def body(qd_ref, qk_ref, u_ref, w_ref, kt_ref, egl_ref, o_ref, sall_ref, s_ref):
        i = pl.program_id(0)

        @pl.when(i == 0)
        def _():
            s_ref[...] = jnp.zeros(s_ref.shape, f32)

        S = s_ref[...]
        for j in range(U):
            sl = pl.ds(j * C, C)
            sall_ref[:, j] = S
            o, S = _scan_step(S, *[r[:, sl, :] for r in (qd_ref, qk_ref, u_ref, w_ref, kt_ref, egl_ref)])
            o_ref[:, sl, :] = o
        s_ref[...] = S

    blk, im = _hm_spec(C * U)
    blk2, im2 = _hm_spec(C * U, C)
    return pl.pallas_call(
        body, name=name, grid=(n,),
        in_specs=[pl.BlockSpec(blk, im), pl.BlockSpec(blk2, im2)] + [pl.BlockSpec(blk, im)] * 4,
        out_specs=[pl.BlockSpec(blk, im), pl.BlockSpec((DN_HEADS, U, DN_DK, DN_DK), lambda i: (0, i, 0, 0))],
        out_shape=[jax.ShapeDtypeStruct((DN_HEADS, T, DN_DK), f32), jax.ShapeDtypeStruct((DN_HEADS, n * U, DN_DK, DN_DK), f32)],
        scratch_shapes=[pltpu.VMEM((DN_HEADS, DN_DK, DN_DK), f32)],
        compiler_params=pltpu.CompilerParams(dimension_semantics=("arbitrary",), vmem_limit_bytes=VMEM_LIMIT),
    )(qd, qk, u, w, kt, egl)


def _scan_bwd(name, qd, qk, u, w, kt, egl, s_all, do):
    T = qd.shape[1]
    C = DN_CHUNK
    U = _pick(T // C, (SCAN_CHUNKS, 1))
    n = T // (C * U)

    def body(qd_ref, qk_ref, u_ref, w_ref, kt_ref, egl_ref, sall_ref, do_ref,
             dqd_ref, dqk_ref, du_ref, dw_ref, dkt_ref, degl_ref, ds_ref):
        i = pl.program_id(0)

        @pl.when(i == 0)
        def _():
            ds_ref[...] = jnp.zeros(ds_ref.shape, f32)

        dS = ds_ref[...]
        for j in reversed(range(U)):
            sl = pl.ds(j * C, C)
            args = [r[:, sl, :].astype(f32) for r in (qd_ref, qk_ref, u_ref, w_ref, kt_ref, egl_ref)]
            _, vjp = jax.vjp(_scan_step, sall_ref[:, j], *args)
            dS, *cts = vjp((do_ref[:, sl, :], dS))
            for r, v in zip((dqd_ref, dqk_ref, du_ref, dw_ref, dkt_ref, degl_ref), cts):
                r[:, sl, :] = v
        ds_ref[...] = dS

    blk = (DN_HEADS, C * U, DN_DK)
    blk2 = (DN_HEADS, C * U, C)
    rim = lambda i: (0, n - 1 - i, 0)
    sp, sp2 = pl.BlockSpec(blk, rim), pl.BlockSpec(blk2, rim)
    full, full2 = jax.ShapeDtypeStruct((DN_HEADS, T, DN_DK), f32), jax.ShapeDtypeStruct((DN_HEADS, T, C), f32)
    return pl.pallas_call(
        body, name=name, grid=(n,),
        in_specs=[sp, sp2, sp, sp, sp, sp, pl.BlockSpec((DN_HEADS, U, DN_DK, DN_DK), lambda i: (0, n - 1 - i, 0, 0)), sp],
        out_specs=[sp, sp2, sp, sp, sp, sp],
        out_shape=[full, full2, full, full, full, full],
        scratch_shapes=[pltpu.VMEM((DN_HEADS, DN_DK, DN_DK), f32)],
        compiler_params=pltpu.CompilerParams(dimension_semantics=("arbitrary",), vmem_limit_bytes=VMEM_LIMIT),
    )(qd, qk, u, w, kt, egl, s_all, do)


def _dn_post_fwd(name, o, p, nw):
    T = p.shape[0]
    tm = _row_tile(T, 512)
    blk, im = _hm_spec(tm)
    W = DN_HEADS * DN_DK
    return _tile_call(name, lambda i, o, z, nw: _dn_post(o, z, nw), T // tm,
                      [(o, blk, im), (p, (tm, W), lambda i: (i, 1536 // W)), _full(nw)],
                      [((T, W), f32, (tm, W), lambda i: (i, 0))])[0]


def _dn_post_bwd(name, o, p, nw, dout):
    T = p.shape[0]
    tm = _row_tile(T, 512)
    blk, im = _hm_spec(tm)
    W = DN_HEADS * DN_DK

    def fn(i, o, z, nw, dout):
        _, vjp = jax.vjp(_dn_post, o, z, nw)
        return vjp(dout)

    return _tile_call(name, fn, T // tm,
                      [(o, blk, im), (p, (tm, W), lambda i: (i, 1536 // W)), _full(nw), _rows(dout, tm)],
                      [((DN_HEADS, T, DN_DK), f32, blk, im), ((T, W), f32, (tm, W), lambda i: (i, 0))],
                      [((1, DN_DK), f32)])


def _swa_tile(T, big):
    return _pick(T, (2 * WINDOW, WINDOW)) if big else WINDOW


@functools.partial(jax.custom_vjp, nondiff_argnums=(1,))
def _heads_split(x, nh):
    return jnp.stack([x[:, SWA_DH * h:SWA_DH * (h + 1)] for h in range(nh)], axis=0)


@jax.custom_vjp
def _heads_merge(y):
    return jnp.concatenate([y[h] for h in range(y.shape[0])], axis=1)


_heads_split.defvjp(lambda x, nh: (_heads_split(x, nh), None), lambda nh, _, g: (_heads_merge(g),))
_heads_merge.defvjp(lambda y: (_heads_merge(y), None), lambda _, g: (_heads_split(g, g.shape[1] // SWA_DH),))


def _swa_rows(first, qp, kc, kp, vc, vp, snk):
    kv = [_heads_split(t, SWA_KV_HEADS) for t in (kc, kp, vc, vp)]
    return _heads_merge(_swa(first, _heads_split(qp, SWA_HEADS), *kv, snk))


def _swa_ins(p, snk, W):
    qw, kw = SWA_HEADS * SWA_DH, SWA_KV_HEADS * SWA_DH
    return [(p, (W, qw), lambda i: (i, 2048 // qw)),
            (p, (W, kw), lambda i: (i, 6144 // kw)), (p, (W, kw), lambda i: (_prev(i), 6144 // kw)),
            (p, (W, kw), lambda i: (i, 6272 // kw)), (p, (W, kw), lambda i: (_prev(i), 6272 // kw)), _full(snk)]


def _swa_fwd(name, p, snk):
    T = p.shape[0]
    W = _swa_tile(T, True)
    qw = SWA_HEADS * SWA_DH
    return _tile_call(name, lambda i, *a: _swa_rows(i == 0, *a), T // W, _swa_ins(p, snk, W),
                      [((T, qw), f32, (W, qw), lambda i: (i, 0))])[0]


def _swa_bwd(name, p, snk, do):
    T = p.shape[0]
    W = _swa_tile(T, False)
    qw, kw = SWA_HEADS * SWA_DH, SWA_KV_HEADS * SWA_DH

    def fn(i, qp, k_c, k_p, v_c, v_p, snk, do):
        _, vjp = jax.vjp(lambda *a: _swa_rows(i == 0, *a), qp, k_c, k_p, v_c, v_p, snk)
        return vjp(do)

    kv_out = ((T, kw), f32, (W, kw), lambda i: (i, 0))
    return _tile_call(name, fn, T // W, _swa_ins(p, snk, W) + [(do, (W, qw), lambda i: (i, 0))],
                      [((T, qw), bf16, (W, qw), lambda i: (i, 0)), kv_out, kv_out, kv_out, kv_out], [(tuple(snk.shape), f32)])


def _xa_fwd(name, p, kv):
    T = p.shape[0]
    tm = _row_tile(T, 256)
    W = XA_HEADS * XA_DH
    return _tile_call(name, lambda i, q, kv: _xa(q, kv), T // tm, [(p, (tm, W), lambda i: (i, 2560 // W)), _full(kv)],
                      [((T, W), f32, (tm, W), lambda i: (i, 0))])[0]


def _xa_bwd(name, p, kv, do):
    T = p.shape[0]
    tm = _row_tile(T, 256)
    W = XA_HEADS * XA_DH

    def fn(i, q, kv, do):
        _, vjp = jax.vjp(_xa, q, kv)
        return vjp(do)

    return _tile_call(name, fn, T // tm, [(p, (tm, W), lambda i: (i, 2560 // W)), _full(kv), _rows(do, tm)],
                      [((T, W), bf16, (tm, W), lambda i: (i, 0))], [(tuple(kv.shape), f32)])


def _merge_fwd(name, o_dn, o_sw, o_xa, p, wb):
    T = p.shape[0]
    tm = _row_tile(T, 256)
    GW = N_BRANCH * D_MODEL
    return _tile_call(name, lambda i, a, b, c, g, w: _merge(a, b, c, g, w.astype(f32)), T // tm,
                      [_rows(o_dn, tm), _rows(o_sw, tm), _rows(o_xa, tm), (p, (tm, GW), lambda i: (i, 1)), _full(wb)],
                      [((T, D_MODEL), bf16, (tm, D_MODEL), lambda i: (i, 0))])[0]


def _merge_bwd(name, o_dn, o_sw, o_xa, p, wb, dm):
    T = p.shape[0]
    tm = _row_tile(T, 512)
    GW = N_BRANCH * D_MODEL

    def fn(i, a, b, c, g, w, dm):
        _, vjp = jax.vjp(_merge, a, b, c, g, w.astype(f32))
        return vjp(dm)

    bo = ((T, BRANCH_W), f32, (tm, BRANCH_W), lambda i: (i, 0))
    return _tile_call(name, fn, T // tm,
                      [_rows(o_dn, tm), _rows(o_sw, tm), _rows(o_xa, tm), (p, (tm, GW), lambda i: (i, 1)), _full(wb), _rows(dm, tm)],
                      [bo, bo, bo, ((T, GW), bf16, (tm, GW), lambda i: (i, 0))], [(tuple(wb.shape), f32)])


def _assemble_dp(name, dxc, dxp, dz, dswq, dxaq, dgates, dkc, dkp, dvc, dvp, dba):
    T = dz.shape[0]
    tm = _row_tile(T, 256)
    n = T // tm
    r = tm // WINDOW
    nb = T // WINDOW

    def fn(i, dxc, dxp, dz, dswq, dxaq, dgates, dkc, dvc, dba, *nxt):
        halo = jnp.where(i == n - 1, 0.0, dxp)
        dqkv = dxc + jnp.concatenate([jnp.zeros((tm - HALO, dxc.shape[1]), f32), halo], axis=0)
        shifted = [jnp.concatenate([jnp.where(i * r + 1 + b <= nb - 1, blk, 0.0) for b, blk in enumerate(half)], axis=0)
                   for half in (nxt[:r], nxt[r:])]
        parts = [dqkv, dz, dswq, dxaq, dgates, dkc + shifted[0], dvc + shifted[1], dba, jnp.zeros((tm, LANES), f32)]
        return jnp.concatenate(parts, axis=1)

    ins = [_rows(dxc, tm), (dxp, (HALO, dxp.shape[1]), lambda i: (jnp.minimum(i + 1, n - 1), 0))]
    ins += [_rows(t, tm) for t in (dz, dswq, dxaq, dgates, dkc, dvc, dba)]
    for t in (dkp, dvp):
        ins += [(t, (WINDOW, t.shape[1]), lambda i, b=b: (jnp.minimum(i * r + 1 + b, nb - 1), 0)) for b in range(r)]
    return _tile_call(name, fn, n, ins, [((T, D_INP), bf16, (tm, D_INP), lambda i: (i, 0))])[0]


def _loss_grad(name, y, tgt):
    T = y.shape[0]
    tm = _row_tile(T, 512)

    def fn(i, y, t):
        e = y - t
        part = 0.5 * jnp.sum(jnp.mean(e * e, axis=-1, keepdims=True), axis=0, keepdims=True)
        return e * (1.0 / D_MODEL), jnp.broadcast_to(part, (1, LANES))

    return _tile_call(name, fn, T // tm, [_rows(y, tm), _rows(tgt, tm)],
                      [((T, D_MODEL), f32, (tm, D_MODEL), lambda i: (i, 0))], [((1, LANES), f32)])


def _mem_ln_fwd(name, mem, g, b):
    M = mem.shape[0]
    return _tile_call(name, lambda i, m, g, b: (_ln(m, g, b),), 1, [_full(mem), _full(g), _full(b)],
                      [((M, D_MODEL), bf16, (M, D_MODEL), lambda i: (0, 0))])[0]


def _mem_ln_bwd(name, mem, g, b, dmn):
    def fn(i, m, g, b, d):
        _, vjp = jax.vjp(lambda g, b: _ln(m, g, b), g, b)
        return vjp(d)

    return _tile_call(name, fn, 1, [_full(mem), _full(g), _full(b), _full(dmn)], [], [((1, D_MODEL), f32), ((1, D_MODEL), f32)])


def _pad_w_in(w):
    z = jnp.zeros((w.shape[0], D_INP - D_IN), w.dtype)
    return jnp.concatenate([w[:, 0:1536], w[:, 1544:2056], w[:, 2056:2568], w[:, 2824:3336], w[:, 3336:6408],
                            w[:, 2568:2696], w[:, 2696:2824], w[:, 1536:1544], z], axis=1)


def _unpad_dw_in(d):
    return jnp.concatenate([d[:, 0:1536], d[:, 6400:6408], d[:, 1536:2048], d[:, 2048:2560], d[:, 6144:6272],
                            d[:, 6272:6400], d[:, 2560:3072], d[:, 3072:6144]], axis=1)


def _lane_row(v, rows=8):
    out = jnp.zeros((rows, LANES), f32)
    return out.at[0, :v.shape[0]].set(v)


def _layer_fwd(l, x, xb, mem_nb, W):
    n = lambda s: f"l{l}_{s}"
    sv = {}
    u1, a1 = _gu_act(n("ffn1_gu"), xb, W['ffn1_w_gu_p'])
    f1, h1, h1b = _mm_res_ln(n("ffn1_down"), a1, W['ffn1_w_down'], x, W['ln_g'][0:1], W['ln_b'][0:1], 0.5)
    p = _mm(n("w_in"), h1b, W['w_inp'], 'nn', out_dtype=bf16)
    pba = _mm(n("w_in_ba"), h1b, W['w_ba'], 'nn')
    q, k, v, gb, bb = _dn_pre_fwd(n("dn_pre"), p, pba, W['conv_w'], W['hp'])
    qd, qk, u, w, kt, egl = _wy_fwd(n("dn_wy"), q, k, v, gb, bb)
    o_raw, s_all = _scan_fwd(n("dn_scan"), qd, qk, u, w, kt, egl)
    o_dn = _dn_post_fwd(n("dn_post"), o_raw, p, W['norm_w'])
    o_sw = _swa_fwd(n("swa"), p, W['snk'])
    kv = _mm(n("mem_kv"), mem_nb, W['w_mem_kv'], 'nn')
    o_xa = _xa_fwd(n("xa"), p, kv)
    merged = _merge_fwd(n("merge"), o_dn, o_sw, o_xa, p, W['w_branch'])
    mo, h2, h2b = _mm_res_ln(n("w_out"), merged, W['w_out'], h1, W['ln_g'][1:2], W['ln_b'][1:2], 1.0)
    u2, a2 = _gu_act(n("ffn2_gu"), h2b, W['ffn2_w_gu_p'])
    f2, h3, h3b = _mm_res_ln(n("ffn2_down"), a2, W['ffn2_w_down'], h2, W['ln_g'][2:3], W['ln_b'][2:3], 0.5)
    sv = dict(x=x, xb=xb, u1=u1, a1=a1, f1=f1, h1=h1, h1b=h1b, p=p, pba=pba, dn=(q, k, v, gb, bb), wy=(qd, qk, u, w, kt, egl),
              o_raw=o_raw, s_all=s_all, o_dn=o_dn, o_sw=o_sw, kv=kv, o_xa=o_xa, merged=merged,
              mo=mo, h2=h2, h2b=h2b, u2=u2, a2=a2, f2=f2)
    return h3, h3b, sv


def _ffn_bwd(n, tag, x, xb, u, a, f, g, b, w_gu_t, w_down_t, dh):
    dx_a, df, dg, db = _res_ln_bwd(n(f"{tag}_ln_bwd"), x, f, g, b, 0.5, dh)
    d_down = _mm(n(f"{tag}_down_dw"), a, df, 'tn')
    du = _down_dx_act_bwd(n(f"{tag}_down_dx"), df, w_down_t, u)
    d_gu = _mm(n(f"{tag}_gu_dw"), xb, du, 'tn', out_cols=(FF_BLK, lambda j: (j % 2) * (D_FF // FF_BLK) + j // 2))
    dx = _mm(n(f"{tag}_gu_dx"), du, w_gu_t, 'nn', add=dx_a)
    return dx, d_gu, d_down, dg, db


def _layer_bwd(l, sv, mem_nb, W, dh3, dmem_acc):
    n = lambda s: f"l{l}_{s}"
    G = {}
    dh2, G['ffn2_w_gu'], G['ffn2_w_down'], dg2, db2 = _ffn_bwd(
        n, "ffn2", sv['h2'], sv['h2b'], sv['u2'], sv['a2'], sv['f2'], W['ln_g'][2:3], W['ln_b'][2:3], W['ffn2_w_gu_t'], W['ffn2_w_down_t'], dh3)
    dh1_a, dmo, dg1, db1 = _res_ln_bwd(n("ln1_bwd"), sv['h1'], sv['mo'], W['ln_g'][1:2], W['ln_b'][1:2], 1.0, dh2)
    G['w_out'] = _mm(n("w_out_dw"), sv['merged'], dmo, 'tn')
    dmerged = _mm(n("w_out_dx"), dmo, W['w_out_t'], 'nn')
    p = sv['p']
    do_dn, do_sw, do_xa, dgates, G['w_branch'] = _merge_bwd(n("merge_bwd"), sv['o_dn'], sv['o_sw'], sv['o_xa'], p, W['w_branch'], dmerged)
    dxaq, dkv = _xa_bwd(n("xa_bwd"), p, sv['kv'], do_xa)
    dkv = dkv.astype(bf16)
    G['w_mem_kv'] = _mm(n("mem_kv_dw"), mem_nb, dkv, 'tn')
    dmem_n = _mm(n("mem_kv_dx"), dkv, W['w_mem_kv_t'], 'nn', add=dmem_acc)
    dswq, dkc, dkp, dvc, dvp, dsnk = _swa_bwd(n("swa_bwd"), p, W['snk'], do_sw)
    do_raw, dz, dnw = _dn_post_bwd(n("dn_post_bwd"), sv['o_raw'], p, W['norm_w'], do_dn)
    cts = _scan_bwd(n("dn_scan_bwd"), *sv['wy'], sv['s_all'], do_raw)
    dcts = _wy_bwd(n("dn_wy_bwd"), *sv['dn'], cts)
    dxc, dxp, dba, dcw, dhp = _dn_pre_bwd(n("dn_pre_bwd"), p, sv['pba'], W['conv_w'], W['hp'], dcts)
    dp = _assemble_dp(n("dp"), dxc, dxp, dz, dswq, dxaq, dgates, dkc, dkp, dvc, dvp, dba)
    G['w_in'] = _unpad_dw_in(_mm(n("w_in_dw"), sv['h1b'], dp, 'tn'))
    dh1 = _mm(n("w_in_dx"), dp, W['w_inp_t'], 'nn', add=dh1_a)
    dx, G['ffn1_w_gu'], G['ffn1_w_down'], dg0, db0 = _ffn_bwd(
        n, "ffn1", sv['x'], sv['xb'], sv['u1'], sv['a1'], sv['f1'], W['ln_g'][0:1], W['ln_b'][0:1], W['ffn1_w_gu_t'], W['ffn1_w_down_t'], dh1)
    G['ln_g'] = jnp.concatenate([dg0, dg1, dg2], axis=0)
    G['ln_b'] = jnp.concatenate([db0, db1, db2], axis=0)
    G['dn_conv_w'] = dcw
    G['dn_a_log'] = dhp[0, :DN_HEADS]
    G['dn_dt_bias'] = dhp[1, :DN_HEADS]
    G['dn_norm_w'] = dnw[0]
    G['swa_sinks'] = dsnk[:, 0]
    return dx, dmem_n, G


def _local_step(x, mem, tgt, Wf):
    mem_g, mem_b = Wf['mem_ln_g'][None, :], Wf['mem_ln_b'][None, :]
    mem_nb = _mem_ln_fwd("mem_ln", mem, mem_g, mem_b)
    layers = []
    for l in range(DEPTH):
        layers.append(dict(
            ln_g=Wf['ln_g'][l], ln_b=Wf['ln_b'][l], ffn1_w_gu_p=_pair_gate_up(Wf['ffn1_w_gu'][l]), ffn1_w_down=Wf['ffn1_w_down'][l],
            w_inp=_pad_w_in(Wf['w_in'][l]), conv_w=Wf['dn_conv_w'][l],
            hp=jnp.zeros((8, LANES), f32).at[0, :DN_HEADS].set(Wf['dn_a_log'][l]).at[1, :DN_HEADS].set(Wf['dn_dt_bias'][l]),
            norm_w=Wf['dn_norm_w'][l][None, :], snk=jnp.broadcast_to(Wf['swa_sinks'][l][:, None], (SWA_HEADS, LANES)),
            w_mem_kv=Wf['w_mem_kv'][l], w_branch=Wf['w_branch'][l], w_out=Wf['w_out'][l],
            ffn2_w_gu_p=_pair_gate_up(Wf['ffn2_w_gu'][l]), ffn2_w_down=Wf['ffn2_w_down'][l]))
        layers[l]['w_ba'] = layers[l]['w_inp'][:, 6400:6400 + LANES]
        for k in ('ffn1_w_gu_p', 'ffn1_w_down', 'w_inp', 'w_mem_kv', 'w_out', 'ffn2_w_gu_p', 'ffn2_w_down'):
            layers[l][k.replace('_p', '') + '_t'] = layers[l][k].T
    h, hb = x, x.astype(bf16)
    saved = []
    for l in range(DEPTH):
        h, hb, sv = _layer_fwd(l, h, hb, mem_nb, layers[l])
        saved.append(sv)
    dh, loss_row = _loss_grad("loss", h, tgt)
    grads = [None] * DEPTH
    dmem_n = None
    for l in reversed(range(DEPTH)):
        dh, dmem_n, grads[l] = _layer_bwd(l, saved[l], mem_nb, layers[l], dh, dmem_n)
    dmg, dmb = _mem_ln_bwd("mem_ln_bwd", mem, mem_g, mem_b, dmem_n)
    G = {k: jnp.stack([grads[l][k] for l in range(DEPTH)], axis=0) for k in grads[0]}
    G['mem_ln_g'], G['mem_ln_b'] = dmg[0], dmb[0]
    return loss_row, dh, G


SEG_ALIGN = 2048


def _round_up(n, m):
    return (n + m - 1) // m * m


def _layout(names, sizes, mult):
    off, table = 0, {}
    for nm in names:
        table[nm] = (off, sizes[nm])
        off += _round_up(sizes[nm], SEG_ALIGN)
    return table, _round_up(off // LANES, mult)


def _pack(table, rows, flat):
    names = list(table)
    lead = flat[names[0]].shape[:-1]
    parts, pos = [], 0
    for nm in names:
        off, size = table[nm]
        if off > pos:
            parts.append(jnp.zeros(lead + (off - pos,), flat[nm].dtype))
        parts.append(flat[nm])
        pos = off + size
    total = rows * LANES
    if total > pos:
        parts.append(jnp.zeros(lead + (total - pos,), parts[-1].dtype))
    return jnp.concatenate(parts, axis=-1).reshape(lead + (rows, LANES))


def _unpack(table, packed, nm):
    off, size = table[nm]
    flat = packed.reshape(packed.shape[:-2] + (-1,))
    return flat[..., off:off + size]


def _split_shards(full, ax):
    shp = full.shape
    t = full.reshape(shp[:ax] + (N_SHARD, shp[ax] // N_SHARD) + shp[ax + 1:])
    return jnp.moveaxis(t, ax, 0)


def _join_shards(sh4, ax):
    return jnp.concatenate([sh4[s] for s in range(N_SHARD)], axis=ax)


ANY = pl.BlockSpec(memory_space=pl.ANY)


def _me():
    return lax.axis_index("x"), lax.axis_index("y"), lax.axis_index("c")


def _comm_call(name, body, arrays, out_shapes, n_sem):
    n = len(arrays)
    scratch = [pltpu.SemaphoreType.DMA((n, n_sem)), pltpu.SemaphoreType.DMA((n, n_sem))]
    return pl.pallas_call(
        body, name=name, out_shape=out_shapes, in_specs=[ANY] * n, out_specs=[ANY] * n, scratch_shapes=scratch,
        compiler_params=pltpu.CompilerParams(has_side_effects=True),
    )(*arrays)


def _all_gather(name, xs):
    n = len(xs)

    def body(*refs):
        x_refs, out_refs = refs[:n], refs[n:2 * n]
        send_sems, recv_sems = refs[2 * n:]
        mx, my, mc = _me()
        chips = [(1 - mx, my), (mx, 1 - my), (1 - mx, 1 - my)]

        def copy(a, k, shard, half, to, src=None):
            dst = out_refs[a].at[shard, half]
            return pltpu.make_async_remote_copy(src_ref=dst if src is None else src, dst_ref=dst, send_sem=send_sems.at[a, k],
                                                recv_sem=recv_sems.at[a, k], device_id=to, device_id_type=MESH)

        first = [copy(a, j, 2 * mx + my, mc, (cx, cy, mc), src=x_refs[a].at[mc]) for j, (cx, cy) in enumerate(chips) for a in range(n)]
        for cp in first:
            cp.start()
        passed = []
        for j, (cx, cy) in enumerate(chips):
            for a in range(n):
                copy(a, j, 2 * cx + cy, mc, (mx, my, mc)).wait_recv()
                passed.append(copy(a, 3 + j, 2 * cx + cy, mc, (mx, my, 1 - mc)))
                passed[-1].start()
        for j, (cx, cy) in enumerate(chips):
            for a in range(n):
                copy(a, 3 + j, 2 * cx + cy, 1 - mc, (mx, my, mc)).wait_recv()
        for cp in first + passed:
            cp.wait_send()

    return _comm_call(name, body, xs, [jax.ShapeDtypeStruct((N_SHARD,) + x.shape, x.dtype) for x in xs], 6)


def _pair_exchange(name, items):
    n = len(items)

    def body(*refs):
        src_refs, dst_refs = refs[:n], refs[n:2 * n]
        send_sems, recv_sems = refs[2 * n:]
        mx, my, mc = _me()
        cps = [pltpu.make_async_remote_copy(src_ref=src_refs[a].at[s, 1 - mc], dst_ref=dst_refs[a].at[s], send_sem=send_sems.at[a, s],
                                            recv_sem=recv_sems.at[a, s], device_id=(mx, my, 1 - mc), device_id_type=MESH)
               for a in range(n) for s in range(N_SHARD)]
        for cp in cps:
            cp.start()
        for cp in cps:
            cp.wait()

    return _comm_call(name, body, items, [jax.ShapeDtypeStruct((N_SHARD,) + t.shape[2:], t.dtype) for t in items], N_SHARD)


def _chip_exchange(name, parts):
    n = len(parts)

    def body(*refs):
        p_refs, dst_refs = refs[:n], refs[n:2 * n]
        send_sems, recv_sems = refs[2 * n:]
        mx, my, mc = _me()
        chips = [(1 - mx, my), (mx, 1 - my), (1 - mx, 1 - my)]
        cps = [pltpu.make_async_remote_copy(src_ref=p_refs[a].at[2 * cx + cy], dst_ref=dst_refs[a].at[j], send_sem=send_sems.at[a, j],
                                            recv_sem=recv_sems.at[a, j], device_id=(cx, cy, mc), device_id_type=MESH)
               for a in range(n) for j, (cx, cy) in enumerate(chips)]
        for cp in cps:
            cp.start()
        for cp in cps:
            cp.wait()

    return _comm_call(name, body, parts, [jax.ShapeDtypeStruct((3,) + t.shape[1:], t.dtype) for t in parts], 3)


def _pair_swap(name, reds):
    n = len(reds)

    def body(*refs):
        r_refs, out_refs = refs[:n], refs[n:2 * n]
        send_sems, recv_sems = refs[2 * n:]
        mx, my, mc = _me()
        cps = [pltpu.make_async_remote_copy(src_ref=r_refs[a], dst_ref=out_refs[a], send_sem=send_sems.at[a, 0],
                                            recv_sem=recv_sems.at[a, 0], device_id=(mx, my, 1 - mc), device_id_type=MESH)
               for a in range(n)]
        for cp in cps:
            cp.start()
        for cp in cps:
            cp.wait()

    return _comm_call(name, body, reds, [jax.ShapeDtypeStruct(t.shape, t.dtype) for t in reds], 1)


EW_BLOCK_BYTES = 1 << 20


def _ew_call(name, fn, ins, n_out, out_dtype=f32):
    shape = ins[0].shape
    last = shape[-1]
    flat = [a.reshape(-1, last) for a in ins]
    R = flat[0].shape[0]
    cands = tuple(c for c in (4096, 2048, 1024, 512, 256, 128, 64, 32, 16, 8) if c * last * 4 <= EW_BLOCK_BYTES)
    tr = _pick(R, cands)
    res = _tile_call(name, lambda i, *a: fn(*a), R // tr, [(a, (tr, last), lambda i: (i, 0)) for a in flat],
                     [((R, last), out_dtype, (tr, last), lambda i: (i, 0))] * n_out)
    return [r.reshape(shape) for r in res]


def _adamw(g, w, m, v):
    m = B1 * m + (1.0 - B1) * g
    v = B2 * v + (1.0 - B2) * jnp.square(g)
    m_hat = m / (1.0 - B1 ** STEP)
    v_hat = v / (1.0 - B2 ** STEP)
    return -LR * (m_hat / (jnp.sqrt(v_hat) + EPS) + WD * w), m, v


def _adamw_call(name, mine, theirs, w, m, v, mc1):
    shape, last = w.shape, w.shape[-1]
    g2 = [t.reshape(-1, last) for t in (mine, theirs)]
    w3 = [t.reshape(2, -1, last) for t in (w, m, v)]
    R = g2[0].shape[0]
    tr = _pick(R, tuple(c for c in (4096, 2048, 1024, 512, 256, 128, 64, 32, 16, 8) if c * last * 4 <= EW_BLOCK_BYTES))

    def body(mc_ref, mine_ref, theirs_ref, w_ref, m_ref, v_ref, g_out, d_out, m_out, v_out):
        g = jnp.where(pl.program_id(0) == mc_ref[0], mine_ref[...], theirs_ref[...])
        d, nm, nv = _adamw(g, w_ref[0], m_ref[0], v_ref[0])
        g_out[0], d_out[0], m_out[0], v_out[0] = g, d, nm, nv

    half = pl.BlockSpec((tr, last), lambda h, i: (i, 0))
    full = pl.BlockSpec((1, tr, last), lambda h, i: (h, i, 0))
    res = pl.pallas_call(
        body, name=name, grid=(2, R // tr),
        in_specs=[pl.BlockSpec(memory_space=pltpu.SMEM), half, half, full, full, full], out_specs=[full] * 4,
        out_shape=[jax.ShapeDtypeStruct((2, R, last), f32)] * 4,
        compiler_params=pltpu.CompilerParams(dimension_semantics=("arbitrary", "arbitrary"), vmem_limit_bytes=VMEM_LIMIT),
    )(mc1, *g2, *w3)
    return tuple(r.reshape(shape) for r in res)


def kernel(x, mem, mem_ln_g, mem_ln_b, ln_g, ln_b, ffn1_w_gu, ffn1_w_down, w_in, dn_conv_w, dn_a_log, dn_dt_bias, dn_norm_w, swa_sinks, w_mem_kv, w_branch, w_out, ffn2_w_gu, ffn2_w_down, loss_target, m_mem_ln_g, m_mem_ln_b, m_ln_g, m_ln_b, m_ffn1_w_gu, m_ffn1_w_down, m_w_in, m_dn_conv_w, m_dn_a_log, m_dn_dt_bias, m_dn_norm_w, m_swa_sinks, m_w_mem_kv, m_w_branch, m_w_out, m_ffn2_w_gu, m_ffn2_w_down, v_mem_ln_g, v_mem_ln_b, v_ln_g, v_ln_b, v_ffn1_w_gu, v_ffn1_w_down, v_w_in, v_dn_conv_w, v_dn_a_log, v_dn_dt_bias, v_dn_norm_w, v_swa_sinks, v_w_mem_kv, v_w_branch, v_w_out, v_ffn2_w_gu, v_ffn2_w_down):
    args = dict(locals())
    Ws = {n: args[n] for n in WEIGHTS}
    Ms = {n: args["m_" + n] for n in WEIGHTS}
    Vs = {n: args["v_" + n] for n in WEIGHTS}
    mc = lax.axis_index("c")
    my_s = 2 * lax.axis_index("x") + lax.axis_index("y")
    small = [n for n in WEIGHTS if n not in MATRICES]

    ag_table, ag_rows = _layout(SMALL_SHARDED, {n: Ws[n].size for n in SMALL_SHARDED}, 16)
    ag_small = _pack(ag_table, ag_rows, {n: Ws[n].reshape(-1) for n in SMALL_SHARDED}).reshape(2, ag_rows // 2, LANES)
    local = [Ws[n].astype(bf16) for n in MATRICES] + [ag_small]
    gathered = _all_gather("all_gather_w", local)
    gathered = [jnp.stack([jnp.where(my_s == s, loc, g[s]) for s in range(N_SHARD)], axis=0) for loc, g in zip(local, gathered)]
    Wf = {n: _join_shards(g, SHARD_AXIS[n]) for n, g in zip(MATRICES, gathered)}
    g_small = gathered[-1].reshape(N_SHARD, ag_rows, LANES)
    for n in SMALL_SHARDED:
        Wf[n] = _join_shards(_unpack(ag_table, g_small, n).reshape((N_SHARD,) + Ws[n].shape), SHARD_AXIS[n])
    for n in WEIGHTS:
        if SHARD_AXIS[n] is None:
            Wf[n] = Ws[n]

    loss_row, dx, G = _local_step(x[0], mem[0], loss_target[0], Wf)

    table, rows = _layout(small + ['loss'], {**{n: Ws[n].size for n in small}, 'loss': 1}, 16)
    gflat = {n: (jnp.broadcast_to(G[n].reshape(1, -1), (N_SHARD, G[n].size)) if SHARD_AXIS[n] is None
                 else _split_shards(G[n], SHARD_AXIS[n]).reshape(N_SHARD, -1)) for n in small}
    gflat['loss'] = jnp.broadcast_to(loss_row[:, :1], (N_SHARD, 1))
    items = [_split_shards(G[n], SHARD_AXIS[n]) for n in MATRICES] + [_pack(table, rows, gflat).reshape(N_SHARD, 2, rows // 2, LANES)]
    tags = MATRICES + ['small']
    got = _pair_exchange("rs_pair", items)
    keep = [lax.dynamic_index_in_dim(a, mc, axis=1, keepdims=False) for a in items]
    wire = [bf16] * len(MATRICES) + [f32]
    part = [_ew_call(f"rs_add_pair_{t}", lambda a, b: a + b, [k, g], 1, out_dtype=dt)[0] for t, k, g, dt in zip(tags, keep, got, wire)]
    others = _chip_exchange("rs_chips", part)
    own = lambda a: lax.dynamic_index_in_dim(a, my_s, axis=0, keepdims=False)
    red = [_ew_call(f"rs_add_chips_{t}", lambda k, g, fx, fy, fxy: ((k + g) + fy) + (fx + fxy), [own(k), own(g), o[0], o[1], o[2]], 1)[0]
           for t, k, g, o in zip(tags, keep, got, others)]
    theirs = _pair_swap("rs_swap", red)

    mc1 = mc.astype(i32).reshape(1)
    outs = {}
    for n, a, b in zip(MATRICES, red, theirs):
        outs[n] = _adamw_call(f"adamw_{n}", a, b, Ws[n], Ms[n], Vs[n], mc1)
    fill = {'loss': jnp.zeros((1,), f32)}
    packs = [_pack(table, rows, {**{n: src[n].reshape(-1) for n in small}, **fill}).reshape(2, rows // 2, LANES) for src in (Ws, Ms, Vs)]
    small_out = [p.reshape(rows, LANES) for p in _adamw_call("adamw_small", red[-1], theirs[-1], *packs, mc1)]
    for n in small:
        outs[n] = tuple(_unpack(table, p, n).reshape(Ws[n].shape) for p in small_out)
    loss = _unpack(table, small_out[0], 'loss').reshape(())
    return (loss, dx[None], *[outs[n][k] for k in range(4) for n in WEIGHTS])
```

```python
import functools

import jax
import jax.numpy as jnp
from jax import lax
from jax.experimental import pallas as pl
from jax.experimental.pallas import tpu as pltpu

f32, bf16, i32 = jnp.float32, jnp.bfloat16, jnp.int32
HI = lax.Precision.HIGHEST
MESH = pl.DeviceIdType.MESH

D_MODEL = 1024
DEPTH = 2
DN_HEADS, DN_DK, DN_CONV, DN_CHUNK = 4, 128, 4, 64
SWA_HEADS, SWA_KV_HEADS, SWA_DH, WINDOW = 8, 2, 64, 128
XA_HEADS, XA_DH = 4, 128
D_FF = 2816
N_BRANCH, BRANCH_W = 3, 512
ALPHA = (2 * DEPTH) ** 0.25
LN_EPS, RMS_EPS, NEG_INF = 1e-5, 1e-6, -1e30
D_IN = 6408
D_INP = 6656
LR, B1, B2, EPS, WD, STEP = 0.001, 0.9, 0.999, 1e-08, 0.01, 10

LANES = 128
VMEM_LIMIT = 56 << 20
N_SHARD = 4
SCAN_CHUNKS = 4
HALO = 8
MM_TILES = {
    'l1_ffn1_gu_dx': (2048, 512, 2816), 'l1_ffn2_gu_dx': (1024, 1024, 2816), 'l1_w_in_dx': (2048, 512, 1664),
    'l1_w_in': (2048, 512, 1024), 'l1_ffn1_gu_dw': (1024, 1408, 1024), 'l1_ffn2_gu_dw': (512, 1408, 4096),
    'l1_ffn1_down_dw': (1408, 1024, 2048), 'l1_ffn2_down_dw': (1408, 512, 4096), 'l1_w_in_dw': (1024, 512, 4096),
    'l1_ffn1_gu': (256,), 'l1_ffn1_down_dx': (512,), 'l1_ffn1_down': (256,), 'l1_ffn2_down': (1024,),
}

WEIGHTS = ['mem_ln_g', 'mem_ln_b', 'ln_g', 'ln_b', 'ffn1_w_gu', 'ffn1_w_down', 'w_in', 'dn_conv_w', 'dn_a_log',
           'dn_dt_bias', 'dn_norm_w', 'swa_sinks', 'w_mem_kv', 'w_branch', 'w_out', 'ffn2_w_gu', 'ffn2_w_down']
SHARD_AXIS = {'mem_ln_g': None, 'mem_ln_b': None, 'ln_g': 2, 'ln_b': 2, 'ffn1_w_gu': 2, 'ffn1_w_down': 1, 'w_in': 2,
              'dn_conv_w': 2, 'dn_a_log': None, 'dn_dt_bias': None, 'dn_norm_w': None, 'swa_sinks': None,
              'w_mem_kv': 1, 'w_branch': 3, 'w_out': 1, 'ffn2_w_gu': 2, 'ffn2_w_down': 1}
MATRICES = ['ffn1_w_gu', 'ffn1_w_down', 'w_in', 'w_mem_kv', 'w_branch', 'w_out', 'ffn2_w_gu', 'ffn2_w_down']
SMALL_SHARDED = ['ln_g', 'ln_b', 'dn_conv_w']


def _dg(a, b, mode, hi):
    nb = a.ndim - 2
    bd = tuple(range(nb))
    ca = nb if mode == 'tn' else nb + 1
    cb = nb + 1 if mode == 'nt' else nb
    dims = (((ca,), (cb,)), (bd, bd))
    dot = lambda x, y: lax.dot_general(x, y, dims, preferred_element_type=f32)
    a_hi, b_hi = a.astype(bf16), b.astype(bf16)
    if not hi:
        return dot(a_hi, b_hi)
    a_lo = (a.astype(f32) - a_hi.astype(f32)).astype(bf16)
    b_lo = (b.astype(f32) - b_hi.astype(f32)).astype(bf16)
    return dot(a_hi, b_hi) + (dot(a_hi, b_lo) + dot(a_lo, b_hi))


@functools.partial(jax.custom_vjp, nondiff_argnums=(2, 3))
def _dot(a, b, mode, hi):
    return _dg(a, b, mode, hi)


def _dot_fwd(a, b, mode, hi):
    return _dg(a, b, mode, hi), (a, b)


def _dot_bwd(mode, hi, res, g):
    a, b = res
    if mode == 'nn':
        da, db = _dg(g, b, 'nt', hi), _dg(a, g, 'tn', hi)
    elif mode == 'nt':
        da, db = _dg(g, b, 'nn', hi), _dg(g, a, 'tn', hi)
    else:
        da, db = _dg(b, g, 'nt', hi), _dg(a, g, 'nn', hi)
    return da.astype(a.dtype), db.astype(b.dtype)


_dot.defvjp(_dot_fwd, _dot_bwd)


def bdot(a, b, mode):
    return _dot(a, b, mode, False)


def hdot(a, b, mode):
    return _dot(a, b, mode, True)


def _shift_rows_impl(x, xprev, s):
    rows = lax.broadcasted_iota(i32, x.shape, 0)
    return jnp.where(rows >= s, pltpu.roll(x, s, 0), pltpu.roll(xprev, s, 0))


@functools.partial(jax.custom_vjp, nondiff_argnums=(2,))
def _shift_rows(x, xprev, s):
    return _shift_rows_impl(x, xprev, s)


def _shift_rows_fwd(x, xprev, s):
    return _shift_rows_impl(x, xprev, s), None


def _shift_rows_bwd(s, _, g):
    n = g.shape[0]
    rows = lax.broadcasted_iota(i32, g.shape, 0)
    r = pltpu.roll(g, n - s, 0)
    keep = rows < n - s
    return jnp.where(keep, r, 0.0), jnp.where(keep, 0.0, r)


_shift_rows.defvjp(_shift_rows_fwd, _shift_rows_bwd)


def _lane_pick(x, lane):
    idx = lax.broadcasted_iota(i32, x.shape, x.ndim - 1)
    return jnp.sum(jnp.where(idx == lane, x, 0.0), axis=-1, keepdims=True)


def _silu(x):
    return x * jax.nn.sigmoid(x)


def _tri_inv(a):
    C = a.shape[-1]
    eye = (lax.broadcasted_iota(i32, a.shape, 1) == lax.broadcasted_iota(i32, a.shape, 2)).astype(f32)
    p = -a
    x = eye + p
    for _ in range((C - 1).bit_length() - 1):
        p = _dg(p, p, 'nn', True)
        x = x + _dg(x, p, 'nn', True)
    return x


@jax.custom_vjp
def _tri_solve(a, rhs):
    return _dg(_tri_inv(a), rhs, 'nn', True)


def _tri_solve_fwd(a, rhs):
    tinv = _tri_inv(a)
    sol = _dg(tinv, rhs, 'nn', True)
    return sol, (tinv, sol)


def _tri_solve_bwd(res, g):
    tinv, sol = res
    d_rhs = _dg(tinv, g, 'tn', True)
    return -_dg(d_rhs, sol, 'nt', True), d_rhs


_tri_solve.defvjp(_tri_solve_fwd, _tri_solve_bwd)


@functools.partial(jax.custom_vjp, nondiff_argnums=(1,))
def _lane_head(x, n):
    return x[:, :, :n]


def _lane_head_fwd(x, n):
    return x[:, :, :n], None


def _lane_head_bwd(n, _, g):
    s = jnp.sum(g, axis=-1, keepdims=True) * (1.0 / LANES)
    return (jnp.broadcast_to(s, g.shape[:-1] + (LANES,)),)


_lane_head.defvjp(_lane_head_fwd, _lane_head_bwd)


@functools.partial(jax.custom_vjp, nondiff_argnums=(1,))
def _last_row(x, c):
    return x[:, c - 1:, :]


def _last_row_fwd(x, c):
    return x[:, c - 1:, :], None


def _last_row_bwd(c, _, g):
    shape = (g.shape[0], c, g.shape[2])
    rows = lax.broadcasted_iota(i32, shape, 1)
    return (jnp.where(rows == c - 1, jnp.broadcast_to(g, shape), 0.0),)


_last_row.defvjp(_last_row_fwd, _last_row_bwd)


def _full(a):
    nd = a.ndim
    return (a, tuple(a.shape), lambda i, _nd=nd: (0,) * _nd)


def _rows(a, tm, col=0, width=None):
    width = a.shape[1] if width is None else width
    return (a, (tm, width), lambda i, _c=col: (i, _c))


def _tile_call(name, fn, n, ins, outs, accs=()):
    n_in, n_out, n_acc = len(ins), len(outs), len(accs)

    def body(*refs):
        i = pl.program_id(0)
        res = fn(i, *[r[...].astype(f32) for r in refs[:n_in]])
        if not isinstance(res, (tuple, list)):
            res = (res,)
        assert len(res) == n_out + n_acc, (name, len(res), n_out, n_acc)
        for r, v in zip(refs[n_in:n_in + n_out], res[:n_out]):
            r[...] = v.astype(r.dtype)
        if n_acc:
            acc_refs = refs[n_in + n_out:]

            @pl.when(i == 0)
            def _():
                for r in acc_refs:
                    r[...] = jnp.zeros(r.shape, r.dtype)

            for r, v in zip(acc_refs, res[n_out:]):
                r[...] += v.astype(r.dtype)

    out_shape = [jax.ShapeDtypeStruct(s, d) for s, d, _, _ in outs] + [jax.ShapeDtypeStruct(s, d) for s, d in accs]
    out_specs = [pl.BlockSpec(b, m) for _, _, b, m in outs]
    out_specs += [pl.BlockSpec(tuple(s), lambda i, _nd=len(s): (0,) * _nd) for s, _ in accs]
    res = pl.pallas_call(
        body, name=name, grid=(n,),
        in_specs=[pl.BlockSpec(b, m) for _, b, m in ins],
        out_specs=out_specs, out_shape=out_shape,
        compiler_params=pltpu.CompilerParams(dimension_semantics=("arbitrary",), vmem_limit_bytes=VMEM_LIMIT),
    )(*[a for a, _, _ in ins])
    return res


def _pick(n, cands):
    for c in cands:
        if n % c == 0:
            return c
    return n


def _mm(name, a, b, mode, out_dtype=f32, add=None, out_cols=None):
    if mode == 'tn':
        K, M = a.shape
    else:
        M, K = a.shape
    N = b.shape[0] if mode == 'nt' else b.shape[1]
    tm = _pick(M, (1024, 1408, 512, 256, 128))
    tn = _pick(N, (512, 1408, 256, 128)) if out_cols is None else out_cols[0]
    tuned = MM_TILES.get(name)
    if tuned is not None and (M % tuned[0] or N % tuned[1] or K % tuned[2]):
        tuned = None
    if tuned is not None:
        tm, tn = tuned[:2]
    col = (lambda j: j) if out_cols is None else out_cols[1]
    tk = K if K <= 3328 else _pick(K, (3328, 2816, 2048, 1024, 512, 256, 128))
    if tuned is not None:
        tk = tuned[2]
    nk = K // tk
    ca = 0 if mode == 'tn' else 1
    cb = 1 if mode == 'nt' else 0
    dims = (((ca,), (cb,)), ((), ()))

    def body(*refs):
        a_ref, b_ref = refs[0], refs[1]
        add_ref = refs[2] if add is not None else None
        part = lax.dot_general(a_ref[...].astype(bf16), b_ref[...].astype(bf16), dims, preferred_element_type=f32)

        def finish(r, o_ref):
            if add_ref is not None:
                r = r + add_ref[...].astype(f32)
            o_ref[...] = r.astype(o_ref.dtype)

        if nk == 1:
            finish(part, refs[-1])
            return
        o_ref, acc_ref = refs[-2], refs[-1]
        k = pl.program_id(2)

        @pl.when(k == 0)
        def _():
            acc_ref[...] = part

        @pl.when(k > 0)
        def _():
            acc_ref[...] += part

        @pl.when(k == nk - 1)
        def _():
            finish(acc_ref[...], o_ref)

    a_spec = pl.BlockSpec((tk, tm), lambda i, j, k: (k, i)) if mode == 'tn' else pl.BlockSpec((tm, tk), lambda i, j, k: (i, k))
    b_spec = pl.BlockSpec((tn, tk), lambda i, j, k: (j, k)) if mode == 'nt' else pl.BlockSpec((tk, tn), lambda i, j, k: (k, j))
    in_specs, args = [a_spec, b_spec], [a, b]
    if add is not None:
        in_specs.append(pl.BlockSpec((tm, tn), lambda i, j, k: (i, j)))
        args.append(add)
    return pl.pallas_call(
        body, name=name, grid=(M // tm, N // tn, nk), in_specs=in_specs,
        out_specs=pl.BlockSpec((tm, tn), lambda i, j, k: (i, col(j))),
        out_shape=jax.ShapeDtypeStruct((M, N), out_dtype),
        scratch_shapes=[pltpu.VMEM((tm, tn), f32)] if nk > 1 else [],
        compiler_params=pltpu.CompilerParams(dimension_semantics=("parallel", "parallel", "arbitrary"), vmem_limit_bytes=VMEM_LIMIT),
    )(*args)


def _swiglu(u):
    return _silu(u[:, :D_FF]) * u[:, D_FF:]


def _res_ln(x, f, g, b, s):
    r = ALPHA * x + s * f
    mu = jnp.mean(r, axis=-1, keepdims=True)
    rc = r - mu
    var = jnp.mean(rc * rc, axis=-1, keepdims=True)
    return rc * lax.rsqrt(var + LN_EPS) * g + b


def _ln(x, g, b):
    mu = jnp.mean(x, axis=-1, keepdims=True)
    xc = x - mu
    var = jnp.mean(xc * xc, axis=-1, keepdims=True)
    return xc * lax.rsqrt(var + LN_EPS) * g + b


def _dn_pre(first, xc, xp, ba, cw, hp):
    xp = jnp.where(first, 0.0, xp)
    y = cw[DN_CONV - 1:DN_CONV, :] * xc
    for j in range(DN_CONV - 1):
        y = y + cw[j:j + 1, :] * _shift_rows(xc, xp, DN_CONV - 1 - j)
    c = _silu(y)
    qs, ks, vs, gs, bs = [], [], [], [], []
    nqk = DN_HEADS * DN_DK
    for h in range(DN_HEADS):
        q = c[:, h * DN_DK:(h + 1) * DN_DK]
        k = c[:, nqk + h * DN_DK:nqk + (h + 1) * DN_DK]
        v = c[:, 2 * nqk + h * DN_DK:2 * nqk + (h + 1) * DN_DK]
        qs.append(q * lax.rsqrt(jnp.sum(q * q, axis=-1, keepdims=True) + RMS_EPS))
        ks.append(k * lax.rsqrt(jnp.sum(k * k, axis=-1, keepdims=True) + RMS_EPS))
        vs.append(v)
        beta = jax.nn.sigmoid(_lane_pick(ba, h))
        a_log = _lane_pick(hp[0:1, :], h)
        dt = _lane_pick(hp[1:2, :], h)
        g = -jnp.exp(a_log) * jax.nn.softplus(_lane_pick(ba, DN_HEADS + h) + dt)
        gs.append(jnp.broadcast_to(g, q.shape))
        bs.append(jnp.broadcast_to(beta, q.shape))
    return tuple(jnp.stack(t, axis=0) for t in (qs, ks, vs, gs, bs))


def _wy(q, k, v, gb, bb):
    B, C, _ = q.shape
    ri = lax.broadcasted_iota(i32, (B, C, C), 1)
    ci = lax.broadcasted_iota(i32, (B, C, C), 2)
    tril, strict = ri >= ci, ri > ci
    gc = hdot(tril.astype(f32), gb, 'nn')
    gl = jnp.broadcast_to(_last_row(gc, C), gc.shape)
    col = _lane_head(gc, C)
    decay = jnp.exp(jnp.where(tril, col - jnp.swapaxes(col, 1, 2), NEG_INF))
    qs = q * (DN_DK ** -0.5)
    kb = k * bb
    a = jnp.where(strict, bdot(kb, k, 'nt') * decay, 0.0)
    sol = _tri_solve(a, jnp.concatenate([v * bb, kb * jnp.exp(gc)], axis=-1))
    qk = jnp.where(tril, bdot(qs, k, 'nt') * decay, 0.0)
    kt = k * jnp.exp(gl - gc)
    qd = qs * jnp.exp(gc)
    return qd, qk, sol[..., :DN_DK], sol[..., DN_DK:], kt, jnp.exp(gl)


def _scan_step(S, qd, qk, u, w, kt, egl):
    vn = u - bdot(w, S, 'nn')
    o = bdot(qd, S, 'nn') + bdot(qk, vn, 'nn')
    e2 = jnp.concatenate([egl] * (DN_DK // DN_CHUNK), axis=1)
    return o, S * e2 + bdot(kt, vn, 'tn')


def _dn_post(o, z, nw):
    outs = []
    for h in range(DN_HEADS):
        oh = o[h]
        oh = oh * lax.rsqrt(jnp.mean(oh * oh, axis=-1, keepdims=True) + RMS_EPS) * nw
        outs.append(oh * _silu(z[:, h * DN_DK:(h + 1) * DN_DK]))
    return jnp.concatenate(outs, axis=1)


def _swa(first, q, kc, kp, vc, vp, snk):
    W = q.shape[1]
    G = SWA_HEADS // SWA_KV_HEADS
    r = lax.broadcasted_iota(i32, (G, W, 2 * W), 1)
    c = lax.broadcasted_iota(i32, (G, W, 2 * W), 2)
    mask = (c > W + r - WINDOW) & (c <= W + r) & jnp.logical_or(c >= W, jnp.logical_not(first))
    sink_all = _lane_pick(snk, 0)
    outs = []
    for j in range(SWA_KV_HEADS):
        qj = q[j * G:(j + 1) * G]
        kk = jnp.broadcast_to(jnp.concatenate([kp[j], kc[j]], axis=0)[None], (G, 2 * W, SWA_DH))
        vv = jnp.broadcast_to(jnp.concatenate([vp[j], vc[j]], axis=0)[None], (G, 2 * W, SWA_DH))
        s = jnp.where(mask, bdot(qj, kk, 'nt') * (SWA_DH ** -0.5), NEG_INF)
        sink = sink_all[j * G:(j + 1) * G][:, :, None]
        m = jnp.maximum(jnp.max(s, axis=-1, keepdims=True), sink)
        p = jnp.exp(s - m)
        p = p / (jnp.sum(p, axis=-1, keepdims=True) + jnp.exp(sink - m))
        outs.append(bdot(p, vv, 'nn'))
    return jnp.concatenate(outs, axis=0)


def _xa(q, kv):
    outs = []
    nk = XA_HEADS * XA_DH
    for h in range(XA_HEADS):
        qh = q[:, h * XA_DH:(h + 1) * XA_DH]
        kh = kv[:, h * XA_DH:(h + 1) * XA_DH]
        vh = kv[:, nk + h * XA_DH:nk + (h + 1) * XA_DH]
        s = bdot(qh, kh, 'nt') * (XA_DH ** -0.5)
        m = jnp.max(s, axis=-1, keepdims=True)
        p = jnp.exp(s - m)
        p = p / jnp.sum(p, axis=-1, keepdims=True)
        outs.append(bdot(p, vh, 'nn'))
    return jnp.concatenate(outs, axis=1)


def _merge(o_dn, o_sw, o_xa, gates, wb):
    acc = None
    for n, o in enumerate((o_dn, o_sw, o_xa)):
        t = jax.nn.sigmoid(gates[:, n * D_MODEL:(n + 1) * D_MODEL]) * bdot(o, wb[n], 'nn')
        acc = t if acc is None else acc + t
    return acc


def _row_tile(T, want):
    return _pick(T, tuple(c for c in (1024, 512, 256, 128, 64) if c <= want))


def _res_ln_fwd(name, x, f, g, b, s):
    T = x.shape[0]
    tm = _row_tile(T, 512)

    def fn(i, x, f, g, b):
        h = _res_ln(x, f, g, b, s)
        return h, h

    return _tile_call(name, fn, T // tm, [_rows(x, tm), _rows(f, tm), _full(g), _full(b)],
                      [((T, D_MODEL), f32, (tm, D_MODEL), lambda i: (i, 0)), ((T, D_MODEL), bf16, (tm, D_MODEL), lambda i: (i, 0))])


def _res_ln_bwd(name, x, f, g, b, s, dh):
    T = x.shape[0]
    tm = _row_tile(T, 512)

    def fn(i, x, f, g, b, dh):
        _, vjp = jax.vjp(lambda x, f, g, b: _res_ln(x, f, g, b, s), x, f, g, b)
        return vjp(dh)

    return _tile_call(name, fn, T // tm, [_rows(x, tm), _rows(f, tm), _full(g), _full(b), _rows(dh, tm)],
                      [((T, D_MODEL), f32, (tm, D_MODEL), lambda i: (i, 0)), ((T, D_MODEL), bf16, (tm, D_MODEL), lambda i: (i, 0))],
                      [((1, D_MODEL), f32), ((1, D_MODEL), f32)])


FF_BLK = D_FF // 2


def _pair_gate_up(w):
    d = w.shape[0]
    return jnp.stack([w[:, :D_FF].reshape(d, -1, FF_BLK), w[:, D_FF:].reshape(d, -1, FF_BLK)], axis=2).reshape(d, 2 * D_FF)


def _swiglu_blk(u):
    return _silu(u[:, :FF_BLK]) * u[:, FF_BLK:]


def _mm_spec(tm, tn):
    return dict(compiler_params=pltpu.CompilerParams(dimension_semantics=("parallel", "arbitrary"), vmem_limit_bytes=VMEM_LIMIT))


def _gu_act(name, xb, w_pair):
    T, D = xb.shape
    tm = _pick(T, (MM_TILES.get(name, (512,))[0], 512, 256, 128))

    def body(x_ref, w_ref, u_ref, a_ref):
        r = lax.dot_general(x_ref[...], w_ref[...], (((1,), (0,)), ((), ())), preferred_element_type=f32)
        u_ref[...] = r.astype(bf16)
        a_ref[...] = _swiglu_blk(r).astype(bf16)

    return pl.pallas_call(
        body, name=name, grid=(T // tm, D_FF // FF_BLK),
        in_specs=[pl.BlockSpec((tm, D), lambda i, j: (i, 0)), pl.BlockSpec((D, 2 * FF_BLK), lambda i, j: (0, j))],
        out_specs=[pl.BlockSpec((tm, 2 * FF_BLK), lambda i, j: (i, j)), pl.BlockSpec((tm, FF_BLK), lambda i, j: (i, j))],
        out_shape=[jax.ShapeDtypeStruct((T, 2 * D_FF), bf16), jax.ShapeDtypeStruct((T, D_FF), bf16)],
        **_mm_spec(tm, FF_BLK))(xb, w_pair)


def _down_dx_act_bwd(name, df, w_down_t, u):
    T, D = df.shape
    tm = _pick(T, (MM_TILES.get(name, (256,))[0], 256, 128))

    def body(df_ref, w_ref, u_ref, du_ref):
        da = lax.dot_general(df_ref[...], w_ref[...], (((1,), (0,)), ((), ())), preferred_element_type=f32)
        _, vjp = jax.vjp(_swiglu_blk, u_ref[...].astype(f32))
        du_ref[...] = vjp(da)[0].astype(bf16)

    return pl.pallas_call(
        body, name=name, grid=(T // tm, D_FF // FF_BLK),
        in_specs=[pl.BlockSpec((tm, D), lambda i, j: (i, 0)), pl.BlockSpec((D, FF_BLK), lambda i, j: (0, j)),
                  pl.BlockSpec((tm, 2 * FF_BLK), lambda i, j: (i, j))],
        out_specs=pl.BlockSpec((tm, 2 * FF_BLK), lambda i, j: (i, j)),
        out_shape=jax.ShapeDtypeStruct((T, 2 * D_FF), bf16),
        **_mm_spec(tm, FF_BLK))(df, w_down_t, u)


def _mm_res_ln(name, a, w, x, g, b, s):
    T, K = a.shape
    tm = _pick(T, (MM_TILES.get(name, (512,))[0], 512, 256, 128))

    def body(a_ref, w_ref, x_ref, g_ref, b_ref, f_ref, h_ref, hb_ref):
        f = lax.dot_general(a_ref[...], w_ref[...], (((1,), (0,)), ((), ())), preferred_element_type=f32)
        h = _res_ln(x_ref[...], f, g_ref[...], b_ref[...], s)
        f_ref[...] = f
        h_ref[...] = h
        hb_ref[...] = h.astype(bf16)

    row = pl.BlockSpec((tm, D_MODEL), lambda i: (i, 0))
    vec = pl.BlockSpec((1, D_MODEL), lambda i: (0, 0))
    return pl.pallas_call(
        body, name=name, grid=(T // tm,),
        in_specs=[pl.BlockSpec((tm, K), lambda i: (i, 0)), pl.BlockSpec((K, D_MODEL), lambda i: (0, 0)), row, vec, vec],
        out_specs=[row, row, row],
        out_shape=[jax.ShapeDtypeStruct((T, D_MODEL), f32), jax.ShapeDtypeStruct((T, D_MODEL), f32), jax.ShapeDtypeStruct((T, D_MODEL), bf16)],
        compiler_params=pltpu.CompilerParams(dimension_semantics=("parallel",), vmem_limit_bytes=VMEM_LIMIT))(a, w, x, g, b)


def _hm_spec(tm, w=DN_DK):
    return ((DN_HEADS, tm, w), lambda i: (0, i, 0))


def _prev(i):
    return jnp.maximum(i - 1, 0)


def _dn_front(first, xc, xp, ba, cw, hp):
    tm = xc.shape[0]
    nc = tm // DN_CHUNK
    outs = _wy(*[t.reshape(DN_HEADS * nc, DN_CHUNK, DN_DK) for t in _dn_pre(first, xc, xp, ba, cw, hp)])
    return tuple(o.reshape(DN_HEADS, tm, o.shape[-1]) for o in outs)


def _dn_front_ins(p, pba, cw, hp, tm):
    W3 = 3 * DN_HEADS * DN_DK
    return [(p, (tm, W3), lambda i: (i, 0)), (p, (tm, W3), lambda i: (_prev(i), 0)),
            (pba, (tm, LANES), lambda i: (i, 0)), _full(cw), _full(hp)]


def _dn_front_fwd(name, p, pba, cw, hp):
    T = p.shape[0]
    tm = _row_tile(T, 256)
    blk, im = _hm_spec(tm)
    blk2, im2 = _hm_spec(tm, DN_CHUNK)
    full = (DN_HEADS, T, DN_DK)
    outs = [(full, bf16, blk, im), ((DN_HEADS, T, DN_CHUNK), bf16, blk2, im2), (full, f32, blk, im),
            (full, bf16, blk, im), (full, bf16, blk, im), (full, f32, blk, im)]
    return _tile_call(name, lambda i, *a: _dn_front(i == 0, *a), T // tm, _dn_front_ins(p, pba, cw, hp, tm), outs)


def _dn_front_bwd(name, p, pba, cw, hp, cts):
    T = p.shape[0]
    tm = _row_tile(T, 256)
    n = T // tm
    W3 = 3 * DN_HEADS * DN_DK
    blk, im = _hm_spec(tm)
    blk2, im2 = _hm_spec(tm, DN_CHUNK)
    ins = _dn_front_ins(p, pba, cw, hp, tm) + [(c, blk2 if c.shape[-1] == DN_CHUNK else blk, im) for c in cts]

    def fn(i, xc, xp, ba, cw, hp, *cts):
        _, vjp = jax.vjp(lambda *a: _dn_front(i == 0, *a), xc, xp, ba, cw, hp)
        dxc, dxp, dba, dcw, dhp = vjp(tuple(cts))
        return dxc, dxp[tm - HALO:], dba, dcw, dhp

    return _tile_call(name, fn, n, ins,
                      [((T, W3), f32, (tm, W3), lambda i: (i, 0)), ((n * HALO, W3), f32, (HALO, W3), lambda i: (i, 0)),
                       ((T, LANES), f32, (tm, LANES), lambda i: (i, 0))],
                      [(tuple(cw.shape), f32), (tuple(hp.shape), f32)])


def _scan_fwd(name, qd, qk, u, w, kt, egl):
    T = qd.shape[1]
    C = DN_CHUNK
    U = _pick(T // C, (SCAN_CHUNKS, 1))
    n = T // (C * U)

    def body(qd_ref, qk_ref, u_ref, w_ref, kt_ref, egl_ref, o_ref, sall_ref, s_ref):
        i = pl.program_id(0)

        @pl.when(i == 0)
        def _():
            s_ref[...] = jnp.zeros(s_ref.shape, f32)

        S = s_ref[...]
        for j in range(U):
            sl = pl.ds(j * C, C)
            sall_ref[:, j] = S
            o, S = _scan_step(S, *[r[:, sl, :] for r in (qd_ref, qk_ref, u_ref, w_ref, kt_ref, egl_ref)])
            o_ref[:, sl, :] = o
        s_ref[...] = S

    blk, im = _hm_spec(C * U)
    blk2, im2 = _hm_spec(C * U, C)
    return pl.pallas_call(
        body, name=name, grid=(n,),
        in_specs=[pl.BlockSpec(blk, im), pl.BlockSpec(blk2, im2)] + [pl.BlockSpec(blk, im)] * 4,
        out_specs=[pl.BlockSpec(blk, im), pl.BlockSpec((DN_HEADS, U, DN_DK, DN_DK), lambda i: (0, i, 0, 0))],
        out_shape=[jax.ShapeDtypeStruct((DN_HEADS, T, DN_DK), f32), jax.ShapeDtypeStruct((DN_HEADS, n * U, DN_DK, DN_DK), f32)],
        scratch_shapes=[pltpu.VMEM((DN_HEADS, DN_DK, DN_DK), f32)],
        compiler_params=pltpu.CompilerParams(dimension_semantics=("arbitrary",), vmem_limit_bytes=VMEM_LIMIT),
    )(qd, qk, u, w, kt, egl)


def _scan_bwd(name, qd, qk, u, w, kt, egl, s_all, do):
    T = qd.shape[1]
    C = DN_CHUNK
    U = _pick(T // C, (SCAN_CHUNKS, 1))
    n = T // (C * U)

    def body(qd_ref, qk_ref, u_ref, w_ref, kt_ref, egl_ref, sall_ref, do_ref,
             dqd_ref, dqk_ref, du_ref, dw_ref, dkt_ref, degl_ref, ds_ref):
        i = pl.program_id(0)

        @pl.when(i == 0)
        def _():
            ds_ref[...] = jnp.zeros(ds_ref.shape, f32)

        dS = ds_ref[...]
        for j in reversed(range(U)):
            sl = pl.ds(j * C, C)
            args = [r[:, sl, :].astype(f32) for r in (qd_ref, qk_ref, u_ref, w_ref, kt_ref, egl_ref)]
            _, vjp = jax.vjp(_scan_step, sall_ref[:, j], *args)
            dS, *cts = vjp((do_ref[:, sl, :], dS))
            for r, v in zip((dqd_ref, dqk_ref, du_ref, dw_ref, dkt_ref, degl_ref), cts):
                r[:, sl, :] = v
        ds_ref[...] = dS

    blk = (DN_HEADS, C * U, DN_DK)
    blk2 = (DN_HEADS, C * U, C)
    rim = lambda i: (0, n - 1 - i, 0)
    sp, sp2 = pl.BlockSpec(blk, rim), pl.BlockSpec(blk2, rim)
    full, full2 = jax.ShapeDtypeStruct((DN_HEADS, T, DN_DK), f32), jax.ShapeDtypeStruct((DN_HEADS, T, C), f32)
    return pl.pallas_call(
        body, name=name, grid=(n,),
        in_specs=[sp, sp2, sp, sp, sp, sp, pl.BlockSpec((DN_HEADS, U, DN_DK, DN_DK), lambda i: (0, n - 1 - i, 0, 0)), sp],
        out_specs=[sp, sp2, sp, sp, sp, sp],
        out_shape=[full, full2, full, full, full, full],
        scratch_shapes=[pltpu.VMEM((DN_HEADS, DN_DK, DN_DK), f32)],
        compiler_params=pltpu.CompilerParams(dimension_semantics=("arbitrary",), vmem_limit_bytes=VMEM_LIMIT),
    )(qd, qk, u, w, kt, egl, s_all, do)


def _dn_post_fwd(name, o, p, nw):
    T = p.shape[0]
    tm = _row_tile(T, 512)
    blk, im = _hm_spec(tm)
    W = DN_HEADS * DN_DK
    return _tile_call(name, lambda i, o, z, nw: _dn_post(o, z, nw), T // tm,
                      [(o, blk, im), (p, (tm, W), lambda i: (i, 1536 // W)), _full(nw)],
                      [((T, W), f32, (tm, W), lambda i: (i, 0))])[0]


def _dn_post_bwd(name, o, p, nw, dout):
    T = p.shape[0]
    tm = _row_tile(T, 512)
    blk, im = _hm_spec(tm)
    W = DN_HEADS * DN_DK

    def fn(i, o, z, nw, dout):
        _, vjp = jax.vjp(_dn_post, o, z, nw)
        return vjp(dout)

    return _tile_call(name, fn, T // tm,
                      [(o, blk, im), (p, (tm, W), lambda i: (i, 1536 // W)), _full(nw), _rows(dout, tm)],
                      [((DN_HEADS, T, DN_DK), f32, blk, im), ((T, W), f32, (tm, W), lambda i: (i, 0))],
                      [((1, DN_DK), f32)])


def _swa_tile(T, big):
    return _pick(T, (2 * WINDOW, WINDOW)) if big else WINDOW


@functools.partial(jax.custom_vjp, nondiff_argnums=(1,))
def _heads_split(x, nh):
    return jnp.stack([x[:, SWA_DH * h:SWA_DH * (h + 1)] for h in range(nh)], axis=0)


@jax.custom_vjp
def _heads_merge(y):
    return jnp.concatenate([y[h] for h in range(y.shape[0])], axis=1)


_heads_split.defvjp(lambda x, nh: (_heads_split(x, nh), None), lambda nh, _, g: (_heads_merge(g),))
_heads_merge.defvjp(lambda y: (_heads_merge(y), None), lambda _, g: (_heads_split(g, g.shape[1] // SWA_DH),))


def _swa_rows(first, qp, kc, kp, vc, vp, snk):
    kv = [_heads_split(t, SWA_KV_HEADS) for t in (kc, kp, vc, vp)]
    return _heads_merge(_swa(first, _heads_split(qp, SWA_HEADS), *kv, snk))


def _swa_ins(p, snk, W):
    qw, kw = SWA_HEADS * SWA_DH, SWA_KV_HEADS * SWA_DH
    return [(p, (W, qw), lambda i: (i, 2048 // qw)),
            (p, (W, kw), lambda i: (i, 6144 // kw)), (p, (W, kw), lambda i: (_prev(i), 6144 // kw)),
            (p, (W, kw), lambda i: (i, 6272 // kw)), (p, (W, kw), lambda i: (_prev(i), 6272 // kw)), _full(snk)]


def _swa_fwd(name, p, snk):
    T = p.shape[0]
    W = _swa_tile(T, True)
    qw = SWA_HEADS * SWA_DH
    return _tile_call(name, lambda i, *a: _swa_rows(i == 0, *a), T // W, _swa_ins(p, snk, W),
                      [((T, qw), f32, (W, qw), lambda i: (i, 0))])[0]


def _swa_bwd(name, p, snk, do):
    T = p.shape[0]
    W = _swa_tile(T, False)
    qw, kw = SWA_HEADS * SWA_DH, SWA_KV_HEADS * SWA_DH

    def fn(i, qp, k_c, k_p, v_c, v_p, snk, do):
        _, vjp = jax.vjp(lambda *a: _swa_rows(i == 0, *a), qp, k_c, k_p, v_c, v_p, snk)
        return vjp(do)

    kv_out = ((T, kw), f32, (W, kw), lambda i: (i, 0))
    return _tile_call(name, fn, T // W, _swa_ins(p, snk, W) + [(do, (W, qw), lambda i: (i, 0))],
                      [((T, qw), bf16, (W, qw), lambda i: (i, 0)), kv_out, kv_out, kv_out, kv_out], [(tuple(snk.shape), f32)])


def _xa_fwd(name, p, kv):
    T = p.shape[0]
    tm = _row_tile(T, 256)
    W = XA_HEADS * XA_DH
    return _tile_call(name, lambda i, q, kv: _xa(q, kv), T // tm, [(p, (tm, W), lambda i: (i, 2560 // W)), _full(kv)],
                      [((T, W), f32, (tm, W), lambda i: (i, 0))])[0]


def _xa_bwd(name, p, kv, do):
    T = p.shape[0]
    tm = _row_tile(T, 256)
    W = XA_HEADS * XA_DH

    def fn(i, q, kv, do):
        _, vjp = jax.vjp(_xa, q, kv)
        return vjp(do)

    return _tile_call(name, fn, T // tm, [(p, (tm, W), lambda i: (i, 2560 // W)), _full(kv), _rows(do, tm)],
                      [((T, W), bf16, (tm, W), lambda i: (i, 0))], [(tuple(kv.shape), f32)])


def _merge_fwd(name, o_dn, o_sw, o_xa, p, wb):
    T = p.shape[0]
    tm = _row_tile(T, 256)
    GW = N_BRANCH * D_MODEL
    return _tile_call(name, lambda i, a, b, c, g, w: _merge(a, b, c, g, w.astype(f32)), T // tm,
                      [_rows(o_dn, tm), _rows(o_sw, tm), _rows(o_xa, tm), (p, (tm, GW), lambda i: (i, 1)), _full(wb)],
                      [((T, D_MODEL), bf16, (tm, D_MODEL), lambda i: (i, 0))])[0]


def _merge_bwd(name, o_dn, o_sw, o_xa, p, wb, dm):
    T = p.shape[0]
    tm = _row_tile(T, 512)
    GW = N_BRANCH * D_MODEL

    def fn(i, a, b, c, g, w, dm):
        _, vjp = jax.vjp(_merge, a, b, c, g, w.astype(f32))
        return vjp(dm)

    bo = ((T, BRANCH_W), f32, (tm, BRANCH_W), lambda i: (i, 0))
    return _tile_call(name, fn, T // tm,
                      [_rows(o_dn, tm), _rows(o_sw, tm), _rows(o_xa, tm), (p, (tm, GW), lambda i: (i, 1)), _full(wb), _rows(dm, tm)],
                      [bo, bo, bo, ((T, GW), bf16, (tm, GW), lambda i: (i, 0))], [(tuple(wb.shape), f32)])


def _assemble_dp(name, dxc, dxp, dz, dswq, dxaq, dgates, dkc, dkp, dvc, dvp, dba):
    T = dz.shape[0]
    tm = _row_tile(T, 256)
    n = T // tm
    r = tm // WINDOW
    nb = T // WINDOW

    def fn(i, dxc, dxp, dz, dswq, dxaq, dgates, dkc, dvc, dba, *nxt):
        halo = jnp.where(i == n - 1, 0.0, dxp)
        dqkv = dxc + jnp.concatenate([jnp.zeros((tm - HALO, dxc.shape[1]), f32), halo], axis=0)
        shifted = [jnp.concatenate([jnp.where(i * r + 1 + b <= nb - 1, blk, 0.0) for b, blk in enumerate(half)], axis=0)
                   for half in (nxt[:r], nxt[r:])]
        parts = [dqkv, dz, dswq, dxaq, dgates, dkc + shifted[0], dvc + shifted[1], dba, jnp.zeros((tm, LANES), f32)]
        return jnp.concatenate(parts, axis=1)

    ins = [_rows(dxc, tm), (dxp, (HALO, dxp.shape[1]), lambda i: (jnp.minimum(i + 1, n - 1), 0))]
    ins += [_rows(t, tm) for t in (dz, dswq, dxaq, dgates, dkc, dvc, dba)]
    for t in (dkp, dvp):
        ins += [(t, (WINDOW, t.shape[1]), lambda i, b=b: (jnp.minimum(i * r + 1 + b, nb - 1), 0)) for b in range(r)]
    return _tile_call(name, fn, n, ins, [((T, D_INP), bf16, (tm, D_INP), lambda i: (i, 0))])[0]


def _loss_grad(name, y, tgt):
    T = y.shape[0]
    tm = _row_tile(T, 512)

    def fn(i, y, t):
        e = y - t
        part = 0.5 * jnp.sum(jnp.mean(e * e, axis=-1, keepdims=True), axis=0, keepdims=True)
        return e * (1.0 / D_MODEL), jnp.broadcast_to(part, (1, LANES))

    return _tile_call(name, fn, T // tm, [_rows(y, tm), _rows(tgt, tm)],
                      [((T, D_MODEL), f32, (tm, D_MODEL), lambda i: (i, 0))], [((1, LANES), f32)])


def _mem_ln_fwd(name, mem, g, b):
    M = mem.shape[0]
    return _tile_call(name, lambda i, m, g, b: (_ln(m, g, b),), 1, [_full(mem), _full(g), _full(b)],
                      [((M, D_MODEL), bf16, (M, D_MODEL), lambda i: (0, 0))])[0]


def _mem_ln_bwd(name, mem, g, b, dmn):
    def fn(i, m, g, b, d):
        _, vjp = jax.vjp(lambda g, b: _ln(m, g, b), g, b)
        return vjp(d)

    return _tile_call(name, fn, 1, [_full(mem), _full(g), _full(b), _full(dmn)], [], [((1, D_MODEL), f32), ((1, D_MODEL), f32)])


def _pad_w_in(w):
    z = jnp.zeros((w.shape[0], D_INP - D_IN), w.dtype)
    return jnp.concatenate([w[:, 0:1536], w[:, 1544:2056], w[:, 2056:2568], w[:, 2824:3336], w[:, 3336:6408],
                            w[:, 2568:2696], w[:, 2696:2824], w[:, 1536:1544], z], axis=1)


def _unpad_dw_in(d):
    return jnp.concatenate([d[:, 0:1536], d[:, 6400:6408], d[:, 1536:2048], d[:, 2048:2560], d[:, 6144:6272],
                            d[:, 6272:6400], d[:, 2560:3072], d[:, 3072:6144]], axis=1)


def _lane_row(v, rows=8):
    out = jnp.zeros((rows, LANES), f32)
    return out.at[0, :v.shape[0]].set(v)


def _layer_fwd(l, x, xb, mem_nb, W):
    n = lambda s: f"l{l}_{s}"
    sv = {}
    u1, a1 = _gu_act(n("ffn1_gu"), xb, W['ffn1_w_gu_p'])
    f1, h1, h1b = _mm_res_ln(n("ffn1_down"), a1, W['ffn1_w_down'], x, W['ln_g'][0:1], W['ln_b'][0:1], 0.5)
    p = _mm(n("w_in"), h1b, W['w_inp'], 'nn', out_dtype=bf16)
    pba = _mm(n("w_in_ba"), h1b, W['w_ba'], 'nn')
    qd, qk, u, w, kt, egl = _dn_front_fwd(n("dn_front"), p, pba, W['conv_w'], W['hp'])
    o_raw, s_all = _scan_fwd(n("dn_scan"), qd, qk, u, w, kt, egl)
    o_dn = _dn_post_fwd(n("dn_post"), o_raw, p, W['norm_w'])
    o_sw = _swa_fwd(n("swa"), p, W['snk'])
    kv = _mm(n("mem_kv"), mem_nb, W['w_mem_kv'], 'nn')
    o_xa = _xa_fwd(n("xa"), p, kv)
    merged = _merge_fwd(n("merge"), o_dn, o_sw, o_xa, p, W['w_branch'])
    mo, h2, h2b = _mm_res_ln(n("w_out"), merged, W['w_out'], h1, W['ln_g'][1:2], W['ln_b'][1:2], 1.0)
    u2, a2 = _gu_act(n("ffn2_gu"), h2b, W['ffn2_w_gu_p'])
    f2, h3, h3b = _mm_res_ln(n("ffn2_down"), a2, W['ffn2_w_down'], h2, W['ln_g'][2:3], W['ln_b'][2:3], 0.5)
    sv = dict(x=x, xb=xb, u1=u1, a1=a1, f1=f1, h1=h1, h1b=h1b, p=p, pba=pba, wy=(qd, qk, u, w, kt, egl),
              o_raw=o_raw, s_all=s_all, o_dn=o_dn, o_sw=o_sw, kv=kv, o_xa=o_xa, merged=merged,
              mo=mo, h2=h2, h2b=h2b, u2=u2, a2=a2, f2=f2)
    return h3, h3b, sv


def _ffn_bwd(n, tag, x, xb, u, a, f, g, b, w_gu_t, w_down_t, dh):
    dx_a, df, dg, db = _res_ln_bwd(n(f"{tag}_ln_bwd"), x, f, g, b, 0.5, dh)
    d_down = _mm(n(f"{tag}_down_dw"), a, df, 'tn')
    du = _down_dx_act_bwd(n(f"{tag}_down_dx"), df, w_down_t, u)
    d_gu = _mm(n(f"{tag}_gu_dw"), xb, du, 'tn', out_cols=(FF_BLK, lambda j: (j % 2) * (D_FF // FF_BLK) + j // 2))
    dx = _mm(n(f"{tag}_gu_dx"), du, w_gu_t, 'nn', add=dx_a)
    return dx, d_gu, d_down, dg, db


def _layer_bwd(l, sv, mem_nb, W, dh3, dmem_acc):
    n = lambda s: f"l{l}_{s}"
    G = {}
    dh2, G['ffn2_w_gu'], G['ffn2_w_down'], dg2, db2 = _ffn_bwd(
        n, "ffn2", sv['h2'], sv['h2b'], sv['u2'], sv['a2'], sv['f2'], W['ln_g'][2:3], W['ln_b'][2:3], W['ffn2_w_gu_t'], W['ffn2_w_down_t'], dh3)
    dh1_a, dmo, dg1, db1 = _res_ln_bwd(n("ln1_bwd"), sv['h1'], sv['mo'], W['ln_g'][1:2], W['ln_b'][1:2], 1.0, dh2)
    G['w_out'] = _mm(n("w_out_dw"), sv['merged'], dmo, 'tn')
    dmerged = _mm(n("w_out_dx"), dmo, W['w_out_t'], 'nn')
    p = sv['p']
    do_dn, do_sw, do_xa, dgates, G['w_branch'] = _merge_bwd(n("merge_bwd"), sv['o_dn'], sv['o_sw'], sv['o_xa'], p, W['w_branch'], dmerged)
    dxaq, dkv = _xa_bwd(n("xa_bwd"), p, sv['kv'], do_xa)
    dkv = dkv.astype(bf16)
    G['w_mem_kv'] = _mm(n("mem_kv_dw"), mem_nb, dkv, 'tn')
    dmem_n = _mm(n("mem_kv_dx"), dkv, W['w_mem_kv_t'], 'nn', add=dmem_acc)
    dswq, dkc, dkp, dvc, dvp, dsnk = _swa_bwd(n("swa_bwd"), p, W['snk'], do_sw)
    do_raw, dz, dnw = _dn_post_bwd(n("dn_post_bwd"), sv['o_raw'], p, W['norm_w'], do_dn)
    cts = _scan_bwd(n("dn_scan_bwd"), *sv['wy'], sv['s_all'], do_raw)
    dxc, dxp, dba, dcw, dhp = _dn_front_bwd(n("dn_front_bwd"), p, sv['pba'], W['conv_w'], W['hp'], cts)
    dp = _assemble_dp(n("dp"), dxc, dxp, dz, dswq, dxaq, dgates, dkc, dkp, dvc, dvp, dba)
    G['w_in'] = _unpad_dw_in(_mm(n("w_in_dw"), sv['h1b'], dp, 'tn'))
    dh1 = _mm(n("w_in_dx"), dp, W['w_inp_t'], 'nn', add=dh1_a)
    dx, G['ffn1_w_gu'], G['ffn1_w_down'], dg0, db0 = _ffn_bwd(
        n, "ffn1", sv['x'], sv['xb'], sv['u1'], sv['a1'], sv['f1'], W['ln_g'][0:1], W['ln_b'][0:1], W['ffn1_w_gu_t'], W['ffn1_w_down_t'], dh1)
    G['ln_g'] = jnp.concatenate([dg0, dg1, dg2], axis=0)
    G['ln_b'] = jnp.concatenate([db0, db1, db2], axis=0)
    G['dn_conv_w'] = dcw
    G['dn_a_log'] = dhp[0, :DN_HEADS]
    G['dn_dt_bias'] = dhp[1, :DN_HEADS]
    G['dn_norm_w'] = dnw[0]
    G['swa_sinks'] = dsnk[:, 0]
    return dx, dmem_n, G


def _local_step(x, mem, tgt, Wf):
    mem_g, mem_b = Wf['mem_ln_g'][None, :], Wf['mem_ln_b'][None, :]
    mem_nb = _mem_ln_fwd("mem_ln", mem, mem_g, mem_b)
    layers = []
    for l in range(DEPTH):
        layers.append(dict(
            ln_g=Wf['ln_g'][l], ln_b=Wf['ln_b'][l], ffn1_w_gu_p=_pair_gate_up(Wf['ffn1_w_gu'][l]), ffn1_w_down=Wf['ffn1_w_down'][l],
            w_inp=_pad_w_in(Wf['w_in'][l]), conv_w=Wf['dn_conv_w'][l],
            hp=jnp.zeros((8, LANES), f32).at[0, :DN_HEADS].set(Wf['dn_a_log'][l]).at[1, :DN_HEADS].set(Wf['dn_dt_bias'][l]),
            norm_w=Wf['dn_norm_w'][l][None, :], snk=jnp.broadcast_to(Wf['swa_sinks'][l][:, None], (SWA_HEADS, LANES)),
            w_mem_kv=Wf['w_mem_kv'][l], w_branch=Wf['w_branch'][l], w_out=Wf['w_out'][l],
            ffn2_w_gu_p=_pair_gate_up(Wf['ffn2_w_gu'][l]), ffn2_w_down=Wf['ffn2_w_down'][l]))
        layers[l]['w_ba'] = layers[l]['w_inp'][:, 6400:6400 + LANES]
        for k in ('ffn1_w_gu_p', 'ffn1_w_down', 'w_inp', 'w_mem_kv', 'w_out', 'ffn2_w_gu_p', 'ffn2_w_down'):
            layers[l][k.replace('_p', '') + '_t'] = layers[l][k].T
    h, hb = x, x.astype(bf16)
    saved = []
    for l in range(DEPTH):
        h, hb, sv = _layer_fwd(l, h, hb, mem_nb, layers[l])
        saved.append(sv)
    dh, loss_row = _loss_grad("loss", h, tgt)
    grads = [None] * DEPTH
    dmem_n = None
    for l in reversed(range(DEPTH)):
        dh, dmem_n, grads[l] = _layer_bwd(l, saved[l], mem_nb, layers[l], dh, dmem_n)
    dmg, dmb = _mem_ln_bwd("mem_ln_bwd", mem, mem_g, mem_b, dmem_n)
    G = {k: jnp.stack([grads[l][k] for l in range(DEPTH)], axis=0) for k in grads[0]}
    G['mem_ln_g'], G['mem_ln_b'] = dmg[0], dmb[0]
    return loss_row, dh, G


SEG_ALIGN = 2048


def _round_up(n, m):
    return (n + m - 1) // m * m


def _layout(names, sizes, mult):
    off, table = 0, {}
    for nm in names:
        table[nm] = (off, sizes[nm])
        off += _round_up(sizes[nm], SEG_ALIGN)
    return table, _round_up(off // LANES, mult)


def _pack(table, rows, flat):
    names = list(table)
    lead = flat[names[0]].shape[:-1]
    parts, pos = [], 0
    for nm in names:
        off, size = table[nm]
        if off > pos:
            parts.append(jnp.zeros(lead + (off - pos,), flat[nm].dtype))
        parts.append(flat[nm])
        pos = off + size
    total = rows * LANES
    if total > pos:
        parts.append(jnp.zeros(lead + (total - pos,), parts[-1].dtype))
    return jnp.concatenate(parts, axis=-1).reshape(lead + (rows, LANES))


def _unpack(table, packed, nm):
    off, size = table[nm]
    flat = packed.reshape(packed.shape[:-2] + (-1,))
    return flat[..., off:off + size]


def _split_shards(full, ax):
    shp = full.shape
    t = full.reshape(shp[:ax] + (N_SHARD, shp[ax] // N_SHARD) + shp[ax + 1:])
    return jnp.moveaxis(t, ax, 0)


def _join_shards(sh4, ax):
    return jnp.concatenate([sh4[s] for s in range(N_SHARD)], axis=ax)


ANY = pl.BlockSpec(memory_space=pl.ANY)


def _me():
    return lax.axis_index("x"), lax.axis_index("y"), lax.axis_index("c")


def _comm_call(name, body, arrays, out_shapes, n_sem):
    n = len(arrays)
    scratch = [pltpu.SemaphoreType.DMA((n, n_sem)), pltpu.SemaphoreType.DMA((n, n_sem))]
    return pl.pallas_call(
        body, name=name, out_shape=out_shapes, in_specs=[ANY] * n, out_specs=[ANY] * n, scratch_shapes=scratch,
        compiler_params=pltpu.CompilerParams(has_side_effects=True),
    )(*arrays)


def _all_gather(name, xs):
    n = len(xs)

    def body(*refs):
        x_refs, out_refs = refs[:n], refs[n:2 * n]
        send_sems, recv_sems = refs[2 * n:]
        mx, my, mc = _me()
        chips = [(1 - mx, my), (mx, 1 - my), (1 - mx, 1 - my)]

        def copy(a, k, shard, half, to, src=None):
            dst = out_refs[a].at[shard, half]
            return pltpu.make_async_remote_copy(src_ref=dst if src is None else src, dst_ref=dst, send_sem=send_sems.at[a, k],
                                                recv_sem=recv_sems.at[a, k], device_id=to, device_id_type=MESH)

        first = [copy(a, j, 2 * mx + my, mc, (cx, cy, mc), src=x_refs[a].at[mc]) for j, (cx, cy) in enumerate(chips) for a in range(n)]
        for cp in first:
            cp.start()
        passed = []
        for j, (cx, cy) in enumerate(chips):
            for a in range(n):
                copy(a, j, 2 * cx + cy, mc, (mx, my, mc)).wait_recv()
                passed.append(copy(a, 3 + j, 2 * cx + cy, mc, (mx, my, 1 - mc)))
                passed[-1].start()
        for j, (cx, cy) in enumerate(chips):
            for a in range(n):
                copy(a, 3 + j, 2 * cx + cy, 1 - mc, (mx, my, mc)).wait_recv()
        for cp in first + passed:
            cp.wait_send()

    return _comm_call(name, body, xs, [jax.ShapeDtypeStruct((N_SHARD,) + x.shape, x.dtype) for x in xs], 6)


def _pair_exchange(name, items):
    n = len(items)

    def body(*refs):
        src_refs, dst_refs = refs[:n], refs[n:2 * n]
        send_sems, recv_sems = refs[2 * n:]
        mx, my, mc = _me()
        cps = [pltpu.make_async_remote_copy(src_ref=src_refs[a].at[s, 1 - mc], dst_ref=dst_refs[a].at[s], send_sem=send_sems.at[a, s],
                                            recv_sem=recv_sems.at[a, s], device_id=(mx, my, 1 - mc), device_id_type=MESH)
               for a in range(n) for s in range(N_SHARD)]
        for cp in cps:
            cp.start()
        for cp in cps:
            cp.wait()

    return _comm_call(name, body, items, [jax.ShapeDtypeStruct((N_SHARD,) + t.shape[2:], t.dtype) for t in items], N_SHARD)


def _chip_exchange(name, parts):
    n = len(parts)

    def body(*refs):
        p_refs, dst_refs = refs[:n], refs[n:2 * n]
        send_sems, recv_sems = refs[2 * n:]
        mx, my, mc = _me()
        chips = [(1 - mx, my), (mx, 1 - my), (1 - mx, 1 - my)]
        cps = [pltpu.make_async_remote_copy(src_ref=p_refs[a].at[2 * cx + cy], dst_ref=dst_refs[a].at[j], send_sem=send_sems.at[a, j],
                                            recv_sem=recv_sems.at[a, j], device_id=(cx, cy, mc), device_id_type=MESH)
               for a in range(n) for j, (cx, cy) in enumerate(chips)]
        for cp in cps:
            cp.start()
        for cp in cps:
            cp.wait()

    return _comm_call(name, body, parts, [jax.ShapeDtypeStruct((3,) + t.shape[1:], t.dtype) for t in parts], 3)


def _pair_swap(name, reds):
    n = len(reds)

    def body(*refs):
        r_refs, out_refs = refs[:n], refs[n:2 * n]
        send_sems, recv_sems = refs[2 * n:]
        mx, my, mc = _me()
        cps = [pltpu.make_async_remote_copy(src_ref=r_refs[a], dst_ref=out_refs[a], send_sem=send_sems.at[a, 0],
                                            recv_sem=recv_sems.at[a, 0], device_id=(mx, my, 1 - mc), device_id_type=MESH)
               for a in range(n)]
        for cp in cps:
            cp.start()
        for cp in cps:
            cp.wait()

    return _comm_call(name, body, reds, [jax.ShapeDtypeStruct(t.shape, t.dtype) for t in reds], 1)


EW_BLOCK_BYTES = 1 << 20


def _ew_call(name, fn, ins, n_out, out_dtype=f32):
    shape = ins[0].shape
    last = shape[-1]
    flat = [a.reshape(-1, last) for a in ins]
    R = flat[0].shape[0]
    cands = tuple(c for c in (4096, 2048, 1024, 512, 256, 128, 64, 32, 16, 8) if c * last * 4 <= EW_BLOCK_BYTES)
    tr = _pick(R, cands)
    res = _tile_call(name, lambda i, *a: fn(*a), R // tr, [(a, (tr, last), lambda i: (i, 0)) for a in flat],
                     [((R, last), out_dtype, (tr, last), lambda i: (i, 0))] * n_out)
    return [r.reshape(shape) for r in res]


def _adamw(g, w, m, v):
    m = B1 * m + (1.0 - B1) * g
    v = B2 * v + (1.0 - B2) * jnp.square(g)
    m_hat = m / (1.0 - B1 ** STEP)
    v_hat = v / (1.0 - B2 ** STEP)
    return -LR * (m_hat / (jnp.sqrt(v_hat) + EPS) + WD * w), m, v


def _adamw_call(name, mine, theirs, w, m, v, mc1):
    shape, last = w.shape, w.shape[-1]
    g2 = [t.reshape(-1, last) for t in (mine, theirs)]
    w3 = [t.reshape(2, -1, last) for t in (w, m, v)]
    R = g2[0].shape[0]
    tr = _pick(R, tuple(c for c in (4096, 2048, 1024, 512, 256, 128, 64, 32, 16, 8) if c * last * 4 <= EW_BLOCK_BYTES))

    def body(mc_ref, mine_ref, theirs_ref, w_ref, m_ref, v_ref, g_out, d_out, m_out, v_out):
        g = jnp.where(pl.program_id(0) == mc_ref[0], mine_ref[...], theirs_ref[...])
        d, nm, nv = _adamw(g, w_ref[0], m_ref[0], v_ref[0])
        g_out[0], d_out[0], m_out[0], v_out[0] = g, d, nm, nv

    half = pl.BlockSpec((tr, last), lambda h, i: (i, 0))
    full = pl.BlockSpec((1, tr, last), lambda h, i: (h, i, 0))
    res = pl.pallas_call(
        body, name=name, grid=(2, R // tr),
        in_specs=[pl.BlockSpec(memory_space=pltpu.SMEM), half, half, full, full, full], out_specs=[full] * 4,
        out_shape=[jax.ShapeDtypeStruct((2, R, last), f32)] * 4,
        compiler_params=pltpu.CompilerParams(dimension_semantics=("arbitrary", "arbitrary"), vmem_limit_bytes=VMEM_LIMIT),
    )(mc1, *g2, *w3)
    return tuple(r.reshape(shape) for r in res)


def kernel(x, mem, mem_ln_g, mem_ln_b, ln_g, ln_b, ffn1_w_gu, ffn1_w_down, w_in, dn_conv_w, dn_a_log, dn_dt_bias, dn_norm_w, swa_sinks, w_mem_kv, w_branch, w_out, ffn2_w_gu, ffn2_w_down, loss_target, m_mem_ln_g, m_mem_ln_b, m_ln_g, m_ln_b, m_ffn1_w_gu, m_ffn1_w_down, m_w_in, m_dn_conv_w, m_dn_a_log, m_dn_dt_bias, m_dn_norm_w, m_swa_sinks, m_w_mem_kv, m_w_branch, m_w_out, m_ffn2_w_gu, m_ffn2_w_down, v_mem_ln_g, v_mem_ln_b, v_ln_g, v_ln_b, v_ffn1_w_gu, v_ffn1_w_down, v_w_in, v_dn_conv_w, v_dn_a_log, v_dn_dt_bias, v_dn_norm_w, v_swa_sinks, v_w_mem_kv, v_w_branch, v_w_out, v_ffn2_w_gu, v_ffn2_w_down):
    args = dict(locals())
    Ws = {n: args[n] for n in WEIGHTS}
    Ms = {n: args["m_" + n] for n in WEIGHTS}
    Vs = {n: args["v_" + n] for n in WEIGHTS}
    mc = lax.axis_index("c")
    my_s = 2 * lax.axis_index("x") + lax.axis_index("y")
    small = [n for n in WEIGHTS if n not in MATRICES]

    ag_table, ag_rows = _layout(SMALL_SHARDED, {n: Ws[n].size for n in SMALL_SHARDED}, 16)
    ag_small = _pack(ag_table, ag_rows, {n: Ws[n].reshape(-1) for n in SMALL_SHARDED}).reshape(2, ag_rows // 2, LANES)
    local = [Ws[n].astype(bf16) for n in MATRICES] + [ag_small]
    gathered = _all_gather("all_gather_w", local)
    gathered = [jnp.stack([jnp.where(my_s == s, loc, g[s]) for s in range(N_SHARD)], axis=0) for loc, g in zip(local, gathered)]
    Wf = {n: _join_shards(g, SHARD_AXIS[n]) for n, g in zip(MATRICES, gathered)}
    g_small = gathered[-1].reshape(N_SHARD, ag_rows, LANES)
    for n in SMALL_SHARDED:
        Wf[n] = _join_shards(_unpack(ag_table, g_small, n).reshape((N_SHARD,) + Ws[n].shape), SHARD_AXIS[n])
    for n in WEIGHTS:
        if SHARD_AXIS[n] is None:
            Wf[n] = Ws[n]

    loss_row, dx, G = _local_step(x[0], mem[0], loss_target[0], Wf)

    table, rows = _layout(small + ['loss'], {**{n: Ws[n].size for n in small}, 'loss': 1}, 16)
    gflat = {n: (jnp.broadcast_to(G[n].reshape(1, -1), (N_SHARD, G[n].size)) if SHARD_AXIS[n] is None
                 else _split_shards(G[n], SHARD_AXIS[n]).reshape(N_SHARD, -1)) for n in small}
    gflat['loss'] = jnp.broadcast_to(loss_row[:, :1], (N_SHARD, 1))
    items = [_split_shards(G[n], SHARD_AXIS[n]) for n in MATRICES] + [_pack(table, rows, gflat).reshape(N_SHARD, 2, rows // 2, LANES)]
    tags = MATRICES + ['small']
    got = _pair_exchange("rs_pair", items)
    keep = [lax.dynamic_index_in_dim(a, mc, axis=1, keepdims=False) for a in items]
    wire = [bf16] * len(MATRICES) + [f32]
    part = [_ew_call(f"rs_add_pair_{t}", lambda a, b: a + b, [k, g], 1, out_dtype=dt)[0] for t, k, g, dt in zip(tags, keep, got, wire)]
    others = _chip_exchange("rs_chips", part)
    own = lambda a: lax.dynamic_index_in_dim(a, my_s, axis=0, keepdims=False)
    red = [_ew_call(f"rs_add_chips_{t}", lambda k, g, fx, fy, fxy: ((k + g) + fy) + (fx + fxy), [own(k), own(g), o[0], o[1], o[2]], 1)[0]
           for t, k, g, o in zip(tags, keep, got, others)]
    theirs = _pair_swap("rs_swap", red)

    mc1 = mc.astype(i32).reshape(1)
    outs = {}
    for n, a, b in zip(MATRICES, red, theirs):
        outs[n] = _adamw_call(f"adamw_{n}", a, b, Ws[n], Ms[n], Vs[n], mc1)
    fill = {'loss': jnp.zeros((1,), f32)}
    packs = [_pack(table, rows, {**{n: src[n].reshape(-1) for n in small}, **fill}).reshape(2, rows // 2, LANES) for src in (Ws, Ms, Vs)]
    small_out = [p.reshape(rows, LANES) for p in _adamw_call("adamw_small", red[-1], theirs[-1], *packs, mc1)]
    for n in small:
        outs[n] = tuple(_unpack(table, p, n).reshape(Ws[n].shape) for p in small_out)
    loss = _unpack(table, small_out[0], 'loss').reshape(())
    return (loss, dx[None], *[outs[n][k] for k in range(4) for n in WEIGHTS])
```

```python
import functools

import jax
import jax.numpy as jnp
from jax import lax
from jax.experimental import pallas as pl
from jax.experimental.pallas import tpu as pltpu

f32, bf16, i32 = jnp.float32, jnp.bfloat16, jnp.int32
HI = lax.Precision.HIGHEST
MESH = pl.DeviceIdType.MESH

D_MODEL = 1024
DEPTH = 2
DN_HEADS, DN_DK, DN_CONV, DN_CHUNK = 4, 128, 4, 64
SWA_HEADS, SWA_KV_HEADS, SWA_DH, WINDOW = 8, 2, 64, 128
XA_HEADS, XA_DH = 4, 128
D_FF = 2816
N_BRANCH, BRANCH_W = 3, 512
ALPHA = (2 * DEPTH) ** 0.25
LN_EPS, RMS_EPS, NEG_INF = 1e-5, 1e-6, -1e30
D_IN = 6408
D_INP = 6656
LR, B1, B2, EPS, WD, STEP = 0.001, 0.9, 0.999, 1e-08, 0.01, 10

LANES = 128
VMEM_LIMIT = 56 << 20
N_SHARD = 4
SCAN_CHUNKS = 4
HALO = 8
MM_TILES = {
    'ffn1_gu_dx': (1024, 1024, 2816), 'ffn2_gu_dx': (1024, 1024, 2816), 'w_in': (2048, 512, 1024),
    'w_in_dw': (1024, 512, 4096), 'ffn1_down_dw': (1408, 512, 4096), 'ffn2_down_dw': (1408, 512, 4096),
    'ffn1_down_dx': (512,), 'ffn2_down_dx': (512,), 'ffn1_down': (1024,), 'ffn2_down': (1024,),
    'l1_w_in': (2048, 1024, 1024), 'l1_w_in_dx': (1024, 1024, 1664), 'l1_w_in_dw': (1024, 1024, 4096),
    'l1_ffn1_down_dx': (1024,), 'l1_ffn1_gu': (1024,), 'l1_w_out': (1024,),
}


def _tiles(name, default=None):
    return MM_TILES.get(name, MM_TILES.get(name.split('_', 1)[-1], default))

WEIGHTS = ['mem_ln_g', 'mem_ln_b', 'ln_g', 'ln_b', 'ffn1_w_gu', 'ffn1_w_down', 'w_in', 'dn_conv_w', 'dn_a_log',
           'dn_dt_bias', 'dn_norm_w', 'swa_sinks', 'w_mem_kv', 'w_branch', 'w_out', 'ffn2_w_gu', 'ffn2_w_down']
SHARD_AXIS = {'mem_ln_g': None, 'mem_ln_b': None, 'ln_g': 2, 'ln_b': 2, 'ffn1_w_gu': 2, 'ffn1_w_down': 1, 'w_in': 2,
              'dn_conv_w': 2, 'dn_a_log': None, 'dn_dt_bias': None, 'dn_norm_w': None, 'swa_sinks': None,
              'w_mem_kv': 1, 'w_branch': 3, 'w_out': 1, 'ffn2_w_gu': 2, 'ffn2_w_down': 1}
MATRICES = ['ffn1_w_gu', 'ffn1_w_down', 'w_in', 'w_mem_kv', 'w_branch', 'w_out', 'ffn2_w_gu', 'ffn2_w_down']
SMALL_SHARDED = ['ln_g', 'ln_b', 'dn_conv_w']


def _dg(a, b, mode, hi):
    nb = a.ndim - 2
    bd = tuple(range(nb))
    ca = nb if mode == 'tn' else nb + 1
    cb = nb + 1 if mode == 'nt' else nb
    dims = (((ca,), (cb,)), (bd, bd))
    dot = lambda x, y: lax.dot_general(x, y, dims, preferred_element_type=f32)
    a_hi, b_hi = a.astype(bf16), b.astype(bf16)
    if not hi:
        return dot(a_hi, b_hi)
    a_lo = (a.astype(f32) - a_hi.astype(f32)).astype(bf16)
    b_lo = (b.astype(f32) - b_hi.astype(f32)).astype(bf16)
    return dot(a_hi, b_hi) + (dot(a_hi, b_lo) + dot(a_lo, b_hi))


@functools.partial(jax.custom_vjp, nondiff_argnums=(2, 3))
def _dot(a, b, mode, hi):
    return _dg(a, b, mode, hi)


def _dot_fwd(a, b, mode, hi):
    return _dg(a, b, mode, hi), (a, b)


def _dot_bwd(mode, hi, res, g):
    a, b = res
    if mode == 'nn':
        da, db = _dg(g, b, 'nt', hi), _dg(a, g, 'tn', hi)
    elif mode == 'nt':
        da, db = _dg(g, b, 'nn', hi), _dg(g, a, 'tn', hi)
    else:
        da, db = _dg(b, g, 'nt', hi), _dg(a, g, 'nn', hi)
    return da.astype(a.dtype), db.astype(b.dtype)


_dot.defvjp(_dot_fwd, _dot_bwd)


def bdot(a, b, mode):
    return _dot(a, b, mode, False)


def hdot(a, b, mode):
    return _dot(a, b, mode, True)


def _shift_rows_impl(x, xprev, s):
    rows = lax.broadcasted_iota(i32, x.shape, 0)
    return jnp.where(rows >= s, pltpu.roll(x, s, 0), pltpu.roll(xprev, s, 0))


@functools.partial(jax.custom_vjp, nondiff_argnums=(2,))
def _shift_rows(x, xprev, s):
    return _shift_rows_impl(x, xprev, s)


def _shift_rows_fwd(x, xprev, s):
    return _shift_rows_impl(x, xprev, s), None


def _shift_rows_bwd(s, _, g):
    n = g.shape[0]
    rows = lax.broadcasted_iota(i32, g.shape, 0)
    r = pltpu.roll(g, n - s, 0)
    keep = rows < n - s
    return jnp.where(keep, r, 0.0), jnp.where(keep, 0.0, r)


_shift_rows.defvjp(_shift_rows_fwd, _shift_rows_bwd)


def _lane_pick(x, lane):
    idx = lax.broadcasted_iota(i32, x.shape, x.ndim - 1)
    return jnp.sum(jnp.where(idx == lane, x, 0.0), axis=-1, keepdims=True)


def _silu(x):
    return x * jax.nn.sigmoid(x)


def _tri_inv(a):
    C = a.shape[-1]
    eye = (lax.broadcasted_iota(i32, a.shape, 1) == lax.broadcasted_iota(i32, a.shape, 2)).astype(f32)
    p = -a
    x = eye + p
    for _ in range((C - 1).bit_length() - 1):
        p = _dg(p, p, 'nn', True)
        x = x + _dg(x, p, 'nn', True)
    return x


@jax.custom_vjp
def _tri_solve(a, rhs):
    return _dg(_tri_inv(a), rhs, 'nn', True)


def _tri_solve_fwd(a, rhs):
    tinv = _tri_inv(a)
    sol = _dg(tinv, rhs, 'nn', True)
    return sol, (tinv, sol)


def _tri_solve_bwd(res, g):
    tinv, sol = res
    d_rhs = _dg(tinv, g, 'tn', True)
    return -_dg(d_rhs, sol, 'nt', True), d_rhs


_tri_solve.defvjp(_tri_solve_fwd, _tri_solve_bwd)


@functools.partial(jax.custom_vjp, nondiff_argnums=(1,))
def _lane_head(x, n):
    return x[:, :, :n]


def _lane_head_fwd(x, n):
    return x[:, :, :n], None


def _lane_head_bwd(n, _, g):
    s = jnp.sum(g, axis=-1, keepdims=True) * (1.0 / LANES)
    return (jnp.broadcast_to(s, g.shape[:-1] + (LANES,)),)


_lane_head.defvjp(_lane_head_fwd, _lane_head_bwd)


@functools.partial(jax.custom_vjp, nondiff_argnums=(1,))
def _last_row(x, c):
    return x[:, c - 1:, :]


def _last_row_fwd(x, c):
    return x[:, c - 1:, :], None


def _last_row_bwd(c, _, g):
    shape = (g.shape[0], c, g.shape[2])
    rows = lax.broadcasted_iota(i32, shape, 1)
    return (jnp.where(rows == c - 1, jnp.broadcast_to(g, shape), 0.0),)


_last_row.defvjp(_last_row_fwd, _last_row_bwd)


def _full(a):
    nd = a.ndim
    return (a, tuple(a.shape), lambda i, _nd=nd: (0,) * _nd)


def _rows(a, tm, col=0, width=None):
    width = a.shape[1] if width is None else width
    return (a, (tm, width), lambda i, _c=col: (i, _c))


def _tile_call(name, fn, n, ins, outs, accs=()):
    n_in, n_out, n_acc = len(ins), len(outs), len(accs)

    def body(*refs):
        i = pl.program_id(0)
        res = fn(i, *[r[...].astype(f32) for r in refs[:n_in]])
        if not isinstance(res, (tuple, list)):
            res = (res,)
        assert len(res) == n_out + n_acc, (name, len(res), n_out, n_acc)
        for r, v in zip(refs[n_in:n_in + n_out], res[:n_out]):
            r[...] = v.astype(r.dtype)
        if n_acc:
            acc_refs = refs[n_in + n_out:]

            @pl.when(i == 0)
            def _():
                for r in acc_refs:
                    r[...] = jnp.zeros(r.shape, r.dtype)

            for r, v in zip(acc_refs, res[n_out:]):
                r[...] += v.astype(r.dtype)

    out_shape = [jax.ShapeDtypeStruct(s, d) for s, d, _, _ in outs] + [jax.ShapeDtypeStruct(s, d) for s, d in accs]
    out_specs = [pl.BlockSpec(b, m) for _, _, b, m in outs]
    out_specs += [pl.BlockSpec(tuple(s), lambda i, _nd=len(s): (0,) * _nd) for s, _ in accs]
    res = pl.pallas_call(
        body, name=name, grid=(n,),
        in_specs=[pl.BlockSpec(b, m) for _, b, m in ins],
        out_specs=out_specs, out_shape=out_shape,
        compiler_params=pltpu.CompilerParams(dimension_semantics=("arbitrary",), vmem_limit_bytes=VMEM_LIMIT),
    )(*[a for a, _, _ in ins])
    return res


def _pick(n, cands):
    for c in cands:
        if n % c == 0:
            return c
    return n


def _mm(name, a, b, mode, out_dtype=f32, add=None, out_cols=None):
    if mode == 'tn':
        K, M = a.shape
    else:
        M, K = a.shape
    N = b.shape[0] if mode == 'nt' else b.shape[1]
    tm = _pick(M, (1024, 1408, 512, 256, 128))
    tn = _pick(N, (512, 1408, 256, 128)) if out_cols is None else out_cols[0]
    tuned = _tiles(name)
    if tuned is not None and (M % tuned[0] or N % tuned[1] or K % tuned[2]):
        tuned = None
    if tuned is not None:
        tm, tn = tuned[:2]
    col = (lambda j: j) if out_cols is None else out_cols[1]
    tk = K if K <= 3328 else _pick(K, (3328, 2816, 2048, 1024, 512, 256, 128))
    if tuned is not None:
        tk = tuned[2]
    nk = K // tk
    ca = 0 if mode == 'tn' else 1
    cb = 1 if mode == 'nt' else 0
    dims = (((ca,), (cb,)), ((), ()))

    def body(*refs):
        a_ref, b_ref = refs[0], refs[1]
        add_ref = refs[2] if add is not None else None
        part = lax.dot_general(a_ref[...].astype(bf16), b_ref[...].astype(bf16), dims, preferred_element_type=f32)

        def finish(r, o_ref):
            if add_ref is not None:
                r = r + add_ref[...].astype(f32)
            o_ref[...] = r.astype(o_ref.dtype)

        if nk == 1:
            finish(part, refs[-1])
            return
        o_ref, acc_ref = refs[-2], refs[-1]
        k = pl.program_id(2)

        @pl.when(k == 0)
        def _():
            acc_ref[...] = part

        @pl.when(k > 0)
        def _():
            acc_ref[...] += part

        @pl.when(k == nk - 1)
        def _():
            finish(acc_ref[...], o_ref)

    a_spec = pl.BlockSpec((tk, tm), lambda i, j, k: (k, i)) if mode == 'tn' else pl.BlockSpec((tm, tk), lambda i, j, k: (i, k))
    b_spec = pl.BlockSpec((tn, tk), lambda i, j, k: (j, k)) if mode == 'nt' else pl.BlockSpec((tk, tn), lambda i, j, k: (k, j))
    in_specs, args = [a_spec, b_spec], [a, b]
    if add is not None:
        in_specs.append(pl.BlockSpec((tm, tn), lambda i, j, k: (i, j)))
        args.append(add)
    return pl.pallas_call(
        body, name=name, grid=(M // tm, N // tn, nk), in_specs=in_specs,
        out_specs=pl.BlockSpec((tm, tn), lambda i, j, k: (i, col(j))),
        out_shape=jax.ShapeDtypeStruct((M, N), out_dtype),
        scratch_shapes=[pltpu.VMEM((tm, tn), f32)] if nk > 1 else [],
        compiler_params=pltpu.CompilerParams(dimension_semantics=("parallel", "parallel", "arbitrary"), vmem_limit_bytes=VMEM_LIMIT),
    )(*args)


def _swiglu(u):
    return _silu(u[:, :D_FF]) * u[:, D_FF:]


def _res_ln(x, f, g, b, s):
    r = ALPHA * x + s * f
    mu = jnp.mean(r, axis=-1, keepdims=True)
    rc = r - mu
    var = jnp.mean(rc * rc, axis=-1, keepdims=True)
    return rc * lax.rsqrt(var + LN_EPS) * g + b


def _ln(x, g, b):
    mu = jnp.mean(x, axis=-1, keepdims=True)
    xc = x - mu
    var = jnp.mean(xc * xc, axis=-1, keepdims=True)
    return xc * lax.rsqrt(var + LN_EPS) * g + b


def _dn_pre(first, xc, xp, ba, cw, hp):
    xp = jnp.where(first, 0.0, xp)
    y = cw[DN_CONV - 1:DN_CONV, :] * xc
    for j in range(DN_CONV - 1):
        y = y + cw[j:j + 1, :] * _shift_rows(xc, xp, DN_CONV - 1 - j)
    c = _silu(y)
    qs, ks, vs, gs, bs = [], [], [], [], []
    nqk = DN_HEADS * DN_DK
    for h in range(DN_HEADS):
        q = c[:, h * DN_DK:(h + 1) * DN_DK]
        k = c[:, nqk + h * DN_DK:nqk + (h + 1) * DN_DK]
        v = c[:, 2 * nqk + h * DN_DK:2 * nqk + (h + 1) * DN_DK]
        qs.append(q * lax.rsqrt(jnp.sum(q * q, axis=-1, keepdims=True) + RMS_EPS))
        ks.append(k * lax.rsqrt(jnp.sum(k * k, axis=-1, keepdims=True) + RMS_EPS))
        vs.append(v)
        beta = jax.nn.sigmoid(_lane_pick(ba, h))
        a_log = _lane_pick(hp[0:1, :], h)
        dt = _lane_pick(hp[1:2, :], h)
        g = -jnp.exp(a_log) * jax.nn.softplus(_lane_pick(ba, DN_HEADS + h) + dt)
        gs.append(jnp.broadcast_to(g, q.shape))
        bs.append(jnp.broadcast_to(beta, q.shape))
    return tuple(jnp.stack(t, axis=0) for t in (qs, ks, vs, gs, bs))


def _wy(q, k, v, gb, bb):
    B, C, _ = q.shape
    ri = lax.broadcasted_iota(i32, (B, C, C), 1)
    ci = lax.broadcasted_iota(i32, (B, C, C), 2)
    tril, strict = ri >= ci, ri > ci
    gc = hdot(tril.astype(f32), gb, 'nn')
    gl = jnp.broadcast_to(_last_row(gc, C), gc.shape)
    col = _lane_head(gc, C)
    decay = jnp.exp(jnp.where(tril, col - jnp.swapaxes(col, 1, 2), NEG_INF))
    qs = q * (DN_DK ** -0.5)
    kb = k * bb
    a = jnp.where(strict, bdot(kb, k, 'nt') * decay, 0.0)
    sol = _tri_solve(a, jnp.concatenate([v * bb, kb * jnp.exp(gc)], axis=-1))
    qk = jnp.where(tril, bdot(qs, k, 'nt') * decay, 0.0)
    kt = k * jnp.exp(gl - gc)
    qd = qs * jnp.exp(gc)
    return qd, qk, sol[..., :DN_DK], sol[..., DN_DK:], kt, jnp.exp(gl)


def _scan_step(S, qd, qk, u, w, kt, egl):
    vn = u - bdot(w, S, 'nn')
    o = bdot(qd, S, 'nn') + bdot(qk, vn, 'nn')
    e2 = jnp.concatenate([egl] * (DN_DK // DN_CHUNK), axis=1)
    return o, S * e2 + bdot(kt, vn, 'tn')


def _dn_post(o, z, nw):
    outs = []
    for h in range(DN_HEADS):
        oh = o[h]
        oh = oh * lax.rsqrt(jnp.mean(oh * oh, axis=-1, keepdims=True) + RMS_EPS) * nw
        outs.append(oh * _silu(z[:, h * DN_DK:(h + 1) * DN_DK]))
    return jnp.concatenate(outs, axis=1)


def _swa(first, q, kc, kp, vc, vp, snk):
    W = q.shape[1]
    G = SWA_HEADS // SWA_KV_HEADS
    r = lax.broadcasted_iota(i32, (G, W, 2 * W), 1)
    c = lax.broadcasted_iota(i32, (G, W, 2 * W), 2)
    mask = (c > W + r - WINDOW) & (c <= W + r) & jnp.logical_or(c >= W, jnp.logical_not(first))
    sink_all = _lane_pick(snk, 0)
    outs = []
    for j in range(SWA_KV_HEADS):
        qj = q[j * G:(j + 1) * G]
        kk = jnp.broadcast_to(jnp.concatenate([kp[j], kc[j]], axis=0)[None], (G, 2 * W, SWA_DH))
        vv = jnp.broadcast_to(jnp.concatenate([vp[j], vc[j]], axis=0)[None], (G, 2 * W, SWA_DH))
        s = jnp.where(mask, bdot(qj, kk, 'nt') * (SWA_DH ** -0.5), NEG_INF)
        sink = sink_all[j * G:(j + 1) * G][:, :, None]
        m = jnp.maximum(jnp.max(s, axis=-1, keepdims=True), sink)
        p = jnp.exp(s - m)
        p = p / (jnp.sum(p, axis=-1, keepdims=True) + jnp.exp(sink - m))
        outs.append(bdot(p, vv, 'nn'))
    return jnp.concatenate(outs, axis=0)


def _xa(q, kv):
    outs = []
    nk = XA_HEADS * XA_DH
    for h in range(XA_HEADS):
        qh = q[:, h * XA_DH:(h + 1) * XA_DH]
        kh = kv[:, h * XA_DH:(h + 1) * XA_DH]
        vh = kv[:, nk + h * XA_DH:nk + (h + 1) * XA_DH]
        s = bdot(qh, kh, 'nt') * (XA_DH ** -0.5)
        m = jnp.max(s, axis=-1, keepdims=True)
        p = jnp.exp(s - m)
        p = p / jnp.sum(p, axis=-1, keepdims=True)
        outs.append(bdot(p, vh, 'nn'))
    return jnp.concatenate(outs, axis=1)


def _merge(o_dn, o_sw, o_xa, gates, wb):
    acc = None
    for n, o in enumerate((o_dn, o_sw, o_xa)):
        t = jax.nn.sigmoid(gates[:, n * D_MODEL:(n + 1) * D_MODEL]) * bdot(o, wb[n], 'nn')
        acc = t if acc is None else acc + t
    return acc


def _row_tile(T, want):
    return _pick(T, tuple(c for c in (1024, 512, 256, 128, 64) if c <= want))


def _res_ln_fwd(name, x, f, g, b, s):
    T = x.shape[0]
    tm = _row_tile(T, 512)

    def fn(i, x, f, g, b):
        h = _res_ln(x, f, g, b, s)
        return h, h

    return _tile_call(name, fn, T // tm, [_rows(x, tm), _rows(f, tm), _full(g), _full(b)],
                      [((T, D_MODEL), f32, (tm, D_MODEL), lambda i: (i, 0)), ((T, D_MODEL), bf16, (tm, D_MODEL), lambda i: (i, 0))])


def _res_ln_bwd(name, x, f, g, b, s, dh):
    T = x.shape[0]
    tm = _row_tile(T, 512)

    def fn(i, x, f, g, b, dh):
        _, vjp = jax.vjp(lambda x, f, g, b: _res_ln(x, f, g, b, s), x, f, g, b)
        return vjp(dh)

    return _tile_call(name, fn, T // tm, [_rows(x, tm), _rows(f, tm), _full(g), _full(b), _rows(dh, tm)],
                      [((T, D_MODEL), f32, (tm, D_MODEL), lambda i: (i, 0)), ((T, D_MODEL), bf16, (tm, D_MODEL), lambda i: (i, 0))],
                      [((1, D_MODEL), f32), ((1, D_MODEL), f32)])


FF_BLK = D_FF // 2


def _pair_gate_up(w):
    d = w.shape[0]
    return jnp.stack([w[:, :D_FF].reshape(d, -1, FF_BLK), w[:, D_FF:].reshape(d, -1, FF_BLK)], axis=2).reshape(d, 2 * D_FF)


def _swiglu_blk(u):
    return _silu(u[:, :FF_BLK]) * u[:, FF_BLK:]


def _mm_spec(tm, tn):
    return dict(compiler_params=pltpu.CompilerParams(dimension_semantics=("parallel", "arbitrary"), vmem_limit_bytes=VMEM_LIMIT))


def _gu_act(name, xb, w_pair):
    T, D = xb.shape
    tm = _pick(T, (_tiles(name, (512,))[0], 512, 256, 128))

    def body(x_ref, w_ref, u_ref, a_ref):
        r = lax.dot_general(x_ref[...], w_ref[...], (((1,), (0,)), ((), ())), preferred_element_type=f32)
        u_ref[...] = r.astype(bf16)
        a_ref[...] = _swiglu_blk(r).astype(bf16)

    return pl.pallas_call(
        body, name=name, grid=(T // tm, D_FF // FF_BLK),
        in_specs=[pl.BlockSpec((tm, D), lambda i, j: (i, 0)), pl.BlockSpec((D, 2 * FF_BLK), lambda i, j: (0, j))],
        out_specs=[pl.BlockSpec((tm, 2 * FF_BLK), lambda i, j: (i, j)), pl.BlockSpec((tm, FF_BLK), lambda i, j: (i, j))],
        out_shape=[jax.ShapeDtypeStruct((T, 2 * D_FF), bf16), jax.ShapeDtypeStruct((T, D_FF), bf16)],
        **_mm_spec(tm, FF_BLK))(xb, w_pair)


def _down_dx_act_bwd(name, df, w_down_t, u):
    T, D = df.shape
    tm = _pick(T, (_tiles(name, (256,))[0], 256, 128))

    def body(df_ref, w_ref, u_ref, du_ref):
        da = lax.dot_general(df_ref[...], w_ref[...], (((1,), (0,)), ((), ())), preferred_element_type=f32)
        _, vjp = jax.vjp(_swiglu_blk, u_ref[...].astype(f32))
        du_ref[...] = vjp(da)[0].astype(bf16)

    return pl.pallas_call(
        body, name=name, grid=(T // tm, D_FF // FF_BLK),
        in_specs=[pl.BlockSpec((tm, D), lambda i, j: (i, 0)), pl.BlockSpec((D, FF_BLK), lambda i, j: (0, j)),
                  pl.BlockSpec((tm, 2 * FF_BLK), lambda i, j: (i, j))],
        out_specs=pl.BlockSpec((tm, 2 * FF_BLK), lambda i, j: (i, j)),
        out_shape=jax.ShapeDtypeStruct((T, 2 * D_FF), bf16),
        **_mm_spec(tm, FF_BLK))(df, w_down_t, u)


def _mm_res_ln(name, a, w, x, g, b, s):
    T, K = a.shape
    tm = _pick(T, (_tiles(name, (512,))[0], 512, 256, 128))

    def body(a_ref, w_ref, x_ref, g_ref, b_ref, f_ref, h_ref, hb_ref):
        f = lax.dot_general(a_ref[...], w_ref[...], (((1,), (0,)), ((), ())), preferred_element_type=f32)
        h = _res_ln(x_ref[...], f, g_ref[...], b_ref[...], s)
        f_ref[...] = f
        h_ref[...] = h
        hb_ref[...] = h.astype(bf16)

    row = pl.BlockSpec((tm, D_MODEL), lambda i: (i, 0))
    vec = pl.BlockSpec((1, D_MODEL), lambda i: (0, 0))
    return pl.pallas_call(
        body, name=name, grid=(T // tm,),
        in_specs=[pl.BlockSpec((tm, K), lambda i: (i, 0)), pl.BlockSpec((K, D_MODEL), lambda i: (0, 0)), row, vec, vec],
        out_specs=[row, row, row],
        out_shape=[jax.ShapeDtypeStruct((T, D_MODEL), f32), jax.ShapeDtypeStruct((T, D_MODEL), f32), jax.ShapeDtypeStruct((T, D_MODEL), bf16)],
        compiler_params=pltpu.CompilerParams(dimension_semantics=("parallel",), vmem_limit_bytes=VMEM_LIMIT))(a, w, x, g, b)


def _hm_spec(tm, w=DN_DK):
    return ((DN_HEADS, tm, w), lambda i: (0, i, 0))


def _prev(i):
    return jnp.maximum(i - 1, 0)


def _dn_front(first, xc, xp, ba, cw, hp):
    tm = xc.shape[0]
    nc = tm // DN_CHUNK
    outs = _wy(*[t.reshape(DN_HEADS * nc, DN_CHUNK, DN_DK) for t in _dn_pre(first, xc, xp, ba, cw, hp)])
    return tuple(o.reshape(DN_HEADS, tm, o.shape[-1]) for o in outs)


def _dn_front_ins(p, pba, cw, hp, tm):
    W3 = 3 * DN_HEADS * DN_DK
    return [(p, (tm, W3), lambda i: (i, 0)), (p, (tm, W3), lambda i: (_prev(i), 0)),
            (pba, (tm, LANES), lambda i: (i, 0)), _full(cw), _full(hp)]


def _dn_front_fwd(name, p, pba, cw, hp):
    T = p.shape[0]
    tm = _row_tile(T, 256)
    blk, im = _hm_spec(tm)
    blk2, im2 = _hm_spec(tm, DN_CHUNK)
    full = (DN_HEADS, T, DN_DK)
    outs = [(full, bf16, blk, im), ((DN_HEADS, T, DN_CHUNK), bf16, blk2, im2), (full, f32, blk, im),
            (full, bf16, blk, im), (full, bf16, blk, im), (full, f32, blk, im)]
    return _tile_call(name, lambda i, *a: _dn_front(i == 0, *a), T // tm, _dn_front_ins(p, pba, cw, hp, tm), outs)


def _dn_front_bwd(name, p, pba, cw, hp, cts):
    T = p.shape[0]
    tm = _row_tile(T, 256)
    n = T // tm
    W3 = 3 * DN_HEADS * DN_DK
    blk, im = _hm_spec(tm)
    blk2, im2 = _hm_spec(tm, DN_CHUNK)
    ins = _dn_front_ins(p, pba, cw, hp, tm) + [(c, blk2 if c.shape[-1] == DN_CHUNK else blk, im) for c in cts]

    def fn(i, xc, xp, ba, cw, hp, *cts):
        _, vjp = jax.vjp(lambda *a: _dn_front(i == 0, *a), xc, xp, ba, cw, hp)
        dxc, dxp, dba, dcw, dhp = vjp(tuple(cts))
        return dxc, dxp[tm - HALO:], dba, dcw, dhp

    return _tile_call(name, fn, n, ins,
                      [((T, W3), f32, (tm, W3), lambda i: (i, 0)), ((n * HALO, W3), f32, (HALO, W3), lambda i: (i, 0)),
                       ((T, LANES), f32, (tm, LANES), lambda i: (i, 0))],
                      [(tuple(cw.shape), f32), (tuple(hp.shape), f32)])


def _scan_fwd(name, qd, qk, u, w, kt, egl):
    T = qd.shape[1]
    C = DN_CHUNK
    U = _pick(T // C, (SCAN_CHUNKS, 1))
    n = T // (C * U)

    def body(qd_ref, qk_ref, u_ref, w_ref, kt_ref, egl_ref, o_ref, sall_ref, s_ref):
        i = pl.program_id(0)

        @pl.when(i == 0)
        def _():
            s_ref[...] = jnp.zeros(s_ref.shape, f32)

        S = s_ref[...]
        for j in range(U):
            sl = pl.ds(j * C, C)
            sall_ref[:, j] = S
            o, S = _scan_step(S, *[r[:, sl, :] for r in (qd_ref, qk_ref, u_ref, w_ref, kt_ref, egl_ref)])
            o_ref[:, sl, :] = o
        s_ref[...] = S

    blk, im = _hm_spec(C * U)
    blk2, im2 = _hm_spec(C * U, C)
    return pl.pallas_call(
        body, name=name, grid=(n,),
        in_specs=[pl.BlockSpec(blk, im), pl.BlockSpec(blk2, im2)] + [pl.BlockSpec(blk, im)] * 4,
        out_specs=[pl.BlockSpec(blk, im), pl.BlockSpec((DN_HEADS, U, DN_DK, DN_DK), lambda i: (0, i, 0, 0))],
        out_shape=[jax.ShapeDtypeStruct((DN_HEADS, T, DN_DK), f32), jax.ShapeDtypeStruct((DN_HEADS, n * U, DN_DK, DN_DK), f32)],
        scratch_shapes=[pltpu.VMEM((DN_HEADS, DN_DK, DN_DK), f32)],
        compiler_params=pltpu.CompilerParams(dimension_semantics=("arbitrary",), vmem_limit_bytes=VMEM_LIMIT),
    )(qd, qk, u, w, kt, egl)


def _scan_bwd(name, qd, qk, u, w, kt, egl, s_all, do):
    T = qd.shape[1]
    C = DN_CHUNK
    U = _pick(T // C, (SCAN_CHUNKS, 1))
    n = T // (C * U)

    def body(qd_ref, qk_ref, u_ref, w_ref, kt_ref, egl_ref, sall_ref, do_ref,
             dqd_ref, dqk_ref, du_ref, dw_ref, dkt_ref, degl_ref, ds_ref):
        i = pl.program_id(0)

        @pl.when(i == 0)
        def _():
            ds_ref[...] = jnp.zeros(ds_ref.shape, f32)

        dS = ds_ref[...]
        for j in reversed(range(U)):
            sl = pl.ds(j * C, C)
            args = [r[:, sl, :].astype(f32) for r in (qd_ref, qk_ref, u_ref, w_ref, kt_ref, egl_ref)]
            _, vjp = jax.vjp(_scan_step, sall_ref[:, j], *args)
            dS, *cts = vjp((do_ref[:, sl, :], dS))
            for r, v in zip((dqd_ref, dqk_ref, du_ref, dw_ref, dkt_ref, degl_ref), cts):
                r[:, sl, :] = v
        ds_ref[...] = dS

    blk = (DN_HEADS, C * U, DN_DK)
    blk2 = (DN_HEADS, C * U, C)
    rim = lambda i: (0, n - 1 - i, 0)
    sp, sp2 = pl.BlockSpec(blk, rim), pl.BlockSpec(blk2, rim)
    full, full2 = jax.ShapeDtypeStruct((DN_HEADS, T, DN_DK), f32), jax.ShapeDtypeStruct((DN_HEADS, T, C), f32)
    return pl.pallas_call(
        body, name=name, grid=(n,),
        in_specs=[sp, sp2, sp, sp, sp, sp, pl.BlockSpec((DN_HEADS, U, DN_DK, DN_DK), lambda i: (0, n - 1 - i, 0, 0)), sp],
        out_specs=[sp, sp2, sp, sp, sp, sp],
        out_shape=[full, full2, full, full, full, full],
        scratch_shapes=[pltpu.VMEM((DN_HEADS, DN_DK, DN_DK), f32)],
        compiler_params=pltpu.CompilerParams(dimension_semantics=("arbitrary",), vmem_limit_bytes=VMEM_LIMIT),
    )(qd, qk, u, w, kt, egl, s_all, do)


def _dn_post_fwd(name, o, p, nw):
    T = p.shape[0]
    tm = _row_tile(T, 512)
    blk, im = _hm_spec(tm)
    W = DN_HEADS * DN_DK
    return _tile_call(name, lambda i, o, z, nw: _dn_post(o, z, nw), T // tm,
                      [(o, blk, im), (p, (tm, W), lambda i: (i, 1536 // W)), _full(nw)],
                      [((T, W), f32, (tm, W), lambda i: (i, 0))])[0]


def _dn_post_bwd(name, o, p, nw, dout):
    T = p.shape[0]
    tm = _row_tile(T, 512)
    blk, im = _hm_spec(tm)
    W = DN_HEADS * DN_DK

    def fn(i, o, z, nw, dout):
        _, vjp = jax.vjp(_dn_post, o, z, nw)
        return vjp(dout)

    return _tile_call(name, fn, T // tm,
                      [(o, blk, im), (p, (tm, W), lambda i: (i, 1536 // W)), _full(nw), _rows(dout, tm)],
                      [((DN_HEADS, T, DN_DK), f32, blk, im), ((T, W), f32, (tm, W), lambda i: (i, 0))],
                      [((1, DN_DK), f32)])


def _swa_tile(T, big):
    return _pick(T, (2 * WINDOW, WINDOW)) if big else WINDOW


@functools.partial(jax.custom_vjp, nondiff_argnums=(1,))
def _heads_split(x, nh):
    return jnp.stack([x[:, SWA_DH * h:SWA_DH * (h + 1)] for h in range(nh)], axis=0)


@jax.custom_vjp
def _heads_merge(y):
    return jnp.concatenate([y[h] for h in range(y.shape[0])], axis=1)


_heads_split.defvjp(lambda x, nh: (_heads_split(x, nh), None), lambda nh, _, g: (_heads_merge(g),))
_heads_merge.defvjp(lambda y: (_heads_merge(y), None), lambda _, g: (_heads_split(g, g.shape[1] // SWA_DH),))


def _swa_rows(first, qp, kc, kp, vc, vp, snk):
    kv = [_heads_split(t, SWA_KV_HEADS) for t in (kc, kp, vc, vp)]
    return _heads_merge(_swa(first, _heads_split(qp, SWA_HEADS), *kv, snk))


def _swa_ins(p, snk, W):
    qw, kw = SWA_HEADS * SWA_DH, SWA_KV_HEADS * SWA_DH
    return [(p, (W, qw), lambda i: (i, 2048 // qw)),
            (p, (W, kw), lambda i: (i, 6144 // kw)), (p, (W, kw), lambda i: (_prev(i), 6144 // kw)),
            (p, (W, kw), lambda i: (i, 6272 // kw)), (p, (W, kw), lambda i: (_prev(i), 6272 // kw)), _full(snk)]


def _swa_fwd(name, p, snk):
    T = p.shape[0]
    W = _swa_tile(T, True)
    qw = SWA_HEADS * SWA_DH
    return _tile_call(name, lambda i, *a: _swa_rows(i == 0, *a), T // W, _swa_ins(p, snk, W),
                      [((T, qw), f32, (W, qw), lambda i: (i, 0))])[0]


def _swa_bwd(name, p, snk, do):
    T = p.shape[0]
    W = _swa_tile(T, False)
    qw, kw = SWA_HEADS * SWA_DH, SWA_KV_HEADS * SWA_DH

    def fn(i, qp, k_c, k_p, v_c, v_p, snk, do):
        _, vjp = jax.vjp(lambda *a: _swa_rows(i == 0, *a), qp, k_c, k_p, v_c, v_p, snk)
        return vjp(do)

    kv_out = ((T, kw), f32, (W, kw), lambda i: (i, 0))
    return _tile_call(name, fn, T // W, _swa_ins(p, snk, W) + [(do, (W, qw), lambda i: (i, 0))],
                      [((T, qw), bf16, (W, qw), lambda i: (i, 0)), kv_out, kv_out, kv_out, kv_out], [(tuple(snk.shape), f32)])


def _xa_fwd(name, p, kv):
    T = p.shape[0]
    tm = _row_tile(T, 256)
    W = XA_HEADS * XA_DH
    return _tile_call(name, lambda i, q, kv: _xa(q, kv), T // tm, [(p, (tm, W), lambda i: (i, 2560 // W)), _full(kv)],
                      [((T, W), f32, (tm, W), lambda i: (i, 0))])[0]


def _xa_bwd(name, p, kv, do):
    T = p.shape[0]
    tm = _row_tile(T, 256)
    W = XA_HEADS * XA_DH

    def fn(i, q, kv, do):
        _, vjp = jax.vjp(_xa, q, kv)
        return vjp(do)

    return _tile_call(name, fn, T // tm, [(p, (tm, W), lambda i: (i, 2560 // W)), _full(kv), _rows(do, tm)],
                      [((T, W), bf16, (tm, W), lambda i: (i, 0))], [(tuple(kv.shape), f32)])


def _merge_fwd(name, o_dn, o_sw, o_xa, p, wb):
    T = p.shape[0]
    tm = _row_tile(T, 256)
    GW = N_BRANCH * D_MODEL
    return _tile_call(name, lambda i, a, b, c, g, w: _merge(a, b, c, g, w.astype(f32)), T // tm,
                      [_rows(o_dn, tm), _rows(o_sw, tm), _rows(o_xa, tm), (p, (tm, GW), lambda i: (i, 1)), _full(wb)],
                      [((T, D_MODEL), bf16, (tm, D_MODEL), lambda i: (i, 0))])[0]


def _merge_bwd(name, o_dn, o_sw, o_xa, p, wb, dm):
    T = p.shape[0]
    tm = _row_tile(T, 512)
    GW = N_BRANCH * D_MODEL

    def fn(i, a, b, c, g, w, dm):
        _, vjp = jax.vjp(_merge, a, b, c, g, w.astype(f32))
        return vjp(dm)

    bo = ((T, BRANCH_W), f32, (tm, BRANCH_W), lambda i: (i, 0))
    return _tile_call(name, fn, T // tm,
                      [_rows(o_dn, tm), _rows(o_sw, tm), _rows(o_xa, tm), (p, (tm, GW), lambda i: (i, 1)), _full(wb), _rows(dm, tm)],
                      [bo, bo, bo, ((T, GW), bf16, (tm, GW), lambda i: (i, 0))], [(tuple(wb.shape), f32)])


def _assemble_dp(name, dxc, dxp, dz, dswq, dxaq, dgates, dkc, dkp, dvc, dvp, dba):
    T = dz.shape[0]
    tm = _row_tile(T, 256)
    n = T // tm
    r = tm // WINDOW
    nb = T // WINDOW

    def fn(i, dxc, dxp, dz, dswq, dxaq, dgates, dkc, dvc, dba, *nxt):
        halo = jnp.where(i == n - 1, 0.0, dxp)
        dqkv = dxc + jnp.concatenate([jnp.zeros((tm - HALO, dxc.shape[1]), f32), halo], axis=0)
        shifted = [jnp.concatenate([jnp.where(i * r + 1 + b <= nb - 1, blk, 0.0) for b, blk in enumerate(half)], axis=0)
                   for half in (nxt[:r], nxt[r:])]
        parts = [dqkv, dz, dswq, dxaq, dgates, dkc + shifted[0], dvc + shifted[1], dba, jnp.zeros((tm, LANES), f32)]
        return jnp.concatenate(parts, axis=1)

    ins = [_rows(dxc, tm), (dxp, (HALO, dxp.shape[1]), lambda i: (jnp.minimum(i + 1, n - 1), 0))]
    ins += [_rows(t, tm) for t in (dz, dswq, dxaq, dgates, dkc, dvc, dba)]
    for t in (dkp, dvp):
        ins += [(t, (WINDOW, t.shape[1]), lambda i, b=b: (jnp.minimum(i * r + 1 + b, nb - 1), 0)) for b in range(r)]
    return _tile_call(name, fn, n, ins, [((T, D_INP), bf16, (tm, D_INP), lambda i: (i, 0))])[0]


def _loss_grad(name, y, tgt):
    T = y.shape[0]
    tm = _row_tile(T, 512)

    def fn(i, y, t):
        e = y - t
        part = 0.5 * jnp.sum(jnp.mean(e * e, axis=-1, keepdims=True), axis=0, keepdims=True)
        return e * (1.0 / D_MODEL), jnp.broadcast_to(part, (1, LANES))

    return _tile_call(name, fn, T // tm, [_rows(y, tm), _rows(tgt, tm)],
                      [((T, D_MODEL), f32, (tm, D_MODEL), lambda i: (i, 0))], [((1, LANES), f32)])


def _mem_ln_fwd(name, mem, g, b):
    M = mem.shape[0]
    return _tile_call(name, lambda i, m, g, b: (_ln(m, g, b),), 1, [_full(mem), _full(g), _full(b)],
                      [((M, D_MODEL), bf16, (M, D_MODEL), lambda i: (0, 0))])[0]


def _mem_ln_bwd(name, mem, g, b, dmn):
    def fn(i, m, g, b, d):
        _, vjp = jax.vjp(lambda g, b: _ln(m, g, b), g, b)
        return vjp(d)

    return _tile_call(name, fn, 1, [_full(mem), _full(g), _full(b), _full(dmn)], [], [((1, D_MODEL), f32), ((1, D_MODEL), f32)])


def _pad_w_in(w):
    z = jnp.zeros((w.shape[0], D_INP - D_IN), w.dtype)
    return jnp.concatenate([w[:, 0:1536], w[:, 1544:2056], w[:, 2056:2568], w[:, 2824:3336], w[:, 3336:6408],
                            w[:, 2568:2696], w[:, 2696:2824], w[:, 1536:1544], z], axis=1)


def _unpad_dw_in(d):
    return jnp.concatenate([d[:, 0:1536], d[:, 6400:6408], d[:, 1536:2048], d[:, 2048:2560], d[:, 6144:6272],
                            d[:, 6272:6400], d[:, 2560:3072], d[:, 3072:6144]], axis=1)


def _lane_row(v, rows=8):
    out = jnp.zeros((rows, LANES), f32)
    return out.at[0, :v.shape[0]].set(v)


def _layer_fwd(l, x, xb, mem_nb, W):
    n = lambda s: f"l{l}_{s}"
    sv = {}
    u1, a1 = _gu_act(n("ffn1_gu"), xb, W['ffn1_w_gu_p'])
    f1, h1, h1b = _mm_res_ln(n("ffn1_down"), a1, W['ffn1_w_down'], x, W['ln_g'][0:1], W['ln_b'][0:1], 0.5)
    p = _mm(n("w_in"), h1b, W['w_inp'], 'nn', out_dtype=bf16)
    pba = _mm(n("w_in_ba"), h1b, W['w_ba'], 'nn')
    qd, qk, u, w, kt, egl = _dn_front_fwd(n("dn_front"), p, pba, W['conv_w'], W['hp'])
    o_raw, s_all = _scan_fwd(n("dn_scan"), qd, qk, u, w, kt, egl)
    o_dn = _dn_post_fwd(n("dn_post"), o_raw, p, W['norm_w'])
    o_sw = _swa_fwd(n("swa"), p, W['snk'])
    kv = _mm(n("mem_kv"), mem_nb, W['w_mem_kv'], 'nn')
    o_xa = _xa_fwd(n("xa"), p, kv)
    merged = _merge_fwd(n("merge"), o_dn, o_sw, o_xa, p, W['w_branch'])
    mo, h2, h2b = _mm_res_ln(n("w_out"), merged, W['w_out'], h1, W['ln_g'][1:2], W['ln_b'][1:2], 1.0)
    u2, a2 = _gu_act(n("ffn2_gu"), h2b, W['ffn2_w_gu_p'])
    f2, h3, h3b = _mm_res_ln(n("ffn2_down"), a2, W['ffn2_w_down'], h2, W['ln_g'][2:3], W['ln_b'][2:3], 0.5)
    sv = dict(x=x, xb=xb, u1=u1, a1=a1, f1=f1, h1=h1, h1b=h1b, p=p, pba=pba, wy=(qd, qk, u, w, kt, egl),
              o_raw=o_raw, s_all=s_all, o_dn=o_dn, o_sw=o_sw, kv=kv, o_xa=o_xa, merged=merged,
              mo=mo, h2=h2, h2b=h2b, u2=u2, a2=a2, f2=f2)
    return h3, h3b, sv


def _ffn_bwd(n, tag, x, xb, u, a, f, g, b, w_gu_t, w_down_t, dh):
    dx_a, df, dg, db = _res_ln_bwd(n(f"{tag}_ln_bwd"), x, f, g, b, 0.5, dh)
    d_down = _mm(n(f"{tag}_down_dw"), a, df, 'tn')
    du = _down_dx_act_bwd(n(f"{tag}_down_dx"), df, w_down_t, u)
    d_gu = _mm(n(f"{tag}_gu_dw"), xb, du, 'tn', out_cols=(FF_BLK, lambda j: (j % 2) * (D_FF // FF_BLK) + j // 2))
    dx = _mm(n(f"{tag}_gu_dx"), du, w_gu_t, 'nn', add=dx_a)
    return dx, d_gu, d_down, dg, db


def _layer_bwd(l, sv, mem_nb, W, dh3, dmem_acc):
    n = lambda s: f"l{l}_{s}"
    G = {}
    dh2, G['ffn2_w_gu'], G['ffn2_w_down'], dg2, db2 = _ffn_bwd(
        n, "ffn2", sv['h2'], sv['h2b'], sv['u2'], sv['a2'], sv['f2'], W['ln_g'][2:3], W['ln_b'][2:3], W['ffn2_w_gu_t'], W['ffn2_w_down_t'], dh3)
    dh1_a, dmo, dg1, db1 = _res_ln_bwd(n("ln1_bwd"), sv['h1'], sv['mo'], W['ln_g'][1:2], W['ln_b'][1:2], 1.0, dh2)
    G['w_out'] = _mm(n("w_out_dw"), sv['merged'], dmo, 'tn')
    dmerged = _mm(n("w_out_dx"), dmo, W['w_out_t'], 'nn')
    p = sv['p']
    do_dn, do_sw, do_xa, dgates, G['w_branch'] = _merge_bwd(n("merge_bwd"), sv['o_dn'], sv['o_sw'], sv['o_xa'], p, W['w_branch'], dmerged)
    dxaq, dkv = _xa_bwd(n("xa_bwd"), p, sv['kv'], do_xa)
    dkv = dkv.astype(bf16)
    G['w_mem_kv'] = _mm(n("mem_kv_dw"), mem_nb, dkv, 'tn')
    dmem_n = _mm(n("mem_kv_dx"), dkv, W['w_mem_kv_t'], 'nn', add=dmem_acc)
    dswq, dkc, dkp, dvc, dvp, dsnk = _swa_bwd(n("swa_bwd"), p, W['snk'], do_sw)
    do_raw, dz, dnw = _dn_post_bwd(n("dn_post_bwd"), sv['o_raw'], p, W['norm_w'], do_dn)
    cts = _scan_bwd(n("dn_scan_bwd"), *sv['wy'], sv['s_all'], do_raw)
    dxc, dxp, dba, dcw, dhp = _dn_front_bwd(n("dn_front_bwd"), p, sv['pba'], W['conv_w'], W['hp'], cts)
    dp = _assemble_dp(n("dp"), dxc, dxp, dz, dswq, dxaq, dgates, dkc, dkp, dvc, dvp, dba)
    G['w_in'] = _unpad_dw_in(_mm(n("w_in_dw"), sv['h1b'], dp, 'tn'))
    dh1 = _mm(n("w_in_dx"), dp, W['w_inp_t'], 'nn', add=dh1_a)
    dx, G['ffn1_w_gu'], G['ffn1_w_down'], dg0, db0 = _ffn_bwd(
        n, "ffn1", sv['x'], sv['xb'], sv['u1'], sv['a1'], sv['f1'], W['ln_g'][0:1], W['ln_b'][0:1], W['ffn1_w_gu_t'], W['ffn1_w_down_t'], dh1)
    G['ln_g'] = jnp.concatenate([dg0, dg1, dg2], axis=0)
    G['ln_b'] = jnp.concatenate([db0, db1, db2], axis=0)
    G['dn_conv_w'] = dcw
    G['dn_a_log'] = dhp[0, :DN_HEADS]
    G['dn_dt_bias'] = dhp[1, :DN_HEADS]
    G['dn_norm_w'] = dnw[0]
    G['swa_sinks'] = dsnk[:, 0]
    return dx, dmem_n, G


def _local_step(x, mem, tgt, Wf):
    mem_g, mem_b = Wf['mem_ln_g'][None, :], Wf['mem_ln_b'][None, :]
    mem_nb = _mem_ln_fwd("mem_ln", mem, mem_g, mem_b)
    layers = []
    for l in range(DEPTH):
        layers.append(dict(
            ln_g=Wf['ln_g'][l], ln_b=Wf['ln_b'][l], ffn1_w_gu_p=_pair_gate_up(Wf['ffn1_w_gu'][l]), ffn1_w_down=Wf['ffn1_w_down'][l],
            w_inp=_pad_w_in(Wf['w_in'][l]), conv_w=Wf['dn_conv_w'][l],
            hp=jnp.zeros((8, LANES), f32).at[0, :DN_HEADS].set(Wf['dn_a_log'][l]).at[1, :DN_HEADS].set(Wf['dn_dt_bias'][l]),
            norm_w=Wf['dn_norm_w'][l][None, :], snk=jnp.broadcast_to(Wf['swa_sinks'][l][:, None], (SWA_HEADS, LANES)),
            w_mem_kv=Wf['w_mem_kv'][l], w_branch=Wf['w_branch'][l], w_out=Wf['w_out'][l],
            ffn2_w_gu_p=_pair_gate_up(Wf['ffn2_w_gu'][l]), ffn2_w_down=Wf['ffn2_w_down'][l]))
        layers[l]['w_ba'] = layers[l]['w_inp'][:, 6400:6400 + LANES]
        for k in ('ffn1_w_gu_p', 'ffn1_w_down', 'w_inp', 'w_mem_kv', 'w_out', 'ffn2_w_gu_p', 'ffn2_w_down'):
            layers[l][k.replace('_p', '') + '_t'] = layers[l][k].T
    h, hb = x, x.astype(bf16)
    saved = []
    for l in range(DEPTH):
        h, hb, sv = _layer_fwd(l, h, hb, mem_nb, layers[l])
        saved.append(sv)
    dh, loss_row = _loss_grad("loss", h, tgt)
    grads = [None] * DEPTH
    dmem_n = None
    for l in reversed(range(DEPTH)):
        dh, dmem_n, grads[l] = _layer_bwd(l, saved[l], mem_nb, layers[l], dh, dmem_n)
    dmg, dmb = _mem_ln_bwd("mem_ln_bwd", mem, mem_g, mem_b, dmem_n)
    G = {k: jnp.stack([grads[l][k] for l in range(DEPTH)], axis=0) for k in grads[0]}
    G['mem_ln_g'], G['mem_ln_b'] = dmg[0], dmb[0]
    return loss_row, dh, G


SEG_ALIGN = 2048


def _round_up(n, m):
    return (n + m - 1) // m * m


def _layout(names, sizes, mult):
    off, table = 0, {}
    for nm in names:
        table[nm] = (off, sizes[nm])
        off += _round_up(sizes[nm], SEG_ALIGN)
    return table, _round_up(off // LANES, mult)


def _pack(table, rows, flat):
    names = list(table)
    lead = flat[names[0]].shape[:-1]
    parts, pos = [], 0
    for nm in names:
        off, size = table[nm]
        if off > pos:
            parts.append(jnp.zeros(lead + (off - pos,), flat[nm].dtype))
        parts.append(flat[nm])
        pos = off + size
    total = rows * LANES
    if total > pos:
        parts.append(jnp.zeros(lead + (total - pos,), parts[-1].dtype))
    return jnp.concatenate(parts, axis=-1).reshape(lead + (rows, LANES))


def _unpack(table, packed, nm):
    off, size = table[nm]
    flat = packed.reshape(packed.shape[:-2] + (-1,))
    return flat[..., off:off + size]


def _split_shards(full, ax):
    shp = full.shape
    t = full.reshape(shp[:ax] + (N_SHARD, shp[ax] // N_SHARD) + shp[ax + 1:])
    return jnp.moveaxis(t, ax, 0)


def _join_shards(sh4, ax):
    return jnp.concatenate([sh4[s] for s in range(N_SHARD)], axis=ax)


ANY = pl.BlockSpec(memory_space=pl.ANY)


def _me():
    return lax.axis_index("x"), lax.axis_index("y"), lax.axis_index("c")


def _comm_call(name, body, arrays, out_shapes, n_sem):
    n = len(arrays)
    scratch = [pltpu.SemaphoreType.DMA((n, n_sem)), pltpu.SemaphoreType.DMA((n, n_sem))]
    return pl.pallas_call(
        body, name=name, out_shape=out_shapes, in_specs=[ANY] * n, out_specs=[ANY] * n, scratch_shapes=scratch,
        compiler_params=pltpu.CompilerParams(has_side_effects=True),
    )(*arrays)


def _all_gather(name, xs):
    n = len(xs)

    def body(*refs):
        x_refs, out_refs = refs[:n], refs[n:2 * n]
        send_sems, recv_sems = refs[2 * n:]
        mx, my, mc = _me()
        chips = [(1 - mx, my), (mx, 1 - my), (1 - mx, 1 - my)]

        def copy(a, k, shard, half, to, src=None):
            dst = out_refs[a].at[shard, half]
            return pltpu.make_async_remote_copy(src_ref=dst if src is None else src, dst_ref=dst, send_sem=send_sems.at[a, k],
                                                recv_sem=recv_sems.at[a, k], device_id=to, device_id_type=MESH)

        first = [copy(a, j, 2 * mx + my, mc, (cx, cy, mc), src=x_refs[a].at[mc]) for j, (cx, cy) in enumerate(chips) for a in range(n)]
        for cp in first:
            cp.start()
        passed = []
        for j, (cx, cy) in enumerate(chips):
            for a in range(n):
                copy(a, j, 2 * cx + cy, mc, (mx, my, mc)).wait_recv()
                passed.append(copy(a, 3 + j, 2 * cx + cy, mc, (mx, my, 1 - mc)))
                passed[-1].start()
        for j, (cx, cy) in enumerate(chips):
            for a in range(n):
                copy(a, 3 + j, 2 * cx + cy, 1 - mc, (mx, my, mc)).wait_recv()
        for cp in first + passed:
            cp.wait_send()

    return _comm_call(name, body, xs, [jax.ShapeDtypeStruct((N_SHARD,) + x.shape, x.dtype) for x in xs], 6)


def _pair_exchange(name, items):
    n = len(items)

    def body(*refs):
        src_refs, dst_refs = refs[:n], refs[n:2 * n]
        send_sems, recv_sems = refs[2 * n:]
        mx, my, mc = _me()
        cps = [pltpu.make_async_remote_copy(src_ref=src_refs[a].at[s, 1 - mc], dst_ref=dst_refs[a].at[s], send_sem=send_sems.at[a, s],
                                            recv_sem=recv_sems.at[a, s], device_id=(mx, my, 1 - mc), device_id_type=MESH)
               for a in range(n) for s in range(N_SHARD)]
        for cp in cps:
            cp.start()
        for cp in cps:
            cp.wait()

    return _comm_call(name, body, items, [jax.ShapeDtypeStruct((N_SHARD,) + t.shape[2:], t.dtype) for t in items], N_SHARD)


def _chip_exchange(name, parts):
    n = len(parts)

    def body(*refs):
        p_refs, dst_refs = refs[:n], refs[n:2 * n]
        send_sems, recv_sems = refs[2 * n:]
        mx, my, mc = _me()
        chips = [(1 - mx, my), (mx, 1 - my), (1 - mx, 1 - my)]
        cps = [pltpu.make_async_remote_copy(src_ref=p_refs[a].at[2 * cx + cy], dst_ref=dst_refs[a].at[j], send_sem=send_sems.at[a, j],
                                            recv_sem=recv_sems.at[a, j], device_id=(cx, cy, mc), device_id_type=MESH)
               for a in range(n) for j, (cx, cy) in enumerate(chips)]
        for cp in cps:
            cp.start()
        for cp in cps:
            cp.wait()

    return _comm_call(name, body, parts, [jax.ShapeDtypeStruct((3,) + t.shape[1:], t.dtype) for t in parts], 3)


def _pair_swap(name, reds):
    n = len(reds)

    def body(*refs):
        r_refs, out_refs = refs[:n], refs[n:2 * n]
        send_sems, recv_sems = refs[2 * n:]
        mx, my, mc = _me()
        cps = [pltpu.make_async_remote_copy(src_ref=r_refs[a], dst_ref=out_refs[a], send_sem=send_sems.at[a, 0],
                                            recv_sem=recv_sems.at[a, 0], device_id=(mx, my, 1 - mc), device_id_type=MESH)
               for a in range(n)]
        for cp in cps:
            cp.start()
        for cp in cps:
            cp.wait()

    return _comm_call(name, body, reds, [jax.ShapeDtypeStruct(t.shape, t.dtype) for t in reds], 1)


EW_BLOCK_BYTES = 1 << 20


def _ew_call(name, fn, ins, n_out, out_dtype=f32):
    shape = ins[0].shape
    last = shape[-1]
    flat = [a.reshape(-1, last) for a in ins]
    R = flat[0].shape[0]
    cands = tuple(c for c in (4096, 2048, 1024, 512, 256, 128, 64, 32, 16, 8) if c * last * 4 <= EW_BLOCK_BYTES)
    tr = _pick(R, cands)
    res = _tile_call(name, lambda i, *a: fn(*a), R // tr, [(a, (tr, last), lambda i: (i, 0)) for a in flat],
                     [((R, last), out_dtype, (tr, last), lambda i: (i, 0))] * n_out)
    return [r.reshape(shape) for r in res]


def _adamw(g, w, m, v):
    m = B1 * m + (1.0 - B1) * g
    v = B2 * v + (1.0 - B2) * jnp.square(g)
    m_hat = m / (1.0 - B1 ** STEP)
    v_hat = v / (1.0 - B2 ** STEP)
    return -LR * (m_hat / (jnp.sqrt(v_hat) + EPS) + WD * w), m, v


def _adamw_call(name, mine, theirs, w, m, v, mc1):
    shape, last = w.shape, w.shape[-1]
    g2 = [t.reshape(-1, last) for t in (mine, theirs)]
    w3 = [t.reshape(2, -1, last) for t in (w, m, v)]
    R = g2[0].shape[0]
    tr = _pick(R, tuple(c for c in (4096, 2048, 1024, 512, 256, 128, 64, 32, 16, 8) if c * last * 4 <= EW_BLOCK_BYTES))

    def body(mc_ref, mine_ref, theirs_ref, w_ref, m_ref, v_ref, g_out, d_out, m_out, v_out):
        g = jnp.where(pl.program_id(0) == mc_ref[0], mine_ref[...], theirs_ref[...])
        d, nm, nv = _adamw(g, w_ref[0], m_ref[0], v_ref[0])
        g_out[0], d_out[0], m_out[0], v_out[0] = g, d, nm, nv

    half = pl.BlockSpec((tr, last), lambda h, i: (i, 0))
    full = pl.BlockSpec((1, tr, last), lambda h, i: (h, i, 0))
    res = pl.pallas_call(
        body, name=name, grid=(2, R // tr),
        in_specs=[pl.BlockSpec(memory_space=pltpu.SMEM), half, half, full, full, full], out_specs=[full] * 4,
        out_shape=[jax.ShapeDtypeStruct((2, R, last), f32)] * 4,
        compiler_params=pltpu.CompilerParams(dimension_semantics=("arbitrary", "arbitrary"), vmem_limit_bytes=VMEM_LIMIT),
    )(mc1, *g2, *w3)
    return tuple(r.reshape(shape) for r in res)


def kernel(x, mem, mem_ln_g, mem_ln_b, ln_g, ln_b, ffn1_w_gu, ffn1_w_down, w_in, dn_conv_w, dn_a_log, dn_dt_bias, dn_norm_w, swa_sinks, w_mem_kv, w_branch, w_out, ffn2_w_gu, ffn2_w_down, loss_target, m_mem_ln_g, m_mem_ln_b, m_ln_g, m_ln_b, m_ffn1_w_gu, m_ffn1_w_down, m_w_in, m_dn_conv_w, m_dn_a_log, m_dn_dt_bias, m_dn_norm_w, m_swa_sinks, m_w_mem_kv, m_w_branch, m_w_out, m_ffn2_w_gu, m_ffn2_w_down, v_mem_ln_g, v_mem_ln_b, v_ln_g, v_ln_b, v_ffn1_w_gu, v_ffn1_w_down, v_w_in, v_dn_conv_w, v_dn_a_log, v_dn_dt_bias, v_dn_norm_w, v_swa_sinks, v_w_mem_kv, v_w_branch, v_w_out, v_ffn2_w_gu, v_ffn2_w_down):
    args = dict(locals())
    Ws = {n: args[n] for n in WEIGHTS}
    Ms = {n: args["m_" + n] for n in WEIGHTS}
    Vs = {n: args["v_" + n] for n in WEIGHTS}
    mc = lax.axis_index("c")
    my_s = 2 * lax.axis_index("x") + lax.axis_index("y")
    small = [n for n in WEIGHTS if n not in MATRICES]

    ag_table, ag_rows = _layout(SMALL_SHARDED, {n: Ws[n].size for n in SMALL_SHARDED}, 16)
    ag_small = _pack(ag_table, ag_rows, {n: Ws[n].reshape(-1) for n in SMALL_SHARDED}).reshape(2, ag_rows // 2, LANES)
    local = [Ws[n].astype(bf16) for n in MATRICES] + [ag_small]
    gathered = _all_gather("all_gather_w", local)
    gathered = [jnp.stack([jnp.where(my_s == s, loc, g[s]) for s in range(N_SHARD)], axis=0) for loc, g in zip(local, gathered)]
    Wf = {n: _join_shards(g, SHARD_AXIS[n]) for n, g in zip(MATRICES, gathered)}
    g_small = gathered[-1].reshape(N_SHARD, ag_rows, LANES)
    for n in SMALL_SHARDED:
        Wf[n] = _join_shards(_unpack(ag_table, g_small, n).reshape((N_SHARD,) + Ws[n].shape), SHARD_AXIS[n])
    for n in WEIGHTS:
        if SHARD_AXIS[n] is None:
            Wf[n] = Ws[n]

    loss_row, dx, G = _local_step(x[0], mem[0], loss_target[0], Wf)

    table, rows = _layout(small + ['loss'], {**{n: Ws[n].size for n in small}, 'loss': 1}, 16)
    gflat = {n: (jnp.broadcast_to(G[n].reshape(1, -1), (N_SHARD, G[n].size)) if SHARD_AXIS[n] is None
                 else _split_shards(G[n], SHARD_AXIS[n]).reshape(N_SHARD, -1)) for n in small}
    gflat['loss'] = jnp.broadcast_to(loss_row[:, :1], (N_SHARD, 1))
    items = [_split_shards(G[n], SHARD_AXIS[n]) for n in MATRICES] + [_pack(table, rows, gflat).reshape(N_SHARD, 2, rows // 2, LANES)]
    tags = MATRICES + ['small']
    got = _pair_exchange("rs_pair", items)
    keep = [lax.dynamic_index_in_dim(a, mc, axis=1, keepdims=False) for a in items]
    wire = [bf16] * len(MATRICES) + [f32]
    part = [_ew_call(f"rs_add_pair_{t}", lambda a, b: a + b, [k, g], 1, out_dtype=dt)[0] for t, k, g, dt in zip(tags, keep, got, wire)]
    others = _chip_exchange("rs_chips", part)
    own = lambda a: lax.dynamic_index_in_dim(a, my_s, axis=0, keepdims=False)
    red = [_ew_call(f"rs_add_chips_{t}", lambda k, g, fx, fy, fxy: ((k + g) + fy) + (fx + fxy), [own(k), own(g), o[0], o[1], o[2]], 1)[0]
           for t, k, g, o in zip(tags, keep, got, others)]
    theirs = _pair_swap("rs_swap", red)

    mc1 = mc.astype(i32).reshape(1)
    outs = {}
    for n, a, b in zip(MATRICES, red, theirs):
        outs[n] = _adamw_call(f"adamw_{n}", a, b, Ws[n], Ms[n], Vs[n], mc1)
    fill = {'loss': jnp.zeros((1,), f32)}
    packs = [_pack(table, rows, {**{n: src[n].reshape(-1) for n in small}, **fill}).reshape(2, rows // 2, LANES) for src in (Ws, Ms, Vs)]
    small_out = [p.reshape(rows, LANES) for p in _adamw_call("adamw_small", red[-1], theirs[-1], *packs, mc1)]
    for n in small:
        outs[n] = tuple(_unpack(table, p, n).reshape(Ws[n].shape) for p in small_out)
    loss = _unpack(table, small_out[0], 'loss').reshape(())
    return (loss, dx[None], *[outs[n][k] for k in range(4) for n in WEIGHTS])
```

```python
import functools

import jax
import jax.numpy as jnp
from jax import lax
from jax.experimental import pallas as pl
from jax.experimental.pallas import tpu as pltpu

f32, bf16, i32 = jnp.float32, jnp.bfloat16, jnp.int32
HI = lax.Precision.HIGHEST
MESH = pl.DeviceIdType.MESH

D_MODEL = 1024
DEPTH = 2
DN_HEADS, DN_DK, DN_CONV, DN_CHUNK = 4, 128, 4, 64
SWA_HEADS, SWA_KV_HEADS, SWA_DH, WINDOW = 8, 2, 64, 128
XA_HEADS, XA_DH = 4, 128
D_FF = 2816
N_BRANCH, BRANCH_W = 3, 512
ALPHA = (2 * DEPTH) ** 0.25
LN_EPS, RMS_EPS, NEG_INF = 1e-5, 1e-6, -1e30
D_IN = 6408
D_INP = 6656
LR, B1, B2, EPS, WD, STEP = 0.001, 0.9, 0.999, 1e-08, 0.01, 10

LANES = 128
VMEM_LIMIT = 56 << 20
N_SHARD = 4
SCAN_CHUNKS = 4
HALO = 16
MM_TILES = {
    'ffn1_gu_dx': (1024, 1024, 2816), 'ffn2_gu_dx': (1024, 1024, 2816), 'w_in': (2048, 512, 1024),
    'w_in_dw': (1024, 512, 4096), 'ffn1_down_dw': (1408, 512, 4096), 'ffn2_down_dw': (1408, 512, 4096),
    'ffn1_down_dx': (1024,), 'ffn2_down_dx': (1024,), 'ffn1_down': (1024,), 'ffn2_down': (1024,), 'w_out': (1024,),
    'ffn1_gu': (1024,), 'ffn2_gu': (1024,),
    'l1_ffn1_gu_dx': (512, 1024, 5632), 'l1_w_in_dx': (256, 1024, 6656), 'l1_ffn2_gu_dx': (256, 1024, 5632),
}


def _tiles(name, default=None):
    return MM_TILES.get(name, MM_TILES.get(name.split('_', 1)[-1], default))

WEIGHTS = ['mem_ln_g', 'mem_ln_b', 'ln_g', 'ln_b', 'ffn1_w_gu', 'ffn1_w_down', 'w_in', 'dn_conv_w', 'dn_a_log',
           'dn_dt_bias', 'dn_norm_w', 'swa_sinks', 'w_mem_kv', 'w_branch', 'w_out', 'ffn2_w_gu', 'ffn2_w_down']
SHARD_AXIS = {'mem_ln_g': None, 'mem_ln_b': None, 'ln_g': 2, 'ln_b': 2, 'ffn1_w_gu': 2, 'ffn1_w_down': 1, 'w_in': 2,
              'dn_conv_w': 2, 'dn_a_log': None, 'dn_dt_bias': None, 'dn_norm_w': None, 'swa_sinks': None,
              'w_mem_kv': 1, 'w_branch': 3, 'w_out': 1, 'ffn2_w_gu': 2, 'ffn2_w_down': 1}
MATRICES = ['ffn1_w_gu', 'ffn1_w_down', 'w_in', 'w_mem_kv', 'w_branch', 'w_out', 'ffn2_w_gu', 'ffn2_w_down']
SMALL_SHARDED = ['ln_g', 'ln_b', 'dn_conv_w']


def _dg(a, b, mode, hi):
    nb = a.ndim - 2
    bd = tuple(range(nb))
    ca = nb if mode == 'tn' else nb + 1
    cb = nb + 1 if mode == 'nt' else nb
    dims = (((ca,), (cb,)), (bd, bd))
    dot = lambda x, y: lax.dot_general(x, y, dims, preferred_element_type=f32)
    a_hi, b_hi = a.astype(bf16), b.astype(bf16)
    if not hi:
        return dot(a_hi, b_hi)
    a_lo = (a.astype(f32) - a_hi.astype(f32)).astype(bf16)
    b_lo = (b.astype(f32) - b_hi.astype(f32)).astype(bf16)
    return dot(a_hi, b_hi) + (dot(a_hi, b_lo) + dot(a_lo, b_hi))


@functools.partial(jax.custom_vjp, nondiff_argnums=(2, 3))
def _dot(a, b, mode, hi):
    return _dg(a, b, mode, hi)


def _dot_fwd(a, b, mode, hi):
    return _dg(a, b, mode, hi), (a, b)


def _dot_bwd(mode, hi, res, g):
    a, b = res
    if mode == 'nn':
        da, db = _dg(g, b, 'nt', hi), _dg(a, g, 'tn', hi)
    elif mode == 'nt':
        da, db = _dg(g, b, 'nn', hi), _dg(g, a, 'tn', hi)
    else:
        da, db = _dg(b, g, 'nt', hi), _dg(a, g, 'nn', hi)
    return da.astype(a.dtype), db.astype(b.dtype)


_dot.defvjp(_dot_fwd, _dot_bwd)


def bdot(a, b, mode):
    return _dot(a, b, mode, False)


def hdot(a, b, mode):
    return _dot(a, b, mode, True)


def _shift_rows_impl(x, tail, s):
    r = pltpu.roll(x, s, 0)
    rows = lax.broadcasted_iota(i32, tail.shape, 0)
    top = jnp.where(rows >= s, r[:HALO], pltpu.roll(tail, s, 0))
    return jnp.concatenate([top, r[HALO:]], axis=0)


@functools.partial(jax.custom_vjp, nondiff_argnums=(2,))
def _shift_rows(x, tail, s):
    return _shift_rows_impl(x, tail, s)


def _shift_rows_fwd(x, tail, s):
    return _shift_rows_impl(x, tail, s), None


def _shift_rows_bwd(s, _, g):
    n = g.shape[0]
    r = pltpu.roll(g, n - s, 0)
    rows = lax.broadcasted_iota(i32, (HALO, g.shape[1]), 0)
    last = r[n - HALO:]
    keep = rows < HALO - s
    dx = jnp.concatenate([r[:n - HALO], jnp.where(keep, last, 0.0)], axis=0)
    return dx, jnp.where(keep, 0.0, pltpu.roll(g[:HALO], HALO - s, 0))


_shift_rows.defvjp(_shift_rows_fwd, _shift_rows_bwd)


def _lane_pick(x, lane):
    idx = lax.broadcasted_iota(i32, x.shape, x.ndim - 1)
    return jnp.sum(jnp.where(idx == lane, x, 0.0), axis=-1, keepdims=True)


def _silu(x):
    return x * jax.nn.sigmoid(x)


def _tri_inv(a):
    C = a.shape[-1]
    eye = (lax.broadcasted_iota(i32, a.shape, 1) == lax.broadcasted_iota(i32, a.shape, 2)).astype(f32)
    p = -a
    x = eye + p
    for _ in range((C - 1).bit_length() - 1):
        p = _dg(p, p, 'nn', True)
        x = x + _dg(x, p, 'nn', True)
    return x


@jax.custom_vjp
def _tri_solve(a, rhs):
    return _dg(_tri_inv(a), rhs, 'nn', True)


def _tri_solve_fwd(a, rhs):
    tinv = _tri_inv(a)
    sol = _dg(tinv, rhs, 'nn', True)
    return sol, (tinv, sol)


def _tri_solve_bwd(res, g):
    tinv, sol = res
    d_rhs = _dg(tinv, g, 'tn', True)
    return -_dg(d_rhs, sol, 'nt', True), d_rhs


_tri_solve.defvjp(_tri_solve_fwd, _tri_solve_bwd)


@functools.partial(jax.custom_vjp, nondiff_argnums=(1,))
def _lane_head(x, n):
    return x[:, :, :n]


def _lane_head_fwd(x, n):
    return x[:, :, :n], None


def _lane_head_bwd(n, _, g):
    s = jnp.sum(g, axis=-1, keepdims=True) * (1.0 / LANES)
    return (jnp.broadcast_to(s, g.shape[:-1] + (LANES,)),)


_lane_head.defvjp(_lane_head_fwd, _lane_head_bwd)


@functools.partial(jax.custom_vjp, nondiff_argnums=(1,))
def _last_row(x, c):
    return x[:, c - 1:, :]


def _last_row_fwd(x, c):
    return x[:, c - 1:, :], None


def _last_row_bwd(c, _, g):
    shape = (g.shape[0], c, g.shape[2])
    rows = lax.broadcasted_iota(i32, shape, 1)
    return (jnp.where(rows == c - 1, jnp.broadcast_to(g, shape), 0.0),)


_last_row.defvjp(_last_row_fwd, _last_row_bwd)


def _full(a):
    nd = a.ndim
    return (a, tuple(a.shape), lambda i, _nd=nd: (0,) * _nd)


def _rows(a, tm, col=0, width=None):
    width = a.shape[1] if width is None else width
    return (a, (tm, width), lambda i, _c=col: (i, _c))


def _tile_call(name, fn, n, ins, outs, accs=()):
    n_in, n_out, n_acc = len(ins), len(outs), len(accs)

    def body(*refs):
        i = pl.program_id(0)
        res = fn(i, *[r[...].astype(f32) for r in refs[:n_in]])
        if not isinstance(res, (tuple, list)):
            res = (res,)
        assert len(res) == n_out + n_acc, (name, len(res), n_out, n_acc)
        for r, v in zip(refs[n_in:n_in + n_out], res[:n_out]):
            r[...] = v.astype(r.dtype)
        if n_acc:
            acc_refs = refs[n_in + n_out:]

            @pl.when(i == 0)
            def _():
                for r in acc_refs:
                    r[...] = jnp.zeros(r.shape, r.dtype)

            for r, v in zip(acc_refs, res[n_out:]):
                r[...] += v.astype(r.dtype)

    out_shape = [jax.ShapeDtypeStruct(s, d) for s, d, _, _ in outs] + [jax.ShapeDtypeStruct(s, d) for s, d in accs]
    out_specs = [pl.BlockSpec(b, m) for _, _, b, m in outs]
    out_specs += [pl.BlockSpec(tuple(s), lambda i, _nd=len(s): (0,) * _nd) for s, _ in accs]
    res = pl.pallas_call(
        body, name=name, grid=(n,),
        in_specs=[pl.BlockSpec(b, m) for _, b, m in ins],
        out_specs=out_specs, out_shape=out_shape,
        compiler_params=pltpu.CompilerParams(dimension_semantics=("arbitrary",), vmem_limit_bytes=VMEM_LIMIT),
    )(*[a for a, _, _ in ins])
    return res


def _pick(n, cands):
    for c in cands:
        if n % c == 0:
            return c
    return n


def _mm(name, a, b, mode, out_dtype=f32, add=None, out_cols=None):
    if mode == 'tn':
        K, M = a.shape
    else:
        M, K = a.shape
    N = b.shape[0] if mode == 'nt' else b.shape[1]
    tm = _pick(M, (1024, 1408, 512, 256, 128))
    tn = _pick(N, (512, 1408, 256, 128)) if out_cols is None else out_cols[0]
    tuned = _tiles(name)
    if tuned is not None and (M % tuned[0] or N % tuned[1] or K % tuned[2]):
        tuned = None
    if tuned is not None:
        tm, tn = tuned[:2]
    col = (lambda j: j) if out_cols is None else out_cols[1]
    tk = K if K <= 3328 else _pick(K, (3328, 2816, 2048, 1024, 512, 256, 128))
    if tuned is not None:
        tk = tuned[2]
    nk = K // tk
    ca = 0 if mode == 'tn' else 1
    cb = 1 if mode == 'nt' else 0
    dims = (((ca,), (cb,)), ((), ()))

    def body(*refs):
        a_ref, b_ref = refs[0], refs[1]
        add_ref = refs[2] if add is not None else None
        part = lax.dot_general(a_ref[...].astype(bf16), b_ref[...].astype(bf16), dims, preferred_element_type=f32)

        def finish(r, o_ref):
            if add_ref is not None:
                r = r + add_ref[...].astype(f32)
            o_ref[...] = r.astype(o_ref.dtype)

        if nk == 1:
            finish(part, refs[-1])
            return
        o_ref, acc_ref = refs[-2], refs[-1]
        k = pl.program_id(2)

        @pl.when(k == 0)
        def _():
            acc_ref[...] = part

        @pl.when(k > 0)
        def _():
            acc_ref[...] += part

        @pl.when(k == nk - 1)
        def _():
            finish(acc_ref[...], o_ref)

    a_spec = pl.BlockSpec((tk, tm), lambda i, j, k: (k, i)) if mode == 'tn' else pl.BlockSpec((tm, tk), lambda i, j, k: (i, k))
    b_spec = pl.BlockSpec((tn, tk), lambda i, j, k: (j, k)) if mode == 'nt' else pl.BlockSpec((tk, tn), lambda i, j, k: (k, j))
    in_specs, args = [a_spec, b_spec], [a, b]
    if add is not None:
        in_specs.append(pl.BlockSpec((tm, tn), lambda i, j, k: (i, j)))
        args.append(add)
    return pl.pallas_call(
        body, name=name, grid=(M // tm, N // tn, nk), in_specs=in_specs,
        out_specs=pl.BlockSpec((tm, tn), lambda i, j, k: (i, col(j))),
        out_shape=jax.ShapeDtypeStruct((M, N), out_dtype),
        scratch_shapes=[pltpu.VMEM((tm, tn), f32)] if nk > 1 else [],
        compiler_params=pltpu.CompilerParams(dimension_semantics=("parallel", "parallel", "arbitrary"), vmem_limit_bytes=VMEM_LIMIT),
    )(*args)


def _swiglu(u):
    return _silu(u[:, :D_FF]) * u[:, D_FF:]


def _res_ln(x, f, g, b, s):
    r = ALPHA * x + s * f
    mu = jnp.mean(r, axis=-1, keepdims=True)
    rc = r - mu
    var = jnp.mean(rc * rc, axis=-1, keepdims=True)
    return rc * lax.rsqrt(var + LN_EPS) * g + b


def _ln(x, g, b):
    mu = jnp.mean(x, axis=-1, keepdims=True)
    xc = x - mu
    var = jnp.mean(xc * xc, axis=-1, keepdims=True)
    return xc * lax.rsqrt(var + LN_EPS) * g + b


def _dn_pre(first, xc, xp, ba, cw, hp):
    xp = jnp.where(first, 0.0, xp)
    y = cw[DN_CONV - 1:DN_CONV, :] * xc
    for j in range(DN_CONV - 1):
        y = y + cw[j:j + 1, :] * _shift_rows(xc, xp, DN_CONV - 1 - j)
    c = _silu(y)
    qs, ks, vs, gs, bs = [], [], [], [], []
    nqk = DN_HEADS * DN_DK
    for h in range(DN_HEADS):
        q = c[:, h * DN_DK:(h + 1) * DN_DK]
        k = c[:, nqk + h * DN_DK:nqk + (h + 1) * DN_DK]
        v = c[:, 2 * nqk + h * DN_DK:2 * nqk + (h + 1) * DN_DK]
        qs.append(q * lax.rsqrt(jnp.sum(q * q, axis=-1, keepdims=True) + RMS_EPS))
        ks.append(k * lax.rsqrt(jnp.sum(k * k, axis=-1, keepdims=True) + RMS_EPS))
        vs.append(v)
        beta = jax.nn.sigmoid(_lane_pick(ba, h))
        a_log = _lane_pick(hp[0:1, :], h)
        dt = _lane_pick(hp[1:2, :], h)
        g = -jnp.exp(a_log) * jax.nn.softplus(_lane_pick(ba, DN_HEADS + h) + dt)
        gs.append(jnp.broadcast_to(g, q.shape))
        bs.append(jnp.broadcast_to(beta, q.shape))
    return tuple(jnp.stack(t, axis=0) for t in (qs, ks, vs, gs, bs))


def _wy(q, k, v, gb, bb):
    B, C, _ = q.shape
    ri = lax.broadcasted_iota(i32, (B, C, C), 1)
    ci = lax.broadcasted_iota(i32, (B, C, C), 2)
    tril, strict = ri >= ci, ri > ci
    gc = hdot(tril.astype(f32), gb, 'nn')
    gl = jnp.broadcast_to(_last_row(gc, C), gc.shape)
    col = _lane_head(gc, C)
    decay = jnp.exp(jnp.where(tril, col - jnp.swapaxes(col, 1, 2), NEG_INF))
    qs = q * (DN_DK ** -0.5)
    kb = k * bb
    a = jnp.where(strict, bdot(kb, k, 'nt') * decay, 0.0)
    sol = _tri_solve(a, jnp.concatenate([v * bb, kb * jnp.exp(gc)], axis=-1))
    qk = jnp.where(tril, bdot(qs, k, 'nt') * decay, 0.0)
    kt = k * jnp.exp(gl - gc)
    qd = qs * jnp.exp(gc)
    return qd, qk, sol[..., :DN_DK], sol[..., DN_DK:], kt, jnp.exp(gl)


def _scan_step(S, qd, qk, u, w, kt, egl):
    vn = u - bdot(w, S, 'nn')
    o = bdot(qd, S, 'nn') + bdot(qk, vn, 'nn')
    e2 = jnp.concatenate([egl] * (DN_DK // DN_CHUNK), axis=1)
    return o, S * e2 + bdot(kt, vn, 'tn')


def _dn_post(o, z, nw):
    outs = []
    for h in range(DN_HEADS):
        oh = o[h]
        oh = oh * lax.rsqrt(jnp.mean(oh * oh, axis=-1, keepdims=True) + RMS_EPS) * nw
        outs.append(oh * _silu(z[:, h * DN_DK:(h + 1) * DN_DK]))
    return jnp.concatenate(outs, axis=1)


def _swa(first, q, kc, kp, vc, vp, snk):
    W = q.shape[1]
    G = SWA_HEADS // SWA_KV_HEADS
    r = lax.broadcasted_iota(i32, (G, W, 2 * W), 1)
    c = lax.broadcasted_iota(i32, (G, W, 2 * W), 2)
    mask = (c > W + r - WINDOW) & (c <= W + r) & jnp.logical_or(c >= W, jnp.logical_not(first))
    sink_all = _lane_pick(snk, 0)
    outs = []
    for j in range(SWA_KV_HEADS):
        qj = q[j * G:(j + 1) * G]
        kk = jnp.broadcast_to(jnp.concatenate([kp[j], kc[j]], axis=0)[None], (G, 2 * W, SWA_DH))
        vv = jnp.broadcast_to(jnp.concatenate([vp[j], vc[j]], axis=0)[None], (G, 2 * W, SWA_DH))
        s = jnp.where(mask, bdot(qj, kk, 'nt') * (SWA_DH ** -0.5), NEG_INF)
        sink = sink_all[j * G:(j + 1) * G][:, :, None]
        m = jnp.maximum(jnp.max(s, axis=-1, keepdims=True), sink)
        p = jnp.exp(s - m)
        p = p / (jnp.sum(p, axis=-1, keepdims=True) + jnp.exp(sink - m))
        outs.append(bdot(p, vv, 'nn'))
    return jnp.concatenate(outs, axis=0)


def _xa(q, kv):
    outs = []
    nk = XA_HEADS * XA_DH
    for h in range(XA_HEADS):
        qh = q[:, h * XA_DH:(h + 1) * XA_DH]
        kh = kv[:, h * XA_DH:(h + 1) * XA_DH]
        vh = kv[:, nk + h * XA_DH:nk + (h + 1) * XA_DH]
        s = bdot(qh, kh, 'nt') * (XA_DH ** -0.5)
        m = jnp.max(s, axis=-1, keepdims=True)
        p = jnp.exp(s - m)
        p = p / jnp.sum(p, axis=-1, keepdims=True)
        outs.append(bdot(p, vh, 'nn'))
    return jnp.concatenate(outs, axis=1)


def _merge(o_dn, o_sw, o_xa, gates, wb):
    acc = None
    for n, o in enumerate((o_dn, o_sw, o_xa)):
        t = jax.nn.sigmoid(gates[:, n * D_MODEL:(n + 1) * D_MODEL]) * bdot(o, wb[n], 'nn')
        acc = t if acc is None else acc + t
    return acc


def _row_tile(T, want):
    return _pick(T, tuple(c for c in (1024, 512, 256, 128, 64) if c <= want))


def _res_ln_fwd(name, x, f, g, b, s):
    T = x.shape[0]
    tm = _row_tile(T, 512)

    def fn(i, x, f, g, b):
        h = _res_ln(x, f, g, b, s)
        return h, h

    return _tile_call(name, fn, T // tm, [_rows(x, tm), _rows(f, tm), _full(g), _full(b)],
                      [((T, D_MODEL), f32, (tm, D_MODEL), lambda i: (i, 0)), ((T, D_MODEL), bf16, (tm, D_MODEL), lambda i: (i, 0))])


def _res_ln_bwd(name, x, f, g, b, s, dh):
    T = x.shape[0]
    tm = _row_tile(T, 512)

    def fn(i, x, f, g, b, dh):
        _, vjp = jax.vjp(lambda x, f, g, b: _res_ln(x, f, g, b, s), x, f, g, b)
        return vjp(dh)

    return _tile_call(name, fn, T // tm, [_rows(x, tm), _rows(f, tm), _full(g), _full(b), _rows(dh, tm)],
                      [((T, D_MODEL), f32, (tm, D_MODEL), lambda i: (i, 0)), ((T, D_MODEL), bf16, (tm, D_MODEL), lambda i: (i, 0))],
                      [((1, D_MODEL), f32), ((1, D_MODEL), f32)])


FF_BLK = D_FF // 2


def _pair_gate_up(w):
    d = w.shape[0]
    return jnp.stack([w[:, :D_FF].reshape(d, -1, FF_BLK), w[:, D_FF:].reshape(d, -1, FF_BLK)], axis=2).reshape(d, 2 * D_FF)


def _swiglu_blk(u):
    return _silu(u[:, :FF_BLK]) * u[:, FF_BLK:]


def _mm_spec(tm, tn):
    return dict(compiler_params=pltpu.CompilerParams(dimension_semantics=("parallel", "arbitrary"), vmem_limit_bytes=VMEM_LIMIT))


def _gu_act(name, xb, w_pair):
    T, D = xb.shape
    tm = _pick(T, (_tiles(name, (512,))[0], 512, 256, 128))

    def body(x_ref, w_ref, u_ref, a_ref):
        r = lax.dot_general(x_ref[...], w_ref[...], (((1,), (0,)), ((), ())), preferred_element_type=f32)
        u_ref[...] = r.astype(bf16)
        a_ref[...] = _swiglu_blk(r).astype(bf16)

    return pl.pallas_call(
        body, name=name, grid=(T // tm, D_FF // FF_BLK),
        in_specs=[pl.BlockSpec((tm, D), lambda i, j: (i, 0)), pl.BlockSpec((D, 2 * FF_BLK), lambda i, j: (0, j))],
        out_specs=[pl.BlockSpec((tm, 2 * FF_BLK), lambda i, j: (i, j)), pl.BlockSpec((tm, FF_BLK), lambda i, j: (i, j))],
        out_shape=[jax.ShapeDtypeStruct((T, 2 * D_FF), bf16), jax.ShapeDtypeStruct((T, D_FF), bf16)],
        **_mm_spec(tm, FF_BLK))(xb, w_pair)


def _down_dx_act_bwd(name, df, w_down_t, u):
    T, D = df.shape
    tm = _pick(T, (_tiles(name, (256,))[0], 256, 128))

    def body(df_ref, w_ref, u_ref, du_ref):
        da = lax.dot_general(df_ref[...], w_ref[...], (((1,), (0,)), ((), ())), preferred_element_type=f32)
        _, vjp = jax.vjp(_swiglu_blk, u_ref[...].astype(f32))
        du_ref[...] = vjp(da)[0].astype(bf16)

    return pl.pallas_call(
        body, name=name, grid=(T // tm, D_FF // FF_BLK),
        in_specs=[pl.BlockSpec((tm, D), lambda i, j: (i, 0)), pl.BlockSpec((D, FF_BLK), lambda i, j: (0, j)),
                  pl.BlockSpec((tm, 2 * FF_BLK), lambda i, j: (i, j))],
        out_specs=pl.BlockSpec((tm, 2 * FF_BLK), lambda i, j: (i, j)),
        out_shape=jax.ShapeDtypeStruct((T, 2 * D_FF), bf16),
        **_mm_spec(tm, FF_BLK))(df, w_down_t, u)


def _mm_res_ln(name, a, w, x, g, b, s):
    T, K = a.shape
    tm = _pick(T, (_tiles(name, (512,))[0], 512, 256, 128))

    def body(a_ref, w_ref, x_ref, g_ref, b_ref, f_ref, h_ref, hb_ref):
        f = lax.dot_general(a_ref[...], w_ref[...], (((1,), (0,)), ((), ())), preferred_element_type=f32)
        h = _res_ln(x_ref[...], f, g_ref[...], b_ref[...], s)
        f_ref[...] = f
        h_ref[...] = h
        hb_ref[...] = h.astype(bf16)

    row = pl.BlockSpec((tm, D_MODEL), lambda i: (i, 0))
    vec = pl.BlockSpec((1, D_MODEL), lambda i: (0, 0))
    return pl.pallas_call(
        body, name=name, grid=(T // tm,),
        in_specs=[pl.BlockSpec((tm, K), lambda i: (i, 0)), pl.BlockSpec((K, D_MODEL), lambda i: (0, 0)), row, vec, vec],
        out_specs=[row, row, row],
        out_shape=[jax.ShapeDtypeStruct((T, D_MODEL), f32), jax.ShapeDtypeStruct((T, D_MODEL), f32), jax.ShapeDtypeStruct((T, D_MODEL), bf16)],
        compiler_params=pltpu.CompilerParams(dimension_semantics=("parallel",), vmem_limit_bytes=VMEM_LIMIT))(a, w, x, g, b)


def _hm_spec(tm, w=DN_DK):
    return ((DN_HEADS, tm, w), lambda i: (0, i, 0))


def _prev(i):
    return jnp.maximum(i - 1, 0)


def _dn_front(first, xc, xp, ba, cw, hp):
    tm = xc.shape[0]
    nc = tm // DN_CHUNK
    outs = _wy(*[t.reshape(DN_HEADS * nc, DN_CHUNK, DN_DK) for t in _dn_pre(first, xc, xp, ba, cw, hp)])
    return tuple(o.reshape(DN_HEADS, tm, o.shape[-1]) for o in outs)


def _dn_front_ins(p, pba, cw, hp, tm):
    W3 = 3 * DN_HEADS * DN_DK
    return [(p, (tm, W3), lambda i: (i, 0)), (p, (HALO, W3), lambda i: (jnp.maximum(i * (tm // HALO) - 1, 0), 0)),
            (pba, (tm, LANES), lambda i: (i, 0)), _full(cw), _full(hp)]


def _dn_front_fwd(name, p, pba, cw, hp):
    T = p.shape[0]
    tm = _row_tile(T, 256)
    blk, im = _hm_spec(tm)
    blk2, im2 = _hm_spec(tm, DN_CHUNK)
    full = (DN_HEADS, T, DN_DK)
    outs = [(full, bf16, blk, im), ((DN_HEADS, T, DN_CHUNK), bf16, blk2, im2), (full, f32, blk, im),
            (full, bf16, blk, im), (full, bf16, blk, im), (full, f32, blk, im)]
    return _tile_call(name, lambda i, *a: _dn_front(i == 0, *a), T // tm, _dn_front_ins(p, pba, cw, hp, tm), outs)


def _dn_front_bwd(name, p, pba, cw, hp, cts):
    T = p.shape[0]
    tm = _row_tile(T, 256)
    n = T // tm
    W3 = 3 * DN_HEADS * DN_DK
    blk, im = _hm_spec(tm)
    blk2, im2 = _hm_spec(tm, DN_CHUNK)
    ins = _dn_front_ins(p, pba, cw, hp, tm) + [(c, blk2 if c.shape[-1] == DN_CHUNK else blk, im) for c in cts]

    def fn(i, xc, xp, ba, cw, hp, *cts):
        _, vjp = jax.vjp(lambda *a: _dn_front(i == 0, *a), xc, xp, ba, cw, hp)
        return vjp(tuple(cts))

    return _tile_call(name, fn, n, ins,
                      [((T, W3), f32, (tm, W3), lambda i: (i, 0)), ((n * HALO, W3), f32, (HALO, W3), lambda i: (i, 0)),
                       ((T, LANES), f32, (tm, LANES), lambda i: (i, 0))],
                      [(tuple(cw.shape), f32), (tuple(hp.shape), f32)])


def _scan_fwd(name, qd, qk, u, w, kt, egl):
    T = qd.shape[1]
    C = DN_CHUNK
    U = _pick(T // C, (SCAN_CHUNKS, 1))
    n = T // (C * U)

    def body(qd_ref, qk_ref, u_ref, w_ref, kt_ref, egl_ref, o_ref, sall_ref, s_ref):
        i = pl.program_id(0)

        @pl.when(i == 0)
        def _():
            s_ref[...] = jnp.zeros(s_ref.shape, f32)

        S = s_ref[...]
        for j in range(U):
            sl = pl.ds(j * C, C)
            sall_ref[:, j] = S
            o, S = _scan_step(S, *[r[:, sl, :] for r in (qd_ref, qk_ref, u_ref, w_ref, kt_ref, egl_ref)])
            o_ref[:, sl, :] = o
        s_ref[...] = S

    blk, im = _hm_spec(C * U)
    blk2, im2 = _hm_spec(C * U, C)
    return pl.pallas_call(
        body, name=name, grid=(n,),
        in_specs=[pl.BlockSpec(blk, im), pl.BlockSpec(blk2, im2)] + [pl.BlockSpec(blk, im)] * 4,
        out_specs=[pl.BlockSpec(blk, im), pl.BlockSpec((DN_HEADS, U, DN_DK, DN_DK), lambda i: (0, i, 0, 0))],
        out_shape=[jax.ShapeDtypeStruct((DN_HEADS, T, DN_DK), f32), jax.ShapeDtypeStruct((DN_HEADS, n * U, DN_DK, DN_DK), f32)],
        scratch_shapes=[pltpu.VMEM((DN_HEADS, DN_DK, DN_DK), f32)],
        compiler_params=pltpu.CompilerParams(dimension_semantics=("arbitrary",), vmem_limit_bytes=VMEM_LIMIT),
    )(qd, qk, u, w, kt, egl)


def _scan_bwd(name, qd, qk, u, w, kt, egl, s_all, do):
    T = qd.shape[1]
    C = DN_CHUNK
    U = _pick(T // C, (SCAN_CHUNKS, 1))
    n = T // (C * U)

    def body(qd_ref, qk_ref, u_ref, w_ref, kt_ref, egl_ref, sall_ref, do_ref,
             dqd_ref, dqk_ref, du_ref, dw_ref, dkt_ref, degl_ref, ds_ref):
        i = pl.program_id(0)

        @pl.when(i == 0)
        def _():
            ds_ref[...] = jnp.zeros(ds_ref.shape, f32)

        dS = ds_ref[...]
        for j in reversed(range(U)):
            sl = pl.ds(j * C, C)
            args = [r[:, sl, :].astype(f32) for r in (qd_ref, qk_ref, u_ref, w_ref, kt_ref, egl_ref)]
            _, vjp = jax.vjp(_scan_step, sall_ref[:, j], *args)
            dS, *cts = vjp((do_ref[:, sl, :], dS))
            for r, v in zip((dqd_ref, dqk_ref, du_ref, dw_ref, dkt_ref, degl_ref), cts):
                r[:, sl, :] = v
        ds_ref[...] = dS

    blk = (DN_HEADS, C * U, DN_DK)
    blk2 = (DN_HEADS, C * U, C)
    rim = lambda i: (0, n - 1 - i, 0)
    sp, sp2 = pl.BlockSpec(blk, rim), pl.BlockSpec(blk2, rim)
    full, full2 = jax.ShapeDtypeStruct((DN_HEADS, T, DN_DK), f32), jax.ShapeDtypeStruct((DN_HEADS, T, C), f32)
    return pl.pallas_call(
        body, name=name, grid=(n,),
        in_specs=[sp, sp2, sp, sp, sp, sp, pl.BlockSpec((DN_HEADS, U, DN_DK, DN_DK), lambda i: (0, n - 1 - i, 0, 0)), sp],
        out_specs=[sp, sp2, sp, sp, sp, sp],
        out_shape=[full, full2, full, full, full, full],
        scratch_shapes=[pltpu.VMEM((DN_HEADS, DN_DK, DN_DK), f32)],
        compiler_params=pltpu.CompilerParams(dimension_semantics=("arbitrary",), vmem_limit_bytes=VMEM_LIMIT),
    )(qd, qk, u, w, kt, egl, s_all, do)


def _dn_post_fwd(name, o, p, nw):
    T = p.shape[0]
    tm = _row_tile(T, 512)
    blk, im = _hm_spec(tm)
    W = DN_HEADS * DN_DK
    return _tile_call(name, lambda i, o, z, nw: _dn_post(o, z, nw), T // tm,
                      [(o, blk, im), (p, (tm, W), lambda i: (i, 1536 // W)), _full(nw)],
                      [((T, W), f32, (tm, W), lambda i: (i, 0))])[0]


def _dn_post_bwd(name, o, p, nw, dout):
    T = p.shape[0]
    tm = _row_tile(T, 512)
    blk, im = _hm_spec(tm)
    W = DN_HEADS * DN_DK

    def fn(i, o, z, nw, dout):
        _, vjp = jax.vjp(_dn_post, o, z, nw)
        return vjp(dout)

    return _tile_call(name, fn, T // tm,
                      [(o, blk, im), (p, (tm, W), lambda i: (i, 1536 // W)), _full(nw), _rows(dout, tm)],
                      [((DN_HEADS, T, DN_DK), f32, blk, im), ((T, W), f32, (tm, W), lambda i: (i, 0))],
                      [((1, DN_DK), f32)])


def _swa_tile(T, big):
    return _pick(T, (2 * WINDOW, WINDOW)) if big else WINDOW


@functools.partial(jax.custom_vjp, nondiff_argnums=(1,))
def _heads_split(x, nh):
    return jnp.stack([x[:, SWA_DH * h:SWA_DH * (h + 1)] for h in range(nh)], axis=0)


@jax.custom_vjp
def _heads_merge(y):
    return jnp.concatenate([y[h] for h in range(y.shape[0])], axis=1)


_heads_split.defvjp(lambda x, nh: (_heads_split(x, nh), None), lambda nh, _, g: (_heads_merge(g),))
_heads_merge.defvjp(lambda y: (_heads_merge(y), None), lambda _, g: (_heads_split(g, g.shape[1] // SWA_DH),))


def _swa_rows(first, qp, kc, kp, vc, vp, snk):
    kv = [_heads_split(t, SWA_KV_HEADS) for t in (kc, kp, vc, vp)]
    return _heads_merge(_swa(first, _heads_split(qp, SWA_HEADS), *kv, snk))


def _swa_ins(p, snk, W):
    qw, kw = SWA_HEADS * SWA_DH, SWA_KV_HEADS * SWA_DH
    return [(p, (W, qw), lambda i: (i, 2048 // qw)),
            (p, (W, kw), lambda i: (i, 6144 // kw)), (p, (W, kw), lambda i: (_prev(i), 6144 // kw)),
            (p, (W, kw), lambda i: (i, 6272 // kw)), (p, (W, kw), lambda i: (_prev(i), 6272 // kw)), _full(snk)]


def _swa_fwd(name, p, snk):
    T = p.shape[0]
    W = _swa_tile(T, True)
    qw = SWA_HEADS * SWA_DH
    return _tile_call(name, lambda i, *a: _swa_rows(i == 0, *a), T // W, _swa_ins(p, snk, W),
                      [((T, qw), f32, (W, qw), lambda i: (i, 0))])[0]


def _swa_bwd(name, p, snk, do):
    T = p.shape[0]
    W = _swa_tile(T, False)
    qw, kw = SWA_HEADS * SWA_DH, SWA_KV_HEADS * SWA_DH

    def fn(i, qp, k_c, k_p, v_c, v_p, snk, do):
        _, vjp = jax.vjp(lambda *a: _swa_rows(i == 0, *a), qp, k_c, k_p, v_c, v_p, snk)
        return vjp(do)

    kv_out = ((T, kw), f32, (W, kw), lambda i: (i, 0))
    return _tile_call(name, fn, T // W, _swa_ins(p, snk, W) + [(do, (W, qw), lambda i: (i, 0))],
                      [((T, qw), bf16, (W, qw), lambda i: (i, 0)), kv_out, kv_out, kv_out, kv_out], [(tuple(snk.shape), f32)])


def _xa_fwd(name, p, kv):
    T = p.shape[0]
    tm = _row_tile(T, 256)
    W = XA_HEADS * XA_DH
    return _tile_call(name, lambda i, q, kv: _xa(q, kv), T // tm, [(p, (tm, W), lambda i: (i, 2560 // W)), _full(kv)],
                      [((T, W), f32, (tm, W), lambda i: (i, 0))])[0]


def _xa_bwd(name, p, kv, do):
    T = p.shape[0]
    tm = _row_tile(T, 256)
    W = XA_HEADS * XA_DH

    def fn(i, q, kv, do):
        _, vjp = jax.vjp(_xa, q, kv)
        return vjp(do)

    return _tile_call(name, fn, T // tm, [(p, (tm, W), lambda i: (i, 2560 // W)), _full(kv), _rows(do, tm)],
                      [((T, W), bf16, (tm, W), lambda i: (i, 0))], [(tuple(kv.shape), f32)])


def _merge_fwd(name, o_dn, o_sw, o_xa, p, wb):
    T = p.shape[0]
    tm = _row_tile(T, 256)
    GW = N_BRANCH * D_MODEL
    return _tile_call(name, lambda i, a, b, c, g, w: _merge(a, b, c, g, w.astype(f32)), T // tm,
                      [_rows(o_dn, tm), _rows(o_sw, tm), _rows(o_xa, tm), (p, (tm, GW), lambda i: (i, 1)), _full(wb)],
                      [((T, D_MODEL), bf16, (tm, D_MODEL), lambda i: (i, 0))])[0]


def _merge_bwd(name, o_dn, o_sw, o_xa, p, wb, dm):
    T = p.shape[0]
    tm = _row_tile(T, 512)
    GW = N_BRANCH * D_MODEL

    def fn(i, a, b, c, g, w, dm):
        _, vjp = jax.vjp(_merge, a, b, c, g, w.astype(f32))
        return vjp(dm)

    bo = ((T, BRANCH_W), f32, (tm, BRANCH_W), lambda i: (i, 0))
    return _tile_call(name, fn, T // tm,
                      [_rows(o_dn, tm), _rows(o_sw, tm), _rows(o_xa, tm), (p, (tm, GW), lambda i: (i, 1)), _full(wb), _rows(dm, tm)],
                      [bo, bo, bo, ((T, GW), bf16, (tm, GW), lambda i: (i, 0))], [(tuple(wb.shape), f32)])


def _assemble_dp(name, dxc, dxp, dz, dswq, dxaq, dgates, dkc, dkp, dvc, dvp, dba):
    T = dz.shape[0]
    tm = _row_tile(T, 256)
    n = T // tm
    r = tm // WINDOW
    nb = T // WINDOW

    def fn(i, dxc, dxp, dz, dswq, dxaq, dgates, dkc, dvc, dba, *nxt):
        halo = jnp.where(i == n - 1, 0.0, dxp)
        dqkv = dxc + jnp.concatenate([jnp.zeros((tm - HALO, dxc.shape[1]), f32), halo], axis=0)
        shifted = [jnp.concatenate([jnp.where(i * r + 1 + b <= nb - 1, blk, 0.0) for b, blk in enumerate(half)], axis=0)
                   for half in (nxt[:r], nxt[r:])]
        parts = [dqkv, dz, dswq, dxaq, dgates, dkc + shifted[0], dvc + shifted[1], dba, jnp.zeros((tm, LANES), f32)]
        return jnp.concatenate(parts, axis=1)

    ins = [_rows(dxc, tm), (dxp, (HALO, dxp.shape[1]), lambda i: (jnp.minimum(i + 1, n - 1), 0))]
    ins += [_rows(t, tm) for t in (dz, dswq, dxaq, dgates, dkc, dvc, dba)]
    for t in (dkp, dvp):
        ins += [(t, (WINDOW, t.shape[1]), lambda i, b=b: (jnp.minimum(i * r + 1 + b, nb - 1), 0)) for b in range(r)]
    return _tile_call(name, fn, n, ins, [((T, D_INP), bf16, (tm, D_INP), lambda i: (i, 0))])[0]


def _loss_grad(name, y, tgt):
    T = y.shape[0]
    tm = _row_tile(T, 512)

    def fn(i, y, t):
        e = y - t
        part = 0.5 * jnp.sum(jnp.mean(e * e, axis=-1, keepdims=True), axis=0, keepdims=True)
        return e * (1.0 / D_MODEL), jnp.broadcast_to(part, (1, LANES))

    return _tile_call(name, fn, T // tm, [_rows(y, tm), _rows(tgt, tm)],
                      [((T, D_MODEL), f32, (tm, D_MODEL), lambda i: (i, 0))], [((1, LANES), f32)])


def _mem_ln_fwd(name, mem, g, b):
    M = mem.shape[0]
    return _tile_call(name, lambda i, m, g, b: (_ln(m, g, b),), 1, [_full(mem), _full(g), _full(b)],
                      [((M, D_MODEL), bf16, (M, D_MODEL), lambda i: (0, 0))])[0]


def _mem_ln_bwd(name, mem, g, b, dmn):
    def fn(i, m, g, b, d):
        _, vjp = jax.vjp(lambda g, b: _ln(m, g, b), g, b)
        return vjp(d)

    return _tile_call(name, fn, 1, [_full(mem), _full(g), _full(b), _full(dmn)], [], [((1, D_MODEL), f32), ((1, D_MODEL), f32)])


def _pad_w_in(w):
    z = jnp.zeros((w.shape[0], D_INP - D_IN), w.dtype)
    return jnp.concatenate([w[:, 0:1536], w[:, 1544:2056], w[:, 2056:2568], w[:, 2824:3336], w[:, 3336:6408],
                            w[:, 2568:2696], w[:, 2696:2824], w[:, 1536:1544], z], axis=1)


def _unpad_dw_in(d):
    return jnp.concatenate([d[:, 0:1536], d[:, 6400:6408], d[:, 1536:2048], d[:, 2048:2560], d[:, 6144:6272],
                            d[:, 6272:6400], d[:, 2560:3072], d[:, 3072:6144]], axis=1)


def _lane_row(v, rows=8):
    out = jnp.zeros((rows, LANES), f32)
    return out.at[0, :v.shape[0]].set(v)


def _layer_fwd(l, x, xb, mem_nb, W):
    n = lambda s: f"l{l}_{s}"
    sv = {}
    u1, a1 = _gu_act(n("ffn1_gu"), xb, W['ffn1_w_gu_p'])
    f1, h1, h1b = _mm_res_ln(n("ffn1_down"), a1, W['ffn1_w_down'], x, W['ln_g'][0:1], W['ln_b'][0:1], 0.5)
    p = _mm(n("w_in"), h1b, W['w_inp'], 'nn', out_dtype=bf16)
    pba = _mm(n("w_in_ba"), h1b, W['w_ba'], 'nn')
    qd, qk, u, w, kt, egl = _dn_front_fwd(n("dn_front"), p, pba, W['conv_w'], W['hp'])
    o_raw, s_all = _scan_fwd(n("dn_scan"), qd, qk, u, w, kt, egl)
    o_dn = _dn_post_fwd(n("dn_post"), o_raw, p, W['norm_w'])
    o_sw = _swa_fwd(n("swa"), p, W['snk'])
    kv = _mm(n("mem_kv"), mem_nb, W['w_mem_kv'], 'nn')
    o_xa = _xa_fwd(n("xa"), p, kv)
    merged = _merge_fwd(n("merge"), o_dn, o_sw, o_xa, p, W['w_branch'])
    mo, h2, h2b = _mm_res_ln(n("w_out"), merged, W['w_out'], h1, W['ln_g'][1:2], W['ln_b'][1:2], 1.0)
    u2, a2 = _gu_act(n("ffn2_gu"), h2b, W['ffn2_w_gu_p'])
    f2, h3, h3b = _mm_res_ln(n("ffn2_down"), a2, W['ffn2_w_down'], h2, W['ln_g'][2:3], W['ln_b'][2:3], 0.5)
    sv = dict(x=x, xb=xb, u1=u1, a1=a1, f1=f1, h1=h1, h1b=h1b, p=p, pba=pba, wy=(qd, qk, u, w, kt, egl),
              o_raw=o_raw, s_all=s_all, o_dn=o_dn, o_sw=o_sw, kv=kv, o_xa=o_xa, merged=merged,
              mo=mo, h2=h2, h2b=h2b, u2=u2, a2=a2, f2=f2)
    return h3, h3b, sv


def _ffn_bwd(n, tag, x, xb, u, a, f, g, b, w_gu_t, w_down_t, dh):
    dx_a, df, dg, db = _res_ln_bwd(n(f"{tag}_ln_bwd"), x, f, g, b, 0.5, dh)
    d_down = _mm(n(f"{tag}_down_dw"), a, df, 'tn')
    du = _down_dx_act_bwd(n(f"{tag}_down_dx"), df, w_down_t, u)
    d_gu = _mm(n(f"{tag}_gu_dw"), xb, du, 'tn', out_cols=(FF_BLK, lambda j: (j % 2) * (D_FF // FF_BLK) + j // 2))
    dx = _mm(n(f"{tag}_gu_dx"), du, w_gu_t, 'nn', add=dx_a)
    return dx, d_gu, d_down, dg, db


def _layer_bwd(l, sv, mem_nb, W, dh3, dmem_acc):
    n = lambda s: f"l{l}_{s}"
    G = {}
    dh2, G['ffn2_w_gu'], G['ffn2_w_down'], dg2, db2 = _ffn_bwd(
        n, "ffn2", sv['h2'], sv['h2b'], sv['u2'], sv['a2'], sv['f2'], W['ln_g'][2:3], W['ln_b'][2:3], W['ffn2_w_gu_t'], W['ffn2_w_down_t'], dh3)
    dh1_a, dmo, dg1, db1 = _res_ln_bwd(n("ln1_bwd"), sv['h1'], sv['mo'], W['ln_g'][1:2], W['ln_b'][1:2], 1.0, dh2)
    G['w_out'] = _mm(n("w_out_dw"), sv['merged'], dmo, 'tn')
    dmerged = _mm(n("w_out_dx"), dmo, W['w_out_t'], 'nn')
    p = sv['p']
    do_dn, do_sw, do_xa, dgates, G['w_branch'] = _merge_bwd(n("merge_bwd"), sv['o_dn'], sv['o_sw'], sv['o_xa'], p, W['w_branch'], dmerged)
    dxaq, dkv = _xa_bwd(n("xa_bwd"), p, sv['kv'], do_xa)
    dkv = dkv.astype(bf16)
    G['w_mem_kv'] = _mm(n("mem_kv_dw"), mem_nb, dkv, 'tn')
    dmem_n = _mm(n("mem_kv_dx"), dkv, W['w_mem_kv_t'], 'nn', add=dmem_acc)
    dswq, dkc, dkp, dvc, dvp, dsnk = _swa_bwd(n("swa_bwd"), p, W['snk'], do_sw)
    do_raw, dz, dnw = _dn_post_bwd(n("dn_post_bwd"), sv['o_raw'], p, W['norm_w'], do_dn)
    cts = _scan_bwd(n("dn_scan_bwd"), *sv['wy'], sv['s_all'], do_raw)
    dxc, dxp, dba, dcw, dhp = _dn_front_bwd(n("dn_front_bwd"), p, sv['pba'], W['conv_w'], W['hp'], cts)
    dp = _assemble_dp(n("dp"), dxc, dxp, dz, dswq, dxaq, dgates, dkc, dkp, dvc, dvp, dba)
    G['w_in'] = _unpad_dw_in(_mm(n("w_in_dw"), sv['h1b'], dp, 'tn'))
    dh1 = _mm(n("w_in_dx"), dp, W['w_inp_t'], 'nn', add=dh1_a)
    dx, G['ffn1_w_gu'], G['ffn1_w_down'], dg0, db0 = _ffn_bwd(
        n, "ffn1", sv['x'], sv['xb'], sv['u1'], sv['a1'], sv['f1'], W['ln_g'][0:1], W['ln_b'][0:1], W['ffn1_w_gu_t'], W['ffn1_w_down_t'], dh1)
    G['ln_g'] = jnp.concatenate([dg0, dg1, dg2], axis=0)
    G['ln_b'] = jnp.concatenate([db0, db1, db2], axis=0)
    G['dn_conv_w'] = dcw
    G['dn_a_log'] = dhp[0, :DN_HEADS]
    G['dn_dt_bias'] = dhp[1, :DN_HEADS]
    G['dn_norm_w'] = dnw[0]
    G['swa_sinks'] = dsnk[:, 0]
    return dx, dmem_n, G


def _local_step(x, mem, tgt, Wf):
    mem_g, mem_b = Wf['mem_ln_g'][None, :], Wf['mem_ln_b'][None, :]
    mem_nb = _mem_ln_fwd("mem_ln", mem, mem_g, mem_b)

    def paired(k, l):
        return Wf[k + '_p'][l] if k + '_p' in Wf else _pair_gate_up(Wf[k][l])

    layers = []
    for l in range(DEPTH):
        layers.append(dict(
            ln_g=Wf['ln_g'][l], ln_b=Wf['ln_b'][l], ffn1_w_gu_p=paired('ffn1_w_gu', l), ffn1_w_down=Wf['ffn1_w_down'][l],
            w_inp=_pad_w_in(Wf['w_in'][l]), conv_w=Wf['dn_conv_w'][l],
            hp=jnp.zeros((8, LANES), f32).at[0, :DN_HEADS].set(Wf['dn_a_log'][l]).at[1, :DN_HEADS].set(Wf['dn_dt_bias'][l]),
            norm_w=Wf['dn_norm_w'][l][None, :], snk=jnp.broadcast_to(Wf['swa_sinks'][l][:, None], (SWA_HEADS, LANES)),
            w_mem_kv=Wf['w_mem_kv'][l], w_branch=Wf['w_branch'][l], w_out=Wf['w_out'][l],
            ffn2_w_gu_p=paired('ffn2_w_gu', l), ffn2_w_down=Wf['ffn2_w_down'][l]))
        layers[l]['w_ba'] = layers[l]['w_inp'][:, 6400:6400 + LANES]
        for k in ('ffn1_w_gu_p', 'ffn1_w_down', 'w_inp', 'w_mem_kv', 'w_out', 'ffn2_w_gu_p', 'ffn2_w_down'):
            layers[l][k.replace('_p', '') + '_t'] = layers[l][k].T
    h, hb = x, x.astype(bf16)
    saved = []
    for l in range(DEPTH):
        h, hb, sv = _layer_fwd(l, h, hb, mem_nb, layers[l])
        saved.append(sv)
    dh, loss_row = _loss_grad("loss", h, tgt)
    grads = [None] * DEPTH
    dmem_n = None
    for l in reversed(range(DEPTH)):
        dh, dmem_n, grads[l] = _layer_bwd(l, saved[l], mem_nb, layers[l], dh, dmem_n)
    dmg, dmb = _mem_ln_bwd("mem_ln_bwd", mem, mem_g, mem_b, dmem_n)
    G = {k: jnp.stack([grads[l][k] for l in range(DEPTH)], axis=0) for k in grads[0]}
    G['mem_ln_g'], G['mem_ln_b'] = dmg[0], dmb[0]
    return loss_row, dh, G


SEG_ALIGN = 2048


def _round_up(n, m):
    return (n + m - 1) // m * m


def _layout(names, sizes, mult):
    off, table = 0, {}
    for nm in names:
        table[nm] = (off, sizes[nm])
        off += _round_up(sizes[nm], SEG_ALIGN)
    return table, _round_up(off // LANES, mult)


def _pack(table, rows, flat):
    names = list(table)
    lead = flat[names[0]].shape[:-1]
    parts, pos = [], 0
    for nm in names:
        off, size = table[nm]
        if off > pos:
            parts.append(jnp.zeros(lead + (off - pos,), flat[nm].dtype))
        parts.append(flat[nm])
        pos = off + size
    total = rows * LANES
    if total > pos:
        parts.append(jnp.zeros(lead + (total - pos,), parts[-1].dtype))
    return jnp.concatenate(parts, axis=-1).reshape(lead + (rows, LANES))


def _unpack(table, packed, nm):
    off, size = table[nm]
    flat = packed.reshape(packed.shape[:-2] + (-1,))
    return flat[..., off:off + size]


def _split_shards(full, ax):
    shp = full.shape
    t = full.reshape(shp[:ax] + (N_SHARD, shp[ax] // N_SHARD) + shp[ax + 1:])
    return jnp.moveaxis(t, ax, 0)


def _join_shards(sh4, ax):
    return jnp.concatenate([sh4[s] for s in range(N_SHARD)], axis=ax)


ANY = pl.BlockSpec(memory_space=pl.ANY)


def _me():
    return lax.axis_index("x"), lax.axis_index("y"), lax.axis_index("c")


def _comm_call(name, body, arrays, out_shapes, n_sem):
    n = len(arrays)
    scratch = [pltpu.SemaphoreType.DMA((n, n_sem)), pltpu.SemaphoreType.DMA((n, n_sem))]
    return pl.pallas_call(
        body, name=name, out_shape=out_shapes, in_specs=[ANY] * n, out_specs=[ANY] * n, scratch_shapes=scratch,
        compiler_params=pltpu.CompilerParams(has_side_effects=True),
    )(*arrays)


def _all_gather(name, xs):
    n = len(xs)

    def body(*refs):
        x_refs, out_refs = refs[:n], refs[n:2 * n]
        send_sems, recv_sems = refs[2 * n:]
        mx, my, mc = _me()
        chips = [(1 - mx, my), (mx, 1 - my), (1 - mx, 1 - my)]

        def copy(a, k, shard, half, to, src=None):
            dst = out_refs[a].at[shard, half]
            return pltpu.make_async_remote_copy(src_ref=dst if src is None else src, dst_ref=dst, send_sem=send_sems.at[a, k],
                                                recv_sem=recv_sems.at[a, k], device_id=to, device_id_type=MESH)

        first = [copy(a, j, 2 * mx + my, mc, (cx, cy, mc), src=x_refs[a].at[mc]) for j, (cx, cy) in enumerate(chips) for a in range(n)]
        for cp in first:
            cp.start()
        passed = []
        for j, (cx, cy) in enumerate(chips):
            for a in range(n):
                copy(a, j, 2 * cx + cy, mc, (mx, my, mc)).wait_recv()
                passed.append(copy(a, 3 + j, 2 * cx + cy, mc, (mx, my, 1 - mc)))
                passed[-1].start()
        for j, (cx, cy) in enumerate(chips):
            for a in range(n):
                copy(a, 3 + j, 2 * cx + cy, 1 - mc, (mx, my, mc)).wait_recv()
        for cp in first + passed:
            cp.wait_send()

    return _comm_call(name, body, xs, [jax.ShapeDtypeStruct((N_SHARD,) + x.shape, x.dtype) for x in xs], 6)


def _pair_exchange(name, items):
    n = len(items)

    def body(*refs):
        src_refs, dst_refs = refs[:n], refs[n:2 * n]
        send_sems, recv_sems = refs[2 * n:]
        mx, my, mc = _me()
        cps = [pltpu.make_async_remote_copy(src_ref=src_refs[a].at[s, 1 - mc], dst_ref=dst_refs[a].at[s], send_sem=send_sems.at[a, s],
                                            recv_sem=recv_sems.at[a, s], device_id=(mx, my, 1 - mc), device_id_type=MESH)
               for a in range(n) for s in range(N_SHARD)]
        for cp in cps:
            cp.start()
        for cp in cps:
            cp.wait()

    return _comm_call(name, body, items, [jax.ShapeDtypeStruct((N_SHARD,) + t.shape[2:], t.dtype) for t in items], N_SHARD)


def _chip_exchange(name, parts):
    n = len(parts)

    def body(*refs):
        p_refs, dst_refs = refs[:n], refs[n:2 * n]
        send_sems, recv_sems = refs[2 * n:]
        mx, my, mc = _me()
        chips = [(1 - mx, my), (mx, 1 - my), (1 - mx, 1 - my)]
        cps = [pltpu.make_async_remote_copy(src_ref=p_refs[a].at[2 * cx + cy], dst_ref=dst_refs[a].at[j], send_sem=send_sems.at[a, j],
                                            recv_sem=recv_sems.at[a, j], device_id=(cx, cy, mc), device_id_type=MESH)
               for a in range(n) for j, (cx, cy) in enumerate(chips)]
        for cp in cps:
            cp.start()
        for cp in cps:
            cp.wait()

    return _comm_call(name, body, parts, [jax.ShapeDtypeStruct((3,) + t.shape[1:], t.dtype) for t in parts], 3)


def _pair_swap(name, reds):
    n = len(reds)

    def body(*refs):
        r_refs, out_refs = refs[:n], refs[n:2 * n]
        send_sems, recv_sems = refs[2 * n:]
        mx, my, mc = _me()
        cps = [pltpu.make_async_remote_copy(src_ref=r_refs[a], dst_ref=out_refs[a], send_sem=send_sems.at[a, 0],
                                            recv_sem=recv_sems.at[a, 0], device_id=(mx, my, 1 - mc), device_id_type=MESH)
               for a in range(n)]
        for cp in cps:
            cp.start()
        for cp in cps:
            cp.wait()

    return _comm_call(name, body, reds, [jax.ShapeDtypeStruct(t.shape, t.dtype) for t in reds], 1)


EW_BLOCK_BYTES = 1 << 20


def _ew_call(name, fn, ins, n_out, out_dtype=f32):
    shape = ins[0].shape
    last = shape[-1]
    flat = [a.reshape(-1, last) for a in ins]
    R = flat[0].shape[0]
    cands = tuple(c for c in (4096, 2048, 1024, 512, 256, 128, 64, 32, 16, 8) if c * last * 4 <= EW_BLOCK_BYTES)
    tr = _pick(R, cands)
    res = _tile_call(name, lambda i, *a: fn(*a), R // tr, [(a, (tr, last), lambda i: (i, 0)) for a in flat],
                     [((R, last), out_dtype, (tr, last), lambda i: (i, 0))] * n_out)
    return [r.reshape(shape) for r in res]


def _adamw(g, w, m, v):
    m = B1 * m + (1.0 - B1) * g
    v = B2 * v + (1.0 - B2) * jnp.square(g)
    m_hat = m / (1.0 - B1 ** STEP)
    v_hat = v / (1.0 - B2 ** STEP)
    return -LR * (m_hat / (jnp.sqrt(v_hat) + EPS) + WD * w), m, v


def _adamw_call(name, mine, theirs, w, m, v, mc1):
    shape, last = w.shape, w.shape[-1]
    g2 = [t.reshape(-1, last) for t in (mine, theirs)]
    w3 = [t.reshape(2, -1, last) for t in (w, m, v)]
    R = g2[0].shape[0]
    tr = _pick(R, tuple(c for c in (4096, 2048, 1024, 512, 256, 128, 64, 32, 16, 8) if c * last * 4 <= EW_BLOCK_BYTES))

    def body(mc_ref, mine_ref, theirs_ref, w_ref, m_ref, v_ref, g_out, d_out, m_out, v_out):
        g = jnp.where(pl.program_id(0) == mc_ref[0], mine_ref[...], theirs_ref[...])
        d, nm, nv = _adamw(g, w_ref[0], m_ref[0], v_ref[0])
        g_out[0], d_out[0], m_out[0], v_out[0] = g, d, nm, nv

    half = pl.BlockSpec((tr, last), lambda h, i: (i, 0))
    full = pl.BlockSpec((1, tr, last), lambda h, i: (h, i, 0))
    res = pl.pallas_call(
        body, name=name, grid=(2, R // tr),
        in_specs=[pl.BlockSpec(memory_space=pltpu.SMEM), half, half, full, full, full], out_specs=[full] * 4,
        out_shape=[jax.ShapeDtypeStruct((2, R, last), f32)] * 4,
        compiler_params=pltpu.CompilerParams(dimension_semantics=("arbitrary", "arbitrary"), vmem_limit_bytes=VMEM_LIMIT),
    )(mc1, *g2, *w3)
    return tuple(r.reshape(shape) for r in res)


def kernel(x, mem, mem_ln_g, mem_ln_b, ln_g, ln_b, ffn1_w_gu, ffn1_w_down, w_in, dn_conv_w, dn_a_log, dn_dt_bias, dn_norm_w, swa_sinks, w_mem_kv, w_branch, w_out, ffn2_w_gu, ffn2_w_down, loss_target, m_mem_ln_g, m_mem_ln_b, m_ln_g, m_ln_b, m_ffn1_w_gu, m_ffn1_w_down, m_w_in, m_dn_conv_w, m_dn_a_log, m_dn_dt_bias, m_dn_norm_w, m_swa_sinks, m_w_mem_kv, m_w_branch, m_w_out, m_ffn2_w_gu, m_ffn2_w_down, v_mem_ln_g, v_mem_ln_b, v_ln_g, v_ln_b, v_ffn1_w_gu, v_ffn1_w_down, v_w_in, v_dn_conv_w, v_dn_a_log, v_dn_dt_bias, v_dn_norm_w, v_swa_sinks, v_w_mem_kv, v_w_branch, v_w_out, v_ffn2_w_gu, v_ffn2_w_down):
    args = dict(locals())
    Ws = {n: args[n] for n in WEIGHTS}
    Ms = {n: args["m_" + n] for n in WEIGHTS}
    Vs = {n: args["v_" + n] for n in WEIGHTS}
    mc = lax.axis_index("c")
    my_s = 2 * lax.axis_index("x") + lax.axis_index("y")
    small = [n for n in WEIGHTS if n not in MATRICES]

    ag_table, ag_rows = _layout(SMALL_SHARDED, {n: Ws[n].size for n in SMALL_SHARDED}, 16)
    ag_small = _pack(ag_table, ag_rows, {n: Ws[n].reshape(-1) for n in SMALL_SHARDED}).reshape(2, ag_rows // 2, LANES)
    local = [Ws[n].astype(bf16) for n in MATRICES] + [ag_small]
    gathered = _all_gather("all_gather_w", local)
    gathered = [jnp.stack([jnp.where(my_s == s, loc, g[s]) for s in range(N_SHARD)], axis=0) for loc, g in zip(local, gathered)]
    Wf = {n: _join_shards(g, SHARD_AXIS[n]) for n, g in zip(MATRICES, gathered) if not n.endswith('w_gu')}
    for n, g in zip(MATRICES, gathered):
        if n.endswith('w_gu'):
            assert g.shape[-1] == FF_BLK
            Wf[n + '_p'] = jnp.concatenate([g[0], g[2], g[1], g[3]], axis=2)
    g_small = gathered[-1].reshape(N_SHARD, ag_rows, LANES)
    for n in SMALL_SHARDED:
        Wf[n] = _join_shards(_unpack(ag_table, g_small, n).reshape((N_SHARD,) + Ws[n].shape), SHARD_AXIS[n])
    for n in WEIGHTS:
        if SHARD_AXIS[n] is None:
            Wf[n] = Ws[n]

    loss_row, dx, G = _local_step(x[0], mem[0], loss_target[0], Wf)

    table, rows = _layout(small + ['loss'], {**{n: Ws[n].size for n in small}, 'loss': 1}, 16)
    gflat = {n: (jnp.broadcast_to(G[n].reshape(1, -1), (N_SHARD, G[n].size)) if SHARD_AXIS[n] is None
                 else _split_shards(G[n], SHARD_AXIS[n]).reshape(N_SHARD, -1)) for n in small}
    gflat['loss'] = jnp.broadcast_to(loss_row[:, :1], (N_SHARD, 1))
    items = [_split_shards(G[n], SHARD_AXIS[n]) for n in MATRICES] + [_pack(table, rows, gflat).reshape(N_SHARD, 2, rows // 2, LANES)]
    tags = MATRICES + ['small']
    got = _pair_exchange("rs_pair", items)
    keep = [lax.dynamic_index_in_dim(a, mc, axis=1, keepdims=False) for a in items]
    wire = [bf16] * len(MATRICES) + [f32]
    part = [_ew_call(f"rs_add_pair_{t}", lambda a, b: a + b, [k, g], 1, out_dtype=dt)[0] for t, k, g, dt in zip(tags, keep, got, wire)]
    others = _chip_exchange("rs_chips", part)
    own = lambda a: lax.dynamic_index_in_dim(a, my_s, axis=0, keepdims=False)
    red = [_ew_call(f"rs_add_chips_{t}", lambda k, g, fx, fy, fxy: ((k + g) + fy) + (fx + fxy), [own(k), own(g), o[0], o[1], o[2]], 1)[0]
           for t, k, g, o in zip(tags, keep, got, others)]
    theirs = _pair_swap("rs_swap", red)

    mc1 = mc.astype(i32).reshape(1)
    outs = {}
    for n, a, b in zip(MATRICES, red, theirs):
        outs[n] = _adamw_call(f"adamw_{n}", a, b, Ws[n], Ms[n], Vs[n], mc1)
    fill = {'loss': jnp.zeros((1,), f32)}
    packs = [_pack(table, rows, {**{n: src[n].reshape(-1) for n in small}, **fill}).reshape(2, rows // 2, LANES) for src in (Ws, Ms, Vs)]
    small_out = [p.reshape(rows, LANES) for p in _adamw_call("adamw_small", red[-1], theirs[-1], *packs, mc1)]
    for n in small:
        outs[n] = tuple(_unpack(table, p, n).reshape(Ws[n].shape) for p in small_out)
    loss = _unpack(table, small_out[0], 'loss').reshape(())
    return (loss, dx[None], *[outs[n][k] for k in range(4) for n in WEIGHTS])
```

```python
import functools

import jax
import jax.numpy as jnp
from jax import lax
from jax.experimental import pallas as pl
from jax.experimental.pallas import tpu as pltpu

f32, bf16, i32 = jnp.float32, jnp.bfloat16, jnp.int32
HI = lax.Precision.HIGHEST
MESH = pl.DeviceIdType.MESH

D_MODEL = 1024
DEPTH = 2
DN_HEADS, DN_DK, DN_CONV, DN_CHUNK = 4, 128, 4, 64
SWA_HEADS, SWA_KV_HEADS, SWA_DH, WINDOW = 8, 2, 64, 128
XA_HEADS, XA_DH = 4, 128
D_FF = 2816
N_BRANCH, BRANCH_W = 3, 512
ALPHA = (2 * DEPTH) ** 0.25
LN_EPS, RMS_EPS, NEG_INF = 1e-5, 1e-6, -1e30
D_IN = 6408
D_INP = 6656
LR, B1, B2, EPS, WD, STEP = 0.001, 0.9, 0.999, 1e-08, 0.01, 10

LANES = 128
VMEM_LIMIT = 56 << 20
N_SHARD = 4
SCAN_CHUNKS = 4
HALO = 16
MM_TILES = {
    'ffn1_gu_dx': (512, 1024, 5632), 'ffn2_gu_dx': (512, 1024, 5632), 'w_in_dx': (256, 1024, 6656), 'w_in': (2048, 512, 1024),
    'w_in_dw': (1024, 512, 4096), 'ffn1_down_dw': (1408, 512, 4096), 'ffn2_down_dw': (1408, 512, 4096),
    'ffn1_down_dx': (1024,), 'ffn2_down_dx': (1024,), 'ffn1_down': (1024,), 'ffn2_down': (1024,), 'w_out': (1024,),
    'ffn1_gu': (1024,), 'ffn2_gu': (1024,),
    'l1_w_in_dx': (512, 1024, 6656), 'l1_w_out_dx': (512, 1024, 1024),
}


def _tiles(name, default=None):
    return MM_TILES.get(name, MM_TILES.get(name.split('_', 1)[-1], default))

WEIGHTS = ['mem_ln_g', 'mem_ln_b', 'ln_g', 'ln_b', 'ffn1_w_gu', 'ffn1_w_down', 'w_in', 'dn_conv_w', 'dn_a_log',
           'dn_dt_bias', 'dn_norm_w', 'swa_sinks', 'w_mem_kv', 'w_branch', 'w_out', 'ffn2_w_gu', 'ffn2_w_down']
SHARD_AXIS = {'mem_ln_g': None, 'mem_ln_b': None, 'ln_g': 2, 'ln_b': 2, 'ffn1_w_gu': 2, 'ffn1_w_down': 1, 'w_in': 2,
              'dn_conv_w': 2, 'dn_a_log': None, 'dn_dt_bias': None, 'dn_norm_w': None, 'swa_sinks': None,
              'w_mem_kv': 1, 'w_branch': 3, 'w_out': 1, 'ffn2_w_gu': 2, 'ffn2_w_down': 1}
MATRICES = ['ffn1_w_gu', 'ffn1_w_down', 'w_in', 'w_mem_kv', 'w_branch', 'w_out', 'ffn2_w_gu', 'ffn2_w_down']
SMALL_SHARDED = ['ln_g', 'ln_b', 'dn_conv_w']


def _dg(a, b, mode, hi):
    nb = a.ndim - 2
    bd = tuple(range(nb))
    ca = nb if mode == 'tn' else nb + 1
    cb = nb + 1 if mode == 'nt' else nb
    dims = (((ca,), (cb,)), (bd, bd))
    dot = lambda x, y: lax.dot_general(x, y, dims, preferred_element_type=f32)
    a_hi, b_hi = a.astype(bf16), b.astype(bf16)
    if not hi:
        return dot(a_hi, b_hi)
    a_lo = (a.astype(f32) - a_hi.astype(f32)).astype(bf16)
    b_lo = (b.astype(f32) - b_hi.astype(f32)).astype(bf16)
    return dot(a_hi, b_hi) + (dot(a_hi, b_lo) + dot(a_lo, b_hi))


@functools.partial(jax.custom_vjp, nondiff_argnums=(2, 3))
def _dot(a, b, mode, hi):
    return _dg(a, b, mode, hi)


def _dot_fwd(a, b, mode, hi):
    return _dg(a, b, mode, hi), (a, b)


def _dot_bwd(mode, hi, res, g):
    a, b = res
    if mode == 'nn':
        da, db = _dg(g, b, 'nt', hi), _dg(a, g, 'tn', hi)
    elif mode == 'nt':
        da, db = _dg(g, b, 'nn', hi), _dg(g, a, 'tn', hi)
    else:
        da, db = _dg(b, g, 'nt', hi), _dg(a, g, 'nn', hi)
    return da.astype(a.dtype), db.astype(b.dtype)


_dot.defvjp(_dot_fwd, _dot_bwd)


def bdot(a, b, mode):
    return _dot(a, b, mode, False)


def hdot(a, b, mode):
    return _dot(a, b, mode, True)


def _shift_rows_impl(x, tail, s):
    r = pltpu.roll(x, s, 0)
    rows = lax.broadcasted_iota(i32, tail.shape, 0)
    top = jnp.where(rows >= s, r[:HALO], pltpu.roll(tail, s, 0))
    return jnp.concatenate([top, r[HALO:]], axis=0)


@functools.partial(jax.custom_vjp, nondiff_argnums=(2,))
def _shift_rows(x, tail, s):
    return _shift_rows_impl(x, tail, s)


def _shift_rows_fwd(x, tail, s):
    return _shift_rows_impl(x, tail, s), None


def _shift_rows_bwd(s, _, g):
    n = g.shape[0]
    r = pltpu.roll(g, n - s, 0)
    rows = lax.broadcasted_iota(i32, (HALO, g.shape[1]), 0)
    last = r[n - HALO:]
    keep = rows < HALO - s
    dx = jnp.concatenate([r[:n - HALO], jnp.where(keep, last, 0.0)], axis=0)
    return dx, jnp.where(keep, 0.0, pltpu.roll(g[:HALO], HALO - s, 0))


_shift_rows.defvjp(_shift_rows_fwd, _shift_rows_bwd)


def _lane_pick(x, lane):
    idx = lax.broadcasted_iota(i32, x.shape, x.ndim - 1)
    return jnp.sum(jnp.where(idx == lane, x, 0.0), axis=-1, keepdims=True)


def _silu(x):
    return x * jax.nn.sigmoid(x)


def _tri_inv(a):
    C = a.shape[-1]
    eye = (lax.broadcasted_iota(i32, a.shape, 1) == lax.broadcasted_iota(i32, a.shape, 2)).astype(f32)
    p = -a
    x = eye + p
    for _ in range((C - 1).bit_length() - 1):
        p = _dg(p, p, 'nn', True)
        x = x + _dg(x, p, 'nn', True)
    return x


@jax.custom_vjp
def _tri_solve(a, rhs):
    return _dg(_tri_inv(a), rhs, 'nn', True)


def _tri_solve_fwd(a, rhs):
    tinv = _tri_inv(a)
    sol = _dg(tinv, rhs, 'nn', True)
    return sol, (tinv, sol)


def _tri_solve_bwd(res, g):
    tinv, sol = res
    d_rhs = _dg(tinv, g, 'tn', True)
    return -_dg(d_rhs, sol, 'nt', True), d_rhs


_tri_solve.defvjp(_tri_solve_fwd, _tri_solve_bwd)


@functools.partial(jax.custom_vjp, nondiff_argnums=(1,))
def _lane_head(x, n):
    return x[:, :, :n]


def _lane_head_fwd(x, n):
    return x[:, :, :n], None


def _lane_head_bwd(n, _, g):
    s = jnp.sum(g, axis=-1, keepdims=True) * (1.0 / LANES)
    return (jnp.broadcast_to(s, g.shape[:-1] + (LANES,)),)


_lane_head.defvjp(_lane_head_fwd, _lane_head_bwd)


@functools.partial(jax.custom_vjp, nondiff_argnums=(1,))
def _last_row(x, c):
    return x[:, c - 1:, :]


def _last_row_fwd(x, c):
    return x[:, c - 1:, :], None


def _last_row_bwd(c, _, g):
    shape = (g.shape[0], c, g.shape[2])
    rows = lax.broadcasted_iota(i32, shape, 1)
    return (jnp.where(rows == c - 1, jnp.broadcast_to(g, shape), 0.0),)


_last_row.defvjp(_last_row_fwd, _last_row_bwd)


def _full(a):
    nd = a.ndim
    return (a, tuple(a.shape), lambda i, _nd=nd: (0,) * _nd)


def _rows(a, tm, col=0, width=None):
    width = a.shape[1] if width is None else width
    return (a, (tm, width), lambda i, _c=col: (i, _c))


def _tile_call(name, fn, n, ins, outs, accs=()):
    n_in, n_out, n_acc = len(ins), len(outs), len(accs)

    def body(*refs):
        i = pl.program_id(0)
        res = fn(i, *[r[...].astype(f32) for r in refs[:n_in]])
        if not isinstance(res, (tuple, list)):
            res = (res,)
        assert len(res) == n_out + n_acc, (name, len(res), n_out, n_acc)
        for r, v in zip(refs[n_in:n_in + n_out], res[:n_out]):
            r[...] = v.astype(r.dtype)
        if n_acc:
            acc_refs = refs[n_in + n_out:]

            @pl.when(i == 0)
            def _():
                for r in acc_refs:
                    r[...] = jnp.zeros(r.shape, r.dtype)

            for r, v in zip(acc_refs, res[n_out:]):
                r[...] += v.astype(r.dtype)

    out_shape = [jax.ShapeDtypeStruct(s, d) for s, d, _, _ in outs] + [jax.ShapeDtypeStruct(s, d) for s, d in accs]
    out_specs = [pl.BlockSpec(b, m) for _, _, b, m in outs]
    out_specs += [pl.BlockSpec(tuple(s), lambda i, _nd=len(s): (0,) * _nd) for s, _ in accs]
    res = pl.pallas_call(
        body, name=name, grid=(n,),
        in_specs=[pl.BlockSpec(b, m) for _, b, m in ins],
        out_specs=out_specs, out_shape=out_shape,
        compiler_params=pltpu.CompilerParams(dimension_semantics=("arbitrary",), vmem_limit_bytes=VMEM_LIMIT),
    )(*[a for a, _, _ in ins])
    return res


def _pick(n, cands):
    for c in cands:
        if n % c == 0:
            return c
    return n


def _mm(name, a, b, mode, out_dtype=f32, add=None, out_cols=None):
    if mode == 'tn':
        K, M = a.shape
    else:
        M, K = a.shape
    N = b.shape[0] if mode == 'nt' else b.shape[1]
    tm = _pick(M, (1024, 1408, 512, 256, 128))
    tn = _pick(N, (512, 1408, 256, 128)) if out_cols is None else out_cols[0]
    tuned = _tiles(name)
    if tuned is not None and (M % tuned[0] or N % tuned[1] or K % tuned[2]):
        tuned = None
    if tuned is not None:
        tm, tn = tuned[:2]
    col = (lambda j: j) if out_cols is None else out_cols[1]
    tk = K if K <= 3328 else _pick(K, (3328, 2816, 2048, 1024, 512, 256, 128))
    if tuned is not None:
        tk = tuned[2]
    nk = K // tk
    ca = 0 if mode == 'tn' else 1
    cb = 1 if mode == 'nt' else 0
    dims = (((ca,), (cb,)), ((), ()))

    def body(*refs):
        a_ref, b_ref = refs[0], refs[1]
        add_ref = refs[2] if add is not None else None
        part = lax.dot_general(a_ref[...].astype(bf16), b_ref[...].astype(bf16), dims, preferred_element_type=f32)

        def finish(r, o_ref):
            if add_ref is not None:
                r = r + add_ref[...].astype(f32)
            o_ref[...] = r.astype(o_ref.dtype)

        if nk == 1:
            finish(part, refs[-1])
            return
        o_ref, acc_ref = refs[-2], refs[-1]
        k = pl.program_id(2)

        @pl.when(k == 0)
        def _():
            acc_ref[...] = part

        @pl.when(k > 0)
        def _():
            acc_ref[...] += part

        @pl.when(k == nk - 1)
        def _():
            finish(acc_ref[...], o_ref)

    a_spec = pl.BlockSpec((tk, tm), lambda i, j, k: (k, i)) if mode == 'tn' else pl.BlockSpec((tm, tk), lambda i, j, k: (i, k))
    b_spec = pl.BlockSpec((tn, tk), lambda i, j, k: (j, k)) if mode == 'nt' else pl.BlockSpec((tk, tn), lambda i, j, k: (k, j))
    in_specs, args = [a_spec, b_spec], [a, b]
    if add is not None:
        in_specs.append(pl.BlockSpec((tm, tn), lambda i, j, k: (i, j)))
        args.append(add)
    return pl.pallas_call(
        body, name=name, grid=(M // tm, N // tn, nk), in_specs=in_specs,
        out_specs=pl.BlockSpec((tm, tn), lambda i, j, k: (i, col(j))),
        out_shape=jax.ShapeDtypeStruct((M, N), out_dtype),
        scratch_shapes=[pltpu.VMEM((tm, tn), f32)] if nk > 1 else [],
        compiler_params=pltpu.CompilerParams(dimension_semantics=("parallel", "parallel", "arbitrary"), vmem_limit_bytes=VMEM_LIMIT),
    )(*args)


def _swiglu(u):
    return _silu(u[:, :D_FF]) * u[:, D_FF:]


def _res_ln(x, f, g, b, s):
    r = ALPHA * x + s * f
    mu = jnp.mean(r, axis=-1, keepdims=True)
    rc = r - mu
    var = jnp.mean(rc * rc, axis=-1, keepdims=True)
    return rc * lax.rsqrt(var + LN_EPS) * g + b


def _ln(x, g, b):
    mu = jnp.mean(x, axis=-1, keepdims=True)
    xc = x - mu
    var = jnp.mean(xc * xc, axis=-1, keepdims=True)
    return xc * lax.rsqrt(var + LN_EPS) * g + b


def _dn_pre(first, xc, xp, ba, cw, hp):
    xp = jnp.where(first, 0.0, xp)
    y = cw[DN_CONV - 1:DN_CONV, :] * xc
    for j in range(DN_CONV - 1):
        y = y + cw[j:j + 1, :] * _shift_rows(xc, xp, DN_CONV - 1 - j)
    c = _silu(y)
    qs, ks, vs, gs, bs = [], [], [], [], []
    nqk = DN_HEADS * DN_DK
    for h in range(DN_HEADS):
        q = c[:, h * DN_DK:(h + 1) * DN_DK]
        k = c[:, nqk + h * DN_DK:nqk + (h + 1) * DN_DK]
        v = c[:, 2 * nqk + h * DN_DK:2 * nqk + (h + 1) * DN_DK]
        qs.append(q * lax.rsqrt(jnp.sum(q * q, axis=-1, keepdims=True) + RMS_EPS))
        ks.append(k * lax.rsqrt(jnp.sum(k * k, axis=-1, keepdims=True) + RMS_EPS))
        vs.append(v)
        beta = jax.nn.sigmoid(_lane_pick(ba, h))
        a_log = _lane_pick(hp[0:1, :], h)
        dt = _lane_pick(hp[1:2, :], h)
        g = -jnp.exp(a_log) * jax.nn.softplus(_lane_pick(ba, DN_HEADS + h) + dt)
        gs.append(jnp.broadcast_to(g, q.shape))
        bs.append(jnp.broadcast_to(beta, q.shape))
    return tuple(jnp.stack(t, axis=0) for t in (qs, ks, vs, gs, bs))


def _wy(q, k, v, gb, bb):
    B, C, _ = q.shape
    ri = lax.broadcasted_iota(i32, (B, C, C), 1)
    ci = lax.broadcasted_iota(i32, (B, C, C), 2)
    tril, strict = ri >= ci, ri > ci
    gc = hdot(tril.astype(f32), gb, 'nn')
    gl = jnp.broadcast_to(_last_row(gc, C), gc.shape)
    col = _lane_head(gc, C)
    decay = jnp.exp(jnp.where(tril, col - jnp.swapaxes(col, 1, 2), NEG_INF))
    qs = q * (DN_DK ** -0.5)
    kb = k * bb
    a = jnp.where(strict, bdot(kb, k, 'nt') * decay, 0.0)
    sol = _tri_solve(a, jnp.concatenate([v * bb, kb * jnp.exp(gc)], axis=-1))
    qk = jnp.where(tril, bdot(qs, k, 'nt') * decay, 0.0)
    kt = k * jnp.exp(gl - gc)
    qd = qs * jnp.exp(gc)
    return qd, qk, sol[..., :DN_DK], sol[..., DN_DK:], kt, jnp.exp(gl)


def _scan_step(S, qd, qk, u, w, kt, egl):
    vn = u - bdot(w, S, 'nn')
    o = bdot(qd, S, 'nn') + bdot(qk, vn, 'nn')
    e2 = jnp.concatenate([egl] * (DN_DK // DN_CHUNK), axis=1)
    return o, S * e2 + bdot(kt, vn, 'tn')


def _dn_post(o, z, nw):
    outs = []
    for h in range(DN_HEADS):
        oh = o[h]
        oh = oh * lax.rsqrt(jnp.mean(oh * oh, axis=-1, keepdims=True) + RMS_EPS) * nw
        outs.append(oh * _silu(z[:, h * DN_DK:(h + 1) * DN_DK]))
    return jnp.concatenate(outs, axis=1)


def _swa(first, q, kc, kp, vc, vp, snk):
    W = q.shape[1]
    G = SWA_HEADS // SWA_KV_HEADS
    r = lax.broadcasted_iota(i32, (G, W, 2 * W), 1)
    c = lax.broadcasted_iota(i32, (G, W, 2 * W), 2)
    mask = (c > W + r - WINDOW) & (c <= W + r) & jnp.logical_or(c >= W, jnp.logical_not(first))
    sink_all = _lane_pick(snk, 0)
    outs = []
    for j in range(SWA_KV_HEADS):
        qj = q[j * G:(j + 1) * G]
        kk = jnp.broadcast_to(jnp.concatenate([kp[j], kc[j]], axis=0)[None], (G, 2 * W, SWA_DH))
        vv = jnp.broadcast_to(jnp.concatenate([vp[j], vc[j]], axis=0)[None], (G, 2 * W, SWA_DH))
        s = jnp.where(mask, bdot(qj, kk, 'nt') * (SWA_DH ** -0.5), NEG_INF)
        sink = sink_all[j * G:(j + 1) * G][:, :, None]
        m = jnp.maximum(jnp.max(s, axis=-1, keepdims=True), sink)
        p = jnp.exp(s - m)
        p = p / (jnp.sum(p, axis=-1, keepdims=True) + jnp.exp(sink - m))
        outs.append(bdot(p, vv, 'nn'))
    return jnp.concatenate(outs, axis=0)


def _xa(q, kv):
    outs = []
    nk = XA_HEADS * XA_DH
    for h in range(XA_HEADS):
        qh = q[:, h * XA_DH:(h + 1) * XA_DH]
        kh = kv[:, h * XA_DH:(h + 1) * XA_DH]
        vh = kv[:, nk + h * XA_DH:nk + (h + 1) * XA_DH]
        s = bdot(qh, kh, 'nt') * (XA_DH ** -0.5)
        m = jnp.max(s, axis=-1, keepdims=True)
        p = jnp.exp(s - m)
        p = p / jnp.sum(p, axis=-1, keepdims=True)
        outs.append(bdot(p, vh, 'nn'))
    return jnp.concatenate(outs, axis=1)


def _merge(o_dn, o_sw, o_xa, gates, wb):
    acc = None
    for n, o in enumerate((o_dn, o_sw, o_xa)):
        t = jax.nn.sigmoid(gates[:, n * D_MODEL:(n + 1) * D_MODEL]) * bdot(o, wb[n], 'nn')
        acc = t if acc is None else acc + t
    return acc


def _row_tile(T, want):
    return _pick(T, tuple(c for c in (1024, 512, 256, 128, 64) if c <= want))


def _res_ln_fwd(name, x, f, g, b, s):
    T = x.shape[0]
    tm = _row_tile(T, 512)

    def fn(i, x, f, g, b):
        h = _res_ln(x, f, g, b, s)
        return h, h

    return _tile_call(name, fn, T // tm, [_rows(x, tm), _rows(f, tm), _full(g), _full(b)],
                      [((T, D_MODEL), f32, (tm, D_MODEL), lambda i: (i, 0)), ((T, D_MODEL), bf16, (tm, D_MODEL), lambda i: (i, 0))])


def _res_ln_bwd(name, x, f, g, b, s, dh):
    T = x.shape[0]
    tm = _row_tile(T, 512)

    def fn(i, x, f, g, b, dh):
        _, vjp = jax.vjp(lambda x, f, g, b: _res_ln(x, f, g, b, s), x, f, g, b)
        return vjp(dh)

    return _tile_call(name, fn, T // tm, [_rows(x, tm), _rows(f, tm), _full(g), _full(b), _rows(dh, tm)],
                      [((T, D_MODEL), f32, (tm, D_MODEL), lambda i: (i, 0)), ((T, D_MODEL), bf16, (tm, D_MODEL), lambda i: (i, 0))],
                      [((1, D_MODEL), f32), ((1, D_MODEL), f32)])


FF_BLK = D_FF // 2


def _pair_gate_up(w):
    d = w.shape[0]
    return jnp.stack([w[:, :D_FF].reshape(d, -1, FF_BLK), w[:, D_FF:].reshape(d, -1, FF_BLK)], axis=2).reshape(d, 2 * D_FF)


def _swiglu_blk(u):
    return _silu(u[:, :FF_BLK]) * u[:, FF_BLK:]


def _mm_spec(tm, tn):
    return dict(compiler_params=pltpu.CompilerParams(dimension_semantics=("parallel", "arbitrary"), vmem_limit_bytes=VMEM_LIMIT))


def _gu_act(name, xb, w_pair):
    T, D = xb.shape
    tm = _pick(T, (_tiles(name, (512,))[0], 512, 256, 128))

    def body(x_ref, w_ref, u_ref, a_ref):
        r = lax.dot_general(x_ref[...], w_ref[...], (((1,), (0,)), ((), ())), preferred_element_type=f32)
        u_ref[...] = r.astype(bf16)
        a_ref[...] = _swiglu_blk(r).astype(bf16)

    return pl.pallas_call(
        body, name=name, grid=(T // tm, D_FF // FF_BLK),
        in_specs=[pl.BlockSpec((tm, D), lambda i, j: (i, 0)), pl.BlockSpec((D, 2 * FF_BLK), lambda i, j: (0, j))],
        out_specs=[pl.BlockSpec((tm, 2 * FF_BLK), lambda i, j: (i, j)), pl.BlockSpec((tm, FF_BLK), lambda i, j: (i, j))],
        out_shape=[jax.ShapeDtypeStruct((T, 2 * D_FF), bf16), jax.ShapeDtypeStruct((T, D_FF), bf16)],
        **_mm_spec(tm, FF_BLK))(xb, w_pair)


def _down_dx_act_bwd(name, df, w_down_t, u):
    T, D = df.shape
    tm = _pick(T, (_tiles(name, (256,))[0], 256, 128))

    def body(df_ref, w_ref, u_ref, du_ref):
        da = lax.dot_general(df_ref[...], w_ref[...], (((1,), (0,)), ((), ())), preferred_element_type=f32)
        _, vjp = jax.vjp(_swiglu_blk, u_ref[...].astype(f32))
        du_ref[...] = vjp(da)[0].astype(bf16)

    return pl.pallas_call(
        body, name=name, grid=(T // tm, D_FF // FF_BLK),
        in_specs=[pl.BlockSpec((tm, D), lambda i, j: (i, 0)), pl.BlockSpec((D, FF_BLK), lambda i, j: (0, j)),
                  pl.BlockSpec((tm, 2 * FF_BLK), lambda i, j: (i, j))],
        out_specs=pl.BlockSpec((tm, 2 * FF_BLK), lambda i, j: (i, j)),
        out_shape=jax.ShapeDtypeStruct((T, 2 * D_FF), bf16),
        **_mm_spec(tm, FF_BLK))(df, w_down_t, u)


def _mm_res_ln(name, a, w, x, g, b, s):
    T, K = a.shape
    tm = _pick(T, (_tiles(name, (512,))[0], 512, 256, 128))

    def body(a_ref, w_ref, x_ref, g_ref, b_ref, f_ref, h_ref, hb_ref):
        f = lax.dot_general(a_ref[...], w_ref[...], (((1,), (0,)), ((), ())), preferred_element_type=f32)
        h = _res_ln(x_ref[...], f, g_ref[...], b_ref[...], s)
        f_ref[...] = f
        h_ref[...] = h
        hb_ref[...] = h.astype(bf16)

    row = pl.BlockSpec((tm, D_MODEL), lambda i: (i, 0))
    vec = pl.BlockSpec((1, D_MODEL), lambda i: (0, 0))
    return pl.pallas_call(
        body, name=name, grid=(T // tm,),
        in_specs=[pl.BlockSpec((tm, K), lambda i: (i, 0)), pl.BlockSpec((K, D_MODEL), lambda i: (0, 0)), row, vec, vec],
        out_specs=[row, row, row],
        out_shape=[jax.ShapeDtypeStruct((T, D_MODEL), f32), jax.ShapeDtypeStruct((T, D_MODEL), f32), jax.ShapeDtypeStruct((T, D_MODEL), bf16)],
        compiler_params=pltpu.CompilerParams(dimension_semantics=("parallel",), vmem_limit_bytes=VMEM_LIMIT))(a, w, x, g, b)


def _hm_spec(tm, w=DN_DK):
    return ((DN_HEADS, tm, w), lambda i: (0, i, 0))


def _prev(i):
    return jnp.maximum(i - 1, 0)


def _dn_front(first, xc, xp, ba, cw, hp):
    tm = xc.shape[0]
    nc = tm // DN_CHUNK
    outs = _wy(*[t.reshape(DN_HEADS * nc, DN_CHUNK, DN_DK) for t in _dn_pre(first, xc, xp, ba, cw, hp)])
    return tuple(o.reshape(DN_HEADS, tm, o.shape[-1]) for o in outs)


def _dn_front_ins(p, pba, cw, hp, tm):
    W3 = 3 * DN_HEADS * DN_DK
    return [(p, (tm, W3), lambda i: (i, 0)), (p, (HALO, W3), lambda i: (jnp.maximum(i * (tm // HALO) - 1, 0), 0)),
            (pba, (tm, LANES), lambda i: (i, 0)), _full(cw), _full(hp)]


def _dn_front_fwd(name, p, pba, cw, hp):
    T = p.shape[0]
    tm = _row_tile(T, 256)
    blk, im = _hm_spec(tm)
    blk2, im2 = _hm_spec(tm, DN_CHUNK)
    full = (DN_HEADS, T, DN_DK)
    outs = [(full, bf16, blk, im), ((DN_HEADS, T, DN_CHUNK), bf16, blk2, im2), (full, f32, blk, im),
            (full, bf16, blk, im), (full, bf16, blk, im), (full, f32, blk, im)]
    return _tile_call(name, lambda i, *a: _dn_front(i == 0, *a), T // tm, _dn_front_ins(p, pba, cw, hp, tm), outs)


def _dn_front_bwd(name, p, pba, cw, hp, cts):
    T = p.shape[0]
    tm = _row_tile(T, 256)
    n = T // tm
    W3 = 3 * DN_HEADS * DN_DK
    blk, im = _hm_spec(tm)
    blk2, im2 = _hm_spec(tm, DN_CHUNK)
    ins = _dn_front_ins(p, pba, cw, hp, tm) + [(c, blk2 if c.shape[-1] == DN_CHUNK else blk, im) for c in cts]

    def fn(i, xc, xp, ba, cw, hp, *cts):
        _, vjp = jax.vjp(lambda *a: _dn_front(i == 0, *a), xc, xp, ba, cw, hp)
        return vjp(tuple(cts))

    return _tile_call(name, fn, n, ins,
                      [((T, W3), bf16, (tm, W3), lambda i: (i, 0)), ((n * HALO, W3), f32, (HALO, W3), lambda i: (i, 0)),
                       ((T, LANES), f32, (tm, LANES), lambda i: (i, 0))],
                      [(tuple(cw.shape), f32), (tuple(hp.shape), f32)])


def _scan_fwd(name, qd, qk, u, w, kt, egl):
    T = qd.shape[1]
    C = DN_CHUNK
    U = _pick(T // C, (SCAN_CHUNKS, 1))
    n = T // (C * U)

    def body(qd_ref, qk_ref, u_ref, w_ref, kt_ref, egl_ref, o_ref, sall_ref, s_ref):
        i = pl.program_id(0)

        @pl.when(i == 0)
        def _():
            s_ref[...] = jnp.zeros(s_ref.shape, f32)

        S = s_ref[...]
        for j in range(U):
            sl = pl.ds(j * C, C)
            sall_ref[:, j] = S
            o, S = _scan_step(S, *[r[:, sl, :] for r in (qd_ref, qk_ref, u_ref, w_ref, kt_ref, egl_ref)])
            o_ref[:, sl, :] = o
        s_ref[...] = S

    blk, im = _hm_spec(C * U)
    blk2, im2 = _hm_spec(C * U, C)
    return pl.pallas_call(
        body, name=name, grid=(n,),
        in_specs=[pl.BlockSpec(blk, im), pl.BlockSpec(blk2, im2)] + [pl.BlockSpec(blk, im)] * 4,
        out_specs=[pl.BlockSpec(blk, im), pl.BlockSpec((DN_HEADS, U, DN_DK, DN_DK), lambda i: (0, i, 0, 0))],
        out_shape=[jax.ShapeDtypeStruct((DN_HEADS, T, DN_DK), f32), jax.ShapeDtypeStruct((DN_HEADS, n * U, DN_DK, DN_DK), f32)],
        scratch_shapes=[pltpu.VMEM((DN_HEADS, DN_DK, DN_DK), f32)],
        compiler_params=pltpu.CompilerParams(dimension_semantics=("arbitrary",), vmem_limit_bytes=VMEM_LIMIT),
    )(qd, qk, u, w, kt, egl)


def _scan_bwd(name, qd, qk, u, w, kt, egl, s_all, do):
    T = qd.shape[1]
    C = DN_CHUNK
    U = _pick(T // C, (SCAN_CHUNKS, 1))
    n = T // (C * U)

    def body(qd_ref, qk_ref, u_ref, w_ref, kt_ref, egl_ref, sall_ref, do_ref,
             dqd_ref, dqk_ref, du_ref, dw_ref, dkt_ref, degl_ref, ds_ref):
        i = pl.program_id(0)

        @pl.when(i == 0)
        def _():
            ds_ref[...] = jnp.zeros(ds_ref.shape, f32)

        dS = ds_ref[...]
        for j in reversed(range(U)):
            sl = pl.ds(j * C, C)
            args = [r[:, sl, :].astype(f32) for r in (qd_ref, qk_ref, u_ref, w_ref, kt_ref, egl_ref)]
            _, vjp = jax.vjp(_scan_step, sall_ref[:, j], *args)
            dS, *cts = vjp((do_ref[:, sl, :], dS))
            for r, v in zip((dqd_ref, dqk_ref, du_ref, dw_ref, dkt_ref, degl_ref), cts):
                r[:, sl, :] = v
        ds_ref[...] = dS

    blk = (DN_HEADS, C * U, DN_DK)
    blk2 = (DN_HEADS, C * U, C)
    rim = lambda i: (0, n - 1 - i, 0)
    sp, sp2 = pl.BlockSpec(blk, rim), pl.BlockSpec(blk2, rim)
    full, full2 = jax.ShapeDtypeStruct((DN_HEADS, T, DN_DK), f32), jax.ShapeDtypeStruct((DN_HEADS, T, C), f32)
    return pl.pallas_call(
        body, name=name, grid=(n,),
        in_specs=[sp, sp2, sp, sp, sp, sp, pl.BlockSpec((DN_HEADS, U, DN_DK, DN_DK), lambda i: (0, n - 1 - i, 0, 0)), sp],
        out_specs=[sp, sp2, sp, sp, sp, sp],
        out_shape=[full, full2, full, full, full, full],
        scratch_shapes=[pltpu.VMEM((DN_HEADS, DN_DK, DN_DK), f32)],
        compiler_params=pltpu.CompilerParams(dimension_semantics=("arbitrary",), vmem_limit_bytes=VMEM_LIMIT),
    )(qd, qk, u, w, kt, egl, s_all, do)


def _dn_post_fwd(name, o, p, nw):
    T = p.shape[0]
    tm = _row_tile(T, 512)
    blk, im = _hm_spec(tm)
    W = DN_HEADS * DN_DK
    return _tile_call(name, lambda i, o, z, nw: _dn_post(o, z, nw), T // tm,
                      [(o, blk, im), (p, (tm, W), lambda i: (i, 1536 // W)), _full(nw)],
                      [((T, W), f32, (tm, W), lambda i: (i, 0))])[0]


def _dn_post_bwd(name, o, p, nw, dout):
    T = p.shape[0]
    tm = _row_tile(T, 512)
    blk, im = _hm_spec(tm)
    W = DN_HEADS * DN_DK

    def fn(i, o, z, nw, dout):
        _, vjp = jax.vjp(_dn_post, o, z, nw)
        return vjp(dout)

    return _tile_call(name, fn, T // tm,
                      [(o, blk, im), (p, (tm, W), lambda i: (i, 1536 // W)), _full(nw), _rows(dout, tm)],
                      [((DN_HEADS, T, DN_DK), f32, blk, im), ((T, W), f32, (tm, W), lambda i: (i, 0))],
                      [((1, DN_DK), f32)])


def _swa_tile(T, big):
    return _pick(T, (2 * WINDOW, WINDOW)) if big else WINDOW


@functools.partial(jax.custom_vjp, nondiff_argnums=(1,))
def _heads_split(x, nh):
    return jnp.stack([x[:, SWA_DH * h:SWA_DH * (h + 1)] for h in range(nh)], axis=0)


@jax.custom_vjp
def _heads_merge(y):
    return jnp.concatenate([y[h] for h in range(y.shape[0])], axis=1)


_heads_split.defvjp(lambda x, nh: (_heads_split(x, nh), None), lambda nh, _, g: (_heads_merge(g),))
_heads_merge.defvjp(lambda y: (_heads_merge(y), None), lambda _, g: (_heads_split(g, g.shape[1] // SWA_DH),))


def _swa_rows(first, qp, kc, kp, vc, vp, snk):
    kv = [_heads_split(t, SWA_KV_HEADS) for t in (kc, kp, vc, vp)]
    return _heads_merge(_swa(first, _heads_split(qp, SWA_HEADS), *kv, snk))


def _swa_ins(p, snk, W):
    qw, kw = SWA_HEADS * SWA_DH, SWA_KV_HEADS * SWA_DH
    return [(p, (W, qw), lambda i: (i, 2048 // qw)),
            (p, (W, kw), lambda i: (i, 6144 // kw)), (p, (W, kw), lambda i: (_prev(i), 6144 // kw)),
            (p, (W, kw), lambda i: (i, 6272 // kw)), (p, (W, kw), lambda i: (_prev(i), 6272 // kw)), _full(snk)]


def _swa_fwd(name, p, snk):
    T = p.shape[0]
    W = _swa_tile(T, True)
    qw = SWA_HEADS * SWA_DH
    return _tile_call(name, lambda i, *a: _swa_rows(i == 0, *a), T // W, _swa_ins(p, snk, W),
                      [((T, qw), f32, (W, qw), lambda i: (i, 0))])[0]


def _swa_bwd(name, p, snk, do):
    T = p.shape[0]
    W = _swa_tile(T, False)
    qw, kw = SWA_HEADS * SWA_DH, SWA_KV_HEADS * SWA_DH

    def fn(i, qp, k_c, k_p, v_c, v_p, snk, do):
        _, vjp = jax.vjp(lambda *a: _swa_rows(i == 0, *a), qp, k_c, k_p, v_c, v_p, snk)
        return vjp(do)

    kv_out = ((T, kw), f32, (W, kw), lambda i: (i, 0))
    return _tile_call(name, fn, T // W, _swa_ins(p, snk, W) + [(do, (W, qw), lambda i: (i, 0))],
                      [((T, qw), bf16, (W, qw), lambda i: (i, 0)), kv_out, kv_out, kv_out, kv_out], [(tuple(snk.shape), f32)])


def _xa_fwd(name, p, kv):
    T = p.shape[0]
    tm = _row_tile(T, 256)
    W = XA_HEADS * XA_DH
    return _tile_call(name, lambda i, q, kv: _xa(q, kv), T // tm, [(p, (tm, W), lambda i: (i, 2560 // W)), _full(kv)],
                      [((T, W), f32, (tm, W), lambda i: (i, 0))])[0]


def _xa_bwd(name, p, kv, do):
    T = p.shape[0]
    tm = _row_tile(T, 256)
    W = XA_HEADS * XA_DH

    def fn(i, q, kv, do):
        _, vjp = jax.vjp(_xa, q, kv)
        return vjp(do)

    return _tile_call(name, fn, T // tm, [(p, (tm, W), lambda i: (i, 2560 // W)), _full(kv), _rows(do, tm)],
                      [((T, W), bf16, (tm, W), lambda i: (i, 0))], [(tuple(kv.shape), f32)])


def _merge_fwd(name, o_dn, o_sw, o_xa, p, wb):
    T = p.shape[0]
    tm = _row_tile(T, 256)
    GW = N_BRANCH * D_MODEL
    return _tile_call(name, lambda i, a, b, c, g, w: _merge(a, b, c, g, w.astype(f32)), T // tm,
                      [_rows(o_dn, tm), _rows(o_sw, tm), _rows(o_xa, tm), (p, (tm, GW), lambda i: (i, 1)), _full(wb)],
                      [((T, D_MODEL), bf16, (tm, D_MODEL), lambda i: (i, 0))])[0]


def _merge_bwd(name, o_dn, o_sw, o_xa, p, wb, dm):
    T = p.shape[0]
    tm = _row_tile(T, 512)
    GW = N_BRANCH * D_MODEL

    def fn(i, a, b, c, g, w, dm):
        _, vjp = jax.vjp(_merge, a, b, c, g, w.astype(f32))
        return vjp(dm)

    bo = ((T, BRANCH_W), f32, (tm, BRANCH_W), lambda i: (i, 0))
    return _tile_call(name, fn, T // tm,
                      [_rows(o_dn, tm), _rows(o_sw, tm), _rows(o_xa, tm), (p, (tm, GW), lambda i: (i, 1)), _full(wb), _rows(dm, tm)],
                      [bo, bo, bo, ((T, GW), bf16, (tm, GW), lambda i: (i, 0))], [(tuple(wb.shape), f32)])


def _assemble_dp(name, dxc, dxp, dz, dswq, dxaq, dgates, dkc, dkp, dvc, dvp, dba):
    T = dz.shape[0]
    tm = _row_tile(T, 256)
    n = T // tm
    r = tm // WINDOW
    nb = T // WINDOW

    def fn(i, dxc, dxp, dz, dswq, dxaq, dgates, dkc, dvc, dba, *nxt):
        halo = jnp.where(i == n - 1, 0.0, dxp)
        dqkv = dxc + jnp.concatenate([jnp.zeros((tm - HALO, dxc.shape[1]), f32), halo], axis=0)
        shifted = [jnp.concatenate([jnp.where(i * r + 1 + b <= nb - 1, blk, 0.0) for b, blk in enumerate(half)], axis=0)
                   for half in (nxt[:r], nxt[r:])]
        parts = [dqkv, dz, dswq, dxaq, dgates, dkc + shifted[0], dvc + shifted[1], dba, jnp.zeros((tm, LANES), f32)]
        return jnp.concatenate(parts, axis=1)

    ins = [_rows(dxc, tm), (dxp, (HALO, dxp.shape[1]), lambda i: (jnp.minimum(i + 1, n - 1), 0))]
    ins += [_rows(t, tm) for t in (dz, dswq, dxaq, dgates, dkc, dvc, dba)]
    for t in (dkp, dvp):
        ins += [(t, (WINDOW, t.shape[1]), lambda i, b=b: (jnp.minimum(i * r + 1 + b, nb - 1), 0)) for b in range(r)]
    return _tile_call(name, fn, n, ins, [((T, D_INP), bf16, (tm, D_INP), lambda i: (i, 0))])[0]


def _loss_grad(name, y, tgt):
    T = y.shape[0]
    tm = _row_tile(T, 512)

    def fn(i, y, t):
        e = y - t
        part = 0.5 * jnp.sum(jnp.mean(e * e, axis=-1, keepdims=True), axis=0, keepdims=True)
        return e * (1.0 / D_MODEL), jnp.broadcast_to(part, (1, LANES))

    return _tile_call(name, fn, T // tm, [_rows(y, tm), _rows(tgt, tm)],
                      [((T, D_MODEL), f32, (tm, D_MODEL), lambda i: (i, 0))], [((1, LANES), f32)])


def _mem_ln_fwd(name, mem, g, b):
    M = mem.shape[0]
    return _tile_call(name, lambda i, m, g, b: (_ln(m, g, b),), 1, [_full(mem), _full(g), _full(b)],
                      [((M, D_MODEL), bf16, (M, D_MODEL), lambda i: (0, 0))])[0]


def _mem_ln_bwd(name, mem, g, b, dmn):
    def fn(i, m, g, b, d):
        _, vjp = jax.vjp(lambda g, b: _ln(m, g, b), g, b)
        return vjp(d)

    return _tile_call(name, fn, 1, [_full(mem), _full(g), _full(b), _full(dmn)], [], [((1, D_MODEL), f32), ((1, D_MODEL), f32)])


def _pad_w_in(w):
    z = jnp.zeros((w.shape[0], D_INP - D_IN), w.dtype)
    return jnp.concatenate([w[:, 0:1536], w[:, 1544:2056], w[:, 2056:2568], w[:, 2824:3336], w[:, 3336:6408],
                            w[:, 2568:2696], w[:, 2696:2824], w[:, 1536:1544], z], axis=1)


def _unpad_dw_in(d):
    return jnp.concatenate([d[:, 0:1536], d[:, 6400:6408], d[:, 1536:2048], d[:, 2048:2560], d[:, 6144:6272],
                            d[:, 6272:6400], d[:, 2560:3072], d[:, 3072:6144]], axis=1)


def _lane_row(v, rows=8):
    out = jnp.zeros((rows, LANES), f32)
    return out.at[0, :v.shape[0]].set(v)


def _layer_fwd(l, x, xb, mem_nb, W):
    n = lambda s: f"l{l}_{s}"
    sv = {}
    u1, a1 = _gu_act(n("ffn1_gu"), xb, W['ffn1_w_gu_p'])
    f1, h1, h1b = _mm_res_ln(n("ffn1_down"), a1, W['ffn1_w_down'], x, W['ln_g'][0:1], W['ln_b'][0:1], 0.5)
    p = _mm(n("w_in"), h1b, W['w_inp'], 'nn', out_dtype=bf16)
    pba = _mm(n("w_in_ba"), h1b, W['w_ba'], 'nn')
    qd, qk, u, w, kt, egl = _dn_front_fwd(n("dn_front"), p, pba, W['conv_w'], W['hp'])
    o_raw, s_all = _scan_fwd(n("dn_scan"), qd, qk, u, w, kt, egl)
    o_dn = _dn_post_fwd(n("dn_post"), o_raw, p, W['norm_w'])
    o_sw = _swa_fwd(n("swa"), p, W['snk'])
    kv = _mm(n("mem_kv"), mem_nb, W['w_mem_kv'], 'nn')
    o_xa = _xa_fwd(n("xa"), p, kv)
    merged = _merge_fwd(n("merge"), o_dn, o_sw, o_xa, p, W['w_branch'])
    mo, h2, h2b = _mm_res_ln(n("w_out"), merged, W['w_out'], h1, W['ln_g'][1:2], W['ln_b'][1:2], 1.0)
    u2, a2 = _gu_act(n("ffn2_gu"), h2b, W['ffn2_w_gu_p'])
    f2, h3, h3b = _mm_res_ln(n("ffn2_down"), a2, W['ffn2_w_down'], h2, W['ln_g'][2:3], W['ln_b'][2:3], 0.5)
    sv = dict(x=x, xb=xb, u1=u1, a1=a1, f1=f1, h1=h1, h1b=h1b, p=p, pba=pba, wy=(qd, qk, u, w, kt, egl),
              o_raw=o_raw, s_all=s_all, o_dn=o_dn, o_sw=o_sw, kv=kv, o_xa=o_xa, merged=merged,
              mo=mo, h2=h2, h2b=h2b, u2=u2, a2=a2, f2=f2)
    return h3, h3b, sv


def _ffn_bwd(n, tag, x, xb, u, a, f, g, b, w_gu_t, w_down_t, dh):
    dx_a, df, dg, db = _res_ln_bwd(n(f"{tag}_ln_bwd"), x, f, g, b, 0.5, dh)
    d_down = _mm(n(f"{tag}_down_dw"), a, df, 'tn')
    du = _down_dx_act_bwd(n(f"{tag}_down_dx"), df, w_down_t, u)
    d_gu = _mm(n(f"{tag}_gu_dw"), xb, du, 'tn', out_cols=(FF_BLK, lambda j: (j % 2) * (D_FF // FF_BLK) + j // 2))
    dx = _mm(n(f"{tag}_gu_dx"), du, w_gu_t, 'nn', add=dx_a)
    return dx, d_gu, d_down, dg, db


def _layer_bwd(l, sv, mem_nb, W, dh3, dmem_acc):
    n = lambda s: f"l{l}_{s}"
    G = {}
    dh2, G['ffn2_w_gu'], G['ffn2_w_down'], dg2, db2 = _ffn_bwd(
        n, "ffn2", sv['h2'], sv['h2b'], sv['u2'], sv['a2'], sv['f2'], W['ln_g'][2:3], W['ln_b'][2:3], W['ffn2_w_gu_t'], W['ffn2_w_down_t'], dh3)
    dh1_a, dmo, dg1, db1 = _res_ln_bwd(n("ln1_bwd"), sv['h1'], sv['mo'], W['ln_g'][1:2], W['ln_b'][1:2], 1.0, dh2)
    G['w_out'] = _mm(n("w_out_dw"), sv['merged'], dmo, 'tn')
    dmerged = _mm(n("w_out_dx"), dmo, W['w_out_t'], 'nn')
    p = sv['p']
    do_dn, do_sw, do_xa, dgates, G['w_branch'] = _merge_bwd(n("merge_bwd"), sv['o_dn'], sv['o_sw'], sv['o_xa'], p, W['w_branch'], dmerged)
    dxaq, dkv = _xa_bwd(n("xa_bwd"), p, sv['kv'], do_xa)
    dkv = dkv.astype(bf16)
    G['w_mem_kv'] = _mm(n("mem_kv_dw"), mem_nb, dkv, 'tn')
    dmem_n = _mm(n("mem_kv_dx"), dkv, W['w_mem_kv_t'], 'nn', add=dmem_acc)
    dswq, dkc, dkp, dvc, dvp, dsnk = _swa_bwd(n("swa_bwd"), p, W['snk'], do_sw)
    do_raw, dz, dnw = _dn_post_bwd(n("dn_post_bwd"), sv['o_raw'], p, W['norm_w'], do_dn)
    cts = _scan_bwd(n("dn_scan_bwd"), *sv['wy'], sv['s_all'], do_raw)
    dxc, dxp, dba, dcw, dhp = _dn_front_bwd(n("dn_front_bwd"), p, sv['pba'], W['conv_w'], W['hp'], cts)
    dp = _assemble_dp(n("dp"), dxc, dxp, dz, dswq, dxaq, dgates, dkc, dkp, dvc, dvp, dba)
    G['w_in'] = _unpad_dw_in(_mm(n("w_in_dw"), sv['h1b'], dp, 'tn'))
    dh1 = _mm(n("w_in_dx"), dp, W['w_inp_t'], 'nn', add=dh1_a)
    dx, G['ffn1_w_gu'], G['ffn1_w_down'], dg0, db0 = _ffn_bwd(
        n, "ffn1", sv['x'], sv['xb'], sv['u1'], sv['a1'], sv['f1'], W['ln_g'][0:1], W['ln_b'][0:1], W['ffn1_w_gu_t'], W['ffn1_w_down_t'], dh1)
    G['ln_g'] = jnp.concatenate([dg0, dg1, dg2], axis=0)
    G['ln_b'] = jnp.concatenate([db0, db1, db2], axis=0)
    G['dn_conv_w'] = dcw
    G['dn_a_log'] = dhp[0, :DN_HEADS]
    G['dn_dt_bias'] = dhp[1, :DN_HEADS]
    G['dn_norm_w'] = dnw[0]
    G['swa_sinks'] = dsnk[:, 0]
    return dx, dmem_n, G


def _local_step(x, mem, tgt, Wf):
    mem_g, mem_b = Wf['mem_ln_g'][None, :], Wf['mem_ln_b'][None, :]
    mem_nb = _mem_ln_fwd("mem_ln", mem, mem_g, mem_b)

    def paired(k, l):
        return Wf[k + '_p'][l] if k + '_p' in Wf else _pair_gate_up(Wf[k][l])

    layers = []
    for l in range(DEPTH):
        layers.append(dict(
            ln_g=Wf['ln_g'][l], ln_b=Wf['ln_b'][l], ffn1_w_gu_p=paired('ffn1_w_gu', l), ffn1_w_down=Wf['ffn1_w_down'][l],
            w_inp=_pad_w_in(Wf['w_in'][l]), conv_w=Wf['dn_conv_w'][l],
            hp=jnp.zeros((8, LANES), f32).at[0, :DN_HEADS].set(Wf['dn_a_log'][l]).at[1, :DN_HEADS].set(Wf['dn_dt_bias'][l]),
            norm_w=Wf['dn_norm_w'][l][None, :], snk=jnp.broadcast_to(Wf['swa_sinks'][l][:, None], (SWA_HEADS, LANES)),
            w_mem_kv=Wf['w_mem_kv'][l], w_branch=Wf['w_branch'][l], w_out=Wf['w_out'][l],
            ffn2_w_gu_p=paired('ffn2_w_gu', l), ffn2_w_down=Wf['ffn2_w_down'][l]))
        layers[l]['w_ba'] = layers[l]['w_inp'][:, 6400:6400 + LANES]
        for k in ('ffn1_w_gu_p', 'ffn1_w_down', 'w_inp', 'w_mem_kv', 'w_out', 'ffn2_w_gu_p', 'ffn2_w_down'):
            layers[l][k.replace('_p', '') + '_t'] = layers[l][k].T
    h, hb = x, x.astype(bf16)
    saved = []
    for l in range(DEPTH):
        h, hb, sv = _layer_fwd(l, h, hb, mem_nb, layers[l])
        saved.append(sv)
    dh, loss_row = _loss_grad("loss", h, tgt)
    grads = [None] * DEPTH
    dmem_n = None
    for l in reversed(range(DEPTH)):
        dh, dmem_n, grads[l] = _layer_bwd(l, saved[l], mem_nb, layers[l], dh, dmem_n)
    dmg, dmb = _mem_ln_bwd("mem_ln_bwd", mem, mem_g, mem_b, dmem_n)
    G = {k: jnp.stack([grads[l][k] for l in range(DEPTH)], axis=0) for k in grads[0]}
    G['mem_ln_g'], G['mem_ln_b'] = dmg[0], dmb[0]
    return loss_row, dh, G


SEG_ALIGN = 2048


def _round_up(n, m):
    return (n + m - 1) // m * m


def _layout(names, sizes, mult):
    off, table = 0, {}
    for nm in names:
        table[nm] = (off, sizes[nm])
        off += _round_up(sizes[nm], SEG_ALIGN)
    return table, _round_up(off // LANES, mult)


def _pack(table, rows, flat):
    names = list(table)
    lead = flat[names[0]].shape[:-1]
    parts, pos = [], 0
    for nm in names:
        off, size = table[nm]
        if off > pos:
            parts.append(jnp.zeros(lead + (off - pos,), flat[nm].dtype))
        parts.append(flat[nm])
        pos = off + size
    total = rows * LANES
    if total > pos:
        parts.append(jnp.zeros(lead + (total - pos,), parts[-1].dtype))
    return jnp.concatenate(parts, axis=-1).reshape(lead + (rows, LANES))


def _unpack(table, packed, nm):
    off, size = table[nm]
    flat = packed.reshape(packed.shape[:-2] + (-1,))
    return flat[..., off:off + size]


def _split_shards(full, ax):
    shp = full.shape
    t = full.reshape(shp[:ax] + (N_SHARD, shp[ax] // N_SHARD) + shp[ax + 1:])
    return jnp.moveaxis(t, ax, 0)


def _join_shards(sh4, ax):
    return jnp.concatenate([sh4[s] for s in range(N_SHARD)], axis=ax)


ANY = pl.BlockSpec(memory_space=pl.ANY)


def _me():
    return lax.axis_index("x"), lax.axis_index("y"), lax.axis_index("c")


def _comm_call(name, body, arrays, out_shapes, n_sem):
    n = len(arrays)
    scratch = [pltpu.SemaphoreType.DMA((n, n_sem)), pltpu.SemaphoreType.DMA((n, n_sem))]
    return pl.pallas_call(
        body, name=name, out_shape=out_shapes, in_specs=[ANY] * n, out_specs=[ANY] * n, scratch_shapes=scratch,
        compiler_params=pltpu.CompilerParams(has_side_effects=True),
    )(*arrays)


def _all_gather(name, xs):
    n = len(xs)

    def split(shape):
        ax = next(i for i, d in enumerate(shape) if d % 2 == 0)
        return ax, shape[ax] // 2

    def body(*refs):
        x_refs, out_refs = refs[:n], refs[n:2 * n]
        send_sems, recv_sems = refs[2 * n:]
        mx, my, mc = _me()
        me_s = 2 * mx + my
        across_x, across_y, diag = (1 - mx, my), (mx, 1 - my), (1 - mx, 1 - my)
        sid = lambda chip: 2 * chip[0] + chip[1]

        def part(ref, a, which):
            if which is None:
                return ref
            ax, size = split(xs[a].shape[1:])
            return ref.at[(slice(None),) * ax + (pl.ds(which * size, size),)]

        def copy(a, k, shard, half, which, to, src=None):
            dst = part(out_refs[a].at[shard, half], a, which)
            return pltpu.make_async_remote_copy(src_ref=dst if src is None else part(src, a, which), dst_ref=dst,
                                                send_sem=send_sems.at[a, k], recv_sem=recv_sems.at[a, k], device_id=to, device_id_type=MESH)

        sibling, here = (mx, my, 1 - mc), (mx, my, mc)
        sent = []
        for a in range(n):
            sent += [copy(a, 0, me_s, mc, None, (*across_x, mc), src=x_refs[a].at[mc]),
                     copy(a, 1, me_s, mc, None, (*across_y, mc), src=x_refs[a].at[mc])]
        for cp in sent:
            cp.start()
        landing = [(0, across_x, None, across_y), (1, across_y, None, across_x), (2, diag, 0, None), (3, diag, 1, None)]
        for k, origin, which, relay_to in landing:
            for a in range(n):
                copy(a, k, sid(origin), mc, which, here).wait_recv()
                if relay_to is not None:
                    sent.append(copy(a, 2 + k, sid(origin), mc, k, (*relay_to, mc)))
                    sent[-1].start()
                sent.append(copy(a, 4 + k, sid(origin), mc, which, sibling))
                sent[-1].start()
        for k, origin, which, _ in landing:
            for a in range(n):
                copy(a, 4 + k, sid(origin), 1 - mc, which, here).wait_recv()
        for cp in sent:
            cp.wait_send()

    return _comm_call(name, body, xs, [jax.ShapeDtypeStruct((N_SHARD,) + x.shape, x.dtype) for x in xs], 8)


def _pair_exchange(name, items):
    n = len(items)

    def body(*refs):
        src_refs, dst_refs = refs[:n], refs[n:2 * n]
        send_sems, recv_sems = refs[2 * n:]
        mx, my, mc = _me()
        cps = [pltpu.make_async_remote_copy(src_ref=src_refs[a].at[s, 1 - mc], dst_ref=dst_refs[a].at[s], send_sem=send_sems.at[a, s],
                                            recv_sem=recv_sems.at[a, s], device_id=(mx, my, 1 - mc), device_id_type=MESH)
               for a in range(n) for s in range(N_SHARD)]
        for cp in cps:
            cp.start()
        for cp in cps:
            cp.wait()

    return _comm_call(name, body, items, [jax.ShapeDtypeStruct((N_SHARD,) + t.shape[2:], t.dtype) for t in items], N_SHARD)


def _chip_exchange(name, parts):
    n = len(parts)

    def body(*refs):
        p_refs, dst_refs = refs[:n], refs[n:2 * n]
        send_sems, recv_sems = refs[2 * n:]
        mx, my, mc = _me()
        chips = [(1 - mx, my), (mx, 1 - my), (1 - mx, 1 - my)]
        cps = [pltpu.make_async_remote_copy(src_ref=p_refs[a].at[2 * cx + cy], dst_ref=dst_refs[a].at[j], send_sem=send_sems.at[a, j],
                                            recv_sem=recv_sems.at[a, j], device_id=(cx, cy, mc), device_id_type=MESH)
               for a in range(n) for j, (cx, cy) in enumerate(chips)]
        for cp in cps:
            cp.start()
        for cp in cps:
            cp.wait()

    return _comm_call(name, body, parts, [jax.ShapeDtypeStruct((3,) + t.shape[1:], t.dtype) for t in parts], 3)


def _pair_swap(name, reds):
    n = len(reds)

    def body(*refs):
        r_refs, out_refs = refs[:n], refs[n:2 * n]
        send_sems, recv_sems = refs[2 * n:]
        mx, my, mc = _me()
        cps = [pltpu.make_async_remote_copy(src_ref=r_refs[a], dst_ref=out_refs[a], send_sem=send_sems.at[a, 0],
                                            recv_sem=recv_sems.at[a, 0], device_id=(mx, my, 1 - mc), device_id_type=MESH)
               for a in range(n)]
        for cp in cps:
            cp.start()
        for cp in cps:
            cp.wait()

    return _comm_call(name, body, reds, [jax.ShapeDtypeStruct(t.shape, t.dtype) for t in reds], 1)


EW_BLOCK_BYTES = 1 << 20


def _ew_call(name, fn, ins, n_out, out_dtype=f32):
    shape = ins[0].shape
    last = shape[-1]
    flat = [a.reshape(-1, last) for a in ins]
    R = flat[0].shape[0]
    cands = tuple(c for c in (4096, 2048, 1024, 512, 256, 128, 64, 32, 16, 8) if c * last * 4 <= EW_BLOCK_BYTES)
    tr = _pick(R, cands)
    res = _tile_call(name, lambda i, *a: fn(*a), R // tr, [(a, (tr, last), lambda i: (i, 0)) for a in flat],
                     [((R, last), out_dtype, (tr, last), lambda i: (i, 0))] * n_out)
    return [r.reshape(shape) for r in res]


def _adamw(g, w, m, v):
    m = B1 * m + (1.0 - B1) * g
    v = B2 * v + (1.0 - B2) * jnp.square(g)
    m_hat = m / (1.0 - B1 ** STEP)
    v_hat = v / (1.0 - B2 ** STEP)
    return -LR * (m_hat / (jnp.sqrt(v_hat) + EPS) + WD * w), m, v


def _adamw_call(name, mine, theirs, w, m, v, mc1):
    shape, last = w.shape, w.shape[-1]
    g2 = [t.reshape(-1, last) for t in (mine, theirs)]
    w3 = [t.reshape(2, -1, last) for t in (w, m, v)]
    R = g2[0].shape[0]
    tr = _pick(R, tuple(c for c in (4096, 2048, 1024, 512, 256, 128, 64, 32, 16, 8) if c * last * 4 <= EW_BLOCK_BYTES))

    def body(mc_ref, mine_ref, theirs_ref, w_ref, m_ref, v_ref, g_out, d_out, m_out, v_out):
        g = jnp.where(pl.program_id(0) == mc_ref[0], mine_ref[...], theirs_ref[...])
        d, nm, nv = _adamw(g, w_ref[0], m_ref[0], v_ref[0])
        g_out[0], d_out[0], m_out[0], v_out[0] = g, d, nm, nv

    half = pl.BlockSpec((tr, last), lambda h, i: (i, 0))
    full = pl.BlockSpec((1, tr, last), lambda h, i: (h, i, 0))
    res = pl.pallas_call(
        body, name=name, grid=(2, R // tr),
        in_specs=[pl.BlockSpec(memory_space=pltpu.SMEM), half, half, full, full, full], out_specs=[full] * 4,
        out_shape=[jax.ShapeDtypeStruct((2, R, last), f32)] * 4,
        compiler_params=pltpu.CompilerParams(dimension_semantics=("arbitrary", "arbitrary"), vmem_limit_bytes=VMEM_LIMIT),
    )(mc1, *g2, *w3)
    return tuple(r.reshape(shape) for r in res)


def kernel(x, mem, mem_ln_g, mem_ln_b, ln_g, ln_b, ffn1_w_gu, ffn1_w_down, w_in, dn_conv_w, dn_a_log, dn_dt_bias, dn_norm_w, swa_sinks, w_mem_kv, w_branch, w_out, ffn2_w_gu, ffn2_w_down, loss_target, m_mem_ln_g, m_mem_ln_b, m_ln_g, m_ln_b, m_ffn1_w_gu, m_ffn1_w_down, m_w_in, m_dn_conv_w, m_dn_a_log, m_dn_dt_bias, m_dn_norm_w, m_swa_sinks, m_w_mem_kv, m_w_branch, m_w_out, m_ffn2_w_gu, m_ffn2_w_down, v_mem_ln_g, v_mem_ln_b, v_ln_g, v_ln_b, v_ffn1_w_gu, v_ffn1_w_down, v_w_in, v_dn_conv_w, v_dn_a_log, v_dn_dt_bias, v_dn_norm_w, v_swa_sinks, v_w_mem_kv, v_w_branch, v_w_out, v_ffn2_w_gu, v_ffn2_w_down):
    args = dict(locals())
    Ws = {n: args[n] for n in WEIGHTS}
    Ms = {n: args["m_" + n] for n in WEIGHTS}
    Vs = {n: args["v_" + n] for n in WEIGHTS}
    mc = lax.axis_index("c")
    my_s = 2 * lax.axis_index("x") + lax.axis_index("y")
    small = [n for n in WEIGHTS if n not in MATRICES]

    ag_table, ag_rows = _layout(SMALL_SHARDED, {n: Ws[n].size for n in SMALL_SHARDED}, 16)
    ag_small = _pack(ag_table, ag_rows, {n: Ws[n].reshape(-1) for n in SMALL_SHARDED}).reshape(2, ag_rows // 2, LANES)
    local = [Ws[n].astype(bf16) for n in MATRICES] + [ag_small]
    gathered = _all_gather("all_gather_w", local)
    gathered = [jnp.stack([jnp.where(my_s == s, loc, g[s]) for s in range(N_SHARD)], axis=0) for loc, g in zip(local, gathered)]
    Wf = {n: _join_shards(g, SHARD_AXIS[n]) for n, g in zip(MATRICES, gathered) if not n.endswith('w_gu')}
    for n, g in zip(MATRICES, gathered):
        if n.endswith('w_gu'):
            assert g.shape[-1] == FF_BLK
            Wf[n + '_p'] = jnp.concatenate([g[0], g[2], g[1], g[3]], axis=2)
    g_small = gathered[-1].reshape(N_SHARD, ag_rows, LANES)
    for n in SMALL_SHARDED:
        Wf[n] = _join_shards(_unpack(ag_table, g_small, n).reshape((N_SHARD,) + Ws[n].shape), SHARD_AXIS[n])
    for n in WEIGHTS:
        if SHARD_AXIS[n] is None:
            Wf[n] = Ws[n]

    loss_row, dx, G = _local_step(x[0], mem[0], loss_target[0], Wf)

    table, rows = _layout(small + ['loss'], {**{n: Ws[n].size for n in small}, 'loss': 1}, 16)
    gflat = {n: (jnp.broadcast_to(G[n].reshape(1, -1), (N_SHARD, G[n].size)) if SHARD_AXIS[n] is None
                 else _split_shards(G[n], SHARD_AXIS[n]).reshape(N_SHARD, -1)) for n in small}
    gflat['loss'] = jnp.broadcast_to(loss_row[:, :1], (N_SHARD, 1))
    items = [_split_shards(G[n], SHARD_AXIS[n]) for n in MATRICES] + [_pack(table, rows, gflat).reshape(N_SHARD, 2, rows // 2, LANES)]
    tags = MATRICES + ['small']
    got = _pair_exchange("rs_pair", items)
    keep = [lax.dynamic_index_in_dim(a, mc, axis=1, keepdims=False) for a in items]
    wire = [bf16] * len(MATRICES) + [f32]
    part = [_ew_call(f"rs_add_pair_{t}", lambda a, b: a + b, [k, g], 1, out_dtype=dt)[0] for t, k, g, dt in zip(tags, keep, got, wire)]
    others = _chip_exchange("rs_chips", part)
    own = lambda a: lax.dynamic_index_in_dim(a, my_s, axis=0, keepdims=False)
    red = [_ew_call(f"rs_add_chips_{t}", lambda k, g, fx, fy, fxy: ((k + g) + fy) + (fx + fxy), [own(k), own(g), o[0], o[1], o[2]], 1)[0]
           for t, k, g, o in zip(tags, keep, got, others)]
    theirs = _pair_swap("rs_swap", red)

    mc1 = mc.astype(i32).reshape(1)
    outs = {}
    for n, a, b in zip(MATRICES, red, theirs):
        outs[n] = _adamw_call(f"adamw_{n}", a, b, Ws[n], Ms[n], Vs[n], mc1)
    fill = {'loss': jnp.zeros((1,), f32)}
    packs = [_pack(table, rows, {**{n: src[n].reshape(-1) for n in small}, **fill}).reshape(2, rows // 2, LANES) for src in (Ws, Ms, Vs)]
    small_out = [p.reshape(rows, LANES) for p in _adamw_call("adamw_small", red[-1], theirs[-1], *packs, mc1)]
    for n in small:
        outs[n] = tuple(_unpack(table, p, n).reshape(Ws[n].shape) for p in small_out)
    loss = _unpack(table, small_out[0], 'loss').reshape(())
    return (loss, dx[None], *[outs[n][k] for k in range(4) for n in WEIGHTS])
```

```python
import functools

import jax
import jax.numpy as jnp
from jax import lax
from jax.experimental import pallas as pl
from jax.experimental.pallas import tpu as pltpu

f32, bf16, i32 = jnp.float32, jnp.bfloat16, jnp.int32
HI = lax.Precision.HIGHEST
MESH = pl.DeviceIdType.MESH

D_MODEL = 1024
DEPTH = 2
DN_HEADS, DN_DK, DN_CONV, DN_CHUNK = 4, 128, 4, 64
SWA_HEADS, SWA_KV_HEADS, SWA_DH, WINDOW = 8, 2, 64, 128
XA_HEADS, XA_DH = 4, 128
D_FF = 2816
N_BRANCH, BRANCH_W = 3, 512
ALPHA = (2 * DEPTH) ** 0.25
LN_EPS, RMS_EPS, NEG_INF = 1e-5, 1e-6, -1e30
D_IN = 6408
D_INP = 6656
LR, B1, B2, EPS, WD, STEP = 0.001, 0.9, 0.999, 1e-08, 0.01, 10

LANES = 128
VMEM_LIMIT = 56 << 20
N_SHARD = 4
SCAN_CHUNKS = 4
HALO = 16
MM_TILES = {
    'ffn1_gu_dx': (512, 1024, 5632), 'ffn2_gu_dx': (512, 1024, 5632), 'w_in_dx': (512, 1024, 6656), 'w_in': (2048, 512, 1024),
    'w_in_dw': (1024, 512, 4096), 'ffn1_down_dw': (1408, 512, 4096), 'ffn2_down_dw': (1408, 512, 4096),
    'ffn1_down_dx': (1024,), 'ffn2_down_dx': (1024,), 'ffn1_down': (1024,), 'ffn2_down': (1024,), 'w_out': (1024,),
    'ffn1_gu': (1024,), 'ffn2_gu': (1024,),
    'w_out_dx': (512, 1024, 1024),
}


def _tiles(name, default=None):
    return MM_TILES.get(name, MM_TILES.get(name.split('_', 1)[-1], default))

WEIGHTS = ['mem_ln_g', 'mem_ln_b', 'ln_g', 'ln_b', 'ffn1_w_gu', 'ffn1_w_down', 'w_in', 'dn_conv_w', 'dn_a_log',
           'dn_dt_bias', 'dn_norm_w', 'swa_sinks', 'w_mem_kv', 'w_branch', 'w_out', 'ffn2_w_gu', 'ffn2_w_down']
SHARD_AXIS = {'mem_ln_g': None, 'mem_ln_b': None, 'ln_g': 2, 'ln_b': 2, 'ffn1_w_gu': 2, 'ffn1_w_down': 1, 'w_in': 2,
              'dn_conv_w': 2, 'dn_a_log': None, 'dn_dt_bias': None, 'dn_norm_w': None, 'swa_sinks': None,
              'w_mem_kv': 1, 'w_branch': 3, 'w_out': 1, 'ffn2_w_gu': 2, 'ffn2_w_down': 1}
MATRICES = ['ffn1_w_gu', 'ffn1_w_down', 'w_in', 'w_mem_kv', 'w_branch', 'w_out', 'ffn2_w_gu', 'ffn2_w_down']
SMALL_SHARDED = ['ln_g', 'ln_b', 'dn_conv_w']


def _dg(a, b, mode, hi):
    nb = a.ndim - 2
    bd = tuple(range(nb))
    ca = nb if mode == 'tn' else nb + 1
    cb = nb + 1 if mode == 'nt' else nb
    dims = (((ca,), (cb,)), (bd, bd))
    dot = lambda x, y: lax.dot_general(x, y, dims, preferred_element_type=f32)
    a_hi, b_hi = a.astype(bf16), b.astype(bf16)
    if not hi:
        return dot(a_hi, b_hi)
    a_lo = (a.astype(f32) - a_hi.astype(f32)).astype(bf16)
    b_lo = (b.astype(f32) - b_hi.astype(f32)).astype(bf16)
    return dot(a_hi, b_hi) + (dot(a_hi, b_lo) + dot(a_lo, b_hi))


@functools.partial(jax.custom_vjp, nondiff_argnums=(2, 3))
def _dot(a, b, mode, hi):
    return _dg(a, b, mode, hi)


def _dot_fwd(a, b, mode, hi):
    return _dg(a, b, mode, hi), (a, b)


def _dot_bwd(mode, hi, res, g):
    a, b = res
    if mode == 'nn':
        da, db = _dg(g, b, 'nt', hi), _dg(a, g, 'tn', hi)
    elif mode == 'nt':
        da, db = _dg(g, b, 'nn', hi), _dg(g, a, 'tn', hi)
    else:
        da, db = _dg(b, g, 'nt', hi), _dg(a, g, 'nn', hi)
    return da.astype(a.dtype), db.astype(b.dtype)


_dot.defvjp(_dot_fwd, _dot_bwd)


def bdot(a, b, mode):
    return _dot(a, b, mode, False)


def hdot(a, b, mode):
    return _dot(a, b, mode, True)


def _shift_rows_impl(x, tail, s):
    r = pltpu.roll(x, s, 0)
    rows = lax.broadcasted_iota(i32, tail.shape, 0)
    top = jnp.where(rows >= s, r[:HALO], pltpu.roll(tail, s, 0))
    return jnp.concatenate([top, r[HALO:]], axis=0)


@functools.partial(jax.custom_vjp, nondiff_argnums=(2,))
def _shift_rows(x, tail, s):
    return _shift_rows_impl(x, tail, s)


def _shift_rows_fwd(x, tail, s):
    return _shift_rows_impl(x, tail, s), None


def _shift_rows_bwd(s, _, g):
    n = g.shape[0]
    r = pltpu.roll(g, n - s, 0)
    rows = lax.broadcasted_iota(i32, (HALO, g.shape[1]), 0)
    last = r[n - HALO:]
    keep = rows < HALO - s
    dx = jnp.concatenate([r[:n - HALO], jnp.where(keep, last, 0.0)], axis=0)
    return dx, jnp.where(keep, 0.0, pltpu.roll(g[:HALO], HALO - s, 0))


_shift_rows.defvjp(_shift_rows_fwd, _shift_rows_bwd)


def _lane_pick(x, lane):
    idx = lax.broadcasted_iota(i32, x.shape, x.ndim - 1)
    return jnp.sum(jnp.where(idx == lane, x, 0.0), axis=-1, keepdims=True)


def _silu(x):
    return x * jax.nn.sigmoid(x)


def _tri_inv(a):
    C = a.shape[-1]
    eye = (lax.broadcasted_iota(i32, a.shape, 1) == lax.broadcasted_iota(i32, a.shape, 2)).astype(f32)
    p = -a
    x = eye + p
    for _ in range((C - 1).bit_length() - 1):
        p = _dg(p, p, 'nn', True)
        x = x + _dg(x, p, 'nn', True)
    return x


@jax.custom_vjp
def _tri_solve(a, rhs):
    return _dg(_tri_inv(a), rhs, 'nn', True)


def _tri_solve_fwd(a, rhs):
    tinv = _tri_inv(a)
    sol = _dg(tinv, rhs, 'nn', True)
    return sol, (tinv, sol)


def _tri_solve_bwd(res, g):
    tinv, sol = res
    d_rhs = _dg(tinv, g, 'tn', True)
    return -_dg(d_rhs, sol, 'nt', True), d_rhs


_tri_solve.defvjp(_tri_solve_fwd, _tri_solve_bwd)


@functools.partial(jax.custom_vjp, nondiff_argnums=(1,))
def _lane_head(x, n):
    return x[:, :, :n]


def _lane_head_fwd(x, n):
    return x[:, :, :n], None


def _lane_head_bwd(n, _, g):
    s = jnp.sum(g, axis=-1, keepdims=True) * (1.0 / LANES)
    return (jnp.broadcast_to(s, g.shape[:-1] + (LANES,)),)


_lane_head.defvjp(_lane_head_fwd, _lane_head_bwd)


@functools.partial(jax.custom_vjp, nondiff_argnums=(1,))
def _last_row(x, c):
    return x[:, c - 1:, :]


def _last_row_fwd(x, c):
    return x[:, c - 1:, :], None


def _last_row_bwd(c, _, g):
    shape = (g.shape[0], c, g.shape[2])
    rows = lax.broadcasted_iota(i32, shape, 1)
    return (jnp.where(rows == c - 1, jnp.broadcast_to(g, shape), 0.0),)


_last_row.defvjp(_last_row_fwd, _last_row_bwd)


def _full(a):
    nd = a.ndim
    return (a, tuple(a.shape), lambda i, _nd=nd: (0,) * _nd)


def _rows(a, tm, col=0, width=None):
    width = a.shape[1] if width is None else width
    return (a, (tm, width), lambda i, _c=col: (i, _c))


def _tile_call(name, fn, n, ins, outs, accs=()):
    n_in, n_out, n_acc = len(ins), len(outs), len(accs)

    def body(*refs):
        i = pl.program_id(0)
        res = fn(i, *[r[...].astype(f32) for r in refs[:n_in]])
        if not isinstance(res, (tuple, list)):
            res = (res,)
        assert len(res) == n_out + n_acc, (name, len(res), n_out, n_acc)
        for r, v in zip(refs[n_in:n_in + n_out], res[:n_out]):
            r[...] = v.astype(r.dtype)
        if n_acc:
            acc_refs = refs[n_in + n_out:]

            @pl.when(i == 0)
            def _():
                for r in acc_refs:
                    r[...] = jnp.zeros(r.shape, r.dtype)

            for r, v in zip(acc_refs, res[n_out:]):
                r[...] += v.astype(r.dtype)

    out_shape = [jax.ShapeDtypeStruct(s, d) for s, d, _, _ in outs] + [jax.ShapeDtypeStruct(s, d) for s, d in accs]
    out_specs = [pl.BlockSpec(b, m) for _, _, b, m in outs]
    out_specs += [pl.BlockSpec(tuple(s), lambda i, _nd=len(s): (0,) * _nd) for s, _ in accs]
    res = pl.pallas_call(
        body, name=name, grid=(n,),
        in_specs=[pl.BlockSpec(b, m) for _, b, m in ins],
        out_specs=out_specs, out_shape=out_shape,
        compiler_params=pltpu.CompilerParams(dimension_semantics=("arbitrary",), vmem_limit_bytes=VMEM_LIMIT),
    )(*[a for a, _, _ in ins])
    return res


def _pick(n, cands):
    for c in cands:
        if n % c == 0:
            return c
    return n


def _mm(name, a, b, mode, out_dtype=f32, add=None, out_cols=None):
    if mode == 'tn':
        K, M = a.shape
    else:
        M, K = a.shape
    N = b.shape[0] if mode == 'nt' else b.shape[1]
    tm = _pick(M, (1024, 1408, 512, 256, 128))
    tn = _pick(N, (512, 1408, 256, 128)) if out_cols is None else out_cols[0]
    tuned = _tiles(name)
    if tuned is not None and (M % tuned[0] or N % tuned[1] or K % tuned[2]):
        tuned = None
    if tuned is not None:
        tm, tn = tuned[:2]
    col = (lambda j: j) if out_cols is None else out_cols[1]
    tk = K if K <= 3328 else _pick(K, (3328, 2816, 2048, 1024, 512, 256, 128))
    if tuned is not None:
        tk = tuned[2]
    nk = K // tk
    ca = 0 if mode == 'tn' else 1
    cb = 1 if mode == 'nt' else 0
    dims = (((ca,), (cb,)), ((), ()))

    def body(*refs):
        a_ref, b_ref = refs[0], refs[1]
        add_ref = refs[2] if add is not None else None
        part = lax.dot_general(a_ref[...].astype(bf16), b_ref[...].astype(bf16), dims, preferred_element_type=f32)

        def finish(r, o_ref):
            if add_ref is not None:
                r = r + add_ref[...].astype(f32)
            o_ref[...] = r.astype(o_ref.dtype)

        if nk == 1:
            finish(part, refs[-1])
            return
        o_ref, acc_ref = refs[-2], refs[-1]
        k = pl.program_id(2)

        @pl.when(k == 0)
        def _():
            acc_ref[...] = part

        @pl.when(k > 0)
        def _():
            acc_ref[...] += part

        @pl.when(k == nk - 1)
        def _():
            finish(acc_ref[...], o_ref)

    a_spec = pl.BlockSpec((tk, tm), lambda i, j, k: (k, i)) if mode == 'tn' else pl.BlockSpec((tm, tk), lambda i, j, k: (i, k))
    b_spec = pl.BlockSpec((tn, tk), lambda i, j, k: (j, k)) if mode == 'nt' else pl.BlockSpec((tk, tn), lambda i, j, k: (k, j))
    in_specs, args = [a_spec, b_spec], [a, b]
    if add is not None:
        in_specs.append(pl.BlockSpec((tm, tn), lambda i, j, k: (i, j)))
        args.append(add)
    return pl.pallas_call(
        body, name=name, grid=(M // tm, N // tn, nk), in_specs=in_specs,
        out_specs=pl.BlockSpec((tm, tn), lambda i, j, k: (i, col(j))),
        out_shape=jax.ShapeDtypeStruct((M, N), out_dtype),
        scratch_shapes=[pltpu.VMEM((tm, tn), f32)] if nk > 1 else [],
        compiler_params=pltpu.CompilerParams(dimension_semantics=("parallel", "parallel", "arbitrary"), vmem_limit_bytes=VMEM_LIMIT),
    )(*args)


def _swiglu(u):
    return _silu(u[:, :D_FF]) * u[:, D_FF:]


def _res_ln(x, f, g, b, s):
    r = ALPHA * x + s * f
    mu = jnp.mean(r, axis=-1, keepdims=True)
    rc = r - mu
    var = jnp.mean(rc * rc, axis=-1, keepdims=True)
    return rc * lax.rsqrt(var + LN_EPS) * g + b


def _ln(x, g, b):
    mu = jnp.mean(x, axis=-1, keepdims=True)
    xc = x - mu
    var = jnp.mean(xc * xc, axis=-1, keepdims=True)
    return xc * lax.rsqrt(var + LN_EPS) * g + b


def _dn_pre(first, xc, xp, ba, cw, hp):
    xp = jnp.where(first, 0.0, xp)
    y = cw[DN_CONV - 1:DN_CONV, :] * xc
    for j in range(DN_CONV - 1):
        y = y + cw[j:j + 1, :] * _shift_rows(xc, xp, DN_CONV - 1 - j)
    c = _silu(y)
    qs, ks, vs, gs, bs = [], [], [], [], []
    nqk = DN_HEADS * DN_DK
    for h in range(DN_HEADS):
        q = c[:, h * DN_DK:(h + 1) * DN_DK]
        k = c[:, nqk + h * DN_DK:nqk + (h + 1) * DN_DK]
        v = c[:, 2 * nqk + h * DN_DK:2 * nqk + (h + 1) * DN_DK]
        qs.append(q * lax.rsqrt(jnp.sum(q * q, axis=-1, keepdims=True) + RMS_EPS))
        ks.append(k * lax.rsqrt(jnp.sum(k * k, axis=-1, keepdims=True) + RMS_EPS))
        vs.append(v)
        beta = jax.nn.sigmoid(_lane_pick(ba, h))
        a_log = _lane_pick(hp[0:1, :], h)
        dt = _lane_pick(hp[1:2, :], h)
        g = -jnp.exp(a_log) * jax.nn.softplus(_lane_pick(ba, DN_HEADS + h) + dt)
        gs.append(jnp.broadcast_to(g, q.shape))
        bs.append(jnp.broadcast_to(beta, q.shape))
    return tuple(jnp.stack(t, axis=0) for t in (qs, ks, vs, gs, bs))


def _wy(q, k, v, gb, bb):
    B, C, _ = q.shape
    ri = lax.broadcasted_iota(i32, (B, C, C), 1)
    ci = lax.broadcasted_iota(i32, (B, C, C), 2)
    tril, strict = ri >= ci, ri > ci
    gc = hdot(tril.astype(f32), gb, 'nn')
    gl = jnp.broadcast_to(_last_row(gc, C), gc.shape)
    col = _lane_head(gc, C)
    decay = jnp.exp(jnp.where(tril, col - jnp.swapaxes(col, 1, 2), NEG_INF))
    qs = q * (DN_DK ** -0.5)
    kb = k * bb
    a = jnp.where(strict, bdot(kb, k, 'nt') * decay, 0.0)
    sol = _tri_solve(a, jnp.concatenate([v * bb, kb * jnp.exp(gc)], axis=-1))
    qk = jnp.where(tril, bdot(qs, k, 'nt') * decay, 0.0)
    kt = k * jnp.exp(gl - gc)
    qd = qs * jnp.exp(gc)
    return qd, qk, sol[..., :DN_DK], sol[..., DN_DK:], kt, jnp.exp(gl)


def _scan_step(S, qd, qk, u, w, kt, egl):
    vn = u - bdot(w, S, 'nn')
    o = bdot(qd, S, 'nn') + bdot(qk, vn, 'nn')
    e2 = jnp.concatenate([egl] * (DN_DK // DN_CHUNK), axis=1)
    return o, S * e2 + bdot(kt, vn, 'tn')


def _dn_post(o, z, nw):
    outs = []
    for h in range(DN_HEADS):
        oh = o[h]
        oh = oh * lax.rsqrt(jnp.mean(oh * oh, axis=-1, keepdims=True) + RMS_EPS) * nw
        outs.append(oh * _silu(z[:, h * DN_DK:(h + 1) * DN_DK]))
    return jnp.concatenate(outs, axis=1)


def _swa(first, q, kc, kp, vc, vp, snk):
    W = q.shape[1]
    G = SWA_HEADS // SWA_KV_HEADS
    r = lax.broadcasted_iota(i32, (G, W, 2 * W), 1)
    c = lax.broadcasted_iota(i32, (G, W, 2 * W), 2)
    mask = (c > W + r - WINDOW) & (c <= W + r) & jnp.logical_or(c >= W, jnp.logical_not(first))
    sink_all = _lane_pick(snk, 0)
    outs = []
    for j in range(SWA_KV_HEADS):
        qj = q[j * G:(j + 1) * G]
        kk = jnp.broadcast_to(jnp.concatenate([kp[j], kc[j]], axis=0)[None], (G, 2 * W, SWA_DH))
        vv = jnp.broadcast_to(jnp.concatenate([vp[j], vc[j]], axis=0)[None], (G, 2 * W, SWA_DH))
        s = jnp.where(mask, bdot(qj, kk, 'nt') * (SWA_DH ** -0.5), NEG_INF)
        sink = sink_all[j * G:(j + 1) * G][:, :, None]
        m = jnp.maximum(jnp.max(s, axis=-1, keepdims=True), sink)
        p = jnp.exp(s - m)
        p = p / (jnp.sum(p, axis=-1, keepdims=True) + jnp.exp(sink - m))
        outs.append(bdot(p, vv, 'nn'))
    return jnp.concatenate(outs, axis=0)


def _xa(q, kv):
    outs = []
    nk = XA_HEADS * XA_DH
    for h in range(XA_HEADS):
        qh = q[:, h * XA_DH:(h + 1) * XA_DH]
        kh = kv[:, h * XA_DH:(h + 1) * XA_DH]
        vh = kv[:, nk + h * XA_DH:nk + (h + 1) * XA_DH]
        s = bdot(qh, kh, 'nt') * (XA_DH ** -0.5)
        m = jnp.max(s, axis=-1, keepdims=True)
        p = jnp.exp(s - m)
        p = p / jnp.sum(p, axis=-1, keepdims=True)
        outs.append(bdot(p, vh, 'nn'))
    return jnp.concatenate(outs, axis=1)


def _merge(o_dn, o_sw, o_xa, gates, wb):
    acc = None
    for n, o in enumerate((o_dn, o_sw, o_xa)):
        t = jax.nn.sigmoid(gates[:, n * D_MODEL:(n + 1) * D_MODEL]) * bdot(o, wb[n], 'nn')
        acc = t if acc is None else acc + t
    return acc


def _row_tile(T, want):
    return _pick(T, tuple(c for c in (1024, 512, 256, 128, 64) if c <= want))


def _res_ln_fwd(name, x, f, g, b, s):
    T = x.shape[0]
    tm = _row_tile(T, 512)

    def fn(i, x, f, g, b):
        h = _res_ln(x, f, g, b, s)
        return h, h

    return _tile_call(name, fn, T // tm, [_rows(x, tm), _rows(f, tm), _full(g), _full(b)],
                      [((T, D_MODEL), f32, (tm, D_MODEL), lambda i: (i, 0)), ((T, D_MODEL), bf16, (tm, D_MODEL), lambda i: (i, 0))])


def _res_ln_bwd(name, x, f, g, b, s, dh):
    T = x.shape[0]
    tm = _row_tile(T, 512)

    def fn(i, x, f, g, b, dh):
        _, vjp = jax.vjp(lambda x, f, g, b: _res_ln(x, f, g, b, s), x, f, g, b)
        return vjp(dh)

    return _tile_call(name, fn, T // tm, [_rows(x, tm), _rows(f, tm), _full(g), _full(b), _rows(dh, tm)],
                      [((T, D_MODEL), f32, (tm, D_MODEL), lambda i: (i, 0)), ((T, D_MODEL), bf16, (tm, D_MODEL), lambda i: (i, 0))],
                      [((1, D_MODEL), f32), ((1, D_MODEL), f32)])


FF_BLK = D_FF // 2


def _pair_gate_up(w):
    d = w.shape[0]
    return jnp.stack([w[:, :D_FF].reshape(d, -1, FF_BLK), w[:, D_FF:].reshape(d, -1, FF_BLK)], axis=2).reshape(d, 2 * D_FF)


def _swiglu_blk(u):
    return _silu(u[:, :FF_BLK]) * u[:, FF_BLK:]


def _mm_spec(tm, tn):
    return dict(compiler_params=pltpu.CompilerParams(dimension_semantics=("parallel", "arbitrary"), vmem_limit_bytes=VMEM_LIMIT))


def _gu_act(name, xb, w_pair):
    T, D = xb.shape
    tm = _pick(T, (_tiles(name, (512,))[0], 512, 256, 128))

    def body(x_ref, w_ref, u_ref, a_ref):
        r = lax.dot_general(x_ref[...], w_ref[...], (((1,), (0,)), ((), ())), preferred_element_type=f32)
        u_ref[...] = r.astype(bf16)
        a_ref[...] = _swiglu_blk(r).astype(bf16)

    return pl.pallas_call(
        body, name=name, grid=(T // tm, D_FF // FF_BLK),
        in_specs=[pl.BlockSpec((tm, D), lambda i, j: (i, 0)), pl.BlockSpec((D, 2 * FF_BLK), lambda i, j: (0, j))],
        out_specs=[pl.BlockSpec((tm, 2 * FF_BLK), lambda i, j: (i, j)), pl.BlockSpec((tm, FF_BLK), lambda i, j: (i, j))],
        out_shape=[jax.ShapeDtypeStruct((T, 2 * D_FF), bf16), jax.ShapeDtypeStruct((T, D_FF), bf16)],
        **_mm_spec(tm, FF_BLK))(xb, w_pair)


def _down_dx_act_bwd(name, df, w_down_t, u):
    T, D = df.shape
    tm = _pick(T, (_tiles(name, (256,))[0], 256, 128))

    def body(df_ref, w_ref, u_ref, du_ref):
        da = lax.dot_general(df_ref[...], w_ref[...], (((1,), (0,)), ((), ())), preferred_element_type=f32)
        _, vjp = jax.vjp(_swiglu_blk, u_ref[...].astype(f32))
        du_ref[...] = vjp(da)[0].astype(bf16)

    return pl.pallas_call(
        body, name=name, grid=(T // tm, D_FF // FF_BLK),
        in_specs=[pl.BlockSpec((tm, D), lambda i, j: (i, 0)), pl.BlockSpec((D, FF_BLK), lambda i, j: (0, j)),
                  pl.BlockSpec((tm, 2 * FF_BLK), lambda i, j: (i, j))],
        out_specs=pl.BlockSpec((tm, 2 * FF_BLK), lambda i, j: (i, j)),
        out_shape=jax.ShapeDtypeStruct((T, 2 * D_FF), bf16),
        **_mm_spec(tm, FF_BLK))(df, w_down_t, u)


def _mm_res_ln(name, a, w, x, g, b, s):
    T, K = a.shape
    tm = _pick(T, (_tiles(name, (512,))[0], 512, 256, 128))

    def body(a_ref, w_ref, x_ref, g_ref, b_ref, f_ref, h_ref, hb_ref):
        f = lax.dot_general(a_ref[...], w_ref[...], (((1,), (0,)), ((), ())), preferred_element_type=f32)
        h = _res_ln(x_ref[...], f, g_ref[...], b_ref[...], s)
        f_ref[...] = f
        h_ref[...] = h
        hb_ref[...] = h.astype(bf16)

    row = pl.BlockSpec((tm, D_MODEL), lambda i: (i, 0))
    vec = pl.BlockSpec((1, D_MODEL), lambda i: (0, 0))
    return pl.pallas_call(
        body, name=name, grid=(T // tm,),
        in_specs=[pl.BlockSpec((tm, K), lambda i: (i, 0)), pl.BlockSpec((K, D_MODEL), lambda i: (0, 0)), row, vec, vec],
        out_specs=[row, row, row],
        out_shape=[jax.ShapeDtypeStruct((T, D_MODEL), f32), jax.ShapeDtypeStruct((T, D_MODEL), f32), jax.ShapeDtypeStruct((T, D_MODEL), bf16)],
        compiler_params=pltpu.CompilerParams(dimension_semantics=("parallel",), vmem_limit_bytes=VMEM_LIMIT))(a, w, x, g, b)


def _hm_spec(tm, w=DN_DK):
    return ((DN_HEADS, tm, w), lambda i: (0, i, 0))


def _prev(i):
    return jnp.maximum(i - 1, 0)


def _dn_front(first, xc, xp, ba, cw, hp):
    tm = xc.shape[0]
    nc = tm // DN_CHUNK
    outs = _wy(*[t.reshape(DN_HEADS * nc, DN_CHUNK, DN_DK) for t in _dn_pre(first, xc, xp, ba, cw, hp)])
    return tuple(o.reshape(DN_HEADS, tm, o.shape[-1]) for o in outs)


def _dn_front_ins(p, pba, cw, hp, tm):
    W3 = 3 * DN_HEADS * DN_DK
    return [(p, (tm, W3), lambda i: (i, 0)), (p, (HALO, W3), lambda i: (jnp.maximum(i * (tm // HALO) - 1, 0), 0)),
            (pba, (tm, LANES), lambda i: (i, 0)), _full(cw), _full(hp)]


def _dn_front_fwd(name, p, pba, cw, hp):
    T = p.shape[0]
    tm = _row_tile(T, 256)
    blk, im = _hm_spec(tm)
    blk2, im2 = _hm_spec(tm, DN_CHUNK)
    full = (DN_HEADS, T, DN_DK)
    outs = [(full, bf16, blk, im), ((DN_HEADS, T, DN_CHUNK), bf16, blk2, im2), (full, f32, blk, im),
            (full, bf16, blk, im), (full, bf16, blk, im), (full, f32, blk, im)]
    return _tile_call(name, lambda i, *a: _dn_front(i == 0, *a), T // tm, _dn_front_ins(p, pba, cw, hp, tm), outs)


def _dn_front_bwd(name, p, pba, cw, hp, cts):
    T = p.shape[0]
    tm = _row_tile(T, 256)
    n = T // tm
    W3 = 3 * DN_HEADS * DN_DK
    blk, im = _hm_spec(tm)
    blk2, im2 = _hm_spec(tm, DN_CHUNK)
    ins = _dn_front_ins(p, pba, cw, hp, tm) + [(c, blk2 if c.shape[-1] == DN_CHUNK else blk, im) for c in cts]

    def fn(i, xc, xp, ba, cw, hp, *cts):
        _, vjp = jax.vjp(lambda *a: _dn_front(i == 0, *a), xc, xp, ba, cw, hp)
        return vjp(tuple(cts))

    return _tile_call(name, fn, n, ins,
                      [((T, W3), bf16, (tm, W3), lambda i: (i, 0)), ((n * HALO, W3), f32, (HALO, W3), lambda i: (i, 0)),
                       ((T, LANES), f32, (tm, LANES), lambda i: (i, 0))],
                      [(tuple(cw.shape), f32), (tuple(hp.shape), f32)])


def _scan_fwd(name, qd, qk, u, w, kt, egl):
    T = qd.shape[1]
    C = DN_CHUNK
    U = _pick(T // C, (SCAN_CHUNKS, 1))
    n = T // (C * U)

    def body(qd_ref, qk_ref, u_ref, w_ref, kt_ref, egl_ref, o_ref, sall_ref, s_ref):
        i = pl.program_id(0)

        @pl.when(i == 0)
        def _():
            s_ref[...] = jnp.zeros(s_ref.shape, f32)

        S = s_ref[...]
        for j in range(U):
            sl = pl.ds(j * C, C)
            sall_ref[:, j] = S
            o, S = _scan_step(S, *[r[:, sl, :] for r in (qd_ref, qk_ref, u_ref, w_ref, kt_ref, egl_ref)])
            o_ref[:, sl, :] = o
        s_ref[...] = S

    blk, im = _hm_spec(C * U)
    blk2, im2 = _hm_spec(C * U, C)
    return pl.pallas_call(
        body, name=name, grid=(n,),
        in_specs=[pl.BlockSpec(blk, im), pl.BlockSpec(blk2, im2)] + [pl.BlockSpec(blk, im)] * 4,
        out_specs=[pl.BlockSpec(blk, im), pl.BlockSpec((DN_HEADS, U, DN_DK, DN_DK), lambda i: (0, i, 0, 0))],
        out_shape=[jax.ShapeDtypeStruct((DN_HEADS, T, DN_DK), f32), jax.ShapeDtypeStruct((DN_HEADS, n * U, DN_DK, DN_DK), f32)],
        scratch_shapes=[pltpu.VMEM((DN_HEADS, DN_DK, DN_DK), f32)],
        compiler_params=pltpu.CompilerParams(dimension_semantics=("arbitrary",), vmem_limit_bytes=VMEM_LIMIT),
    )(qd, qk, u, w, kt, egl)


def _scan_bwd(name, qd, qk, u, w, kt, egl, s_all, do):
    T = qd.shape[1]
    C = DN_CHUNK
    U = _pick(T // C, (SCAN_CHUNKS, 1))
    n = T // (C * U)

    def body(qd_ref, qk_ref, u_ref, w_ref, kt_ref, egl_ref, sall_ref, do_ref,
             dqd_ref, dqk_ref, du_ref, dw_ref, dkt_ref, degl_ref, ds_ref):
        i = pl.program_id(0)

        @pl.when(i == 0)
        def _():
            ds_ref[...] = jnp.zeros(ds_ref.shape, f32)

        dS = ds_ref[...]
        for j in reversed(range(U)):
            sl = pl.ds(j * C, C)
            args = [r[:, sl, :].astype(f32) for r in (qd_ref, qk_ref, u_ref, w_ref, kt_ref, egl_ref)]
            _, vjp = jax.vjp(_scan_step, sall_ref[:, j], *args)
            dS, *cts = vjp((do_ref[:, sl, :], dS))
            for r, v in zip((dqd_ref, dqk_ref, du_ref, dw_ref, dkt_ref, degl_ref), cts):
                r[:, sl, :] = v
        ds_ref[...] = dS

    blk = (DN_HEADS, C * U, DN_DK)
    blk2 = (DN_HEADS, C * U, C)
    rim = lambda i: (0, n - 1 - i, 0)
    sp, sp2 = pl.BlockSpec(blk, rim), pl.BlockSpec(blk2, rim)
    full, full2 = jax.ShapeDtypeStruct((DN_HEADS, T, DN_DK), f32), jax.ShapeDtypeStruct((DN_HEADS, T, C), f32)
    return pl.pallas_call(
        body, name=name, grid=(n,),
        in_specs=[sp, sp2, sp, sp, sp, sp, pl.BlockSpec((DN_HEADS, U, DN_DK, DN_DK), lambda i: (0, n - 1 - i, 0, 0)), sp],
        out_specs=[sp, sp2, sp, sp, sp, sp],
        out_shape=[full, full2, full, full, full, full],
        scratch_shapes=[pltpu.VMEM((DN_HEADS, DN_DK, DN_DK), f32)],
        compiler_params=pltpu.CompilerParams(dimension_semantics=("arbitrary",), vmem_limit_bytes=VMEM_LIMIT),
    )(qd, qk, u, w, kt, egl, s_all, do)


def _dn_post_fwd(name, o, p, nw):
    T = p.shape[0]
    tm = _row_tile(T, 512)
    blk, im = _hm_spec(tm)
    W = DN_HEADS * DN_DK
    return _tile_call(name, lambda i, o, z, nw: _dn_post(o, z, nw), T // tm,
                      [(o, blk, im), (p, (tm, W), lambda i: (i, 1536 // W)), _full(nw)],
                      [((T, W), f32, (tm, W), lambda i: (i, 0))])[0]


def _dn_post_bwd(name, o, p, nw, dout):
    T = p.shape[0]
    tm = _row_tile(T, 512)
    blk, im = _hm_spec(tm)
    W = DN_HEADS * DN_DK

    def fn(i, o, z, nw, dout):
        _, vjp = jax.vjp(_dn_post, o, z, nw)
        return vjp(dout)

    return _tile_call(name, fn, T // tm,
                      [(o, blk, im), (p, (tm, W), lambda i: (i, 1536 // W)), _full(nw), _rows(dout, tm)],
                      [((DN_HEADS, T, DN_DK), f32, blk, im), ((T, W), f32, (tm, W), lambda i: (i, 0))],
                      [((1, DN_DK), f32)])


def _swa_tile(T, big):
    return _pick(T, (2 * WINDOW, WINDOW)) if big else WINDOW


@functools.partial(jax.custom_vjp, nondiff_argnums=(1,))
def _heads_split(x, nh):
    return jnp.stack([x[:, SWA_DH * h:SWA_DH * (h + 1)] for h in range(nh)], axis=0)


@jax.custom_vjp
def _heads_merge(y):
    return jnp.concatenate([y[h] for h in range(y.shape[0])], axis=1)


_heads_split.defvjp(lambda x, nh: (_heads_split(x, nh), None), lambda nh, _, g: (_heads_merge(g),))
_heads_merge.defvjp(lambda y: (_heads_merge(y), None), lambda _, g: (_heads_split(g, g.shape[1] // SWA_DH),))


def _swa_rows(first, qp, kc, kp, vc, vp, snk):
    kv = [_heads_split(t, SWA_KV_HEADS) for t in (kc, kp, vc, vp)]
    return _heads_merge(_swa(first, _heads_split(qp, SWA_HEADS), *kv, snk))


def _swa_ins(p, snk, W):
    qw, kw = SWA_HEADS * SWA_DH, SWA_KV_HEADS * SWA_DH
    return [(p, (W, qw), lambda i: (i, 2048 // qw)),
            (p, (W, kw), lambda i: (i, 6144 // kw)), (p, (W, kw), lambda i: (_prev(i), 6144 // kw)),
            (p, (W, kw), lambda i: (i, 6272 // kw)), (p, (W, kw), lambda i: (_prev(i), 6272 // kw)), _full(snk)]


def _swa_fwd(name, p, snk):
    T = p.shape[0]
    W = _swa_tile(T, True)
    qw = SWA_HEADS * SWA_DH
    return _tile_call(name, lambda i, *a: _swa_rows(i == 0, *a), T // W, _swa_ins(p, snk, W),
                      [((T, qw), f32, (W, qw), lambda i: (i, 0))])[0]


def _swa_bwd(name, p, snk, do):
    T = p.shape[0]
    W = _swa_tile(T, False)
    qw, kw = SWA_HEADS * SWA_DH, SWA_KV_HEADS * SWA_DH

    def fn(i, qp, k_c, k_p, v_c, v_p, snk, do):
        _, vjp = jax.vjp(lambda *a: _swa_rows(i == 0, *a), qp, k_c, k_p, v_c, v_p, snk)
        return vjp(do)

    kv_out = ((T, kw), f32, (W, kw), lambda i: (i, 0))
    return _tile_call(name, fn, T // W, _swa_ins(p, snk, W) + [(do, (W, qw), lambda i: (i, 0))],
                      [((T, qw), bf16, (W, qw), lambda i: (i, 0)), kv_out, kv_out, kv_out, kv_out], [(tuple(snk.shape), f32)])


def _xa_fwd(name, p, kv):
    T = p.shape[0]
    tm = _row_tile(T, 512)
    W = XA_HEADS * XA_DH
    return _tile_call(name, lambda i, q, kv: _xa(q, kv), T // tm, [(p, (tm, W), lambda i: (i, 2560 // W)), _full(kv)],
                      [((T, W), f32, (tm, W), lambda i: (i, 0))])[0]


def _xa_bwd(name, p, kv, do):
    T = p.shape[0]
    tm = _row_tile(T, 512)
    W = XA_HEADS * XA_DH

    def fn(i, q, kv, do):
        _, vjp = jax.vjp(_xa, q, kv)
        return vjp(do)

    return _tile_call(name, fn, T // tm, [(p, (tm, W), lambda i: (i, 2560 // W)), _full(kv), _rows(do, tm)],
                      [((T, W), bf16, (tm, W), lambda i: (i, 0))], [(tuple(kv.shape), f32)])


def _merge_fwd(name, o_dn, o_sw, o_xa, p, wb):
    T = p.shape[0]
    tm = _row_tile(T, 512)
    GW = N_BRANCH * D_MODEL
    return _tile_call(name, lambda i, a, b, c, g, w: _merge(a, b, c, g, w.astype(f32)), T // tm,
                      [_rows(o_dn, tm), _rows(o_sw, tm), _rows(o_xa, tm), (p, (tm, GW), lambda i: (i, 1)), _full(wb)],
                      [((T, D_MODEL), bf16, (tm, D_MODEL), lambda i: (i, 0))])[0]


def _merge_bwd(name, o_dn, o_sw, o_xa, p, wb, dm):
    T = p.shape[0]
    tm = _row_tile(T, 512)
    GW = N_BRANCH * D_MODEL

    def fn(i, a, b, c, g, w, dm):
        _, vjp = jax.vjp(_merge, a, b, c, g, w.astype(f32))
        return vjp(dm)

    bo = ((T, BRANCH_W), f32, (tm, BRANCH_W), lambda i: (i, 0))
    return _tile_call(name, fn, T // tm,
                      [_rows(o_dn, tm), _rows(o_sw, tm), _rows(o_xa, tm), (p, (tm, GW), lambda i: (i, 1)), _full(wb), _rows(dm, tm)],
                      [bo, bo, bo, ((T, GW), bf16, (tm, GW), lambda i: (i, 0))], [(tuple(wb.shape), f32)])


def _assemble_dp(name, dxc, dxp, dz, dswq, dxaq, dgates, dkc, dkp, dvc, dvp, dba):
    T = dz.shape[0]
    tm = _row_tile(T, 256)
    n = T // tm
    r = tm // WINDOW
    nb = T // WINDOW

    def fn(i, dxc, dxp, dz, dswq, dxaq, dgates, dkc, dvc, dba, *nxt):
        halo = jnp.where(i == n - 1, 0.0, dxp)
        dqkv = dxc + jnp.concatenate([jnp.zeros((tm - HALO, dxc.shape[1]), f32), halo], axis=0)
        shifted = [jnp.concatenate([jnp.where(i * r + 1 + b <= nb - 1, blk, 0.0) for b, blk in enumerate(half)], axis=0)
                   for half in (nxt[:r], nxt[r:])]
        parts = [dqkv, dz, dswq, dxaq, dgates, dkc + shifted[0], dvc + shifted[1], dba, jnp.zeros((tm, LANES), f32)]
        return jnp.concatenate(parts, axis=1)

    ins = [_rows(dxc, tm), (dxp, (HALO, dxp.shape[1]), lambda i: (jnp.minimum(i + 1, n - 1), 0))]
    ins += [_rows(t, tm) for t in (dz, dswq, dxaq, dgates, dkc, dvc, dba)]
    for t in (dkp, dvp):
        ins += [(t, (WINDOW, t.shape[1]), lambda i, b=b: (jnp.minimum(i * r + 1 + b, nb - 1), 0)) for b in range(r)]
    return _tile_call(name, fn, n, ins, [((T, D_INP), bf16, (tm, D_INP), lambda i: (i, 0))])[0]


def _loss_grad(name, y, tgt):
    T = y.shape[0]
    tm = _row_tile(T, 512)

    def fn(i, y, t):
        e = y - t
        part = 0.5 * jnp.sum(jnp.mean(e * e, axis=-1, keepdims=True), axis=0, keepdims=True)
        return e * (1.0 / D_MODEL), jnp.broadcast_to(part, (1, LANES))

    return _tile_call(name, fn, T // tm, [_rows(y, tm), _rows(tgt, tm)],
                      [((T, D_MODEL), f32, (tm, D_MODEL), lambda i: (i, 0))], [((1, LANES), f32)])


def _mem_ln_fwd(name, mem, g, b):
    M = mem.shape[0]
    return _tile_call(name, lambda i, m, g, b: (_ln(m, g, b),), 1, [_full(mem), _full(g), _full(b)],
                      [((M, D_MODEL), bf16, (M, D_MODEL), lambda i: (0, 0))])[0]


def _mem_ln_bwd(name, mem, g, b, dmn):
    def fn(i, m, g, b, d):
        _, vjp = jax.vjp(lambda g, b: _ln(m, g, b), g, b)
        return vjp(d)

    return _tile_call(name, fn, 1, [_full(mem), _full(g), _full(b), _full(dmn)], [], [((1, D_MODEL), f32), ((1, D_MODEL), f32)])


W_IN_RANGES = ((0, 1536), (1544, 2056), (2056, 2568), (2824, 3336), (3336, 6408), (2568, 2696), (2696, 2824), (1536, 1544))
W_IN_OFFS = tuple(sum(b - a for a, b in W_IN_RANGES[:k]) for k in range(len(W_IN_RANGES)))


def _pad_w_in(cols):
    parts = [cols(a, b) for a, b in W_IN_RANGES]
    return jnp.concatenate(parts + [jnp.zeros((parts[0].shape[0], D_INP - D_IN), parts[0].dtype)], axis=1)


def _unpad_dw_in(d, lo=0, hi=D_IN):
    parts = []
    for (a, b), off in sorted(zip(W_IN_RANGES, W_IN_OFFS)):
        a2, b2 = max(a, lo), min(b, hi)
        if a2 < b2:
            parts.append(d[:, off + a2 - a:off + b2 - a])
    return jnp.concatenate(parts, axis=1)


def _shard_cols(shards, l):
    n = shards[0].shape[-1]

    def cols(a, b):
        parts = [shards[s][l][:, max(a, s * n) - s * n:min(b, (s + 1) * n) - s * n] for s in range(N_SHARD)
                 if max(a, s * n) < min(b, (s + 1) * n)]
        return parts[0] if len(parts) == 1 else jnp.concatenate(parts, axis=1)

    return cols


def _lane_row(v, rows=8):
    out = jnp.zeros((rows, LANES), f32)
    return out.at[0, :v.shape[0]].set(v)


def _layer_fwd(l, x, xb, mem_nb, W):
    n = lambda s: f"l{l}_{s}"
    sv = {}
    u1, a1 = _gu_act(n("ffn1_gu"), xb, W['ffn1_w_gu_p'])
    f1, h1, h1b = _mm_res_ln(n("ffn1_down"), a1, W['ffn1_w_down'], x, W['ln_g'][0:1], W['ln_b'][0:1], 0.5)
    p = _mm(n("w_in"), h1b, W['w_inp'], 'nn', out_dtype=bf16)
    pba = _mm(n("w_in_ba"), h1b, W['w_ba'], 'nn')
    qd, qk, u, w, kt, egl = _dn_front_fwd(n("dn_front"), p, pba, W['conv_w'], W['hp'])
    o_raw, s_all = _scan_fwd(n("dn_scan"), qd, qk, u, w, kt, egl)
    o_dn = _dn_post_fwd(n("dn_post"), o_raw, p, W['norm_w'])
    o_sw = _swa_fwd(n("swa"), p, W['snk'])
    kv = _mm(n("mem_kv"), mem_nb, W['w_mem_kv'], 'nn')
    o_xa = _xa_fwd(n("xa"), p, kv)
    merged = _merge_fwd(n("merge"), o_dn, o_sw, o_xa, p, W['w_branch'])
    mo, h2, h2b = _mm_res_ln(n("w_out"), merged, W['w_out'], h1, W['ln_g'][1:2], W['ln_b'][1:2], 1.0)
    u2, a2 = _gu_act(n("ffn2_gu"), h2b, W['ffn2_w_gu_p'])
    f2, h3, h3b = _mm_res_ln(n("ffn2_down"), a2, W['ffn2_w_down'], h2, W['ln_g'][2:3], W['ln_b'][2:3], 0.5)
    sv = dict(x=x, xb=xb, u1=u1, a1=a1, f1=f1, h1=h1, h1b=h1b, p=p, pba=pba, wy=(qd, qk, u, w, kt, egl),
              o_raw=o_raw, s_all=s_all, o_dn=o_dn, o_sw=o_sw, kv=kv, o_xa=o_xa, merged=merged,
              mo=mo, h2=h2, h2b=h2b, u2=u2, a2=a2, f2=f2)
    return h3, h3b, sv


def _ffn_bwd(n, tag, x, xb, u, a, f, g, b, w_gu_t, w_down_t, dh):
    dx_a, df, dg, db = _res_ln_bwd(n(f"{tag}_ln_bwd"), x, f, g, b, 0.5, dh)
    d_down = _mm(n(f"{tag}_down_dw"), a, df, 'tn')
    du = _down_dx_act_bwd(n(f"{tag}_down_dx"), df, w_down_t, u)
    d_gu = _mm(n(f"{tag}_gu_dw"), xb, du, 'tn', out_cols=(FF_BLK, lambda j: (j % 2) * (D_FF // FF_BLK) + j // 2))
    dx = _mm(n(f"{tag}_gu_dx"), du, w_gu_t, 'nn', add=dx_a)
    return dx, d_gu, d_down, dg, db


def _layer_bwd(l, sv, mem_nb, W, dh3, dmem_acc):
    n = lambda s: f"l{l}_{s}"
    G = {}
    dh2, G['ffn2_w_gu'], G['ffn2_w_down'], dg2, db2 = _ffn_bwd(
        n, "ffn2", sv['h2'], sv['h2b'], sv['u2'], sv['a2'], sv['f2'], W['ln_g'][2:3], W['ln_b'][2:3], W['ffn2_w_gu_t'], W['ffn2_w_down_t'], dh3)
    dh1_a, dmo, dg1, db1 = _res_ln_bwd(n("ln1_bwd"), sv['h1'], sv['mo'], W['ln_g'][1:2], W['ln_b'][1:2], 1.0, dh2)
    G['w_out'] = _mm(n("w_out_dw"), sv['merged'], dmo, 'tn')
    dmerged = _mm(n("w_out_dx"), dmo, W['w_out_t'], 'nn')
    p = sv['p']
    do_dn, do_sw, do_xa, dgates, G['w_branch'] = _merge_bwd(n("merge_bwd"), sv['o_dn'], sv['o_sw'], sv['o_xa'], p, W['w_branch'], dmerged)
    dxaq, dkv = _xa_bwd(n("xa_bwd"), p, sv['kv'], do_xa)
    dkv = dkv.astype(bf16)
    G['w_mem_kv'] = _mm(n("mem_kv_dw"), mem_nb, dkv, 'tn')
    dmem_n = _mm(n("mem_kv_dx"), dkv, W['w_mem_kv_t'], 'nn', add=dmem_acc)
    dswq, dkc, dkp, dvc, dvp, dsnk = _swa_bwd(n("swa_bwd"), p, W['snk'], do_sw)
    do_raw, dz, dnw = _dn_post_bwd(n("dn_post_bwd"), sv['o_raw'], p, W['norm_w'], do_dn)
    cts = _scan_bwd(n("dn_scan_bwd"), *sv['wy'], sv['s_all'], do_raw)
    dxc, dxp, dba, dcw, dhp = _dn_front_bwd(n("dn_front_bwd"), p, sv['pba'], W['conv_w'], W['hp'], cts)
    dp = _assemble_dp(n("dp"), dxc, dxp, dz, dswq, dxaq, dgates, dkc, dkp, dvc, dvp, dba)
    G['w_in_p'] = _mm(n("w_in_dw"), sv['h1b'], dp, 'tn')
    dh1 = _mm(n("w_in_dx"), dp, W['w_inp_t'], 'nn', add=dh1_a)
    dx, G['ffn1_w_gu'], G['ffn1_w_down'], dg0, db0 = _ffn_bwd(
        n, "ffn1", sv['x'], sv['xb'], sv['u1'], sv['a1'], sv['f1'], W['ln_g'][0:1], W['ln_b'][0:1], W['ffn1_w_gu_t'], W['ffn1_w_down_t'], dh1)
    G['ln_g'] = jnp.concatenate([dg0, dg1, dg2], axis=0)
    G['ln_b'] = jnp.concatenate([db0, db1, db2], axis=0)
    G['dn_conv_w'] = dcw
    G['dn_a_log'] = dhp[0, :DN_HEADS]
    G['dn_dt_bias'] = dhp[1, :DN_HEADS]
    G['dn_norm_w'] = dnw[0]
    G['swa_sinks'] = dsnk[:, 0]
    return dx, dmem_n, G


def _local_step(x, mem, tgt, Wf):
    mem_g, mem_b = Wf['mem_ln_g'][None, :], Wf['mem_ln_b'][None, :]
    mem_nb = _mem_ln_fwd("mem_ln", mem, mem_g, mem_b)

    def paired(k, l):
        return Wf[k + '_p'][l] if k + '_p' in Wf else _pair_gate_up(Wf[k][l])

    def w_in_cols(l):
        return _shard_cols(Wf['w_in_sh'], l) if 'w_in_sh' in Wf else (lambda a, b: Wf['w_in'][l][:, a:b])

    layers = []
    for l in range(DEPTH):
        layers.append(dict(
            ln_g=Wf['ln_g'][l], ln_b=Wf['ln_b'][l], ffn1_w_gu_p=paired('ffn1_w_gu', l), ffn1_w_down=Wf['ffn1_w_down'][l],
            w_inp=_pad_w_in(w_in_cols(l)), conv_w=Wf['dn_conv_w'][l],
            hp=jnp.zeros((8, LANES), f32).at[0, :DN_HEADS].set(Wf['dn_a_log'][l]).at[1, :DN_HEADS].set(Wf['dn_dt_bias'][l]),
            norm_w=Wf['dn_norm_w'][l][None, :], snk=jnp.broadcast_to(Wf['swa_sinks'][l][:, None], (SWA_HEADS, LANES)),
            w_mem_kv=Wf['w_mem_kv'][l], w_branch=Wf['w_branch'][l], w_out=Wf['w_out'][l],
            ffn2_w_gu_p=paired('ffn2_w_gu', l), ffn2_w_down=Wf['ffn2_w_down'][l]))
        layers[l]['w_ba'] = layers[l]['w_inp'][:, 6400:6400 + LANES]
        for k in ('ffn1_w_gu_p', 'ffn1_w_down', 'w_inp', 'w_mem_kv', 'w_out', 'ffn2_w_gu_p', 'ffn2_w_down'):
            layers[l][k.replace('_p', '') + '_t'] = layers[l][k].T
    h, hb = x, x.astype(bf16)
    saved = []
    for l in range(DEPTH):
        h, hb, sv = _layer_fwd(l, h, hb, mem_nb, layers[l])
        saved.append(sv)
    dh, loss_row = _loss_grad("loss", h, tgt)
    grads = [None] * DEPTH
    dmem_n = None
    for l in reversed(range(DEPTH)):
        dh, dmem_n, grads[l] = _layer_bwd(l, saved[l], mem_nb, layers[l], dh, dmem_n)
    dmg, dmb = _mem_ln_bwd("mem_ln_bwd", mem, mem_g, mem_b, dmem_n)
    G = {k: jnp.stack([grads[l][k] for l in range(DEPTH)], axis=0) for k in grads[0]}
    G['mem_ln_g'], G['mem_ln_b'] = dmg[0], dmb[0]
    return loss_row, dh, G


SEG_ALIGN = 2048


def _round_up(n, m):
    return (n + m - 1) // m * m


def _layout(names, sizes, mult):
    off, table = 0, {}
    for nm in names:
        table[nm] = (off, sizes[nm])
        off += _round_up(sizes[nm], SEG_ALIGN)
    return table, _round_up(off // LANES, mult)


def _pack(table, rows, flat):
    names = list(table)
    lead = flat[names[0]].shape[:-1]
    parts, pos = [], 0
    for nm in names:
        off, size = table[nm]
        if off > pos:
            parts.append(jnp.zeros(lead + (off - pos,), flat[nm].dtype))
        parts.append(flat[nm])
        pos = off + size
    total = rows * LANES
    if total > pos:
        parts.append(jnp.zeros(lead + (total - pos,), parts[-1].dtype))
    return jnp.concatenate(parts, axis=-1).reshape(lead + (rows, LANES))


def _unpack(table, packed, nm):
    off, size = table[nm]
    flat = packed.reshape(packed.shape[:-2] + (-1,))
    return flat[..., off:off + size]


def _split_shards(full, ax):
    shp = full.shape
    t = full.reshape(shp[:ax] + (N_SHARD, shp[ax] // N_SHARD) + shp[ax + 1:])
    return jnp.moveaxis(t, ax, 0)


def _join_shards(sh4, ax):
    return jnp.concatenate([sh4[s] for s in range(N_SHARD)], axis=ax)


ANY = pl.BlockSpec(memory_space=pl.ANY)


def _me():
    return lax.axis_index("x"), lax.axis_index("y"), lax.axis_index("c")


def _comm_call(name, body, arrays, out_shapes, n_sem):
    n = len(arrays)
    scratch = [pltpu.SemaphoreType.DMA((n, n_sem)), pltpu.SemaphoreType.DMA((n, n_sem))]
    return pl.pallas_call(
        body, name=name, out_shape=out_shapes, in_specs=[ANY] * n, out_specs=[ANY] * n, scratch_shapes=scratch,
        compiler_params=pltpu.CompilerParams(has_side_effects=True),
    )(*arrays)


def _all_gather(name, xs):
    n = len(xs)

    def split(shape):
        ax = next(i for i, d in enumerate(shape) if d % 2 == 0)
        return ax, shape[ax] // 2

    def body(*refs):
        x_refs, out_refs = refs[:n], refs[n:2 * n]
        send_sems, recv_sems = refs[2 * n:]
        mx, my, mc = _me()
        me_s = 2 * mx + my
        across_x, across_y, diag = (1 - mx, my), (mx, 1 - my), (1 - mx, 1 - my)
        sid = lambda chip: 2 * chip[0] + chip[1]

        def part(ref, a, which):
            if which is None:
                return ref
            ax, size = split(xs[a].shape[1:])
            return ref.at[(slice(None),) * ax + (pl.ds(which * size, size),)]

        def copy(a, k, shard, half, which, to, src=None):
            dst = part(out_refs[a].at[shard, half], a, which)
            return pltpu.make_async_remote_copy(src_ref=dst if src is None else part(src, a, which), dst_ref=dst,
                                                send_sem=send_sems.at[a, k], recv_sem=recv_sems.at[a, k], device_id=to, device_id_type=MESH)

        sibling, here = (mx, my, 1 - mc), (mx, my, mc)
        sent = []
        for a in range(n):
            sent += [copy(a, 0, me_s, mc, None, (*across_x, mc), src=x_refs[a].at[mc]),
                     copy(a, 1, me_s, mc, None, (*across_y, mc), src=x_refs[a].at[mc])]
        for cp in sent:
            cp.start()
        landing = [(0, across_x, None, across_y), (1, across_y, None, across_x), (2, diag, 0, None), (3, diag, 1, None)]
        for k, origin, which, relay_to in landing:
            for a in range(n):
                copy(a, k, sid(origin), mc, which, here).wait_recv()
                if relay_to is not None:
                    sent.append(copy(a, 2 + k, sid(origin), mc, k, (*relay_to, mc)))
                    sent[-1].start()
                sent.append(copy(a, 4 + k, sid(origin), mc, which, sibling))
                sent[-1].start()
        for k, origin, which, _ in landing:
            for a in range(n):
                copy(a, 4 + k, sid(origin), 1 - mc, which, here).wait_recv()
        for cp in sent:
            cp.wait_send()

    return _comm_call(name, body, xs, [jax.ShapeDtypeStruct((N_SHARD,) + x.shape, x.dtype) for x in xs], 8)


def _pair_exchange(name, items):
    n = len(items)

    def body(*refs):
        src_refs, dst_refs = refs[:n], refs[n:2 * n]
        send_sems, recv_sems = refs[2 * n:]
        mx, my, mc = _me()
        cps = [pltpu.make_async_remote_copy(src_ref=src_refs[a].at[s, 1 - mc], dst_ref=dst_refs[a].at[s], send_sem=send_sems.at[a, s],
                                            recv_sem=recv_sems.at[a, s], device_id=(mx, my, 1 - mc), device_id_type=MESH)
               for a in range(n) for s in range(N_SHARD)]
        for cp in cps:
            cp.start()
        for cp in cps:
            cp.wait()

    return _comm_call(name, body, items, [jax.ShapeDtypeStruct((N_SHARD,) + t.shape[2:], t.dtype) for t in items], N_SHARD)


def _chip_exchange(name, parts):
    n = len(parts)

    def body(*refs):
        p_refs, dst_refs = refs[:n], refs[n:2 * n]
        send_sems, recv_sems = refs[2 * n:]
        mx, my, mc = _me()
        chips = [(1 - mx, my), (mx, 1 - my), (1 - mx, 1 - my)]
        cps = [pltpu.make_async_remote_copy(src_ref=p_refs[a].at[2 * cx + cy], dst_ref=dst_refs[a].at[j], send_sem=send_sems.at[a, j],
                                            recv_sem=recv_sems.at[a, j], device_id=(cx, cy, mc), device_id_type=MESH)
               for a in range(n) for j, (cx, cy) in enumerate(chips)]
        for cp in cps:
            cp.start()
        for cp in cps:
            cp.wait()

    return _comm_call(name, body, parts, [jax.ShapeDtypeStruct((3,) + t.shape[1:], t.dtype) for t in parts], 3)


def _pair_swap(name, reds):
    n = len(reds)

    def body(*refs):
        r_refs, out_refs = refs[:n], refs[n:2 * n]
        send_sems, recv_sems = refs[2 * n:]
        mx, my, mc = _me()
        cps = [pltpu.make_async_remote_copy(src_ref=r_refs[a], dst_ref=out_refs[a], send_sem=send_sems.at[a, 0],
                                            recv_sem=recv_sems.at[a, 0], device_id=(mx, my, 1 - mc), device_id_type=MESH)
               for a in range(n)]
        for cp in cps:
            cp.start()
        for cp in cps:
            cp.wait()

    return _comm_call(name, body, reds, [jax.ShapeDtypeStruct(t.shape, t.dtype) for t in reds], 1)


EW_BLOCK_BYTES = 1 << 20


def _ew_call(name, fn, ins, n_out, out_dtype=f32):
    shape = ins[0].shape
    last = shape[-1]
    flat = [a.reshape(-1, last) for a in ins]
    R = flat[0].shape[0]
    cands = tuple(c for c in (4096, 2048, 1024, 512, 256, 128, 64, 32, 16, 8) if c * last * 4 <= EW_BLOCK_BYTES)
    tr = _pick(R, cands)
    res = _tile_call(name, lambda i, *a: fn(*a), R // tr, [(a, (tr, last), lambda i: (i, 0)) for a in flat],
                     [((R, last), out_dtype, (tr, last), lambda i: (i, 0))] * n_out)
    return [r.reshape(shape) for r in res]


def _adamw(g, w, m, v):
    m = B1 * m + (1.0 - B1) * g
    v = B2 * v + (1.0 - B2) * jnp.square(g)
    m_hat = m / (1.0 - B1 ** STEP)
    v_hat = v / (1.0 - B2 ** STEP)
    return -LR * (m_hat / (jnp.sqrt(v_hat) + EPS) + WD * w), m, v


def _adamw_call(name, mine, theirs, w, m, v, mc1):
    shape, last = w.shape, w.shape[-1]
    g2 = [t.reshape(-1, last) for t in (mine, theirs)]
    w3 = [t.reshape(2, -1, last) for t in (w, m, v)]
    R = g2[0].shape[0]
    tr = _pick(R, tuple(c for c in (4096, 2048, 1024, 512, 256, 128, 64, 32, 16, 8) if c * last * 4 <= EW_BLOCK_BYTES))

    def body(mc_ref, mine_ref, theirs_ref, w_ref, m_ref, v_ref, g_out, d_out, m_out, v_out):
        g = jnp.where(pl.program_id(0) == mc_ref[0], mine_ref[...], theirs_ref[...])
        d, nm, nv = _adamw(g, w_ref[0], m_ref[0], v_ref[0])
        g_out[0], d_out[0], m_out[0], v_out[0] = g, d, nm, nv

    half = pl.BlockSpec((tr, last), lambda h, i: (i, 0))
    full = pl.BlockSpec((1, tr, last), lambda h, i: (h, i, 0))
    res = pl.pallas_call(
        body, name=name, grid=(2, R // tr),
        in_specs=[pl.BlockSpec(memory_space=pltpu.SMEM), half, half, full, full, full], out_specs=[full] * 4,
        out_shape=[jax.ShapeDtypeStruct((2, R, last), f32)] * 4,
        compiler_params=pltpu.CompilerParams(dimension_semantics=("arbitrary", "arbitrary"), vmem_limit_bytes=VMEM_LIMIT),
    )(mc1, *g2, *w3)
    return tuple(r.reshape(shape) for r in res)


def kernel(x, mem, mem_ln_g, mem_ln_b, ln_g, ln_b, ffn1_w_gu, ffn1_w_down, w_in, dn_conv_w, dn_a_log, dn_dt_bias, dn_norm_w, swa_sinks, w_mem_kv, w_branch, w_out, ffn2_w_gu, ffn2_w_down, loss_target, m_mem_ln_g, m_mem_ln_b, m_ln_g, m_ln_b, m_ffn1_w_gu, m_ffn1_w_down, m_w_in, m_dn_conv_w, m_dn_a_log, m_dn_dt_bias, m_dn_norm_w, m_swa_sinks, m_w_mem_kv, m_w_branch, m_w_out, m_ffn2_w_gu, m_ffn2_w_down, v_mem_ln_g, v_mem_ln_b, v_ln_g, v_ln_b, v_ffn1_w_gu, v_ffn1_w_down, v_w_in, v_dn_conv_w, v_dn_a_log, v_dn_dt_bias, v_dn_norm_w, v_swa_sinks, v_w_mem_kv, v_w_branch, v_w_out, v_ffn2_w_gu, v_ffn2_w_down):
    args = dict(locals())
    Ws = {n: args[n] for n in WEIGHTS}
    Ms = {n: args["m_" + n] for n in WEIGHTS}
    Vs = {n: args["v_" + n] for n in WEIGHTS}
    mc = lax.axis_index("c")
    my_s = 2 * lax.axis_index("x") + lax.axis_index("y")
    small = [n for n in WEIGHTS if n not in MATRICES]

    ag_table, ag_rows = _layout(SMALL_SHARDED, {n: Ws[n].size for n in SMALL_SHARDED}, 16)
    ag_small = _pack(ag_table, ag_rows, {n: Ws[n].reshape(-1) for n in SMALL_SHARDED}).reshape(2, ag_rows // 2, LANES)
    local = [Ws[n].astype(bf16) for n in MATRICES] + [ag_small]
    gathered = _all_gather("all_gather_w", local)
    gathered = [jnp.stack([jnp.where(my_s == s, loc, g[s]) for s in range(N_SHARD)], axis=0) for loc, g in zip(local, gathered)]
    Wf = {n: _join_shards(g, SHARD_AXIS[n]) for n, g in zip(MATRICES, gathered) if not n.endswith('w_gu') and n != 'w_in'}
    for n, g in zip(MATRICES, gathered):
        if n.endswith('w_gu'):
            assert g.shape[-1] == FF_BLK
            Wf[n + '_p'] = jnp.concatenate([g[0], g[2], g[1], g[3]], axis=2)
        if n == 'w_in':
            Wf['w_in_sh'] = g
    g_small = gathered[-1].reshape(N_SHARD, ag_rows, LANES)
    for n in SMALL_SHARDED:
        Wf[n] = _join_shards(_unpack(ag_table, g_small, n).reshape((N_SHARD,) + Ws[n].shape), SHARD_AXIS[n])
    for n in WEIGHTS:
        if SHARD_AXIS[n] is None:
            Wf[n] = Ws[n]

    loss_row, dx, G = _local_step(x[0], mem[0], loss_target[0], Wf)

    table, rows = _layout(small + ['loss'], {**{n: Ws[n].size for n in small}, 'loss': 1}, 16)
    gflat = {n: (jnp.broadcast_to(G[n].reshape(1, -1), (N_SHARD, G[n].size)) if SHARD_AXIS[n] is None
                 else _split_shards(G[n], SHARD_AXIS[n]).reshape(N_SHARD, -1)) for n in small}
    gflat['loss'] = jnp.broadcast_to(loss_row[:, :1], (N_SHARD, 1))
    n_in = Ws['w_in'].shape[-1]
    items =[_split_shards(G[n], SHARD_AXIS[n]) if n != 'w_in' else
             jnp.stack([jnp.stack([_unpad_dw_in(G['w_in_p'][l], s * n_in, (s + 1) * n_in) for l in range(DEPTH)]) for s in range(N_SHARD)])
             for n in MATRICES] + [_pack(table, rows, gflat).reshape(N_SHARD, 2, rows // 2, LANES)]
    tags = MATRICES + ['small']
    got = _pair_exchange("rs_pair", items)
    keep = [lax.dynamic_index_in_dim(a, mc, axis=1, keepdims=False) for a in items]
    wire = [bf16] * len(MATRICES) + [f32]
    part = [_ew_call(f"rs_add_pair_{t}", lambda a, b: a + b, [k, g], 1, out_dtype=dt)[0] for t, k, g, dt in zip(tags, keep, got, wire)]
    others = _chip_exchange("rs_chips", part)
    own = lambda a: lax.dynamic_index_in_dim(a, my_s, axis=0, keepdims=False)
    red = [_ew_call(f"rs_add_chips_{t}", lambda k, g, fx, fy, fxy: ((k + g) + fy) + (fx + fxy), [own(k), own(g), o[0], o[1], o[2]], 1)[0]
           for t, k, g, o in zip(tags, keep, got, others)]
    theirs = _pair_swap("rs_swap", red)

    mc1 = mc.astype(i32).reshape(1)
    outs = {}
    for n, a, b in zip(MATRICES, red, theirs):
        outs[n] = _adamw_call(f"adamw_{n}", a, b, Ws[n], Ms[n], Vs[n], mc1)
    fill = {'loss': jnp.zeros((1,), f32)}
    packs = [_pack(table, rows, {**{n: src[n].reshape(-1) for n in small}, **fill}).reshape(2, rows // 2, LANES) for src in (Ws, Ms, Vs)]
    small_out = [p.reshape(rows, LANES) for p in _adamw_call("adamw_small", red[-1], theirs[-1], *packs, mc1)]
    for n in small:
        outs[n] = tuple(_unpack(table, p, n).reshape(Ws[n].shape) for p in small_out)
    loss = _unpack(table, small_out[0], 'loss').reshape(())
    return (loss, dx[None], *[outs[n][k] for k in range(4) for n in WEIGHTS])
```

```python
import functools

import jax
import jax.numpy as jnp
from jax import lax
from jax.experimental import pallas as pl
from jax.experimental.pallas import tpu as pltpu

f32, bf16, i32 = jnp.float32, jnp.bfloat16, jnp.int32
HI = lax.Precision.HIGHEST
MESH = pl.DeviceIdType.MESH

D_MODEL = 1024
DEPTH = 2
DN_HEADS, DN_DK, DN_CONV, DN_CHUNK = 4, 128, 4, 64
SWA_HEADS, SWA_KV_HEADS, SWA_DH, WINDOW = 8, 2, 64, 128
XA_HEADS, XA_DH = 4, 128
D_FF = 2816
N_BRANCH, BRANCH_W = 3, 512
ALPHA = (2 * DEPTH) ** 0.25
LN_EPS, RMS_EPS, NEG_INF = 1e-5, 1e-6, -1e30
D_IN = 6408
D_INP = 6656
LR, B1, B2, EPS, WD, STEP = 0.001, 0.9, 0.999, 1e-08, 0.01, 10

LANES = 128
VMEM_LIMIT = 56 << 20
N_SHARD = 4
SCAN_CHUNKS = 4
HALO = 16
MM_TILES = {
    'ffn1_gu_dx': (512, 1024, 5632), 'ffn2_gu_dx': (512, 1024, 5632), 'w_in_dx': (512, 1024, 6656), 'w_in': (2048, 512, 1024),
    'w_in_dw': (1024, 512, 4096), 'ffn1_down_dw': (1408, 512, 4096), 'ffn2_down_dw': (1408, 512, 4096),
    'ffn1_down_dx': (1024,), 'ffn2_down_dx': (1024,), 'ffn1_down': (1024,), 'ffn2_down': (1024,), 'w_out': (1024,),
    'ffn1_gu': (1024,), 'ffn2_gu': (1024,),
    'w_out_dx': (512, 1024, 1024),
}


def _tiles(name, default=None):
    return MM_TILES.get(name, MM_TILES.get(name.split('_', 1)[-1], default))

WEIGHTS = ['mem_ln_g', 'mem_ln_b', 'ln_g', 'ln_b', 'ffn1_w_gu', 'ffn1_w_down', 'w_in', 'dn_conv_w', 'dn_a_log',
           'dn_dt_bias', 'dn_norm_w', 'swa_sinks', 'w_mem_kv', 'w_branch', 'w_out', 'ffn2_w_gu', 'ffn2_w_down']
SHARD_AXIS = {'mem_ln_g': None, 'mem_ln_b': None, 'ln_g': 2, 'ln_b': 2, 'ffn1_w_gu': 2, 'ffn1_w_down': 1, 'w_in': 2,
              'dn_conv_w': 2, 'dn_a_log': None, 'dn_dt_bias': None, 'dn_norm_w': None, 'swa_sinks': None,
              'w_mem_kv': 1, 'w_branch': 3, 'w_out': 1, 'ffn2_w_gu': 2, 'ffn2_w_down': 1}
MATRICES = ['ffn1_w_gu', 'ffn1_w_down', 'w_in', 'w_mem_kv', 'w_branch', 'w_out', 'ffn2_w_gu', 'ffn2_w_down']
SMALL_SHARDED = ['ln_g', 'ln_b', 'dn_conv_w']


def _dg(a, b, mode, hi):
    nb = a.ndim - 2
    bd = tuple(range(nb))
    ca = nb if mode == 'tn' else nb + 1
    cb = nb + 1 if mode == 'nt' else nb
    dims = (((ca,), (cb,)), (bd, bd))
    dot = lambda x, y: lax.dot_general(x, y, dims, preferred_element_type=f32)
    a_hi, b_hi = a.astype(bf16), b.astype(bf16)
    if not hi:
        return dot(a_hi, b_hi)
    a_lo = (a.astype(f32) - a_hi.astype(f32)).astype(bf16)
    b_lo = (b.astype(f32) - b_hi.astype(f32)).astype(bf16)
    return dot(a_hi, b_hi) + (dot(a_hi, b_lo) + dot(a_lo, b_hi))


@functools.partial(jax.custom_vjp, nondiff_argnums=(2, 3))
def _dot(a, b, mode, hi):
    return _dg(a, b, mode, hi)


def _dot_fwd(a, b, mode, hi):
    return _dg(a, b, mode, hi), (a, b)


def _dot_bwd(mode, hi, res, g):
    a, b = res
    if mode == 'nn':
        da, db = _dg(g, b, 'nt', hi), _dg(a, g, 'tn', hi)
    elif mode == 'nt':
        da, db = _dg(g, b, 'nn', hi), _dg(g, a, 'tn', hi)
    else:
        da, db = _dg(b, g, 'nt', hi), _dg(a, g, 'nn', hi)
    return da.astype(a.dtype), db.astype(b.dtype)


_dot.defvjp(_dot_fwd, _dot_bwd)


def bdot(a, b, mode):
    return _dot(a, b, mode, False)


def hdot(a, b, mode):
    return _dot(a, b, mode, True)


def _shift_rows_impl(x, tail, s):
    r = pltpu.roll(x, s, 0)
    rows = lax.broadcasted_iota(i32, tail.shape, 0)
    top = jnp.where(rows >= s, r[:HALO], pltpu.roll(tail, s, 0))
    return jnp.concatenate([top, r[HALO:]], axis=0)


@functools.partial(jax.custom_vjp, nondiff_argnums=(2,))
def _shift_rows(x, tail, s):
    return _shift_rows_impl(x, tail, s)


def _shift_rows_fwd(x, tail, s):
    return _shift_rows_impl(x, tail, s), None


def _shift_rows_bwd(s, _, g):
    n = g.shape[0]
    r = pltpu.roll(g, n - s, 0)
    rows = lax.broadcasted_iota(i32, (HALO, g.shape[1]), 0)
    last = r[n - HALO:]
    keep = rows < HALO - s
    dx = jnp.concatenate([r[:n - HALO], jnp.where(keep, last, 0.0)], axis=0)
    return dx, jnp.where(keep, 0.0, pltpu.roll(g[:HALO], HALO - s, 0))


_shift_rows.defvjp(_shift_rows_fwd, _shift_rows_bwd)


def _lane_pick(x, lane):
    idx = lax.broadcasted_iota(i32, x.shape, x.ndim - 1)
    return jnp.sum(jnp.where(idx == lane, x, 0.0), axis=-1, keepdims=True)


def _silu(x):
    return x * jax.nn.sigmoid(x)


def _tri_inv(a):
    C = a.shape[-1]
    eye = (lax.broadcasted_iota(i32, a.shape, 1) == lax.broadcasted_iota(i32, a.shape, 2)).astype(f32)
    p = -a
    x = eye + p
    for _ in range((C - 1).bit_length() - 1):
        p = _dg(p, p, 'nn', True)
        x = x + _dg(x, p, 'nn', True)
    return x


@jax.custom_vjp
def _tri_solve(a, rhs):
    return _dg(_tri_inv(a), rhs, 'nn', True)


def _tri_solve_fwd(a, rhs):
    tinv = _tri_inv(a)
    sol = _dg(tinv, rhs, 'nn', True)
    return sol, (tinv, sol)


def _tri_solve_bwd(res, g):
    tinv, sol = res
    d_rhs = _dg(tinv, g, 'tn', True)
    return -_dg(d_rhs, sol, 'nt', True), d_rhs


_tri_solve.defvjp(_tri_solve_fwd, _tri_solve_bwd)


@functools.partial(jax.custom_vjp, nondiff_argnums=(1,))
def _lane_head(x, n):
    return x[:, :, :n]


def _lane_head_fwd(x, n):
    return x[:, :, :n], None


def _lane_head_bwd(n, _, g):
    s = jnp.sum(g, axis=-1, keepdims=True) * (1.0 / LANES)
    return (jnp.broadcast_to(s, g.shape[:-1] + (LANES,)),)


_lane_head.defvjp(_lane_head_fwd, _lane_head_bwd)


@functools.partial(jax.custom_vjp, nondiff_argnums=(1,))
def _last_row(x, c):
    return x[:, c - 1:, :]


def _last_row_fwd(x, c):
    return x[:, c - 1:, :], None


def _last_row_bwd(c, _, g):
    shape = (g.shape[0], c, g.shape[2])
    rows = lax.broadcasted_iota(i32, shape, 1)
    return (jnp.where(rows == c - 1, jnp.broadcast_to(g, shape), 0.0),)


_last_row.defvjp(_last_row_fwd, _last_row_bwd)


def _full(a):
    nd = a.ndim
    return (a, tuple(a.shape), lambda i, _nd=nd: (0,) * _nd)


def _rows(a, tm, col=0, width=None):
    width = a.shape[1] if width is None else width
    return (a, (tm, width), lambda i, _c=col: (i, _c))


def _tile_call(name, fn, n, ins, outs, accs=()):
    n_in, n_out, n_acc = len(ins), len(outs), len(accs)

    def body(*refs):
        i = pl.program_id(0)
        res = fn(i, *[r[...].astype(f32) for r in refs[:n_in]])
        if not isinstance(res, (tuple, list)):
            res = (res,)
        assert len(res) == n_out + n_acc, (name, len(res), n_out, n_acc)
        for r, v in zip(refs[n_in:n_in + n_out], res[:n_out]):
            r[...] = v.astype(r.dtype)
        if n_acc:
            acc_refs = refs[n_in + n_out:]

            @pl.when(i == 0)
            def _():
                for r in acc_refs:
                    r[...] = jnp.zeros(r.shape, r.dtype)

            for r, v in zip(acc_refs, res[n_out:]):
                r[...] += v.astype(r.dtype)

    out_shape = [jax.ShapeDtypeStruct(s, d) for s, d, _, _ in outs] + [jax.ShapeDtypeStruct(s, d) for s, d in accs]
    out_specs = [pl.BlockSpec(b, m) for _, _, b, m in outs]
    out_specs += [pl.BlockSpec(tuple(s), lambda i, _nd=len(s): (0,) * _nd) for s, _ in accs]
    res = pl.pallas_call(
        body, name=name, grid=(n,),
        in_specs=[pl.BlockSpec(b, m) for _, b, m in ins],
        out_specs=out_specs, out_shape=out_shape,
        compiler_params=pltpu.CompilerParams(dimension_semantics=("arbitrary",), vmem_limit_bytes=VMEM_LIMIT),
    )(*[a for a, _, _ in ins])
    return res


def _pick(n, cands):
    for c in cands:
        if n % c == 0:
            return c
    return n


def _mm(name, a, b, mode, out_dtype=f32, add=None, out_cols=None):
    if mode == 'tn':
        K, M = a.shape
    else:
        M, K = a.shape
    N = b.shape[0] if mode == 'nt' else b.shape[1]
    tm = _pick(M, (1024, 1408, 512, 256, 128))
    tn = _pick(N, (512, 1408, 256, 128)) if out_cols is None else out_cols[0]
    tuned = _tiles(name)
    if tuned is not None and (M % tuned[0] or N % tuned[1] or K % tuned[2]):
        tuned = None
    if tuned is not None:
        tm, tn = tuned[:2]
    col = (lambda j: j) if out_cols is None else out_cols[1]
    tk = K if K <= 3328 else _pick(K, (3328, 2816, 2048, 1024, 512, 256, 128))
    if tuned is not None:
        tk = tuned[2]
    nk = K // tk
    ca = 0 if mode == 'tn' else 1
    cb = 1 if mode == 'nt' else 0
    dims = (((ca,), (cb,)), ((), ()))

    def body(*refs):
        a_ref, b_ref = refs[0], refs[1]
        add_ref = refs[2] if add is not None else None
        part = lax.dot_general(a_ref[...].astype(bf16), b_ref[...].astype(bf16), dims, preferred_element_type=f32)

        def finish(r, o_ref):
            if add_ref is not None:
                r = r + add_ref[...].astype(f32)
            o_ref[...] = r.astype(o_ref.dtype)

        if nk == 1:
            finish(part, refs[-1])
            return
        o_ref, acc_ref = refs[-2], refs[-1]
        k = pl.program_id(2)

        @pl.when(k == 0)
        def _():
            acc_ref[...] = part

        @pl.when(k > 0)
        def _():
            acc_ref[...] += part

        @pl.when(k == nk - 1)
        def _():
            finish(acc_ref[...], o_ref)

    a_spec = pl.BlockSpec((tk, tm), lambda i, j, k: (k, i)) if mode == 'tn' else pl.BlockSpec((tm, tk), lambda i, j, k: (i, k))
    b_spec = pl.BlockSpec((tn, tk), lambda i, j, k: (j, k)) if mode == 'nt' else pl.BlockSpec((tk, tn), lambda i, j, k: (k, j))
    in_specs, args = [a_spec, b_spec], [a, b]
    if add is not None:
        in_specs.append(pl.BlockSpec((tm, tn), lambda i, j, k: (i, j)))
        args.append(add)
    return pl.pallas_call(
        body, name=name, grid=(M // tm, N // tn, nk), in_specs=in_specs,
        out_specs=pl.BlockSpec((tm, tn), lambda i, j, k: (i, col(j))),
        out_shape=jax.ShapeDtypeStruct((M, N), out_dtype),
        scratch_shapes=[pltpu.VMEM((tm, tn), f32)] if nk > 1 else [],
        compiler_params=pltpu.CompilerParams(dimension_semantics=("parallel", "parallel", "arbitrary"), vmem_limit_bytes=VMEM_LIMIT),
    )(*args)


def _swiglu(u):
    return _silu(u[:, :D_FF]) * u[:, D_FF:]


def _res_ln(x, f, g, b, s):
    r = ALPHA * x + s * f
    mu = jnp.mean(r, axis=-1, keepdims=True)
    rc = r - mu
    var = jnp.mean(rc * rc, axis=-1, keepdims=True)
    return rc * lax.rsqrt(var + LN_EPS) * g + b


def _ln(x, g, b):
    mu = jnp.mean(x, axis=-1, keepdims=True)
    xc = x - mu
    var = jnp.mean(xc * xc, axis=-1, keepdims=True)
    return xc * lax.rsqrt(var + LN_EPS) * g + b


def _dn_pre(first, xc, xp, ba, cw, hp):
    xp = jnp.where(first, 0.0, xp)
    y = cw[DN_CONV - 1:DN_CONV, :] * xc
    for j in range(DN_CONV - 1):
        y = y + cw[j:j + 1, :] * _shift_rows(xc, xp, DN_CONV - 1 - j)
    c = _silu(y)
    qs, ks, vs, gs, bs = [], [], [], [], []
    nqk = DN_HEADS * DN_DK
    for h in range(DN_HEADS):
        q = c[:, h * DN_DK:(h + 1) * DN_DK]
        k = c[:, nqk + h * DN_DK:nqk + (h + 1) * DN_DK]
        v = c[:, 2 * nqk + h * DN_DK:2 * nqk + (h + 1) * DN_DK]
        qs.append(q * lax.rsqrt(jnp.sum(q * q, axis=-1, keepdims=True) + RMS_EPS))
        ks.append(k * lax.rsqrt(jnp.sum(k * k, axis=-1, keepdims=True) + RMS_EPS))
        vs.append(v)
        beta = jax.nn.sigmoid(_lane_pick(ba, h))
        a_log = _lane_pick(hp[0:1, :], h)
        dt = _lane_pick(hp[1:2, :], h)
        g = -jnp.exp(a_log) * jax.nn.softplus(_lane_pick(ba, DN_HEADS + h) + dt)
        gs.append(jnp.broadcast_to(g, q.shape))
        bs.append(jnp.broadcast_to(beta, q.shape))
    return tuple(jnp.stack(t, axis=0) for t in (qs, ks, vs, gs, bs))


def _wy(q, k, v, gb, bb):
    B, C, _ = q.shape
    ri = lax.broadcasted_iota(i32, (B, C, C), 1)
    ci = lax.broadcasted_iota(i32, (B, C, C), 2)
    tril, strict = ri >= ci, ri > ci
    gc = hdot(tril.astype(f32), gb, 'nn')
    gl = jnp.broadcast_to(_last_row(gc, C), gc.shape)
    col = _lane_head(gc, C)
    decay = jnp.exp(jnp.where(tril, col - jnp.swapaxes(col, 1, 2), NEG_INF))
    qs = q * (DN_DK ** -0.5)
    kb = k * bb
    a = jnp.where(strict, bdot(kb, k, 'nt') * decay, 0.0)
    sol = _tri_solve(a, jnp.concatenate([v * bb, kb * jnp.exp(gc)], axis=-1))
    qk = jnp.where(tril, bdot(qs, k, 'nt') * decay, 0.0)
    kt = k * jnp.exp(gl - gc)
    qd = qs * jnp.exp(gc)
    return qd, qk, sol[..., :DN_DK], sol[..., DN_DK:], kt, jnp.exp(gl)


def _scan_step(S, qd, qk, u, w, kt, egl):
    vn = u - bdot(w, S, 'nn')
    o = bdot(qd, S, 'nn') + bdot(qk, vn, 'nn')
    e2 = jnp.concatenate([egl] * (DN_DK // DN_CHUNK), axis=1)
    return o, S * e2 + bdot(kt, vn, 'tn')


def _dn_post(o, z, nw):
    outs = []
    for h in range(DN_HEADS):
        oh = o[h]
        oh = oh * lax.rsqrt(jnp.mean(oh * oh, axis=-1, keepdims=True) + RMS_EPS) * nw
        outs.append(oh * _silu(z[:, h * DN_DK:(h + 1) * DN_DK]))
    return jnp.concatenate(outs, axis=1)


def _swa(first, q, kc, kp, vc, vp, snk):
    W = q.shape[1]
    G = SWA_HEADS // SWA_KV_HEADS
    r = lax.broadcasted_iota(i32, (G, W, 2 * W), 1)
    c = lax.broadcasted_iota(i32, (G, W, 2 * W), 2)
    mask = (c > W + r - WINDOW) & (c <= W + r) & jnp.logical_or(c >= W, jnp.logical_not(first))
    sink_all = _lane_pick(snk, 0)
    outs = []
    for j in range(SWA_KV_HEADS):
        qj = q[j * G:(j + 1) * G]
        kk = jnp.broadcast_to(jnp.concatenate([kp[j], kc[j]], axis=0)[None], (G, 2 * W, SWA_DH))
        vv = jnp.broadcast_to(jnp.concatenate([vp[j], vc[j]], axis=0)[None], (G, 2 * W, SWA_DH))
        s = jnp.where(mask, bdot(qj, kk, 'nt') * (SWA_DH ** -0.5), NEG_INF)
        sink = sink_all[j * G:(j + 1) * G][:, :, None]
        m = jnp.maximum(jnp.max(s, axis=-1, keepdims=True), sink)
        p = jnp.exp(s - m)
        p = p / (jnp.sum(p, axis=-1, keepdims=True) + jnp.exp(sink - m))
        outs.append(bdot(p, vv, 'nn'))
    return jnp.concatenate(outs, axis=0)


def _xa(q, kv):
    outs = []
    nk = XA_HEADS * XA_DH
    for h in range(XA_HEADS):
        qh = q[:, h * XA_DH:(h + 1) * XA_DH]
        kh = kv[:, h * XA_DH:(h + 1) * XA_DH]
        vh = kv[:, nk + h * XA_DH:nk + (h + 1) * XA_DH]
        s = bdot(qh, kh, 'nt') * (XA_DH ** -0.5)
        m = jnp.max(s, axis=-1, keepdims=True)
        p = jnp.exp(s - m)
        p = p / jnp.sum(p, axis=-1, keepdims=True)
        outs.append(bdot(p, vh, 'nn'))
    return jnp.concatenate(outs, axis=1)


def _merge(o_dn, o_sw, o_xa, gates, wb):
    acc = None
    for n, o in enumerate((o_dn, o_sw, o_xa)):
        t = jax.nn.sigmoid(gates[:, n * D_MODEL:(n + 1) * D_MODEL]) * bdot(o, wb[n], 'nn')
        acc = t if acc is None else acc + t
    return acc


def _row_tile(T, want):
    return _pick(T, tuple(c for c in (1024, 512, 256, 128, 64) if c <= want))


def _res_ln_fwd(name, x, f, g, b, s):
    T = x.shape[0]
    tm = _row_tile(T, 512)

    def fn(i, x, f, g, b):
        h = _res_ln(x, f, g, b, s)
        return h, h

    return _tile_call(name, fn, T // tm, [_rows(x, tm), _rows(f, tm), _full(g), _full(b)],
                      [((T, D_MODEL), f32, (tm, D_MODEL), lambda i: (i, 0)), ((T, D_MODEL), bf16, (tm, D_MODEL), lambda i: (i, 0))])


def _res_ln_bwd(name, x, f, g, b, s, dh):
    T = x.shape[0]
    tm = _row_tile(T, 512)

    def fn(i, x, f, g, b, dh):
        _, vjp = jax.vjp(lambda x, f, g, b: _res_ln(x, f, g, b, s), x, f, g, b)
        return vjp(dh)

    return _tile_call(name, fn, T // tm, [_rows(x, tm), _rows(f, tm), _full(g), _full(b), _rows(dh, tm)],
                      [((T, D_MODEL), f32, (tm, D_MODEL), lambda i: (i, 0)), ((T, D_MODEL), bf16, (tm, D_MODEL), lambda i: (i, 0))],
                      [((1, D_MODEL), f32), ((1, D_MODEL), f32)])


FF_BLK = D_FF // 2


def _pair_gate_up(w):
    d = w.shape[0]
    return jnp.stack([w[:, :D_FF].reshape(d, -1, FF_BLK), w[:, D_FF:].reshape(d, -1, FF_BLK)], axis=2).reshape(d, 2 * D_FF)


def _swiglu_blk(u):
    return _silu(u[:, :FF_BLK]) * u[:, FF_BLK:]


def _mm_spec(tm, tn):
    return dict(compiler_params=pltpu.CompilerParams(dimension_semantics=("parallel", "arbitrary"), vmem_limit_bytes=VMEM_LIMIT))


def _gu_act(name, xb, w_pair):
    T, D = xb.shape
    tm = _pick(T, (_tiles(name, (512,))[0], 512, 256, 128))

    def body(x_ref, w_ref, u_ref, a_ref):
        r = lax.dot_general(x_ref[...], w_ref[...], (((1,), (0,)), ((), ())), preferred_element_type=f32)
        u_ref[...] = r.astype(bf16)
        a_ref[...] = _swiglu_blk(r).astype(bf16)

    return pl.pallas_call(
        body, name=name, grid=(T // tm, D_FF // FF_BLK),
        in_specs=[pl.BlockSpec((tm, D), lambda i, j: (i, 0)), pl.BlockSpec((D, 2 * FF_BLK), lambda i, j: (0, j))],
        out_specs=[pl.BlockSpec((tm, 2 * FF_BLK), lambda i, j: (i, j)), pl.BlockSpec((tm, FF_BLK), lambda i, j: (i, j))],
        out_shape=[jax.ShapeDtypeStruct((T, 2 * D_FF), bf16), jax.ShapeDtypeStruct((T, D_FF), bf16)],
        **_mm_spec(tm, FF_BLK))(xb, w_pair)


def _down_dx_act_bwd(name, df, w_down_t, u):
    T, D = df.shape
    tm = _pick(T, (_tiles(name, (256,))[0], 256, 128))

    def body(df_ref, w_ref, u_ref, du_ref):
        da = lax.dot_general(df_ref[...], w_ref[...], (((1,), (0,)), ((), ())), preferred_element_type=f32)
        _, vjp = jax.vjp(_swiglu_blk, u_ref[...].astype(f32))
        du_ref[...] = vjp(da)[0].astype(bf16)

    return pl.pallas_call(
        body, name=name, grid=(T // tm, D_FF // FF_BLK),
        in_specs=[pl.BlockSpec((tm, D), lambda i, j: (i, 0)), pl.BlockSpec((D, FF_BLK), lambda i, j: (0, j)),
                  pl.BlockSpec((tm, 2 * FF_BLK), lambda i, j: (i, j))],
        out_specs=pl.BlockSpec((tm, 2 * FF_BLK), lambda i, j: (i, j)),
        out_shape=jax.ShapeDtypeStruct((T, 2 * D_FF), bf16),
        **_mm_spec(tm, FF_BLK))(df, w_down_t, u)


def _mm_res_ln(name, a, w, x, g, b, s):
    T, K = a.shape
    tm = _pick(T, (_tiles(name, (512,))[0], 512, 256, 128))

    def body(a_ref, w_ref, x_ref, g_ref, b_ref, f_ref, h_ref, hb_ref):
        f = lax.dot_general(a_ref[...], w_ref[...], (((1,), (0,)), ((), ())), preferred_element_type=f32)
        h = _res_ln(x_ref[...], f, g_ref[...], b_ref[...], s)
        f_ref[...] = f.astype(bf16)
        h_ref[...] = h
        hb_ref[...] = h.astype(bf16)

    row = pl.BlockSpec((tm, D_MODEL), lambda i: (i, 0))
    vec = pl.BlockSpec((1, D_MODEL), lambda i: (0, 0))
    return pl.pallas_call(
        body, name=name, grid=(T // tm,),
        in_specs=[pl.BlockSpec((tm, K), lambda i: (i, 0)), pl.BlockSpec((K, D_MODEL), lambda i: (0, 0)), row, vec, vec],
        out_specs=[row, row, row],
        out_shape=[jax.ShapeDtypeStruct((T, D_MODEL), bf16), jax.ShapeDtypeStruct((T, D_MODEL), f32), jax.ShapeDtypeStruct((T, D_MODEL), bf16)],
        compiler_params=pltpu.CompilerParams(dimension_semantics=("parallel",), vmem_limit_bytes=VMEM_LIMIT))(a, w, x, g, b)


def _hm_spec(tm, w=DN_DK):
    return ((DN_HEADS, tm, w), lambda i: (0, i, 0))


def _prev(i):
    return jnp.maximum(i - 1, 0)


def _dn_front(first, xc, xp, ba, cw, hp):
    tm = xc.shape[0]
    nc = tm // DN_CHUNK
    outs = _wy(*[t.reshape(DN_HEADS * nc, DN_CHUNK, DN_DK) for t in _dn_pre(first, xc, xp, ba, cw, hp)])
    return tuple(o.reshape(DN_HEADS, tm, o.shape[-1]) for o in outs)


def _dn_front_ins(p, pba, cw, hp, tm):
    W3 = 3 * DN_HEADS * DN_DK
    return [(p, (tm, W3), lambda i: (i, 0)), (p, (HALO, W3), lambda i: (jnp.maximum(i * (tm // HALO) - 1, 0), 0)),
            (pba, (tm, LANES), lambda i: (i, 0)), _full(cw), _full(hp)]


def _dn_front_fwd(name, p, pba, cw, hp):
    T = p.shape[0]
    tm = _row_tile(T, 256)
    blk, im = _hm_spec(tm)
    blk2, im2 = _hm_spec(tm, DN_CHUNK)
    full = (DN_HEADS, T, DN_DK)
    outs = [(full, bf16, blk, im), ((DN_HEADS, T, DN_CHUNK), bf16, blk2, im2), (full, f32, blk, im),
            (full, bf16, blk, im), (full, bf16, blk, im), (full, f32, blk, im)]
    return _tile_call(name, lambda i, *a: _dn_front(i == 0, *a), T // tm, _dn_front_ins(p, pba, cw, hp, tm), outs)


def _dn_front_bwd(name, p, pba, cw, hp, cts):
    T = p.shape[0]
    tm = _row_tile(T, 256)
    n = T // tm
    W3 = 3 * DN_HEADS * DN_DK
    blk, im = _hm_spec(tm)
    blk2, im2 = _hm_spec(tm, DN_CHUNK)
    ins = _dn_front_ins(p, pba, cw, hp, tm) + [(c, blk2 if c.shape[-1] == DN_CHUNK else blk, im) for c in cts]

    def fn(i, xc, xp, ba, cw, hp, *cts):
        _, vjp = jax.vjp(lambda *a: _dn_front(i == 0, *a), xc, xp, ba, cw, hp)
        return vjp(tuple(cts))

    return _tile_call(name, fn, n, ins,
                      [((T, W3), bf16, (tm, W3), lambda i: (i, 0)), ((n * HALO, W3), f32, (HALO, W3), lambda i: (i, 0)),
                       ((T, LANES), f32, (tm, LANES), lambda i: (i, 0))],
                      [(tuple(cw.shape), f32), (tuple(hp.shape), f32)])


def _scan_fwd(name, qd, qk, u, w, kt, egl):
    T = qd.shape[1]
    C = DN_CHUNK
    U = _pick(T // C, (SCAN_CHUNKS, 1))
    n = T // (C * U)

    def body(qd_ref, qk_ref, u_ref, w_ref, kt_ref, egl_ref, o_ref, sall_ref, s_ref):
        i = pl.program_id(0)

        @pl.when(i == 0)
        def _():
            s_ref[...] = jnp.zeros(s_ref.shape, f32)

        S = s_ref[...]
        for j in range(U):
            sl = pl.ds(j * C, C)
            sall_ref[:, j] = S
            o, S = _scan_step(S, *[r[:, sl, :] for r in (qd_ref, qk_ref, u_ref, w_ref, kt_ref, egl_ref)])
            o_ref[:, sl, :] = o
        s_ref[...] = S

    blk, im = _hm_spec(C * U)
    blk2, im2 = _hm_spec(C * U, C)
    return pl.pallas_call(
        body, name=name, grid=(n,),
        in_specs=[pl.BlockSpec(blk, im), pl.BlockSpec(blk2, im2)] + [pl.BlockSpec(blk, im)] * 4,
        out_specs=[pl.BlockSpec(blk, im), pl.BlockSpec((DN_HEADS, U, DN_DK, DN_DK), lambda i: (0, i, 0, 0))],
        out_shape=[jax.ShapeDtypeStruct((DN_HEADS, T, DN_DK), f32), jax.ShapeDtypeStruct((DN_HEADS, n * U, DN_DK, DN_DK), f32)],
        scratch_shapes=[pltpu.VMEM((DN_HEADS, DN_DK, DN_DK), f32)],
        compiler_params=pltpu.CompilerParams(dimension_semantics=("arbitrary",), vmem_limit_bytes=VMEM_LIMIT),
    )(qd, qk, u, w, kt, egl)


def _scan_bwd(name, qd, qk, u, w, kt, egl, s_all, do):
    T = qd.shape[1]
    C = DN_CHUNK
    U = _pick(T // C, (SCAN_CHUNKS, 1))
    n = T // (C * U)

    def body(qd_ref, qk_ref, u_ref, w_ref, kt_ref, egl_ref, sall_ref, do_ref,
             dqd_ref, dqk_ref, du_ref, dw_ref, dkt_ref, degl_ref, ds_ref):
        i = pl.program_id(0)

        @pl.when(i == 0)
        def _():
            ds_ref[...] = jnp.zeros(ds_ref.shape, f32)

        dS = ds_ref[...]
        for j in reversed(range(U)):
            sl = pl.ds(j * C, C)
            args = [r[:, sl, :].astype(f32) for r in (qd_ref, qk_ref, u_ref, w_ref, kt_ref, egl_ref)]
            _, vjp = jax.vjp(_scan_step, sall_ref[:, j], *args)
            dS, *cts = vjp((do_ref[:, sl, :], dS))
            for r, v in zip((dqd_ref, dqk_ref, du_ref, dw_ref, dkt_ref, degl_ref), cts):
                r[:, sl, :] = v
        ds_ref[...] = dS

    blk = (DN_HEADS, C * U, DN_DK)
    blk2 = (DN_HEADS, C * U, C)
    rim = lambda i: (0, n - 1 - i, 0)
    sp, sp2 = pl.BlockSpec(blk, rim), pl.BlockSpec(blk2, rim)
    full, full2 = jax.ShapeDtypeStruct((DN_HEADS, T, DN_DK), f32), jax.ShapeDtypeStruct((DN_HEADS, T, C), f32)
    return pl.pallas_call(
        body, name=name, grid=(n,),
        in_specs=[sp, sp2, sp, sp, sp, sp, pl.BlockSpec((DN_HEADS, U, DN_DK, DN_DK), lambda i: (0, n - 1 - i, 0, 0)), sp],
        out_specs=[sp, sp2, sp, sp, sp, sp],
        out_shape=[full, full2, full, full, full, full],
        scratch_shapes=[pltpu.VMEM((DN_HEADS, DN_DK, DN_DK), f32)],
        compiler_params=pltpu.CompilerParams(dimension_semantics=("arbitrary",), vmem_limit_bytes=VMEM_LIMIT),
    )(qd, qk, u, w, kt, egl, s_all, do)


def _dn_post_fwd(name, o, p, nw):
    T = p.shape[0]
    tm = _row_tile(T, 512)
    blk, im = _hm_spec(tm)
    W = DN_HEADS * DN_DK
    return _tile_call(name, lambda i, o, z, nw: _dn_post(o, z, nw), T // tm,
                      [(o, blk, im), (p, (tm, W), lambda i: (i, 1536 // W)), _full(nw)],
                      [((T, W), f32, (tm, W), lambda i: (i, 0))])[0]


def _dn_post_bwd(name, o, p, nw, dout):
    T = p.shape[0]
    tm = _row_tile(T, 512)
    blk, im = _hm_spec(tm)
    W = DN_HEADS * DN_DK

    def fn(i, o, z, nw, dout):
        _, vjp = jax.vjp(_dn_post, o, z, nw)
        return vjp(dout)

    return _tile_call(name, fn, T // tm,
                      [(o, blk, im), (p, (tm, W), lambda i: (i, 1536 // W)), _full(nw), _rows(dout, tm)],
                      [((DN_HEADS, T, DN_DK), f32, blk, im), ((T, W), f32, (tm, W), lambda i: (i, 0))],
                      [((1, DN_DK), f32)])


def _swa_tile(T, big):
    return _pick(T, (2 * WINDOW, WINDOW)) if big else WINDOW


@functools.partial(jax.custom_vjp, nondiff_argnums=(1,))
def _heads_split(x, nh):
    return jnp.stack([x[:, SWA_DH * h:SWA_DH * (h + 1)] for h in range(nh)], axis=0)


@jax.custom_vjp
def _heads_merge(y):
    return jnp.concatenate([y[h] for h in range(y.shape[0])], axis=1)


_heads_split.defvjp(lambda x, nh: (_heads_split(x, nh), None), lambda nh, _, g: (_heads_merge(g),))
_heads_merge.defvjp(lambda y: (_heads_merge(y), None), lambda _, g: (_heads_split(g, g.shape[1] // SWA_DH),))


def _swa_rows(first, qp, kc, kp, vc, vp, snk):
    kv = [_heads_split(t, SWA_KV_HEADS) for t in (kc, kp, vc, vp)]
    return _heads_merge(_swa(first, _heads_split(qp, SWA_HEADS), *kv, snk))


def _swa_ins(p, snk, W):
    qw, kw = SWA_HEADS * SWA_DH, SWA_KV_HEADS * SWA_DH
    return [(p, (W, qw), lambda i: (i, 2048 // qw)),
            (p, (W, kw), lambda i: (i, 6144 // kw)), (p, (W, kw), lambda i: (_prev(i), 6144 // kw)),
            (p, (W, kw), lambda i: (i, 6272 // kw)), (p, (W, kw), lambda i: (_prev(i), 6272 // kw)), _full(snk)]


def _swa_fwd(name, p, snk):
    T = p.shape[0]
    W = _swa_tile(T, True)
    qw = SWA_HEADS * SWA_DH
    return _tile_call(name, lambda i, *a: _swa_rows(i == 0, *a), T // W, _swa_ins(p, snk, W),
                      [((T, qw), f32, (W, qw), lambda i: (i, 0))])[0]


def _swa_bwd(name, p, snk, do):
    T = p.shape[0]
    W = _swa_tile(T, False)
    qw, kw = SWA_HEADS * SWA_DH, SWA_KV_HEADS * SWA_DH

    def fn(i, qp, k_c, k_p, v_c, v_p, snk, do):
        _, vjp = jax.vjp(lambda *a: _swa_rows(i == 0, *a), qp, k_c, k_p, v_c, v_p, snk)
        return vjp(do)

    kv_out = ((T, kw), f32, (W, kw), lambda i: (i, 0))
    return _tile_call(name, fn, T // W, _swa_ins(p, snk, W) + [(do, (W, qw), lambda i: (i, 0))],
                      [((T, qw), bf16, (W, qw), lambda i: (i, 0)), kv_out, kv_out, kv_out, kv_out], [(tuple(snk.shape), f32)])


def _xa_fwd(name, p, kv):
    T = p.shape[0]
    tm = _row_tile(T, 512)
    W = XA_HEADS * XA_DH
    return _tile_call(name, lambda i, q, kv: _xa(q, kv), T // tm, [(p, (tm, W), lambda i: (i, 2560 // W)), _full(kv)],
                      [((T, W), f32, (tm, W), lambda i: (i, 0))])[0]


def _xa_bwd(name, p, kv, do):
    T = p.shape[0]
    tm = _row_tile(T, 512)
    W = XA_HEADS * XA_DH

    def fn(i, q, kv, do):
        _, vjp = jax.vjp(_xa, q, kv)
        return vjp(do)

    return _tile_call(name, fn, T // tm, [(p, (tm, W), lambda i: (i, 2560 // W)), _full(kv), _rows(do, tm)],
                      [((T, W), bf16, (tm, W), lambda i: (i, 0))], [(tuple(kv.shape), f32)])


def _merge_fwd(name, o_dn, o_sw, o_xa, p, wb):
    T = p.shape[0]
    tm = _row_tile(T, 512)
    GW = N_BRANCH * D_MODEL
    return _tile_call(name, lambda i, a, b, c, g, w: _merge(a, b, c, g, w.astype(f32)), T // tm,
                      [_rows(o_dn, tm), _rows(o_sw, tm), _rows(o_xa, tm), (p, (tm, GW), lambda i: (i, 1)), _full(wb)],
                      [((T, D_MODEL), bf16, (tm, D_MODEL), lambda i: (i, 0))])[0]


def _merge_bwd(name, o_dn, o_sw, o_xa, p, wb, dm):
    T = p.shape[0]
    tm = _row_tile(T, 512)
    GW = N_BRANCH * D_MODEL

    def fn(i, a, b, c, g, w, dm):
        _, vjp = jax.vjp(_merge, a, b, c, g, w.astype(f32))
        return vjp(dm)

    bo = ((T, BRANCH_W), bf16, (tm, BRANCH_W), lambda i: (i, 0))
    return _tile_call(name, fn, T // tm,
                      [_rows(o_dn, tm), _rows(o_sw, tm), _rows(o_xa, tm), (p, (tm, GW), lambda i: (i, 1)), _full(wb), _rows(dm, tm)],
                      [bo, bo, bo, ((T, GW), bf16, (tm, GW), lambda i: (i, 0))], [(tuple(wb.shape), f32)])


def _assemble_dp(name, dxc, dxp, dz, dswq, dxaq, dgates, dkc, dkp, dvc, dvp, dba):
    T = dz.shape[0]
    tm = _row_tile(T, 256)
    n = T // tm
    r = tm // WINDOW
    nb = T // WINDOW

    def fn(i, dxc, dxp, dz, dswq, dxaq, dgates, dkc, dvc, dba, *nxt):
        halo = jnp.where(i == n - 1, 0.0, dxp)
        dqkv = dxc + jnp.concatenate([jnp.zeros((tm - HALO, dxc.shape[1]), f32), halo], axis=0)
        shifted = [jnp.concatenate([jnp.where(i * r + 1 + b <= nb - 1, blk, 0.0) for b, blk in enumerate(half)], axis=0)
                   for half in (nxt[:r], nxt[r:])]
        parts = [dqkv, dz, dswq, dxaq, dgates, dkc + shifted[0], dvc + shifted[1], dba, jnp.zeros((tm, LANES), f32)]
        return jnp.concatenate(parts, axis=1)

    ins = [_rows(dxc, tm), (dxp, (HALO, dxp.shape[1]), lambda i: (jnp.minimum(i + 1, n - 1), 0))]
    ins += [_rows(t, tm) for t in (dz, dswq, dxaq, dgates, dkc, dvc, dba)]
    for t in (dkp, dvp):
        ins += [(t, (WINDOW, t.shape[1]), lambda i, b=b: (jnp.minimum(i * r + 1 + b, nb - 1), 0)) for b in range(r)]
    return _tile_call(name, fn, n, ins, [((T, D_INP), bf16, (tm, D_INP), lambda i: (i, 0))])[0]


def _loss_grad(name, y, tgt):
    T = y.shape[0]
    tm = _row_tile(T, 512)

    def fn(i, y, t):
        e = y - t
        part = 0.5 * jnp.sum(jnp.mean(e * e, axis=-1, keepdims=True), axis=0, keepdims=True)
        return e * (1.0 / D_MODEL), jnp.broadcast_to(part, (1, LANES))

    return _tile_call(name, fn, T // tm, [_rows(y, tm), _rows(tgt, tm)],
                      [((T, D_MODEL), f32, (tm, D_MODEL), lambda i: (i, 0))], [((1, LANES), f32)])


def _mem_ln_fwd(name, mem, g, b):
    M = mem.shape[0]
    return _tile_call(name, lambda i, m, g, b: (_ln(m, g, b),), 1, [_full(mem), _full(g), _full(b)],
                      [((M, D_MODEL), bf16, (M, D_MODEL), lambda i: (0, 0))])[0]


def _mem_ln_bwd(name, mem, g, b, dmn):
    def fn(i, m, g, b, d):
        _, vjp = jax.vjp(lambda g, b: _ln(m, g, b), g, b)
        return vjp(d)

    return _tile_call(name, fn, 1, [_full(mem), _full(g), _full(b), _full(dmn)], [], [((1, D_MODEL), f32), ((1, D_MODEL), f32)])


W_IN_RANGES = ((0, 1536), (1544, 2056), (2056, 2568), (2824, 3336), (3336, 6408), (2568, 2696), (2696, 2824), (1536, 1544))
W_IN_OFFS = tuple(sum(b - a for a, b in W_IN_RANGES[:k]) for k in range(len(W_IN_RANGES)))


def _pad_w_in(cols):
    parts = [cols(a, b) for a, b in W_IN_RANGES]
    return jnp.concatenate(parts + [jnp.zeros((parts[0].shape[0], D_INP - D_IN), parts[0].dtype)], axis=1)


def _unpad_dw_in(d, lo=0, hi=D_IN):
    parts = []
    for (a, b), off in sorted(zip(W_IN_RANGES, W_IN_OFFS)):
        a2, b2 = max(a, lo), min(b, hi)
        if a2 < b2:
            parts.append(d[:, off + a2 - a:off + b2 - a])
    return jnp.concatenate(parts, axis=1)


def _shard_cols(shards, l):
    n = shards[0].shape[-1]

    def cols(a, b):
        parts = [shards[s][l][:, max(a, s * n) - s * n:min(b, (s + 1) * n) - s * n] for s in range(N_SHARD)
                 if max(a, s * n) < min(b, (s + 1) * n)]
        return parts[0] if len(parts) == 1 else jnp.concatenate(parts, axis=1)

    return cols


def _lane_row(v, rows=8):
    out = jnp.zeros((rows, LANES), f32)
    return out.at[0, :v.shape[0]].set(v)


def _layer_fwd(l, x, xb, mem_nb, W):
    n = lambda s: f"l{l}_{s}"
    sv = {}
    u1, a1 = _gu_act(n("ffn1_gu"), xb, W['ffn1_w_gu_p'])
    f1, h1, h1b = _mm_res_ln(n("ffn1_down"), a1, W['ffn1_w_down'], x, W['ln_g'][0:1], W['ln_b'][0:1], 0.5)
    p = _mm(n("w_in"), h1b, W['w_inp'], 'nn', out_dtype=bf16)
    pba = _mm(n("w_in_ba"), h1b, W['w_ba'], 'nn')
    qd, qk, u, w, kt, egl = _dn_front_fwd(n("dn_front"), p, pba, W['conv_w'], W['hp'])
    o_raw, s_all = _scan_fwd(n("dn_scan"), qd, qk, u, w, kt, egl)
    o_dn = _dn_post_fwd(n("dn_post"), o_raw, p, W['norm_w'])
    o_sw = _swa_fwd(n("swa"), p, W['snk'])
    kv = _mm(n("mem_kv"), mem_nb, W['w_mem_kv'], 'nn')
    o_xa = _xa_fwd(n("xa"), p, kv)
    merged = _merge_fwd(n("merge"), o_dn, o_sw, o_xa, p, W['w_branch'])
    mo, h2, h2b = _mm_res_ln(n("w_out"), merged, W['w_out'], h1, W['ln_g'][1:2], W['ln_b'][1:2], 1.0)
    u2, a2 = _gu_act(n("ffn2_gu"), h2b, W['ffn2_w_gu_p'])
    f2, h3, h3b = _mm_res_ln(n("ffn2_down"), a2, W['ffn2_w_down'], h2, W['ln_g'][2:3], W['ln_b'][2:3], 0.5)
    sv = dict(x=x, xb=xb, u1=u1, a1=a1, f1=f1, h1=h1, h1b=h1b, p=p, pba=pba, wy=(qd, qk, u, w, kt, egl),
              o_raw=o_raw, s_all=s_all, o_dn=o_dn, o_sw=o_sw, kv=kv, o_xa=o_xa, merged=merged,
              mo=mo, h2=h2, h2b=h2b, u2=u2, a2=a2, f2=f2)
    return h3, h3b, sv


def _ffn_bwd(n, tag, x, xb, u, a, f, g, b, w_gu_t, w_down_t, dh):
    dx_a, df, dg, db = _res_ln_bwd(n(f"{tag}_ln_bwd"), x, f, g, b, 0.5, dh)
    d_down = _mm(n(f"{tag}_down_dw"), a, df, 'tn')
    du = _down_dx_act_bwd(n(f"{tag}_down_dx"), df, w_down_t, u)
    d_gu = _mm(n(f"{tag}_gu_dw"), xb, du, 'tn', out_cols=(FF_BLK, lambda j: (j % 2) * (D_FF // FF_BLK) + j // 2))
    dx = _mm(n(f"{tag}_gu_dx"), du, w_gu_t, 'nn', add=dx_a)
    return dx, d_gu, d_down, dg, db


def _layer_bwd(l, sv, mem_nb, W, dh3, dmem_acc):
    n = lambda s: f"l{l}_{s}"
    G = {}
    dh2, G['ffn2_w_gu'], G['ffn2_w_down'], dg2, db2 = _ffn_bwd(
        n, "ffn2", sv['h2'], sv['h2b'], sv['u2'], sv['a2'], sv['f2'], W['ln_g'][2:3], W['ln_b'][2:3], W['ffn2_w_gu_t'], W['ffn2_w_down_t'], dh3)
    dh1_a, dmo, dg1, db1 = _res_ln_bwd(n("ln1_bwd"), sv['h1'], sv['mo'], W['ln_g'][1:2], W['ln_b'][1:2], 1.0, dh2)
    G['w_out'] = _mm(n("w_out_dw"), sv['merged'], dmo, 'tn')
    dmerged = _mm(n("w_out_dx"), dmo, W['w_out_t'], 'nn')
    p = sv['p']
    do_dn, do_sw, do_xa, dgates, G['w_branch'] = _merge_bwd(n("merge_bwd"), sv['o_dn'], sv['o_sw'], sv['o_xa'], p, W['w_branch'], dmerged)
    dxaq, dkv = _xa_bwd(n("xa_bwd"), p, sv['kv'], do_xa)
    dkv = dkv.astype(bf16)
    G['w_mem_kv'] = _mm(n("mem_kv_dw"), mem_nb, dkv, 'tn')
    dmem_n = _mm(n("mem_kv_dx"), dkv, W['w_mem_kv_t'], 'nn', add=dmem_acc)
    dswq, dkc, dkp, dvc, dvp, dsnk = _swa_bwd(n("swa_bwd"), p, W['snk'], do_sw)
    do_raw, dz, dnw = _dn_post_bwd(n("dn_post_bwd"), sv['o_raw'], p, W['norm_w'], do_dn)
    cts = _scan_bwd(n("dn_scan_bwd"), *sv['wy'], sv['s_all'], do_raw)
    dxc, dxp, dba, dcw, dhp = _dn_front_bwd(n("dn_front_bwd"), p, sv['pba'], W['conv_w'], W['hp'], cts)
    dp = _assemble_dp(n("dp"), dxc, dxp, dz, dswq, dxaq, dgates, dkc, dkp, dvc, dvp, dba)
    G['w_in_p'] = _mm(n("w_in_dw"), sv['h1b'], dp, 'tn')
    dh1 = _mm(n("w_in_dx"), dp, W['w_inp_t'], 'nn', add=dh1_a)
    dx, G['ffn1_w_gu'], G['ffn1_w_down'], dg0, db0 = _ffn_bwd(
        n, "ffn1", sv['x'], sv['xb'], sv['u1'], sv['a1'], sv['f1'], W['ln_g'][0:1], W['ln_b'][0:1], W['ffn1_w_gu_t'], W['ffn1_w_down_t'], dh1)
    G['ln_g'] = jnp.concatenate([dg0, dg1, dg2], axis=0)
    G['ln_b'] = jnp.concatenate([db0, db1, db2], axis=0)
    G['dn_conv_w'] = dcw
    G['dn_a_log'] = dhp[0, :DN_HEADS]
    G['dn_dt_bias'] = dhp[1, :DN_HEADS]
    G['dn_norm_w'] = dnw[0]
    G['swa_sinks'] = dsnk[:, 0]
    return dx, dmem_n, G


def _local_step(x, mem, tgt, Wf):
    mem_g, mem_b = Wf['mem_ln_g'][None, :], Wf['mem_ln_b'][None, :]
    mem_nb = _mem_ln_fwd("mem_ln", mem, mem_g, mem_b)

    def paired(k, l):
        return Wf[k + '_p'][l] if k + '_p' in Wf else _pair_gate_up(Wf[k][l])

    def w_in_cols(l):
        return _shard_cols(Wf['w_in_sh'], l) if 'w_in_sh' in Wf else (lambda a, b: Wf['w_in'][l][:, a:b])

    layers = []
    for l in range(DEPTH):
        layers.append(dict(
            ln_g=Wf['ln_g'][l], ln_b=Wf['ln_b'][l], ffn1_w_gu_p=paired('ffn1_w_gu', l), ffn1_w_down=Wf['ffn1_w_down'][l],
            w_inp=_pad_w_in(w_in_cols(l)), conv_w=Wf['dn_conv_w'][l],
            hp=jnp.zeros((8, LANES), f32).at[0, :DN_HEADS].set(Wf['dn_a_log'][l]).at[1, :DN_HEADS].set(Wf['dn_dt_bias'][l]),
            norm_w=Wf['dn_norm_w'][l][None, :], snk=jnp.broadcast_to(Wf['swa_sinks'][l][:, None], (SWA_HEADS, LANES)),
            w_mem_kv=Wf['w_mem_kv'][l], w_branch=Wf['w_branch'][l], w_out=Wf['w_out'][l],
            ffn2_w_gu_p=paired('ffn2_w_gu', l), ffn2_w_down=Wf['ffn2_w_down'][l]))
        layers[l]['w_ba'] = layers[l]['w_inp'][:, 6400:6400 + LANES]
        for k in ('ffn1_w_gu_p', 'ffn1_w_down', 'w_inp', 'w_mem_kv', 'w_out', 'ffn2_w_gu_p', 'ffn2_w_down'):
            layers[l][k.replace('_p', '') + '_t'] = layers[l][k].T
    h, hb = x, x.astype(bf16)
    saved = []
    for l in range(DEPTH):
        h, hb, sv = _layer_fwd(l, h, hb, mem_nb, layers[l])
        saved.append(sv)
    dh, loss_row = _loss_grad("loss", h, tgt)
    grads = [None] * DEPTH
    dmem_n = None
    for l in reversed(range(DEPTH)):
        dh, dmem_n, grads[l] = _layer_bwd(l, saved[l], mem_nb, layers[l], dh, dmem_n)
    dmg, dmb = _mem_ln_bwd("mem_ln_bwd", mem, mem_g, mem_b, dmem_n)
    G = {k: jnp.stack([grads[l][k] for l in range(DEPTH)], axis=0) for k in grads[0]}
    G['mem_ln_g'], G['mem_ln_b'] = dmg[0], dmb[0]
    return loss_row, dh, G


SEG_ALIGN = 2048


def _round_up(n, m):
    return (n + m - 1) // m * m


def _layout(names, sizes, mult):
    off, table = 0, {}
    for nm in names:
        table[nm] = (off, sizes[nm])
        off += _round_up(sizes[nm], SEG_ALIGN)
    return table, _round_up(off // LANES, mult)


def _pack(table, rows, flat):
    names = list(table)
    lead = flat[names[0]].shape[:-1]
    parts, pos = [], 0
    for nm in names:
        off, size = table[nm]
        if off > pos:
            parts.append(jnp.zeros(lead + (off - pos,), flat[nm].dtype))
        parts.append(flat[nm])
        pos = off + size
    total = rows * LANES
    if total > pos:
        parts.append(jnp.zeros(lead + (total - pos,), parts[-1].dtype))
    return jnp.concatenate(parts, axis=-1).reshape(lead + (rows, LANES))


def _unpack(table, packed, nm):
    off, size = table[nm]
    flat = packed.reshape(packed.shape[:-2] + (-1,))
    return flat[..., off:off + size]


def _split_shards(full, ax):
    shp = full.shape
    t = full.reshape(shp[:ax] + (N_SHARD, shp[ax] // N_SHARD) + shp[ax + 1:])
    return jnp.moveaxis(t, ax, 0)


def _join_shards(sh4, ax):
    return jnp.concatenate([sh4[s] for s in range(N_SHARD)], axis=ax)


ANY = pl.BlockSpec(memory_space=pl.ANY)


def _me():
    return lax.axis_index("x"), lax.axis_index("y"), lax.axis_index("c")


def _comm_call(name, body, arrays, out_shapes, n_sem):
    n = len(arrays)
    scratch = [pltpu.SemaphoreType.DMA((n, n_sem)), pltpu.SemaphoreType.DMA((n, n_sem))]
    return pl.pallas_call(
        body, name=name, out_shape=out_shapes, in_specs=[ANY] * n, out_specs=[ANY] * n, scratch_shapes=scratch,
        compiler_params=pltpu.CompilerParams(has_side_effects=True),
    )(*arrays)


def _all_gather(name, xs):
    n = len(xs)

    def split(shape):
        ax = next(i for i, d in enumerate(shape) if d % 2 == 0)
        return ax, shape[ax] // 2

    def body(*refs):
        x_refs, out_refs = refs[:n], refs[n:2 * n]
        send_sems, recv_sems = refs[2 * n:]
        mx, my, mc = _me()
        me_s = 2 * mx + my
        across_x, across_y, diag = (1 - mx, my), (mx, 1 - my), (1 - mx, 1 - my)
        sid = lambda chip: 2 * chip[0] + chip[1]

        def part(ref, a, which):
            if which is None:
                return ref
            ax, size = split(xs[a].shape[1:])
            return ref.at[(slice(None),) * ax + (pl.ds(which * size, size),)]

        def copy(a, k, shard, half, which, to, src=None):
            dst = part(out_refs[a].at[shard, half], a, which)
            return pltpu.make_async_remote_copy(src_ref=dst if src is None else part(src, a, which), dst_ref=dst,
                                                send_sem=send_sems.at[a, k], recv_sem=recv_sems.at[a, k], device_id=to, device_id_type=MESH)

        sibling, here = (mx, my, 1 - mc), (mx, my, mc)
        sent = []
        for a in range(n):
            sent += [copy(a, 0, me_s, mc, None, (*across_x, mc), src=x_refs[a].at[mc]),
                     copy(a, 1, me_s, mc, None, (*across_y, mc), src=x_refs[a].at[mc])]
        for cp in sent:
            cp.start()
        landing = [(0, across_x, None, across_y), (1, across_y, None, across_x), (2, diag, 0, None), (3, diag, 1, None)]
        for k, origin, which, relay_to in landing:
            for a in range(n):
                copy(a, k, sid(origin), mc, which, here).wait_recv()
                if relay_to is not None:
                    sent.append(copy(a, 2 + k, sid(origin), mc, k, (*relay_to, mc)))
                    sent[-1].start()
                sent.append(copy(a, 4 + k, sid(origin), mc, which, sibling))
                sent[-1].start()
        for k, origin, which, _ in landing:
            for a in range(n):
                copy(a, 4 + k, sid(origin), 1 - mc, which, here).wait_recv()
        for cp in sent:
            cp.wait_send()

    return _comm_call(name, body, xs, [jax.ShapeDtypeStruct((N_SHARD,) + x.shape, x.dtype) for x in xs], 8)


def _pair_exchange(name, items):
    n = len(items)

    def body(*refs):
        src_refs, dst_refs = refs[:n], refs[n:2 * n]
        send_sems, recv_sems = refs[2 * n:]
        mx, my, mc = _me()
        cps = [pltpu.make_async_remote_copy(src_ref=src_refs[a].at[s, 1 - mc], dst_ref=dst_refs[a].at[s], send_sem=send_sems.at[a, s],
                                            recv_sem=recv_sems.at[a, s], device_id=(mx, my, 1 - mc), device_id_type=MESH)
               for a in range(n) for s in range(N_SHARD)]
        for cp in cps:
            cp.start()
        for cp in cps:
            cp.wait()

    return _comm_call(name, body, items, [jax.ShapeDtypeStruct((N_SHARD,) + t.shape[2:], t.dtype) for t in items], N_SHARD)


def _chip_exchange(name, parts):
    n = len(parts)

    def body(*refs):
        p_refs, dst_refs = refs[:n], refs[n:2 * n]
        send_sems, recv_sems = refs[2 * n:]
        mx, my, mc = _me()
        chips = [(1 - mx, my), (mx, 1 - my), (1 - mx, 1 - my)]
        cps = [pltpu.make_async_remote_copy(src_ref=p_refs[a].at[2 * cx + cy], dst_ref=dst_refs[a].at[j], send_sem=send_sems.at[a, j],
                                            recv_sem=recv_sems.at[a, j], device_id=(cx, cy, mc), device_id_type=MESH)
               for a in range(n) for j, (cx, cy) in enumerate(chips)]
        for cp in cps:
            cp.start()
        for cp in cps:
            cp.wait()

    return _comm_call(name, body, parts, [jax.ShapeDtypeStruct((3,) + t.shape[1:], t.dtype) for t in parts], 3)


def _pair_swap(name, reds):
    n = len(reds)

    def body(*refs):
        r_refs, out_refs = refs[:n], refs[n:2 * n]
        send_sems, recv_sems = refs[2 * n:]
        mx, my, mc = _me()
        cps = [pltpu.make_async_remote_copy(src_ref=r_refs[a], dst_ref=out_refs[a], send_sem=send_sems.at[a, 0],
                                            recv_sem=recv_sems.at[a, 0], device_id=(mx, my, 1 - mc), device_id_type=MESH)
               for a in range(n)]
        for cp in cps:
            cp.start()
        for cp in cps:
            cp.wait()

    return _comm_call(name, body, reds, [jax.ShapeDtypeStruct(t.shape, t.dtype) for t in reds], 1)


EW_BLOCK_BYTES = 1 << 20


def _ew_call(name, fn, ins, n_out, out_dtype=f32):
    shape = ins[0].shape
    last = shape[-1]
    flat = [a.reshape(-1, last) for a in ins]
    R = flat[0].shape[0]
    cands = tuple(c for c in (4096, 2048, 1024, 512, 256, 128, 64, 32, 16, 8) if c * last * 4 <= EW_BLOCK_BYTES)
    tr = _pick(R, cands)
    res = _tile_call(name, lambda i, *a: fn(*a), R // tr, [(a, (tr, last), lambda i: (i, 0)) for a in flat],
                     [((R, last), out_dtype, (tr, last), lambda i: (i, 0))] * n_out)
    return [r.reshape(shape) for r in res]


def _adamw(g, w, m, v):
    m = B1 * m + (1.0 - B1) * g
    v = B2 * v + (1.0 - B2) * jnp.square(g)
    m_hat = m / (1.0 - B1 ** STEP)
    v_hat = v / (1.0 - B2 ** STEP)
    return -LR * (m_hat / (jnp.sqrt(v_hat) + EPS) + WD * w), m, v


def _adamw_call(name, mine, theirs, w, m, v, mc1):
    shape, last = w.shape, w.shape[-1]
    g2 = [t.reshape(-1, last) for t in (mine, theirs)]
    w3 = [t.reshape(2, -1, last) for t in (w, m, v)]
    R = g2[0].shape[0]
    tr = _pick(R, tuple(c for c in (4096, 2048, 1024, 512, 256, 128, 64, 32, 16, 8) if c * last * 4 <= EW_BLOCK_BYTES))

    def body(mc_ref, mine_ref, theirs_ref, w_ref, m_ref, v_ref, g_out, d_out, m_out, v_out):
        g = jnp.where(pl.program_id(0) == mc_ref[0], mine_ref[...], theirs_ref[...])
        d, nm, nv = _adamw(g, w_ref[0], m_ref[0], v_ref[0])
        g_out[0], d_out[0], m_out[0], v_out[0] = g, d, nm, nv

    half = pl.BlockSpec((tr, last), lambda h, i: (i, 0))
    full = pl.BlockSpec((1, tr, last), lambda h, i: (h, i, 0))
    res = pl.pallas_call(
        body, name=name, grid=(2, R // tr),
        in_specs=[pl.BlockSpec(memory_space=pltpu.SMEM), half, half, full, full, full], out_specs=[full] * 4,
        out_shape=[jax.ShapeDtypeStruct((2, R, last), f32)] * 4,
        compiler_params=pltpu.CompilerParams(dimension_semantics=("arbitrary", "arbitrary"), vmem_limit_bytes=VMEM_LIMIT),
    )(mc1, *g2, *w3)
    return tuple(r.reshape(shape) for r in res)


def kernel(x, mem, mem_ln_g, mem_ln_b, ln_g, ln_b, ffn1_w_gu, ffn1_w_down, w_in, dn_conv_w, dn_a_log, dn_dt_bias, dn_norm_w, swa_sinks, w_mem_kv, w_branch, w_out, ffn2_w_gu, ffn2_w_down, loss_target, m_mem_ln_g, m_mem_ln_b, m_ln_g, m_ln_b, m_ffn1_w_gu, m_ffn1_w_down, m_w_in, m_dn_conv_w, m_dn_a_log, m_dn_dt_bias, m_dn_norm_w, m_swa_sinks, m_w_mem_kv, m_w_branch, m_w_out, m_ffn2_w_gu, m_ffn2_w_down, v_mem_ln_g, v_mem_ln_b, v_ln_g, v_ln_b, v_ffn1_w_gu, v_ffn1_w_down, v_w_in, v_dn_conv_w, v_dn_a_log, v_dn_dt_bias, v_dn_norm_w, v_swa_sinks, v_w_mem_kv, v_w_branch, v_w_out, v_ffn2_w_gu, v_ffn2_w_down):
    args = dict(locals())
    Ws = {n: args[n] for n in WEIGHTS}
    Ms = {n: args["m_" + n] for n in WEIGHTS}
    Vs = {n: args["v_" + n] for n in WEIGHTS}
    mc = lax.axis_index("c")
    my_s = 2 * lax.axis_index("x") + lax.axis_index("y")
    small = [n for n in WEIGHTS if n not in MATRICES]

    ag_table, ag_rows = _layout(SMALL_SHARDED, {n: Ws[n].size for n in SMALL_SHARDED}, 16)
    ag_small = _pack(ag_table, ag_rows, {n: Ws[n].reshape(-1) for n in SMALL_SHARDED}).reshape(2, ag_rows // 2, LANES)
    local = [Ws[n].astype(bf16) for n in MATRICES] + [ag_small]
    gathered = _all_gather("all_gather_w", local)
    gathered = [jnp.stack([jnp.where(my_s == s, loc, g[s]) for s in range(N_SHARD)], axis=0) for loc, g in zip(local, gathered)]
    Wf = {n: _join_shards(g, SHARD_AXIS[n]) for n, g in zip(MATRICES, gathered) if not n.endswith('w_gu') and n != 'w_in'}
    for n, g in zip(MATRICES, gathered):
        if n.endswith('w_gu'):
            assert g.shape[-1] == FF_BLK
            Wf[n + '_p'] = jnp.concatenate([g[0], g[2], g[1], g[3]], axis=2)
        if n == 'w_in':
            Wf['w_in_sh'] = g
    g_small = gathered[-1].reshape(N_SHARD, ag_rows, LANES)
    for n in SMALL_SHARDED:
        Wf[n] = _join_shards(_unpack(ag_table, g_small, n).reshape((N_SHARD,) + Ws[n].shape), SHARD_AXIS[n])
    for n in WEIGHTS:
        if SHARD_AXIS[n] is None:
            Wf[n] = Ws[n]

    loss_row, dx, G = _local_step(x[0], mem[0], loss_target[0], Wf)

    table, rows = _layout(small + ['loss'], {**{n: Ws[n].size for n in small}, 'loss': 1}, 16)
    gflat = {n: (jnp.broadcast_to(G[n].reshape(1, -1), (N_SHARD, G[n].size)) if SHARD_AXIS[n] is None
                 else _split_shards(G[n], SHARD_AXIS[n]).reshape(N_SHARD, -1)) for n in small}
    gflat['loss'] = jnp.broadcast_to(loss_row[:, :1], (N_SHARD, 1))
    n_in = Ws['w_in'].shape[-1]
    items =[_split_shards(G[n], SHARD_AXIS[n]) if n != 'w_in' else
             jnp.stack([jnp.stack([_unpad_dw_in(G['w_in_p'][l], s * n_in, (s + 1) * n_in) for l in range(DEPTH)]) for s in range(N_SHARD)])
             for n in MATRICES] + [_pack(table, rows, gflat).reshape(N_SHARD, 2, rows // 2, LANES)]
    tags = MATRICES + ['small']
    got = _pair_exchange("rs_pair", items)
    keep = [lax.dynamic_index_in_dim(a, mc, axis=1, keepdims=False) for a in items]
    wire = [bf16] * len(MATRICES) + [f32]
    part = [_ew_call(f"rs_add_pair_{t}", lambda a, b: a + b, [k, g], 1, out_dtype=dt)[0] for t, k, g, dt in zip(tags, keep, got, wire)]
    others = _chip_exchange("rs_chips", part)
    own = lambda a: lax.dynamic_index_in_dim(a, my_s, axis=0, keepdims=False)
    red = [_ew_call(f"rs_add_chips_{t}", lambda k, g, fx, fy, fxy: ((k + g) + fy) + (fx + fxy), [own(k), own(g), o[0], o[1], o[2]], 1)[0]
           for t, k, g, o in zip(tags, keep, got, others)]
    theirs = _pair_swap("rs_swap", red)

    mc1 = mc.astype(i32).reshape(1)
    outs = {}
    for n, a, b in zip(MATRICES, red, theirs):
        outs[n] = _adamw_call(f"adamw_{n}", a, b, Ws[n], Ms[n], Vs[n], mc1)
    fill = {'loss': jnp.zeros((1,), f32)}
    packs = [_pack(table, rows, {**{n: src[n].reshape(-1) for n in small}, **fill}).reshape(2, rows // 2, LANES) for src in (Ws, Ms, Vs)]
    small_out = [p.reshape(rows, LANES) for p in _adamw_call("adamw_small", red[-1], theirs[-1], *packs, mc1)]
    for n in small:
        outs[n] = tuple(_unpack(table, p, n).reshape(Ws[n].shape) for p in small_out)
    loss = _unpack(table, small_out[0], 'loss').reshape(())
    return (loss, dx[None], *[outs[n][k] for k in range(4) for n in WEIGHTS])
```

```python
import functools

import jax
import jax.numpy as jnp
from jax import lax
from jax.experimental import pallas as pl
from jax.experimental.pallas import tpu as pltpu

f32, bf16, i32 = jnp.float32, jnp.bfloat16, jnp.int32
HI = lax.Precision.HIGHEST
MESH = pl.DeviceIdType.MESH

D_MODEL = 1024
DEPTH = 2
DN_HEADS, DN_DK, DN_CONV, DN_CHUNK = 4, 128, 4, 64
SWA_HEADS, SWA_KV_HEADS, SWA_DH, WINDOW = 8, 2, 64, 128
XA_HEADS, XA_DH = 4, 128
D_FF = 2816
N_BRANCH, BRANCH_W = 3, 512
ALPHA = (2 * DEPTH) ** 0.25
LN_EPS, RMS_EPS, NEG_INF = 1e-5, 1e-6, -1e30
D_IN = 6408
D_INP = 6656
LR, B1, B2, EPS, WD, STEP = 0.001, 0.9, 0.999, 1e-08, 0.01, 10

LANES = 128
VMEM_LIMIT = 56 << 20
N_SHARD = 4
SCAN_CHUNKS = 8
HALO = 16
MM_TILES = {
    'ffn1_gu_dx': (512, 1024, 5632), 'ffn2_gu_dx': (512, 1024, 5632), 'w_in_dx': (512, 1024, 6656), 'w_in': (2048, 512, 1024),
    'w_in_dw': (1024, 512, 4096), 'ffn1_down_dw': (1408, 512, 4096), 'ffn2_down_dw': (1408, 512, 4096),
    'ffn1_down_dx': (1024,), 'ffn2_down_dx': (1024,), 'ffn1_down': (1024,), 'ffn2_down': (1024,), 'w_out': (1024,),
    'ffn1_gu': (1024,), 'ffn2_gu': (1024,),
    'w_out_dx': (512, 1024, 1024),
}


def _tiles(name, default=None):
    return MM_TILES.get(name, MM_TILES.get(name.split('_', 1)[-1], default))

WEIGHTS = ['mem_ln_g', 'mem_ln_b', 'ln_g', 'ln_b', 'ffn1_w_gu', 'ffn1_w_down', 'w_in', 'dn_conv_w', 'dn_a_log',
           'dn_dt_bias', 'dn_norm_w', 'swa_sinks', 'w_mem_kv', 'w_branch', 'w_out', 'ffn2_w_gu', 'ffn2_w_down']
SHARD_AXIS = {'mem_ln_g': None, 'mem_ln_b': None, 'ln_g': 2, 'ln_b': 2, 'ffn1_w_gu': 2, 'ffn1_w_down': 1, 'w_in': 2,
              'dn_conv_w': 2, 'dn_a_log': None, 'dn_dt_bias': None, 'dn_norm_w': None, 'swa_sinks': None,
              'w_mem_kv': 1, 'w_branch': 3, 'w_out': 1, 'ffn2_w_gu': 2, 'ffn2_w_down': 1}
MATRICES = ['ffn1_w_gu', 'ffn1_w_down', 'w_in', 'w_mem_kv', 'w_branch', 'w_out', 'ffn2_w_gu', 'ffn2_w_down']
SMALL_SHARDED = ['ln_g', 'ln_b', 'dn_conv_w']


def _dg(a, b, mode, hi):
    nb = a.ndim - 2
    bd = tuple(range(nb))
    ca = nb if mode == 'tn' else nb + 1
    cb = nb + 1 if mode == 'nt' else nb
    dims = (((ca,), (cb,)), (bd, bd))
    dot = lambda x, y: lax.dot_general(x, y, dims, preferred_element_type=f32)
    a_hi, b_hi = a.astype(bf16), b.astype(bf16)
    if not hi:
        return dot(a_hi, b_hi)
    a_lo = (a.astype(f32) - a_hi.astype(f32)).astype(bf16)
    b_lo = (b.astype(f32) - b_hi.astype(f32)).astype(bf16)
    return dot(a_hi, b_hi) + (dot(a_hi, b_lo) + dot(a_lo, b_hi))


@functools.partial(jax.custom_vjp, nondiff_argnums=(2, 3))
def _dot(a, b, mode, hi):
    return _dg(a, b, mode, hi)


def _dot_fwd(a, b, mode, hi):
    return _dg(a, b, mode, hi), (a, b)


def _dot_bwd(mode, hi, res, g):
    a, b = res
    if mode == 'nn':
        da, db = _dg(g, b, 'nt', hi), _dg(a, g, 'tn', hi)
    elif mode == 'nt':
        da, db = _dg(g, b, 'nn', hi), _dg(g, a, 'tn', hi)
    else:
        da, db = _dg(b, g, 'nt', hi), _dg(a, g, 'nn', hi)
    return da.astype(a.dtype), db.astype(b.dtype)


_dot.defvjp(_dot_fwd, _dot_bwd)


def bdot(a, b, mode):
    return _dot(a, b, mode, False)


def hdot(a, b, mode):
    return _dot(a, b, mode, True)


def _shift_rows_impl(x, tail, s):
    r = pltpu.roll(x, s, 0)
    rows = lax.broadcasted_iota(i32, tail.shape, 0)
    top = jnp.where(rows >= s, r[:HALO], pltpu.roll(tail, s, 0))
    return jnp.concatenate([top, r[HALO:]], axis=0)


@functools.partial(jax.custom_vjp, nondiff_argnums=(2,))
def _shift_rows(x, tail, s):
    return _shift_rows_impl(x, tail, s)


def _shift_rows_fwd(x, tail, s):
    return _shift_rows_impl(x, tail, s), None


def _shift_rows_bwd(s, _, g):
    n = g.shape[0]
    r = pltpu.roll(g, n - s, 0)
    rows = lax.broadcasted_iota(i32, (HALO, g.shape[1]), 0)
    last = r[n - HALO:]
    keep = rows < HALO - s
    dx = jnp.concatenate([r[:n - HALO], jnp.where(keep, last, 0.0)], axis=0)
    return dx, jnp.where(keep, 0.0, pltpu.roll(g[:HALO], HALO - s, 0))


_shift_rows.defvjp(_shift_rows_fwd, _shift_rows_bwd)


def _lane_pick(x, lane):
    idx = lax.broadcasted_iota(i32, x.shape, x.ndim - 1)
    return jnp.sum(jnp.where(idx == lane, x, 0.0), axis=-1, keepdims=True)


def _silu(x):
    return x * jax.nn.sigmoid(x)


def _tri_inv(a):
    C = a.shape[-1]
    eye = (lax.broadcasted_iota(i32, a.shape, 1) == lax.broadcasted_iota(i32, a.shape, 2)).astype(f32)
    p = -a
    x = eye + p
    for _ in range((C - 1).bit_length() - 1):
        p = _dg(p, p, 'nn', True)
        x = x + _dg(x, p, 'nn', True)
    return x


@jax.custom_vjp
def _tri_solve(a, rhs):
    return _dg(_tri_inv(a), rhs, 'nn', True)


def _tri_solve_fwd(a, rhs):
    tinv = _tri_inv(a)
    sol = _dg(tinv, rhs, 'nn', True)
    return sol, (tinv, sol)


def _tri_solve_bwd(res, g):
    tinv, sol = res
    d_rhs = _dg(tinv, g, 'tn', True)
    return -_dg(d_rhs, sol, 'nt', True), d_rhs


_tri_solve.defvjp(_tri_solve_fwd, _tri_solve_bwd)


@functools.partial(jax.custom_vjp, nondiff_argnums=(1,))
def _lane_head(x, n):
    return x[:, :, :n]


def _lane_head_fwd(x, n):
    return x[:, :, :n], None


def _lane_head_bwd(n, _, g):
    s = jnp.sum(g, axis=-1, keepdims=True) * (1.0 / LANES)
    return (jnp.broadcast_to(s, g.shape[:-1] + (LANES,)),)


_lane_head.defvjp(_lane_head_fwd, _lane_head_bwd)


@functools.partial(jax.custom_vjp, nondiff_argnums=(1,))
def _last_row(x, c):
    return x[:, c - 1:, :]


def _last_row_fwd(x, c):
    return x[:, c - 1:, :], None


def _last_row_bwd(c, _, g):
    shape = (g.shape[0], c, g.shape[2])
    rows = lax.broadcasted_iota(i32, shape, 1)
    return (jnp.where(rows == c - 1, jnp.broadcast_to(g, shape), 0.0),)


_last_row.defvjp(_last_row_fwd, _last_row_bwd)


def _full(a):
    nd = a.ndim
    return (a, tuple(a.shape), lambda i, _nd=nd: (0,) * _nd)


def _rows(a, tm, col=0, width=None):
    width = a.shape[1] if width is None else width
    return (a, (tm, width), lambda i, _c=col: (i, _c))


def _tile_call(name, fn, n, ins, outs, accs=()):
    n_in, n_out, n_acc = len(ins), len(outs), len(accs)

    def body(*refs):
        i = pl.program_id(0)
        res = fn(i, *[r[...].astype(f32) for r in refs[:n_in]])
        if not isinstance(res, (tuple, list)):
            res = (res,)
        assert len(res) == n_out + n_acc, (name, len(res), n_out, n_acc)
        for r, v in zip(refs[n_in:n_in + n_out], res[:n_out]):
            r[...] = v.astype(r.dtype)
        if n_acc:
            acc_refs = refs[n_in + n_out:]

            @pl.when(i == 0)
            def _():
                for r in acc_refs:
                    r[...] = jnp.zeros(r.shape, r.dtype)

            for r, v in zip(acc_refs, res[n_out:]):
                r[...] += v.astype(r.dtype)

    out_shape = [jax.ShapeDtypeStruct(s, d) for s, d, _, _ in outs] + [jax.ShapeDtypeStruct(s, d) for s, d in accs]
    out_specs = [pl.BlockSpec(b, m) for _, _, b, m in outs]
    out_specs += [pl.BlockSpec(tuple(s), lambda i, _nd=len(s): (0,) * _nd) for s, _ in accs]
    res = pl.pallas_call(
        body, name=name, grid=(n,),
        in_specs=[pl.BlockSpec(b, m) for _, b, m in ins],
        out_specs=out_specs, out_shape=out_shape,
        compiler_params=pltpu.CompilerParams(dimension_semantics=("arbitrary",), vmem_limit_bytes=VMEM_LIMIT),
    )(*[a for a, _, _ in ins])
    return res


def _pick(n, cands):
    for c in cands:
        if n % c == 0:
            return c
    return n


def _mm(name, a, b, mode, out_dtype=f32, add=None, out_cols=None):
    if mode == 'tn':
        K, M = a.shape
    else:
        M, K = a.shape
    N = b.shape[0] if mode == 'nt' else b.shape[1]
    tm = _pick(M, (1024, 1408, 512, 256, 128))
    tn = _pick(N, (512, 1408, 256, 128)) if out_cols is None else out_cols[0]
    tuned = _tiles(name)
    if tuned is not None and (M % tuned[0] or N % tuned[1] or K % tuned[2]):
        tuned = None
    if tuned is not None:
        tm, tn = tuned[:2]
    col = (lambda j: j) if out_cols is None else out_cols[1]
    tk = K if K <= 3328 else _pick(K, (3328, 2816, 2048, 1024, 512, 256, 128))
    if tuned is not None:
        tk = tuned[2]
    nk = K // tk
    ca = 0 if mode == 'tn' else 1
    cb = 1 if mode == 'nt' else 0
    dims = (((ca,), (cb,)), ((), ()))

    def body(*refs):
        a_ref, b_ref = refs[0], refs[1]
        add_ref = refs[2] if add is not None else None
        part = lax.dot_general(a_ref[...].astype(bf16), b_ref[...].astype(bf16), dims, preferred_element_type=f32)

        def finish(r, o_ref):
            if add_ref is not None:
                r = r + add_ref[...].astype(f32)
            o_ref[...] = r.astype(o_ref.dtype)

        if nk == 1:
            finish(part, refs[-1])
            return
        o_ref, acc_ref = refs[-2], refs[-1]
        k = pl.program_id(2)

        @pl.when(k == 0)
        def _():
            acc_ref[...] = part

        @pl.when(k > 0)
        def _():
            acc_ref[...] += part

        @pl.when(k == nk - 1)
        def _():
            finish(acc_ref[...], o_ref)

    a_spec = pl.BlockSpec((tk, tm), lambda i, j, k: (k, i)) if mode == 'tn' else pl.BlockSpec((tm, tk), lambda i, j, k: (i, k))
    b_spec = pl.BlockSpec((tn, tk), lambda i, j, k: (j, k)) if mode == 'nt' else pl.BlockSpec((tk, tn), lambda i, j, k: (k, j))
    in_specs, args = [a_spec, b_spec], [a, b]
    if add is not None:
        in_specs.append(pl.BlockSpec((tm, tn), lambda i, j, k: (i, j)))
        args.append(add)
    return pl.pallas_call(
        body, name=name, grid=(M // tm, N // tn, nk), in_specs=in_specs,
        out_specs=pl.BlockSpec((tm, tn), lambda i, j, k: (i, col(j))),
        out_shape=jax.ShapeDtypeStruct((M, N), out_dtype),
        scratch_shapes=[pltpu.VMEM((tm, tn), f32)] if nk > 1 else [],
        compiler_params=pltpu.CompilerParams(dimension_semantics=("parallel", "parallel", "arbitrary"), vmem_limit_bytes=VMEM_LIMIT),
    )(*args)


def _swiglu(u):
    return _silu(u[:, :D_FF]) * u[:, D_FF:]


def _res_ln(x, f, g, b, s):
    r = ALPHA * x + s * f
    mu = jnp.mean(r, axis=-1, keepdims=True)
    rc = r - mu
    var = jnp.mean(rc * rc, axis=-1, keepdims=True)
    return rc * lax.rsqrt(var + LN_EPS) * g + b


def _ln(x, g, b):
    mu = jnp.mean(x, axis=-1, keepdims=True)
    xc = x - mu
    var = jnp.mean(xc * xc, axis=-1, keepdims=True)
    return xc * lax.rsqrt(var + LN_EPS) * g + b


def _dn_pre(first, xc, xp, ba, cw, hp):
    xp = jnp.where(first, 0.0, xp)
    y = cw[DN_CONV - 1:DN_CONV, :] * xc
    for j in range(DN_CONV - 1):
        y = y + cw[j:j + 1, :] * _shift_rows(xc, xp, DN_CONV - 1 - j)
    c = _silu(y)
    qs, ks, vs, gs, bs = [], [], [], [], []
    nqk = DN_HEADS * DN_DK
    for h in range(DN_HEADS):
        q = c[:, h * DN_DK:(h + 1) * DN_DK]
        k = c[:, nqk + h * DN_DK:nqk + (h + 1) * DN_DK]
        v = c[:, 2 * nqk + h * DN_DK:2 * nqk + (h + 1) * DN_DK]
        qs.append(q * lax.rsqrt(jnp.sum(q * q, axis=-1, keepdims=True) + RMS_EPS))
        ks.append(k * lax.rsqrt(jnp.sum(k * k, axis=-1, keepdims=True) + RMS_EPS))
        vs.append(v)
        beta = jax.nn.sigmoid(_lane_pick(ba, h))
        a_log = _lane_pick(hp[0:1, :], h)
        dt = _lane_pick(hp[1:2, :], h)
        g = -jnp.exp(a_log) * jax.nn.softplus(_lane_pick(ba, DN_HEADS + h) + dt)
        gs.append(jnp.broadcast_to(g, q.shape))
        bs.append(jnp.broadcast_to(beta, q.shape))
    return tuple(jnp.stack(t, axis=0) for t in (qs, ks, vs, gs, bs))


def _wy(q, k, v, gb, bb):
    B, C, _ = q.shape
    ri = lax.broadcasted_iota(i32, (B, C, C), 1)
    ci = lax.broadcasted_iota(i32, (B, C, C), 2)
    tril, strict = ri >= ci, ri > ci
    gc = hdot(tril.astype(f32), gb, 'nn')
    gl = jnp.broadcast_to(_last_row(gc, C), gc.shape)
    col = _lane_head(gc, C)
    decay = jnp.exp(jnp.where(tril, col - jnp.swapaxes(col, 1, 2), NEG_INF))
    qs = q * (DN_DK ** -0.5)
    kb = k * bb
    a = jnp.where(strict, bdot(kb, k, 'nt') * decay, 0.0)
    sol = _tri_solve(a, jnp.concatenate([v * bb, kb * jnp.exp(gc)], axis=-1))
    qk = jnp.where(tril, bdot(qs, k, 'nt') * decay, 0.0)
    kt = k * jnp.exp(gl - gc)
    qd = qs * jnp.exp(gc)
    return qd, qk, sol[..., :DN_DK], sol[..., DN_DK:], kt, jnp.exp(gl)


def _scan_step(S, qd, qk, u, w, kt, egl):
    vn = u - bdot(w, S, 'nn')
    o = bdot(qd, S, 'nn') + bdot(qk, vn, 'nn')
    e2 = jnp.concatenate([egl] * (DN_DK // DN_CHUNK), axis=1)
    return o, S * e2 + bdot(kt, vn, 'tn')


def _dn_post(o, z, nw):
    outs = []
    for h in range(DN_HEADS):
        oh = o[h]
        oh = oh * lax.rsqrt(jnp.mean(oh * oh, axis=-1, keepdims=True) + RMS_EPS) * nw
        outs.append(oh * _silu(z[:, h * DN_DK:(h + 1) * DN_DK]))
    return jnp.concatenate(outs, axis=1)


def _swa(first, q, kc, kp, vc, vp, snk):
    W = q.shape[1]
    G = SWA_HEADS // SWA_KV_HEADS
    r = lax.broadcasted_iota(i32, (G, W, 2 * W), 1).reshape(G * W, 2 * W)
    c = lax.broadcasted_iota(i32, (G * W, 2 * W), 1)
    mask = (c > W + r - WINDOW) & (c <= W + r) & jnp.logical_or(c >= W, jnp.logical_not(first))
    sink_all = _lane_pick(snk, 0)
    outs = []
    for j in range(SWA_KV_HEADS):
        qj = q[j * G:(j + 1) * G].reshape(G * W, SWA_DH)
        kk = jnp.concatenate([kp[j], kc[j]], axis=0)
        vv = jnp.concatenate([vp[j], vc[j]], axis=0)
        s = jnp.where(mask, bdot(qj, kk, 'nt') * (SWA_DH ** -0.5), NEG_INF)
        sink = jnp.broadcast_to(sink_all[j * G:(j + 1) * G][:, None, :], (G, W, 1)).reshape(G * W, 1)
        m = jnp.maximum(jnp.max(s, axis=-1, keepdims=True), sink)
        p = jnp.exp(s - m)
        p = p / (jnp.sum(p, axis=-1, keepdims=True) + jnp.exp(sink - m))
        outs.append(bdot(p, vv, 'nn').reshape(G, W, SWA_DH))
    return jnp.concatenate(outs, axis=0)


def _xa(q, kv):
    outs = []
    nk = XA_HEADS * XA_DH
    for h in range(XA_HEADS):
        qh = q[:, h * XA_DH:(h + 1) * XA_DH]
        kh = kv[:, h * XA_DH:(h + 1) * XA_DH]
        vh = kv[:, nk + h * XA_DH:nk + (h + 1) * XA_DH]
        s = bdot(qh, kh, 'nt') * (XA_DH ** -0.5)
        m = jnp.max(s, axis=-1, keepdims=True)
        p = jnp.exp(s - m)
        p = p / jnp.sum(p, axis=-1, keepdims=True)
        outs.append(bdot(p, vh, 'nn'))
    return jnp.concatenate(outs, axis=1)


def _merge(o_dn, o_sw, o_xa, gates, wb):
    acc = None
    for n, o in enumerate((o_dn, o_sw, o_xa)):
        t = jax.nn.sigmoid(gates[:, n * D_MODEL:(n + 1) * D_MODEL]) * bdot(o, wb[n], 'nn')
        acc = t if acc is None else acc + t
    return acc


def _row_tile(T, want):
    return _pick(T, tuple(c for c in (1024, 512, 256, 128, 64) if c <= want))


def _res_ln_fwd(name, x, f, g, b, s):
    T = x.shape[0]
    tm = _row_tile(T, 512)

    def fn(i, x, f, g, b):
        h = _res_ln(x, f, g, b, s)
        return h, h

    return _tile_call(name, fn, T // tm, [_rows(x, tm), _rows(f, tm), _full(g), _full(b)],
                      [((T, D_MODEL), f32, (tm, D_MODEL), lambda i: (i, 0)), ((T, D_MODEL), bf16, (tm, D_MODEL), lambda i: (i, 0))])


def _res_ln_bwd(name, x, f, g, b, s, dh):
    T = x.shape[0]
    tm = _row_tile(T, 512)

    def fn(i, x, f, g, b, dh):
        _, vjp = jax.vjp(lambda x, f, g, b: _res_ln(x, f, g, b, s), x, f, g, b)
        return vjp(dh)

    return _tile_call(name, fn, T // tm, [_rows(x, tm), _rows(f, tm), _full(g), _full(b), _rows(dh, tm)],
                      [((T, D_MODEL), f32, (tm, D_MODEL), lambda i: (i, 0)), ((T, D_MODEL), bf16, (tm, D_MODEL), lambda i: (i, 0))],
                      [((1, D_MODEL), f32), ((1, D_MODEL), f32)])


FF_BLK = D_FF // 2


def _pair_gate_up(w):
    d = w.shape[0]
    return jnp.stack([w[:, :D_FF].reshape(d, -1, FF_BLK), w[:, D_FF:].reshape(d, -1, FF_BLK)], axis=2).reshape(d, 2 * D_FF)


def _swiglu_blk(u):
    return _silu(u[:, :FF_BLK]) * u[:, FF_BLK:]


def _mm_spec(tm, tn):
    return dict(compiler_params=pltpu.CompilerParams(dimension_semantics=("parallel", "arbitrary"), vmem_limit_bytes=VMEM_LIMIT))


def _gu_act(name, xb, w_pair):
    T, D = xb.shape
    tm = _pick(T, (_tiles(name, (512,))[0], 512, 256, 128))

    def body(x_ref, w_ref, u_ref, a_ref):
        r = lax.dot_general(x_ref[...], w_ref[...], (((1,), (0,)), ((), ())), preferred_element_type=f32)
        u_ref[...] = r.astype(bf16)
        a_ref[...] = _swiglu_blk(r).astype(bf16)

    return pl.pallas_call(
        body, name=name, grid=(T // tm, D_FF // FF_BLK),
        in_specs=[pl.BlockSpec((tm, D), lambda i, j: (i, 0)), pl.BlockSpec((D, 2 * FF_BLK), lambda i, j: (0, j))],
        out_specs=[pl.BlockSpec((tm, 2 * FF_BLK), lambda i, j: (i, j)), pl.BlockSpec((tm, FF_BLK), lambda i, j: (i, j))],
        out_shape=[jax.ShapeDtypeStruct((T, 2 * D_FF), bf16), jax.ShapeDtypeStruct((T, D_FF), bf16)],
        **_mm_spec(tm, FF_BLK))(xb, w_pair)


def _down_dx_act_bwd(name, df, w_down_t, u):
    T, D = df.shape
    tm = _pick(T, (_tiles(name, (256,))[0], 256, 128))

    def body(df_ref, w_ref, u_ref, du_ref):
        da = lax.dot_general(df_ref[...], w_ref[...], (((1,), (0,)), ((), ())), preferred_element_type=f32)
        _, vjp = jax.vjp(_swiglu_blk, u_ref[...].astype(f32))
        du_ref[...] = vjp(da)[0].astype(bf16)

    return pl.pallas_call(
        body, name=name, grid=(T // tm, D_FF // FF_BLK),
        in_specs=[pl.BlockSpec((tm, D), lambda i, j: (i, 0)), pl.BlockSpec((D, FF_BLK), lambda i, j: (0, j)),
                  pl.BlockSpec((tm, 2 * FF_BLK), lambda i, j: (i, j))],
        out_specs=pl.BlockSpec((tm, 2 * FF_BLK), lambda i, j: (i, j)),
        out_shape=jax.ShapeDtypeStruct((T, 2 * D_FF), bf16),
        **_mm_spec(tm, FF_BLK))(df, w_down_t, u)


def _mm_res_ln(name, a, w, x, g, b, s):
    T, K = a.shape
    tm = _pick(T, (_tiles(name, (512,))[0], 512, 256, 128))

    def body(a_ref, w_ref, x_ref, g_ref, b_ref, f_ref, h_ref, hb_ref):
        f = lax.dot_general(a_ref[...], w_ref[...], (((1,), (0,)), ((), ())), preferred_element_type=f32)
        h = _res_ln(x_ref[...], f, g_ref[...], b_ref[...], s)
        f_ref[...] = f.astype(bf16)
        h_ref[...] = h
        hb_ref[...] = h.astype(bf16)

    row = pl.BlockSpec((tm, D_MODEL), lambda i: (i, 0))
    vec = pl.BlockSpec((1, D_MODEL), lambda i: (0, 0))
    return pl.pallas_call(
        body, name=name, grid=(T // tm,),
        in_specs=[pl.BlockSpec((tm, K), lambda i: (i, 0)), pl.BlockSpec((K, D_MODEL), lambda i: (0, 0)), row, vec, vec],
        out_specs=[row, row, row],
        out_shape=[jax.ShapeDtypeStruct((T, D_MODEL), bf16), jax.ShapeDtypeStruct((T, D_MODEL), f32), jax.ShapeDtypeStruct((T, D_MODEL), bf16)],
        compiler_params=pltpu.CompilerParams(dimension_semantics=("parallel",), vmem_limit_bytes=VMEM_LIMIT))(a, w, x, g, b)


def _hm_spec(tm, w=DN_DK):
    return ((DN_HEADS, tm, w), lambda i: (0, i, 0))


def _prev(i):
    return jnp.maximum(i - 1, 0)


def _dn_front(first, xc, xp, ba, cw, hp):
    tm = xc.shape[0]
    nc = tm // DN_CHUNK
    outs = _wy(*[t.reshape(DN_HEADS * nc, DN_CHUNK, DN_DK) for t in _dn_pre(first, xc, xp, ba, cw, hp)])
    return tuple(o.reshape(DN_HEADS, tm, o.shape[-1]) for o in outs)


def _dn_front_ins(p, pba, cw, hp, tm):
    W3 = 3 * DN_HEADS * DN_DK
    return [(p, (tm, W3), lambda i: (i, 0)), (p, (HALO, W3), lambda i: (jnp.maximum(i * (tm // HALO) - 1, 0), 0)),
            (pba, (tm, LANES), lambda i: (i, 0)), _full(cw), _full(hp)]


def _dn_front_fwd(name, p, pba, cw, hp):
    T = p.shape[0]
    tm = _row_tile(T, 256)
    blk, im = _hm_spec(tm)
    blk2, im2 = _hm_spec(tm, DN_CHUNK)
    full = (DN_HEADS, T, DN_DK)
    outs = [(full, bf16, blk, im), ((DN_HEADS, T, DN_CHUNK), bf16, blk2, im2), (full, f32, blk, im),
            (full, bf16, blk, im), (full, bf16, blk, im), (full, f32, blk, im)]
    return _tile_call(name, lambda i, *a: _dn_front(i == 0, *a), T // tm, _dn_front_ins(p, pba, cw, hp, tm), outs)


def _dn_front_bwd(name, p, pba, cw, hp, cts):
    T = p.shape[0]
    tm = _row_tile(T, 256)
    n = T // tm
    W3 = 3 * DN_HEADS * DN_DK
    blk, im = _hm_spec(tm)
    blk2, im2 = _hm_spec(tm, DN_CHUNK)
    ins = _dn_front_ins(p, pba, cw, hp, tm) + [(c, blk2 if c.shape[-1] == DN_CHUNK else blk, im) for c in cts]

    def fn(i, xc, xp, ba, cw, hp, *cts):
        _, vjp = jax.vjp(lambda *a: _dn_front(i == 0, *a), xc, xp, ba, cw, hp)
        return vjp(tuple(cts))

    return _tile_call(name, fn, n, ins,
                      [((T, W3), bf16, (tm, W3), lambda i: (i, 0)), ((n * HALO, W3), f32, (HALO, W3), lambda i: (i, 0)),
                       ((T, LANES), f32, (tm, LANES), lambda i: (i, 0))],
                      [(tuple(cw.shape), f32), (tuple(hp.shape), f32)])


def _scan_fwd(name, qd, qk, u, w, kt, egl):
    T = qd.shape[1]
    C = DN_CHUNK
    U = _pick(T // C, (SCAN_CHUNKS, 1))
    n = T // (C * U)

    def body(qd_ref, qk_ref, u_ref, w_ref, kt_ref, egl_ref, o_ref, sall_ref, s_ref):
        i = pl.program_id(0)

        @pl.when(i == 0)
        def _():
            s_ref[...] = jnp.zeros(s_ref.shape, f32)

        S = s_ref[...]
        for j in range(U):
            sl = pl.ds(j * C, C)
            sall_ref[:, j] = S
            o, S = _scan_step(S, *[r[:, sl, :] for r in (qd_ref, qk_ref, u_ref, w_ref, kt_ref, egl_ref)])
            o_ref[:, sl, :] = o
        s_ref[...] = S

    blk, im = _hm_spec(C * U)
    blk2, im2 = _hm_spec(C * U, C)
    return pl.pallas_call(
        body, name=name, grid=(n,),
        in_specs=[pl.BlockSpec(blk, im), pl.BlockSpec(blk2, im2)] + [pl.BlockSpec(blk, im)] * 4,
        out_specs=[pl.BlockSpec(blk, im), pl.BlockSpec((DN_HEADS, U, DN_DK, DN_DK), lambda i: (0, i, 0, 0))],
        out_shape=[jax.ShapeDtypeStruct((DN_HEADS, T, DN_DK), f32), jax.ShapeDtypeStruct((DN_HEADS, n * U, DN_DK, DN_DK), f32)],
        scratch_shapes=[pltpu.VMEM((DN_HEADS, DN_DK, DN_DK), f32)],
        compiler_params=pltpu.CompilerParams(dimension_semantics=("arbitrary",), vmem_limit_bytes=VMEM_LIMIT),
    )(qd, qk, u, w, kt, egl)


def _scan_bwd(name, qd, qk, u, w, kt, egl, s_all, do):
    T = qd.shape[1]
    C = DN_CHUNK
    U = _pick(T // C, (SCAN_CHUNKS, 1))
    n = T // (C * U)

    def body(qd_ref, qk_ref, u_ref, w_ref, kt_ref, egl_ref, sall_ref, do_ref,
             dqd_ref, dqk_ref, du_ref, dw_ref, dkt_ref, degl_ref, ds_ref):
        i = pl.program_id(0)

        @pl.when(i == 0)
        def _():
            ds_ref[...] = jnp.zeros(ds_ref.shape, f32)

        dS = ds_ref[...]
        for j in reversed(range(U)):
            sl = pl.ds(j * C, C)
            args = [r[:, sl, :].astype(f32) for r in (qd_ref, qk_ref, u_ref, w_ref, kt_ref, egl_ref)]
            _, vjp = jax.vjp(_scan_step, sall_ref[:, j], *args)
            dS, *cts = vjp((do_ref[:, sl, :], dS))
            for r, v in zip((dqd_ref, dqk_ref, du_ref, dw_ref, dkt_ref, degl_ref), cts):
                r[:, sl, :] = v
        ds_ref[...] = dS

    blk = (DN_HEADS, C * U, DN_DK)
    blk2 = (DN_HEADS, C * U, C)
    rim = lambda i: (0, n - 1 - i, 0)
    sp, sp2 = pl.BlockSpec(blk, rim), pl.BlockSpec(blk2, rim)
    full, full2 = jax.ShapeDtypeStruct((DN_HEADS, T, DN_DK), f32), jax.ShapeDtypeStruct((DN_HEADS, T, C), f32)
    return pl.pallas_call(
        body, name=name, grid=(n,),
        in_specs=[sp, sp2, sp, sp, sp, sp, pl.BlockSpec((DN_HEADS, U, DN_DK, DN_DK), lambda i: (0, n - 1 - i, 0, 0)), sp],
        out_specs=[sp, sp2, sp, sp, sp, sp],
        out_shape=[full, full2, full, full, full, full],
        scratch_shapes=[pltpu.VMEM((DN_HEADS, DN_DK, DN_DK), f32)],
        compiler_params=pltpu.CompilerParams(dimension_semantics=("arbitrary",), vmem_limit_bytes=VMEM_LIMIT),
    )(qd, qk, u, w, kt, egl, s_all, do)


def _dn_post_fwd(name, o, p, nw):
    T = p.shape[0]
    tm = _row_tile(T, 512)
    blk, im = _hm_spec(tm)
    W = DN_HEADS * DN_DK
    return _tile_call(name, lambda i, o, z, nw: _dn_post(o, z, nw), T // tm,
                      [(o, blk, im), (p, (tm, W), lambda i: (i, 1536 // W)), _full(nw)],
                      [((T, W), f32, (tm, W), lambda i: (i, 0))])[0]


def _dn_post_bwd(name, o, p, nw, dout):
    T = p.shape[0]
    tm = _row_tile(T, 512)
    blk, im = _hm_spec(tm)
    W = DN_HEADS * DN_DK

    def fn(i, o, z, nw, dout):
        _, vjp = jax.vjp(_dn_post, o, z, nw)
        return vjp(dout)

    return _tile_call(name, fn, T // tm,
                      [(o, blk, im), (p, (tm, W), lambda i: (i, 1536 // W)), _full(nw), _rows(dout, tm)],
                      [((DN_HEADS, T, DN_DK), f32, blk, im), ((T, W), f32, (tm, W), lambda i: (i, 0))],
                      [((1, DN_DK), f32)])


def _swa_tile(T, big):
    return _pick(T, (2 * WINDOW, WINDOW)) if big else WINDOW


@functools.partial(jax.custom_vjp, nondiff_argnums=(1,))
def _heads_split(x, nh):
    return jnp.stack([x[:, SWA_DH * h:SWA_DH * (h + 1)] for h in range(nh)], axis=0)


@jax.custom_vjp
def _heads_merge(y):
    return jnp.concatenate([y[h] for h in range(y.shape[0])], axis=1)


_heads_split.defvjp(lambda x, nh: (_heads_split(x, nh), None), lambda nh, _, g: (_heads_merge(g),))
_heads_merge.defvjp(lambda y: (_heads_merge(y), None), lambda _, g: (_heads_split(g, g.shape[1] // SWA_DH),))


def _swa_rows(first, qp, kc, kp, vc, vp, snk):
    kv = [_heads_split(t, SWA_KV_HEADS) for t in (kc, kp, vc, vp)]
    return _heads_merge(_swa(first, _heads_split(qp, SWA_HEADS), *kv, snk))


def _swa_ins(p, snk, W):
    qw, kw = SWA_HEADS * SWA_DH, SWA_KV_HEADS * SWA_DH
    return [(p, (W, qw), lambda i: (i, 2048 // qw)),
            (p, (W, kw), lambda i: (i, 6144 // kw)), (p, (W, kw), lambda i: (_prev(i), 6144 // kw)),
            (p, (W, kw), lambda i: (i, 6272 // kw)), (p, (W, kw), lambda i: (_prev(i), 6272 // kw)), _full(snk)]


def _swa_fwd(name, p, snk):
    T = p.shape[0]
    W = _swa_tile(T, True)
    qw = SWA_HEADS * SWA_DH
    return _tile_call(name, lambda i, *a: _swa_rows(i == 0, *a), T // W, _swa_ins(p, snk, W),
                      [((T, qw), f32, (W, qw), lambda i: (i, 0))])[0]


def _swa_bwd(name, p, snk, do):
    T = p.shape[0]
    W = _swa_tile(T, False)
    qw, kw = SWA_HEADS * SWA_DH, SWA_KV_HEADS * SWA_DH

    def fn(i, qp, k_c, k_p, v_c, v_p, snk, do):
        _, vjp = jax.vjp(lambda *a: _swa_rows(i == 0, *a), qp, k_c, k_p, v_c, v_p, snk)
        return vjp(do)

    kv_out = ((T, kw), f32, (W, kw), lambda i: (i, 0))
    return _tile_call(name, fn, T // W, _swa_ins(p, snk, W) + [(do, (W, qw), lambda i: (i, 0))],
                      [((T, qw), bf16, (W, qw), lambda i: (i, 0)), kv_out, kv_out, kv_out, kv_out], [(tuple(snk.shape), f32)])


def _xa_fwd(name, p, kv):
    T = p.shape[0]
    tm = _row_tile(T, 512)
    W = XA_HEADS * XA_DH
    return _tile_call(name, lambda i, q, kv: _xa(q, kv), T // tm, [(p, (tm, W), lambda i: (i, 2560 // W)), _full(kv)],
                      [((T, W), f32, (tm, W), lambda i: (i, 0))])[0]


def _xa_bwd(name, p, kv, do):
    T = p.shape[0]
    tm = _row_tile(T, 512)
    W = XA_HEADS * XA_DH

    def fn(i, q, kv, do):
        _, vjp = jax.vjp(_xa, q, kv)
        return vjp(do)

    return _tile_call(name, fn, T // tm, [(p, (tm, W), lambda i: (i, 2560 // W)), _full(kv), _rows(do, tm)],
                      [((T, W), bf16, (tm, W), lambda i: (i, 0))], [(tuple(kv.shape), f32)])


def _merge_fwd(name, o_dn, o_sw, o_xa, p, wb):
    T = p.shape[0]
    tm = _row_tile(T, 512)
    GW = N_BRANCH * D_MODEL
    return _tile_call(name, lambda i, a, b, c, g, w: _merge(a, b, c, g, w.astype(f32)), T // tm,
                      [_rows(o_dn, tm), _rows(o_sw, tm), _rows(o_xa, tm), (p, (tm, GW), lambda i: (i, 1)), _full(wb)],
                      [((T, D_MODEL), bf16, (tm, D_MODEL), lambda i: (i, 0))])[0]


def _merge_bwd(name, o_dn, o_sw, o_xa, p, wb, dm):
    T = p.shape[0]
    tm = _row_tile(T, 512)
    GW = N_BRANCH * D_MODEL

    def fn(i, a, b, c, g, w, dm):
        _, vjp = jax.vjp(_merge, a, b, c, g, w.astype(f32))
        return vjp(dm)

    bo = ((T, BRANCH_W), bf16, (tm, BRANCH_W), lambda i: (i, 0))
    return _tile_call(name, fn, T // tm,
                      [_rows(o_dn, tm), _rows(o_sw, tm), _rows(o_xa, tm), (p, (tm, GW), lambda i: (i, 1)), _full(wb), _rows(dm, tm)],
                      [bo, bo, bo, ((T, GW), bf16, (tm, GW), lambda i: (i, 0))], [(tuple(wb.shape), f32)])


def _assemble_dp(name, dxc, dxp, dz, dswq, dxaq, dgates, dkc, dkp, dvc, dvp, dba):
    T = dz.shape[0]
    tm = _row_tile(T, 256)
    n = T // tm
    r = tm // WINDOW
    nb = T // WINDOW

    def fn(i, dxc, dxp, dz, dswq, dxaq, dgates, dkc, dvc, dba, *nxt):
        halo = jnp.where(i == n - 1, 0.0, dxp)
        dqkv = dxc + jnp.concatenate([jnp.zeros((tm - HALO, dxc.shape[1]), f32), halo], axis=0)
        shifted = [jnp.concatenate([jnp.where(i * r + 1 + b <= nb - 1, blk, 0.0) for b, blk in enumerate(half)], axis=0)
                   for half in (nxt[:r], nxt[r:])]
        parts = [dqkv, dz, dswq, dxaq, dgates, dkc + shifted[0], dvc + shifted[1], dba, jnp.zeros((tm, LANES), f32)]
        return jnp.concatenate(parts, axis=1)

    ins = [_rows(dxc, tm), (dxp, (HALO, dxp.shape[1]), lambda i: (jnp.minimum(i + 1, n - 1), 0))]
    ins += [_rows(t, tm) for t in (dz, dswq, dxaq, dgates, dkc, dvc, dba)]
    for t in (dkp, dvp):
        ins += [(t, (WINDOW, t.shape[1]), lambda i, b=b: (jnp.minimum(i * r + 1 + b, nb - 1), 0)) for b in range(r)]
    return _tile_call(name, fn, n, ins, [((T, D_INP), bf16, (tm, D_INP), lambda i: (i, 0))])[0]


def _loss_grad(name, y, tgt):
    T = y.shape[0]
    tm = _row_tile(T, 512)

    def fn(i, y, t):
        e = y - t
        part = 0.5 * jnp.sum(jnp.mean(e * e, axis=-1, keepdims=True), axis=0, keepdims=True)
        return e * (1.0 / D_MODEL), jnp.broadcast_to(part, (1, LANES))

    return _tile_call(name, fn, T // tm, [_rows(y, tm), _rows(tgt, tm)],
                      [((T, D_MODEL), f32, (tm, D_MODEL), lambda i: (i, 0))], [((1, LANES), f32)])


def _mem_ln_fwd(name, mem, g, b):
    M = mem.shape[0]
    return _tile_call(name, lambda i, m, g, b: (_ln(m, g, b),), 1, [_full(mem), _full(g), _full(b)],
                      [((M, D_MODEL), bf16, (M, D_MODEL), lambda i: (0, 0))])[0]


def _mem_ln_bwd(name, mem, g, b, dmn):
    def fn(i, m, g, b, d):
        _, vjp = jax.vjp(lambda g, b: _ln(m, g, b), g, b)
        return vjp(d)

    return _tile_call(name, fn, 1, [_full(mem), _full(g), _full(b), _full(dmn)], [], [((1, D_MODEL), f32), ((1, D_MODEL), f32)])


W_IN_RANGES = ((0, 1536), (1544, 2056), (2056, 2568), (2824, 3336), (3336, 6408), (2568, 2696), (2696, 2824), (1536, 1544))
W_IN_OFFS = tuple(sum(b - a for a, b in W_IN_RANGES[:k]) for k in range(len(W_IN_RANGES)))


def _pad_w_in(cols):
    parts = [cols(a, b) for a, b in W_IN_RANGES]
    return jnp.concatenate(parts + [jnp.zeros((parts[0].shape[0], D_INP - D_IN), parts[0].dtype)], axis=1)


def _unpad_dw_in(d, lo=0, hi=D_IN):
    parts = []
    for (a, b), off in sorted(zip(W_IN_RANGES, W_IN_OFFS)):
        a2, b2 = max(a, lo), min(b, hi)
        if a2 < b2:
            parts.append(d[:, off + a2 - a:off + b2 - a])
    return jnp.concatenate(parts, axis=1)


def _shard_cols(shards, l):
    n = shards[0].shape[-1]

    def cols(a, b):
        parts = [shards[s][l][:, max(a, s * n) - s * n:min(b, (s + 1) * n) - s * n] for s in range(N_SHARD)
                 if max(a, s * n) < min(b, (s + 1) * n)]
        return parts[0] if len(parts) == 1 else jnp.concatenate(parts, axis=1)

    return cols


def _lane_row(v, rows=8):
    out = jnp.zeros((rows, LANES), f32)
    return out.at[0, :v.shape[0]].set(v)


def _layer_fwd(l, x, xb, mem_nb, W):
    n = lambda s: f"l{l}_{s}"
    sv = {}
    u1, a1 = _gu_act(n("ffn1_gu"), xb, W['ffn1_w_gu_p'])
    f1, h1, h1b = _mm_res_ln(n("ffn1_down"), a1, W['ffn1_w_down'], x, W['ln_g'][0:1], W['ln_b'][0:1], 0.5)
    p = _mm(n("w_in"), h1b, W['w_inp'], 'nn', out_dtype=bf16)
    pba = _mm(n("w_in_ba"), h1b, W['w_ba'], 'nn')
    qd, qk, u, w, kt, egl = _dn_front_fwd(n("dn_front"), p, pba, W['conv_w'], W['hp'])
    o_raw, s_all = _scan_fwd(n("dn_scan"), qd, qk, u, w, kt, egl)
    o_dn = _dn_post_fwd(n("dn_post"), o_raw, p, W['norm_w'])
    o_sw = _swa_fwd(n("swa"), p, W['snk'])
    kv = _mm(n("mem_kv"), mem_nb, W['w_mem_kv'], 'nn')
    o_xa = _xa_fwd(n("xa"), p, kv)
    merged = _merge_fwd(n("merge"), o_dn, o_sw, o_xa, p, W['w_branch'])
    mo, h2, h2b = _mm_res_ln(n("w_out"), merged, W['w_out'], h1, W['ln_g'][1:2], W['ln_b'][1:2], 1.0)
    u2, a2 = _gu_act(n("ffn2_gu"), h2b, W['ffn2_w_gu_p'])
    f2, h3, h3b = _mm_res_ln(n("ffn2_down"), a2, W['ffn2_w_down'], h2, W['ln_g'][2:3], W['ln_b'][2:3], 0.5)
    sv = dict(x=x, xb=xb, u1=u1, a1=a1, f1=f1, h1=h1, h1b=h1b, p=p, pba=pba, wy=(qd, qk, u, w, kt, egl),
              o_raw=o_raw, s_all=s_all, o_dn=o_dn, o_sw=o_sw, kv=kv, o_xa=o_xa, merged=merged,
              mo=mo, h2=h2, h2b=h2b, u2=u2, a2=a2, f2=f2)
    return h3, h3b, sv


def _ffn_bwd(n, tag, x, xb, u, a, f, g, b, w_gu_t, w_down_t, dh):
    dx_a, df, dg, db = _res_ln_bwd(n(f"{tag}_ln_bwd"), x, f, g, b, 0.5, dh)
    d_down = _mm(n(f"{tag}_down_dw"), a, df, 'tn')
    du = _down_dx_act_bwd(n(f"{tag}_down_dx"), df, w_down_t, u)
    d_gu = _mm(n(f"{tag}_gu_dw"), xb, du, 'tn', out_cols=(FF_BLK, lambda j: (j % 2) * (D_FF // FF_BLK) + j // 2))
    dx = _mm(n(f"{tag}_gu_dx"), du, w_gu_t, 'nn', add=dx_a)
    return dx, d_gu, d_down, dg, db


def _layer_bwd(l, sv, mem_nb, W, dh3, dmem_acc):
    n = lambda s: f"l{l}_{s}"
    G = {}
    dh2, G['ffn2_w_gu'], G['ffn2_w_down'], dg2, db2 = _ffn_bwd(
        n, "ffn2", sv['h2'], sv['h2b'], sv['u2'], sv['a2'], sv['f2'], W['ln_g'][2:3], W['ln_b'][2:3], W['ffn2_w_gu_t'], W['ffn2_w_down_t'], dh3)
    dh1_a, dmo, dg1, db1 = _res_ln_bwd(n("ln1_bwd"), sv['h1'], sv['mo'], W['ln_g'][1:2], W['ln_b'][1:2], 1.0, dh2)
    G['w_out'] = _mm(n("w_out_dw"), sv['merged'], dmo, 'tn')
    dmerged = _mm(n("w_out_dx"), dmo, W['w_out_t'], 'nn')
    p = sv['p']
    do_dn, do_sw, do_xa, dgates, G['w_branch'] = _merge_bwd(n("merge_bwd"), sv['o_dn'], sv['o_sw'], sv['o_xa'], p, W['w_branch'], dmerged)
    dxaq, dkv = _xa_bwd(n("xa_bwd"), p, sv['kv'], do_xa)
    dkv = dkv.astype(bf16)
    G['w_mem_kv'] = _mm(n("mem_kv_dw"), mem_nb, dkv, 'tn')
    dmem_n = _mm(n("mem_kv_dx"), dkv, W['w_mem_kv_t'], 'nn', add=dmem_acc)
    dswq, dkc, dkp, dvc, dvp, dsnk = _swa_bwd(n("swa_bwd"), p, W['snk'], do_sw)
    do_raw, dz, dnw = _dn_post_bwd(n("dn_post_bwd"), sv['o_raw'], p, W['norm_w'], do_dn)
    cts = _scan_bwd(n("dn_scan_bwd"), *sv['wy'], sv['s_all'], do_raw)
    dxc, dxp, dba, dcw, dhp = _dn_front_bwd(n("dn_front_bwd"), p, sv['pba'], W['conv_w'], W['hp'], cts)
    dp = _assemble_dp(n("dp"), dxc, dxp, dz, dswq, dxaq, dgates, dkc, dkp, dvc, dvp, dba)
    G['w_in_p'] = _mm(n("w_in_dw"), sv['h1b'], dp, 'tn')
    dh1 = _mm(n("w_in_dx"), dp, W['w_inp_t'], 'nn', add=dh1_a)
    dx, G['ffn1_w_gu'], G['ffn1_w_down'], dg0, db0 = _ffn_bwd(
        n, "ffn1", sv['x'], sv['xb'], sv['u1'], sv['a1'], sv['f1'], W['ln_g'][0:1], W['ln_b'][0:1], W['ffn1_w_gu_t'], W['ffn1_w_down_t'], dh1)
    G['ln_g'] = jnp.concatenate([dg0, dg1, dg2], axis=0)
    G['ln_b'] = jnp.concatenate([db0, db1, db2], axis=0)
    G['dn_conv_w'] = dcw
    G['dn_a_log'] = dhp[0, :DN_HEADS]
    G['dn_dt_bias'] = dhp[1, :DN_HEADS]
    G['dn_norm_w'] = dnw[0]
    G['swa_sinks'] = dsnk[:, 0]
    return dx, dmem_n, G


def _local_step(x, mem, tgt, Wf):
    mem_g, mem_b = Wf['mem_ln_g'][None, :], Wf['mem_ln_b'][None, :]
    mem_nb = _mem_ln_fwd("mem_ln", mem, mem_g, mem_b)

    def paired(k, l):
        return Wf[k + '_p'][l] if k + '_p' in Wf else _pair_gate_up(Wf[k][l])

    def w_in_cols(l):
        return _shard_cols(Wf['w_in_sh'], l) if 'w_in_sh' in Wf else (lambda a, b: Wf['w_in'][l][:, a:b])

    layers = []
    for l in range(DEPTH):
        layers.append(dict(
            ln_g=Wf['ln_g'][l], ln_b=Wf['ln_b'][l], ffn1_w_gu_p=paired('ffn1_w_gu', l), ffn1_w_down=Wf['ffn1_w_down'][l],
            w_inp=_pad_w_in(w_in_cols(l)), conv_w=Wf['dn_conv_w'][l],
            hp=jnp.zeros((8, LANES), f32).at[0, :DN_HEADS].set(Wf['dn_a_log'][l]).at[1, :DN_HEADS].set(Wf['dn_dt_bias'][l]),
            norm_w=Wf['dn_norm_w'][l][None, :], snk=jnp.broadcast_to(Wf['swa_sinks'][l][:, None], (SWA_HEADS, LANES)),
            w_mem_kv=Wf['w_mem_kv'][l], w_branch=Wf['w_branch'][l], w_out=Wf['w_out'][l],
            ffn2_w_gu_p=paired('ffn2_w_gu', l), ffn2_w_down=Wf['ffn2_w_down'][l]))
        layers[l]['w_ba'] = layers[l]['w_inp'][:, 6400:6400 + LANES]
        for k in ('ffn1_w_gu_p', 'ffn1_w_down', 'w_inp', 'w_mem_kv', 'w_out', 'ffn2_w_gu_p', 'ffn2_w_down'):
            layers[l][k.replace('_p', '') + '_t'] = layers[l][k].T
    h, hb = x, x.astype(bf16)
    saved = []
    for l in range(DEPTH):
        h, hb, sv = _layer_fwd(l, h, hb, mem_nb, layers[l])
        saved.append(sv)
    dh, loss_row = _loss_grad("loss", h, tgt)
    grads = [None] * DEPTH
    dmem_n = None
    for l in reversed(range(DEPTH)):
        dh, dmem_n, grads[l] = _layer_bwd(l, saved[l], mem_nb, layers[l], dh, dmem_n)
    dmg, dmb = _mem_ln_bwd("mem_ln_bwd", mem, mem_g, mem_b, dmem_n)
    G = {k: jnp.stack([grads[l][k] for l in range(DEPTH)], axis=0) for k in grads[0]}
    G['mem_ln_g'], G['mem_ln_b'] = dmg[0], dmb[0]
    return loss_row, dh, G


SEG_ALIGN = 2048


def _round_up(n, m):
    return (n + m - 1) // m * m


def _layout(names, sizes, mult):
    off, table = 0, {}
    for nm in names:
        table[nm] = (off, sizes[nm])
        off += _round_up(sizes[nm], SEG_ALIGN)
    return table, _round_up(off // LANES, mult)


def _pack(table, rows, flat):
    names = list(table)
    lead = flat[names[0]].shape[:-1]
    parts, pos = [], 0
    for nm in names:
        off, size = table[nm]
        if off > pos:
            parts.append(jnp.zeros(lead + (off - pos,), flat[nm].dtype))
        parts.append(flat[nm])
        pos = off + size
    total = rows * LANES
    if total > pos:
        parts.append(jnp.zeros(lead + (total - pos,), parts[-1].dtype))
    return jnp.concatenate(parts, axis=-1).reshape(lead + (rows, LANES))


def _unpack(table, packed, nm):
    off, size = table[nm]
    flat = packed.reshape(packed.shape[:-2] + (-1,))
    return flat[..., off:off + size]


def _split_shards(full, ax):
    shp = full.shape
    t = full.reshape(shp[:ax] + (N_SHARD, shp[ax] // N_SHARD) + shp[ax + 1:])
    return jnp.moveaxis(t, ax, 0)


def _join_shards(sh4, ax):
    return jnp.concatenate([sh4[s] for s in range(N_SHARD)], axis=ax)


ANY = pl.BlockSpec(memory_space=pl.ANY)


def _me():
    return lax.axis_index("x"), lax.axis_index("y"), lax.axis_index("c")


def _comm_call(name, body, arrays, out_shapes, n_sem):
    n = len(arrays)
    scratch = [pltpu.SemaphoreType.DMA((n, n_sem)), pltpu.SemaphoreType.DMA((n, n_sem))]
    return pl.pallas_call(
        body, name=name, out_shape=out_shapes, in_specs=[ANY] * n, out_specs=[ANY] * n, scratch_shapes=scratch,
        compiler_params=pltpu.CompilerParams(has_side_effects=True),
    )(*arrays)


def _all_gather(name, xs):
    n = len(xs)

    def split(shape):
        ax = next(i for i, d in enumerate(shape) if d % 2 == 0)
        return ax, shape[ax] // 2

    def body(*refs):
        x_refs, out_refs = refs[:n], refs[n:2 * n]
        send_sems, recv_sems = refs[2 * n:]
        mx, my, mc = _me()
        me_s = 2 * mx + my
        across_x, across_y, diag = (1 - mx, my), (mx, 1 - my), (1 - mx, 1 - my)
        sid = lambda chip: 2 * chip[0] + chip[1]

        def part(ref, a, which):
            if which is None:
                return ref
            ax, size = split(xs[a].shape[1:])
            return ref.at[(slice(None),) * ax + (pl.ds(which * size, size),)]

        def copy(a, k, shard, half, which, to, src=None):
            dst = part(out_refs[a].at[shard, half], a, which)
            return pltpu.make_async_remote_copy(src_ref=dst if src is None else part(src, a, which), dst_ref=dst,
                                                send_sem=send_sems.at[a, k], recv_sem=recv_sems.at[a, k], device_id=to, device_id_type=MESH)

        sibling, here = (mx, my, 1 - mc), (mx, my, mc)
        sent = []
        for a in range(n):
            sent += [copy(a, 0, me_s, mc, None, (*across_x, mc), src=x_refs[a].at[mc]),
                     copy(a, 1, me_s, mc, None, (*across_y, mc), src=x_refs[a].at[mc])]
        for cp in sent:
            cp.start()
        landing = [(0, across_x, None, across_y), (1, across_y, None, across_x), (2, diag, 0, None), (3, diag, 1, None)]
        for k, origin, which, relay_to in landing:
            for a in range(n):
                copy(a, k, sid(origin), mc, which, here).wait_recv()
                if relay_to is not None:
                    sent.append(copy(a, 2 + k, sid(origin), mc, k, (*relay_to, mc)))
                    sent[-1].start()
                sent.append(copy(a, 4 + k, sid(origin), mc, which, sibling))
                sent[-1].start()
        for k, origin, which, _ in landing:
            for a in range(n):
                copy(a, 4 + k, sid(origin), 1 - mc, which, here).wait_recv()
        for cp in sent:
            cp.wait_send()

    return _comm_call(name, body, xs, [jax.ShapeDtypeStruct((N_SHARD,) + x.shape, x.dtype) for x in xs], 8)


def _pair_exchange(name, items):
    n = len(items)

    def body(*refs):
        src_refs, dst_refs = refs[:n], refs[n:2 * n]
        send_sems, recv_sems = refs[2 * n:]
        mx, my, mc = _me()
        cps = [pltpu.make_async_remote_copy(src_ref=src_refs[a].at[s, 1 - mc], dst_ref=dst_refs[a].at[s], send_sem=send_sems.at[a, s],
                                            recv_sem=recv_sems.at[a, s], device_id=(mx, my, 1 - mc), device_id_type=MESH)
               for a in range(n) for s in range(N_SHARD)]
        for cp in cps:
            cp.start()
        for cp in cps:
            cp.wait()

    return _comm_call(name, body, items, [jax.ShapeDtypeStruct((N_SHARD,) + t.shape[2:], t.dtype) for t in items], N_SHARD)


def _chip_exchange(name, parts):
    n = len(parts)

    def body(*refs):
        p_refs, dst_refs = refs[:n], refs[n:2 * n]
        send_sems, recv_sems = refs[2 * n:]
        mx, my, mc = _me()
        chips = [(1 - mx, my), (mx, 1 - my), (1 - mx, 1 - my)]
        cps = [pltpu.make_async_remote_copy(src_ref=p_refs[a].at[2 * cx + cy], dst_ref=dst_refs[a].at[j], send_sem=send_sems.at[a, j],
                                            recv_sem=recv_sems.at[a, j], device_id=(cx, cy, mc), device_id_type=MESH)
               for a in range(n) for j, (cx, cy) in enumerate(chips)]
        for cp in cps:
            cp.start()
        for cp in cps:
            cp.wait()

    return _comm_call(name, body, parts, [jax.ShapeDtypeStruct((3,) + t.shape[1:], t.dtype) for t in parts], 3)


def _pair_swap(name, reds):
    n = len(reds)

    def body(*refs):
        r_refs, out_refs = refs[:n], refs[n:2 * n]
        send_sems, recv_sems = refs[2 * n:]
        mx, my, mc = _me()
        cps = [pltpu.make_async_remote_copy(src_ref=r_refs[a], dst_ref=out_refs[a], send_sem=send_sems.at[a, 0],
                                            recv_sem=recv_sems.at[a, 0], device_id=(mx, my, 1 - mc), device_id_type=MESH)
               for a in range(n)]
        for cp in cps:
            cp.start()
        for cp in cps:
            cp.wait()

    return _comm_call(name, body, reds, [jax.ShapeDtypeStruct(t.shape, t.dtype) for t in reds], 1)


EW_BLOCK_BYTES = 1 << 20


def _ew_call(name, fn, ins, n_out, out_dtype=f32):
    shape = ins[0].shape
    last = shape[-1]
    flat = [a.reshape(-1, last) for a in ins]
    R = flat[0].shape[0]
    cands = tuple(c for c in (4096, 2048, 1024, 512, 256, 128, 64, 32, 16, 8) if c * last * 4 <= EW_BLOCK_BYTES)
    tr = _pick(R, cands)
    res = _tile_call(name, lambda i, *a: fn(*a), R // tr, [(a, (tr, last), lambda i: (i, 0)) for a in flat],
                     [((R, last), out_dtype, (tr, last), lambda i: (i, 0))] * n_out)
    return [r.reshape(shape) for r in res]


def _adamw(g, w, m, v):
    m = B1 * m + (1.0 - B1) * g
    v = B2 * v + (1.0 - B2) * jnp.square(g)
    m_hat = m / (1.0 - B1 ** STEP)
    v_hat = v / (1.0 - B2 ** STEP)
    return -LR * (m_hat / (jnp.sqrt(v_hat) + EPS) + WD * w), m, v


def _adamw_call(name, mine, theirs, w, m, v, mc1):
    shape, last = w.shape, w.shape[-1]
    g2 = [t.reshape(-1, last) for t in (mine, theirs)]
    w3 = [t.reshape(2, -1, last) for t in (w, m, v)]
    R = g2[0].shape[0]
    tr = _pick(R, tuple(c for c in (4096, 2048, 1024, 512, 256, 128, 64, 32, 16, 8) if c * last * 4 <= EW_BLOCK_BYTES))

    def body(mc_ref, mine_ref, theirs_ref, w_ref, m_ref, v_ref, g_out, d_out, m_out, v_out):
        g = jnp.where(pl.program_id(0) == mc_ref[0], mine_ref[...], theirs_ref[...])
        d, nm, nv = _adamw(g, w_ref[0], m_ref[0], v_ref[0])
        g_out[0], d_out[0], m_out[0], v_out[0] = g, d, nm, nv

    half = pl.BlockSpec((tr, last), lambda h, i: (i, 0))
    full = pl.BlockSpec((1, tr, last), lambda h, i: (h, i, 0))
    res = pl.pallas_call(
        body, name=name, grid=(2, R // tr),
        in_specs=[pl.BlockSpec(memory_space=pltpu.SMEM), half, half, full, full, full], out_specs=[full] * 4,
        out_shape=[jax.ShapeDtypeStruct((2, R, last), f32)] * 4,
        compiler_params=pltpu.CompilerParams(dimension_semantics=("arbitrary", "arbitrary"), vmem_limit_bytes=VMEM_LIMIT),
    )(mc1, *g2, *w3)
    return tuple(r.reshape(shape) for r in res)


def kernel(x, mem, mem_ln_g, mem_ln_b, ln_g, ln_b, ffn1_w_gu, ffn1_w_down, w_in, dn_conv_w, dn_a_log, dn_dt_bias, dn_norm_w, swa_sinks, w_mem_kv, w_branch, w_out, ffn2_w_gu, ffn2_w_down, loss_target, m_mem_ln_g, m_mem_ln_b, m_ln_g, m_ln_b, m_ffn1_w_gu, m_ffn1_w_down, m_w_in, m_dn_conv_w, m_dn_a_log, m_dn_dt_bias, m_dn_norm_w, m_swa_sinks, m_w_mem_kv, m_w_branch, m_w_out, m_ffn2_w_gu, m_ffn2_w_down, v_mem_ln_g, v_mem_ln_b, v_ln_g, v_ln_b, v_ffn1_w_gu, v_ffn1_w_down, v_w_in, v_dn_conv_w, v_dn_a_log, v_dn_dt_bias, v_dn_norm_w, v_swa_sinks, v_w_mem_kv, v_w_branch, v_w_out, v_ffn2_w_gu, v_ffn2_w_down):
    args = dict(locals())
    Ws = {n: args[n] for n in WEIGHTS}
    Ms = {n: args["m_" + n] for n in WEIGHTS}
    Vs = {n: args["v_" + n] for n in WEIGHTS}
    mc = lax.axis_index("c")
    my_s = 2 * lax.axis_index("x") + lax.axis_index("y")
    small = [n for n in WEIGHTS if n not in MATRICES]

    ag_table, ag_rows = _layout(SMALL_SHARDED, {n: Ws[n].size for n in SMALL_SHARDED}, 16)
    ag_small = _pack(ag_table, ag_rows, {n: Ws[n].reshape(-1) for n in SMALL_SHARDED}).reshape(2, ag_rows // 2, LANES)
    local = [Ws[n].astype(bf16) for n in MATRICES] + [ag_small]
    gathered = _all_gather("all_gather_w", local)
    gathered = [jnp.stack([jnp.where(my_s == s, loc, g[s]) for s in range(N_SHARD)], axis=0) for loc, g in zip(local, gathered)]
    Wf = {n: _join_shards(g, SHARD_AXIS[n]) for n, g in zip(MATRICES, gathered) if not n.endswith('w_gu') and n != 'w_in'}
    for n, g in zip(MATRICES, gathered):
        if n.endswith('w_gu'):
            assert g.shape[-1] == FF_BLK
            Wf[n + '_p'] = jnp.concatenate([g[0], g[2], g[1], g[3]], axis=2)
        if n == 'w_in':
            Wf['w_in_sh'] = g
    g_small = gathered[-1].reshape(N_SHARD, ag_rows, LANES)
    for n in SMALL_SHARDED:
        Wf[n] = _join_shards(_unpack(ag_table, g_small, n).reshape((N_SHARD,) + Ws[n].shape), SHARD_AXIS[n])
    for n in WEIGHTS:
        if SHARD_AXIS[n] is None:
            Wf[n] = Ws[n]

    loss_row, dx, G = _local_step(x[0], mem[0], loss_target[0], Wf)

    table, rows = _layout(small + ['loss'], {**{n: Ws[n].size for n in small}, 'loss': 1}, 16)
    gflat = {n: (jnp.broadcast_to(G[n].reshape(1, -1), (N_SHARD, G[n].size)) if SHARD_AXIS[n] is None
                 else _split_shards(G[n], SHARD_AXIS[n]).reshape(N_SHARD, -1)) for n in small}
    gflat['loss'] = jnp.broadcast_to(loss_row[:, :1], (N_SHARD, 1))
    n_in = Ws['w_in'].shape[-1]
    items =[_split_shards(G[n], SHARD_AXIS[n]) if n != 'w_in' else
             jnp.stack([jnp.stack([_unpad_dw_in(G['w_in_p'][l], s * n_in, (s + 1) * n_in) for l in range(DEPTH)]) for s in range(N_SHARD)])
             for n in MATRICES] + [_pack(table, rows, gflat).reshape(N_SHARD, 2, rows // 2, LANES)]
    tags = MATRICES + ['small']
    got = _pair_exchange("rs_pair", items)
    keep = [lax.dynamic_index_in_dim(a, mc, axis=1, keepdims=False) for a in items]
    wire = [bf16] * len(MATRICES) + [f32]
    part = [_ew_call(f"rs_add_pair_{t}", lambda a, b: a + b, [k, g], 1, out_dtype=dt)[0] for t, k, g, dt in zip(tags, keep, got, wire)]
    others = _chip_exchange("rs_chips", part)
    own = lambda a: lax.dynamic_index_in_dim(a, my_s, axis=0, keepdims=False)
    red = [_ew_call(f"rs_add_chips_{t}", lambda k, g, fx, fy, fxy: ((k + g) + fy) + (fx + fxy), [own(k), own(g), o[0], o[1], o[2]], 1)[0]
           for t, k, g, o in zip(tags, keep, got, others)]
    theirs = _pair_swap("rs_swap", red)

    mc1 = mc.astype(i32).reshape(1)
    outs = {}
    for n, a, b in zip(MATRICES, red, theirs):
        outs[n] = _adamw_call(f"adamw_{n}", a, b, Ws[n], Ms[n], Vs[n], mc1)
    fill = {'loss': jnp.zeros((1,), f32)}
    packs = [_pack(table, rows, {**{n: src[n].reshape(-1) for n in small}, **fill}).reshape(2, rows // 2, LANES) for src in (Ws, Ms, Vs)]
    small_out = [p.reshape(rows, LANES) for p in _adamw_call("adamw_small", red[-1], theirs[-1], *packs, mc1)]
    for n in small:
        outs[n] = tuple(_unpack(table, p, n).reshape(Ws[n].shape) for p in small_out)
    loss = _unpack(table, small_out[0], 'loss').reshape(())
    return (loss, dx[None], *[outs[n][k] for k in range(4) for n in WEIGHTS])
```

```python
import functools

import jax
import jax.numpy as jnp
from jax import lax
from jax.experimental import pallas as pl
from jax.experimental.pallas import tpu as pltpu

f32, bf16, i32 = jnp.float32, jnp.bfloat16, jnp.int32
HI = lax.Precision.HIGHEST
MESH = pl.DeviceIdType.MESH

D_MODEL = 1024
DEPTH = 2
DN_HEADS, DN_DK, DN_CONV, DN_CHUNK = 4, 128, 4, 64
SWA_HEADS, SWA_KV_HEADS, SWA_DH, WINDOW = 8, 2, 64, 128
XA_HEADS, XA_DH = 4, 128
D_FF = 2816
N_BRANCH, BRANCH_W = 3, 512
ALPHA = (2 * DEPTH) ** 0.25
LN_EPS, RMS_EPS, NEG_INF = 1e-5, 1e-6, -1e30
D_IN = 6408
D_INP = 6656
LR, B1, B2, EPS, WD, STEP = 0.001, 0.9, 0.999, 1e-08, 0.01, 10

LANES = 128
VMEM_LIMIT = 56 << 20
N_SHARD = 4
SCAN_CHUNKS = 16
HALO = 16
MM_TILES = {
    'ffn1_gu_dx': (512, 1024, 5632), 'ffn2_gu_dx': (512, 1024, 5632), 'w_in_dx': (512, 1024, 6656), 'w_in': (2048, 512, 1024),
    'w_in_dw': (1024, 512, 4096), 'ffn1_down_dw': (1408, 512, 4096), 'ffn2_down_dw': (1408, 512, 4096),
    'ffn1_down_dx': (1024,), 'ffn2_down_dx': (1024,), 'ffn1_down': (1024,), 'ffn2_down': (1024,), 'w_out': (1024,),
    'ffn1_gu': (1024,), 'ffn2_gu': (1024,),
    'w_out_dx': (512, 1024, 1024),
}


def _tiles(name, default=None):
    return MM_TILES.get(name, MM_TILES.get(name.split('_', 1)[-1], default))

WEIGHTS = ['mem_ln_g', 'mem_ln_b', 'ln_g', 'ln_b', 'ffn1_w_gu', 'ffn1_w_down', 'w_in', 'dn_conv_w', 'dn_a_log',
           'dn_dt_bias', 'dn_norm_w', 'swa_sinks', 'w_mem_kv', 'w_branch', 'w_out', 'ffn2_w_gu', 'ffn2_w_down']
SHARD_AXIS = {'mem_ln_g': None, 'mem_ln_b': None, 'ln_g': 2, 'ln_b': 2, 'ffn1_w_gu': 2, 'ffn1_w_down': 1, 'w_in': 2,
              'dn_conv_w': 2, 'dn_a_log': None, 'dn_dt_bias': None, 'dn_norm_w': None, 'swa_sinks': None,
              'w_mem_kv': 1, 'w_branch': 3, 'w_out': 1, 'ffn2_w_gu': 2, 'ffn2_w_down': 1}
MATRICES = ['ffn1_w_gu', 'ffn1_w_down', 'w_in', 'w_mem_kv', 'w_branch', 'w_out', 'ffn2_w_gu', 'ffn2_w_down']
SMALL_SHARDED = ['ln_g', 'ln_b', 'dn_conv_w']


def _dg(a, b, mode, hi):
    nb = a.ndim - 2
    bd = tuple(range(nb))
    ca = nb if mode == 'tn' else nb + 1
    cb = nb + 1 if mode == 'nt' else nb
    dims = (((ca,), (cb,)), (bd, bd))
    dot = lambda x, y: lax.dot_general(x, y, dims, preferred_element_type=f32)
    a_hi, b_hi = a.astype(bf16), b.astype(bf16)
    if not hi:
        return dot(a_hi, b_hi)
    a_lo = (a.astype(f32) - a_hi.astype(f32)).astype(bf16)
    b_lo = (b.astype(f32) - b_hi.astype(f32)).astype(bf16)
    return dot(a_hi, b_hi) + (dot(a_hi, b_lo) + dot(a_lo, b_hi))


@functools.partial(jax.custom_vjp, nondiff_argnums=(2, 3))
def _dot(a, b, mode, hi):
    return _dg(a, b, mode, hi)


def _dot_fwd(a, b, mode, hi):
    return _dg(a, b, mode, hi), (a, b)


def _dot_bwd(mode, hi, res, g):
    a, b = res
    if mode == 'nn':
        da, db = _dg(g, b, 'nt', hi), _dg(a, g, 'tn', hi)
    elif mode == 'nt':
        da, db = _dg(g, b, 'nn', hi), _dg(g, a, 'tn', hi)
    else:
        da, db = _dg(b, g, 'nt', hi), _dg(a, g, 'nn', hi)
    return da.astype(a.dtype), db.astype(b.dtype)


_dot.defvjp(_dot_fwd, _dot_bwd)


def bdot(a, b, mode):
    return _dot(a, b, mode, False)


def hdot(a, b, mode):
    return _dot(a, b, mode, True)


def _shift_rows_impl(x, tail, s):
    r = pltpu.roll(x, s, 0)
    rows = lax.broadcasted_iota(i32, tail.shape, 0)
    top = jnp.where(rows >= s, r[:HALO], pltpu.roll(tail, s, 0))
    return jnp.concatenate([top, r[HALO:]], axis=0)


@functools.partial(jax.custom_vjp, nondiff_argnums=(2,))
def _shift_rows(x, tail, s):
    return _shift_rows_impl(x, tail, s)


def _shift_rows_fwd(x, tail, s):
    return _shift_rows_impl(x, tail, s), None


def _shift_rows_bwd(s, _, g):
    n = g.shape[0]
    r = pltpu.roll(g, n - s, 0)
    rows = lax.broadcasted_iota(i32, (HALO, g.shape[1]), 0)
    last = r[n - HALO:]
    keep = rows < HALO - s
    dx = jnp.concatenate([r[:n - HALO], jnp.where(keep, last, 0.0)], axis=0)
    return dx, jnp.where(keep, 0.0, pltpu.roll(g[:HALO], HALO - s, 0))


_shift_rows.defvjp(_shift_rows_fwd, _shift_rows_bwd)


def _lane_pick(x, lane):
    idx = lax.broadcasted_iota(i32, x.shape, x.ndim - 1)
    return jnp.sum(jnp.where(idx == lane, x, 0.0), axis=-1, keepdims=True)


def _silu(x):
    return x * jax.nn.sigmoid(x)


def _tri_inv(a):
    C = a.shape[-1]
    eye = (lax.broadcasted_iota(i32, a.shape, 1) == lax.broadcasted_iota(i32, a.shape, 2)).astype(f32)
    p = -a
    x = eye + p
    for _ in range((C - 1).bit_length() - 1):
        p = _dg(p, p, 'nn', True)
        x = x + _dg(x, p, 'nn', True)
    return x


@jax.custom_vjp
def _tri_solve(a, rhs):
    return _dg(_tri_inv(a), rhs, 'nn', True)


def _tri_solve_fwd(a, rhs):
    tinv = _tri_inv(a)
    sol = _dg(tinv, rhs, 'nn', True)
    return sol, (tinv, sol)


def _tri_solve_bwd(res, g):
    tinv, sol = res
    d_rhs = _dg(tinv, g, 'tn', True)
    return -_dg(d_rhs, sol, 'nt', True), d_rhs


_tri_solve.defvjp(_tri_solve_fwd, _tri_solve_bwd)


@functools.partial(jax.custom_vjp, nondiff_argnums=(1,))
def _lane_head(x, n):
    return x[:, :, :n]


def _lane_head_fwd(x, n):
    return x[:, :, :n], None


def _lane_head_bwd(n, _, g):
    s = jnp.sum(g, axis=-1, keepdims=True) * (1.0 / LANES)
    return (jnp.broadcast_to(s, g.shape[:-1] + (LANES,)),)


_lane_head.defvjp(_lane_head_fwd, _lane_head_bwd)


@functools.partial(jax.custom_vjp, nondiff_argnums=(1,))
def _last_row(x, c):
    return x[:, c - 1:, :]


def _last_row_fwd(x, c):
    return x[:, c - 1:, :], None


def _last_row_bwd(c, _, g):
    shape = (g.shape[0], c, g.shape[2])
    rows = lax.broadcasted_iota(i32, shape, 1)
    return (jnp.where(rows == c - 1, jnp.broadcast_to(g, shape), 0.0),)


_last_row.defvjp(_last_row_fwd, _last_row_bwd)


def _full(a):
    nd = a.ndim
    return (a, tuple(a.shape), lambda i, _nd=nd: (0,) * _nd)


def _rows(a, tm, col=0, width=None):
    width = a.shape[1] if width is None else width
    return (a, (tm, width), lambda i, _c=col: (i, _c))


def _tile_call(name, fn, n, ins, outs, accs=()):
    n_in, n_out, n_acc = len(ins), len(outs), len(accs)

    def body(*refs):
        i = pl.program_id(0)
        res = fn(i, *[r[...].astype(f32) for r in refs[:n_in]])
        if not isinstance(res, (tuple, list)):
            res = (res,)
        assert len(res) == n_out + n_acc, (name, len(res), n_out, n_acc)
        for r, v in zip(refs[n_in:n_in + n_out], res[:n_out]):
            r[...] = v.astype(r.dtype)
        if n_acc:
            acc_refs = refs[n_in + n_out:]

            @pl.when(i == 0)
            def _():
                for r in acc_refs:
                    r[...] = jnp.zeros(r.shape, r.dtype)

            for r, v in zip(acc_refs, res[n_out:]):
                r[...] += v.astype(r.dtype)

    out_shape = [jax.ShapeDtypeStruct(s, d) for s, d, _, _ in outs] + [jax.ShapeDtypeStruct(s, d) for s, d in accs]
    out_specs = [pl.BlockSpec(b, m) for _, _, b, m in outs]
    out_specs += [pl.BlockSpec(tuple(s), lambda i, _nd=len(s): (0,) * _nd) for s, _ in accs]
    res = pl.pallas_call(
        body, name=name, grid=(n,),
        in_specs=[pl.BlockSpec(b, m) for _, b, m in ins],
        out_specs=out_specs, out_shape=out_shape,
        compiler_params=pltpu.CompilerParams(dimension_semantics=("arbitrary",), vmem_limit_bytes=VMEM_LIMIT),
    )(*[a for a, _, _ in ins])
    return res


def _pick(n, cands):
    for c in cands:
        if n % c == 0:
            return c
    return n


def _mm(name, a, b, mode, out_dtype=f32, add=None, out_cols=None):
    if mode == 'tn':
        K, M = a.shape
    else:
        M, K = a.shape
    N = b.shape[0] if mode == 'nt' else b.shape[1]
    tm = _pick(M, (1024, 1408, 512, 256, 128))
    tn = _pick(N, (512, 1408, 256, 128)) if out_cols is None else out_cols[0]
    tuned = _tiles(name)
    if tuned is not None and (M % tuned[0] or N % tuned[1] or K % tuned[2]):
        tuned = None
    if tuned is not None:
        tm, tn = tuned[:2]
    col = (lambda j: j) if out_cols is None else out_cols[1]
    tk = K if K <= 3328 else _pick(K, (3328, 2816, 2048, 1024, 512, 256, 128))
    if tuned is not None:
        tk = tuned[2]
    nk = K // tk
    ca = 0 if mode == 'tn' else 1
    cb = 1 if mode == 'nt' else 0
    dims = (((ca,), (cb,)), ((), ()))

    def body(*refs):
        a_ref, b_ref = refs[0], refs[1]
        add_ref = refs[2] if add is not None else None
        part = lax.dot_general(a_ref[...].astype(bf16), b_ref[...].astype(bf16), dims, preferred_element_type=f32)

        def finish(r, o_ref):
            if add_ref is not None:
                r = r + add_ref[...].astype(f32)
            o_ref[...] = r.astype(o_ref.dtype)

        if nk == 1:
            finish(part, refs[-1])
            return
        o_ref, acc_ref = refs[-2], refs[-1]
        k = pl.program_id(2)

        @pl.when(k == 0)
        def _():
            acc_ref[...] = part

        @pl.when(k > 0)
        def _():
            acc_ref[...] += part

        @pl.when(k == nk - 1)
        def _():
            finish(acc_ref[...], o_ref)

    a_spec = pl.BlockSpec((tk, tm), lambda i, j, k: (k, i)) if mode == 'tn' else pl.BlockSpec((tm, tk), lambda i, j, k: (i, k))
    b_spec = pl.BlockSpec((tn, tk), lambda i, j, k: (j, k)) if mode == 'nt' else pl.BlockSpec((tk, tn), lambda i, j, k: (k, j))
    in_specs, args = [a_spec, b_spec], [a, b]
    if add is not None:
        in_specs.append(pl.BlockSpec((tm, tn), lambda i, j, k: (i, j)))
        args.append(add)
    return pl.pallas_call(
        body, name=name, grid=(M // tm, N // tn, nk), in_specs=in_specs,
        out_specs=pl.BlockSpec((tm, tn), lambda i, j, k: (i, col(j))),
        out_shape=jax.ShapeDtypeStruct((M, N), out_dtype),
        scratch_shapes=[pltpu.VMEM((tm, tn), f32)] if nk > 1 else [],
        compiler_params=pltpu.CompilerParams(dimension_semantics=("parallel", "parallel", "arbitrary"), vmem_limit_bytes=VMEM_LIMIT),
    )(*args)


def _swiglu(u):
    return _silu(u[:, :D_FF]) * u[:, D_FF:]


def _res_ln(x, f, g, b, s):
    r = ALPHA * x + s * f
    mu = jnp.mean(r, axis=-1, keepdims=True)
    rc = r - mu
    var = jnp.mean(rc * rc, axis=-1, keepdims=True)
    return rc * lax.rsqrt(var + LN_EPS) * g + b


def _ln(x, g, b):
    mu = jnp.mean(x, axis=-1, keepdims=True)
    xc = x - mu
    var = jnp.mean(xc * xc, axis=-1, keepdims=True)
    return xc * lax.rsqrt(var + LN_EPS) * g + b


def _dn_pre(first, xc, xp, ba, cw, hp):
    xp = jnp.where(first, 0.0, xp)
    y = cw[DN_CONV - 1:DN_CONV, :] * xc
    for j in range(DN_CONV - 1):
        y = y + cw[j:j + 1, :] * _shift_rows(xc, xp, DN_CONV - 1 - j)
    c = _silu(y)
    qs, ks, vs, gs, bs = [], [], [], [], []
    nqk = DN_HEADS * DN_DK
    for h in range(DN_HEADS):
        q = c[:, h * DN_DK:(h + 1) * DN_DK]
        k = c[:, nqk + h * DN_DK:nqk + (h + 1) * DN_DK]
        v = c[:, 2 * nqk + h * DN_DK:2 * nqk + (h + 1) * DN_DK]
        qs.append(q * lax.rsqrt(jnp.sum(q * q, axis=-1, keepdims=True) + RMS_EPS))
        ks.append(k * lax.rsqrt(jnp.sum(k * k, axis=-1, keepdims=True) + RMS_EPS))
        vs.append(v)
        beta = jax.nn.sigmoid(_lane_pick(ba, h))
        a_log = _lane_pick(hp[0:1, :], h)
        dt = _lane_pick(hp[1:2, :], h)
        g = -jnp.exp(a_log) * jax.nn.softplus(_lane_pick(ba, DN_HEADS + h) + dt)
        gs.append(jnp.broadcast_to(g, q.shape))
        bs.append(jnp.broadcast_to(beta, q.shape))
    return tuple(jnp.stack(t, axis=0) for t in (qs, ks, vs, gs, bs))


def _wy(q, k, v, gb, bb):
    B, C, _ = q.shape
    ri = lax.broadcasted_iota(i32, (B, C, C), 1)
    ci = lax.broadcasted_iota(i32, (B, C, C), 2)
    tril, strict = ri >= ci, ri > ci
    gc = hdot(tril.astype(f32), gb, 'nn')
    gl = jnp.broadcast_to(_last_row(gc, C), gc.shape)
    col = _lane_head(gc, C)
    decay = jnp.exp(jnp.where(tril, col - jnp.swapaxes(col, 1, 2), NEG_INF))
    qs = q * (DN_DK ** -0.5)
    kb = k * bb
    a = jnp.where(strict, bdot(kb, k, 'nt') * decay, 0.0)
    sol = _tri_solve(a, jnp.concatenate([v * bb, kb * jnp.exp(gc)], axis=-1))
    qk = jnp.where(tril, bdot(qs, k, 'nt') * decay, 0.0)
    kt = k * jnp.exp(gl - gc)
    qd = qs * jnp.exp(gc)
    return qd, qk, sol[..., :DN_DK], sol[..., DN_DK:], kt, jnp.exp(gl)


def _scan_step(S, qd, qk, u, w, kt, egl):
    vn = u - bdot(w, S, 'nn')
    o = bdot(qd, S, 'nn') + bdot(qk, vn, 'nn')
    e2 = jnp.concatenate([egl] * (DN_DK // DN_CHUNK), axis=1)
    return o, S * e2 + bdot(kt, vn, 'tn')


def _dn_post(o, z, nw):
    outs = []
    for h in range(DN_HEADS):
        oh = o[h]
        oh = oh * lax.rsqrt(jnp.mean(oh * oh, axis=-1, keepdims=True) + RMS_EPS) * nw
        outs.append(oh * _silu(z[:, h * DN_DK:(h + 1) * DN_DK]))
    return jnp.concatenate(outs, axis=1)


def _swa(first, q, kc, kp, vc, vp, snk):
    W = q.shape[1]
    G = SWA_HEADS // SWA_KV_HEADS
    r = lax.broadcasted_iota(i32, (G, W, 2 * W), 1)
    c = lax.broadcasted_iota(i32, (G, W, 2 * W), 2)
    mask = (c > W + r - WINDOW) & (c <= W + r) & jnp.logical_or(c >= W, jnp.logical_not(first))
    sink_all = _lane_pick(snk, 0)
    outs = []
    for j in range(SWA_KV_HEADS):
        qj = q[j * G:(j + 1) * G]
        kk = jnp.broadcast_to(jnp.concatenate([kp[j], kc[j]], axis=0)[None], (G, 2 * W, SWA_DH))
        vv = jnp.broadcast_to(jnp.concatenate([vp[j], vc[j]], axis=0)[None], (G, 2 * W, SWA_DH))
        s = jnp.where(mask, bdot(qj, kk, 'nt') * (SWA_DH ** -0.5), NEG_INF)
        sink = sink_all[j * G:(j + 1) * G][:, :, None]
        m = jnp.maximum(jnp.max(s, axis=-1, keepdims=True), sink)
        p = jnp.exp(s - m)
        p = p / (jnp.sum(p, axis=-1, keepdims=True) + jnp.exp(sink - m))
        outs.append(bdot(p, vv, 'nn'))
    return jnp.concatenate(outs, axis=0)


def _xa(q, kv):
    outs = []
    nk = XA_HEADS * XA_DH
    for h in range(XA_HEADS):
        qh = q[:, h * XA_DH:(h + 1) * XA_DH]
        kh = kv[:, h * XA_DH:(h + 1) * XA_DH]
        vh = kv[:, nk + h * XA_DH:nk + (h + 1) * XA_DH]
        s = bdot(qh, kh, 'nt') * (XA_DH ** -0.5)
        m = jnp.max(s, axis=-1, keepdims=True)
        p = jnp.exp(s - m)
        p = p / jnp.sum(p, axis=-1, keepdims=True)
        outs.append(bdot(p, vh, 'nn'))
    return jnp.concatenate(outs, axis=1)


def _merge(o_dn, o_sw, o_xa, gates, wb):
    acc = None
    for n, o in enumerate((o_dn, o_sw, o_xa)):
        t = jax.nn.sigmoid(gates[:, n * D_MODEL:(n + 1) * D_MODEL]) * bdot(o, wb[n], 'nn')
        acc = t if acc is None else acc + t
    return acc


def _row_tile(T, want):
    return _pick(T, tuple(c for c in (1024, 512, 256, 128, 64) if c <= want))


def _res_ln_fwd(name, x, f, g, b, s):
    T = x.shape[0]
    tm = _row_tile(T, 512)

    def fn(i, x, f, g, b):
        h = _res_ln(x, f, g, b, s)
        return h, h

    return _tile_call(name, fn, T // tm, [_rows(x, tm), _rows(f, tm), _full(g), _full(b)],
                      [((T, D_MODEL), f32, (tm, D_MODEL), lambda i: (i, 0)), ((T, D_MODEL), bf16, (tm, D_MODEL), lambda i: (i, 0))])


def _res_ln_bwd(name, x, f, g, b, s, dh):
    T = x.shape[0]
    tm = _row_tile(T, 512)

    def fn(i, x, f, g, b, dh):
        _, vjp = jax.vjp(lambda x, f, g, b: _res_ln(x, f, g, b, s), x, f, g, b)
        return vjp(dh)

    return _tile_call(name, fn, T // tm, [_rows(x, tm), _rows(f, tm), _full(g), _full(b), _rows(dh, tm)],
                      [((T, D_MODEL), f32, (tm, D_MODEL), lambda i: (i, 0)), ((T, D_MODEL), bf16, (tm, D_MODEL), lambda i: (i, 0))],
                      [((1, D_MODEL), f32), ((1, D_MODEL), f32)])


FF_BLK = D_FF // 2


def _pair_gate_up(w):
    d = w.shape[0]
    return jnp.stack([w[:, :D_FF].reshape(d, -1, FF_BLK), w[:, D_FF:].reshape(d, -1, FF_BLK)], axis=2).reshape(d, 2 * D_FF)


def _swiglu_blk(u):
    return _silu(u[:, :FF_BLK]) * u[:, FF_BLK:]


def _mm_spec(tm, tn):
    return dict(compiler_params=pltpu.CompilerParams(dimension_semantics=("parallel", "arbitrary"), vmem_limit_bytes=VMEM_LIMIT))


def _gu_act(name, xb, w_pair):
    T, D = xb.shape
    tm = _pick(T, (_tiles(name, (512,))[0], 512, 256, 128))

    def body(x_ref, w_ref, u_ref, a_ref):
        r = lax.dot_general(x_ref[...], w_ref[...], (((1,), (0,)), ((), ())), preferred_element_type=f32)
        u_ref[...] = r.astype(bf16)
        a_ref[...] = _swiglu_blk(r).astype(bf16)

    return pl.pallas_call(
        body, name=name, grid=(T // tm, D_FF // FF_BLK),
        in_specs=[pl.BlockSpec((tm, D), lambda i, j: (i, 0)), pl.BlockSpec((D, 2 * FF_BLK), lambda i, j: (0, j))],
        out_specs=[pl.BlockSpec((tm, 2 * FF_BLK), lambda i, j: (i, j)), pl.BlockSpec((tm, FF_BLK), lambda i, j: (i, j))],
        out_shape=[jax.ShapeDtypeStruct((T, 2 * D_FF), bf16), jax.ShapeDtypeStruct((T, D_FF), bf16)],
        **_mm_spec(tm, FF_BLK))(xb, w_pair)


def _down_dx_act_bwd(name, df, w_down_t, u):
    T, D = df.shape
    tm = _pick(T, (_tiles(name, (256,))[0], 256, 128))

    def body(df_ref, w_ref, u_ref, du_ref):
        da = lax.dot_general(df_ref[...], w_ref[...], (((1,), (0,)), ((), ())), preferred_element_type=f32)
        _, vjp = jax.vjp(_swiglu_blk, u_ref[...].astype(f32))
        du_ref[...] = vjp(da)[0].astype(bf16)

    return pl.pallas_call(
        body, name=name, grid=(T // tm, D_FF // FF_BLK),
        in_specs=[pl.BlockSpec((tm, D), lambda i, j: (i, 0)), pl.BlockSpec((D, FF_BLK), lambda i, j: (0, j)),
                  pl.BlockSpec((tm, 2 * FF_BLK), lambda i, j: (i, j))],
        out_specs=pl.BlockSpec((tm, 2 * FF_BLK), lambda i, j: (i, j)),
        out_shape=jax.ShapeDtypeStruct((T, 2 * D_FF), bf16),
        **_mm_spec(tm, FF_BLK))(df, w_down_t, u)


def _mm_res_ln(name, a, w, x, g, b, s):
    T, K = a.shape
    tm = _pick(T, (_tiles(name, (512,))[0], 512, 256, 128))

    def body(a_ref, w_ref, x_ref, g_ref, b_ref, f_ref, h_ref, hb_ref):
        f = lax.dot_general(a_ref[...], w_ref[...], (((1,), (0,)), ((), ())), preferred_element_type=f32)
        h = _res_ln(x_ref[...], f, g_ref[...], b_ref[...], s)
        f_ref[...] = f.astype(bf16)
        h_ref[...] = h
        hb_ref[...] = h.astype(bf16)

    row = pl.BlockSpec((tm, D_MODEL), lambda i: (i, 0))
    vec = pl.BlockSpec((1, D_MODEL), lambda i: (0, 0))
    return pl.pallas_call(
        body, name=name, grid=(T // tm,),
        in_specs=[pl.BlockSpec((tm, K), lambda i: (i, 0)), pl.BlockSpec((K, D_MODEL), lambda i: (0, 0)), row, vec, vec],
        out_specs=[row, row, row],
        out_shape=[jax.ShapeDtypeStruct((T, D_MODEL), bf16), jax.ShapeDtypeStruct((T, D_MODEL), f32), jax.ShapeDtypeStruct((T, D_MODEL), bf16)],
        compiler_params=pltpu.CompilerParams(dimension_semantics=("parallel",), vmem_limit_bytes=VMEM_LIMIT))(a, w, x, g, b)


def _hm_spec(tm, w=DN_DK):
    return ((DN_HEADS, tm, w), lambda i: (0, i, 0))


def _prev(i):
    return jnp.maximum(i - 1, 0)


def _dn_front(first, xc, xp, ba, cw, hp):
    tm = xc.shape[0]
    nc = tm // DN_CHUNK
    outs = _wy(*[t.reshape(DN_HEADS * nc, DN_CHUNK, DN_DK) for t in _dn_pre(first, xc, xp, ba, cw, hp)])
    return tuple(o.reshape(DN_HEADS, tm, o.shape[-1]) for o in outs)


def _dn_front_ins(p, pba, cw, hp, tm):
    W3 = 3 * DN_HEADS * DN_DK
    return [(p, (tm, W3), lambda i: (i, 0)), (p, (HALO, W3), lambda i: (jnp.maximum(i * (tm // HALO) - 1, 0), 0)),
            (pba, (tm, LANES), lambda i: (i, 0)), _full(cw), _full(hp)]


def _dn_front_fwd(name, p, pba, cw, hp):
    T = p.shape[0]
    tm = _row_tile(T, 256)
    blk, im = _hm_spec(tm)
    blk2, im2 = _hm_spec(tm, DN_CHUNK)
    full = (DN_HEADS, T, DN_DK)
    outs = [(full, bf16, blk, im), ((DN_HEADS, T, DN_CHUNK), bf16, blk2, im2), (full, f32, blk, im),
            (full, bf16, blk, im), (full, bf16, blk, im), (full, f32, blk, im)]
    return _tile_call(name, lambda i, *a: _dn_front(i == 0, *a), T // tm, _dn_front_ins(p, pba, cw, hp, tm), outs)


def _dn_front_bwd(name, p, pba, cw, hp, cts):
    T = p.shape[0]
    tm = _row_tile(T, 256)
    n = T // tm
    W3 = 3 * DN_HEADS * DN_DK
    blk, im = _hm_spec(tm)
    blk2, im2 = _hm_spec(tm, DN_CHUNK)
    ins = _dn_front_ins(p, pba, cw, hp, tm) + [(c, blk2 if c.shape[-1] == DN_CHUNK else blk, im) for c in cts]

    def fn(i, xc, xp, ba, cw, hp, *cts):
        _, vjp = jax.vjp(lambda *a: _dn_front(i == 0, *a), xc, xp, ba, cw, hp)
        return vjp(tuple(cts))

    return _tile_call(name, fn, n, ins,
                      [((T, W3), bf16, (tm, W3), lambda i: (i, 0)), ((n * HALO, W3), f32, (HALO, W3), lambda i: (i, 0)),
                       ((T, LANES), f32, (tm, LANES), lambda i: (i, 0))],
                      [(tuple(cw.shape), f32), (tuple(hp.shape), f32)])


def _scan_fwd(name, qd, qk, u, w, kt, egl):
    T = qd.shape[1]
    C = DN_CHUNK
    U = _pick(T // C, (SCAN_CHUNKS, 1))
    n = T // (C * U)

    def body(qd_ref, qk_ref, u_ref, w_ref, kt_ref, egl_ref, o_ref, sall_ref, s_ref):
        i = pl.program_id(0)

        @pl.when(i == 0)
        def _():
            s_ref[...] = jnp.zeros(s_ref.shape, f32)

        S = s_ref[...]
        for j in range(U):
            sl = pl.ds(j * C, C)
            sall_ref[:, j] = S
            o, S = _scan_step(S, *[r[:, sl, :] for r in (qd_ref, qk_ref, u_ref, w_ref, kt_ref, egl_ref)])
            o_ref[:, sl, :] = o
        s_ref[...] = S

    blk, im = _hm_spec(C * U)
    blk2, im2 = _hm_spec(C * U, C)
    return pl.pallas_call(
        body, name=name, grid=(n,),
        in_specs=[pl.BlockSpec(blk, im), pl.BlockSpec(blk2, im2)] + [pl.BlockSpec(blk, im)] * 4,
        out_specs=[pl.BlockSpec(blk, im), pl.BlockSpec((DN_HEADS, U, DN_DK, DN_DK), lambda i: (0, i, 0, 0))],
        out_shape=[jax.ShapeDtypeStruct((DN_HEADS, T, DN_DK), f32), jax.ShapeDtypeStruct((DN_HEADS, n * U, DN_DK, DN_DK), f32)],
        scratch_shapes=[pltpu.VMEM((DN_HEADS, DN_DK, DN_DK), f32)],
        compiler_params=pltpu.CompilerParams(dimension_semantics=("arbitrary",), vmem_limit_bytes=VMEM_LIMIT),
    )(qd, qk, u, w, kt, egl)


def _scan_bwd(name, qd, qk, u, w, kt, egl, s_all, do):
    T = qd.shape[1]
    C = DN_CHUNK
    U = _pick(T // C, (SCAN_CHUNKS, 1))
    n = T // (C * U)

    def body(qd_ref, qk_ref, u_ref, w_ref, kt_ref, egl_ref, sall_ref, do_ref,
             dqd_ref, dqk_ref, du_ref, dw_ref, dkt_ref, degl_ref, ds_ref):
        i = pl.program_id(0)

        @pl.when(i == 0)
        def _():
            ds_ref[...] = jnp.zeros(ds_ref.shape, f32)

        dS = ds_ref[...]
        for j in reversed(range(U)):
            sl = pl.ds(j * C, C)
            args = [r[:, sl, :].astype(f32) for r in (qd_ref, qk_ref, u_ref, w_ref, kt_ref, egl_ref)]
            _, vjp = jax.vjp(_scan_step, sall_ref[:, j], *args)
            dS, *cts = vjp((do_ref[:, sl, :], dS))
            for r, v in zip((dqd_ref, dqk_ref, du_ref, dw_ref, dkt_ref, degl_ref), cts):
                r[:, sl, :] = v
        ds_ref[...] = dS

    blk = (DN_HEADS, C * U, DN_DK)
    blk2 = (DN_HEADS, C * U, C)
    rim = lambda i: (0, n - 1 - i, 0)
    sp, sp2 = pl.BlockSpec(blk, rim), pl.BlockSpec(blk2, rim)
    full, full2 = jax.ShapeDtypeStruct((DN_HEADS, T, DN_DK), f32), jax.ShapeDtypeStruct((DN_HEADS, T, C), f32)
    return pl.pallas_call(
        body, name=name, grid=(n,),
        in_specs=[sp, sp2, sp, sp, sp, sp, pl.BlockSpec((DN_HEADS, U, DN_DK, DN_DK), lambda i: (0, n - 1 - i, 0, 0)), sp],
        out_specs=[sp, sp2, sp, sp, sp, sp],
        out_shape=[full, full2, full, full, full, full],
        scratch_shapes=[pltpu.VMEM((DN_HEADS, DN_DK, DN_DK), f32)],
        compiler_params=pltpu.CompilerParams(dimension_semantics=("arbitrary",), vmem_limit_bytes=VMEM_LIMIT),
    )(qd, qk, u, w, kt, egl, s_all, do)


def _dn_post_fwd(name, o, p, nw):
    T = p.shape[0]
    tm = _row_tile(T, 512)
    blk, im = _hm_spec(tm)
    W = DN_HEADS * DN_DK
    return _tile_call(name, lambda i, o, z, nw: _dn_post(o, z, nw), T // tm,
                      [(o, blk, im), (p, (tm, W), lambda i: (i, 1536 // W)), _full(nw)],
                      [((T, W), f32, (tm, W), lambda i: (i, 0))])[0]


def _dn_post_bwd(name, o, p, nw, dout):
    T = p.shape[0]
    tm = _row_tile(T, 512)
    blk, im = _hm_spec(tm)
    W = DN_HEADS * DN_DK

    def fn(i, o, z, nw, dout):
        _, vjp = jax.vjp(_dn_post, o, z, nw)
        return vjp(dout)

    return _tile_call(name, fn, T // tm,
                      [(o, blk, im), (p, (tm, W), lambda i: (i, 1536 // W)), _full(nw), _rows(dout, tm)],
                      [((DN_HEADS, T, DN_DK), f32, blk, im), ((T, W), f32, (tm, W), lambda i: (i, 0))],
                      [((1, DN_DK), f32)])


def _swa_tile(T, big):
    return _pick(T, (2 * WINDOW, WINDOW)) if big else WINDOW


@functools.partial(jax.custom_vjp, nondiff_argnums=(1,))
def _heads_split(x, nh):
    return jnp.stack([x[:, SWA_DH * h:SWA_DH * (h + 1)] for h in range(nh)], axis=0)


@jax.custom_vjp
def _heads_merge(y):
    return jnp.concatenate([y[h] for h in range(y.shape[0])], axis=1)


_heads_split.defvjp(lambda x, nh: (_heads_split(x, nh), None), lambda nh, _, g: (_heads_merge(g),))
_heads_merge.defvjp(lambda y: (_heads_merge(y), None), lambda _, g: (_heads_split(g, g.shape[1] // SWA_DH),))


def _swa_rows(first, qp, kc, kp, vc, vp, snk):
    kv = [_heads_split(t, SWA_KV_HEADS) for t in (kc, kp, vc, vp)]
    return _heads_merge(_swa(first, _heads_split(qp, SWA_HEADS), *kv, snk))


def _swa_ins(p, snk, W):
    qw, kw = SWA_HEADS * SWA_DH, SWA_KV_HEADS * SWA_DH
    return [(p, (W, qw), lambda i: (i, 2048 // qw)),
            (p, (W, kw), lambda i: (i, 6144 // kw)), (p, (W, kw), lambda i: (_prev(i), 6144 // kw)),
            (p, (W, kw), lambda i: (i, 6272 // kw)), (p, (W, kw), lambda i: (_prev(i), 6272 // kw)), _full(snk)]


def _swa_fwd(name, p, snk):
    T = p.shape[0]
    W = _swa_tile(T, True)
    qw = SWA_HEADS * SWA_DH
    return _tile_call(name, lambda i, *a: _swa_rows(i == 0, *a), T // W, _swa_ins(p, snk, W),
                      [((T, qw), f32, (W, qw), lambda i: (i, 0))])[0]


def _swa_bwd(name, p, snk, do):
    T = p.shape[0]
    W = _swa_tile(T, False)
    qw, kw = SWA_HEADS * SWA_DH, SWA_KV_HEADS * SWA_DH

    def fn(i, qp, k_c, k_p, v_c, v_p, snk, do):
        _, vjp = jax.vjp(lambda *a: _swa_rows(i == 0, *a), qp, k_c, k_p, v_c, v_p, snk)
        return vjp(do)

    kv_out = ((T, kw), f32, (W, kw), lambda i: (i, 0))
    return _tile_call(name, fn, T // W, _swa_ins(p, snk, W) + [(do, (W, qw), lambda i: (i, 0))],
                      [((T, qw), bf16, (W, qw), lambda i: (i, 0)), kv_out, kv_out, kv_out, kv_out], [(tuple(snk.shape), f32)])


def _xa_fwd(name, p, kv):
    T = p.shape[0]
    tm = _row_tile(T, 512)
    W = XA_HEADS * XA_DH
    return _tile_call(name, lambda i, q, kv: _xa(q, kv), T // tm, [(p, (tm, W), lambda i: (i, 2560 // W)), _full(kv)],
                      [((T, W), f32, (tm, W), lambda i: (i, 0))])[0]


def _xa_bwd(name, p, kv, do):
    T = p.shape[0]
    tm = _row_tile(T, 512)
    W = XA_HEADS * XA_DH

    def fn(i, q, kv, do):
        _, vjp = jax.vjp(_xa, q, kv)
        return vjp(do)

    return _tile_call(name, fn, T // tm, [(p, (tm, W), lambda i: (i, 2560 // W)), _full(kv), _rows(do, tm)],
                      [((T, W), bf16, (tm, W), lambda i: (i, 0))], [(tuple(kv.shape), f32)])


def _merge_fwd(name, o_dn, o_sw, o_xa, p, wb):
    T = p.shape[0]
    tm = _row_tile(T, 512)
    GW = N_BRANCH * D_MODEL
    return _tile_call(name, lambda i, a, b, c, g, w: _merge(a, b, c, g, w.astype(f32)), T // tm,
                      [_rows(o_dn, tm), _rows(o_sw, tm), _rows(o_xa, tm), (p, (tm, GW), lambda i: (i, 1)), _full(wb)],
                      [((T, D_MODEL), bf16, (tm, D_MODEL), lambda i: (i, 0))])[0]


def _merge_bwd(name, o_dn, o_sw, o_xa, p, wb, dm):
    T = p.shape[0]
    tm = _row_tile(T, 512)
    GW = N_BRANCH * D_MODEL

    def fn(i, a, b, c, g, w, dm):
        _, vjp = jax.vjp(_merge, a, b, c, g, w.astype(f32))
        return vjp(dm)

    bo = ((T, BRANCH_W), bf16, (tm, BRANCH_W), lambda i: (i, 0))
    return _tile_call(name, fn, T // tm,
                      [_rows(o_dn, tm), _rows(o_sw, tm), _rows(o_xa, tm), (p, (tm, GW), lambda i: (i, 1)), _full(wb), _rows(dm, tm)],
                      [bo, bo, bo, ((T, GW), bf16, (tm, GW), lambda i: (i, 0))], [(tuple(wb.shape), f32)])


def _assemble_dp(name, dxc, dxp, dz, dswq, dxaq, dgates, dkc, dkp, dvc, dvp, dba):
    T = dz.shape[0]
    tm = _row_tile(T, 256)
    n = T // tm
    r = tm // WINDOW
    nb = T // WINDOW

    def fn(i, dxc, dxp, dz, dswq, dxaq, dgates, dkc, dvc, dba, *nxt):
        halo = jnp.where(i == n - 1, 0.0, dxp)
        dqkv = dxc + jnp.concatenate([jnp.zeros((tm - HALO, dxc.shape[1]), f32), halo], axis=0)
        shifted = [jnp.concatenate([jnp.where(i * r + 1 + b <= nb - 1, blk, 0.0) for b, blk in enumerate(half)], axis=0)
                   for half in (nxt[:r], nxt[r:])]
        parts = [dqkv, dz, dswq, dxaq, dgates, dkc + shifted[0], dvc + shifted[1], dba, jnp.zeros((tm, LANES), f32)]
        return jnp.concatenate(parts, axis=1)

    ins = [_rows(dxc, tm), (dxp, (HALO, dxp.shape[1]), lambda i: (jnp.minimum(i + 1, n - 1), 0))]
    ins += [_rows(t, tm) for t in (dz, dswq, dxaq, dgates, dkc, dvc, dba)]
    for t in (dkp, dvp):
        ins += [(t, (WINDOW, t.shape[1]), lambda i, b=b: (jnp.minimum(i * r + 1 + b, nb - 1), 0)) for b in range(r)]
    return _tile_call(name, fn, n, ins, [((T, D_INP), bf16, (tm, D_INP), lambda i: (i, 0))])[0]


def _loss_grad(name, y, tgt):
    T = y.shape[0]
    tm = _row_tile(T, 512)

    def fn(i, y, t):
        e = y - t
        part = 0.5 * jnp.sum(jnp.mean(e * e, axis=-1, keepdims=True), axis=0, keepdims=True)
        return e * (1.0 / D_MODEL), jnp.broadcast_to(part, (1, LANES))

    return _tile_call(name, fn, T // tm, [_rows(y, tm), _rows(tgt, tm)],
                      [((T, D_MODEL), f32, (tm, D_MODEL), lambda i: (i, 0))], [((1, LANES), f32)])


def _mem_ln_fwd(name, mem, g, b):
    M = mem.shape[0]
    return _tile_call(name, lambda i, m, g, b: (_ln(m, g, b),), 1, [_full(mem), _full(g), _full(b)],
                      [((M, D_MODEL), bf16, (M, D_MODEL), lambda i: (0, 0))])[0]


def _mem_ln_bwd(name, mem, g, b, dmn):
    def fn(i, m, g, b, d):
        _, vjp = jax.vjp(lambda g, b: _ln(m, g, b), g, b)
        return vjp(d)

    return _tile_call(name, fn, 1, [_full(mem), _full(g), _full(b), _full(dmn)], [], [((1, D_MODEL), f32), ((1, D_MODEL), f32)])


W_IN_RANGES = ((0, 1536), (1544, 2056), (2056, 2568), (2824, 3336), (3336, 6408), (2568, 2696), (2696, 2824), (1536, 1544))
W_IN_OFFS = tuple(sum(b - a for a, b in W_IN_RANGES[:k]) for k in range(len(W_IN_RANGES)))


def _pad_w_in(cols):
    parts = [cols(a, b) for a, b in W_IN_RANGES]
    return jnp.concatenate(parts + [jnp.zeros((parts[0].shape[0], D_INP - D_IN), parts[0].dtype)], axis=1)


def _unpad_dw_in(d, lo=0, hi=D_IN):
    parts = []
    for (a, b), off in sorted(zip(W_IN_RANGES, W_IN_OFFS)):
        a2, b2 = max(a, lo), min(b, hi)
        if a2 < b2:
            parts.append(d[:, off + a2 - a:off + b2 - a])
    return jnp.concatenate(parts, axis=1)


def _shard_cols(shards, l):
    n = shards[0].shape[-1]

    def cols(a, b):
        parts = [shards[s][l][:, max(a, s * n) - s * n:min(b, (s + 1) * n) - s * n] for s in range(N_SHARD)
                 if max(a, s * n) < min(b, (s + 1) * n)]
        return parts[0] if len(parts) == 1 else jnp.concatenate(parts, axis=1)

    return cols


def _lane_row(v, rows=8):
    out = jnp.zeros((rows, LANES), f32)
    return out.at[0, :v.shape[0]].set(v)


def _layer_fwd(l, x, xb, mem_nb, W):
    n = lambda s: f"l{l}_{s}"
    sv = {}
    u1, a1 = _gu_act(n("ffn1_gu"), xb, W['ffn1_w_gu_p'])
    f1, h1, h1b = _mm_res_ln(n("ffn1_down"), a1, W['ffn1_w_down'], x, W['ln_g'][0:1], W['ln_b'][0:1], 0.5)
    p = _mm(n("w_in"), h1b, W['w_inp'], 'nn', out_dtype=bf16)
    pba = _mm(n("w_in_ba"), h1b, W['w_ba'], 'nn')
    qd, qk, u, w, kt, egl = _dn_front_fwd(n("dn_front"), p, pba, W['conv_w'], W['hp'])
    o_raw, s_all = _scan_fwd(n("dn_scan"), qd, qk, u, w, kt, egl)
    o_dn = _dn_post_fwd(n("dn_post"), o_raw, p, W['norm_w'])
    o_sw = _swa_fwd(n("swa"), p, W['snk'])
    kv = _mm(n("mem_kv"), mem_nb, W['w_mem_kv'], 'nn')
    o_xa = _xa_fwd(n("xa"), p, kv)
    merged = _merge_fwd(n("merge"), o_dn, o_sw, o_xa, p, W['w_branch'])
    mo, h2, h2b = _mm_res_ln(n("w_out"), merged, W['w_out'], h1, W['ln_g'][1:2], W['ln_b'][1:2], 1.0)
    u2, a2 = _gu_act(n("ffn2_gu"), h2b, W['ffn2_w_gu_p'])
    f2, h3, h3b = _mm_res_ln(n("ffn2_down"), a2, W['ffn2_w_down'], h2, W['ln_g'][2:3], W['ln_b'][2:3], 0.5)
    sv = dict(x=x, xb=xb, u1=u1, a1=a1, f1=f1, h1=h1, h1b=h1b, p=p, pba=pba, wy=(qd, qk, u, w, kt, egl),
              o_raw=o_raw, s_all=s_all, o_dn=o_dn, o_sw=o_sw, kv=kv, o_xa=o_xa, merged=merged,
              mo=mo, h2=h2, h2b=h2b, u2=u2, a2=a2, f2=f2)
    return h3, h3b, sv


def _ffn_bwd(n, tag, x, xb, u, a, f, g, b, w_gu_t, w_down_t, dh):
    dx_a, df, dg, db = _res_ln_bwd(n(f"{tag}_ln_bwd"), x, f, g, b, 0.5, dh)
    d_down = _mm(n(f"{tag}_down_dw"), a, df, 'tn')
    du = _down_dx_act_bwd(n(f"{tag}_down_dx"), df, w_down_t, u)
    d_gu = _mm(n(f"{tag}_gu_dw"), xb, du, 'tn', out_cols=(FF_BLK, lambda j: (j % 2) * (D_FF // FF_BLK) + j // 2))
    dx = _mm(n(f"{tag}_gu_dx"), du, w_gu_t, 'nn', add=dx_a)
    return dx, d_gu, d_down, dg, db


def _layer_bwd(l, sv, mem_nb, W, dh3, dmem_acc):
    n = lambda s: f"l{l}_{s}"
    G = {}
    dh2, G['ffn2_w_gu'], G['ffn2_w_down'], dg2, db2 = _ffn_bwd(
        n, "ffn2", sv['h2'], sv['h2b'], sv['u2'], sv['a2'], sv['f2'], W['ln_g'][2:3], W['ln_b'][2:3], W['ffn2_w_gu_t'], W['ffn2_w_down_t'], dh3)
    dh1_a, dmo, dg1, db1 = _res_ln_bwd(n("ln1_bwd"), sv['h1'], sv['mo'], W['ln_g'][1:2], W['ln_b'][1:2], 1.0, dh2)
    G['w_out'] = _mm(n("w_out_dw"), sv['merged'], dmo, 'tn')
    dmerged = _mm(n("w_out_dx"), dmo, W['w_out_t'], 'nn')
    p = sv['p']
    do_dn, do_sw, do_xa, dgates, G['w_branch'] = _merge_bwd(n("merge_bwd"), sv['o_dn'], sv['o_sw'], sv['o_xa'], p, W['w_branch'], dmerged)
    dxaq, dkv = _xa_bwd(n("xa_bwd"), p, sv['kv'], do_xa)
    dkv = dkv.astype(bf16)
    G['w_mem_kv'] = _mm(n("mem_kv_dw"), mem_nb, dkv, 'tn')
    dmem_n = _mm(n("mem_kv_dx"), dkv, W['w_mem_kv_t'], 'nn', add=dmem_acc)
    dswq, dkc, dkp, dvc, dvp, dsnk = _swa_bwd(n("swa_bwd"), p, W['snk'], do_sw)
    do_raw, dz, dnw = _dn_post_bwd(n("dn_post_bwd"), sv['o_raw'], p, W['norm_w'], do_dn)
    cts = _scan_bwd(n("dn_scan_bwd"), *sv['wy'], sv['s_all'], do_raw)
    dxc, dxp, dba, dcw, dhp = _dn_front_bwd(n("dn_front_bwd"), p, sv['pba'], W['conv_w'], W['hp'], cts)
    dp = _assemble_dp(n("dp"), dxc, dxp, dz, dswq, dxaq, dgates, dkc, dkp, dvc, dvp, dba)
    G['w_in_p'] = _mm(n("w_in_dw"), sv['h1b'], dp, 'tn')
    dh1 = _mm(n("w_in_dx"), dp, W['w_inp_t'], 'nn', add=dh1_a)
    dx, G['ffn1_w_gu'], G['ffn1_w_down'], dg0, db0 = _ffn_bwd(
        n, "ffn1", sv['x'], sv['xb'], sv['u1'], sv['a1'], sv['f1'], W['ln_g'][0:1], W['ln_b'][0:1], W['ffn1_w_gu_t'], W['ffn1_w_down_t'], dh1)
    G['ln_g'] = jnp.concatenate([dg0, dg1, dg2], axis=0)
    G['ln_b'] = jnp.concatenate([db0, db1, db2], axis=0)
    G['dn_conv_w'] = dcw
    G['dn_a_log'] = dhp[0, :DN_HEADS]
    G['dn_dt_bias'] = dhp[1, :DN_HEADS]
    G['dn_norm_w'] = dnw[0]
    G['swa_sinks'] = dsnk[:, 0]
    return dx, dmem_n, G


def _local_step(x, mem, tgt, Wf):
    mem_g, mem_b = Wf['mem_ln_g'][None, :], Wf['mem_ln_b'][None, :]
    mem_nb = _mem_ln_fwd("mem_ln", mem, mem_g, mem_b)

    def paired(k, l):
        return Wf[k + '_p'][l] if k + '_p' in Wf else _pair_gate_up(Wf[k][l])

    def w_in_cols(l):
        return _shard_cols(Wf['w_in_sh'], l) if 'w_in_sh' in Wf else (lambda a, b: Wf['w_in'][l][:, a:b])

    layers = []
    for l in range(DEPTH):
        layers.append(dict(
            ln_g=Wf['ln_g'][l], ln_b=Wf['ln_b'][l], ffn1_w_gu_p=paired('ffn1_w_gu', l), ffn1_w_down=Wf['ffn1_w_down'][l],
            w_inp=_pad_w_in(w_in_cols(l)), conv_w=Wf['dn_conv_w'][l],
            hp=jnp.zeros((8, LANES), f32).at[0, :DN_HEADS].set(Wf['dn_a_log'][l]).at[1, :DN_HEADS].set(Wf['dn_dt_bias'][l]),
            norm_w=Wf['dn_norm_w'][l][None, :], snk=jnp.broadcast_to(Wf['swa_sinks'][l][:, None], (SWA_HEADS, LANES)),
            w_mem_kv=Wf['w_mem_kv'][l], w_branch=Wf['w_branch'][l], w_out=Wf['w_out'][l],
            ffn2_w_gu_p=paired('ffn2_w_gu', l), ffn2_w_down=Wf['ffn2_w_down'][l]))
        layers[l]['w_ba'] = layers[l]['w_inp'][:, 6400:6400 + LANES]
        for k in ('ffn1_w_gu_p', 'ffn1_w_down', 'w_inp', 'w_mem_kv', 'w_out', 'ffn2_w_gu_p', 'ffn2_w_down'):
            layers[l][k.replace('_p', '') + '_t'] = layers[l][k].T
    h, hb = x, x.astype(bf16)
    saved = []
    for l in range(DEPTH):
        h, hb, sv = _layer_fwd(l, h, hb, mem_nb, layers[l])
        saved.append(sv)
    dh, loss_row = _loss_grad("loss", h, tgt)
    grads = [None] * DEPTH
    dmem_n = None
    for l in reversed(range(DEPTH)):
        dh, dmem_n, grads[l] = _layer_bwd(l, saved[l], mem_nb, layers[l], dh, dmem_n)
    dmg, dmb = _mem_ln_bwd("mem_ln_bwd", mem, mem_g, mem_b, dmem_n)
    G = {k: jnp.stack([grads[l][k] for l in range(DEPTH)], axis=0) for k in grads[0]}
    G['mem_ln_g'], G['mem_ln_b'] = dmg[0], dmb[0]
    return loss_row, dh, G


SEG_ALIGN = 2048


def _round_up(n, m):
    return (n + m - 1) // m * m


def _layout(names, sizes, mult):
    off, table = 0, {}
    for nm in names:
        table[nm] = (off, sizes[nm])
        off += _round_up(sizes[nm], SEG_ALIGN)
    return table, _round_up(off // LANES, mult)


def _pack(table, rows, flat):
    names = list(table)
    lead = flat[names[0]].shape[:-1]
    parts, pos = [], 0
    for nm in names:
        off, size = table[nm]
        if off > pos:
            parts.append(jnp.zeros(lead + (off - pos,), flat[nm].dtype))
        parts.append(flat[nm])
        pos = off + size
    total = rows * LANES
    if total > pos:
        parts.append(jnp.zeros(lead + (total - pos,), parts[-1].dtype))
    return jnp.concatenate(parts, axis=-1).reshape(lead + (rows, LANES))


def _unpack(table, packed, nm):
    off, size = table[nm]
    flat = packed.reshape(packed.shape[:-2] + (-1,))
    return flat[..., off:off + size]


def _split_shards(full, ax):
    shp = full.shape
    t = full.reshape(shp[:ax] + (N_SHARD, shp[ax] // N_SHARD) + shp[ax + 1:])
    return jnp.moveaxis(t, ax, 0)


def _join_shards(sh4, ax):
    return jnp.concatenate([sh4[s] for s in range(N_SHARD)], axis=ax)


ANY = pl.BlockSpec(memory_space=pl.ANY)


def _me():
    return lax.axis_index("x"), lax.axis_index("y"), lax.axis_index("c")


def _comm_call(name, body, arrays, out_shapes, n_sem):
    n = len(arrays)
    scratch = [pltpu.SemaphoreType.DMA((n, n_sem)), pltpu.SemaphoreType.DMA((n, n_sem))]
    return pl.pallas_call(
        body, name=name, out_shape=out_shapes, in_specs=[ANY] * n, out_specs=[ANY] * n, scratch_shapes=scratch,
        compiler_params=pltpu.CompilerParams(has_side_effects=True),
    )(*arrays)


def _all_gather(name, xs):
    n = len(xs)

    def split(shape):
        ax = next(i for i, d in enumerate(shape) if d % 2 == 0)
        return ax, shape[ax] // 2

    def body(*refs):
        x_refs, out_refs = refs[:n], refs[n:2 * n]
        send_sems, recv_sems = refs[2 * n:]
        mx, my, mc = _me()
        me_s = 2 * mx + my
        across_x, across_y, diag = (1 - mx, my), (mx, 1 - my), (1 - mx, 1 - my)
        sid = lambda chip: 2 * chip[0] + chip[1]

        def part(ref, a, which):
            if which is None:
                return ref
            ax, size = split(xs[a].shape[1:])
            return ref.at[(slice(None),) * ax + (pl.ds(which * size, size),)]

        def copy(a, k, shard, half, which, to, src=None):
            dst = part(out_refs[a].at[shard, half], a, which)
            return pltpu.make_async_remote_copy(src_ref=dst if src is None else part(src, a, which), dst_ref=dst,
                                                send_sem=send_sems.at[a, k], recv_sem=recv_sems.at[a, k], device_id=to, device_id_type=MESH)

        sibling, here = (mx, my, 1 - mc), (mx, my, mc)
        sent = []
        for a in range(n):
            sent += [copy(a, 0, me_s, mc, None, (*across_x, mc), src=x_refs[a].at[mc]),
                     copy(a, 1, me_s, mc, None, (*across_y, mc), src=x_refs[a].at[mc])]
        for cp in sent:
            cp.start()
        landing = [(0, across_x, None, across_y), (1, across_y, None, across_x), (2, diag, 0, None), (3, diag, 1, None)]
        for k, origin, which, relay_to in landing:
            for a in range(n):
                copy(a, k, sid(origin), mc, which, here).wait_recv()
                if relay_to is not None:
                    sent.append(copy(a, 2 + k, sid(origin), mc, k, (*relay_to, mc)))
                    sent[-1].start()
                sent.append(copy(a, 4 + k, sid(origin), mc, which, sibling))
                sent[-1].start()
        for k, origin, which, _ in landing:
            for a in range(n):
                copy(a, 4 + k, sid(origin), 1 - mc, which, here).wait_recv()
        for cp in sent:
            cp.wait_send()

    return _comm_call(name, body, xs, [jax.ShapeDtypeStruct((N_SHARD,) + x.shape, x.dtype) for x in xs], 8)


def _pair_exchange(name, items):
    n = len(items)

    def body(*refs):
        src_refs, dst_refs = refs[:n], refs[n:2 * n]
        send_sems, recv_sems = refs[2 * n:]
        mx, my, mc = _me()
        cps = [pltpu.make_async_remote_copy(src_ref=src_refs[a].at[s, 1 - mc], dst_ref=dst_refs[a].at[s], send_sem=send_sems.at[a, s],
                                            recv_sem=recv_sems.at[a, s], device_id=(mx, my, 1 - mc), device_id_type=MESH)
               for a in range(n) for s in range(N_SHARD)]
        for cp in cps:
            cp.start()
        for cp in cps:
            cp.wait()

    return _comm_call(name, body, items, [jax.ShapeDtypeStruct((N_SHARD,) + t.shape[2:], t.dtype) for t in items], N_SHARD)


def _chip_exchange(name, parts):
    n = len(parts)

    def body(*refs):
        p_refs, dst_refs = refs[:n], refs[n:2 * n]
        send_sems, recv_sems = refs[2 * n:]
        mx, my, mc = _me()
        chips = [(1 - mx, my), (mx, 1 - my), (1 - mx, 1 - my)]
        cps = [pltpu.make_async_remote_copy(src_ref=p_refs[a].at[2 * cx + cy], dst_ref=dst_refs[a].at[j], send_sem=send_sems.at[a, j],
                                            recv_sem=recv_sems.at[a, j], device_id=(cx, cy, mc), device_id_type=MESH)
               for a in range(n) for j, (cx, cy) in enumerate(chips)]
        for cp in cps:
            cp.start()
        for cp in cps:
            cp.wait()

    return _comm_call(name, body, parts, [jax.ShapeDtypeStruct((3,) + t.shape[1:], t.dtype) for t in parts], 3)


def _pair_swap(name, reds):
    n = len(reds)

    def body(*refs):
        r_refs, out_refs = refs[:n], refs[n:2 * n]
        send_sems, recv_sems = refs[2 * n:]
        mx, my, mc = _me()
        cps = [pltpu.make_async_remote_copy(src_ref=r_refs[a], dst_ref=out_refs[a], send_sem=send_sems.at[a, 0],
                                            recv_sem=recv_sems.at[a, 0], device_id=(mx, my, 1 - mc), device_id_type=MESH)
               for a in range(n)]
        for cp in cps:
            cp.start()
        for cp in cps:
            cp.wait()

    return _comm_call(name, body, reds, [jax.ShapeDtypeStruct(t.shape, t.dtype) for t in reds], 1)


EW_BLOCK_BYTES = 1 << 20


def _ew_call(name, fn, ins, n_out, out_dtype=f32):
    shape = ins[0].shape
    last = shape[-1]
    flat = [a.reshape(-1, last) for a in ins]
    R = flat[0].shape[0]
    cands = tuple(c for c in (4096, 2048, 1024, 512, 256, 128, 64, 32, 16, 8) if c * last * 4 <= EW_BLOCK_BYTES)
    tr = _pick(R, cands)
    res = _tile_call(name, lambda i, *a: fn(*a), R // tr, [(a, (tr, last), lambda i: (i, 0)) for a in flat],
                     [((R, last), out_dtype, (tr, last), lambda i: (i, 0))] * n_out)
    return [r.reshape(shape) for r in res]


def _adamw(g, w, m, v):
    m = B1 * m + (1.0 - B1) * g
    v = B2 * v + (1.0 - B2) * jnp.square(g)
    m_hat = m / (1.0 - B1 ** STEP)
    v_hat = v / (1.0 - B2 ** STEP)
    return -LR * (m_hat / (jnp.sqrt(v_hat) + EPS) + WD * w), m, v


def _adamw_call(name, mine, theirs, w, m, v, mc1):
    shape, last = w.shape, w.shape[-1]
    g2 = [t.reshape(-1, last) for t in (mine, theirs)]
    w3 = [t.reshape(2, -1, last) for t in (w, m, v)]
    R = g2[0].shape[0]
    tr = _pick(R, tuple(c for c in (4096, 2048, 1024, 512, 256, 128, 64, 32, 16, 8) if c * last * 4 <= EW_BLOCK_BYTES))

    def body(mc_ref, mine_ref, theirs_ref, w_ref, m_ref, v_ref, g_out, d_out, m_out, v_out):
        g = jnp.where(pl.program_id(0) == mc_ref[0], mine_ref[...], theirs_ref[...])
        d, nm, nv = _adamw(g, w_ref[0], m_ref[0], v_ref[0])
        g_out[0], d_out[0], m_out[0], v_out[0] = g, d, nm, nv

    half = pl.BlockSpec((tr, last), lambda h, i: (i, 0))
    full = pl.BlockSpec((1, tr, last), lambda h, i: (h, i, 0))
    res = pl.pallas_call(
        body, name=name, grid=(2, R // tr),
        in_specs=[pl.BlockSpec(memory_space=pltpu.SMEM), half, half, full, full, full], out_specs=[full] * 4,
        out_shape=[jax.ShapeDtypeStruct((2, R, last), f32)] * 4,
        compiler_params=pltpu.CompilerParams(dimension_semantics=("arbitrary", "arbitrary"), vmem_limit_bytes=VMEM_LIMIT),
    )(mc1, *g2, *w3)
    return tuple(r.reshape(shape) for r in res)


def kernel(x, mem, mem_ln_g, mem_ln_b, ln_g, ln_b, ffn1_w_gu, ffn1_w_down, w_in, dn_conv_w, dn_a_log, dn_dt_bias, dn_norm_w, swa_sinks, w_mem_kv, w_branch, w_out, ffn2_w_gu, ffn2_w_down, loss_target, m_mem_ln_g, m_mem_ln_b, m_ln_g, m_ln_b, m_ffn1_w_gu, m_ffn1_w_down, m_w_in, m_dn_conv_w, m_dn_a_log, m_dn_dt_bias, m_dn_norm_w, m_swa_sinks, m_w_mem_kv, m_w_branch, m_w_out, m_ffn2_w_gu, m_ffn2_w_down, v_mem_ln_g, v_mem_ln_b, v_ln_g, v_ln_b, v_ffn1_w_gu, v_ffn1_w_down, v_w_in, v_dn_conv_w, v_dn_a_log, v_dn_dt_bias, v_dn_norm_w, v_swa_sinks, v_w_mem_kv, v_w_branch, v_w_out, v_ffn2_w_gu, v_ffn2_w_down):
    args = dict(locals())
    Ws = {n: args[n] for n in WEIGHTS}
    Ms = {n: args["m_" + n] for n in WEIGHTS}
    Vs = {n: args["v_" + n] for n in WEIGHTS}
    mc = lax.axis_index("c")
    my_s = 2 * lax.axis_index("x") + lax.axis_index("y")
    small = [n for n in WEIGHTS if n not in MATRICES]

    ag_table, ag_rows = _layout(SMALL_SHARDED, {n: Ws[n].size for n in SMALL_SHARDED}, 16)
    ag_small = _pack(ag_table, ag_rows, {n: Ws[n].reshape(-1) for n in SMALL_SHARDED}).reshape(2, ag_rows // 2, LANES)
    local = [Ws[n].astype(bf16) for n in MATRICES] + [ag_small]
    gathered = _all_gather("all_gather_w", local)
    gathered = [jnp.stack([jnp.where(my_s == s, loc, g[s]) for s in range(N_SHARD)], axis=0) for loc, g in zip(local, gathered)]
    Wf = {n: _join_shards(g, SHARD_AXIS[n]) for n, g in zip(MATRICES, gathered) if not n.endswith('w_gu') and n != 'w_in'}
    for n, g in zip(MATRICES, gathered):
        if n.endswith('w_gu'):
            assert g.shape[-1] == FF_BLK
            Wf[n + '_p'] = jnp.concatenate([g[0], g[2], g[1], g[3]], axis=2)
        if n == 'w_in':
            Wf['w_in_sh'] = g
    g_small = gathered[-1].reshape(N_SHARD, ag_rows, LANES)
    for n in SMALL_SHARDED:
        Wf[n] = _join_shards(_unpack(ag_table, g_small, n).reshape((N_SHARD,) + Ws[n].shape), SHARD_AXIS[n])
    for n in WEIGHTS:
        if SHARD_AXIS[n] is None:
            Wf[n] = Ws[n]

    loss_row, dx, G = _local_step(x[0], mem[0], loss_target[0], Wf)

    table, rows = _layout(small + ['loss'], {**{n: Ws[n].size for n in small}, 'loss': 1}, 16)
    gflat = {n: (jnp.broadcast_to(G[n].reshape(1, -1), (N_SHARD, G[n].size)) if SHARD_AXIS[n] is None
                 else _split_shards(G[n], SHARD_AXIS[n]).reshape(N_SHARD, -1)) for n in small}
    gflat['loss'] = jnp.broadcast_to(loss_row[:, :1], (N_SHARD, 1))
    n_in = Ws['w_in'].shape[-1]
    items =[_split_shards(G[n], SHARD_AXIS[n]) if n != 'w_in' else
             jnp.stack([jnp.stack([_unpad_dw_in(G['w_in_p'][l], s * n_in, (s + 1) * n_in) for l in range(DEPTH)]) for s in range(N_SHARD)])
             for n in MATRICES] + [_pack(table, rows, gflat).reshape(N_SHARD, 2, rows // 2, LANES)]
    tags = MATRICES + ['small']
    got = _pair_exchange("rs_pair", items)
    keep = [lax.dynamic_index_in_dim(a, mc, axis=1, keepdims=False) for a in items]
    wire = [bf16] * len(MATRICES) + [f32]
    part = [_ew_call(f"rs_add_pair_{t}", lambda a, b: a + b, [k, g], 1, out_dtype=dt)[0] for t, k, g, dt in zip(tags, keep, got, wire)]
    others = _chip_exchange("rs_chips", part)
    own = lambda a: lax.dynamic_index_in_dim(a, my_s, axis=0, keepdims=False)
    red = [_ew_call(f"rs_add_chips_{t}", lambda k, g, fx, fy, fxy: ((k + g) + fy) + (fx + fxy), [own(k), own(g), o[0], o[1], o[2]], 1)[0]
           for t, k, g, o in zip(tags, keep, got, others)]
    theirs = _pair_swap("rs_swap", red)

    mc1 = mc.astype(i32).reshape(1)
    outs = {}
    for n, a, b in zip(MATRICES, red, theirs):
        outs[n] = _adamw_call(f"adamw_{n}", a, b, Ws[n], Ms[n], Vs[n], mc1)
    fill = {'loss': jnp.zeros((1,), f32)}
    packs = [_pack(table, rows, {**{n: src[n].reshape(-1) for n in small}, **fill}).reshape(2, rows // 2, LANES) for src in (Ws, Ms, Vs)]
    small_out = [p.reshape(rows, LANES) for p in _adamw_call("adamw_small", red[-1], theirs[-1], *packs, mc1)]
    for n in small:
        outs[n] = tuple(_unpack(table, p, n).reshape(Ws[n].shape) for p in small_out)
    loss = _unpack(table, small_out[0], 'loss').reshape(())
    return (loss, dx[None], *[outs[n][k] for k in range(4) for n in WEIGHTS])
```

```python
import functools

import jax
import jax.numpy as jnp
from jax import lax
from jax.experimental import pallas as pl
from jax.experimental.pallas import tpu as pltpu

f32, bf16, i32 = jnp.float32, jnp.bfloat16, jnp.int32
MESH = pl.DeviceIdType.MESH

D_MODEL = 1024
DEPTH = 2
DN_HEADS, DN_DK, DN_CONV, DN_CHUNK = 4, 128, 4, 64
SWA_HEADS, SWA_KV_HEADS, SWA_DH, WINDOW = 8, 2, 64, 128
XA_HEADS, XA_DH = 4, 128
D_FF = 2816
N_BRANCH, BRANCH_W = 3, 512
ALPHA = (2 * DEPTH) ** 0.25
LN_EPS, RMS_EPS, NEG_INF = 1e-5, 1e-6, -1e30
D_IN = 6408
D_INP = 6656
LR, B1, B2, EPS, WD, STEP = 0.001, 0.9, 0.999, 1e-08, 0.01, 10

LANES = 128
VMEM_LIMIT = 56 << 20
N_SHARD = 4
SCAN_CHUNKS = 16
HALO = 16
MM_TILES = {
    'ffn1_gu_dx': (512, 1024, 5632), 'ffn2_gu_dx': (512, 1024, 5632), 'w_in_dx': (512, 1024, 6656), 'w_in': (2048, 512, 1024),
    'w_in_dw': (1024, 512, 4096), 'ffn1_down_dw': (1408, 512, 4096), 'ffn2_down_dw': (1408, 512, 4096),
    'ffn1_down_dx': (1024,), 'ffn2_down_dx': (1024,), 'ffn1_down': (1024,), 'ffn2_down': (1024,), 'w_out': (1024,),
    'ffn1_gu': (1024,), 'ffn2_gu': (1024,),
    'w_out_dx': (512, 1024, 1024),
}


def _tiles(name, default=None):
    return MM_TILES.get(name, MM_TILES.get(name.split('_', 1)[-1], default))

WEIGHTS = ['mem_ln_g', 'mem_ln_b', 'ln_g', 'ln_b', 'ffn1_w_gu', 'ffn1_w_down', 'w_in', 'dn_conv_w', 'dn_a_log',
           'dn_dt_bias', 'dn_norm_w', 'swa_sinks', 'w_mem_kv', 'w_branch', 'w_out', 'ffn2_w_gu', 'ffn2_w_down']
SHARD_AXIS = {'mem_ln_g': None, 'mem_ln_b': None, 'ln_g': 2, 'ln_b': 2, 'ffn1_w_gu': 2, 'ffn1_w_down': 1, 'w_in': 2,
              'dn_conv_w': 2, 'dn_a_log': None, 'dn_dt_bias': None, 'dn_norm_w': None, 'swa_sinks': None,
              'w_mem_kv': 1, 'w_branch': 3, 'w_out': 1, 'ffn2_w_gu': 2, 'ffn2_w_down': 1}
MATRICES = ['ffn1_w_gu', 'ffn1_w_down', 'w_in', 'w_mem_kv', 'w_branch', 'w_out', 'ffn2_w_gu', 'ffn2_w_down']
SMALL_SHARDED = ['ln_g', 'ln_b', 'dn_conv_w']


def _dg(a, b, mode, hi):
    nb = a.ndim - 2
    bd = tuple(range(nb))
    ca = nb if mode == 'tn' else nb + 1
    cb = nb + 1 if mode == 'nt' else nb
    dims = (((ca,), (cb,)), (bd, bd))
    dot = lambda x, y: lax.dot_general(x, y, dims, preferred_element_type=f32)
    a_hi, b_hi = a.astype(bf16), b.astype(bf16)
    if not hi:
        return dot(a_hi, b_hi)
    a_lo = (a.astype(f32) - a_hi.astype(f32)).astype(bf16)
    b_lo = (b.astype(f32) - b_hi.astype(f32)).astype(bf16)
    return dot(a_hi, b_hi) + (dot(a_hi, b_lo) + dot(a_lo, b_hi))


@functools.partial(jax.custom_vjp, nondiff_argnums=(2, 3))
def _dot(a, b, mode, hi):
    return _dg(a, b, mode, hi)


def _dot_fwd(a, b, mode, hi):
    return _dg(a, b, mode, hi), (a, b)


def _dot_bwd(mode, hi, res, g):
    a, b = res
    if mode == 'nn':
        da, db = _dg(g, b, 'nt', hi), _dg(a, g, 'tn', hi)
    elif mode == 'nt':
        da, db = _dg(g, b, 'nn', hi), _dg(g, a, 'tn', hi)
    else:
        da, db = _dg(b, g, 'nt', hi), _dg(a, g, 'nn', hi)
    return da.astype(a.dtype), db.astype(b.dtype)


_dot.defvjp(_dot_fwd, _dot_bwd)


def bdot(a, b, mode):
    return _dot(a, b, mode, False)


def hdot(a, b, mode):
    return _dot(a, b, mode, True)


def _shift_rows_impl(x, tail, s):
    r = pltpu.roll(x, s, 0)
    rows = lax.broadcasted_iota(i32, tail.shape, 0)
    top = jnp.where(rows >= s, r[:HALO], pltpu.roll(tail, s, 0))
    return jnp.concatenate([top, r[HALO:]], axis=0)


@functools.partial(jax.custom_vjp, nondiff_argnums=(2,))
def _shift_rows(x, tail, s):
    return _shift_rows_impl(x, tail, s)


def _shift_rows_fwd(x, tail, s):
    return _shift_rows_impl(x, tail, s), None


def _shift_rows_bwd(s, _, g):
    n = g.shape[0]
    r = pltpu.roll(g, n - s, 0)
    rows = lax.broadcasted_iota(i32, (HALO, g.shape[1]), 0)
    last = r[n - HALO:]
    keep = rows < HALO - s
    dx = jnp.concatenate([r[:n - HALO], jnp.where(keep, last, 0.0)], axis=0)
    return dx, jnp.where(keep, 0.0, pltpu.roll(g[:HALO], HALO - s, 0))


_shift_rows.defvjp(_shift_rows_fwd, _shift_rows_bwd)


def _lane_pick(x, lane):
    idx = lax.broadcasted_iota(i32, x.shape, x.ndim - 1)
    return jnp.sum(jnp.where(idx == lane, x, 0.0), axis=-1, keepdims=True)


def _silu(x):
    return x * jax.nn.sigmoid(x)


def _tri_inv(a):
    C = a.shape[-1]
    eye = (lax.broadcasted_iota(i32, a.shape, 1) == lax.broadcasted_iota(i32, a.shape, 2)).astype(f32)
    p = -a
    x = eye + p
    for _ in range((C - 1).bit_length() - 1):
        p = _dg(p, p, 'nn', True)
        x = x + _dg(x, p, 'nn', True)
    return x


@jax.custom_vjp
def _tri_solve(a, rhs):
    return _dg(_tri_inv(a), rhs, 'nn', True)


def _tri_solve_fwd(a, rhs):
    tinv = _tri_inv(a)
    sol = _dg(tinv, rhs, 'nn', True)
    return sol, (tinv, sol)


def _tri_solve_bwd(res, g):
    tinv, sol = res
    d_rhs = _dg(tinv, g, 'tn', True)
    return -_dg(d_rhs, sol, 'nt', True), d_rhs


_tri_solve.defvjp(_tri_solve_fwd, _tri_solve_bwd)


@functools.partial(jax.custom_vjp, nondiff_argnums=(1,))
def _lane_head(x, n):
    return x[:, :, :n]


def _lane_head_fwd(x, n):
    return x[:, :, :n], None


def _lane_head_bwd(n, _, g):
    s = jnp.sum(g, axis=-1, keepdims=True) * (1.0 / LANES)
    return (jnp.broadcast_to(s, g.shape[:-1] + (LANES,)),)


_lane_head.defvjp(_lane_head_fwd, _lane_head_bwd)


@functools.partial(jax.custom_vjp, nondiff_argnums=(1,))
def _last_row(x, c):
    return x[:, c - 1:, :]


def _last_row_fwd(x, c):
    return x[:, c - 1:, :], None


def _last_row_bwd(c, _, g):
    shape = (g.shape[0], c, g.shape[2])
    rows = lax.broadcasted_iota(i32, shape, 1)
    return (jnp.where(rows == c - 1, jnp.broadcast_to(g, shape), 0.0),)


_last_row.defvjp(_last_row_fwd, _last_row_bwd)


def _full(a):
    nd = a.ndim
    return (a, tuple(a.shape), lambda i, _nd=nd: (0,) * _nd)


def _rows(a, tm, col=0, width=None):
    width = a.shape[1] if width is None else width
    return (a, (tm, width), lambda i, _c=col: (i, _c))


def _tile_call(name, fn, n, ins, outs, accs=()):
    n_in, n_out, n_acc = len(ins), len(outs), len(accs)

    def body(*refs):
        i = pl.program_id(0)
        res = fn(i, *[r[...].astype(f32) for r in refs[:n_in]])
        if not isinstance(res, (tuple, list)):
            res = (res,)
        assert len(res) == n_out + n_acc, (name, len(res), n_out, n_acc)
        for r, v in zip(refs[n_in:n_in + n_out], res[:n_out]):
            r[...] = v.astype(r.dtype)
        if n_acc:
            acc_refs = refs[n_in + n_out:]

            @pl.when(i == 0)
            def _():
                for r in acc_refs:
                    r[...] = jnp.zeros(r.shape, r.dtype)

            for r, v in zip(acc_refs, res[n_out:]):
                r[...] += v.astype(r.dtype)

    out_shape = [jax.ShapeDtypeStruct(s, d) for s, d, _, _ in outs] + [jax.ShapeDtypeStruct(s, d) for s, d in accs]
    out_specs = [pl.BlockSpec(b, m) for _, _, b, m in outs]
    out_specs += [pl.BlockSpec(tuple(s), lambda i, _nd=len(s): (0,) * _nd) for s, _ in accs]
    res = pl.pallas_call(
        body, name=name, grid=(n,),
        in_specs=[pl.BlockSpec(b, m) for _, b, m in ins],
        out_specs=out_specs, out_shape=out_shape,
        compiler_params=pltpu.CompilerParams(dimension_semantics=("arbitrary",), vmem_limit_bytes=VMEM_LIMIT),
    )(*[a for a, _, _ in ins])
    return res


def _pick(n, cands):
    for c in cands:
        if n % c == 0:
            return c
    return n


def _mm(name, a, b, mode, out_dtype=f32, add=None, out_cols=None):
    if mode == 'tn':
        K, M = a.shape
    else:
        M, K = a.shape
    N = b.shape[0] if mode == 'nt' else b.shape[1]
    tm = _pick(M, (1024, 1408, 512, 256, 128))
    tn = _pick(N, (512, 1408, 256, 128)) if out_cols is None else out_cols[0]
    tuned = _tiles(name)
    if tuned is not None and (M % tuned[0] or N % tuned[1] or K % tuned[2]):
        tuned = None
    if tuned is not None:
        tm, tn = tuned[:2]
    col = (lambda j: j) if out_cols is None else out_cols[1]
    tk = K if K <= 3328 else _pick(K, (3328, 2816, 2048, 1024, 512, 256, 128))
    if tuned is not None:
        tk = tuned[2]
    nk = K // tk
    ca = 0 if mode == 'tn' else 1
    cb = 1 if mode == 'nt' else 0
    dims = (((ca,), (cb,)), ((), ()))

    def body(*refs):
        a_ref, b_ref = refs[0], refs[1]
        add_ref = refs[2] if add is not None else None
        part = lax.dot_general(a_ref[...].astype(bf16), b_ref[...].astype(bf16), dims, preferred_element_type=f32)

        def finish(r, o_ref):
            if add_ref is not None:
                r = r + add_ref[...].astype(f32)
            o_ref[...] = r.astype(o_ref.dtype)

        if nk == 1:
            finish(part, refs[-1])
            return
        o_ref, acc_ref = refs[-2], refs[-1]
        k = pl.program_id(2)

        @pl.when(k == 0)
        def _():
            acc_ref[...] = part

        @pl.when(k > 0)
        def _():
            acc_ref[...] += part

        @pl.when(k == nk - 1)
        def _():
            finish(acc_ref[...], o_ref)

    a_spec = pl.BlockSpec((tk, tm), lambda i, j, k: (k, i)) if mode == 'tn' else pl.BlockSpec((tm, tk), lambda i, j, k: (i, k))
    b_spec = pl.BlockSpec((tn, tk), lambda i, j, k: (j, k)) if mode == 'nt' else pl.BlockSpec((tk, tn), lambda i, j, k: (k, j))
    in_specs, args = [a_spec, b_spec], [a, b]
    if add is not None:
        in_specs.append(pl.BlockSpec((tm, tn), lambda i, j, k: (i, j)))
        args.append(add)
    return pl.pallas_call(
        body, name=name, grid=(M // tm, N // tn, nk), in_specs=in_specs,
        out_specs=pl.BlockSpec((tm, tn), lambda i, j, k: (i, col(j))),
        out_shape=jax.ShapeDtypeStruct((M, N), out_dtype),
        scratch_shapes=[pltpu.VMEM((tm, tn), f32)] if nk > 1 else [],
        compiler_params=pltpu.CompilerParams(dimension_semantics=("parallel", "parallel", "arbitrary"), vmem_limit_bytes=VMEM_LIMIT),
    )(*args)


def _res_ln(x, f, g, b, s):
    r = ALPHA * x + s * f
    mu = jnp.mean(r, axis=-1, keepdims=True)
    rc = r - mu
    var = jnp.mean(rc * rc, axis=-1, keepdims=True)
    return rc * lax.rsqrt(var + LN_EPS) * g + b


def _ln(x, g, b):
    mu = jnp.mean(x, axis=-1, keepdims=True)
    xc = x - mu
    var = jnp.mean(xc * xc, axis=-1, keepdims=True)
    return xc * lax.rsqrt(var + LN_EPS) * g + b


def _dn_pre(first, xc, xp, ba, cw, hp):
    xp = jnp.where(first, 0.0, xp)
    y = cw[DN_CONV - 1:DN_CONV, :] * xc
    for j in range(DN_CONV - 1):
        y = y + cw[j:j + 1, :] * _shift_rows(xc, xp, DN_CONV - 1 - j)
    c = _silu(y)
    qs, ks, vs, gs, bs = [], [], [], [], []
    nqk = DN_HEADS * DN_DK
    for h in range(DN_HEADS):
        q = c[:, h * DN_DK:(h + 1) * DN_DK]
        k = c[:, nqk + h * DN_DK:nqk + (h + 1) * DN_DK]
        v = c[:, 2 * nqk + h * DN_DK:2 * nqk + (h + 1) * DN_DK]
        qs.append(q * lax.rsqrt(jnp.sum(q * q, axis=-1, keepdims=True) + RMS_EPS))
        ks.append(k * lax.rsqrt(jnp.sum(k * k, axis=-1, keepdims=True) + RMS_EPS))
        vs.append(v)
        beta = jax.nn.sigmoid(_lane_pick(ba, h))
        a_log = _lane_pick(hp[0:1, :], h)
        dt = _lane_pick(hp[1:2, :], h)
        g = -jnp.exp(a_log) * jax.nn.softplus(_lane_pick(ba, DN_HEADS + h) + dt)
        gs.append(jnp.broadcast_to(g, q.shape))
        bs.append(jnp.broadcast_to(beta, q.shape))
    return tuple(jnp.stack(t, axis=0) for t in (qs, ks, vs, gs, bs))


def _wy(q, k, v, gb, bb):
    B, C, _ = q.shape
    ri = lax.broadcasted_iota(i32, (B, C, C), 1)
    ci = lax.broadcasted_iota(i32, (B, C, C), 2)
    tril, strict = ri >= ci, ri > ci
    gc = hdot(tril.astype(f32), gb, 'nn')
    gl = jnp.broadcast_to(_last_row(gc, C), gc.shape)
    col = _lane_head(gc, C)
    decay = jnp.exp(jnp.where(tril, col - jnp.swapaxes(col, 1, 2), NEG_INF))
    qs = q * (DN_DK ** -0.5)
    kb = k * bb
    a = jnp.where(strict, bdot(kb, k, 'nt') * decay, 0.0)
    sol = _tri_solve(a, jnp.concatenate([v * bb, kb * jnp.exp(gc)], axis=-1))
    qk = jnp.where(tril, bdot(qs, k, 'nt') * decay, 0.0)
    kt = k * jnp.exp(gl - gc)
    qd = qs * jnp.exp(gc)
    return qd, qk, sol[..., :DN_DK], sol[..., DN_DK:], kt, jnp.exp(gl)


def _scan_step(S, qd, qk, u, w, kt, egl):
    vn = u - bdot(w, S, 'nn')
    o = bdot(qd, S, 'nn') + bdot(qk, vn, 'nn')
    e2 = jnp.concatenate([egl] * (DN_DK // DN_CHUNK), axis=1)
    return o, S * e2 + bdot(kt, vn, 'tn')


def _dn_post(o, z, nw):
    outs = []
    for h in range(DN_HEADS):
        oh = o[h]
        oh = oh * lax.rsqrt(jnp.mean(oh * oh, axis=-1, keepdims=True) + RMS_EPS) * nw
        outs.append(oh * _silu(z[:, h * DN_DK:(h + 1) * DN_DK]))
    return jnp.concatenate(outs, axis=1)


def _swa(first, q, kc, kp, vc, vp, snk):
    W = q.shape[1]
    G = SWA_HEADS // SWA_KV_HEADS
    r = lax.broadcasted_iota(i32, (G, W, 2 * W), 1)
    c = lax.broadcasted_iota(i32, (G, W, 2 * W), 2)
    mask = (c > W + r - WINDOW) & (c <= W + r) & jnp.logical_or(c >= W, jnp.logical_not(first))
    sink_all = _lane_pick(snk, 0)
    outs = []
    for j in range(SWA_KV_HEADS):
        qj = q[j * G:(j + 1) * G]
        kk = jnp.broadcast_to(jnp.concatenate([kp[j], kc[j]], axis=0)[None], (G, 2 * W, SWA_DH))
        vv = jnp.broadcast_to(jnp.concatenate([vp[j], vc[j]], axis=0)[None], (G, 2 * W, SWA_DH))
        s = jnp.where(mask, bdot(qj, kk, 'nt') * (SWA_DH ** -0.5), NEG_INF)
        sink = sink_all[j * G:(j + 1) * G][:, :, None]
        m = jnp.maximum(jnp.max(s, axis=-1, keepdims=True), sink)
        p = jnp.exp(s - m)
        p = p / (jnp.sum(p, axis=-1, keepdims=True) + jnp.exp(sink - m))
        outs.append(bdot(p, vv, 'nn'))
    return jnp.concatenate(outs, axis=0)


def _xa(q, kv):
    outs = []
    nk = XA_HEADS * XA_DH
    for h in range(XA_HEADS):
        qh = q[:, h * XA_DH:(h + 1) * XA_DH]
        kh = kv[:, h * XA_DH:(h + 1) * XA_DH]
        vh = kv[:, nk + h * XA_DH:nk + (h + 1) * XA_DH]
        s = bdot(qh, kh, 'nt') * (XA_DH ** -0.5)
        m = jnp.max(s, axis=-1, keepdims=True)
        p = jnp.exp(s - m)
        p = p / jnp.sum(p, axis=-1, keepdims=True)
        outs.append(bdot(p, vh, 'nn'))
    return jnp.concatenate(outs, axis=1)


def _merge(o_dn, o_sw, o_xa, gates, wb):
    acc = None
    for n, o in enumerate((o_dn, o_sw, o_xa)):
        t = jax.nn.sigmoid(gates[:, n * D_MODEL:(n + 1) * D_MODEL]) * bdot(o, wb[n], 'nn')
        acc = t if acc is None else acc + t
    return acc


def _row_tile(T, want):
    return _pick(T, tuple(c for c in (1024, 512, 256, 128, 64) if c <= want))


def _res_ln_bwd(name, x, f, g, b, s, dh):
    T = x.shape[0]
    tm = _row_tile(T, 1024)

    def fn(i, x, f, g, b, dh):
        _, vjp = jax.vjp(lambda x, f, g, b: _res_ln(x, f, g, b, s), x, f, g, b)
        return vjp(dh)

    return _tile_call(name, fn, T // tm, [_rows(x, tm), _rows(f, tm), _full(g), _full(b), _rows(dh, tm)],
                      [((T, D_MODEL), f32, (tm, D_MODEL), lambda i: (i, 0)), ((T, D_MODEL), bf16, (tm, D_MODEL), lambda i: (i, 0))],
                      [((1, D_MODEL), f32), ((1, D_MODEL), f32)])


FF_BLK = D_FF // 2


def _pair_gate_up(w):
    d = w.shape[0]
    return jnp.stack([w[:, :D_FF].reshape(d, -1, FF_BLK), w[:, D_FF:].reshape(d, -1, FF_BLK)], axis=2).reshape(d, 2 * D_FF)


def _swiglu_blk(u):
    return _silu(u[:, :FF_BLK]) * u[:, FF_BLK:]


def _mm_spec():
    return dict(compiler_params=pltpu.CompilerParams(dimension_semantics=("parallel", "arbitrary"), vmem_limit_bytes=VMEM_LIMIT))


def _gu_act(name, xb, w_pair):
    T, D = xb.shape
    tm = _pick(T, (_tiles(name, (512,))[0], 512, 256, 128))

    def body(x_ref, w_ref, u_ref, a_ref):
        r = lax.dot_general(x_ref[...], w_ref[...], (((1,), (0,)), ((), ())), preferred_element_type=f32)
        u_ref[...] = r.astype(bf16)
        a_ref[...] = _swiglu_blk(r).astype(bf16)

    return pl.pallas_call(
        body, name=name, grid=(T // tm, D_FF // FF_BLK),
        in_specs=[pl.BlockSpec((tm, D), lambda i, j: (i, 0)), pl.BlockSpec((D, 2 * FF_BLK), lambda i, j: (0, j))],
        out_specs=[pl.BlockSpec((tm, 2 * FF_BLK), lambda i, j: (i, j)), pl.BlockSpec((tm, FF_BLK), lambda i, j: (i, j))],
        out_shape=[jax.ShapeDtypeStruct((T, 2 * D_FF), bf16), jax.ShapeDtypeStruct((T, D_FF), bf16)],
        **_mm_spec())(xb, w_pair)


def _down_dx_act_bwd(name, df, w_down_t, u):
    T, D = df.shape
    tm = _pick(T, (_tiles(name, (256,))[0], 256, 128))

    def body(df_ref, w_ref, u_ref, du_ref):
        da = lax.dot_general(df_ref[...], w_ref[...], (((1,), (0,)), ((), ())), preferred_element_type=f32)
        _, vjp = jax.vjp(_swiglu_blk, u_ref[...].astype(f32))
        du_ref[...] = vjp(da)[0].astype(bf16)

    return pl.pallas_call(
        body, name=name, grid=(T // tm, D_FF // FF_BLK),
        in_specs=[pl.BlockSpec((tm, D), lambda i, j: (i, 0)), pl.BlockSpec((D, FF_BLK), lambda i, j: (0, j)),
                  pl.BlockSpec((tm, 2 * FF_BLK), lambda i, j: (i, j))],
        out_specs=pl.BlockSpec((tm, 2 * FF_BLK), lambda i, j: (i, j)),
        out_shape=jax.ShapeDtypeStruct((T, 2 * D_FF), bf16),
        **_mm_spec())(df, w_down_t, u)


def _mm_res_ln(name, a, w, x, g, b, s):
    T, K = a.shape
    tm = _pick(T, (_tiles(name, (512,))[0], 512, 256, 128))

    def body(a_ref, w_ref, x_ref, g_ref, b_ref, f_ref, h_ref, hb_ref):
        f = lax.dot_general(a_ref[...], w_ref[...], (((1,), (0,)), ((), ())), preferred_element_type=f32)
        h = _res_ln(x_ref[...], f, g_ref[...], b_ref[...], s)
        f_ref[...] = f.astype(bf16)
        h_ref[...] = h
        hb_ref[...] = h.astype(bf16)

    row = pl.BlockSpec((tm, D_MODEL), lambda i: (i, 0))
    vec = pl.BlockSpec((1, D_MODEL), lambda i: (0, 0))
    return pl.pallas_call(
        body, name=name, grid=(T // tm,),
        in_specs=[pl.BlockSpec((tm, K), lambda i: (i, 0)), pl.BlockSpec((K, D_MODEL), lambda i: (0, 0)), row, vec, vec],
        out_specs=[row, row, row],
        out_shape=[jax.ShapeDtypeStruct((T, D_MODEL), bf16), jax.ShapeDtypeStruct((T, D_MODEL), f32), jax.ShapeDtypeStruct((T, D_MODEL), bf16)],
        compiler_params=pltpu.CompilerParams(dimension_semantics=("parallel",), vmem_limit_bytes=VMEM_LIMIT))(a, w, x, g, b)


def _hm_spec(tm, w=DN_DK):
    return ((DN_HEADS, tm, w), lambda i: (0, i, 0))


def _prev(i):
    return jnp.maximum(i - 1, 0)


def _dn_front(first, xc, xp, ba, cw, hp):
    tm = xc.shape[0]
    nc = tm // DN_CHUNK
    outs = _wy(*[t.reshape(DN_HEADS * nc, DN_CHUNK, DN_DK) for t in _dn_pre(first, xc, xp, ba, cw, hp)])
    return tuple(o.reshape(DN_HEADS, tm, o.shape[-1]) for o in outs)


def _dn_front_ins(p, pba, cw, hp, tm):
    W3 = 3 * DN_HEADS * DN_DK
    return [(p, (tm, W3), lambda i: (i, 0)), (p, (HALO, W3), lambda i: (jnp.maximum(i * (tm // HALO) - 1, 0), 0)),
            (pba, (tm, LANES), lambda i: (i, 0)), _full(cw), _full(hp)]


def _dn_front_fwd(name, p, pba, cw, hp):
    T = p.shape[0]
    tm = _row_tile(T, 256)
    blk, im = _hm_spec(tm)
    blk2, im2 = _hm_spec(tm, DN_CHUNK)
    full = (DN_HEADS, T, DN_DK)
    outs = [(full, bf16, blk, im), ((DN_HEADS, T, DN_CHUNK), bf16, blk2, im2), (full, f32, blk, im),
            (full, bf16, blk, im), (full, bf16, blk, im), (full, f32, blk, im)]
    return _tile_call(name, lambda i, *a: _dn_front(i == 0, *a), T // tm, _dn_front_ins(p, pba, cw, hp, tm), outs)


def _dn_front_bwd(name, p, pba, cw, hp, cts):
    T = p.shape[0]
    tm = _row_tile(T, 256)
    n = T // tm
    W3 = 3 * DN_HEADS * DN_DK
    blk, im = _hm_spec(tm)
    blk2, im2 = _hm_spec(tm, DN_CHUNK)
    ins = _dn_front_ins(p, pba, cw, hp, tm) + [(c, blk2 if c.shape[-1] == DN_CHUNK else blk, im) for c in cts]

    def fn(i, xc, xp, ba, cw, hp, *cts):
        _, vjp = jax.vjp(lambda *a: _dn_front(i == 0, *a), xc, xp, ba, cw, hp)
        return vjp(tuple(cts))

    return _tile_call(name, fn, n, ins,
                      [((T, W3), bf16, (tm, W3), lambda i: (i, 0)), ((n * HALO, W3), f32, (HALO, W3), lambda i: (i, 0)),
                       ((T, LANES), f32, (tm, LANES), lambda i: (i, 0))],
                      [(tuple(cw.shape), f32), (tuple(hp.shape), f32)])


def _scan_fwd(name, qd, qk, u, w, kt, egl):
    T = qd.shape[1]
    C = DN_CHUNK
    U = _pick(T // C, (SCAN_CHUNKS, 1))
    n = T // (C * U)

    def body(qd_ref, qk_ref, u_ref, w_ref, kt_ref, egl_ref, o_ref, sall_ref, s_ref):
        i = pl.program_id(0)

        @pl.when(i == 0)
        def _():
            s_ref[...] = jnp.zeros(s_ref.shape, f32)

        S = s_ref[...]
        for j in range(U):
            sl = pl.ds(j * C, C)
            sall_ref[:, j] = S
            o, S = _scan_step(S, *[r[:, sl, :] for r in (qd_ref, qk_ref, u_ref, w_ref, kt_ref, egl_ref)])
            o_ref[:, sl, :] = o
        s_ref[...] = S

    blk, im = _hm_spec(C * U)
    blk2, im2 = _hm_spec(C * U, C)
    return pl.pallas_call(
        body, name=name, grid=(n,),
        in_specs=[pl.BlockSpec(blk, im), pl.BlockSpec(blk2, im2)] + [pl.BlockSpec(blk, im)] * 4,
        out_specs=[pl.BlockSpec(blk, im), pl.BlockSpec((DN_HEADS, U, DN_DK, DN_DK), lambda i: (0, i, 0, 0))],
        out_shape=[jax.ShapeDtypeStruct((DN_HEADS, T, DN_DK), f32), jax.ShapeDtypeStruct((DN_HEADS, n * U, DN_DK, DN_DK), f32)],
        scratch_shapes=[pltpu.VMEM((DN_HEADS, DN_DK, DN_DK), f32)],
        compiler_params=pltpu.CompilerParams(dimension_semantics=("arbitrary",), vmem_limit_bytes=VMEM_LIMIT),
    )(qd, qk, u, w, kt, egl)


def _scan_bwd(name, qd, qk, u, w, kt, egl, s_all, do):
    T = qd.shape[1]
    C = DN_CHUNK
    U = _pick(T // C, (SCAN_CHUNKS, 1))
    n = T // (C * U)

    def body(qd_ref, qk_ref, u_ref, w_ref, kt_ref, egl_ref, sall_ref, do_ref,
             dqd_ref, dqk_ref, du_ref, dw_ref, dkt_ref, degl_ref, ds_ref):
        i = pl.program_id(0)

        @pl.when(i == 0)
        def _():
            ds_ref[...] = jnp.zeros(ds_ref.shape, f32)

        dS = ds_ref[...]
        for j in reversed(range(U)):
            sl = pl.ds(j * C, C)
            args = [r[:, sl, :].astype(f32) for r in (qd_ref, qk_ref, u_ref, w_ref, kt_ref, egl_ref)]
            _, vjp = jax.vjp(_scan_step, sall_ref[:, j], *args)
            dS, *cts = vjp((do_ref[:, sl, :], dS))
            for r, v in zip((dqd_ref, dqk_ref, du_ref, dw_ref, dkt_ref, degl_ref), cts):
                r[:, sl, :] = v
        ds_ref[...] = dS

    blk = (DN_HEADS, C * U, DN_DK)
    blk2 = (DN_HEADS, C * U, C)
    rim = lambda i: (0, n - 1 - i, 0)
    sp, sp2 = pl.BlockSpec(blk, rim), pl.BlockSpec(blk2, rim)
    full, full2 = jax.ShapeDtypeStruct((DN_HEADS, T, DN_DK), f32), jax.ShapeDtypeStruct((DN_HEADS, T, C), f32)
    return pl.pallas_call(
        body, name=name, grid=(n,),
        in_specs=[sp, sp2, sp, sp, sp, sp, pl.BlockSpec((DN_HEADS, U, DN_DK, DN_DK), lambda i: (0, n - 1 - i, 0, 0)), sp],
        out_specs=[sp, sp2, sp, sp, sp, sp],
        out_shape=[full, full2, full, full, full, full],
        scratch_shapes=[pltpu.VMEM((DN_HEADS, DN_DK, DN_DK), f32)],
        compiler_params=pltpu.CompilerParams(dimension_semantics=("arbitrary",), vmem_limit_bytes=VMEM_LIMIT),
    )(qd, qk, u, w, kt, egl, s_all, do)


def _dn_post_fwd(name, o, p, nw):
    T = p.shape[0]
    tm = _row_tile(T, 512)
    blk, im = _hm_spec(tm)
    W = DN_HEADS * DN_DK
    return _tile_call(name, lambda i, o, z, nw: _dn_post(o, z, nw), T // tm,
                      [(o, blk, im), (p, (tm, W), lambda i: (i, 1536 // W)), _full(nw)],
                      [((T, W), f32, (tm, W), lambda i: (i, 0))])[0]


def _dn_post_bwd(name, o, p, nw, dout):
    T = p.shape[0]
    tm = _row_tile(T, 512)
    blk, im = _hm_spec(tm)
    W = DN_HEADS * DN_DK

    def fn(i, o, z, nw, dout):
        _, vjp = jax.vjp(_dn_post, o, z, nw)
        return vjp(dout)

    return _tile_call(name, fn, T // tm,
                      [(o, blk, im), (p, (tm, W), lambda i: (i, 1536 // W)), _full(nw), _rows(dout, tm)],
                      [((DN_HEADS, T, DN_DK), f32, blk, im), ((T, W), f32, (tm, W), lambda i: (i, 0))],
                      [((1, DN_DK), f32)])


def _swa_tile(T, big):
    return _pick(T, (2 * WINDOW, WINDOW)) if big else WINDOW


@functools.partial(jax.custom_vjp, nondiff_argnums=(1,))
def _heads_split(x, nh):
    return jnp.stack([x[:, SWA_DH * h:SWA_DH * (h + 1)] for h in range(nh)], axis=0)


@jax.custom_vjp
def _heads_merge(y):
    return jnp.concatenate([y[h] for h in range(y.shape[0])], axis=1)


_heads_split.defvjp(lambda x, nh: (_heads_split(x, nh), None), lambda nh, _, g: (_heads_merge(g),))
_heads_merge.defvjp(lambda y: (_heads_merge(y), None), lambda _, g: (_heads_split(g, g.shape[1] // SWA_DH),))


def _swa_rows(first, qp, kc, kp, vc, vp, snk):
    kv = [_heads_split(t, SWA_KV_HEADS) for t in (kc, kp, vc, vp)]
    return _heads_merge(_swa(first, _heads_split(qp, SWA_HEADS), *kv, snk))


def _swa_ins(p, snk, W):
    qw, kw = SWA_HEADS * SWA_DH, SWA_KV_HEADS * SWA_DH
    return [(p, (W, qw), lambda i: (i, 2048 // qw)),
            (p, (W, kw), lambda i: (i, 6144 // kw)), (p, (W, kw), lambda i: (_prev(i), 6144 // kw)),
            (p, (W, kw), lambda i: (i, 6272 // kw)), (p, (W, kw), lambda i: (_prev(i), 6272 // kw)), _full(snk)]


def _swa_fwd(name, p, snk):
    T = p.shape[0]
    W = _swa_tile(T, True)
    qw = SWA_HEADS * SWA_DH
    return _tile_call(name, lambda i, *a: _swa_rows(i == 0, *a), T // W, _swa_ins(p, snk, W),
                      [((T, qw), f32, (W, qw), lambda i: (i, 0))])[0]


def _swa_bwd(name, p, snk, do):
    T = p.shape[0]
    W = _swa_tile(T, False)
    qw, kw = SWA_HEADS * SWA_DH, SWA_KV_HEADS * SWA_DH

    def fn(i, qp, k_c, k_p, v_c, v_p, snk, do):
        _, vjp = jax.vjp(lambda *a: _swa_rows(i == 0, *a), qp, k_c, k_p, v_c, v_p, snk)
        return vjp(do)

    kv_out = ((T, kw), f32, (W, kw), lambda i: (i, 0))
    return _tile_call(name, fn, T // W, _swa_ins(p, snk, W) + [(do, (W, qw), lambda i: (i, 0))],
                      [((T, qw), bf16, (W, qw), lambda i: (i, 0)), kv_out, kv_out, kv_out, kv_out], [(tuple(snk.shape), f32)])


def _xa_fwd(name, p, kv):
    T = p.shape[0]
    tm = _row_tile(T, 512)
    W = XA_HEADS * XA_DH
    return _tile_call(name, lambda i, q, kv: _xa(q, kv), T // tm, [(p, (tm, W), lambda i: (i, 2560 // W)), _full(kv)],
                      [((T, W), f32, (tm, W), lambda i: (i, 0))])[0]


def _xa_bwd(name, p, kv, do):
    T = p.shape[0]
    tm = _row_tile(T, 512)
    W = XA_HEADS * XA_DH

    def fn(i, q, kv, do):
        _, vjp = jax.vjp(_xa, q, kv)
        return vjp(do)

    return _tile_call(name, fn, T // tm, [(p, (tm, W), lambda i: (i, 2560 // W)), _full(kv), _rows(do, tm)],
                      [((T, W), bf16, (tm, W), lambda i: (i, 0))], [(tuple(kv.shape), f32)])


def _merge_fwd(name, o_dn, o_sw, o_xa, p, wb):
    T = p.shape[0]
    tm = _row_tile(T, 512)
    GW = N_BRANCH * D_MODEL
    return _tile_call(name, lambda i, a, b, c, g, w: _merge(a, b, c, g, w.astype(f32)), T // tm,
                      [_rows(o_dn, tm), _rows(o_sw, tm), _rows(o_xa, tm), (p, (tm, GW), lambda i: (i, 1)), _full(wb)],
                      [((T, D_MODEL), bf16, (tm, D_MODEL), lambda i: (i, 0))])[0]


def _merge_bwd(name, o_dn, o_sw, o_xa, p, wb, dm):
    T = p.shape[0]
    tm = _row_tile(T, 512)
    GW = N_BRANCH * D_MODEL

    def fn(i, a, b, c, g, w, dm):
        _, vjp = jax.vjp(_merge, a, b, c, g, w.astype(f32))
        return vjp(dm)

    bo = ((T, BRANCH_W), bf16, (tm, BRANCH_W), lambda i: (i, 0))
    return _tile_call(name, fn, T // tm,
                      [_rows(o_dn, tm), _rows(o_sw, tm), _rows(o_xa, tm), (p, (tm, GW), lambda i: (i, 1)), _full(wb), _rows(dm, tm)],
                      [bo, bo, bo, ((T, GW), bf16, (tm, GW), lambda i: (i, 0))], [(tuple(wb.shape), f32)])


def _assemble_dp(name, dxc, dxp, dz, dswq, dxaq, dgates, dkc, dkp, dvc, dvp, dba):
    T = dz.shape[0]
    tm = _row_tile(T, 256)
    n = T // tm
    r = tm // WINDOW
    nb = T // WINDOW

    def fn(i, dxc, dxp, dz, dswq, dxaq, dgates, dkc, dvc, dba, *nxt):
        halo = jnp.where(i == n - 1, 0.0, dxp)
        dqkv = dxc + jnp.concatenate([jnp.zeros((tm - HALO, dxc.shape[1]), f32), halo], axis=0)
        shifted = [jnp.concatenate([jnp.where(i * r + 1 + b <= nb - 1, blk, 0.0) for b, blk in enumerate(half)], axis=0)
                   for half in (nxt[:r], nxt[r:])]
        parts = [dqkv, dz, dswq, dxaq, dgates, dkc + shifted[0], dvc + shifted[1], dba, jnp.zeros((tm, LANES), f32)]
        return jnp.concatenate(parts, axis=1)

    ins = [_rows(dxc, tm), (dxp, (HALO, dxp.shape[1]), lambda i: (jnp.minimum(i + 1, n - 1), 0))]
    ins += [_rows(t, tm) for t in (dz, dswq, dxaq, dgates, dkc, dvc, dba)]
    for t in (dkp, dvp):
        ins += [(t, (WINDOW, t.shape[1]), lambda i, b=b: (jnp.minimum(i * r + 1 + b, nb - 1), 0)) for b in range(r)]
    return _tile_call(name, fn, n, ins, [((T, D_INP), bf16, (tm, D_INP), lambda i: (i, 0))])[0]


def _loss_grad(name, y, tgt):
    T = y.shape[0]
    tm = _row_tile(T, 1024)

    def fn(i, y, t):
        e = y - t
        part = 0.5 * jnp.sum(jnp.mean(e * e, axis=-1, keepdims=True), axis=0, keepdims=True)
        return e * (1.0 / D_MODEL), jnp.broadcast_to(part, (1, LANES))

    return _tile_call(name, fn, T // tm, [_rows(y, tm), _rows(tgt, tm)],
                      [((T, D_MODEL), f32, (tm, D_MODEL), lambda i: (i, 0))], [((1, LANES), f32)])


def _mem_ln_fwd(name, mem, g, b):
    M = mem.shape[0]
    return _tile_call(name, lambda i, m, g, b: (_ln(m, g, b),), 1, [_full(mem), _full(g), _full(b)],
                      [((M, D_MODEL), bf16, (M, D_MODEL), lambda i: (0, 0))])[0]


def _mem_ln_bwd(name, mem, g, b, dmn):
    def fn(i, m, g, b, d):
        _, vjp = jax.vjp(lambda g, b: _ln(m, g, b), g, b)
        return vjp(d)

    return _tile_call(name, fn, 1, [_full(mem), _full(g), _full(b), _full(dmn)], [], [((1, D_MODEL), f32), ((1, D_MODEL), f32)])


W_IN_RANGES = ((0, 1536), (1544, 2056), (2056, 2568), (2824, 3336), (3336, 6408), (2568, 2696), (2696, 2824), (1536, 1544))
W_IN_OFFS = tuple(sum(b - a for a, b in W_IN_RANGES[:k]) for k in range(len(W_IN_RANGES)))


def _pad_w_in(cols):
    parts = [cols(a, b) for a, b in W_IN_RANGES]
    return jnp.concatenate(parts + [jnp.zeros((parts[0].shape[0], D_INP - D_IN), parts[0].dtype)], axis=1)


def _unpad_dw_in(d, lo=0, hi=D_IN):
    parts = []
    for (a, b), off in sorted(zip(W_IN_RANGES, W_IN_OFFS)):
        a2, b2 = max(a, lo), min(b, hi)
        if a2 < b2:
            parts.append(d[:, off + a2 - a:off + b2 - a])
    return jnp.concatenate(parts, axis=1)


def _shard_cols(shards, l):
    n = shards[0].shape[-1]

    def cols(a, b):
        parts = [shards[s][l][:, max(a, s * n) - s * n:min(b, (s + 1) * n) - s * n] for s in range(N_SHARD)
                 if max(a, s * n) < min(b, (s + 1) * n)]
        return parts[0] if len(parts) == 1 else jnp.concatenate(parts, axis=1)

    return cols


def _layer_fwd(l, x, xb, mem_nb, W):
    n = lambda s: f"l{l}_{s}"
    sv = {}
    u1, a1 = _gu_act(n("ffn1_gu"), xb, W['ffn1_w_gu_p'])
    f1, h1, h1b = _mm_res_ln(n("ffn1_down"), a1, W['ffn1_w_down'], x, W['ln_g'][0:1], W['ln_b'][0:1], 0.5)
    p = _mm(n("w_in"), h1b, W['w_inp'], 'nn', out_dtype=bf16)
    pba = _mm(n("w_in_ba"), h1b, W['w_ba'], 'nn')
    qd, qk, u, w, kt, egl = _dn_front_fwd(n("dn_front"), p, pba, W['conv_w'], W['hp'])
    o_raw, s_all = _scan_fwd(n("dn_scan"), qd, qk, u, w, kt, egl)
    o_dn = _dn_post_fwd(n("dn_post"), o_raw, p, W['norm_w'])
    o_sw = _swa_fwd(n("swa"), p, W['snk'])
    kv = _mm(n("mem_kv"), mem_nb, W['w_mem_kv'], 'nn')
    o_xa = _xa_fwd(n("xa"), p, kv)
    merged = _merge_fwd(n("merge"), o_dn, o_sw, o_xa, p, W['w_branch'])
    mo, h2, h2b = _mm_res_ln(n("w_out"), merged, W['w_out'], h1, W['ln_g'][1:2], W['ln_b'][1:2], 1.0)
    u2, a2 = _gu_act(n("ffn2_gu"), h2b, W['ffn2_w_gu_p'])
    f2, h3, h3b = _mm_res_ln(n("ffn2_down"), a2, W['ffn2_w_down'], h2, W['ln_g'][2:3], W['ln_b'][2:3], 0.5)
    sv = dict(x=x, xb=xb, u1=u1, a1=a1, f1=f1, h1=h1, h1b=h1b, p=p, pba=pba, wy=(qd, qk, u, w, kt, egl),
              o_raw=o_raw, s_all=s_all, o_dn=o_dn, o_sw=o_sw, kv=kv, o_xa=o_xa, merged=merged,
              mo=mo, h2=h2, h2b=h2b, u2=u2, a2=a2, f2=f2)
    return h3, h3b, sv


def _ffn_bwd(n, tag, x, xb, u, a, f, g, b, w_gu_t, w_down_t, dh):
    dx_a, df, dg, db = _res_ln_bwd(n(f"{tag}_ln_bwd"), x, f, g, b, 0.5, dh)
    d_down = _mm(n(f"{tag}_down_dw"), a, df, 'tn')
    du = _down_dx_act_bwd(n(f"{tag}_down_dx"), df, w_down_t, u)
    d_gu = _mm(n(f"{tag}_gu_dw"), xb, du, 'tn', out_cols=(FF_BLK, lambda j: (j % 2) * (D_FF // FF_BLK) + j // 2))
    dx = _mm(n(f"{tag}_gu_dx"), du, w_gu_t, 'nn', add=dx_a)
    return dx, d_gu, d_down, dg, db


def _layer_bwd(l, sv, mem_nb, W, dh3, dmem_acc):
    n = lambda s: f"l{l}_{s}"
    G = {}
    dh2, G['ffn2_w_gu'], G['ffn2_w_down'], dg2, db2 = _ffn_bwd(
        n, "ffn2", sv['h2'], sv['h2b'], sv['u2'], sv['a2'], sv['f2'], W['ln_g'][2:3], W['ln_b'][2:3], W['ffn2_w_gu_t'], W['ffn2_w_down_t'], dh3)
    dh1_a, dmo, dg1, db1 = _res_ln_bwd(n("ln1_bwd"), sv['h1'], sv['mo'], W['ln_g'][1:2], W['ln_b'][1:2], 1.0, dh2)
    G['w_out'] = _mm(n("w_out_dw"), sv['merged'], dmo, 'tn')
    dmerged = _mm(n("w_out_dx"), dmo, W['w_out_t'], 'nn')
    p = sv['p']
    do_dn, do_sw, do_xa, dgates, G['w_branch'] = _merge_bwd(n("merge_bwd"), sv['o_dn'], sv['o_sw'], sv['o_xa'], p, W['w_branch'], dmerged)
    dxaq, dkv = _xa_bwd(n("xa_bwd"), p, sv['kv'], do_xa)
    dkv = dkv.astype(bf16)
    G['w_mem_kv'] = _mm(n("mem_kv_dw"), mem_nb, dkv, 'tn')
    dmem_n = _mm(n("mem_kv_dx"), dkv, W['w_mem_kv_t'], 'nn', add=dmem_acc)
    dswq, dkc, dkp, dvc, dvp, dsnk = _swa_bwd(n("swa_bwd"), p, W['snk'], do_sw)
    do_raw, dz, dnw = _dn_post_bwd(n("dn_post_bwd"), sv['o_raw'], p, W['norm_w'], do_dn)
    cts = _scan_bwd(n("dn_scan_bwd"), *sv['wy'], sv['s_all'], do_raw)
    dxc, dxp, dba, dcw, dhp = _dn_front_bwd(n("dn_front_bwd"), p, sv['pba'], W['conv_w'], W['hp'], cts)
    dp = _assemble_dp(n("dp"), dxc, dxp, dz, dswq, dxaq, dgates, dkc, dkp, dvc, dvp, dba)
    G['w_in_p'] = _mm(n("w_in_dw"), sv['h1b'], dp, 'tn')
    dh1 = _mm(n("w_in_dx"), dp, W['w_inp_t'], 'nn', add=dh1_a)
    dx, G['ffn1_w_gu'], G['ffn1_w_down'], dg0, db0 = _ffn_bwd(
        n, "ffn1", sv['x'], sv['xb'], sv['u1'], sv['a1'], sv['f1'], W['ln_g'][0:1], W['ln_b'][0:1], W['ffn1_w_gu_t'], W['ffn1_w_down_t'], dh1)
    G['ln_g'] = jnp.concatenate([dg0, dg1, dg2], axis=0)
    G['ln_b'] = jnp.concatenate([db0, db1, db2], axis=0)
    G['dn_conv_w'] = dcw
    G['dn_a_log'] = dhp[0, :DN_HEADS]
    G['dn_dt_bias'] = dhp[1, :DN_HEADS]
    G['dn_norm_w'] = dnw[0]
    G['swa_sinks'] = dsnk[:, 0]
    return dx, dmem_n, G


def _local_step(x, mem, tgt, Wf):
    mem_g, mem_b = Wf['mem_ln_g'][None, :], Wf['mem_ln_b'][None, :]
    mem_nb = _mem_ln_fwd("mem_ln", mem, mem_g, mem_b)

    def paired(k, l):
        return Wf[k + '_p'][l] if k + '_p' in Wf else _pair_gate_up(Wf[k][l])

    def w_in_cols(l):
        return _shard_cols(Wf['w_in_sh'], l) if 'w_in_sh' in Wf else (lambda a, b: Wf['w_in'][l][:, a:b])

    layers = []
    for l in range(DEPTH):
        layers.append(dict(
            ln_g=Wf['ln_g'][l], ln_b=Wf['ln_b'][l], ffn1_w_gu_p=paired('ffn1_w_gu', l), ffn1_w_down=Wf['ffn1_w_down'][l],
            w_inp=_pad_w_in(w_in_cols(l)), conv_w=Wf['dn_conv_w'][l],
            hp=jnp.zeros((8, LANES), f32).at[0, :DN_HEADS].set(Wf['dn_a_log'][l]).at[1, :DN_HEADS].set(Wf['dn_dt_bias'][l]),
            norm_w=Wf['dn_norm_w'][l][None, :], snk=jnp.broadcast_to(Wf['swa_sinks'][l][:, None], (SWA_HEADS, LANES)),
            w_mem_kv=Wf['w_mem_kv'][l], w_branch=Wf['w_branch'][l], w_out=Wf['w_out'][l],
            ffn2_w_gu_p=paired('ffn2_w_gu', l), ffn2_w_down=Wf['ffn2_w_down'][l]))
        layers[l]['w_ba'] = layers[l]['w_inp'][:, 6400:6400 + LANES]
        for k in ('ffn1_w_gu_p', 'ffn1_w_down', 'w_inp', 'w_mem_kv', 'w_out', 'ffn2_w_gu_p', 'ffn2_w_down'):
            layers[l][k.replace('_p', '') + '_t'] = layers[l][k].T
    h, hb = x, x.astype(bf16)
    saved = []
    for l in range(DEPTH):
        h, hb, sv = _layer_fwd(l, h, hb, mem_nb, layers[l])
        saved.append(sv)
    dh, loss_row = _loss_grad("loss", h, tgt)
    grads = [None] * DEPTH
    dmem_n = None
    for l in reversed(range(DEPTH)):
        dh, dmem_n, grads[l] = _layer_bwd(l, saved[l], mem_nb, layers[l], dh, dmem_n)
    dmg, dmb = _mem_ln_bwd("mem_ln_bwd", mem, mem_g, mem_b, dmem_n)
    G = {k: jnp.stack([grads[l][k] for l in range(DEPTH)], axis=0) for k in grads[0]}
    G['mem_ln_g'], G['mem_ln_b'] = dmg[0], dmb[0]
    return loss_row, dh, G


SEG_ALIGN = 2048


def _round_up(n, m):
    return (n + m - 1) // m * m


def _layout(names, sizes, mult):
    off, table = 0, {}
    for nm in names:
        table[nm] = (off, sizes[nm])
        off += _round_up(sizes[nm], SEG_ALIGN)
    return table, _round_up(off // LANES, mult)


def _pack(table, rows, flat):
    names = list(table)
    lead = flat[names[0]].shape[:-1]
    parts, pos = [], 0
    for nm in names:
        off, size = table[nm]
        if off > pos:
            parts.append(jnp.zeros(lead + (off - pos,), flat[nm].dtype))
        parts.append(flat[nm])
        pos = off + size
    total = rows * LANES
    if total > pos:
        parts.append(jnp.zeros(lead + (total - pos,), parts[-1].dtype))
    return jnp.concatenate(parts, axis=-1).reshape(lead + (rows, LANES))


def _unpack(table, packed, nm):
    off, size = table[nm]
    flat = packed.reshape(packed.shape[:-2] + (-1,))
    return flat[..., off:off + size]


def _split_shards(full, ax):
    shp = full.shape
    t = full.reshape(shp[:ax] + (N_SHARD, shp[ax] // N_SHARD) + shp[ax + 1:])
    return jnp.moveaxis(t, ax, 0)


def _join_shards(sh4, ax):
    return jnp.concatenate([sh4[s] for s in range(N_SHARD)], axis=ax)


ANY = pl.BlockSpec(memory_space=pl.ANY)


def _me():
    return lax.axis_index("x"), lax.axis_index("y"), lax.axis_index("c")


def _comm_call(name, body, arrays, out_shapes, n_sem):
    n = len(arrays)
    scratch = [pltpu.SemaphoreType.DMA((n, n_sem)), pltpu.SemaphoreType.DMA((n, n_sem))]
    return pl.pallas_call(
        body, name=name, out_shape=out_shapes, in_specs=[ANY] * n, out_specs=[ANY] * n, scratch_shapes=scratch,
        compiler_params=pltpu.CompilerParams(has_side_effects=True),
    )(*arrays)


def _all_gather(name, xs):
    n = len(xs)

    def split(shape):
        ax = next(i for i, d in enumerate(shape) if d % 2 == 0)
        return ax, shape[ax] // 2

    def body(*refs):
        x_refs, out_refs = refs[:n], refs[n:2 * n]
        send_sems, recv_sems = refs[2 * n:]
        mx, my, mc = _me()
        me_s = 2 * mx + my
        across_x, across_y, diag = (1 - mx, my), (mx, 1 - my), (1 - mx, 1 - my)
        sid = lambda chip: 2 * chip[0] + chip[1]

        def part(ref, a, which):
            if which is None:
                return ref
            ax, size = split(xs[a].shape[1:])
            return ref.at[(slice(None),) * ax + (pl.ds(which * size, size),)]

        def copy(a, k, shard, half, which, to, src=None):
            dst = part(out_refs[a].at[shard, half], a, which)
            return pltpu.make_async_remote_copy(src_ref=dst if src is None else part(src, a, which), dst_ref=dst,
                                                send_sem=send_sems.at[a, k], recv_sem=recv_sems.at[a, k], device_id=to, device_id_type=MESH)

        sibling, here = (mx, my, 1 - mc), (mx, my, mc)
        sent = []
        for a in range(n):
            sent += [copy(a, 0, me_s, mc, None, (*across_x, mc), src=x_refs[a].at[mc]),
                     copy(a, 1, me_s, mc, None, (*across_y, mc), src=x_refs[a].at[mc])]
        for cp in sent:
            cp.start()
        landing = [(0, across_x, None, across_y), (1, across_y, None, across_x), (2, diag, 0, None), (3, diag, 1, None)]
        for k, origin, which, relay_to in landing:
            for a in range(n):
                copy(a, k, sid(origin), mc, which, here).wait_recv()
                if relay_to is not None:
                    sent.append(copy(a, 2 + k, sid(origin), mc, k, (*relay_to, mc)))
                    sent[-1].start()
                sent.append(copy(a, 4 + k, sid(origin), mc, which, sibling))
                sent[-1].start()
        for k, origin, which, _ in landing:
            for a in range(n):
                copy(a, 4 + k, sid(origin), 1 - mc, which, here).wait_recv()
        for cp in sent:
            cp.wait_send()

    return _comm_call(name, body, xs, [jax.ShapeDtypeStruct((N_SHARD,) + x.shape, x.dtype) for x in xs], 8)


def _pair_exchange(name, items):
    n = len(items)

    def body(*refs):
        src_refs, dst_refs = refs[:n], refs[n:2 * n]
        send_sems, recv_sems = refs[2 * n:]
        mx, my, mc = _me()
        cps = [pltpu.make_async_remote_copy(src_ref=src_refs[a].at[s, 1 - mc], dst_ref=dst_refs[a].at[s], send_sem=send_sems.at[a, s],
                                            recv_sem=recv_sems.at[a, s], device_id=(mx, my, 1 - mc), device_id_type=MESH)
               for a in range(n) for s in range(N_SHARD)]
        for cp in cps:
            cp.start()
        for cp in cps:
            cp.wait()

    return _comm_call(name, body, items, [jax.ShapeDtypeStruct((N_SHARD,) + t.shape[2:], t.dtype) for t in items], N_SHARD)


def _chip_exchange(name, parts):
    n = len(parts)

    def body(*refs):
        p_refs, dst_refs = refs[:n], refs[n:2 * n]
        send_sems, recv_sems = refs[2 * n:]
        mx, my, mc = _me()
        chips = [(1 - mx, my), (mx, 1 - my), (1 - mx, 1 - my)]
        cps = [pltpu.make_async_remote_copy(src_ref=p_refs[a].at[2 * cx + cy], dst_ref=dst_refs[a].at[j], send_sem=send_sems.at[a, j],
                                            recv_sem=recv_sems.at[a, j], device_id=(cx, cy, mc), device_id_type=MESH)
               for a in range(n) for j, (cx, cy) in enumerate(chips)]
        for cp in cps:
            cp.start()
        for cp in cps:
            cp.wait()

    return _comm_call(name, body, parts, [jax.ShapeDtypeStruct((3,) + t.shape[1:], t.dtype) for t in parts], 3)


def _pair_swap(name, reds):
    n = len(reds)

    def body(*refs):
        r_refs, out_refs = refs[:n], refs[n:2 * n]
        send_sems, recv_sems = refs[2 * n:]
        mx, my, mc = _me()
        cps = [pltpu.make_async_remote_copy(src_ref=r_refs[a], dst_ref=out_refs[a], send_sem=send_sems.at[a, 0],
                                            recv_sem=recv_sems.at[a, 0], device_id=(mx, my, 1 - mc), device_id_type=MESH)
               for a in range(n)]
        for cp in cps:
            cp.start()
        for cp in cps:
            cp.wait()

    return _comm_call(name, body, reds, [jax.ShapeDtypeStruct(t.shape, t.dtype) for t in reds], 1)


EW_BLOCK_BYTES = 1 << 20


def _ew_call(name, fn, ins, n_out, out_dtype=f32):
    shape = ins[0].shape
    last = shape[-1]
    flat = [a.reshape(-1, last) for a in ins]
    R = flat[0].shape[0]
    cands = tuple(c for c in (4096, 2048, 1024, 512, 256, 128, 64, 32, 16, 8) if c * last * 4 <= EW_BLOCK_BYTES)
    tr = _pick(R, cands)
    res = _tile_call(name, lambda i, *a: fn(*a), R // tr, [(a, (tr, last), lambda i: (i, 0)) for a in flat],
                     [((R, last), out_dtype, (tr, last), lambda i: (i, 0))] * n_out)
    return [r.reshape(shape) for r in res]


def _adamw(g, w, m, v):
    m = B1 * m + (1.0 - B1) * g
    v = B2 * v + (1.0 - B2) * jnp.square(g)
    m_hat = m / (1.0 - B1 ** STEP)
    v_hat = v / (1.0 - B2 ** STEP)
    return -LR * (m_hat / (jnp.sqrt(v_hat) + EPS) + WD * w), m, v


def _adamw_call(name, mine, theirs, w, m, v, mc1):
    shape, last = w.shape, w.shape[-1]
    g2 = [t.reshape(-1, last) for t in (mine, theirs)]
    w3 = [t.reshape(2, -1, last) for t in (w, m, v)]
    R = g2[0].shape[0]
    tr = _pick(R, tuple(c for c in (4096, 2048, 1024, 512, 256, 128, 64, 32, 16, 8) if c * last * 4 <= EW_BLOCK_BYTES))

    def body(mc_ref, mine_ref, theirs_ref, w_ref, m_ref, v_ref, g_out, d_out, m_out, v_out):
        g = jnp.where(pl.program_id(0) == mc_ref[0], mine_ref[...], theirs_ref[...])
        d, nm, nv = _adamw(g, w_ref[0], m_ref[0], v_ref[0])
        g_out[0], d_out[0], m_out[0], v_out[0] = g, d, nm, nv

    half = pl.BlockSpec((tr, last), lambda h, i: (i, 0))
    full = pl.BlockSpec((1, tr, last), lambda h, i: (h, i, 0))
    res = pl.pallas_call(
        body, name=name, grid=(2, R // tr),
        in_specs=[pl.BlockSpec(memory_space=pltpu.SMEM), half, half, full, full, full], out_specs=[full] * 4,
        out_shape=[jax.ShapeDtypeStruct((2, R, last), f32)] * 4,
        compiler_params=pltpu.CompilerParams(dimension_semantics=("arbitrary", "arbitrary"), vmem_limit_bytes=VMEM_LIMIT),
    )(mc1, *g2, *w3)
    return tuple(r.reshape(shape) for r in res)


def kernel(x, mem, mem_ln_g, mem_ln_b, ln_g, ln_b, ffn1_w_gu, ffn1_w_down, w_in, dn_conv_w, dn_a_log, dn_dt_bias, dn_norm_w, swa_sinks, w_mem_kv, w_branch, w_out, ffn2_w_gu, ffn2_w_down, loss_target, m_mem_ln_g, m_mem_ln_b, m_ln_g, m_ln_b, m_ffn1_w_gu, m_ffn1_w_down, m_w_in, m_dn_conv_w, m_dn_a_log, m_dn_dt_bias, m_dn_norm_w, m_swa_sinks, m_w_mem_kv, m_w_branch, m_w_out, m_ffn2_w_gu, m_ffn2_w_down, v_mem_ln_g, v_mem_ln_b, v_ln_g, v_ln_b, v_ffn1_w_gu, v_ffn1_w_down, v_w_in, v_dn_conv_w, v_dn_a_log, v_dn_dt_bias, v_dn_norm_w, v_swa_sinks, v_w_mem_kv, v_w_branch, v_w_out, v_ffn2_w_gu, v_ffn2_w_down):
    args = dict(locals())
    Ws = {n: args[n] for n in WEIGHTS}
    Ms = {n: args["m_" + n] for n in WEIGHTS}
    Vs = {n: args["v_" + n] for n in WEIGHTS}
    mc = lax.axis_index("c")
    my_s = 2 * lax.axis_index("x") + lax.axis_index("y")
    small = [n for n in WEIGHTS if n not in MATRICES]

    ag_table, ag_rows = _layout(SMALL_SHARDED, {n: Ws[n].size for n in SMALL_SHARDED}, 16)
    ag_small = _pack(ag_table, ag_rows, {n: Ws[n].reshape(-1) for n in SMALL_SHARDED}).reshape(2, ag_rows // 2, LANES)
    local = [Ws[n].astype(bf16) for n in MATRICES] + [ag_small]
    gathered = _all_gather("all_gather_w", local)
    gathered = [jnp.stack([jnp.where(my_s == s, loc, g[s]) for s in range(N_SHARD)], axis=0) for loc, g in zip(local, gathered)]
    Wf = {n: _join_shards(g, SHARD_AXIS[n]) for n, g in zip(MATRICES, gathered) if not n.endswith('w_gu') and n != 'w_in'}
    for n, g in zip(MATRICES, gathered):
        if n.endswith('w_gu'):
            assert g.shape[-1] == FF_BLK
            Wf[n + '_p'] = jnp.concatenate([g[0], g[2], g[1], g[3]], axis=2)
        if n == 'w_in':
            Wf['w_in_sh'] = g
    g_small = gathered[-1].reshape(N_SHARD, ag_rows, LANES)
    for n in SMALL_SHARDED:
        Wf[n] = _join_shards(_unpack(ag_table, g_small, n).reshape((N_SHARD,) + Ws[n].shape), SHARD_AXIS[n])
    for n in WEIGHTS:
        if SHARD_AXIS[n] is None:
            Wf[n] = Ws[n]

    loss_row, dx, G = _local_step(x[0], mem[0], loss_target[0], Wf)

    table, rows = _layout(small + ['loss'], {**{n: Ws[n].size for n in small}, 'loss': 1}, 16)
    gflat = {n: (jnp.broadcast_to(G[n].reshape(1, -1), (N_SHARD, G[n].size)) if SHARD_AXIS[n] is None
                 else _split_shards(G[n], SHARD_AXIS[n]).reshape(N_SHARD, -1)) for n in small}
    gflat['loss'] = jnp.broadcast_to(loss_row[:, :1], (N_SHARD, 1))
    n_in = Ws['w_in'].shape[-1]
    items =[_split_shards(G[n], SHARD_AXIS[n]) if n != 'w_in' else
             jnp.stack([jnp.stack([_unpad_dw_in(G['w_in_p'][l], s * n_in, (s + 1) * n_in) for l in range(DEPTH)]) for s in range(N_SHARD)])
             for n in MATRICES] + [_pack(table, rows, gflat).reshape(N_SHARD, 2, rows // 2, LANES)]
    tags = MATRICES + ['small']
    got = _pair_exchange("rs_pair", items)
    keep = [lax.dynamic_index_in_dim(a, mc, axis=1, keepdims=False) for a in items]
    wire = [bf16] * len(MATRICES) + [f32]
    part = [_ew_call(f"rs_add_pair_{t}", lambda a, b: a + b, [k, g], 1, out_dtype=dt)[0] for t, k, g, dt in zip(tags, keep, got, wire)]
    others = _chip_exchange("rs_chips", part)
    own = lambda a: lax.dynamic_index_in_dim(a, my_s, axis=0, keepdims=False)
    red = [_ew_call(f"rs_add_chips_{t}", lambda k, g, fx, fy, fxy: ((k + g) + fy) + (fx + fxy), [own(k), own(g), o[0], o[1], o[2]], 1)[0]
           for t, k, g, o in zip(tags, keep, got, others)]
    theirs = _pair_swap("rs_swap", red)

    mc1 = mc.astype(i32).reshape(1)
    outs = {}
    for n, a, b in zip(MATRICES, red, theirs):
        outs[n] = _adamw_call(f"adamw_{n}", a, b, Ws[n], Ms[n], Vs[n], mc1)
    fill = {'loss': jnp.zeros((1,), f32)}
    packs = [_pack(table, rows, {**{n: src[n].reshape(-1) for n in small}, **fill}).reshape(2, rows // 2, LANES) for src in (Ws, Ms, Vs)]
    small_out = [p.reshape(rows, LANES) for p in _adamw_call("adamw_small", red[-1], theirs[-1], *packs, mc1)]
    for n in small:
        outs[n] = tuple(_unpack(table, p, n).reshape(Ws[n].shape) for p in small_out)
    loss = _unpack(table, small_out[0], 'loss').reshape(())
    return (loss, dx[None], *[outs[n][k] for k in range(4) for n in WEIGHTS])
```

```python
import functools

import jax
import jax.numpy as jnp
from jax import lax
from jax.experimental import pallas as pl
from jax.experimental.pallas import tpu as pltpu

f32, bf16, i32 = jnp.float32, jnp.bfloat16, jnp.int32
MESH = pl.DeviceIdType.MESH

D_MODEL = 1024
DEPTH = 2
DN_HEADS, DN_DK, DN_CONV, DN_CHUNK = 4, 128, 4, 64
SWA_HEADS, SWA_KV_HEADS, SWA_DH, WINDOW = 8, 2, 64, 128
XA_HEADS, XA_DH = 4, 128
D_FF = 2816
N_BRANCH, BRANCH_W = 3, 512
ALPHA = (2 * DEPTH) ** 0.25
LN_EPS, RMS_EPS, NEG_INF = 1e-5, 1e-6, -1e30
D_IN = 6408
D_INP = 6656
LR, B1, B2, EPS, WD, STEP = 0.001, 0.9, 0.999, 1e-08, 0.01, 10

LANES = 128
VMEM_LIMIT = 56 << 20
N_SHARD = 4
SCAN_CHUNKS = 16
HALO = 16
MM_TILES = {
    'ffn1_gu_dx': (512, 1024, 5632), 'ffn2_gu_dx': (512, 1024, 5632), 'w_in_dx': (512, 1024, 6656), 'w_in': (2048, 512, 1024),
    'w_in_dw': (1024, 512, 4096), 'ffn1_down_dw': (1408, 512, 4096), 'ffn2_down_dw': (1408, 512, 4096),
    'ffn1_down_dx': (1024,), 'ffn2_down_dx': (1024,), 'ffn1_down': (1024,), 'ffn2_down': (1024,), 'w_out': (1024,),
    'ffn1_gu': (1024,), 'ffn2_gu': (1024,),
    'w_out_dx': (512, 1024, 1024), 'l1_w_in': (2048, 1664, 1024),
}


def _tiles(name, default=None):
    return MM_TILES.get(name, MM_TILES.get(name.split('_', 1)[-1], default))

WEIGHTS = ['mem_ln_g', 'mem_ln_b', 'ln_g', 'ln_b', 'ffn1_w_gu', 'ffn1_w_down', 'w_in', 'dn_conv_w', 'dn_a_log',
           'dn_dt_bias', 'dn_norm_w', 'swa_sinks', 'w_mem_kv', 'w_branch', 'w_out', 'ffn2_w_gu', 'ffn2_w_down']
SHARD_AXIS = {'mem_ln_g': None, 'mem_ln_b': None, 'ln_g': 2, 'ln_b': 2, 'ffn1_w_gu': 2, 'ffn1_w_down': 1, 'w_in': 2,
              'dn_conv_w': 2, 'dn_a_log': None, 'dn_dt_bias': None, 'dn_norm_w': None, 'swa_sinks': None,
              'w_mem_kv': 1, 'w_branch': 3, 'w_out': 1, 'ffn2_w_gu': 2, 'ffn2_w_down': 1}
MATRICES = ['ffn1_w_gu', 'ffn1_w_down', 'w_in', 'w_mem_kv', 'w_branch', 'w_out', 'ffn2_w_gu', 'ffn2_w_down']
SMALL_SHARDED = ['ln_g', 'ln_b', 'dn_conv_w']


def _dg(a, b, mode, hi):
    nb = a.ndim - 2
    bd = tuple(range(nb))
    ca = nb if mode == 'tn' else nb + 1
    cb = nb + 1 if mode == 'nt' else nb
    dims = (((ca,), (cb,)), (bd, bd))
    dot = lambda x, y: lax.dot_general(x, y, dims, preferred_element_type=f32)
    a_hi, b_hi = a.astype(bf16), b.astype(bf16)
    if not hi:
        return dot(a_hi, b_hi)
    a_lo = (a.astype(f32) - a_hi.astype(f32)).astype(bf16)
    b_lo = (b.astype(f32) - b_hi.astype(f32)).astype(bf16)
    return dot(a_hi, b_hi) + (dot(a_hi, b_lo) + dot(a_lo, b_hi))


@functools.partial(jax.custom_vjp, nondiff_argnums=(2, 3))
def _dot(a, b, mode, hi):
    return _dg(a, b, mode, hi)


def _dot_fwd(a, b, mode, hi):
    return _dg(a, b, mode, hi), (a, b)


def _dot_bwd(mode, hi, res, g):
    a, b = res
    if mode == 'nn':
        da, db = _dg(g, b, 'nt', hi), _dg(a, g, 'tn', hi)
    elif mode == 'nt':
        da, db = _dg(g, b, 'nn', hi), _dg(g, a, 'tn', hi)
    else:
        da, db = _dg(b, g, 'nt', hi), _dg(a, g, 'nn', hi)
    return da.astype(a.dtype), db.astype(b.dtype)


_dot.defvjp(_dot_fwd, _dot_bwd)


def bdot(a, b, mode):
    return _dot(a, b, mode, False)


def hdot(a, b, mode):
    return _dot(a, b, mode, True)


def _shift_rows_impl(x, tail, s):
    r = pltpu.roll(x, s, 0)
    rows = lax.broadcasted_iota(i32, tail.shape, 0)
    top = jnp.where(rows >= s, r[:HALO], pltpu.roll(tail, s, 0))
    return jnp.concatenate([top, r[HALO:]], axis=0)


@functools.partial(jax.custom_vjp, nondiff_argnums=(2,))
def _shift_rows(x, tail, s):
    return _shift_rows_impl(x, tail, s)


def _shift_rows_fwd(x, tail, s):
    return _shift_rows_impl(x, tail, s), None


def _shift_rows_bwd(s, _, g):
    n = g.shape[0]
    r = pltpu.roll(g, n - s, 0)
    rows = lax.broadcasted_iota(i32, (HALO, g.shape[1]), 0)
    last = r[n - HALO:]
    keep = rows < HALO - s
    dx = jnp.concatenate([r[:n - HALO], jnp.where(keep, last, 0.0)], axis=0)
    return dx, jnp.where(keep, 0.0, pltpu.roll(g[:HALO], HALO - s, 0))


_shift_rows.defvjp(_shift_rows_fwd, _shift_rows_bwd)


def _lane_pick(x, lane):
    idx = lax.broadcasted_iota(i32, x.shape, x.ndim - 1)
    return jnp.sum(jnp.where(idx == lane, x, 0.0), axis=-1, keepdims=True)


def _silu(x):
    return x * jax.nn.sigmoid(x)


def _tri_inv(a):
    C = a.shape[-1]
    eye = (lax.broadcasted_iota(i32, a.shape, 1) == lax.broadcasted_iota(i32, a.shape, 2)).astype(f32)
    p = -a
    x = eye + p
    for _ in range((C - 1).bit_length() - 1):
        p = _dg(p, p, 'nn', True)
        x = x + _dg(x, p, 'nn', True)
    return x


@jax.custom_vjp
def _tri_solve(a, rhs):
    return _dg(_tri_inv(a), rhs, 'nn', True)


def _tri_solve_fwd(a, rhs):
    tinv = _tri_inv(a)
    sol = _dg(tinv, rhs, 'nn', True)
    return sol, (tinv, sol)


def _tri_solve_bwd(res, g):
    tinv, sol = res
    d_rhs = _dg(tinv, g, 'tn', True)
    return -_dg(d_rhs, sol, 'nt', True), d_rhs


_tri_solve.defvjp(_tri_solve_fwd, _tri_solve_bwd)


@functools.partial(jax.custom_vjp, nondiff_argnums=(1,))
def _lane_head(x, n):
    return x[:, :, :n]


def _lane_head_fwd(x, n):
    return x[:, :, :n], None


def _lane_head_bwd(n, _, g):
    s = jnp.sum(g, axis=-1, keepdims=True) * (1.0 / LANES)
    return (jnp.broadcast_to(s, g.shape[:-1] + (LANES,)),)


_lane_head.defvjp(_lane_head_fwd, _lane_head_bwd)


@functools.partial(jax.custom_vjp, nondiff_argnums=(1,))
def _last_row(x, c):
    return x[:, c - 1:, :]


def _last_row_fwd(x, c):
    return x[:, c - 1:, :], None


def _last_row_bwd(c, _, g):
    shape = (g.shape[0], c, g.shape[2])
    rows = lax.broadcasted_iota(i32, shape, 1)
    return (jnp.where(rows == c - 1, jnp.broadcast_to(g, shape), 0.0),)


_last_row.defvjp(_last_row_fwd, _last_row_bwd)


def _full(a):
    nd = a.ndim
    return (a, tuple(a.shape), lambda i, _nd=nd: (0,) * _nd)


def _rows(a, tm, col=0, width=None):
    width = a.shape[1] if width is None else width
    return (a, (tm, width), lambda i, _c=col: (i, _c))


def _tile_call(name, fn, n, ins, outs, accs=()):
    n_in, n_out, n_acc = len(ins), len(outs), len(accs)

    def body(*refs):
        i = pl.program_id(0)
        res = fn(i, *[r[...].astype(f32) for r in refs[:n_in]])
        if not isinstance(res, (tuple, list)):
            res = (res,)
        assert len(res) == n_out + n_acc, (name, len(res), n_out, n_acc)
        for r, v in zip(refs[n_in:n_in + n_out], res[:n_out]):
            r[...] = v.astype(r.dtype)
        if n_acc:
            acc_refs = refs[n_in + n_out:]

            @pl.when(i == 0)
            def _():
                for r in acc_refs:
                    r[...] = jnp.zeros(r.shape, r.dtype)

            for r, v in zip(acc_refs, res[n_out:]):
                r[...] += v.astype(r.dtype)

    out_shape = [jax.ShapeDtypeStruct(s, d) for s, d, _, _ in outs] + [jax.ShapeDtypeStruct(s, d) for s, d in accs]
    out_specs = [pl.BlockSpec(b, m) for _, _, b, m in outs]
    out_specs += [pl.BlockSpec(tuple(s), lambda i, _nd=len(s): (0,) * _nd) for s, _ in accs]
    res = pl.pallas_call(
        body, name=name, grid=(n,),
        in_specs=[pl.BlockSpec(b, m) for _, b, m in ins],
        out_specs=out_specs, out_shape=out_shape,
        compiler_params=pltpu.CompilerParams(dimension_semantics=("arbitrary",), vmem_limit_bytes=VMEM_LIMIT),
    )(*[a for a, _, _ in ins])
    return res


def _pick(n, cands):
    for c in cands:
        if n % c == 0:
            return c
    return n


def _mm(name, a, b, mode, out_dtype=f32, add=None, out_cols=None):
    if mode == 'tn':
        K, M = a.shape
    else:
        M, K = a.shape
    N = b.shape[0] if mode == 'nt' else b.shape[1]
    tm = _pick(M, (1024, 1408, 512, 256, 128))
    tn = _pick(N, (512, 1408, 256, 128)) if out_cols is None else out_cols[0]
    tuned = _tiles(name)
    if tuned is not None and (M % tuned[0] or N % tuned[1] or K % tuned[2]):
        tuned = None
    if tuned is not None:
        tm, tn = tuned[:2]
    col = (lambda j: j) if out_cols is None else out_cols[1]
    tk = K if K <= 3328 else _pick(K, (3328, 2816, 2048, 1024, 512, 256, 128))
    if tuned is not None:
        tk = tuned[2]
    nk = K // tk
    ca = 0 if mode == 'tn' else 1
    cb = 1 if mode == 'nt' else 0
    dims = (((ca,), (cb,)), ((), ()))

    def body(*refs):
        a_ref, b_ref = refs[0], refs[1]
        add_ref = refs[2] if add is not None else None
        part = lax.dot_general(a_ref[...].astype(bf16), b_ref[...].astype(bf16), dims, preferred_element_type=f32)

        def finish(r, o_ref):
            if add_ref is not None:
                r = r + add_ref[...].astype(f32)
            o_ref[...] = r.astype(o_ref.dtype)

        if nk == 1:
            finish(part, refs[-1])
            return
        o_ref, acc_ref = refs[-2], refs[-1]
        k = pl.program_id(2)

        @pl.when(k == 0)
        def _():
            acc_ref[...] = part

        @pl.when(k > 0)
        def _():
            acc_ref[...] += part

        @pl.when(k == nk - 1)
        def _():
            finish(acc_ref[...], o_ref)

    a_spec = pl.BlockSpec((tk, tm), lambda i, j, k: (k, i)) if mode == 'tn' else pl.BlockSpec((tm, tk), lambda i, j, k: (i, k))
    b_spec = pl.BlockSpec((tn, tk), lambda i, j, k: (j, k)) if mode == 'nt' else pl.BlockSpec((tk, tn), lambda i, j, k: (k, j))
    in_specs, args = [a_spec, b_spec], [a, b]
    if add is not None:
        in_specs.append(pl.BlockSpec((tm, tn), lambda i, j, k: (i, j)))
        args.append(add)
    return pl.pallas_call(
        body, name=name, grid=(M // tm, N // tn, nk), in_specs=in_specs,
        out_specs=pl.BlockSpec((tm, tn), lambda i, j, k: (i, col(j))),
        out_shape=jax.ShapeDtypeStruct((M, N), out_dtype),
        scratch_shapes=[pltpu.VMEM((tm, tn), f32)] if nk > 1 else [],
        compiler_params=pltpu.CompilerParams(dimension_semantics=("parallel", "parallel", "arbitrary"), vmem_limit_bytes=VMEM_LIMIT),
    )(*args)


def _res_ln(x, f, g, b, s):
    r = ALPHA * x + s * f
    mu = jnp.mean(r, axis=-1, keepdims=True)
    rc = r - mu
    var = jnp.mean(rc * rc, axis=-1, keepdims=True)
    return rc * lax.rsqrt(var + LN_EPS) * g + b


def _ln(x, g, b):
    mu = jnp.mean(x, axis=-1, keepdims=True)
    xc = x - mu
    var = jnp.mean(xc * xc, axis=-1, keepdims=True)
    return xc * lax.rsqrt(var + LN_EPS) * g + b


def _dn_pre(first, xc, xp, ba, cw, hp):
    xp = jnp.where(first, 0.0, xp)
    y = cw[DN_CONV - 1:DN_CONV, :] * xc
    for j in range(DN_CONV - 1):
        y = y + cw[j:j + 1, :] * _shift_rows(xc, xp, DN_CONV - 1 - j)
    c = _silu(y)
    qs, ks, vs, gs, bs = [], [], [], [], []
    nqk = DN_HEADS * DN_DK
    for h in range(DN_HEADS):
        q = c[:, h * DN_DK:(h + 1) * DN_DK]
        k = c[:, nqk + h * DN_DK:nqk + (h + 1) * DN_DK]
        v = c[:, 2 * nqk + h * DN_DK:2 * nqk + (h + 1) * DN_DK]
        qs.append(q * lax.rsqrt(jnp.sum(q * q, axis=-1, keepdims=True) + RMS_EPS))
        ks.append(k * lax.rsqrt(jnp.sum(k * k, axis=-1, keepdims=True) + RMS_EPS))
        vs.append(v)
        beta = jax.nn.sigmoid(_lane_pick(ba, h))
        a_log = _lane_pick(hp[0:1, :], h)
        dt = _lane_pick(hp[1:2, :], h)
        g = -jnp.exp(a_log) * jax.nn.softplus(_lane_pick(ba, DN_HEADS + h) + dt)
        gs.append(jnp.broadcast_to(g, q.shape))
        bs.append(jnp.broadcast_to(beta, q.shape))
    return tuple(jnp.stack(t, axis=0) for t in (qs, ks, vs, gs, bs))


def _wy(q, k, v, gb, bb):
    B, C, _ = q.shape
    ri = lax.broadcasted_iota(i32, (B, C, C), 1)
    ci = lax.broadcasted_iota(i32, (B, C, C), 2)
    tril, strict = ri >= ci, ri > ci
    gc = hdot(tril.astype(f32), gb, 'nn')
    gl = jnp.broadcast_to(_last_row(gc, C), gc.shape)
    col = _lane_head(gc, C)
    decay = jnp.exp(jnp.where(tril, col - jnp.swapaxes(col, 1, 2), NEG_INF))
    qs = q * (DN_DK ** -0.5)
    kb = k * bb
    a = jnp.where(strict, bdot(kb, k, 'nt') * decay, 0.0)
    sol = _tri_solve(a, jnp.concatenate([v * bb, kb * jnp.exp(gc)], axis=-1))
    qk = jnp.where(tril, bdot(qs, k, 'nt') * decay, 0.0)
    kt = k * jnp.exp(gl - gc)
    qd = qs * jnp.exp(gc)
    return qd, qk, sol[..., :DN_DK], sol[..., DN_DK:], kt, jnp.exp(gl)


def _scan_step(S, qd, qk, u, w, kt, egl):
    vn = u - bdot(w, S, 'nn')
    o = bdot(qd, S, 'nn') + bdot(qk, vn, 'nn')
    e2 = jnp.concatenate([egl] * (DN_DK // DN_CHUNK), axis=1)
    return o, S * e2 + bdot(kt, vn, 'tn')


def _dn_post(o, z, nw):
    outs = []
    for h in range(DN_HEADS):
        oh = o[h]
        oh = oh * lax.rsqrt(jnp.mean(oh * oh, axis=-1, keepdims=True) + RMS_EPS) * nw
        outs.append(oh * _silu(z[:, h * DN_DK:(h + 1) * DN_DK]))
    return jnp.concatenate(outs, axis=1)


def _swa(first, q, kc, kp, vc, vp, snk):
    W = q.shape[1]
    G = SWA_HEADS // SWA_KV_HEADS
    r = lax.broadcasted_iota(i32, (G, W, 2 * W), 1)
    c = lax.broadcasted_iota(i32, (G, W, 2 * W), 2)
    mask = (c > W + r - WINDOW) & (c <= W + r) & jnp.logical_or(c >= W, jnp.logical_not(first))
    sink_all = _lane_pick(snk, 0)
    outs = []
    for j in range(SWA_KV_HEADS):
        qj = q[j * G:(j + 1) * G]
        kk = jnp.broadcast_to(jnp.concatenate([kp[j], kc[j]], axis=0)[None], (G, 2 * W, SWA_DH))
        vv = jnp.broadcast_to(jnp.concatenate([vp[j], vc[j]], axis=0)[None], (G, 2 * W, SWA_DH))
        s = jnp.where(mask, bdot(qj, kk, 'nt') * (SWA_DH ** -0.5), NEG_INF)
        sink = sink_all[j * G:(j + 1) * G][:, :, None]
        m = jnp.maximum(jnp.max(s, axis=-1, keepdims=True), sink)
        p = jnp.exp(s - m)
        p = p / (jnp.sum(p, axis=-1, keepdims=True) + jnp.exp(sink - m))
        outs.append(bdot(p, vv, 'nn'))
    return jnp.concatenate(outs, axis=0)


def _xa(q, kv):
    outs = []
    nk = XA_HEADS * XA_DH
    for h in range(XA_HEADS):
        qh = q[:, h * XA_DH:(h + 1) * XA_DH]
        kh = kv[:, h * XA_DH:(h + 1) * XA_DH]
        vh = kv[:, nk + h * XA_DH:nk + (h + 1) * XA_DH]
        s = bdot(qh, kh, 'nt') * (XA_DH ** -0.5)
        m = jnp.max(s, axis=-1, keepdims=True)
        p = jnp.exp(s - m)
        p = p / jnp.sum(p, axis=-1, keepdims=True)
        outs.append(bdot(p, vh, 'nn'))
    return jnp.concatenate(outs, axis=1)


def _merge(o_dn, o_sw, o_xa, gates, wb):
    acc = None
    for n, o in enumerate((o_dn, o_sw, o_xa)):
        t = jax.nn.sigmoid(gates[:, n * D_MODEL:(n + 1) * D_MODEL]) * bdot(o, wb[n], 'nn')
        acc = t if acc is None else acc + t
    return acc


def _row_tile(T, want):
    return _pick(T, tuple(c for c in (1024, 512, 256, 128, 64) if c <= want))


def _res_ln_bwd(name, x, f, g, b, s, dh):
    T = x.shape[0]
    tm = _row_tile(T, 1024)

    def fn(i, x, f, g, b, dh):
        _, vjp = jax.vjp(lambda x, f, g, b: _res_ln(x, f, g, b, s), x, f, g, b)
        return vjp(dh)

    return _tile_call(name, fn, T // tm, [_rows(x, tm), _rows(f, tm), _full(g), _full(b), _rows(dh, tm)],
                      [((T, D_MODEL), f32, (tm, D_MODEL), lambda i: (i, 0)), ((T, D_MODEL), bf16, (tm, D_MODEL), lambda i: (i, 0))],
                      [((1, D_MODEL), f32), ((1, D_MODEL), f32)])


FF_BLK = D_FF // 2


def _pair_gate_up(w):
    d = w.shape[0]
    return jnp.stack([w[:, :D_FF].reshape(d, -1, FF_BLK), w[:, D_FF:].reshape(d, -1, FF_BLK)], axis=2).reshape(d, 2 * D_FF)


def _swiglu_blk(u):
    return _silu(u[:, :FF_BLK]) * u[:, FF_BLK:]


def _mm_spec():
    return dict(compiler_params=pltpu.CompilerParams(dimension_semantics=("parallel", "arbitrary"), vmem_limit_bytes=VMEM_LIMIT))


def _gu_act(name, xb, w_pair):
    T, D = xb.shape
    tm = _pick(T, (_tiles(name, (512,))[0], 512, 256, 128))

    def body(x_ref, w_ref, u_ref, a_ref):
        r = lax.dot_general(x_ref[...], w_ref[...], (((1,), (0,)), ((), ())), preferred_element_type=f32)
        u_ref[...] = r.astype(bf16)
        a_ref[...] = _swiglu_blk(r).astype(bf16)

    return pl.pallas_call(
        body, name=name, grid=(T // tm, D_FF // FF_BLK),
        in_specs=[pl.BlockSpec((tm, D), lambda i, j: (i, 0)), pl.BlockSpec((D, 2 * FF_BLK), lambda i, j: (0, j))],
        out_specs=[pl.BlockSpec((tm, 2 * FF_BLK), lambda i, j: (i, j)), pl.BlockSpec((tm, FF_BLK), lambda i, j: (i, j))],
        out_shape=[jax.ShapeDtypeStruct((T, 2 * D_FF), bf16), jax.ShapeDtypeStruct((T, D_FF), bf16)],
        **_mm_spec())(xb, w_pair)


def _down_dx_act_bwd(name, df, w_down_t, u):
    T, D = df.shape
    tm = _pick(T, (_tiles(name, (256,))[0], 256, 128))

    def body(df_ref, w_ref, u_ref, du_ref):
        da = lax.dot_general(df_ref[...], w_ref[...], (((1,), (0,)), ((), ())), preferred_element_type=f32)
        _, vjp = jax.vjp(_swiglu_blk, u_ref[...].astype(f32))
        du_ref[...] = vjp(da)[0].astype(bf16)

    return pl.pallas_call(
        body, name=name, grid=(T // tm, D_FF // FF_BLK),
        in_specs=[pl.BlockSpec((tm, D), lambda i, j: (i, 0)), pl.BlockSpec((D, FF_BLK), lambda i, j: (0, j)),
                  pl.BlockSpec((tm, 2 * FF_BLK), lambda i, j: (i, j))],
        out_specs=pl.BlockSpec((tm, 2 * FF_BLK), lambda i, j: (i, j)),
        out_shape=jax.ShapeDtypeStruct((T, 2 * D_FF), bf16),
        **_mm_spec())(df, w_down_t, u)


def _mm_res_ln(name, a, w, x, g, b, s):
    T, K = a.shape
    tm = _pick(T, (_tiles(name, (512,))[0], 512, 256, 128))

    def body(a_ref, w_ref, x_ref, g_ref, b_ref, f_ref, h_ref, hb_ref):
        f = lax.dot_general(a_ref[...], w_ref[...], (((1,), (0,)), ((), ())), preferred_element_type=f32)
        h = _res_ln(x_ref[...], f, g_ref[...], b_ref[...], s)
        f_ref[...] = f.astype(bf16)
        h_ref[...] = h
        hb_ref[...] = h.astype(bf16)

    row = pl.BlockSpec((tm, D_MODEL), lambda i: (i, 0))
    vec = pl.BlockSpec((1, D_MODEL), lambda i: (0, 0))
    return pl.pallas_call(
        body, name=name, grid=(T // tm,),
        in_specs=[pl.BlockSpec((tm, K), lambda i: (i, 0)), pl.BlockSpec((K, D_MODEL), lambda i: (0, 0)), row, vec, vec],
        out_specs=[row, row, row],
        out_shape=[jax.ShapeDtypeStruct((T, D_MODEL), bf16), jax.ShapeDtypeStruct((T, D_MODEL), f32), jax.ShapeDtypeStruct((T, D_MODEL), bf16)],
        compiler_params=pltpu.CompilerParams(dimension_semantics=("parallel",), vmem_limit_bytes=VMEM_LIMIT))(a, w, x, g, b)


def _hm_spec(tm, w=DN_DK):
    return ((DN_HEADS, tm, w), lambda i: (0, i, 0))


def _prev(i):
    return jnp.maximum(i - 1, 0)


def _dn_front(first, xc, xp, ba, cw, hp):
    tm = xc.shape[0]
    nc = tm // DN_CHUNK
    outs = _wy(*[t.reshape(DN_HEADS * nc, DN_CHUNK, DN_DK) for t in _dn_pre(first, xc, xp, ba, cw, hp)])
    return tuple(o.reshape(DN_HEADS, tm, o.shape[-1]) for o in outs)


def _dn_front_ins(p, pba, cw, hp, tm):
    W3 = 3 * DN_HEADS * DN_DK
    return [(p, (tm, W3), lambda i: (i, 0)), (p, (HALO, W3), lambda i: (jnp.maximum(i * (tm // HALO) - 1, 0), 0)),
            (pba, (tm, LANES), lambda i: (i, 0)), _full(cw), _full(hp)]


def _dn_front_fwd(name, p, pba, cw, hp):
    T = p.shape[0]
    tm = _row_tile(T, 256)
    blk, im = _hm_spec(tm)
    blk2, im2 = _hm_spec(tm, DN_CHUNK)
    full = (DN_HEADS, T, DN_DK)
    outs = [(full, bf16, blk, im), ((DN_HEADS, T, DN_CHUNK), bf16, blk2, im2), (full, f32, blk, im),
            (full, bf16, blk, im), (full, bf16, blk, im), (full, f32, blk, im)]
    return _tile_call(name, lambda i, *a: _dn_front(i == 0, *a), T // tm, _dn_front_ins(p, pba, cw, hp, tm), outs)


def _dn_front_bwd(name, p, pba, cw, hp, cts):
    T = p.shape[0]
    tm = _row_tile(T, 256)
    n = T // tm
    W3 = 3 * DN_HEADS * DN_DK
    blk, im = _hm_spec(tm)
    blk2, im2 = _hm_spec(tm, DN_CHUNK)
    ins = _dn_front_ins(p, pba, cw, hp, tm) + [(c, blk2 if c.shape[-1] == DN_CHUNK else blk, im) for c in cts]

    def fn(i, xc, xp, ba, cw, hp, *cts):
        _, vjp = jax.vjp(lambda *a: _dn_front(i == 0, *a), xc, xp, ba, cw, hp)
        return vjp(tuple(cts))

    return _tile_call(name, fn, n, ins,
                      [((T, W3), bf16, (tm, W3), lambda i: (i, 0)), ((n * HALO, W3), f32, (HALO, W3), lambda i: (i, 0)),
                       ((T, LANES), f32, (tm, LANES), lambda i: (i, 0))],
                      [(tuple(cw.shape), f32), (tuple(hp.shape), f32)])


def _scan_fwd(name, qd, qk, u, w, kt, egl):
    T = qd.shape[1]
    C = DN_CHUNK
    U = _pick(T // C, (SCAN_CHUNKS, 1))
    n = T // (C * U)

    def body(qd_ref, qk_ref, u_ref, w_ref, kt_ref, egl_ref, o_ref, sall_ref, s_ref):
        i = pl.program_id(0)

        @pl.when(i == 0)
        def _():
            s_ref[...] = jnp.zeros(s_ref.shape, f32)

        S = s_ref[...]
        for j in range(U):
            sl = pl.ds(j * C, C)
            sall_ref[:, j] = S
            o, S = _scan_step(S, *[r[:, sl, :] for r in (qd_ref, qk_ref, u_ref, w_ref, kt_ref, egl_ref)])
            o_ref[:, sl, :] = o
        s_ref[...] = S

    blk, im = _hm_spec(C * U)
    blk2, im2 = _hm_spec(C * U, C)
    return pl.pallas_call(
        body, name=name, grid=(n,),
        in_specs=[pl.BlockSpec(blk, im), pl.BlockSpec(blk2, im2)] + [pl.BlockSpec(blk, im)] * 4,
        out_specs=[pl.BlockSpec(blk, im), pl.BlockSpec((DN_HEADS, U, DN_DK, DN_DK), lambda i: (0, i, 0, 0))],
        out_shape=[jax.ShapeDtypeStruct((DN_HEADS, T, DN_DK), f32), jax.ShapeDtypeStruct((DN_HEADS, n * U, DN_DK, DN_DK), f32)],
        scratch_shapes=[pltpu.VMEM((DN_HEADS, DN_DK, DN_DK), f32)],
        compiler_params=pltpu.CompilerParams(dimension_semantics=("arbitrary",), vmem_limit_bytes=VMEM_LIMIT),
    )(qd, qk, u, w, kt, egl)


def _scan_bwd(name, qd, qk, u, w, kt, egl, s_all, do):
    T = qd.shape[1]
    C = DN_CHUNK
    U = _pick(T // C, (SCAN_CHUNKS, 1))
    n = T // (C * U)

    def body(qd_ref, qk_ref, u_ref, w_ref, kt_ref, egl_ref, sall_ref, do_ref,
             dqd_ref, dqk_ref, du_ref, dw_ref, dkt_ref, degl_ref, ds_ref):
        i = pl.program_id(0)

        @pl.when(i == 0)
        def _():
            ds_ref[...] = jnp.zeros(ds_ref.shape, f32)

        dS = ds_ref[...]
        for j in reversed(range(U)):
            sl = pl.ds(j * C, C)
            args = [r[:, sl, :].astype(f32) for r in (qd_ref, qk_ref, u_ref, w_ref, kt_ref, egl_ref)]
            _, vjp = jax.vjp(_scan_step, sall_ref[:, j], *args)
            dS, *cts = vjp((do_ref[:, sl, :], dS))
            for r, v in zip((dqd_ref, dqk_ref, du_ref, dw_ref, dkt_ref, degl_ref), cts):
                r[:, sl, :] = v
        ds_ref[...] = dS

    blk = (DN_HEADS, C * U, DN_DK)
    blk2 = (DN_HEADS, C * U, C)
    rim = lambda i: (0, n - 1 - i, 0)
    sp, sp2 = pl.BlockSpec(blk, rim), pl.BlockSpec(blk2, rim)
    full, full2 = jax.ShapeDtypeStruct((DN_HEADS, T, DN_DK), f32), jax.ShapeDtypeStruct((DN_HEADS, T, C), f32)
    return pl.pallas_call(
        body, name=name, grid=(n,),
        in_specs=[sp, sp2, sp, sp, sp, sp, pl.BlockSpec((DN_HEADS, U, DN_DK, DN_DK), lambda i: (0, n - 1 - i, 0, 0)), sp],
        out_specs=[sp, sp2, sp, sp, sp, sp],
        out_shape=[full, full2, full, full, full, full],
        scratch_shapes=[pltpu.VMEM((DN_HEADS, DN_DK, DN_DK), f32)],
        compiler_params=pltpu.CompilerParams(dimension_semantics=("arbitrary",), vmem_limit_bytes=VMEM_LIMIT),
    )(qd, qk, u, w, kt, egl, s_all, do)


def _dn_post_fwd(name, o, p, nw):
    T = p.shape[0]
    tm = _row_tile(T, 1024)
    blk, im = _hm_spec(tm)
    W = DN_HEADS * DN_DK
    return _tile_call(name, lambda i, o, z, nw: _dn_post(o, z, nw), T // tm,
                      [(o, blk, im), (p, (tm, W), lambda i: (i, 1536 // W)), _full(nw)],
                      [((T, W), f32, (tm, W), lambda i: (i, 0))])[0]


def _dn_post_bwd(name, o, p, nw, dout):
    T = p.shape[0]
    tm = _row_tile(T, 1024)
    blk, im = _hm_spec(tm)
    W = DN_HEADS * DN_DK

    def fn(i, o, z, nw, dout):
        _, vjp = jax.vjp(_dn_post, o, z, nw)
        return vjp(dout)

    return _tile_call(name, fn, T // tm,
                      [(o, blk, im), (p, (tm, W), lambda i: (i, 1536 // W)), _full(nw), _rows(dout, tm)],
                      [((DN_HEADS, T, DN_DK), f32, blk, im), ((T, W), f32, (tm, W), lambda i: (i, 0))],
                      [((1, DN_DK), f32)])


def _swa_tile(T, big):
    return _pick(T, (2 * WINDOW, WINDOW)) if big else WINDOW


@functools.partial(jax.custom_vjp, nondiff_argnums=(1,))
def _heads_split(x, nh):
    return jnp.stack([x[:, SWA_DH * h:SWA_DH * (h + 1)] for h in range(nh)], axis=0)


@jax.custom_vjp
def _heads_merge(y):
    return jnp.concatenate([y[h] for h in range(y.shape[0])], axis=1)


_heads_split.defvjp(lambda x, nh: (_heads_split(x, nh), None), lambda nh, _, g: (_heads_merge(g),))
_heads_merge.defvjp(lambda y: (_heads_merge(y), None), lambda _, g: (_heads_split(g, g.shape[1] // SWA_DH),))


def _swa_rows(first, qp, kc, kp, vc, vp, snk):
    kv = [_heads_split(t, SWA_KV_HEADS) for t in (kc, kp, vc, vp)]
    return _heads_merge(_swa(first, _heads_split(qp, SWA_HEADS), *kv, snk))


def _swa_ins(p, snk, W):
    qw, kw = SWA_HEADS * SWA_DH, SWA_KV_HEADS * SWA_DH
    return [(p, (W, qw), lambda i: (i, 2048 // qw)),
            (p, (W, kw), lambda i: (i, 6144 // kw)), (p, (W, kw), lambda i: (_prev(i), 6144 // kw)),
            (p, (W, kw), lambda i: (i, 6272 // kw)), (p, (W, kw), lambda i: (_prev(i), 6272 // kw)), _full(snk)]


def _swa_fwd(name, p, snk):
    T = p.shape[0]
    W = _swa_tile(T, True)
    qw = SWA_HEADS * SWA_DH
    return _tile_call(name, lambda i, *a: _swa_rows(i == 0, *a), T // W, _swa_ins(p, snk, W),
                      [((T, qw), f32, (W, qw), lambda i: (i, 0))])[0]


def _swa_bwd(name, p, snk, do):
    T = p.shape[0]
    W = _swa_tile(T, False)
    qw, kw = SWA_HEADS * SWA_DH, SWA_KV_HEADS * SWA_DH

    def fn(i, qp, k_c, k_p, v_c, v_p, snk, do):
        _, vjp = jax.vjp(lambda *a: _swa_rows(i == 0, *a), qp, k_c, k_p, v_c, v_p, snk)
        return vjp(do)

    kv_out = ((T, kw), f32, (W, kw), lambda i: (i, 0))
    return _tile_call(name, fn, T // W, _swa_ins(p, snk, W) + [(do, (W, qw), lambda i: (i, 0))],
                      [((T, qw), bf16, (W, qw), lambda i: (i, 0)), kv_out, kv_out, kv_out, kv_out], [(tuple(snk.shape), f32)])


def _xa_fwd(name, p, kv):
    T = p.shape[0]
    tm = _row_tile(T, 512)
    W = XA_HEADS * XA_DH
    return _tile_call(name, lambda i, q, kv: _xa(q, kv), T // tm, [(p, (tm, W), lambda i: (i, 2560 // W)), _full(kv)],
                      [((T, W), f32, (tm, W), lambda i: (i, 0))])[0]


def _xa_bwd(name, p, kv, do):
    T = p.shape[0]
    tm = _row_tile(T, 512)
    W = XA_HEADS * XA_DH

    def fn(i, q, kv, do):
        _, vjp = jax.vjp(_xa, q, kv)
        return vjp(do)

    return _tile_call(name, fn, T // tm, [(p, (tm, W), lambda i: (i, 2560 // W)), _full(kv), _rows(do, tm)],
                      [((T, W), bf16, (tm, W), lambda i: (i, 0))], [(tuple(kv.shape), f32)])


def _merge_fwd(name, o_dn, o_sw, o_xa, p, wb):
    T = p.shape[0]
    tm = _row_tile(T, 512)
    GW = N_BRANCH * D_MODEL
    return _tile_call(name, lambda i, a, b, c, g, w: _merge(a, b, c, g, w.astype(f32)), T // tm,
                      [_rows(o_dn, tm), _rows(o_sw, tm), _rows(o_xa, tm), (p, (tm, GW), lambda i: (i, 1)), _full(wb)],
                      [((T, D_MODEL), bf16, (tm, D_MODEL), lambda i: (i, 0))])[0]


def _merge_bwd(name, o_dn, o_sw, o_xa, p, wb, dm):
    T = p.shape[0]
    tm = _row_tile(T, 512)
    GW = N_BRANCH * D_MODEL

    def fn(i, a, b, c, g, w, dm):
        _, vjp = jax.vjp(_merge, a, b, c, g, w.astype(f32))
        return vjp(dm)

    bo = ((T, BRANCH_W), bf16, (tm, BRANCH_W), lambda i: (i, 0))
    return _tile_call(name, fn, T // tm,
                      [_rows(o_dn, tm), _rows(o_sw, tm), _rows(o_xa, tm), (p, (tm, GW), lambda i: (i, 1)), _full(wb), _rows(dm, tm)],
                      [bo, bo, bo, ((T, GW), bf16, (tm, GW), lambda i: (i, 0))], [(tuple(wb.shape), f32)])


def _assemble_dp(name, dxc, dxp, dz, dswq, dxaq, dgates, dkc, dkp, dvc, dvp, dba):
    T = dz.shape[0]
    tm = _row_tile(T, 256)
    n = T // tm
    r = tm // WINDOW
    nb = T // WINDOW

    def fn(i, dxc, dxp, dz, dswq, dxaq, dgates, dkc, dvc, dba, *nxt):
        halo = jnp.where(i == n - 1, 0.0, dxp)
        dqkv = dxc + jnp.concatenate([jnp.zeros((tm - HALO, dxc.shape[1]), f32), halo], axis=0)
        shifted = [jnp.concatenate([jnp.where(i * r + 1 + b <= nb - 1, blk, 0.0) for b, blk in enumerate(half)], axis=0)
                   for half in (nxt[:r], nxt[r:])]
        parts = [dqkv, dz, dswq, dxaq, dgates, dkc + shifted[0], dvc + shifted[1], dba, jnp.zeros((tm, LANES), f32)]
        return jnp.concatenate(parts, axis=1)

    ins = [_rows(dxc, tm), (dxp, (HALO, dxp.shape[1]), lambda i: (jnp.minimum(i + 1, n - 1), 0))]
    ins += [_rows(t, tm) for t in (dz, dswq, dxaq, dgates, dkc, dvc, dba)]
    for t in (dkp, dvp):
        ins += [(t, (WINDOW, t.shape[1]), lambda i, b=b: (jnp.minimum(i * r + 1 + b, nb - 1), 0)) for b in range(r)]
    return _tile_call(name, fn, n, ins, [((T, D_INP), bf16, (tm, D_INP), lambda i: (i, 0))])[0]


def _loss_grad(name, y, tgt):
    T = y.shape[0]
    tm = _row_tile(T, 1024)

    def fn(i, y, t):
        e = y - t
        part = 0.5 * jnp.sum(jnp.mean(e * e, axis=-1, keepdims=True), axis=0, keepdims=True)
        return e * (1.0 / D_MODEL), jnp.broadcast_to(part, (1, LANES))

    return _tile_call(name, fn, T // tm, [_rows(y, tm), _rows(tgt, tm)],
                      [((T, D_MODEL), f32, (tm, D_MODEL), lambda i: (i, 0))], [((1, LANES), f32)])


def _mem_ln_fwd(name, mem, g, b):
    M = mem.shape[0]
    return _tile_call(name, lambda i, m, g, b: (_ln(m, g, b),), 1, [_full(mem), _full(g), _full(b)],
                      [((M, D_MODEL), bf16, (M, D_MODEL), lambda i: (0, 0))])[0]


def _mem_ln_bwd(name, mem, g, b, dmn):
    def fn(i, m, g, b, d):
        _, vjp = jax.vjp(lambda g, b: _ln(m, g, b), g, b)
        return vjp(d)

    return _tile_call(name, fn, 1, [_full(mem), _full(g), _full(b), _full(dmn)], [], [((1, D_MODEL), f32), ((1, D_MODEL), f32)])


W_IN_RANGES = ((0, 1536), (1544, 2056), (2056, 2568), (2824, 3336), (3336, 6408), (2568, 2696), (2696, 2824), (1536, 1544))
W_IN_OFFS = tuple(sum(b - a for a, b in W_IN_RANGES[:k]) for k in range(len(W_IN_RANGES)))


def _pad_w_in(cols):
    parts = [cols(a, b) for a, b in W_IN_RANGES]
    return jnp.concatenate(parts + [jnp.zeros((parts[0].shape[0], D_INP - D_IN), parts[0].dtype)], axis=1)


def _unpad_dw_in(d, lo=0, hi=D_IN):
    parts = []
    for (a, b), off in sorted(zip(W_IN_RANGES, W_IN_OFFS)):
        a2, b2 = max(a, lo), min(b, hi)
        if a2 < b2:
            parts.append(d[:, off + a2 - a:off + b2 - a])
    return jnp.concatenate(parts, axis=1)


def _shard_cols(shards, l):
    n = shards[0].shape[-1]

    def cols(a, b):
        parts = [shards[s][l][:, max(a, s * n) - s * n:min(b, (s + 1) * n) - s * n] for s in range(N_SHARD)
                 if max(a, s * n) < min(b, (s + 1) * n)]
        return parts[0] if len(parts) == 1 else jnp.concatenate(parts, axis=1)

    return cols


def _layer_fwd(l, x, xb, mem_nb, W):
    n = lambda s: f"l{l}_{s}"
    sv = {}
    u1, a1 = _gu_act(n("ffn1_gu"), xb, W['ffn1_w_gu_p'])
    f1, h1, h1b = _mm_res_ln(n("ffn1_down"), a1, W['ffn1_w_down'], x, W['ln_g'][0:1], W['ln_b'][0:1], 0.5)
    p = _mm(n("w_in"), h1b, W['w_inp'], 'nn', out_dtype=bf16)
    pba = _mm(n("w_in_ba"), h1b, W['w_ba'], 'nn')
    qd, qk, u, w, kt, egl = _dn_front_fwd(n("dn_front"), p, pba, W['conv_w'], W['hp'])
    o_raw, s_all = _scan_fwd(n("dn_scan"), qd, qk, u, w, kt, egl)
    o_dn = _dn_post_fwd(n("dn_post"), o_raw, p, W['norm_w'])
    o_sw = _swa_fwd(n("swa"), p, W['snk'])
    kv = _mm(n("mem_kv"), mem_nb, W['w_mem_kv'], 'nn')
    o_xa = _xa_fwd(n("xa"), p, kv)
    merged = _merge_fwd(n("merge"), o_dn, o_sw, o_xa, p, W['w_branch'])
    mo, h2, h2b = _mm_res_ln(n("w_out"), merged, W['w_out'], h1, W['ln_g'][1:2], W['ln_b'][1:2], 1.0)
    u2, a2 = _gu_act(n("ffn2_gu"), h2b, W['ffn2_w_gu_p'])
    f2, h3, h3b = _mm_res_ln(n("ffn2_down"), a2, W['ffn2_w_down'], h2, W['ln_g'][2:3], W['ln_b'][2:3], 0.5)
    sv = dict(x=x, xb=xb, u1=u1, a1=a1, f1=f1, h1=h1, h1b=h1b, p=p, pba=pba, wy=(qd, qk, u, w, kt, egl),
              o_raw=o_raw, s_all=s_all, o_dn=o_dn, o_sw=o_sw, kv=kv, o_xa=o_xa, merged=merged,
              mo=mo, h2=h2, h2b=h2b, u2=u2, a2=a2, f2=f2)
    return h3, h3b, sv


def _ffn_bwd(n, tag, x, xb, u, a, f, g, b, w_gu_t, w_down_t, dh):
    dx_a, df, dg, db = _res_ln_bwd(n(f"{tag}_ln_bwd"), x, f, g, b, 0.5, dh)
    d_down = _mm(n(f"{tag}_down_dw"), a, df, 'tn')
    du = _down_dx_act_bwd(n(f"{tag}_down_dx"), df, w_down_t, u)
    d_gu = _mm(n(f"{tag}_gu_dw"), xb, du, 'tn', out_cols=(FF_BLK, lambda j: (j % 2) * (D_FF // FF_BLK) + j // 2))
    dx = _mm(n(f"{tag}_gu_dx"), du, w_gu_t, 'nn', add=dx_a)
    return dx, d_gu, d_down, dg, db


def _layer_bwd(l, sv, mem_nb, W, dh3, dmem_acc):
    n = lambda s: f"l{l}_{s}"
    G = {}
    dh2, G['ffn2_w_gu'], G['ffn2_w_down'], dg2, db2 = _ffn_bwd(
        n, "ffn2", sv['h2'], sv['h2b'], sv['u2'], sv['a2'], sv['f2'], W['ln_g'][2:3], W['ln_b'][2:3], W['ffn2_w_gu_t'], W['ffn2_w_down_t'], dh3)
    dh1_a, dmo, dg1, db1 = _res_ln_bwd(n("ln1_bwd"), sv['h1'], sv['mo'], W['ln_g'][1:2], W['ln_b'][1:2], 1.0, dh2)
    G['w_out'] = _mm(n("w_out_dw"), sv['merged'], dmo, 'tn')
    dmerged = _mm(n("w_out_dx"), dmo, W['w_out_t'], 'nn')
    p = sv['p']
    do_dn, do_sw, do_xa, dgates, G['w_branch'] = _merge_bwd(n("merge_bwd"), sv['o_dn'], sv['o_sw'], sv['o_xa'], p, W['w_branch'], dmerged)
    dxaq, dkv = _xa_bwd(n("xa_bwd"), p, sv['kv'], do_xa)
    dkv = dkv.astype(bf16)
    G['w_mem_kv'] = _mm(n("mem_kv_dw"), mem_nb, dkv, 'tn')
    dmem_n = _mm(n("mem_kv_dx"), dkv, W['w_mem_kv_t'], 'nn', add=dmem_acc)
    dswq, dkc, dkp, dvc, dvp, dsnk = _swa_bwd(n("swa_bwd"), p, W['snk'], do_sw)
    do_raw, dz, dnw = _dn_post_bwd(n("dn_post_bwd"), sv['o_raw'], p, W['norm_w'], do_dn)
    cts = _scan_bwd(n("dn_scan_bwd"), *sv['wy'], sv['s_all'], do_raw)
    dxc, dxp, dba, dcw, dhp = _dn_front_bwd(n("dn_front_bwd"), p, sv['pba'], W['conv_w'], W['hp'], cts)
    dp = _assemble_dp(n("dp"), dxc, dxp, dz, dswq, dxaq, dgates, dkc, dkp, dvc, dvp, dba)
    G['w_in_p'] = _mm(n("w_in_dw"), sv['h1b'], dp, 'tn')
    dh1 = _mm(n("w_in_dx"), dp, W['w_inp_t'], 'nn', add=dh1_a)
    dx, G['ffn1_w_gu'], G['ffn1_w_down'], dg0, db0 = _ffn_bwd(
        n, "ffn1", sv['x'], sv['xb'], sv['u1'], sv['a1'], sv['f1'], W['ln_g'][0:1], W['ln_b'][0:1], W['ffn1_w_gu_t'], W['ffn1_w_down_t'], dh1)
    G['ln_g'] = jnp.concatenate([dg0, dg1, dg2], axis=0)
    G['ln_b'] = jnp.concatenate([db0, db1, db2], axis=0)
    G['dn_conv_w'] = dcw
    G['dn_a_log'] = dhp[0, :DN_HEADS]
    G['dn_dt_bias'] = dhp[1, :DN_HEADS]
    G['dn_norm_w'] = dnw[0]
    G['swa_sinks'] = dsnk[:, 0]
    return dx, dmem_n, G


def _local_step(x, mem, tgt, Wf):
    mem_g, mem_b = Wf['mem_ln_g'][None, :], Wf['mem_ln_b'][None, :]
    mem_nb = _mem_ln_fwd("mem_ln", mem, mem_g, mem_b)

    def paired(k, l):
        return Wf[k + '_p'][l] if k + '_p' in Wf else _pair_gate_up(Wf[k][l])

    def w_in_cols(l):
        return _shard_cols(Wf['w_in_sh'], l) if 'w_in_sh' in Wf else (lambda a, b: Wf['w_in'][l][:, a:b])

    layers = []
    for l in range(DEPTH):
        layers.append(dict(
            ln_g=Wf['ln_g'][l], ln_b=Wf['ln_b'][l], ffn1_w_gu_p=paired('ffn1_w_gu', l), ffn1_w_down=Wf['ffn1_w_down'][l],
            w_inp=_pad_w_in(w_in_cols(l)), conv_w=Wf['dn_conv_w'][l],
            hp=jnp.zeros((8, LANES), f32).at[0, :DN_HEADS].set(Wf['dn_a_log'][l]).at[1, :DN_HEADS].set(Wf['dn_dt_bias'][l]),
            norm_w=Wf['dn_norm_w'][l][None, :], snk=jnp.broadcast_to(Wf['swa_sinks'][l][:, None], (SWA_HEADS, LANES)),
            w_mem_kv=Wf['w_mem_kv'][l], w_branch=Wf['w_branch'][l], w_out=Wf['w_out'][l],
            ffn2_w_gu_p=paired('ffn2_w_gu', l), ffn2_w_down=Wf['ffn2_w_down'][l]))
        layers[l]['w_ba'] = layers[l]['w_inp'][:, 6400:6400 + LANES]
        for k in ('ffn1_w_gu_p', 'ffn1_w_down', 'w_inp', 'w_mem_kv', 'w_out', 'ffn2_w_gu_p', 'ffn2_w_down'):
            layers[l][k.replace('_p', '') + '_t'] = layers[l][k].T
    h, hb = x, x.astype(bf16)
    saved = []
    for l in range(DEPTH):
        h, hb, sv = _layer_fwd(l, h, hb, mem_nb, layers[l])
        saved.append(sv)
    dh, loss_row = _loss_grad("loss", h, tgt)
    grads = [None] * DEPTH
    dmem_n = None
    for l in reversed(range(DEPTH)):
        dh, dmem_n, grads[l] = _layer_bwd(l, saved[l], mem_nb, layers[l], dh, dmem_n)
    dmg, dmb = _mem_ln_bwd("mem_ln_bwd", mem, mem_g, mem_b, dmem_n)
    G = {k: jnp.stack([grads[l][k] for l in range(DEPTH)], axis=0) for k in grads[0]}
    G['mem_ln_g'], G['mem_ln_b'] = dmg[0], dmb[0]
    return loss_row, dh, G


SEG_ALIGN = 2048


def _round_up(n, m):
    return (n + m - 1) // m * m


def _layout(names, sizes, mult):
    off, table = 0, {}
    for nm in names:
        table[nm] = (off, sizes[nm])
        off += _round_up(sizes[nm], SEG_ALIGN)
    return table, _round_up(off // LANES, mult)


def _pack(table, rows, flat):
    names = list(table)
    lead = flat[names[0]].shape[:-1]
    parts, pos = [], 0
    for nm in names:
        off, size = table[nm]
        if off > pos:
            parts.append(jnp.zeros(lead + (off - pos,), flat[nm].dtype))
        parts.append(flat[nm])
        pos = off + size
    total = rows * LANES
    if total > pos:
        parts.append(jnp.zeros(lead + (total - pos,), parts[-1].dtype))
    return jnp.concatenate(parts, axis=-1).reshape(lead + (rows, LANES))


def _unpack(table, packed, nm):
    off, size = table[nm]
    flat = packed.reshape(packed.shape[:-2] + (-1,))
    return flat[..., off:off + size]


def _split_shards(full, ax):
    shp = full.shape
    t = full.reshape(shp[:ax] + (N_SHARD, shp[ax] // N_SHARD) + shp[ax + 1:])
    return jnp.moveaxis(t, ax, 0)


def _join_shards(sh4, ax):
    return jnp.concatenate([sh4[s] for s in range(N_SHARD)], axis=ax)


ANY = pl.BlockSpec(memory_space=pl.ANY)


def _me():
    return lax.axis_index("x"), lax.axis_index("y"), lax.axis_index("c")


def _comm_call(name, body, arrays, out_shapes, n_sem):
    n = len(arrays)
    scratch = [pltpu.SemaphoreType.DMA((n, n_sem)), pltpu.SemaphoreType.DMA((n, n_sem))]
    return pl.pallas_call(
        body, name=name, out_shape=out_shapes, in_specs=[ANY] * n, out_specs=[ANY] * n, scratch_shapes=scratch,
        compiler_params=pltpu.CompilerParams(has_side_effects=True),
    )(*arrays)


def _all_gather(name, xs):
    n = len(xs)

    def split(shape):
        ax = next(i for i, d in enumerate(shape) if d % 2 == 0)
        return ax, shape[ax] // 2

    def body(*refs):
        x_refs, out_refs = refs[:n], refs[n:2 * n]
        send_sems, recv_sems = refs[2 * n:]
        mx, my, mc = _me()
        me_s = 2 * mx + my
        across_x, across_y, diag = (1 - mx, my), (mx, 1 - my), (1 - mx, 1 - my)
        sid = lambda chip: 2 * chip[0] + chip[1]

        def part(ref, a, which):
            if which is None:
                return ref
            ax, size = split(xs[a].shape[1:])
            return ref.at[(slice(None),) * ax + (pl.ds(which * size, size),)]

        def copy(a, k, shard, half, which, to, src=None):
            dst = part(out_refs[a].at[shard, half], a, which)
            return pltpu.make_async_remote_copy(src_ref=dst if src is None else part(src, a, which), dst_ref=dst,
                                                send_sem=send_sems.at[a, k], recv_sem=recv_sems.at[a, k], device_id=to, device_id_type=MESH)

        sibling, here = (mx, my, 1 - mc), (mx, my, mc)
        sent = []
        for a in range(n):
            sent += [copy(a, 0, me_s, mc, None, (*across_x, mc), src=x_refs[a].at[mc]),
                     copy(a, 1, me_s, mc, None, (*across_y, mc), src=x_refs[a].at[mc])]
        for cp in sent:
            cp.start()
        landing = [(0, across_x, None, across_y), (1, across_y, None, across_x), (2, diag, 0, None), (3, diag, 1, None)]
        for k, origin, which, relay_to in landing:
            for a in range(n):
                copy(a, k, sid(origin), mc, which, here).wait_recv()
                if relay_to is not None:
                    sent.append(copy(a, 2 + k, sid(origin), mc, k, (*relay_to, mc)))
                    sent[-1].start()
                sent.append(copy(a, 4 + k, sid(origin), mc, which, sibling))
                sent[-1].start()
        for k, origin, which, _ in landing:
            for a in range(n):
                copy(a, 4 + k, sid(origin), 1 - mc, which, here).wait_recv()
        for cp in sent:
            cp.wait_send()

    return _comm_call(name, body, xs, [jax.ShapeDtypeStruct((N_SHARD,) + x.shape, x.dtype) for x in xs], 8)


def _pair_exchange(name, items):
    n = len(items)

    def body(*refs):
        src_refs, dst_refs = refs[:n], refs[n:2 * n]
        send_sems, recv_sems = refs[2 * n:]
        mx, my, mc = _me()
        cps = [pltpu.make_async_remote_copy(src_ref=src_refs[a].at[s, 1 - mc], dst_ref=dst_refs[a].at[s], send_sem=send_sems.at[a, s],
                                            recv_sem=recv_sems.at[a, s], device_id=(mx, my, 1 - mc), device_id_type=MESH)
               for a in range(n) for s in range(N_SHARD)]
        for cp in cps:
            cp.start()
        for cp in cps:
            cp.wait()

    return _comm_call(name, body, items, [jax.ShapeDtypeStruct((N_SHARD,) + t.shape[2:], t.dtype) for t in items], N_SHARD)


def _chip_exchange(name, parts):
    n = len(parts)

    def body(*refs):
        p_refs, dst_refs = refs[:n], refs[n:2 * n]
        send_sems, recv_sems = refs[2 * n:]
        mx, my, mc = _me()
        chips = [(1 - mx, my), (mx, 1 - my), (1 - mx, 1 - my)]
        cps = [pltpu.make_async_remote_copy(src_ref=p_refs[a].at[2 * cx + cy], dst_ref=dst_refs[a].at[j], send_sem=send_sems.at[a, j],
                                            recv_sem=recv_sems.at[a, j], device_id=(cx, cy, mc), device_id_type=MESH)
               for a in range(n) for j, (cx, cy) in enumerate(chips)]
        for cp in cps:
            cp.start()
        for cp in cps:
            cp.wait()

    return _comm_call(name, body, parts, [jax.ShapeDtypeStruct((3,) + t.shape[1:], t.dtype) for t in parts], 3)


def _pair_swap(name, reds):
    n = len(reds)

    def body(*refs):
        r_refs, out_refs = refs[:n], refs[n:2 * n]
        send_sems, recv_sems = refs[2 * n:]
        mx, my, mc = _me()
        cps = [pltpu.make_async_remote_copy(src_ref=r_refs[a], dst_ref=out_refs[a], send_sem=send_sems.at[a, 0],
                                            recv_sem=recv_sems.at[a, 0], device_id=(mx, my, 1 - mc), device_id_type=MESH)
               for a in range(n)]
        for cp in cps:
            cp.start()
        for cp in cps:
            cp.wait()

    return _comm_call(name, body, reds, [jax.ShapeDtypeStruct(t.shape, t.dtype) for t in reds], 1)


EW_BLOCK_BYTES = 2 << 20


def _ew_call(name, fn, ins, n_out, out_dtype=f32):
    shape = ins[0].shape
    last = shape[-1]
    flat = [a.reshape(-1, last) for a in ins]
    R = flat[0].shape[0]
    cands = tuple(c for c in (4096, 2048, 1024, 512, 256, 128, 64, 32, 16, 8) if c * last * 4 <= EW_BLOCK_BYTES)
    tr = _pick(R, cands)
    res = _tile_call(name, lambda i, *a: fn(*a), R // tr, [(a, (tr, last), lambda i: (i, 0)) for a in flat],
                     [((R, last), out_dtype, (tr, last), lambda i: (i, 0))] * n_out)
    return [r.reshape(shape) for r in res]


def _adamw(g, w, m, v):
    m = B1 * m + (1.0 - B1) * g
    v = B2 * v + (1.0 - B2) * jnp.square(g)
    m_hat = m / (1.0 - B1 ** STEP)
    v_hat = v / (1.0 - B2 ** STEP)
    return -LR * (m_hat / (jnp.sqrt(v_hat) + EPS) + WD * w), m, v


def _adamw_call(name, mine, theirs, w, m, v, mc1):
    shape, last = w.shape, w.shape[-1]
    g2 = [t.reshape(-1, last) for t in (mine, theirs)]
    w3 = [t.reshape(2, -1, last) for t in (w, m, v)]
    R = g2[0].shape[0]
    tr = _pick(R, tuple(c for c in (4096, 2048, 1024, 512, 256, 128, 64, 32, 16, 8) if c * last * 4 <= EW_BLOCK_BYTES))

    def body(mc_ref, mine_ref, theirs_ref, w_ref, m_ref, v_ref, g_out, d_out, m_out, v_out):
        g = jnp.where(pl.program_id(0) == mc_ref[0], mine_ref[...], theirs_ref[...])
        d, nm, nv = _adamw(g, w_ref[0], m_ref[0], v_ref[0])
        g_out[0], d_out[0], m_out[0], v_out[0] = g, d, nm, nv

    half = pl.BlockSpec((tr, last), lambda h, i: (i, 0))
    full = pl.BlockSpec((1, tr, last), lambda h, i: (h, i, 0))
    res = pl.pallas_call(
        body, name=name, grid=(2, R // tr),
        in_specs=[pl.BlockSpec(memory_space=pltpu.SMEM), half, half, full, full, full], out_specs=[full] * 4,
        out_shape=[jax.ShapeDtypeStruct((2, R, last), f32)] * 4,
        compiler_params=pltpu.CompilerParams(dimension_semantics=("arbitrary", "arbitrary"), vmem_limit_bytes=VMEM_LIMIT),
    )(mc1, *g2, *w3)
    return tuple(r.reshape(shape) for r in res)


def kernel(x, mem, mem_ln_g, mem_ln_b, ln_g, ln_b, ffn1_w_gu, ffn1_w_down, w_in, dn_conv_w, dn_a_log, dn_dt_bias, dn_norm_w, swa_sinks, w_mem_kv, w_branch, w_out, ffn2_w_gu, ffn2_w_down, loss_target, m_mem_ln_g, m_mem_ln_b, m_ln_g, m_ln_b, m_ffn1_w_gu, m_ffn1_w_down, m_w_in, m_dn_conv_w, m_dn_a_log, m_dn_dt_bias, m_dn_norm_w, m_swa_sinks, m_w_mem_kv, m_w_branch, m_w_out, m_ffn2_w_gu, m_ffn2_w_down, v_mem_ln_g, v_mem_ln_b, v_ln_g, v_ln_b, v_ffn1_w_gu, v_ffn1_w_down, v_w_in, v_dn_conv_w, v_dn_a_log, v_dn_dt_bias, v_dn_norm_w, v_swa_sinks, v_w_mem_kv, v_w_branch, v_w_out, v_ffn2_w_gu, v_ffn2_w_down):
    args = dict(locals())
    Ws = {n: args[n] for n in WEIGHTS}
    Ms = {n: args["m_" + n] for n in WEIGHTS}
    Vs = {n: args["v_" + n] for n in WEIGHTS}
    mc = lax.axis_index("c")
    my_s = 2 * lax.axis_index("x") + lax.axis_index("y")
    small = [n for n in WEIGHTS if n not in MATRICES]

    ag_table, ag_rows = _layout(SMALL_SHARDED, {n: Ws[n].size for n in SMALL_SHARDED}, 16)
    ag_small = _pack(ag_table, ag_rows, {n: Ws[n].reshape(-1) for n in SMALL_SHARDED}).reshape(2, ag_rows // 2, LANES)
    local = [Ws[n].astype(bf16) for n in MATRICES] + [ag_small]
    gathered = _all_gather("all_gather_w", local)
    gathered = [jnp.stack([jnp.where(my_s == s, loc, g[s]) for s in range(N_SHARD)], axis=0) for loc, g in zip(local, gathered)]
    Wf = {n: _join_shards(g, SHARD_AXIS[n]) for n, g in zip(MATRICES, gathered) if not n.endswith('w_gu') and n != 'w_in'}
    for n, g in zip(MATRICES, gathered):
        if n.endswith('w_gu'):
            assert g.shape[-1] == FF_BLK
            Wf[n + '_p'] = jnp.concatenate([g[0], g[2], g[1], g[3]], axis=2)
        if n == 'w_in':
            Wf['w_in_sh'] = g
    g_small = gathered[-1].reshape(N_SHARD, ag_rows, LANES)
    for n in SMALL_SHARDED:
        Wf[n] = _join_shards(_unpack(ag_table, g_small, n).reshape((N_SHARD,) + Ws[n].shape), SHARD_AXIS[n])
    for n in WEIGHTS:
        if SHARD_AXIS[n] is None:
            Wf[n] = Ws[n]

    loss_row, dx, G = _local_step(x[0], mem[0], loss_target[0], Wf)

    table, rows = _layout(small + ['loss'], {**{n: Ws[n].size for n in small}, 'loss': 1}, 16)
    gflat = {n: (jnp.broadcast_to(G[n].reshape(1, -1), (N_SHARD, G[n].size)) if SHARD_AXIS[n] is None
                 else _split_shards(G[n], SHARD_AXIS[n]).reshape(N_SHARD, -1)) for n in small}
    gflat['loss'] = jnp.broadcast_to(loss_row[:, :1], (N_SHARD, 1))
    n_in = Ws['w_in'].shape[-1]
    items =[_split_shards(G[n], SHARD_AXIS[n]) if n != 'w_in' else
             jnp.stack([jnp.stack([_unpad_dw_in(G['w_in_p'][l], s * n_in, (s + 1) * n_in) for l in range(DEPTH)]) for s in range(N_SHARD)])
             for n in MATRICES] + [_pack(table, rows, gflat).reshape(N_SHARD, 2, rows // 2, LANES)]
    tags = MATRICES + ['small']
    got = _pair_exchange("rs_pair", items)
    keep = [lax.dynamic_index_in_dim(a, mc, axis=1, keepdims=False) for a in items]
    wire = [bf16] * len(MATRICES) + [f32]
    part = [_ew_call(f"rs_add_pair_{t}", lambda a, b: a + b, [k, g], 1, out_dtype=dt)[0] for t, k, g, dt in zip(tags, keep, got, wire)]
    others = _chip_exchange("rs_chips", part)
    own = lambda a: lax.dynamic_index_in_dim(a, my_s, axis=0, keepdims=False)
    red = [_ew_call(f"rs_add_chips_{t}", lambda k, g, fx, fy, fxy: ((k + g) + fy) + (fx + fxy), [own(k), own(g), o[0], o[1], o[2]], 1)[0]
           for t, k, g, o in zip(tags, keep, got, others)]
    theirs = _pair_swap("rs_swap", red)

    mc1 = mc.astype(i32).reshape(1)
    outs = {}
    for n, a, b in zip(MATRICES, red, theirs):
        outs[n] = _adamw_call(f"adamw_{n}", a, b, Ws[n], Ms[n], Vs[n], mc1)
    fill = {'loss': jnp.zeros((1,), f32)}
    packs = [_pack(table, rows, {**{n: src[n].reshape(-1) for n in small}, **fill}).reshape(2, rows // 2, LANES) for src in (Ws, Ms, Vs)]
    small_out = [p.reshape(rows, LANES) for p in _adamw_call("adamw_small", red[-1], theirs[-1], *packs, mc1)]
    for n in small:
        outs[n] = tuple(_unpack(table, p, n).reshape(Ws[n].shape) for p in small_out)
    loss = _unpack(table, small_out[0], 'loss').reshape(())
    return (loss, dx[None], *[outs[n][k] for k in range(4) for n in WEIGHTS])
```

```python
import functools

import jax
import jax.numpy as jnp
from jax import lax
from jax.experimental import pallas as pl
from jax.experimental.pallas import tpu as pltpu

f32, bf16, i32 = jnp.float32, jnp.bfloat16, jnp.int32
MESH = pl.DeviceIdType.MESH

D_MODEL = 1024
DEPTH = 2
DN_HEADS, DN_DK, DN_CONV, DN_CHUNK = 4, 128, 4, 64
SWA_HEADS, SWA_KV_HEADS, SWA_DH, WINDOW = 8, 2, 64, 128
XA_HEADS, XA_DH = 4, 128
D_FF = 2816
N_BRANCH, BRANCH_W = 3, 512
ALPHA = (2 * DEPTH) ** 0.25
LN_EPS, RMS_EPS, NEG_INF = 1e-5, 1e-6, -1e30
D_IN = 6408
D_INP = 6656
LR, B1, B2, EPS, WD, STEP = 0.001, 0.9, 0.999, 1e-08, 0.01, 10

LANES = 128
VMEM_LIMIT = 56 << 20
N_SHARD = 4
SCAN_CHUNKS = 16
DN_TILE = 128
HALO = 16
MM_TILES = {
    'ffn1_gu_dx': (512, 1024, 5632), 'ffn2_gu_dx': (512, 1024, 5632), 'w_in_dx': (512, 1024, 6656), 'w_in': (2048, 512, 1024),
    'w_in_dw': (1024, 512, 4096), 'ffn1_down_dw': (1408, 512, 4096), 'ffn2_down_dw': (1408, 512, 4096),
    'ffn1_down_dx': (1024,), 'ffn2_down_dx': (1024,), 'ffn1_down': (1024,), 'ffn2_down': (1024,), 'w_out': (1024,),
    'ffn1_gu': (1024,), 'ffn2_gu': (1024,),
    'w_out_dx': (512, 1024, 1024), 'l1_w_in': (2048, 1664, 1024),
}


def _tiles(name, default=None):
    return MM_TILES.get(name, MM_TILES.get(name.split('_', 1)[-1], default))

WEIGHTS = ['mem_ln_g', 'mem_ln_b', 'ln_g', 'ln_b', 'ffn1_w_gu', 'ffn1_w_down', 'w_in', 'dn_conv_w', 'dn_a_log',
           'dn_dt_bias', 'dn_norm_w', 'swa_sinks', 'w_mem_kv', 'w_branch', 'w_out', 'ffn2_w_gu', 'ffn2_w_down']
SHARD_AXIS = {'mem_ln_g': None, 'mem_ln_b': None, 'ln_g': 2, 'ln_b': 2, 'ffn1_w_gu': 2, 'ffn1_w_down': 1, 'w_in': 2,
              'dn_conv_w': 2, 'dn_a_log': None, 'dn_dt_bias': None, 'dn_norm_w': None, 'swa_sinks': None,
              'w_mem_kv': 1, 'w_branch': 3, 'w_out': 1, 'ffn2_w_gu': 2, 'ffn2_w_down': 1}
MATRICES = ['ffn1_w_gu', 'ffn1_w_down', 'w_in', 'w_mem_kv', 'w_branch', 'w_out', 'ffn2_w_gu', 'ffn2_w_down']
SMALL_SHARDED = ['ln_g', 'ln_b', 'dn_conv_w']


def _dg(a, b, mode, hi):
    nb = a.ndim - 2
    bd = tuple(range(nb))
    ca = nb if mode == 'tn' else nb + 1
    cb = nb + 1 if mode == 'nt' else nb
    dims = (((ca,), (cb,)), (bd, bd))
    dot = lambda x, y: lax.dot_general(x, y, dims, preferred_element_type=f32)
    a_hi, b_hi = a.astype(bf16), b.astype(bf16)
    if not hi:
        return dot(a_hi, b_hi)
    a_lo = (a.astype(f32) - a_hi.astype(f32)).astype(bf16)
    b_lo = (b.astype(f32) - b_hi.astype(f32)).astype(bf16)
    return dot(a_hi, b_hi) + (dot(a_hi, b_lo) + dot(a_lo, b_hi))


@functools.partial(jax.custom_vjp, nondiff_argnums=(2, 3))
def _dot(a, b, mode, hi):
    return _dg(a, b, mode, hi)


def _dot_fwd(a, b, mode, hi):
    return _dg(a, b, mode, hi), (a, b)


def _dot_bwd(mode, hi, res, g):
    a, b = res
    if mode == 'nn':
        da, db = _dg(g, b, 'nt', hi), _dg(a, g, 'tn', hi)
    elif mode == 'nt':
        da, db = _dg(g, b, 'nn', hi), _dg(g, a, 'tn', hi)
    else:
        da, db = _dg(b, g, 'nt', hi), _dg(a, g, 'nn', hi)
    return da.astype(a.dtype), db.astype(b.dtype)


_dot.defvjp(_dot_fwd, _dot_bwd)


def bdot(a, b, mode):
    return _dot(a, b, mode, False)


def hdot(a, b, mode):
    return _dot(a, b, mode, True)


def _shift_rows_impl(x, tail, s):
    r = pltpu.roll(x, s, 0)
    rows = lax.broadcasted_iota(i32, tail.shape, 0)
    top = jnp.where(rows >= s, r[:HALO], pltpu.roll(tail, s, 0))
    return jnp.concatenate([top, r[HALO:]], axis=0)


@functools.partial(jax.custom_vjp, nondiff_argnums=(2,))
def _shift_rows(x, tail, s):
    return _shift_rows_impl(x, tail, s)


def _shift_rows_fwd(x, tail, s):
    return _shift_rows_impl(x, tail, s), None


def _shift_rows_bwd(s, _, g):
    n = g.shape[0]
    r = pltpu.roll(g, n - s, 0)
    rows = lax.broadcasted_iota(i32, (HALO, g.shape[1]), 0)
    last = r[n - HALO:]
    keep = rows < HALO - s
    dx = jnp.concatenate([r[:n - HALO], jnp.where(keep, last, 0.0)], axis=0)
    return dx, jnp.where(keep, 0.0, pltpu.roll(g[:HALO], HALO - s, 0))


_shift_rows.defvjp(_shift_rows_fwd, _shift_rows_bwd)


def _lane_pick(x, lane):
    idx = lax.broadcasted_iota(i32, x.shape, x.ndim - 1)
    return jnp.sum(jnp.where(idx == lane, x, 0.0), axis=-1, keepdims=True)


def _silu(x):
    return x * jax.nn.sigmoid(x)


def _tri_inv(a):
    C = a.shape[-1]
    eye = (lax.broadcasted_iota(i32, a.shape, 1) == lax.broadcasted_iota(i32, a.shape, 2)).astype(f32)
    p = -a
    x = eye + p
    for _ in range((C - 1).bit_length() - 1):
        p = _dg(p, p, 'nn', True)
        x = x + _dg(x, p, 'nn', True)
    return x


@jax.custom_vjp
def _tri_solve(a, rhs):
    return _dg(_tri_inv(a), rhs, 'nn', True)


def _tri_solve_fwd(a, rhs):
    tinv = _tri_inv(a)
    sol = _dg(tinv, rhs, 'nn', True)
    return sol, (tinv, sol)


def _tri_solve_bwd(res, g):
    tinv, sol = res
    d_rhs = _dg(tinv, g, 'tn', True)
    return -_dg(d_rhs, sol, 'nt', True), d_rhs


_tri_solve.defvjp(_tri_solve_fwd, _tri_solve_bwd)


@functools.partial(jax.custom_vjp, nondiff_argnums=(1,))
def _lane_head(x, n):
    return x[:, :, :n]


def _lane_head_fwd(x, n):
    return x[:, :, :n], None


def _lane_head_bwd(n, _, g):
    s = jnp.sum(g, axis=-1, keepdims=True) * (1.0 / LANES)
    return (jnp.broadcast_to(s, g.shape[:-1] + (LANES,)),)


_lane_head.defvjp(_lane_head_fwd, _lane_head_bwd)


@functools.partial(jax.custom_vjp, nondiff_argnums=(1,))
def _last_row(x, c):
    return x[:, c - 1:, :]


def _last_row_fwd(x, c):
    return x[:, c - 1:, :], None


def _last_row_bwd(c, _, g):
    shape = (g.shape[0], c, g.shape[2])
    rows = lax.broadcasted_iota(i32, shape, 1)
    return (jnp.where(rows == c - 1, jnp.broadcast_to(g, shape), 0.0),)


_last_row.defvjp(_last_row_fwd, _last_row_bwd)


def _full(a):
    nd = a.ndim
    return (a, tuple(a.shape), lambda i, _nd=nd: (0,) * _nd)


def _rows(a, tm, col=0, width=None):
    width = a.shape[1] if width is None else width
    return (a, (tm, width), lambda i, _c=col: (i, _c))


def _tile_call(name, fn, n, ins, outs, accs=()):
    n_in, n_out, n_acc = len(ins), len(outs), len(accs)

    def body(*refs):
        i = pl.program_id(0)
        res = fn(i, *[r[...].astype(f32) for r in refs[:n_in]])
        if not isinstance(res, (tuple, list)):
            res = (res,)
        assert len(res) == n_out + n_acc, (name, len(res), n_out, n_acc)
        for r, v in zip(refs[n_in:n_in + n_out], res[:n_out]):
            r[...] = v.astype(r.dtype)
        if n_acc:
            acc_refs = refs[n_in + n_out:]

            @pl.when(i == 0)
            def _():
                for r in acc_refs:
                    r[...] = jnp.zeros(r.shape, r.dtype)

            for r, v in zip(acc_refs, res[n_out:]):
                r[...] += v.astype(r.dtype)

    out_shape = [jax.ShapeDtypeStruct(s, d) for s, d, _, _ in outs] + [jax.ShapeDtypeStruct(s, d) for s, d in accs]
    out_specs = [pl.BlockSpec(b, m) for _, _, b, m in outs]
    out_specs += [pl.BlockSpec(tuple(s), lambda i, _nd=len(s): (0,) * _nd) for s, _ in accs]
    res = pl.pallas_call(
        body, name=name, grid=(n,),
        in_specs=[pl.BlockSpec(b, m) for _, b, m in ins],
        out_specs=out_specs, out_shape=out_shape,
        compiler_params=pltpu.CompilerParams(dimension_semantics=("arbitrary",), vmem_limit_bytes=VMEM_LIMIT),
    )(*[a for a, _, _ in ins])
    return res


def _pick(n, cands):
    for c in cands:
        if n % c == 0:
            return c
    return n


def _mm(name, a, b, mode, out_dtype=f32, add=None, out_cols=None):
    if mode == 'tn':
        K, M = a.shape
    else:
        M, K = a.shape
    N = b.shape[0] if mode == 'nt' else b.shape[1]
    tm = _pick(M, (1024, 1408, 512, 256, 128))
    tn = _pick(N, (512, 1408, 256, 128)) if out_cols is None else out_cols[0]
    tuned = _tiles(name)
    if tuned is not None and (M % tuned[0] or N % tuned[1] or K % tuned[2]):
        tuned = None
    if tuned is not None:
        tm, tn = tuned[:2]
    col = (lambda j: j) if out_cols is None else out_cols[1]
    tk = K if K <= 3328 else _pick(K, (3328, 2816, 2048, 1024, 512, 256, 128))
    if tuned is not None:
        tk = tuned[2]
    nk = K // tk
    ca = 0 if mode == 'tn' else 1
    cb = 1 if mode == 'nt' else 0
    dims = (((ca,), (cb,)), ((), ()))

    def body(*refs):
        a_ref, b_ref = refs[0], refs[1]
        add_ref = refs[2] if add is not None else None
        part = lax.dot_general(a_ref[...].astype(bf16), b_ref[...].astype(bf16), dims, preferred_element_type=f32)

        def finish(r, o_ref):
            if add_ref is not None:
                r = r + add_ref[...].astype(f32)
            o_ref[...] = r.astype(o_ref.dtype)

        if nk == 1:
            finish(part, refs[-1])
            return
        o_ref, acc_ref = refs[-2], refs[-1]
        k = pl.program_id(2)

        @pl.when(k == 0)
        def _():
            acc_ref[...] = part

        @pl.when(k > 0)
        def _():
            acc_ref[...] += part

        @pl.when(k == nk - 1)
        def _():
            finish(acc_ref[...], o_ref)

    a_spec = pl.BlockSpec((tk, tm), lambda i, j, k: (k, i)) if mode == 'tn' else pl.BlockSpec((tm, tk), lambda i, j, k: (i, k))
    b_spec = pl.BlockSpec((tn, tk), lambda i, j, k: (j, k)) if mode == 'nt' else pl.BlockSpec((tk, tn), lambda i, j, k: (k, j))
    in_specs, args = [a_spec, b_spec], [a, b]
    if add is not None:
        in_specs.append(pl.BlockSpec((tm, tn), lambda i, j, k: (i, j)))
        args.append(add)
    return pl.pallas_call(
        body, name=name, grid=(M // tm, N // tn, nk), in_specs=in_specs,
        out_specs=pl.BlockSpec((tm, tn), lambda i, j, k: (i, col(j))),
        out_shape=jax.ShapeDtypeStruct((M, N), out_dtype),
        scratch_shapes=[pltpu.VMEM((tm, tn), f32)] if nk > 1 else [],
        compiler_params=pltpu.CompilerParams(dimension_semantics=("parallel", "parallel", "arbitrary"), vmem_limit_bytes=VMEM_LIMIT),
    )(*args)


def _res_ln(x, f, g, b, s):
    r = ALPHA * x + s * f
    mu = jnp.mean(r, axis=-1, keepdims=True)
    rc = r - mu
    var = jnp.mean(rc * rc, axis=-1, keepdims=True)
    return rc * lax.rsqrt(var + LN_EPS) * g + b


def _ln(x, g, b):
    mu = jnp.mean(x, axis=-1, keepdims=True)
    xc = x - mu
    var = jnp.mean(xc * xc, axis=-1, keepdims=True)
    return xc * lax.rsqrt(var + LN_EPS) * g + b


def _dn_pre(first, xc, xp, ba, cw, hp):
    xp = jnp.where(first, 0.0, xp)
    y = cw[DN_CONV - 1:DN_CONV, :] * xc
    for j in range(DN_CONV - 1):
        y = y + cw[j:j + 1, :] * _shift_rows(xc, xp, DN_CONV - 1 - j)
    c = _silu(y)
    qs, ks, vs, gs, bs = [], [], [], [], []
    nqk = DN_HEADS * DN_DK
    for h in range(DN_HEADS):
        q = c[:, h * DN_DK:(h + 1) * DN_DK]
        k = c[:, nqk + h * DN_DK:nqk + (h + 1) * DN_DK]
        v = c[:, 2 * nqk + h * DN_DK:2 * nqk + (h + 1) * DN_DK]
        qs.append(q * lax.rsqrt(jnp.sum(q * q, axis=-1, keepdims=True) + RMS_EPS))
        ks.append(k * lax.rsqrt(jnp.sum(k * k, axis=-1, keepdims=True) + RMS_EPS))
        vs.append(v)
        beta = jax.nn.sigmoid(_lane_pick(ba, h))
        a_log = _lane_pick(hp[0:1, :], h)
        dt = _lane_pick(hp[1:2, :], h)
        g = -jnp.exp(a_log) * jax.nn.softplus(_lane_pick(ba, DN_HEADS + h) + dt)
        gs.append(jnp.broadcast_to(g, q.shape))
        bs.append(jnp.broadcast_to(beta, q.shape))
    return tuple(jnp.stack(t, axis=0) for t in (qs, ks, vs, gs, bs))


def _wy(q, k, v, gb, bb):
    B, C, _ = q.shape
    ri = lax.broadcasted_iota(i32, (B, C, C), 1)
    ci = lax.broadcasted_iota(i32, (B, C, C), 2)
    tril, strict = ri >= ci, ri > ci
    gc = hdot(tril.astype(f32), gb, 'nn')
    gl = jnp.broadcast_to(_last_row(gc, C), gc.shape)
    col = _lane_head(gc, C)
    decay = jnp.exp(jnp.where(tril, col - jnp.swapaxes(col, 1, 2), NEG_INF))
    qs = q * (DN_DK ** -0.5)
    kb = k * bb
    a = jnp.where(strict, bdot(kb, k, 'nt') * decay, 0.0)
    sol = _tri_solve(a, jnp.concatenate([v * bb, kb * jnp.exp(gc)], axis=-1))
    qk = jnp.where(tril, bdot(qs, k, 'nt') * decay, 0.0)
    kt = k * jnp.exp(gl - gc)
    qd = qs * jnp.exp(gc)
    return qd, qk, sol[..., :DN_DK], sol[..., DN_DK:], kt, jnp.exp(gl)


def _scan_step(S, qd, qk, u, w, kt, egl):
    vn = u - bdot(w, S, 'nn')
    o = bdot(qd, S, 'nn') + bdot(qk, vn, 'nn')
    e2 = jnp.concatenate([egl] * (DN_DK // DN_CHUNK), axis=1)
    return o, S * e2 + bdot(kt, vn, 'tn')


def _dn_post(o, z, nw):
    outs = []
    for h in range(DN_HEADS):
        oh = o[h]
        oh = oh * lax.rsqrt(jnp.mean(oh * oh, axis=-1, keepdims=True) + RMS_EPS) * nw
        outs.append(oh * _silu(z[:, h * DN_DK:(h + 1) * DN_DK]))
    return jnp.concatenate(outs, axis=1)


def _swa(first, q, kc, kp, vc, vp, snk):
    W = q.shape[1]
    G = SWA_HEADS // SWA_KV_HEADS
    r = lax.broadcasted_iota(i32, (G, W, 2 * W), 1)
    c = lax.broadcasted_iota(i32, (G, W, 2 * W), 2)
    mask = (c > W + r - WINDOW) & (c <= W + r) & jnp.logical_or(c >= W, jnp.logical_not(first))
    sink_all = _lane_pick(snk, 0)
    outs = []
    for j in range(SWA_KV_HEADS):
        qj = q[j * G:(j + 1) * G]
        kk = jnp.broadcast_to(jnp.concatenate([kp[j], kc[j]], axis=0)[None], (G, 2 * W, SWA_DH))
        vv = jnp.broadcast_to(jnp.concatenate([vp[j], vc[j]], axis=0)[None], (G, 2 * W, SWA_DH))
        s = jnp.where(mask, bdot(qj, kk, 'nt') * (SWA_DH ** -0.5), NEG_INF)
        sink = sink_all[j * G:(j + 1) * G][:, :, None]
        m = jnp.maximum(jnp.max(s, axis=-1, keepdims=True), sink)
        p = jnp.exp(s - m)
        p = p / (jnp.sum(p, axis=-1, keepdims=True) + jnp.exp(sink - m))
        outs.append(bdot(p, vv, 'nn'))
    return jnp.concatenate(outs, axis=0)


def _xa(q, kv):
    outs = []
    nk = XA_HEADS * XA_DH
    for h in range(XA_HEADS):
        qh = q[:, h * XA_DH:(h + 1) * XA_DH]
        kh = kv[:, h * XA_DH:(h + 1) * XA_DH]
        vh = kv[:, nk + h * XA_DH:nk + (h + 1) * XA_DH]
        s = bdot(qh, kh, 'nt') * (XA_DH ** -0.5)
        m = jnp.max(s, axis=-1, keepdims=True)
        p = jnp.exp(s - m)
        p = p / jnp.sum(p, axis=-1, keepdims=True)
        outs.append(bdot(p, vh, 'nn'))
    return jnp.concatenate(outs, axis=1)


def _merge(o_dn, o_sw, o_xa, gates, wb):
    acc = None
    for n, o in enumerate((o_dn, o_sw, o_xa)):
        t = jax.nn.sigmoid(gates[:, n * D_MODEL:(n + 1) * D_MODEL]) * bdot(o, wb[n], 'nn')
        acc = t if acc is None else acc + t
    return acc


def _row_tile(T, want):
    return _pick(T, tuple(c for c in (1024, 512, 256, 128, 64) if c <= want))


def _res_ln_bwd(name, x, f, g, b, s, dh):
    T = x.shape[0]
    tm = _row_tile(T, 1024)

    def fn(i, x, f, g, b, dh):
        _, vjp = jax.vjp(lambda x, f, g, b: _res_ln(x, f, g, b, s), x, f, g, b)
        return vjp(dh)

    return _tile_call(name, fn, T // tm, [_rows(x, tm), _rows(f, tm), _full(g), _full(b), _rows(dh, tm)],
                      [((T, D_MODEL), f32, (tm, D_MODEL), lambda i: (i, 0)), ((T, D_MODEL), bf16, (tm, D_MODEL), lambda i: (i, 0))],
                      [((1, D_MODEL), f32), ((1, D_MODEL), f32)])


FF_BLK = D_FF // 2


def _pair_gate_up(w):
    d = w.shape[0]
    return jnp.stack([w[:, :D_FF].reshape(d, -1, FF_BLK), w[:, D_FF:].reshape(d, -1, FF_BLK)], axis=2).reshape(d, 2 * D_FF)


def _swiglu_blk(u):
    return _silu(u[:, :FF_BLK]) * u[:, FF_BLK:]


def _mm_spec():
    return dict(compiler_params=pltpu.CompilerParams(dimension_semantics=("parallel", "arbitrary"), vmem_limit_bytes=VMEM_LIMIT))


def _gu_act(name, xb, w_pair):
    T, D = xb.shape
    tm = _pick(T, (_tiles(name, (512,))[0], 512, 256, 128))

    def body(x_ref, w_ref, u_ref, a_ref):
        r = lax.dot_general(x_ref[...], w_ref[...], (((1,), (0,)), ((), ())), preferred_element_type=f32)
        u_ref[...] = r.astype(bf16)
        a_ref[...] = _swiglu_blk(r).astype(bf16)

    return pl.pallas_call(
        body, name=name, grid=(T // tm, D_FF // FF_BLK),
        in_specs=[pl.BlockSpec((tm, D), lambda i, j: (i, 0)), pl.BlockSpec((D, 2 * FF_BLK), lambda i, j: (0, j))],
        out_specs=[pl.BlockSpec((tm, 2 * FF_BLK), lambda i, j: (i, j)), pl.BlockSpec((tm, FF_BLK), lambda i, j: (i, j))],
        out_shape=[jax.ShapeDtypeStruct((T, 2 * D_FF), bf16), jax.ShapeDtypeStruct((T, D_FF), bf16)],
        **_mm_spec())(xb, w_pair)


def _down_dx_act_bwd(name, df, w_down_t, u):
    T, D = df.shape
    tm = _pick(T, (_tiles(name, (256,))[0], 256, 128))

    def body(df_ref, w_ref, u_ref, du_ref):
        da = lax.dot_general(df_ref[...], w_ref[...], (((1,), (0,)), ((), ())), preferred_element_type=f32)
        _, vjp = jax.vjp(_swiglu_blk, u_ref[...].astype(f32))
        du_ref[...] = vjp(da)[0].astype(bf16)

    return pl.pallas_call(
        body, name=name, grid=(T // tm, D_FF // FF_BLK),
        in_specs=[pl.BlockSpec((tm, D), lambda i, j: (i, 0)), pl.BlockSpec((D, FF_BLK), lambda i, j: (0, j)),
                  pl.BlockSpec((tm, 2 * FF_BLK), lambda i, j: (i, j))],
        out_specs=pl.BlockSpec((tm, 2 * FF_BLK), lambda i, j: (i, j)),
        out_shape=jax.ShapeDtypeStruct((T, 2 * D_FF), bf16),
        **_mm_spec())(df, w_down_t, u)


def _mm_res_ln(name, a, w, x, g, b, s):
    T, K = a.shape
    tm = _pick(T, (_tiles(name, (512,))[0], 512, 256, 128))

    def body(a_ref, w_ref, x_ref, g_ref, b_ref, f_ref, h_ref, hb_ref):
        f = lax.dot_general(a_ref[...], w_ref[...], (((1,), (0,)), ((), ())), preferred_element_type=f32)
        h = _res_ln(x_ref[...], f, g_ref[...], b_ref[...], s)
        f_ref[...] = f.astype(bf16)
        h_ref[...] = h
        hb_ref[...] = h.astype(bf16)

    row = pl.BlockSpec((tm, D_MODEL), lambda i: (i, 0))
    vec = pl.BlockSpec((1, D_MODEL), lambda i: (0, 0))
    return pl.pallas_call(
        body, name=name, grid=(T // tm,),
        in_specs=[pl.BlockSpec((tm, K), lambda i: (i, 0)), pl.BlockSpec((K, D_MODEL), lambda i: (0, 0)), row, vec, vec],
        out_specs=[row, row, row],
        out_shape=[jax.ShapeDtypeStruct((T, D_MODEL), bf16), jax.ShapeDtypeStruct((T, D_MODEL), f32), jax.ShapeDtypeStruct((T, D_MODEL), bf16)],
        compiler_params=pltpu.CompilerParams(dimension_semantics=("parallel",), vmem_limit_bytes=VMEM_LIMIT))(a, w, x, g, b)


def _hm_spec(tm, w=DN_DK):
    return ((DN_HEADS, tm, w), lambda i: (0, i, 0))


def _prev(i):
    return jnp.maximum(i - 1, 0)


def _dn_front(first, xc, xp, ba, cw, hp):
    tm = xc.shape[0]
    nc = tm // DN_CHUNK
    outs = _wy(*[t.reshape(DN_HEADS * nc, DN_CHUNK, DN_DK) for t in _dn_pre(first, xc, xp, ba, cw, hp)])
    return tuple(o.reshape(DN_HEADS, tm, o.shape[-1]) for o in outs)


def _dn_front_ins(p, pba, cw, hp, tm):
    W3 = 3 * DN_HEADS * DN_DK
    return [(p, (tm, W3), lambda i: (i, 0)), (p, (HALO, W3), lambda i: (jnp.maximum(i * (tm // HALO) - 1, 0), 0)),
            (pba, (tm, LANES), lambda i: (i, 0)), _full(cw), _full(hp)]


def _dn_front_fwd(name, p, pba, cw, hp):
    T = p.shape[0]
    tm = _row_tile(T, DN_TILE)
    blk, im = _hm_spec(tm)
    blk2, im2 = _hm_spec(tm, DN_CHUNK)
    full = (DN_HEADS, T, DN_DK)
    outs = [(full, bf16, blk, im), ((DN_HEADS, T, DN_CHUNK), bf16, blk2, im2), (full, f32, blk, im),
            (full, bf16, blk, im), (full, bf16, blk, im), (full, f32, blk, im)]
    return _tile_call(name, lambda i, *a: _dn_front(i == 0, *a), T // tm, _dn_front_ins(p, pba, cw, hp, tm), outs)


def _dn_front_bwd(name, p, pba, cw, hp, cts):
    T = p.shape[0]
    tm = _row_tile(T, DN_TILE)
    n = T // tm
    W3 = 3 * DN_HEADS * DN_DK
    blk, im = _hm_spec(tm)
    blk2, im2 = _hm_spec(tm, DN_CHUNK)
    ins = _dn_front_ins(p, pba, cw, hp, tm) + [(c, blk2 if c.shape[-1] == DN_CHUNK else blk, im) for c in cts]

    def fn(i, xc, xp, ba, cw, hp, *cts):
        _, vjp = jax.vjp(lambda *a: _dn_front(i == 0, *a), xc, xp, ba, cw, hp)
        return vjp(tuple(cts))

    return _tile_call(name, fn, n, ins,
                      [((T, W3), bf16, (tm, W3), lambda i: (i, 0)), ((n * HALO, W3), f32, (HALO, W3), lambda i: (i, 0)),
                       ((T, LANES), f32, (tm, LANES), lambda i: (i, 0))],
                      [(tuple(cw.shape), f32), (tuple(hp.shape), f32)])


def _scan_fwd(name, qd, qk, u, w, kt, egl):
    T = qd.shape[1]
    C = DN_CHUNK
    U = _pick(T // C, (SCAN_CHUNKS, 1))
    n = T // (C * U)

    def body(qd_ref, qk_ref, u_ref, w_ref, kt_ref, egl_ref, o_ref, sall_ref, s_ref):
        i = pl.program_id(0)

        @pl.when(i == 0)
        def _():
            s_ref[...] = jnp.zeros(s_ref.shape, f32)

        S = s_ref[...]
        for j in range(U):
            sl = pl.ds(j * C, C)
            sall_ref[:, j] = S
            o, S = _scan_step(S, *[r[:, sl, :] for r in (qd_ref, qk_ref, u_ref, w_ref, kt_ref, egl_ref)])
            o_ref[:, sl, :] = o
        s_ref[...] = S

    blk, im = _hm_spec(C * U)
    blk2, im2 = _hm_spec(C * U, C)
    return pl.pallas_call(
        body, name=name, grid=(n,),
        in_specs=[pl.BlockSpec(blk, im), pl.BlockSpec(blk2, im2)] + [pl.BlockSpec(blk, im)] * 4,
        out_specs=[pl.BlockSpec(blk, im), pl.BlockSpec((DN_HEADS, U, DN_DK, DN_DK), lambda i: (0, i, 0, 0))],
        out_shape=[jax.ShapeDtypeStruct((DN_HEADS, T, DN_DK), f32), jax.ShapeDtypeStruct((DN_HEADS, n * U, DN_DK, DN_DK), f32)],
        scratch_shapes=[pltpu.VMEM((DN_HEADS, DN_DK, DN_DK), f32)],
        compiler_params=pltpu.CompilerParams(dimension_semantics=("arbitrary",), vmem_limit_bytes=VMEM_LIMIT),
    )(qd, qk, u, w, kt, egl)


def _scan_bwd(name, qd, qk, u, w, kt, egl, s_all, do):
    T = qd.shape[1]
    C = DN_CHUNK
    U = _pick(T // C, (SCAN_CHUNKS, 1))
    n = T // (C * U)

    def body(qd_ref, qk_ref, u_ref, w_ref, kt_ref, egl_ref, sall_ref, do_ref,
             dqd_ref, dqk_ref, du_ref, dw_ref, dkt_ref, degl_ref, ds_ref):
        i = pl.program_id(0)

        @pl.when(i == 0)
        def _():
            ds_ref[...] = jnp.zeros(ds_ref.shape, f32)

        dS = ds_ref[...]
        for j in reversed(range(U)):
            sl = pl.ds(j * C, C)
            args = [r[:, sl, :].astype(f32) for r in (qd_ref, qk_ref, u_ref, w_ref, kt_ref, egl_ref)]
            _, vjp = jax.vjp(_scan_step, sall_ref[:, j], *args)
            dS, *cts = vjp((do_ref[:, sl, :], dS))
            for r, v in zip((dqd_ref, dqk_ref, du_ref, dw_ref, dkt_ref, degl_ref), cts):
                r[:, sl, :] = v
        ds_ref[...] = dS

    blk = (DN_HEADS, C * U, DN_DK)
    blk2 = (DN_HEADS, C * U, C)
    rim = lambda i: (0, n - 1 - i, 0)
    sp, sp2 = pl.BlockSpec(blk, rim), pl.BlockSpec(blk2, rim)
    full, full2 = jax.ShapeDtypeStruct((DN_HEADS, T, DN_DK), f32), jax.ShapeDtypeStruct((DN_HEADS, T, C), f32)
    return pl.pallas_call(
        body, name=name, grid=(n,),
        in_specs=[sp, sp2, sp, sp, sp, sp, pl.BlockSpec((DN_HEADS, U, DN_DK, DN_DK), lambda i: (0, n - 1 - i, 0, 0)), sp],
        out_specs=[sp, sp2, sp, sp, sp, sp],
        out_shape=[full, full2, full, full, full, full],
        scratch_shapes=[pltpu.VMEM((DN_HEADS, DN_DK, DN_DK), f32)],
        compiler_params=pltpu.CompilerParams(dimension_semantics=("arbitrary",), vmem_limit_bytes=VMEM_LIMIT),
    )(qd, qk, u, w, kt, egl, s_all, do)


def _dn_post_fwd(name, o, p, nw):
    T = p.shape[0]
    tm = _row_tile(T, 1024)
    blk, im = _hm_spec(tm)
    W = DN_HEADS * DN_DK
    return _tile_call(name, lambda i, o, z, nw: _dn_post(o, z, nw), T // tm,
                      [(o, blk, im), (p, (tm, W), lambda i: (i, 1536 // W)), _full(nw)],
                      [((T, W), f32, (tm, W), lambda i: (i, 0))])[0]


def _dn_post_bwd(name, o, p, nw, dout):
    T = p.shape[0]
    tm = _row_tile(T, 1024)
    blk, im = _hm_spec(tm)
    W = DN_HEADS * DN_DK

    def fn(i, o, z, nw, dout):
        _, vjp = jax.vjp(_dn_post, o, z, nw)
        return vjp(dout)

    return _tile_call(name, fn, T // tm,
                      [(o, blk, im), (p, (tm, W), lambda i: (i, 1536 // W)), _full(nw), _rows(dout, tm)],
                      [((DN_HEADS, T, DN_DK), f32, blk, im), ((T, W), f32, (tm, W), lambda i: (i, 0))],
                      [((1, DN_DK), f32)])


def _swa_tile(T, big):
    return _pick(T, (2 * WINDOW, WINDOW)) if big else WINDOW


@functools.partial(jax.custom_vjp, nondiff_argnums=(1,))
def _heads_split(x, nh):
    return jnp.stack([x[:, SWA_DH * h:SWA_DH * (h + 1)] for h in range(nh)], axis=0)


@jax.custom_vjp
def _heads_merge(y):
    return jnp.concatenate([y[h] for h in range(y.shape[0])], axis=1)


_heads_split.defvjp(lambda x, nh: (_heads_split(x, nh), None), lambda nh, _, g: (_heads_merge(g),))
_heads_merge.defvjp(lambda y: (_heads_merge(y), None), lambda _, g: (_heads_split(g, g.shape[1] // SWA_DH),))


def _swa_rows(first, qp, kc, kp, vc, vp, snk):
    kv = [_heads_split(t, SWA_KV_HEADS) for t in (kc, kp, vc, vp)]
    return _heads_merge(_swa(first, _heads_split(qp, SWA_HEADS), *kv, snk))


def _swa_ins(p, snk, W):
    qw, kw = SWA_HEADS * SWA_DH, SWA_KV_HEADS * SWA_DH
    return [(p, (W, qw), lambda i: (i, 2048 // qw)),
            (p, (W, kw), lambda i: (i, 6144 // kw)), (p, (W, kw), lambda i: (_prev(i), 6144 // kw)),
            (p, (W, kw), lambda i: (i, 6272 // kw)), (p, (W, kw), lambda i: (_prev(i), 6272 // kw)), _full(snk)]


def _swa_fwd(name, p, snk):
    T = p.shape[0]
    W = _swa_tile(T, True)
    qw = SWA_HEADS * SWA_DH
    return _tile_call(name, lambda i, *a: _swa_rows(i == 0, *a), T // W, _swa_ins(p, snk, W),
                      [((T, qw), f32, (W, qw), lambda i: (i, 0))])[0]


def _swa_bwd(name, p, snk, do):
    T = p.shape[0]
    W = _swa_tile(T, False)
    qw, kw = SWA_HEADS * SWA_DH, SWA_KV_HEADS * SWA_DH

    def fn(i, qp, k_c, k_p, v_c, v_p, snk, do):
        _, vjp = jax.vjp(lambda *a: _swa_rows(i == 0, *a), qp, k_c, k_p, v_c, v_p, snk)
        return vjp(do)

    kv_out = ((T, kw), f32, (W, kw), lambda i: (i, 0))
    return _tile_call(name, fn, T // W, _swa_ins(p, snk, W) + [(do, (W, qw), lambda i: (i, 0))],
                      [((T, qw), bf16, (W, qw), lambda i: (i, 0)), kv_out, kv_out, kv_out, kv_out], [(tuple(snk.shape), f32)])


def _xa_fwd(name, p, kv):
    T = p.shape[0]
    tm = _row_tile(T, 512)
    W = XA_HEADS * XA_DH
    return _tile_call(name, lambda i, q, kv: _xa(q, kv), T // tm, [(p, (tm, W), lambda i: (i, 2560 // W)), _full(kv)],
                      [((T, W), f32, (tm, W), lambda i: (i, 0))])[0]


def _xa_bwd(name, p, kv, do):
    T = p.shape[0]
    tm = _row_tile(T, 512)
    W = XA_HEADS * XA_DH

    def fn(i, q, kv, do):
        _, vjp = jax.vjp(_xa, q, kv)
        return vjp(do)

    return _tile_call(name, fn, T // tm, [(p, (tm, W), lambda i: (i, 2560 // W)), _full(kv), _rows(do, tm)],
                      [((T, W), bf16, (tm, W), lambda i: (i, 0))], [(tuple(kv.shape), f32)])


def _merge_fwd(name, o_dn, o_sw, o_xa, p, wb):
    T = p.shape[0]
    tm = _row_tile(T, 512)
    GW = N_BRANCH * D_MODEL
    return _tile_call(name, lambda i, a, b, c, g, w: _merge(a, b, c, g, w.astype(f32)), T // tm,
                      [_rows(o_dn, tm), _rows(o_sw, tm), _rows(o_xa, tm), (p, (tm, GW), lambda i: (i, 1)), _full(wb)],
                      [((T, D_MODEL), bf16, (tm, D_MODEL), lambda i: (i, 0))])[0]


def _merge_bwd(name, o_dn, o_sw, o_xa, p, wb, dm):
    T = p.shape[0]
    tm = _row_tile(T, 512)
    GW = N_BRANCH * D_MODEL

    def fn(i, a, b, c, g, w, dm):
        _, vjp = jax.vjp(_merge, a, b, c, g, w.astype(f32))
        return vjp(dm)

    bo = ((T, BRANCH_W), bf16, (tm, BRANCH_W), lambda i: (i, 0))
    return _tile_call(name, fn, T // tm,
                      [_rows(o_dn, tm), _rows(o_sw, tm), _rows(o_xa, tm), (p, (tm, GW), lambda i: (i, 1)), _full(wb), _rows(dm, tm)],
                      [bo, bo, bo, ((T, GW), bf16, (tm, GW), lambda i: (i, 0))], [(tuple(wb.shape), f32)])


def _assemble_dp(name, dxc, dxp, dz, dswq, dxaq, dgates, dkc, dkp, dvc, dvp, dba):
    T = dz.shape[0]
    tm = _row_tile(T, DN_TILE)
    n = T // tm
    r = tm // WINDOW
    nb = T // WINDOW

    def fn(i, dxc, dxp, dz, dswq, dxaq, dgates, dkc, dvc, dba, *nxt):
        halo = jnp.where(i == n - 1, 0.0, dxp)
        dqkv = dxc + jnp.concatenate([jnp.zeros((tm - HALO, dxc.shape[1]), f32), halo], axis=0)
        shifted = [jnp.concatenate([jnp.where(i * r + 1 + b <= nb - 1, blk, 0.0) for b, blk in enumerate(half)], axis=0)
                   for half in (nxt[:r], nxt[r:])]
        parts = [dqkv, dz, dswq, dxaq, dgates, dkc + shifted[0], dvc + shifted[1], dba, jnp.zeros((tm, LANES), f32)]
        return jnp.concatenate(parts, axis=1)

    ins = [_rows(dxc, tm), (dxp, (HALO, dxp.shape[1]), lambda i: (jnp.minimum(i + 1, n - 1), 0))]
    ins += [_rows(t, tm) for t in (dz, dswq, dxaq, dgates, dkc, dvc, dba)]
    for t in (dkp, dvp):
        ins += [(t, (WINDOW, t.shape[1]), lambda i, b=b: (jnp.minimum(i * r + 1 + b, nb - 1), 0)) for b in range(r)]
    return _tile_call(name, fn, n, ins, [((T, D_INP), bf16, (tm, D_INP), lambda i: (i, 0))])[0]


def _loss_grad(name, y, tgt):
    T = y.shape[0]
    tm = _row_tile(T, 1024)

    def fn(i, y, t):
        e = y - t
        part = 0.5 * jnp.sum(jnp.mean(e * e, axis=-1, keepdims=True), axis=0, keepdims=True)
        return e * (1.0 / D_MODEL), jnp.broadcast_to(part, (1, LANES))

    return _tile_call(name, fn, T // tm, [_rows(y, tm), _rows(tgt, tm)],
                      [((T, D_MODEL), f32, (tm, D_MODEL), lambda i: (i, 0))], [((1, LANES), f32)])


def _mem_ln_fwd(name, mem, g, b):
    M = mem.shape[0]
    return _tile_call(name, lambda i, m, g, b: (_ln(m, g, b),), 1, [_full(mem), _full(g), _full(b)],
                      [((M, D_MODEL), bf16, (M, D_MODEL), lambda i: (0, 0))])[0]


def _mem_ln_bwd(name, mem, g, b, dmn):
    def fn(i, m, g, b, d):
        _, vjp = jax.vjp(lambda g, b: _ln(m, g, b), g, b)
        return vjp(d)

    return _tile_call(name, fn, 1, [_full(mem), _full(g), _full(b), _full(dmn)], [], [((1, D_MODEL), f32), ((1, D_MODEL), f32)])


W_IN_RANGES = ((0, 1536), (1544, 2056), (2056, 2568), (2824, 3336), (3336, 6408), (2568, 2696), (2696, 2824), (1536, 1544))
W_IN_OFFS = tuple(sum(b - a for a, b in W_IN_RANGES[:k]) for k in range(len(W_IN_RANGES)))


def _pad_w_in(cols):
    parts = [cols(a, b) for a, b in W_IN_RANGES]
    return jnp.concatenate(parts + [jnp.zeros((parts[0].shape[0], D_INP - D_IN), parts[0].dtype)], axis=1)


def _unpad_dw_in(d, lo=0, hi=D_IN):
    parts = []
    for (a, b), off in sorted(zip(W_IN_RANGES, W_IN_OFFS)):
        a2, b2 = max(a, lo), min(b, hi)
        if a2 < b2:
            parts.append(d[:, off + a2 - a:off + b2 - a])
    return jnp.concatenate(parts, axis=1)


def _shard_cols(shards, l):
    n = shards[0].shape[-1]

    def cols(a, b):
        parts = [shards[s][l][:, max(a, s * n) - s * n:min(b, (s + 1) * n) - s * n] for s in range(N_SHARD)
                 if max(a, s * n) < min(b, (s + 1) * n)]
        return parts[0] if len(parts) == 1 else jnp.concatenate(parts, axis=1)

    return cols


def _layer_fwd(l, x, xb, mem_nb, W):
    n = lambda s: f"l{l}_{s}"
    sv = {}
    u1, a1 = _gu_act(n("ffn1_gu"), xb, W['ffn1_w_gu_p'])
    f1, h1, h1b = _mm_res_ln(n("ffn1_down"), a1, W['ffn1_w_down'], x, W['ln_g'][0:1], W['ln_b'][0:1], 0.5)
    p = _mm(n("w_in"), h1b, W['w_inp'], 'nn', out_dtype=bf16)
    pba = _mm(n("w_in_ba"), h1b, W['w_ba'], 'nn')
    qd, qk, u, w, kt, egl = _dn_front_fwd(n("dn_front"), p, pba, W['conv_w'], W['hp'])
    o_raw, s_all = _scan_fwd(n("dn_scan"), qd, qk, u, w, kt, egl)
    o_dn = _dn_post_fwd(n("dn_post"), o_raw, p, W['norm_w'])
    o_sw = _swa_fwd(n("swa"), p, W['snk'])
    kv = _mm(n("mem_kv"), mem_nb, W['w_mem_kv'], 'nn')
    o_xa = _xa_fwd(n("xa"), p, kv)
    merged = _merge_fwd(n("merge"), o_dn, o_sw, o_xa, p, W['w_branch'])
    mo, h2, h2b = _mm_res_ln(n("w_out"), merged, W['w_out'], h1, W['ln_g'][1:2], W['ln_b'][1:2], 1.0)
    u2, a2 = _gu_act(n("ffn2_gu"), h2b, W['ffn2_w_gu_p'])
    f2, h3, h3b = _mm_res_ln(n("ffn2_down"), a2, W['ffn2_w_down'], h2, W['ln_g'][2:3], W['ln_b'][2:3], 0.5)
    sv = dict(x=x, xb=xb, u1=u1, a1=a1, f1=f1, h1=h1, h1b=h1b, p=p, pba=pba, wy=(qd, qk, u, w, kt, egl),
              o_raw=o_raw, s_all=s_all, o_dn=o_dn, o_sw=o_sw, kv=kv, o_xa=o_xa, merged=merged,
              mo=mo, h2=h2, h2b=h2b, u2=u2, a2=a2, f2=f2)
    return h3, h3b, sv


def _ffn_bwd(n, tag, x, xb, u, a, f, g, b, w_gu_t, w_down_t, dh):
    dx_a, df, dg, db = _res_ln_bwd(n(f"{tag}_ln_bwd"), x, f, g, b, 0.5, dh)
    d_down = _mm(n(f"{tag}_down_dw"), a, df, 'tn')
    du = _down_dx_act_bwd(n(f"{tag}_down_dx"), df, w_down_t, u)
    d_gu = _mm(n(f"{tag}_gu_dw"), xb, du, 'tn', out_cols=(FF_BLK, lambda j: (j % 2) * (D_FF // FF_BLK) + j // 2))
    dx = _mm(n(f"{tag}_gu_dx"), du, w_gu_t, 'nn', add=dx_a)
    return dx, d_gu, d_down, dg, db


def _layer_bwd(l, sv, mem_nb, W, dh3, dmem_acc):
    n = lambda s: f"l{l}_{s}"
    G = {}
    dh2, G['ffn2_w_gu'], G['ffn2_w_down'], dg2, db2 = _ffn_bwd(
        n, "ffn2", sv['h2'], sv['h2b'], sv['u2'], sv['a2'], sv['f2'], W['ln_g'][2:3], W['ln_b'][2:3], W['ffn2_w_gu_t'], W['ffn2_w_down_t'], dh3)
    dh1_a, dmo, dg1, db1 = _res_ln_bwd(n("ln1_bwd"), sv['h1'], sv['mo'], W['ln_g'][1:2], W['ln_b'][1:2], 1.0, dh2)
    G['w_out'] = _mm(n("w_out_dw"), sv['merged'], dmo, 'tn')
    dmerged = _mm(n("w_out_dx"), dmo, W['w_out_t'], 'nn')
    p = sv['p']
    do_dn, do_sw, do_xa, dgates, G['w_branch'] = _merge_bwd(n("merge_bwd"), sv['o_dn'], sv['o_sw'], sv['o_xa'], p, W['w_branch'], dmerged)
    dxaq, dkv = _xa_bwd(n("xa_bwd"), p, sv['kv'], do_xa)
    dkv = dkv.astype(bf16)
    G['w_mem_kv'] = _mm(n("mem_kv_dw"), mem_nb, dkv, 'tn')
    dmem_n = _mm(n("mem_kv_dx"), dkv, W['w_mem_kv_t'], 'nn', add=dmem_acc)
    dswq, dkc, dkp, dvc, dvp, dsnk = _swa_bwd(n("swa_bwd"), p, W['snk'], do_sw)
    do_raw, dz, dnw = _dn_post_bwd(n("dn_post_bwd"), sv['o_raw'], p, W['norm_w'], do_dn)
    cts = _scan_bwd(n("dn_scan_bwd"), *sv['wy'], sv['s_all'], do_raw)
    dxc, dxp, dba, dcw, dhp = _dn_front_bwd(n("dn_front_bwd"), p, sv['pba'], W['conv_w'], W['hp'], cts)
    dp = _assemble_dp(n("dp"), dxc, dxp, dz, dswq, dxaq, dgates, dkc, dkp, dvc, dvp, dba)
    G['w_in_p'] = _mm(n("w_in_dw"), sv['h1b'], dp, 'tn')
    dh1 = _mm(n("w_in_dx"), dp, W['w_inp_t'], 'nn', add=dh1_a)
    dx, G['ffn1_w_gu'], G['ffn1_w_down'], dg0, db0 = _ffn_bwd(
        n, "ffn1", sv['x'], sv['xb'], sv['u1'], sv['a1'], sv['f1'], W['ln_g'][0:1], W['ln_b'][0:1], W['ffn1_w_gu_t'], W['ffn1_w_down_t'], dh1)
    G['ln_g'] = jnp.concatenate([dg0, dg1, dg2], axis=0)
    G['ln_b'] = jnp.concatenate([db0, db1, db2], axis=0)
    G['dn_conv_w'] = dcw
    G['dn_a_log'] = dhp[0, :DN_HEADS]
    G['dn_dt_bias'] = dhp[1, :DN_HEADS]
    G['dn_norm_w'] = dnw[0]
    G['swa_sinks'] = dsnk[:, 0]
    return dx, dmem_n, G


def _local_step(x, mem, tgt, Wf):
    mem_g, mem_b = Wf['mem_ln_g'][None, :], Wf['mem_ln_b'][None, :]
    mem_nb = _mem_ln_fwd("mem_ln", mem, mem_g, mem_b)

    def paired(k, l):
        return Wf[k + '_p'][l] if k + '_p' in Wf else _pair_gate_up(Wf[k][l])

    def w_in_cols(l):
        return _shard_cols(Wf['w_in_sh'], l) if 'w_in_sh' in Wf else (lambda a, b: Wf['w_in'][l][:, a:b])

    layers = []
    for l in range(DEPTH):
        layers.append(dict(
            ln_g=Wf['ln_g'][l], ln_b=Wf['ln_b'][l], ffn1_w_gu_p=paired('ffn1_w_gu', l), ffn1_w_down=Wf['ffn1_w_down'][l],
            w_inp=_pad_w_in(w_in_cols(l)), conv_w=Wf['dn_conv_w'][l],
            hp=jnp.zeros((8, LANES), f32).at[0, :DN_HEADS].set(Wf['dn_a_log'][l]).at[1, :DN_HEADS].set(Wf['dn_dt_bias'][l]),
            norm_w=Wf['dn_norm_w'][l][None, :], snk=jnp.broadcast_to(Wf['swa_sinks'][l][:, None], (SWA_HEADS, LANES)),
            w_mem_kv=Wf['w_mem_kv'][l], w_branch=Wf['w_branch'][l], w_out=Wf['w_out'][l],
            ffn2_w_gu_p=paired('ffn2_w_gu', l), ffn2_w_down=Wf['ffn2_w_down'][l]))
        layers[l]['w_ba'] = layers[l]['w_inp'][:, 6400:6400 + LANES]
        for k in ('ffn1_w_gu_p', 'ffn1_w_down', 'w_inp', 'w_mem_kv', 'w_out', 'ffn2_w_gu_p', 'ffn2_w_down'):
            layers[l][k.replace('_p', '') + '_t'] = layers[l][k].T
    h, hb = x, x.astype(bf16)
    saved = []
    for l in range(DEPTH):
        h, hb, sv = _layer_fwd(l, h, hb, mem_nb, layers[l])
        saved.append(sv)
    dh, loss_row = _loss_grad("loss", h, tgt)
    grads = [None] * DEPTH
    dmem_n = None
    for l in reversed(range(DEPTH)):
        dh, dmem_n, grads[l] = _layer_bwd(l, saved[l], mem_nb, layers[l], dh, dmem_n)
    dmg, dmb = _mem_ln_bwd("mem_ln_bwd", mem, mem_g, mem_b, dmem_n)
    G = {k: jnp.stack([grads[l][k] for l in range(DEPTH)], axis=0) for k in grads[0]}
    G['mem_ln_g'], G['mem_ln_b'] = dmg[0], dmb[0]
    return loss_row, dh, G


SEG_ALIGN = 2048


def _round_up(n, m):
    return (n + m - 1) // m * m


def _layout(names, sizes, mult):
    off, table = 0, {}
    for nm in names:
        table[nm] = (off, sizes[nm])
        off += _round_up(sizes[nm], SEG_ALIGN)
    return table, _round_up(off // LANES, mult)


def _pack(table, rows, flat):
    names = list(table)
    lead = flat[names[0]].shape[:-1]
    parts, pos = [], 0
    for nm in names:
        off, size = table[nm]
        if off > pos:
            parts.append(jnp.zeros(lead + (off - pos,), flat[nm].dtype))
        parts.append(flat[nm])
        pos = off + size
    total = rows * LANES
    if total > pos:
        parts.append(jnp.zeros(lead + (total - pos,), parts[-1].dtype))
    return jnp.concatenate(parts, axis=-1).reshape(lead + (rows, LANES))


def _unpack(table, packed, nm):
    off, size = table[nm]
    flat = packed.reshape(packed.shape[:-2] + (-1,))
    return flat[..., off:off + size]


def _split_shards(full, ax):
    shp = full.shape
    t = full.reshape(shp[:ax] + (N_SHARD, shp[ax] // N_SHARD) + shp[ax + 1:])
    return jnp.moveaxis(t, ax, 0)


def _join_shards(sh4, ax):
    return jnp.concatenate([sh4[s] for s in range(N_SHARD)], axis=ax)


ANY = pl.BlockSpec(memory_space=pl.ANY)


def _me():
    return lax.axis_index("x"), lax.axis_index("y"), lax.axis_index("c")


def _comm_call(name, body, arrays, out_shapes, n_sem):
    n = len(arrays)
    scratch = [pltpu.SemaphoreType.DMA((n, n_sem)), pltpu.SemaphoreType.DMA((n, n_sem))]
    return pl.pallas_call(
        body, name=name, out_shape=out_shapes, in_specs=[ANY] * n, out_specs=[ANY] * n, scratch_shapes=scratch,
        compiler_params=pltpu.CompilerParams(has_side_effects=True),
    )(*arrays)


def _all_gather(name, xs):
    n = len(xs)

    def split(shape):
        ax = next(i for i, d in enumerate(shape) if d % 2 == 0)
        return ax, shape[ax] // 2

    def body(*refs):
        x_refs, out_refs = refs[:n], refs[n:2 * n]
        send_sems, recv_sems = refs[2 * n:]
        mx, my, mc = _me()
        me_s = 2 * mx + my
        across_x, across_y, diag = (1 - mx, my), (mx, 1 - my), (1 - mx, 1 - my)
        sid = lambda chip: 2 * chip[0] + chip[1]

        def part(ref, a, which):
            if which is None:
                return ref
            ax, size = split(xs[a].shape[1:])
            return ref.at[(slice(None),) * ax + (pl.ds(which * size, size),)]

        def copy(a, k, shard, half, which, to, src=None):
            dst = part(out_refs[a].at[shard, half], a, which)
            return pltpu.make_async_remote_copy(src_ref=dst if src is None else part(src, a, which), dst_ref=dst,
                                                send_sem=send_sems.at[a, k], recv_sem=recv_sems.at[a, k], device_id=to, device_id_type=MESH)

        sibling, here = (mx, my, 1 - mc), (mx, my, mc)
        sent = []
        for a in range(n):
            sent += [copy(a, 0, me_s, mc, None, (*across_x, mc), src=x_refs[a].at[mc]),
                     copy(a, 1, me_s, mc, None, (*across_y, mc), src=x_refs[a].at[mc])]
        for cp in sent:
            cp.start()
        landing = [(0, across_x, None, across_y), (1, across_y, None, across_x), (2, diag, 0, None), (3, diag, 1, None)]
        for k, origin, which, relay_to in landing:
            for a in range(n):
                copy(a, k, sid(origin), mc, which, here).wait_recv()
                if relay_to is not None:
                    sent.append(copy(a, 2 + k, sid(origin), mc, k, (*relay_to, mc)))
                    sent[-1].start()
                sent.append(copy(a, 4 + k, sid(origin), mc, which, sibling))
                sent[-1].start()
        for k, origin, which, _ in landing:
            for a in range(n):
                copy(a, 4 + k, sid(origin), 1 - mc, which, here).wait_recv()
        for cp in sent:
            cp.wait_send()

    return _comm_call(name, body, xs, [jax.ShapeDtypeStruct((N_SHARD,) + x.shape, x.dtype) for x in xs], 8)


def _pair_exchange(name, items):
    n = len(items)

    def body(*refs):
        src_refs, dst_refs = refs[:n], refs[n:2 * n]
        send_sems, recv_sems = refs[2 * n:]
        mx, my, mc = _me()
        cps = [pltpu.make_async_remote_copy(src_ref=src_refs[a].at[s, 1 - mc], dst_ref=dst_refs[a].at[s], send_sem=send_sems.at[a, s],
                                            recv_sem=recv_sems.at[a, s], device_id=(mx, my, 1 - mc), device_id_type=MESH)
               for a in range(n) for s in range(N_SHARD)]
        for cp in cps:
            cp.start()
        for cp in cps:
            cp.wait()

    return _comm_call(name, body, items, [jax.ShapeDtypeStruct((N_SHARD,) + t.shape[2:], t.dtype) for t in items], N_SHARD)


def _chip_exchange(name, parts):
    n = len(parts)

    def body(*refs):
        p_refs, dst_refs = refs[:n], refs[n:2 * n]
        send_sems, recv_sems = refs[2 * n:]
        mx, my, mc = _me()
        chips = [(1 - mx, my), (mx, 1 - my), (1 - mx, 1 - my)]
        cps = [pltpu.make_async_remote_copy(src_ref=p_refs[a].at[2 * cx + cy], dst_ref=dst_refs[a].at[j], send_sem=send_sems.at[a, j],
                                            recv_sem=recv_sems.at[a, j], device_id=(cx, cy, mc), device_id_type=MESH)
               for a in range(n) for j, (cx, cy) in enumerate(chips)]
        for cp in cps:
            cp.start()
        for cp in cps:
            cp.wait()

    return _comm_call(name, body, parts, [jax.ShapeDtypeStruct((3,) + t.shape[1:], t.dtype) for t in parts], 3)


def _pair_swap(name, reds):
    n = len(reds)

    def body(*refs):
        r_refs, out_refs = refs[:n], refs[n:2 * n]
        send_sems, recv_sems = refs[2 * n:]
        mx, my, mc = _me()
        cps = [pltpu.make_async_remote_copy(src_ref=r_refs[a], dst_ref=out_refs[a], send_sem=send_sems.at[a, 0],
                                            recv_sem=recv_sems.at[a, 0], device_id=(mx, my, 1 - mc), device_id_type=MESH)
               for a in range(n)]
        for cp in cps:
            cp.start()
        for cp in cps:
            cp.wait()

    return _comm_call(name, body, reds, [jax.ShapeDtypeStruct(t.shape, t.dtype) for t in reds], 1)


EW_BLOCK_BYTES = 2 << 20


def _ew_call(name, fn, ins, n_out, out_dtype=f32):
    shape = ins[0].shape
    last = shape[-1]
    flat = [a.reshape(-1, last) for a in ins]
    R = flat[0].shape[0]
    cands = tuple(c for c in (4096, 2048, 1024, 512, 256, 128, 64, 32, 16, 8) if c * last * 4 <= EW_BLOCK_BYTES)
    tr = _pick(R, cands)
    res = _tile_call(name, lambda i, *a: fn(*a), R // tr, [(a, (tr, last), lambda i: (i, 0)) for a in flat],
                     [((R, last), out_dtype, (tr, last), lambda i: (i, 0))] * n_out)
    return [r.reshape(shape) for r in res]


def _adamw(g, w, m, v):
    m = B1 * m + (1.0 - B1) * g
    v = B2 * v + (1.0 - B2) * jnp.square(g)
    m_hat = m / (1.0 - B1 ** STEP)
    v_hat = v / (1.0 - B2 ** STEP)
    return -LR * (m_hat / (jnp.sqrt(v_hat) + EPS) + WD * w), m, v


def _adamw_call(name, mine, theirs, w, m, v, mc1):
    shape, last = w.shape, w.shape[-1]
    g2 = [t.reshape(-1, last) for t in (mine, theirs)]
    w3 = [t.reshape(2, -1, last) for t in (w, m, v)]
    R = g2[0].shape[0]
    tr = _pick(R, tuple(c for c in (4096, 2048, 1024, 512, 256, 128, 64, 32, 16, 8) if c * last * 4 <= EW_BLOCK_BYTES))

    def body(mc_ref, mine_ref, theirs_ref, w_ref, m_ref, v_ref, g_out, d_out, m_out, v_out):
        g = jnp.where(pl.program_id(0) == mc_ref[0], mine_ref[...], theirs_ref[...])
        d, nm, nv = _adamw(g, w_ref[0], m_ref[0], v_ref[0])
        g_out[0], d_out[0], m_out[0], v_out[0] = g, d, nm, nv

    half = pl.BlockSpec((tr, last), lambda h, i: (i, 0))
    full = pl.BlockSpec((1, tr, last), lambda h, i: (h, i, 0))
    res = pl.pallas_call(
        body, name=name, grid=(2, R // tr),
        in_specs=[pl.BlockSpec(memory_space=pltpu.SMEM), half, half, full, full, full], out_specs=[full] * 4,
        out_shape=[jax.ShapeDtypeStruct((2, R, last), f32)] * 4,
        compiler_params=pltpu.CompilerParams(dimension_semantics=("arbitrary", "arbitrary"), vmem_limit_bytes=VMEM_LIMIT),
    )(mc1, *g2, *w3)
    return tuple(r.reshape(shape) for r in res)


def kernel(x, mem, mem_ln_g, mem_ln_b, ln_g, ln_b, ffn1_w_gu, ffn1_w_down, w_in, dn_conv_w, dn_a_log, dn_dt_bias, dn_norm_w, swa_sinks, w_mem_kv, w_branch, w_out, ffn2_w_gu, ffn2_w_down, loss_target, m_mem_ln_g, m_mem_ln_b, m_ln_g, m_ln_b, m_ffn1_w_gu, m_ffn1_w_down, m_w_in, m_dn_conv_w, m_dn_a_log, m_dn_dt_bias, m_dn_norm_w, m_swa_sinks, m_w_mem_kv, m_w_branch, m_w_out, m_ffn2_w_gu, m_ffn2_w_down, v_mem_ln_g, v_mem_ln_b, v_ln_g, v_ln_b, v_ffn1_w_gu, v_ffn1_w_down, v_w_in, v_dn_conv_w, v_dn_a_log, v_dn_dt_bias, v_dn_norm_w, v_swa_sinks, v_w_mem_kv, v_w_branch, v_w_out, v_ffn2_w_gu, v_ffn2_w_down):
    args = dict(locals())
    Ws = {n: args[n] for n in WEIGHTS}
    Ms = {n: args["m_" + n] for n in WEIGHTS}
    Vs = {n: args["v_" + n] for n in WEIGHTS}
    mc = lax.axis_index("c")
    my_s = 2 * lax.axis_index("x") + lax.axis_index("y")
    small = [n for n in WEIGHTS if n not in MATRICES]

    ag_table, ag_rows = _layout(SMALL_SHARDED, {n: Ws[n].size for n in SMALL_SHARDED}, 16)
    ag_small = _pack(ag_table, ag_rows, {n: Ws[n].reshape(-1) for n in SMALL_SHARDED}).reshape(2, ag_rows // 2, LANES)
    local = [Ws[n].astype(bf16) for n in MATRICES] + [ag_small]
    gathered = _all_gather("all_gather_w", local)
    gathered = [jnp.stack([jnp.where(my_s == s, loc, g[s]) for s in range(N_SHARD)], axis=0) for loc, g in zip(local, gathered)]
    Wf = {n: _join_shards(g, SHARD_AXIS[n]) for n, g in zip(MATRICES, gathered) if not n.endswith('w_gu') and n != 'w_in'}
    for n, g in zip(MATRICES, gathered):
        if n.endswith('w_gu'):
            assert g.shape[-1] == FF_BLK
            Wf[n + '_p'] = jnp.concatenate([g[0], g[2], g[1], g[3]], axis=2)
        if n == 'w_in':
            Wf['w_in_sh'] = g
    g_small = gathered[-1].reshape(N_SHARD, ag_rows, LANES)
    for n in SMALL_SHARDED:
        Wf[n] = _join_shards(_unpack(ag_table, g_small, n).reshape((N_SHARD,) + Ws[n].shape), SHARD_AXIS[n])
    for n in WEIGHTS:
        if SHARD_AXIS[n] is None:
            Wf[n] = Ws[n]

    loss_row, dx, G = _local_step(x[0], mem[0], loss_target[0], Wf)

    table, rows = _layout(small + ['loss'], {**{n: Ws[n].size for n in small}, 'loss': 1}, 16)
    gflat = {n: (jnp.broadcast_to(G[n].reshape(1, -1), (N_SHARD, G[n].size)) if SHARD_AXIS[n] is None
                 else _split_shards(G[n], SHARD_AXIS[n]).reshape(N_SHARD, -1)) for n in small}
    gflat['loss'] = jnp.broadcast_to(loss_row[:, :1], (N_SHARD, 1))
    n_in = Ws['w_in'].shape[-1]
    items =[_split_shards(G[n], SHARD_AXIS[n]) if n != 'w_in' else
             jnp.stack([jnp.stack([_unpad_dw_in(G['w_in_p'][l], s * n_in, (s + 1) * n_in) for l in range(DEPTH)]) for s in range(N_SHARD)])
             for n in MATRICES] + [_pack(table, rows, gflat).reshape(N_SHARD, 2, rows // 2, LANES)]
    tags = MATRICES + ['small']
    got = _pair_exchange("rs_pair", items)
    keep = [lax.dynamic_index_in_dim(a, mc, axis=1, keepdims=False) for a in items]
    wire = [bf16] * len(MATRICES) + [f32]
    part = [_ew_call(f"rs_add_pair_{t}", lambda a, b: a + b, [k, g], 1, out_dtype=dt)[0] for t, k, g, dt in zip(tags, keep, got, wire)]
    others = _chip_exchange("rs_chips", part)
    own = lambda a: lax.dynamic_index_in_dim(a, my_s, axis=0, keepdims=False)
    red = [_ew_call(f"rs_add_chips_{t}", lambda k, g, fx, fy, fxy: ((k + g) + fy) + (fx + fxy), [own(k), own(g), o[0], o[1], o[2]], 1)[0]
           for t, k, g, o in zip(tags, keep, got, others)]
    theirs = _pair_swap("rs_swap", red)

    mc1 = mc.astype(i32).reshape(1)
    outs = {}
    for n, a, b in zip(MATRICES, red, theirs):
        outs[n] = _adamw_call(f"adamw_{n}", a, b, Ws[n], Ms[n], Vs[n], mc1)
    fill = {'loss': jnp.zeros((1,), f32)}
    packs = [_pack(table, rows, {**{n: src[n].reshape(-1) for n in small}, **fill}).reshape(2, rows // 2, LANES) for src in (Ws, Ms, Vs)]
    small_out = [p.reshape(rows, LANES) for p in _adamw_call("adamw_small", red[-1], theirs[-1], *packs, mc1)]
    for n in small:
        outs[n] = tuple(_unpack(table, p, n).reshape(Ws[n].shape) for p in small_out)
    loss = _unpack(table, small_out[0], 'loss').reshape(())
    return (loss, dx[None], *[outs[n][k] for k in range(4) for n in WEIGHTS])
```
